```python
import jax
import jax.numpy as jnp
from jax import lax

D_MODEL = 2048
BATCH = 2
SEQ = 4096
DEPTH = 2
DEC_BATCH = 8
DEC_SEQ = 1
PAST_LEN = 16384
PAGE_SIZE = 128

RG_WIDTH = D_MODEL // 2
RG_BLOCKS = 8
RG_BLOCK = RG_WIDTH // RG_BLOCKS
CONV_W = 4
RG_C = 8.0
B_HEADS = 8
B_DH = 128
B_WIDTH = B_HEADS * B_DH
DIL_PATTERNS = ((128, 1), (512, 4), (2048, 16))
W_MAX = 2048
ROT_DIMS = B_DH // 4
ROPE_THETA = 500000.0
Q_BLOCK = 128
C_HEADS = 8
C_DH = 128
C_WIDTH = C_HEADS * C_DH
SB_BIAS_INIT = -6.0
M_HEADS = 4
M_DH = 256
M_WIDTH = M_HEADS * M_DH
M_CHUNK = 128
FORGET_BIAS = 3.0
E_IN = 2 * RG_WIDTH + 3 * B_WIDTH
E_MIX = RG_WIDTH + B_WIDTH
O_IN = 3 * C_WIDTH + 4 * M_WIDTH + 2 * M_HEADS
O_MIX = C_WIDTH + M_WIDTH
PEER_HEADS = 8
N_KEYS = 128
N_EXPERTS = N_KEYS * N_KEYS
PEER_DQ = 256
PEER_TOPK = 16
PEER_BLOCK = 128
N_EVEN = (DEPTH + 1) // 2
N_ODD = DEPTH // 2
EPS = 1e-6

kernel_name = 'hybrid_rglru_dilswa_stickbreak_mlstm_peer_step'


def rmsnorm(x, g):
    xf = x.astype(jnp.float32)
    y = xf * lax.rsqrt(jnp.mean(xf * xf, axis=-1, keepdims=True) + EPS)
    return (y * g.astype(jnp.float32)).astype(x.dtype)


def adaln(c, w, b):
    m = jax.nn.silu(c) @ w + b
    return jnp.split(m[:, None, :], 6, axis=-1)


def modulate(x, g, shift, scale):
    return rmsnorm(x, g) * (1 + scale) + shift


def partial_rope(x, pos):
    half = ROT_DIMS // 2
    inv = ROPE_THETA ** (-jnp.arange(half, dtype=jnp.float32) / half)
    ang = pos.astype(jnp.float32)[:, None] * inv[None, :]
    cos = jnp.cos(ang)[None, :, None, :]
    sin = jnp.sin(ang)[None, :, None, :]
    xr = x[..., :ROT_DIMS].astype(jnp.float32)
    x1, x2 = xr[..., :half], xr[..., half:]
    rot = jnp.concatenate([x1 * cos - x2 * sin, x2 * cos + x1 * sin], axis=-1).astype(x.dtype)
    return jnp.concatenate([rot, x[..., ROT_DIMS:]], axis=-1)


def causal_conv(x, buf, w, b):
    s = x.shape[1]
    xp = jnp.concatenate([buf.astype(x.dtype), x], axis=1)
    y = b + sum(w[j] * xp[:, j:j + s] for j in range(CONV_W))
    return y, xp[:, xp.shape[1] - (CONV_W - 1):]


def rglru(x, h0, w_r, b_r, w_i, b_i, lam):
    bsz, s, _ = x.shape
    xb = x.reshape(bsz, s, RG_BLOCKS, RG_BLOCK)
    r = jax.nn.sigmoid((jnp.einsum('bshi,hij->bshj', xb, w_r).reshape(bsz, s, RG_WIDTH) + b_r).astype(jnp.float32))
    ig = jax.nn.sigmoid((jnp.einsum('bshi,hij->bshj', xb, w_i).reshape(bsz, s, RG_WIDTH) + b_i).astype(jnp.float32))
    log_a = -RG_C * r * jax.nn.softplus(-lam.astype(jnp.float32))
    a = jnp.exp(log_a)
    u = jnp.sqrt(-jnp.expm1(2.0 * log_a)) * ig * x.astype(jnp.float32)

    def comb(p, q):
        return p[0] * q[0], q[0] * p[1] + q[1]

    a_cum, b_cum = lax.associative_scan(comb, (a, u), axis=1)
    h = a_cum * h0.astype(jnp.float32)[:, None, :] + b_cum
    return h.astype(x.dtype), h[:, -1].astype(h0.dtype)


def dilated_attn(q, k_span, v_span, base, pos0):
    nq = q.shape[1]
    qf = q.astype(jnp.float32) * (B_DH ** -0.5)
    outs, lses = [], []
    for (w, d) in DIL_PATTERNS:
        n = w // d + 1
        idx = base + jnp.arange(nq)[:, None] - d * jnp.arange(n)[None, :]
        valid = (idx >= 0) & (idx + pos0 >= 0)
        idx = jnp.maximum(idx, 0)
        kg = k_span[:, idx].astype(jnp.float32)
        vg = v_span[:, idx].astype(jnp.float32)
        s = jnp.einsum('bqhd,bqnhd->bhqn', qf, kg)
        s = jnp.where(valid[None, None], s, -jnp.inf)
        lse = jax.nn.logsumexp(s, axis=-1)
        p = jnp.exp(s - lse[..., None])
        outs.append(jnp.einsum('bhqn,bqnhd->bqhd', p, vg))
        lses.append(lse)
    wgt = jax.nn.softmax(jnp.stack(lses, axis=0), axis=0)
    wgt = jnp.transpose(wgt, (0, 1, 3, 2))[..., None]
    return jnp.sum(wgt * jnp.stack(outs, axis=0), axis=0).astype(q.dtype)


def dilated_prompt(q, k, v):
    bsz, s, h, dh = q.shape
    pad = ((0, 0), (W_MAX, 0), (0, 0), (0, 0))
    kp = jnp.pad(k, pad)
    vp = jnp.pad(v, pad)

    def blk(b):
        t0 = b * Q_BLOCK
        qb = lax.dynamic_slice_in_dim(q, t0, Q_BLOCK, axis=1)
        ks = lax.dynamic_slice_in_dim(kp, t0, W_MAX + Q_BLOCK, axis=1)
        vs = lax.dynamic_slice_in_dim(vp, t0, W_MAX + Q_BLOCK, axis=1)
        return dilated_attn(qb, ks, vs, W_MAX, t0 - W_MAX)

    o = lax.map(blk, jnp.arange(s // Q_BLOCK))
    return jnp.moveaxis(o, 0, 1).reshape(bsz, s, h, dh)


def stick_breaking(q, k, v, q_idx, bias):
    L = k.shape[1]
    z = jnp.einsum('bqhd,bkhd->bhqk', q.astype(jnp.float32), k.astype(jnp.float32)) * (C_DH ** -0.5)
    z = z + bias.astype(jnp.float32)[None, :, None, None]
    mask = (jnp.arange(L)[None, :] < q_idx[:, None])[None, None]
    log_1mb = jnp.where(mask, jax.nn.log_sigmoid(-z), 0.0)
    suffix = lax.cumsum(log_1mb, axis=3, reverse=True) - log_1mb
    log_w = jnp.where(mask, jax.nn.log_sigmoid(z) + suffix, -jnp.inf)
    return jnp.einsum('bhqk,bkhd->bqhd', jnp.exp(log_w), v.astype(jnp.float32)).astype(q.dtype)


def mlstm(q, k, v, ig, fg, C0, n0, m0, chunk):
    bsz, s, h, _ = q.shape
    nc = s // chunk

    def to_chunks(t):
        return jnp.moveaxis(t.astype(jnp.float32).reshape((bsz, nc, chunk) + t.shape[2:]), 1, 0)

    tri = jnp.tril(jnp.ones((chunk, chunk), dtype=bool))[None, :, :, None]

    def step(carry, inp):
        C, n, m = carry
        qc, kc, vc, ic, fc = inp
        bcum = jnp.cumsum(jax.nn.log_sigmoid(fc), axis=1)
        dlog = bcum[:, :, None, :] - bcum[:, None, :, :] + ic[:, None, :, :]
        dlog = jnp.where(tri, dlog, -jnp.inf)
        inter = bcum + m[:, None, :]
        m_t = jnp.maximum(inter, jnp.max(dlog, axis=2))
        dw = jnp.exp(dlog - m_t[:, :, None, :])
        iw = jnp.exp(inter - m_t)
        sw = dw * jnp.einsum('bthd,bshd->btsh', qc, kc)
        num = jnp.einsum('btsh,bshv->bthv', sw, vc) + iw[..., None] * jnp.einsum('bhvd,bthd->bthv', C, qc)
        den = jnp.sum(sw, axis=2) + iw * jnp.einsum('bhd,bthd->bth', n, qc)
        hout = num / jnp.maximum(jnp.abs(den), jnp.exp(-m_t))[..., None]
        m_l = m_t[:, -1]
        w_l = jnp.exp(bcum[:, -1:, :] - bcum + ic - m_l[:, None, :])
        decay = jnp.exp(bcum[:, -1] + m - m_l)
        C_new = decay[..., None, None] * C + jnp.einsum('bsh,bshv,bshd->bhvd', w_l, vc, kc)
        n_new = decay[..., None] * n + jnp.einsum('bsh,bshd->bhd', w_l, kc)
        return (C_new, n_new, m_l), hout

    carry0 = (C0.astype(jnp.float32), n0.astype(jnp.float32), m0.astype(jnp.float32))
    (C1, n1, m1), hs = lax.scan(step, carry0, (to_chunks(q), to_chunks(k), to_chunks(v), to_chunks(ig), to_chunks(fg)))
    hs = jnp.moveaxis(hs, 0, 1).reshape(bsz, s, h, v.shape[-1])
    return hs, C1.astype(C0.dtype), n1.astype(n0.dtype), m1.astype(m0.dtype)


def peer(x, w_q, keys, u, v):
    bsz, s, d = x.shape
    t = bsz * s
    nblk = -(-t // PEER_BLOCK)
    xt = jnp.pad(x.reshape(t, d), ((0, nblk * PEER_BLOCK - t), (0, 0))).reshape(nblk, PEER_BLOCK, d)
    kf = keys.astype(jnp.float32)

    def blk(xb):
        p = xb.shape[0]
        q = (xb @ w_q).reshape(p, PEER_HEADS, 2, PEER_DQ // 2).astype(jnp.float32)
        sc = jnp.einsum('phcd,ckd->phck', q, kf)
        sv, si = lax.top_k(sc, PEER_TOPK)
        cand = (sv[:, :, 0, :, None] + sv[:, :, 1, None, :]).reshape(p, PEER_HEADS, PEER_TOPK * PEER_TOPK)
        cidx = (si[:, :, 0, :, None] * N_KEYS + si[:, :, 1, None, :]).reshape(p, PEER_HEADS, PEER_TOPK * PEER_TOPK)
        top_s, pos = lax.top_k(cand, PEER_TOPK)
        eidx = jnp.take_along_axis(cidx, pos, axis=-1)
        g = jax.nn.softmax(top_s, axis=-1)
        act = jax.nn.gelu(jnp.einsum('phkd,pd->phk', u[eidx], xb).astype(jnp.float32))
        return jnp.einsum('phk,phkd->pd', (g * act).astype(xb.dtype), v[eidx])

    y = lax.map(blk, xt).reshape(nblk * PEER_BLOCK, d)[:t]
    return y.reshape(bsz, s, d)


def even_mixer(xn, pos, conv_buf, h0, win_k, win_v, win_pos0, w_in, conv_w, conv_b, w_r, b_r, w_i, b_i, lam, w_out):
    bsz, s = xn.shape[:2]
    proj = xn @ w_in
    xa, ga, q, k, v = jnp.split(proj, [RG_WIDTH, 2 * RG_WIDTH, 2 * RG_WIDTH + B_WIDTH, 2 * RG_WIDTH + 2 * B_WIDTH], axis=-1)
    xc, conv_new = causal_conv(xa, conv_buf, conv_w, conv_b)
    ya, h_last = rglru(xc, h0, w_r, b_r, w_i, b_i, lam)
    ya = ya * jax.nn.gelu(ga)
    q = partial_rope(q.reshape(bsz, s, B_HEADS, B_DH), pos)
    k = partial_rope(k.reshape(bsz, s, B_HEADS, B_DH), pos)
    v = v.reshape(bsz, s, B_HEADS, B_DH)
    if win_k is None:
        o = dilated_prompt(q, k, v)
        wl = min(W_MAX, s)
        new_k, new_v = k[:, s - wl:], v[:, s - wl:]
    else:
        ks = jnp.concatenate([win_k.astype(k.dtype), k], axis=1)
        vs = jnp.concatenate([win_v.astype(v.dtype), v], axis=1)
        o = dilated_attn(q, ks, vs, win_k.shape[1], win_pos0)
        new_k, new_v = k, v
    y = jnp.concatenate([ya, o.reshape(bsz, s, B_WIDTH)], axis=-1) @ w_out
    return y, conv_new, h_last, new_k, new_v


def odd_mixer(xn, past_k, past_v, C0, n0, m0, chunk, w_in, b_if, sb_bias, g_mnorm, w_out):
    bsz, s = xn.shape[:2]
    proj = xn @ w_in
    cuts = [C_WIDTH, 2 * C_WIDTH, 3 * C_WIDTH, 3 * C_WIDTH + M_WIDTH, 3 * C_WIDTH + 2 * M_WIDTH, 3 * C_WIDTH + 3 * M_WIDTH, 3 * C_WIDTH + 4 * M_WIDTH]
    qc, kc, vc, qm, km, vm, om, gif = jnp.split(proj, cuts, axis=-1)
    qc = qc.reshape(bsz, s, C_HEADS, C_DH)
    kc = kc.reshape(bsz, s, C_HEADS, C_DH)
    vc = vc.reshape(bsz, s, C_HEADS, C_DH)
    if past_k is None:
        def blk(b):
            t0 = b * Q_BLOCK
            qb = lax.dynamic_slice_in_dim(qc, t0, Q_BLOCK, axis=1)
            return stick_breaking(qb, kc, vc, t0 + jnp.arange(Q_BLOCK), sb_bias)
        oc = jnp.moveaxis(lax.map(blk, jnp.arange(s // Q_BLOCK)), 0, 1).reshape(bsz, s, C_WIDTH)
        new_k = kc.reshape(bsz, s // PAGE_SIZE, PAGE_SIZE, C_HEADS, C_DH)
        new_v = vc.reshape(bsz, s // PAGE_SIZE, PAGE_SIZE, C_HEADS, C_DH)
    else:
        k_all = jnp.concatenate([past_k.astype(kc.dtype), kc], axis=1)
        v_all = jnp.concatenate([past_v.astype(vc.dtype), vc], axis=1)
        oc = stick_breaking(qc, k_all, v_all, past_k.shape[1] + jnp.arange(s), sb_bias).reshape(bsz, s, C_WIDTH)
        new_k, new_v = kc, vc
    gif = gif.reshape(bsz, s, 2, M_HEADS).astype(jnp.float32) + b_if.astype(jnp.float32)
    hm, C1, n1, m1 = mlstm(qm.reshape(bsz, s, M_HEADS, M_DH), km.reshape(bsz, s, M_HEADS, M_DH) * (M_DH ** -0.5), vm.reshape(bsz, s, M_HEADS, M_DH), gif[:, :, 0], gif[:, :, 1], C0, n0, m0, chunk)
    hm = hm * lax.rsqrt(jnp.mean(hm * hm, axis=-1, keepdims=True) + EPS)
    hm = hm.reshape(bsz, s, M_WIDTH) * g_mnorm.astype(jnp.float32) * jax.nn.sigmoid(om.astype(jnp.float32))
    y = jnp.concatenate([oc, hm.astype(xn.dtype)], axis=-1) @ w_out
    return y, new_k, new_v, C1, n1, m1


def setup_inputs(seed: int = 0) -> dict:
    key = jax.random.key(seed)
    keys = jax.random.split(key, 40)

    def nrm(i, shape, scale):
        return jax.random.normal(keys[i], shape, jnp.float32) * scale

    n_pages = PAST_LEN // PAGE_SIZE
    n_used = DEC_BATCH * n_pages
    n_pool = n_used + max(1, n_used // 4)
    win = min(W_MAX, PAST_LEN)
    lam_s = jax.random.uniform(keys[0], (N_EVEN, RG_WIDTH), jnp.float32, 0.9, 0.999) ** (1.0 / RG_C)
    page_table = jax.random.permutation(keys[1], n_pool)[:n_used].reshape(DEC_BATCH, n_pages).astype(jnp.int32)
    return {
        'x_prompt': nrm(2, (BATCH, SEQ, D_MODEL), 1.0),
        'x_sample': nrm(3, (DEC_BATCH, DEC_SEQ, D_MODEL), 1.0),
        'c_prompt': nrm(4, (BATCH, D_MODEL), 1.0),
        'c_sample': nrm(5, (DEC_BATCH, D_MODEL), 1.0),
        'state_rglru_conv': nrm(6, (N_EVEN, DEC_BATCH, CONV_W - 1, RG_WIDTH), 1.0),
        'state_rglru_h': nrm(7, (N_EVEN, DEC_BATCH, RG_WIDTH), 0.5),
        'cache_swa_k': nrm(8, (N_EVEN, DEC_BATCH, win, B_HEADS, B_DH), 1.0),
        'cache_swa_v': nrm(9, (N_EVEN, DEC_BATCH, win, B_HEADS, B_DH), 1.0),
        'cache_sb_k': nrm(10, (N_ODD, n_pool, PAGE_SIZE, C_HEADS, C_DH), 1.0),
        'cache_sb_v': nrm(11, (N_ODD, n_pool, PAGE_SIZE, C_HEADS, C_DH), 1.0),
        'state_mlstm_C': nrm(12, (N_ODD, DEC_BATCH, M_HEADS, M_DH, M_DH), 0.1),
        'state_mlstm_n': nrm(13, (N_ODD, DEC_BATCH, M_HEADS, M_DH), 0.1),
        'state_mlstm_m': nrm(14, (N_ODD, DEC_BATCH, M_HEADS), 1.0),
        'page_table': page_table,
        'w_ada': nrm(15, (DEPTH, D_MODEL, 6 * D_MODEL), 0.5 * D_MODEL ** -0.5),
        'b_ada': nrm(16, (DEPTH, 6 * D_MODEL), 0.02),
        'g_norm_mix': 1.0 + nrm(17, (DEPTH, D_MODEL), 0.02),
        'g_norm_ffn': 1.0 + nrm(18, (DEPTH, D_MODEL), 0.02),
        'e_w_in': nrm(19, (N_EVEN, D_MODEL, E_IN), D_MODEL ** -0.5),
        'e_conv_w': nrm(20, (N_EVEN, CONV_W, RG_WIDTH), CONV_W ** -0.5),
        'e_conv_b': nrm(21, (N_EVEN, RG_WIDTH), 0.02),
        'e_w_r': nrm(22, (N_EVEN, RG_BLOCKS, RG_BLOCK, RG_BLOCK), RG_BLOCK ** -0.5),
        'e_b_r': nrm(23, (N_EVEN, RG_WIDTH), 0.02),
        'e_w_i': nrm(24, (N_EVEN, RG_BLOCKS, RG_BLOCK, RG_BLOCK), RG_BLOCK ** -0.5),
        'e_b_i': nrm(25, (N_EVEN, RG_WIDTH), 0.02),
        'e_lambda': jnp.log(lam_s) - jnp.log1p(-lam_s),
        'e_w_out': nrm(26, (N_EVEN, E_MIX, D_MODEL), E_MIX ** -0.5),
        'o_w_in': nrm(27, (N_ODD, D_MODEL, O_IN), D_MODEL ** -0.5),
        'o_b_if': jnp.stack([nrm(28, (N_ODD, M_HEADS), 0.1), FORGET_BIAS + nrm(29, (N_ODD, M_HEADS), 0.5)], axis=1),
        'o_sb_bias': SB_BIAS_INIT + nrm(37, (N_ODD, C_HEADS), 0.5),
        'o_g_mnorm': 1.0 + nrm(30, (N_ODD, M_WIDTH), 0.02),
        'o_w_out': nrm(31, (N_ODD, O_MIX, D_MODEL), O_MIX ** -0.5),
        'peer_w_q': nrm(32, (DEPTH, D_MODEL, PEER_HEADS * PEER_DQ), D_MODEL ** -0.5),
        'peer_keys': nrm(33, (DEPTH, 2, N_KEYS, PEER_DQ // 2), (PEER_DQ // 2) ** -0.5),
        'peer_u': nrm(34, (DEPTH, N_EXPERTS, D_MODEL), D_MODEL ** -0.5),
        'peer_v': nrm(35, (DEPTH, N_EXPERTS, D_MODEL), 0.5),
        'g_final': 1.0 + nrm(36, (D_MODEL,), 0.02),
    }


def reference(x_prompt, x_sample, c_prompt, c_sample, state_rglru_conv, state_rglru_h, cache_swa_k, cache_swa_v, cache_sb_k, cache_sb_v, state_mlstm_C, state_mlstm_n, state_mlstm_m, page_table, w_ada, b_ada, g_norm_mix, g_norm_ffn, e_w_in, e_conv_w, e_conv_b, e_w_r, e_b_r, e_w_i, e_b_i, e_lambda, e_w_out, o_w_in, o_b_if, o_sb_bias, o_g_mnorm, o_w_out, peer_w_q, peer_keys, peer_u, peer_v, g_final):
    bp, s = x_prompt.shape[:2]
    bs, sd = x_sample.shape[:2]
    n_pages = page_table.shape[1]
    pos_p = jnp.arange(s, dtype=jnp.int32)
    pos_s = PAST_LEN + jnp.arange(sd, dtype=jnp.int32)
    xp, xs = x_prompt, x_sample
    conv_p, conv_s, h_p, h_s, swk_p, swk_s, swv_p, swv_s = [], [], [], [], [], [], [], []
    sbk_p, sbk_s, sbv_p, sbv_s, mC_p, mC_s, mn_p, mn_s, mm_p, mm_s = [], [], [], [], [], [], [], [], [], []
    for l in range(DEPTH):
        mp = adaln(c_prompt, w_ada[l], b_ada[l])
        ms = adaln(c_sample, w_ada[l], b_ada[l])
        xnp = modulate(xp, g_norm_mix[l], mp[0], mp[1])
        xns = modulate(xs, g_norm_mix[l], ms[0], ms[1])
        if l % 2 == 0:
            e = l // 2
            ew = (e_w_in[e], e_conv_w[e], e_conv_b[e], e_w_r[e], e_b_r[e], e_w_i[e], e_b_i[e], e_lambda[e], e_w_out[e])
            yp, cp, hp, kp, vp = even_mixer(xnp, pos_p, jnp.zeros((bp, CONV_W - 1, RG_WIDTH), xp.dtype), jnp.zeros((bp, RG_WIDTH), xp.dtype), None, None, 0, *ew)
            win = cache_swa_k.shape[2]
            ys, cs, hs, ks, vs = even_mixer(xns, pos_s, state_rglru_conv[e], state_rglru_h[e], cache_swa_k[e], cache_swa_v[e], PAST_LEN - win, *ew)
            conv_p.append(cp); conv_s.append(cs); h_p.append(hp); h_s.append(hs)
            swk_p.append(kp); swk_s.append(ks); swv_p.append(vp); swv_s.append(vs)
        else:
            o = l // 2
            ow = (o_w_in[o], o_b_if[o], o_sb_bias[o], o_g_mnorm[o], o_w_out[o])
            yp, kp, vp, Cp, np_, mp_ = odd_mixer(xnp, None, None, jnp.zeros((bp, M_HEADS, M_DH, M_DH), xp.dtype), jnp.zeros((bp, M_HEADS, M_DH), xp.dtype), jnp.zeros((bp, M_HEADS), xp.dtype), M_CHUNK, *ow)
            past_k = cache_sb_k[o][page_table].reshape(bs, n_pages * PAGE_SIZE, C_HEADS, C_DH)
            past_v = cache_sb_v[o][page_table].reshape(bs, n_pages * PAGE_SIZE, C_HEADS, C_DH)
            ys, ks, vs, Cs, ns_, ms_ = odd_mixer(xns, past_k, past_v, state_mlstm_C[o], state_mlstm_n[o], state_mlstm_m[o], sd, *ow)
            sbk_p.append(kp); sbk_s.append(ks); sbv_p.append(vp); sbv_s.append(vs)
            mC_p.append(Cp); mC_s.append(Cs); mn_p.append(np_); mn_s.append(ns_); mm_p.append(mp_); mm_s.append(ms_)
        xp = xp + mp[2] * yp
        xs = xs + ms[2] * ys
        xp = xp + mp[5] * peer(modulate(xp, g_norm_ffn[l], mp[3], mp[4]), peer_w_q[l], peer_keys[l], peer_u[l], peer_v[l])
        xs = xs + ms[5] * peer(modulate(xs, g_norm_ffn[l], ms[3], ms[4]), peer_w_q[l], peer_keys[l], peer_u[l], peer_v[l])
    y_prompt = rmsnorm(xp, g_final)
    y_sample = rmsnorm(xs, g_final)
    rglru_conv_p = jnp.stack(conv_p); rglru_conv_s = jnp.stack(conv_s)
    rglru_h_p = jnp.stack(h_p); rglru_h_s = jnp.stack(h_s)
    swa_k_p = jnp.stack(swk_p); swa_k_s = jnp.stack(swk_s)
    swa_v_p = jnp.stack(swv_p); swa_v_s = jnp.stack(swv_s)
    sb_k_p = jnp.stack(sbk_p); sb_k_s = jnp.stack(sbk_s)
    sb_v_p = jnp.stack(sbv_p); sb_v_s = jnp.stack(sbv_s)
    mlstm_C_p = jnp.stack(mC_p); mlstm_C_s = jnp.stack(mC_s)
    mlstm_n_p = jnp.stack(mn_p); mlstm_n_s = jnp.stack(mn_s)
    mlstm_m_p = jnp.stack(mm_p); mlstm_m_s = jnp.stack(mm_s)
    return (y_prompt, y_sample, rglru_conv_p, rglru_conv_s, rglru_h_p, rglru_h_s, swa_k_p, swa_k_s, swa_v_p, swa_v_s, sb_k_p, sb_k_s, sb_v_p, sb_v_s, mlstm_C_p, mlstm_C_s, mlstm_n_p, mlstm_n_s, mlstm_m_p, mlstm_m_s)
```

```python
import functools
import math

import jax
import jax.numpy as jnp
from jax import lax
from jax.experimental import pallas as pl
from jax.experimental.pallas import tpu as pltpu

f32 = jnp.float32
bf16 = jnp.bfloat16

D_MODEL = 2048
PAST_LEN = 16384
PAGE = 128
RG_WIDTH = 1024
RG_BLOCKS = 8
CONV_W = 4
RG_C = 8.0
HEADS = 8
DH = 128
ATT_W = HEADS * DH
DIL_PATTERNS = ((128, 1), (512, 4), (2048, 16))
ROT_DIMS = 32
ROPE_THETA = 500000.0
M_HEADS = 4
M_DH = 256
M_WIDTH = M_HEADS * M_DH
CHUNK = 128
E_IN = 2 * RG_WIDTH + 3 * ATT_W
O_MAIN = 3 * ATT_W + 4 * M_WIDTH
PEER_HEADS = 8
N_KEYS = 128
N_EXPERTS = N_KEYS * N_KEYS
PEER_TOPK = 16
EPS = 1e-6
LANES = 128
VMEM_LIMIT = 56 * 1024 * 1024
NEG_INF = float("-inf")


def _cparams(*sem):
    return pltpu.CompilerParams(dimension_semantics=sem, vmem_limit_bytes=VMEM_LIMIT)


def _dot(a, b):
    return jnp.dot(a, b, preferred_element_type=f32)


def _dot_nt(a, b):
    return lax.dot_general(a, b, (((1,), (1,)), ((), ())), preferred_element_type=f32)


def _split3(x):
    hi = x.astype(bf16)
    r = x - hi.astype(f32)
    mid = r.astype(bf16)
    lo = (r - mid.astype(f32)).astype(bf16)
    return hi, mid, lo


def _sigmoid(x):
    return 1.0 / (1.0 + jnp.exp(-x))


def _log_sigmoid_pair(z):
    l1p = jnp.log1p(jnp.exp(-jnp.abs(z)))
    return jnp.minimum(z, 0.0) - l1p, -jnp.maximum(z, 0.0) - l1p


def _gelu(x):
    c = math.sqrt(2.0 / math.pi)
    return 0.5 * x * (1.0 + jnp.tanh(c * (x + 0.044715 * (x * x * x))))


def _iota(shape, dim):
    return lax.broadcasted_iota(jnp.int32, shape, dim)


def _adaln_kernel(c_ref, w_ref, b_ref, o_ref):
    c = c_ref[...]
    s = c * _sigmoid(c)
    w = w_ref[0]
    s_hi = s.astype(bf16)
    s_lo = (s - s_hi.astype(f32)).astype(bf16)
    w_hi = w.astype(bf16)
    w_lo = (w - w_hi.astype(f32)).astype(bf16)
    o_ref[0] = _dot(s_hi, w_hi) + _dot(s_hi, w_lo) + _dot(s_lo, w_hi) + b_ref[0]


def adaln_all(c_all, w_ada, b_ada):
    depth, d, n = w_ada.shape
    rows = c_all.shape[0]
    tn = 1024
    return pl.pallas_call(
        _adaln_kernel,
        grid=(depth, n // tn),
        in_specs=[
            pl.BlockSpec((rows, d), lambda l, j: (0, 0)),
            pl.BlockSpec((1, d, tn), lambda l, j: (l, 0, j)),
            pl.BlockSpec((1, 1, tn), lambda l, j: (l, 0, j)),
        ],
        out_specs=pl.BlockSpec((1, rows, tn), lambda l, j: (l, 0, j)),
        out_shape=jax.ShapeDtypeStruct((depth, rows, n), f32),
        compiler_params=_cparams("arbitrary", "arbitrary"),
        name="adaln",
    )(c_all, w_ada, b_ada.reshape(depth, 1, n))


def _modulated(x, g, sh, sc):
    ms = jnp.mean(x * x, axis=-1, keepdims=True)
    y = x * lax.rsqrt(ms + EPS) * g
    return y * (1.0 + sc) + sh


def _modmm_kernel(x_ref, g_ref, sh_ref, sc_ref, w_ref, o_ref, xn_ref):
    @pl.when(pl.program_id(1) == 0)
    def _():
        xn_ref[...] = _modulated(x_ref[...], g_ref[...], sh_ref[0], sc_ref[0]).astype(bf16)

    o_ref[...] = _dot(xn_ref[...], w_ref[...])


def _modmm_gate_kernel(x_ref, g_ref, sh_ref, sc_ref, w_ref, wg_ref, o_ref, og_ref, xn_ref):
    @pl.when(pl.program_id(1) == 0)
    def _():
        xn = _modulated(x_ref[...], g_ref[...], sh_ref[0], sc_ref[0]).astype(bf16)
        xn_ref[...] = xn
        og_ref[...] = _dot(xn, wg_ref[...])

    o_ref[...] = _dot(xn_ref[...], w_ref[...])


def _mod_specs(mod, m, tm, k, rows_per_batch):
    if rows_per_batch >= tm:
        assert rows_per_batch % tm == 0
        per = rows_per_batch // tm
        return mod.reshape(-1, 1, k), pl.BlockSpec((1, 1, k), lambda i, j: (i // per, 0, 0))
    assert rows_per_batch == 1
    rows = mod
    if rows.shape[0] < m:
        rows = jnp.pad(rows, ((0, m - rows.shape[0]), (0, 0)))
    return rows.reshape(1, m, k), pl.BlockSpec((1, tm, k), lambda i, j: (0, i, 0))


def mod_matmul(x, g, shift, scale, w, n_out, rows_per_batch, tm, tn, w_gate=None, name="modmm"):
    m, k = x.shape
    sh, sh_spec = _mod_specs(shift, m, tm, k, rows_per_batch)
    sc, sc_spec = _mod_specs(scale, m, tm, k, rows_per_batch)
    in_specs = [
        pl.BlockSpec((tm, k), lambda i, j: (i, 0)),
        pl.BlockSpec((1, k), lambda i, j: (0, 0)),
        sh_spec,
        sc_spec,
        pl.BlockSpec((k, tn), lambda i, j: (0, j)),
    ]
    args = [x, g.reshape(1, k), sh, sc, w]
    out_specs = pl.BlockSpec((tm, tn), lambda i, j: (i, j))
    out_shape = jax.ShapeDtypeStruct((m, n_out), f32)
    kern = _modmm_kernel
    if w_gate is not None:
        in_specs.append(pl.BlockSpec((k, LANES), lambda i, j: (0, 0)))
        args.append(w_gate)
        out_specs = [out_specs, pl.BlockSpec((tm, LANES), lambda i, j: (i, 0))]
        out_shape = [out_shape, jax.ShapeDtypeStruct((m, LANES), f32)]
        kern = _modmm_gate_kernel
    return pl.pallas_call(
        kern,
        grid=(m // tm, n_out // tn),
        in_specs=in_specs,
        out_specs=out_specs,
        out_shape=out_shape,
        scratch_shapes=[pltpu.VMEM((tm, k), bf16)],
        compiler_params=_cparams("arbitrary", "arbitrary"),
        name=name,
    )(*args)


def _outproj_kernel(a1_ref, a2_ref, w1_ref, w2_ref, x_ref, gt_ref, o_ref):
    y = _dot(a1_ref[...].astype(bf16), w1_ref[...]) + _dot(a2_ref[...].astype(bf16), w2_ref[...])
    o_ref[...] = x_ref[...] + gt_ref[0] * y


def out_proj(a1, a2, w, x, gate, rows_per_batch, tm, tn, name="outproj"):
    m, k1 = a1.shape
    k2 = a2.shape[1]
    n = w.shape[1]
    gt, gt_spec = _mod_specs(gate, m, tm, n, rows_per_batch)
    if gt.shape[1] == 1:
        per = rows_per_batch // tm
        gt_spec = pl.BlockSpec((1, 1, tn), lambda i, j: (i // per, 0, j))
    else:
        gt_spec = pl.BlockSpec((1, tm, tn), lambda i, j: (0, i, j))
    return pl.pallas_call(
        _outproj_kernel,
        grid=(m // tm, n // tn),
        in_specs=[
            pl.BlockSpec((tm, k1), lambda i, j: (i, 0)),
            pl.BlockSpec((tm, k2), lambda i, j: (i, 0)),
            pl.BlockSpec((k1, tn), lambda i, j: (0, j)),
            pl.BlockSpec((k2, tn), lambda i, j: (k1 // k2, j)),
            pl.BlockSpec((tm, tn), lambda i, j: (i, j)),
            gt_spec,
        ],
        out_specs=pl.BlockSpec((tm, tn), lambda i, j: (i, j)),
        out_shape=jax.ShapeDtypeStruct((m, n), f32),
        compiler_params=_cparams("arbitrary", "arbitrary"),
        name=name,
    )(a1, a2, w, w, x, gt)


def _rglru_gates(xc, wr_ref, br, wi_ref, bi, lam):
    xb = xc.astype(bf16)
    rs, gs = [], []
    for hb in range(RG_BLOCKS):
        sl = slice(hb * LANES, (hb + 1) * LANES)
        rs.append(_dot(xb[:, sl], wr_ref[hb]))
        gs.append(_dot(xb[:, sl], wi_ref[hb]))
    r = _sigmoid(jnp.concatenate(rs, axis=1) + br)
    ig = _sigmoid(jnp.concatenate(gs, axis=1) + bi)
    softplus_neg_lam = jnp.maximum(-lam, 0.0) + jnp.log1p(jnp.exp(-jnp.abs(lam)))
    log_a = -RG_C * r * softplus_neg_lam
    a = jnp.exp(log_a)
    u = jnp.sqrt(-jnp.tanh(log_a) * (a * a + 1.0)) * ig * xc
    return a, u


def _rglru_kernel(xa_ref, ga_ref, cw_ref, cb_ref, wr_ref, br_ref, wi_ref, bi_ref, lam_ref,
                  ya_ref, hl_ref, xprev_ref, hc_ref):
    t_idx = pl.program_id(1)
    tt = xa_ref.shape[0]

    @pl.when(t_idx == 0)
    def _():
        xprev_ref[...] = jnp.zeros_like(xprev_ref)
        hc_ref[...] = jnp.zeros_like(hc_ref)

    xa = xa_ref[...]
    xprev = xprev_ref[...]
    row8 = _iota((8, RG_WIDTH), 0)
    xc = cb_ref[...] + cw_ref[CONV_W - 1:CONV_W, :] * xa
    for k in range(1, CONV_W):
        rolled = pltpu.roll(xa, k, 0)
        head = jnp.where(row8 < k, pltpu.roll(xprev, k, 0), rolled[0:8])
        shifted = jnp.concatenate([head, rolled[8:]], axis=0)
        xc = xc + cw_ref[CONV_W - 1 - k:CONV_W - k, :] * shifted
    xprev_ref[...] = xa[tt - 8:tt]

    a, u = _rglru_gates(xc, wr_ref, br_ref[...], wi_ref, bi_ref[...], lam_ref[...])
    row = _iota((tt, RG_WIDTH), 0)
    s = 1
    while s < tt:
        a_sh = pltpu.roll(a, s, 0)
        u_sh = pltpu.roll(u, s, 0)
        ok = row >= s
        u = jnp.where(ok, a * u_sh + u, u)
        a = jnp.where(ok, a * a_sh, a)
        s *= 2
    h = a * hc_ref[...] + u
    hc_ref[...] = h[tt - 1:tt]
    hl_ref[0] = h[tt - 1:tt]
    ya_ref[...] = h * _gelu(ga_ref[...])


def rglru_prompt(proj, bsz, seq, cw, cb, wr, br, wi, bi, lam, tt=256):
    nt = seq // tt
    vec = lambda: pl.BlockSpec((1, RG_WIDTH), lambda b, t: (0, 0))
    return pl.pallas_call(
        _rglru_kernel,
        grid=(bsz, nt),
        in_specs=[
            pl.BlockSpec((tt, RG_WIDTH), lambda b, t: (b * nt + t, 0)),
            pl.BlockSpec((tt, RG_WIDTH), lambda b, t: (b * nt + t, 1)),
            pl.BlockSpec((CONV_W, RG_WIDTH), lambda b, t: (0, 0)),
            vec(),
            pl.BlockSpec((RG_BLOCKS, LANES, LANES), lambda b, t: (0, 0, 0)),
            vec(),
            pl.BlockSpec((RG_BLOCKS, LANES, LANES), lambda b, t: (0, 0, 0)),
            vec(),
            vec(),
        ],
        out_specs=[
            pl.BlockSpec((tt, RG_WIDTH), lambda b, t: (b * nt + t, 0)),
            pl.BlockSpec((1, 1, RG_WIDTH), lambda b, t: (b, 0, 0)),
        ],
        out_shape=[
            jax.ShapeDtypeStruct((bsz * seq, RG_WIDTH), f32),
            jax.ShapeDtypeStruct((bsz, 1, RG_WIDTH), f32),
        ],
        scratch_shapes=[pltpu.VMEM((8, RG_WIDTH), f32), pltpu.VMEM((1, RG_WIDTH), f32)],
        compiler_params=_cparams("arbitrary", "arbitrary"),
        name="rglru_prompt",
    )(proj, proj, cw, cb, wr, br, wi, bi, lam)


def _rglru_step_kernel(xa_ref, ga_ref, b0_ref, b1_ref, b2_ref, h0_ref, cw_ref, cb_ref,
                       wr_ref, br_ref, wi_ref, bi_ref, lam_ref, ya_ref, h_ref):
    xa = xa_ref[...]
    xc = (cb_ref[...] + cw_ref[0:1, :] * b0_ref[...] + cw_ref[1:2, :] * b1_ref[...]
          + cw_ref[2:3, :] * b2_ref[...] + cw_ref[3:4, :] * xa)
    a, u = _rglru_gates(xc, wr_ref, br_ref[...], wi_ref, bi_ref[...], lam_ref[...])
    h = a * h0_ref[...] + u
    h_ref[...] = h
    ya_ref[...] = h * _gelu(ga_ref[...])


def rglru_step(proj_s, conv_state, h0, cw, cb, wr, br, wi, bi, lam):
    n = proj_s.shape[0]
    full = lambda shape: pl.BlockSpec(shape, lambda i: tuple(0 for _ in shape))
    return pl.pallas_call(
        _rglru_step_kernel,
        grid=(1,),
        in_specs=[
            pl.BlockSpec((n, RG_WIDTH), lambda i: (0, 0)),
            pl.BlockSpec((n, RG_WIDTH), lambda i: (0, 1)),
            full((n, RG_WIDTH)), full((n, RG_WIDTH)), full((n, RG_WIDTH)), full((n, RG_WIDTH)),
            full((CONV_W, RG_WIDTH)), full((1, RG_WIDTH)),
            full((RG_BLOCKS, LANES, LANES)), full((1, RG_WIDTH)),
            full((RG_BLOCKS, LANES, LANES)), full((1, RG_WIDTH)), full((1, RG_WIDTH)),
        ],
        out_specs=[full((n, RG_WIDTH)), full((n, RG_WIDTH))],
        out_shape=[jax.ShapeDtypeStruct((n, RG_WIDTH), f32)] * 2,
        compiler_params=_cparams("arbitrary"),
        name="rglru_step",
    )(proj_s, proj_s, conv_state[:, 0], conv_state[:, 1], conv_state[:, 2], h0,
      cw, cb, wr, br, wi, bi, lam)


def _rope_tables(pos):
    half = ROT_DIMS // 2
    inv = ROPE_THETA ** (-jnp.arange(half, dtype=f32) / half)
    ang = pos.astype(f32)[:, None] * inv[None, :]
    cos, sin = jnp.cos(ang), jnp.sin(ang)
    n = pos.shape[0]
    ctab = jnp.concatenate([cos, cos, jnp.ones((n, DH - ROT_DIMS), f32)], axis=1)
    stab = jnp.concatenate([-sin, sin, jnp.zeros((n, DH - ROT_DIMS), f32)], axis=1)
    return ctab, stab


def _rope_apply(x, ctab, stab, lane):
    half = ROT_DIMS // 2
    outs = []
    for hb in range(HEADS):
        xh = x[:, hb * DH:(hb + 1) * DH]
        partner = jnp.where(lane < half, pltpu.roll(xh, DH - half, 1), pltpu.roll(xh, half, 1))
        outs.append(xh * ctab + partner * stab)
    return jnp.concatenate(outs, axis=1)


def _rope_kernel(q_ref, k_ref, c_ref, s_ref, qo_ref, ko_ref):
    ctab, stab = c_ref[...], s_ref[...]
    lane = _iota((q_ref.shape[0], DH), 1)
    qo_ref[...] = _rope_apply(q_ref[...], ctab, stab, lane) * (DH ** -0.5)
    ko_ref[...] = _rope_apply(k_ref[...], ctab, stab, lane)


def rope_qk(proj, ctab, stab, rows_per_seq, tt):
    m = proj.shape[0]
    nt = max(rows_per_seq // tt, 1)
    if ctab.shape[0] == 1:
        tab_spec = pl.BlockSpec((1, DH), lambda i: (0, 0))
    else:
        tab_spec = pl.BlockSpec((tt, DH), lambda i: (i % nt, 0))
    return pl.pallas_call(
        _rope_kernel,
        grid=(m // tt,),
        in_specs=[
            pl.BlockSpec((tt, ATT_W), lambda i: (i, 2)),
            pl.BlockSpec((tt, ATT_W), lambda i: (i, 3)),
            tab_spec, tab_spec,
        ],
        out_specs=[pl.BlockSpec((tt, ATT_W), lambda i: (i, 0))] * 2,
        out_shape=[jax.ShapeDtypeStruct((m, ATT_W), f32)] * 2,
        compiler_params=_cparams("arbitrary"),
        name="rope",
    )(proj, proj, ctab, stab)


def _dil_kernel(*refs, span, first, last):
    if first:
        q_ref, kc_ref, kp_ref, vc_ref, vp_ref = refs[:5]
        outs = refs[5:]
    else:
        q_ref, kc_ref, kp_ref, vc_ref, vp_ref, op_ref, lp_ref = refs[:7]
        outs = refs[7:]
    o_ref = outs[0]
    tq = q_ref.shape[1]
    blk = pl.program_id(2)
    qi = _iota((tq, tq), 0)
    ki = _iota((tq, tq), 1)
    rel_c = qi - ki
    ok_c = (rel_c >= 0) & (rel_c <= span)
    rel_p = rel_c + tq
    ok_p = (rel_p <= span) & (blk > 0)
    for hb in range(HEADS):
        sl = slice(hb * DH, (hb + 1) * DH)
        qh = q_ref[0, :, sl].astype(bf16)
        s_c = jnp.where(ok_c, _dot_nt(qh, kc_ref[0, :, sl].astype(bf16)), NEG_INF)
        s_p = jnp.where(ok_p, _dot_nt(qh, kp_ref[0, :, sl].astype(bf16)), NEG_INF)
        mx = jnp.maximum(jnp.max(s_c, axis=1, keepdims=True), jnp.max(s_p, axis=1, keepdims=True))
        p_c = jnp.exp(s_c - mx)
        p_p = jnp.exp(s_p - mx)
        den = jnp.sum(p_c, axis=1, keepdims=True) + jnp.sum(p_p, axis=1, keepdims=True)
        num = (_dot(p_c.astype(bf16), vc_ref[0, :, sl].astype(bf16))
               + _dot(p_p.astype(bf16), vp_ref[0, :, sl].astype(bf16)))
        o_g = num / den
        lse_g = jnp.broadcast_to(mx + jnp.log(den), (tq, DH))
        if first:
            o_new, lse_new = o_g, lse_g
        else:
            lse_prev = lp_ref[0, :, sl]
            top = jnp.maximum(lse_prev, lse_g)
            e_prev = jnp.exp(lse_prev - top)
            e_g = jnp.exp(lse_g - top)
            tot = e_prev + e_g
            o_new = (op_ref[0, :, sl] * e_prev + o_g * e_g) / tot
            lse_new = top + jnp.log(tot)
        o_ref[0, :, sl] = o_new
        if not last:
            outs[1][0, :, sl] = lse_new


def dilated_prompt(q_rot, k_rot, proj, bsz, seq, tq=128):
    o = lse = None
    n_pat = len(DIL_PATTERNS)
    for gi, (w, d) in enumerate(DIL_PATTERNS):
        first, last = gi == 0, gi == n_pat - 1
        sd = seq // d
        nq = sd // tq
        span = w // d
        qv = q_rot.reshape(bsz, sd, d * ATT_W)
        kv = k_rot.reshape(bsz, sd, d * ATT_W)
        pv = proj.reshape(bsz, sd, d * E_IN)
        vcol = E_IN // ATT_W
        cur = lambda b, r, i: (b, i, r)
        prev = lambda b, r, i: (b, jnp.maximum(i - 1, 0), r)
        vcur = lambda b, r, i: (b, i, r * vcol + vcol - 1)
        vprev = lambda b, r, i: (b, jnp.maximum(i - 1, 0), r * vcol + vcol - 1)
        blk = (1, tq, ATT_W)
        in_specs = [pl.BlockSpec(blk, cur), pl.BlockSpec(blk, cur), pl.BlockSpec(blk, prev),
                    pl.BlockSpec(blk, vcur), pl.BlockSpec(blk, vprev)]
        args = [qv, kv, kv, pv, pv]
        if not first:
            in_specs += [pl.BlockSpec(blk, cur), pl.BlockSpec(blk, cur)]
            args += [o.reshape(bsz, sd, d * ATT_W), lse.reshape(bsz, sd, d * ATT_W)]
        out_specs = [pl.BlockSpec(blk, cur)]
        out_shape = [jax.ShapeDtypeStruct((bsz, sd, d * ATT_W), f32)]
        if not last:
            out_specs.append(pl.BlockSpec(blk, cur))
            out_shape.append(jax.ShapeDtypeStruct((bsz, sd, d * ATT_W), f32))
        res = pl.pallas_call(
            functools.partial(_dil_kernel, span=span, first=first, last=last),
            grid=(bsz, d, nq),
            in_specs=in_specs,
            out_specs=out_specs,
            out_shape=out_shape,
            compiler_params=_cparams("arbitrary", "arbitrary", "arbitrary"),
            name=f"dilattn_d{d}",
        )(*args)
        o = res[0].reshape(bsz * seq, ATT_W)
        if not last:
            lse = res[1].reshape(bsz * seq, ATT_W)
    return o


def _dil_step_kernel(q_ref, kn_ref, vn_ref, k1_ref, k4_ref, k16_ref, v1_ref, v4_ref, v16_ref, o_ref):
    q = q_ref[0]
    head_of_lane = _iota((HEADS, ATT_W), 1) >> 7
    on_head = head_of_lane == _iota((HEADS, ATT_W), 0)
    qblk = jnp.where(on_head, q, 0.0).astype(bf16)
    kn = kn_ref[0].astype(bf16).astype(f32)
    vn = vn_ref[0].astype(bf16).astype(f32)
    s_self = jnp.sum(qblk.astype(f32) * kn, axis=1, keepdims=True)
    o_gs, lse_gs = [], []
    for k_ref, v_ref in ((k1_ref, v1_ref), (k4_ref, v4_ref), (k16_ref, v16_ref)):
        s = _dot_nt(qblk, k_ref[0].astype(bf16))
        mx = jnp.maximum(jnp.max(s, axis=1, keepdims=True), s_self)
        p = jnp.exp(s - mx)
        p_self = jnp.exp(s_self - mx)
        den = jnp.sum(p, axis=1, keepdims=True) + p_self
        num = _dot(p.astype(bf16), v_ref[0].astype(bf16)) + p_self.astype(bf16).astype(f32) * vn
        o_gs.append(num / den)
        lse_gs.append(mx + jnp.log(den))
    top = jnp.maximum(jnp.maximum(lse_gs[0], lse_gs[1]), lse_gs[2])
    es = [jnp.exp(l - top) for l in lse_gs]
    tot = es[0] + es[1] + es[2]
    o = (o_gs[0] * es[0] + o_gs[1] * es[1] + o_gs[2] * es[2]) / tot
    o_ref[0] = jnp.sum(jnp.where(on_head, o, 0.0), axis=0, keepdims=True)


def dilated_step(q_rot, k_rot, proj_s, cache_k, cache_v):
    n, win = cache_k.shape[0], cache_k.shape[1]
    row = lambda a: a.reshape(n, 1, ATT_W)
    one = pl.BlockSpec((1, 1, ATT_W), lambda b: (b, 0, 0))
    args = [row(q_rot), row(k_rot), proj_s.reshape(n, 1, E_IN)]
    in_specs = [one, one, pl.BlockSpec((1, 1, ATT_W), lambda b: (b, 0, E_IN // ATT_W - 1))]
    for cache in (cache_k, cache_v):
        for (w, d) in DIL_PATTERNS:
            nkeys = w // d
            assert nkeys == PAGE and win % d == 0 and (win // d) % nkeys == 0
            args.append(cache.reshape(n, win // d, d * ATT_W))
            in_specs.append(pl.BlockSpec((1, nkeys, ATT_W),
                                         lambda b, blk=(win - w) // d // nkeys: (b, blk, 0)))
    return pl.pallas_call(
        _dil_step_kernel,
        grid=(n,),
        in_specs=in_specs,
        out_specs=pl.BlockSpec((1, 1, ATT_W), lambda b: (b, 0, 0)),
        out_shape=jax.ShapeDtypeStruct((n, 1, ATT_W), f32),
        compiler_params=_cparams("arbitrary"),
        name="dilattn_step",
    )(*args).reshape(n, ATT_W)


def _sb_kernel(bias_ref, q_ref, k_ref, v_ref, o_ref, acc_ref, carry_ref):
    tq = q_ref.shape[1]
    sub = PAGE
    hb = pl.program_id(1)
    qb = pl.program_id(2)
    bias = bias_ref[hb]
    q = (q_ref[0] * (DH ** -0.5)).astype(bf16)
    rr = _iota((sub, 2 * sub), 0)
    cc = _iota((sub, 2 * sub), 1)
    tri = jnp.where((rr > cc) | (cc >= sub), 1.0, 0.0).astype(bf16)
    q_pos = qb * tq + _iota((tq, sub), 0)
    k_loc = _iota((tq, sub), 1)
    acc_ref[...] = jnp.zeros_like(acc_ref)
    carry_ref[...] = jnp.zeros_like(carry_ref)
    nsub = (qb + 1) * (tq // sub)

    def body(it, _):
        sb = nsub - 1 - it
        start = pl.multiple_of(sb * sub, sub)
        kk = k_ref[0, pl.ds(start, sub), :].astype(bf16)
        vv = v_ref[0, pl.ds(start, sub), :].astype(bf16)
        z = _dot_nt(q, kk) + bias
        ok = (start + k_loc) < q_pos
        lp, lm = _log_sigmoid_pair(z)
        lm = jnp.where(ok, lm, 0.0)
        hi = lm.astype(bf16)
        lo = (lm - hi.astype(f32)).astype(bf16)
        cs = _dot(hi, tri) + _dot(lo, tri)
        carry = carry_ref[...]
        w = jnp.where(ok, jnp.exp(lp + cs[:, :sub] + carry), 0.0)
        acc_ref[...] += _dot(w.astype(bf16), vv)
        carry_ref[...] = carry + cs[:, sub:]
        return 0

    lax.fori_loop(0, nsub, body, 0)
    o_ref[0] = acc_ref[...]


def sb_prompt(proj2, bias, bsz, seq, tq=256):
    p3 = proj2.reshape(bsz, seq, proj2.shape[1])
    nq = seq // tq
    return pl.pallas_call(
        _sb_kernel,
        grid=(bsz, HEADS, nq),
        in_specs=[
            pl.BlockSpec(memory_space=pltpu.SMEM),
            pl.BlockSpec((1, tq, DH), lambda b, h, i: (b, i, h)),
            pl.BlockSpec((1, seq, DH), lambda b, h, i: (b, 0, HEADS + h)),
            pl.BlockSpec((1, seq, DH), lambda b, h, i: (b, 0, 2 * HEADS + h)),
        ],
        out_specs=pl.BlockSpec((1, tq, DH), lambda b, h, i: (b, i, h)),
        out_shape=jax.ShapeDtypeStruct((bsz, seq, ATT_W), f32),
        scratch_shapes=[pltpu.VMEM((tq, DH), f32), pltpu.VMEM((tq, PAGE), f32)],
        compiler_params=_cparams("arbitrary", "arbitrary", "arbitrary"),
        name="sb_prompt",
    )(bias, p3, p3, p3).reshape(bsz * seq, ATT_W)


def _sb_step_kernel(pt_ref, qbd_ref, bias_ref, k_ref, v_ref, o_ref, acc_ref, carry_ref):
    j = pl.program_id(1)

    @pl.when(j == 0)
    def _():
        acc_ref[...] = jnp.zeros_like(acc_ref)
        carry_ref[...] = jnp.zeros_like(carry_ref)

    kk = k_ref[0].astype(bf16)
    z = _dot(kk, qbd_ref[0]) + bias_ref[...]
    lp, lm = _log_sigmoid_pair(z)
    rr = _iota((PAGE + 8, PAGE), 0)
    cc = _iota((PAGE + 8, PAGE), 1)
    tri = jnp.where((cc > rr) | (rr >= PAGE), 1.0, 0.0).astype(bf16)
    hi, mid, lo = _split3(lm)
    cs = _dot(tri, hi) + _dot(tri, mid) + _dot(tri, lo)
    carry = carry_ref[...]
    w = jnp.exp(lp + cs[:PAGE] + carry)
    er = _iota((LANES, ATT_W), 0)
    ec = _iota((LANES, ATT_W), 1) >> 7
    expand = jnp.where(er == ec, 1.0, 0.0).astype(bf16)
    wexp = _dot(w.astype(bf16), expand)
    contrib = wexp * v_ref[0].astype(bf16).astype(f32)
    acc_ref[...] += jnp.sum(contrib.reshape(PAGE // 8, 8, ATT_W), axis=0)
    carry_ref[...] = carry + cs[PAGE:PAGE + 1]

    @pl.when(j == pl.num_programs(1) - 1)
    def _():
        o_ref[0] = jnp.sum(acc_ref[...], axis=0, keepdims=True)


def sb_step(q_s, bias, cache_k, cache_v, page_table):
    n, n_pages = page_table.shape
    pool = cache_k.shape[0]
    qs = q_s * (DH ** -0.5)
    eye = (jnp.arange(ATT_W)[:, None] // DH == jnp.arange(LANES)[None, :]).astype(f32)
    qbd = (qs[:, :, None] * eye[None]).astype(bf16)
    bias_row = jnp.pad(bias, (0, LANES - HEADS)).reshape(1, LANES)
    page = lambda b, j, pt: (pt[b, n_pages - 1 - j], 0, 0)
    grid_spec = pltpu.PrefetchScalarGridSpec(
        num_scalar_prefetch=1,
        grid=(n, n_pages),
        in_specs=[
            pl.BlockSpec((1, ATT_W, LANES), lambda b, j, pt: (b, 0, 0)),
            pl.BlockSpec((1, LANES), lambda b, j, pt: (0, 0)),
            pl.BlockSpec((1, PAGE, ATT_W), page),
            pl.BlockSpec((1, PAGE, ATT_W), page),
        ],
        out_specs=pl.BlockSpec((1, 1, ATT_W), lambda b, j, pt: (b, 0, 0)),
        scratch_shapes=[pltpu.VMEM((8, ATT_W), f32), pltpu.VMEM((1, LANES), f32)],
    )
    return pl.pallas_call(
        _sb_step_kernel,
        grid_spec=grid_spec,
        out_shape=jax.ShapeDtypeStruct((n, 1, ATT_W), f32),
        compiler_params=_cparams("arbitrary", "arbitrary"),
        name="sb_step",
    )(page_table, qbd, bias_row, cache_k.reshape(pool, PAGE, ATT_W),
      cache_v.reshape(pool, PAGE, ATT_W)).reshape(n, ATT_W)


def _mlstm_kernel(q_ref, k_ref, v_ref, og_ref, gate_ref, gb_ref, gn_ref,
                  h_ref, c_out, n_out, m_out, c_s, n_s, m_s):
    ci = pl.program_id(1)
    L = CHUNK

    @pl.when(ci == 0)
    def _():
        c_s[...] = jnp.zeros_like(c_s)
        n_s[...] = jnp.zeros_like(n_s)
        m_s[...] = jnp.zeros_like(m_s)

    gt = gate_ref[...] + gb_ref[...]
    gt_t = gt.T
    ri = _iota((L, L), 0)
    li = _iota((L, L), 1)
    causal = li <= ri
    tri_incl = jnp.where(causal, 1.0, 0.0).astype(bf16)
    tri_incl_t = jnp.where(ri <= li, 1.0, 0.0).astype(bf16)
    for h in range(M_HEADS):
        sl = slice(h * M_DH, (h + 1) * M_DH)
        ig_col = gt[:, h:h + 1]
        ig_row = gt_t[h:h + 1, :]
        lf_col = _log_sigmoid_pair(gt[:, M_HEADS + h:M_HEADS + h + 1])[0]
        lf_row = _log_sigmoid_pair(gt_t[M_HEADS + h:M_HEADS + h + 1, :])[0]
        c_hi, c_mid, c_lo = _split3(jnp.broadcast_to(lf_col, (L, L)))
        bcum_col = _dot(tri_incl, c_hi) + _dot(tri_incl, c_mid) + _dot(tri_incl, c_lo)
        r_hi, r_mid, r_lo = _split3(jnp.broadcast_to(lf_row, (L, L)))
        bcum_row = _dot(r_hi, tri_incl_t) + _dot(r_mid, tri_incl_t) + _dot(r_lo, tri_incl_t)
        m_prev = m_s[h:h + 1, :]
        dlog = jnp.where(causal, bcum_col - bcum_row + ig_row, NEG_INF)
        inter = bcum_col + m_prev
        m_t = jnp.maximum(inter, jnp.max(dlog, axis=1, keepdims=True))
        dw = jnp.exp(dlog - m_t)
        iw = jnp.exp(inter - m_t)
        qh = q_ref[:, sl]
        kh = k_ref[:, sl] * (M_DH ** -0.5)
        vh = v_ref[:, sl]
        qb, kb, vb = qh.astype(bf16), kh.astype(bf16), vh.astype(bf16)
        sw = dw * _dot_nt(qb, kb)
        c_prev = c_s[h]
        n_prev = n_s[h:h + 1, :]
        iw_col = iw[:, 0:1]
        num = _dot(sw.astype(bf16), vb) + iw_col * _dot_nt(qb, c_prev.astype(bf16))
        qn = jnp.sum(qb.astype(f32) * n_prev.astype(bf16).astype(f32), axis=1, keepdims=True)
        den = jnp.sum(sw, axis=1, keepdims=True) + iw_col * qn
        m_col = m_t[:, 0:1]
        hout = num / jnp.maximum(jnp.abs(den), jnp.exp(-m_col))
        m_last = m_t[L - 1:L, :]
        b_last = bcum_col[L - 1:L, :]
        wl_col = jnp.exp(b_last[:, 0:1] - bcum_col[:, 0:1] + ig_col - m_last[:, 0:1])
        wl_row = jnp.exp(b_last - bcum_row[0:1, :] + ig_row - m_last)
        decay = jnp.exp(b_last + m_prev - m_last)
        dsc = decay[:, 0:1]
        c_s[h] = dsc * c_prev + _dot((vh * wl_col).T.astype(bf16), kb)
        wl8 = jnp.broadcast_to(wl_row, (8, L)).astype(bf16)
        n_s[h:h + 1, :] = dsc * n_prev + _dot(wl8, kb)[0:1, :]
        m_s[h:h + 1, :] = m_last
        hn = hout * lax.rsqrt(jnp.mean(hout * hout, axis=1, keepdims=True) + EPS)
        h_ref[:, sl] = hn * gn_ref[:, sl] * _sigmoid(og_ref[:, sl])

    @pl.when(ci == pl.num_programs(1) - 1)
    def _():
        c_out[0] = c_s[...]
        n_out[0] = n_s[...]
        m_out[0] = m_s[...]


def mlstm_prompt(proj2, gates, gate_bias, g_mnorm, bsz, seq):
    nc = seq // CHUNK
    col = lambda c: pl.BlockSpec((CHUNK, M_WIDTH), lambda b, i, c=c: (b * nc + i, c))
    hm, c1, n1, m1 = pl.pallas_call(
        _mlstm_kernel,
        grid=(bsz, nc),
        in_specs=[
            col(3), col(4), col(5), col(6),
            pl.BlockSpec((CHUNK, LANES), lambda b, i: (b * nc + i, 0)),
            pl.BlockSpec((1, LANES), lambda b, i: (0, 0)),
            pl.BlockSpec((1, M_WIDTH), lambda b, i: (0, 0)),
        ],
        out_specs=[
            pl.BlockSpec((CHUNK, M_WIDTH), lambda b, i: (b * nc + i, 0)),
            pl.BlockSpec((1, M_HEADS, M_DH, M_DH), lambda b, i: (b, 0, 0, 0)),
            pl.BlockSpec((1, M_HEADS, M_DH), lambda b, i: (b, 0, 0)),
            pl.BlockSpec((1, M_HEADS, LANES), lambda b, i: (b, 0, 0)),
        ],
        out_shape=[
            jax.ShapeDtypeStruct((bsz * seq, M_WIDTH), f32),
            jax.ShapeDtypeStruct((bsz, M_HEADS, M_DH, M_DH), f32),
            jax.ShapeDtypeStruct((bsz, M_HEADS, M_DH), f32),
            jax.ShapeDtypeStruct((bsz, M_HEADS, LANES), f32),
        ],
        scratch_shapes=[
            pltpu.VMEM((M_HEADS, M_DH, M_DH), f32),
            pltpu.VMEM((M_HEADS, M_DH), f32),
            pltpu.VMEM((M_HEADS, LANES), f32),
        ],
        compiler_params=_cparams("arbitrary", "arbitrary"),
        name="mlstm_prompt",
    )(proj2, proj2, proj2, proj2, gates, gate_bias, g_mnorm)
    return hm, c1, n1, m1[:, :, 0]


def _mlstm_step_kernel(q_ref, k_ref, v_ref, og_ref, gn_ref, ig_ref, fg_ref, c_ref, n_ref, m_ref,
                       h_ref, c_out, n_out, m_out):
    q = q_ref[0, 0]
    k = k_ref[0, 0] * (M_DH ** -0.5)
    v = v_ref[0, 0]
    ig = ig_ref[0, 0]
    lf = _log_sigmoid_pair(fg_ref[0, 0])[0]
    m0 = m_ref[0, 0]
    c0 = c_ref[0, 0]
    n0 = n_ref[0, 0]
    inter = lf + m0
    m_t = jnp.maximum(inter, ig)
    dw = jnp.exp(ig - m_t)
    iw = jnp.exp(inter - m_t)
    rnd = lambda a: a.astype(bf16).astype(f32)
    qr, kr, vr = rnd(q), rnd(k), rnd(v)
    qk = jnp.sum(qr * kr, axis=1, keepdims=True)
    sw = dw * qk
    cq = jnp.sum(rnd(c0) * qr, axis=1, keepdims=True)
    num = rnd(sw) * vr + iw * cq
    den = sw + iw * jnp.sum(rnd(n0) * qr, axis=1, keepdims=True)
    hout = num / jnp.maximum(jnp.abs(den), jnp.exp(-m_t))
    c_out[0, 0] = iw * c0 + rnd(dw * v) * kr
    n_out[0, 0] = iw * n0 + rnd(dw) * kr
    m_out[0, 0] = m_t
    hn = hout * lax.rsqrt(jnp.mean(hout * hout, axis=0, keepdims=True) + EPS)
    h_ref[0, 0] = hn * gn_ref[0] * _sigmoid(og_ref[0, 0])


def mlstm_step(proj2_s, gates_s, gate_bias, g_mnorm, c0, n0, m0):
    n = proj2_s.shape[0]
    base = 3 * ATT_W
    seg = lambda i: proj2_s[:, base + i * M_WIDTH: base + (i + 1) * M_WIDTH].reshape(n, M_HEADS, M_DH)
    g = gates_s + gate_bias
    rowb = pl.BlockSpec((1, 1, 1, M_DH), lambda b, h: (b, h, 0, 0))
    colb = pl.BlockSpec((1, 1, M_DH, 1), lambda b, h: (b, h, 0, 0))
    scal = pl.BlockSpec((1, 1, 1, 1), lambda b, h: (b, h, 0, 0))
    hcol, c1, n1, m1 = pl.pallas_call(
        _mlstm_step_kernel,
        grid=(n, M_HEADS),
        in_specs=[
            rowb, rowb, colb, colb,
            pl.BlockSpec((1, M_DH, 1), lambda b, h: (h, 0, 0)),
            scal, scal,
            pl.BlockSpec((1, 1, M_DH, M_DH), lambda b, h: (b, h, 0, 0)),
            rowb, scal,
        ],
        out_specs=[colb, pl.BlockSpec((1, 1, M_DH, M_DH), lambda b, h: (b, h, 0, 0)), rowb, scal],
        out_shape=[
            jax.ShapeDtypeStruct((n, M_HEADS, M_DH, 1), f32),
            jax.ShapeDtypeStruct((n, M_HEADS, M_DH, M_DH), f32),
            jax.ShapeDtypeStruct((n, M_HEADS, 1, M_DH), f32),
            jax.ShapeDtypeStruct((n, M_HEADS, 1, 1), f32),
        ],
        compiler_params=_cparams("arbitrary", "arbitrary"),
        name="mlstm_step",
    )(seg(0)[:, :, None, :], seg(1)[:, :, None, :], seg(2)[..., None], seg(3)[..., None],
      g_mnorm.reshape(M_HEADS, M_DH, 1),
      g[:, 0:M_HEADS].reshape(n, M_HEADS, 1, 1), g[:, M_HEADS:2 * M_HEADS].reshape(n, M_HEADS, 1, 1),
      c0, n0[:, :, None, :], m0.reshape(n, M_HEADS, 1, 1))
    return hcol.reshape(n, M_WIDTH), c1, n1.reshape(n, M_HEADS, M_DH), m1.reshape(n, M_HEADS)


def _peerq_kernel(x_ref, g_ref, sh_ref, sc_ref, wt_ref, qt_ref, xmt_ref, xs_ref):
    @pl.when(pl.program_id(1) == 0)
    def _():
        xm = _modulated(x_ref[...], g_ref[...], sh_ref[0], sc_ref[0])
        xt = xm.T.astype(bf16)
        xs_ref[...] = xt
        xmt_ref[...] = xt

    qt_ref[...] = _dot(wt_ref[...], xs_ref[...])


def peer_query(x, g, shift, scale, wq_t, rows_per_batch, tm, tn=512):
    m, k = x.shape
    n = wq_t.shape[0]
    sh, sh_spec = _mod_specs(shift, m, tm, k, rows_per_batch)
    sc, sc_spec = _mod_specs(scale, m, tm, k, rows_per_batch)
    return pl.pallas_call(
        _peerq_kernel,
        grid=(m // tm, n // tn),
        in_specs=[
            pl.BlockSpec((tm, k), lambda i, j: (i, 0)),
            pl.BlockSpec((1, k), lambda i, j: (0, 0)),
            sh_spec, sc_spec,
            pl.BlockSpec((tn, k), lambda i, j: (j, 0)),
        ],
        out_specs=[pl.BlockSpec((tn, tm), lambda i, j: (j, i)),
                   pl.BlockSpec((k, tm), lambda i, j: (0, i))],
        out_shape=[jax.ShapeDtypeStruct((n, m), f32), jax.ShapeDtypeStruct((k, m), bf16)],
        scratch_shapes=[pltpu.VMEM((k, tm), bf16)],
        compiler_params=_cparams("arbitrary", "arbitrary"),
        name="peer_query",
    )(x, g.reshape(1, k), sh, sc, wq_t)


_N_TOP = PEER_TOPK + 1
_CAND_PAIRS = [(a, b) for a in range(_N_TOP) for b in range(_N_TOP) if (a + 1) * (b + 1) <= _N_TOP]
_N_CAND = -(-len(_CAND_PAIRS) // 8) * 8


def _extract_top(cur, ridx, n):
    vals = []
    big = cur.shape[0]
    for _ in range(n):
        mx = jnp.max(cur, axis=0, keepdims=True)
        first = jnp.min(jnp.where(cur == mx, ridx, big), axis=0, keepdims=True)
        cur = jnp.where(ridx == first, NEG_INF, cur)
        vals.append(mx)
    return vals


def _router_kernel(qt_ref, keys_ref, e1_ref, e2_ref, th_ref, cand_ref):
    ridx = _iota((N_KEYS, LANES), 0)
    cidx = _iota((_N_CAND, LANES), 0)
    k0 = keys_ref[0].astype(bf16)
    k1 = keys_ref[1].astype(bf16)
    cand_ref[...] = jnp.full(cand_ref.shape, NEG_INF, f32)

    def body(h, _):
        r0 = pl.multiple_of(h * 2 * N_KEYS, 2 * N_KEYS)
        s1 = _dot(k0, qt_ref[pl.ds(r0, N_KEYS), :].astype(bf16))
        s2 = _dot(k1, qt_ref[pl.ds(r0 + N_KEYS, N_KEYS), :].astype(bf16))
        top_a = _extract_top(s1, ridx, _N_TOP)
        top_b = _extract_top(s2, ridx, _N_TOP)
        for ci, (a, b) in enumerate(_CAND_PAIRS):
            cand_ref[ci:ci + 1, :] = top_a[a] + top_b[b]
        cs = _extract_top(cand_ref[...], cidx, _N_TOP)
        z = jnp.ones_like(cs[0])
        for r in range(1, PEER_TOPK):
            z = z + jnp.exp(cs[r] - cs[0])
        inv_z = 1.0 / z
        mid = 0.5 * (cs[PEER_TOPK - 1] + cs[PEER_TOPK])
        o0 = pl.multiple_of(h * N_KEYS, N_KEYS)
        e1 = jnp.exp(s1 - top_a[0]) * inv_z
        e1_ref[:, h] = e1.reshape(N_KEYS // 8, 8, LANES)
        e2_ref[pl.ds(o0, N_KEYS), :] = jnp.exp(s2 - top_b[0])
        th_ref[pl.ds(h, 1), :] = jnp.exp(mid - cs[0]) * inv_z
        return 0

    lax.fori_loop(0, PEER_HEADS, body, 0)


def peer_router(qt, keys):
    n, m = qt.shape
    half = PEER_HEADS * N_KEYS
    return pl.pallas_call(
        _router_kernel,
        grid=(m // LANES,),
        in_specs=[pl.BlockSpec((n, LANES), lambda i: (0, i)),
                  pl.BlockSpec((2, N_KEYS, N_KEYS), lambda i: (0, 0, 0))],
        out_specs=[pl.BlockSpec((N_KEYS // 8, PEER_HEADS, 8, LANES), lambda i: (0, 0, 0, i)),
                   pl.BlockSpec((half, LANES), lambda i: (0, i)),
                   pl.BlockSpec((PEER_HEADS, LANES), lambda i: (0, i))],
        out_shape=[jax.ShapeDtypeStruct((N_KEYS // 8, PEER_HEADS, 8, m), f32),
                   jax.ShapeDtypeStruct((half, m), f32),
                   jax.ShapeDtypeStruct((PEER_HEADS, m), f32)],
        scratch_shapes=[pltpu.VMEM((_N_CAND, LANES), f32)],
        compiler_params=_cparams("arbitrary"),
        name="peer_router",
    )(qt, keys)


def _experts_kernel(xmt_ref, e1_ref, e2_ref, th_ref, u_ref, v_ref, x_ref, gt_ref, gf_ref,
                    o_ref, wt_ref, *, final_norm):
    c = pl.program_id(1)
    te, tm = wt_ref.shape
    ni = te // N_KEYS

    @pl.when(c == 0)
    def _():
        o_ref[...] = jnp.zeros_like(o_ref)

    act = _gelu(_dot(u_ref[...], xmt_ref[...]))
    for ii in range(ni):
        for lc in range(tm // LANES):
            ls = slice(lc * LANES, (lc + 1) * LANES)
            g = jnp.zeros((N_KEYS, LANES), f32)
            for h in range(PEER_HEADS):
                row = e1_ref[ii // 8, h, ii % 8:ii % 8 + 1, ls]
                p = e2_ref[h * N_KEYS:(h + 1) * N_KEYS, ls] * row
                g = g + jnp.where(p >= th_ref[h:h + 1, ls], p, 0.0)
            wt_ref[ii * N_KEYS:(ii + 1) * N_KEYS, ls] = g * act[ii * N_KEYS:(ii + 1) * N_KEYS, ls]
    o_ref[...] += _dot(wt_ref[...].T.astype(bf16), v_ref[...])

    @pl.when(c == pl.num_programs(1) - 1)
    def _():
        y = x_ref[...] + gt_ref[0] * o_ref[...]
        if final_norm:
            ms = jnp.mean(y * y, axis=-1, keepdims=True)
            y = y * lax.rsqrt(ms + EPS) * gf_ref[...]
        o_ref[...] = y


def peer_experts(xmt, e1t, e2t, th, u, v, x, gate, g_final, rows_per_batch, tm, te, final_norm):
    m, d = x.shape
    gt, gt_spec = _mod_specs(gate, m, tm, d, rows_per_batch)
    half = PEER_HEADS * N_KEYS
    return pl.pallas_call(
        functools.partial(_experts_kernel, final_norm=final_norm),
        grid=(m // tm, N_EXPERTS // te),
        in_specs=[
            pl.BlockSpec((d, tm), lambda i, c: (0, i)),
            pl.BlockSpec((te // N_KEYS // 8, PEER_HEADS, 8, tm), lambda i, c: (c, 0, 0, i)),
            pl.BlockSpec((half, tm), lambda i, c: (0, i)),
            pl.BlockSpec((PEER_HEADS, tm), lambda i, c: (0, i)),
            pl.BlockSpec((te, d), lambda i, c: (c, 0)),
            pl.BlockSpec((te, d), lambda i, c: (c, 0)),
            pl.BlockSpec((tm, d), lambda i, c: (i, 0)),
            gt_spec,
            pl.BlockSpec((1, d), lambda i, c: (0, 0)),
        ],
        out_specs=pl.BlockSpec((tm, d), lambda i, c: (i, 0)),
        out_shape=jax.ShapeDtypeStruct((m, d), f32),
        scratch_shapes=[pltpu.VMEM((te, tm), f32)],
        compiler_params=_cparams("arbitrary", "arbitrary"),
        name="peer_experts",
    )(xmt, e1t, e2t, th, u, v, x, gt, g_final.reshape(1, d))


def peer_block(x, g, shift, scale, gate, wq_t, keys, u, v, g_final, rows_per_batch, tm, te, final_norm):
    qt, xmt = peer_query(x, g, shift, scale, wq_t, rows_per_batch, tm)
    e1t, e2t, th = peer_router(qt, keys)
    return peer_experts(xmt, e1t, e2t, th, u, v, x, gate, g_final, rows_per_batch, tm, te, final_norm)


def kernel(x_prompt, x_sample, c_prompt, c_sample, state_rglru_conv, state_rglru_h, cache_swa_k, cache_swa_v, cache_sb_k, cache_sb_v, state_mlstm_C, state_mlstm_n, state_mlstm_m, page_table, w_ada, b_ada, g_norm_mix, g_norm_ffn, e_w_in, e_conv_w, e_conv_b, e_w_r, e_b_r, e_w_i, e_b_i, e_lambda, e_w_out, o_w_in, o_b_if, o_sb_bias, o_g_mnorm, o_w_out, peer_w_q, peer_keys, peer_u, peer_v, g_final):
    bp, seq, d = x_prompt.shape
    bs = x_sample.shape[0]
    mp = bp * seq
    pad_s = LANES
    xp = x_prompt.reshape(mp, d)
    xs = x_sample.reshape(bs, d)

    c_rows = 16
    c_all = jnp.concatenate([c_prompt, c_sample, jnp.zeros((c_rows - bp - bs, d), f32)], axis=0)
    mod = adaln_all(c_all, w_ada, b_ada)

    def mods(layer):
        parts = [mod[layer, :, i * d:(i + 1) * d] for i in range(6)]
        return [p[:bp] for p in parts], [p[bp:bp + bs] for p in parts]

    ctab_p, stab_p = _rope_tables(jnp.arange(seq, dtype=jnp.int32))
    ctab_s, stab_s = _rope_tables(jnp.full((1,), PAST_LEN, jnp.int32))

    TM = 512
    m_p, m_s = mods(0)
    w_in = e_w_in[0].astype(bf16)
    w_out = e_w_out[0].astype(bf16)
    cw, cb = e_conv_w[0], e_conv_b[0].reshape(1, -1)
    wr, wi = e_w_r[0].astype(bf16), e_w_i[0].astype(bf16)
    br, bi, lam = e_b_r[0].reshape(1, -1), e_b_i[0].reshape(1, -1), e_lambda[0].reshape(1, -1)

    proj_p = mod_matmul(xp, g_norm_mix[0], m_p[0], m_p[1], w_in, E_IN, seq, 1024, 512, name="e_in_p")
    proj_s = mod_matmul(xs, g_norm_mix[0], m_s[0], m_s[1], w_in, E_IN, 1, bs, 512, name="e_in_s")

    ya_p, h_p = rglru_prompt(proj_p, bp, seq, cw, cb, wr, br, wi, bi, lam)
    ya_s, h_s = rglru_step(proj_s, state_rglru_conv[0], state_rglru_h[0], cw, cb, wr, br, wi, bi, lam)
    conv_p = proj_p.reshape(bp, seq, E_IN)[:, seq - (CONV_W - 1):, :RG_WIDTH]
    conv_s = jnp.concatenate([state_rglru_conv[0][:, 1:], proj_s[:, None, :RG_WIDTH]], axis=1)

    q_p, k_p = rope_qk(proj_p, ctab_p, stab_p, seq, 512)
    q_s, k_s = rope_qk(proj_s, ctab_s, stab_s, 1, bs)
    o_p = dilated_prompt(q_p, k_p, proj_p, bp, seq)
    o_s = dilated_step(q_s, k_s, proj_s, cache_swa_k[0], cache_swa_v[0])
    wl = min(2048, seq)
    swa_k_p = k_p.reshape(bp, seq, HEADS, DH)[:, seq - wl:]
    swa_v_p = proj_p.reshape(bp, seq, E_IN)[:, seq - wl:, E_IN - ATT_W:].reshape(bp, wl, HEADS, DH)
    swa_k_s = k_s.reshape(bs, 1, HEADS, DH)
    swa_v_s = proj_s[:, E_IN - ATT_W:].reshape(bs, 1, HEADS, DH)

    xp = out_proj(ya_p, o_p, w_out, xp, m_p[2], seq, 1024, 512, name="e_out_p")
    xs = out_proj(ya_s, o_s, w_out, xs, m_s[2], 1, bs, 512, name="e_out_s")

    def peer_layer(layer, xp, xs, m_p, m_s, final_norm):
        wq_t = peer_w_q[layer].T.astype(bf16)
        u = peer_u[layer].astype(bf16)
        v = peer_v[layer].astype(bf16)
        xp = peer_block(xp, g_norm_ffn[layer], m_p[3], m_p[4], m_p[5], wq_t, peer_keys[layer], u, v,
                        g_final, seq, TM, 1024, final_norm)
        xs_pad = jnp.pad(xs, ((0, pad_s - bs), (0, 0)))
        xs_new = peer_block(xs_pad, g_norm_ffn[layer], m_s[3], m_s[4], m_s[5], wq_t, peer_keys[layer],
                            u, v, g_final, 1, pad_s, 1024, final_norm)
        return xp, xs_new[:bs]

    xp, xs = peer_layer(0, xp, xs, m_p, m_s, False)

    m_p, m_s = mods(1)
    w_in2 = o_w_in[0].astype(bf16)
    w_gate = jnp.pad(w_in2[:, O_MAIN:], ((0, 0), (0, LANES - 2 * M_HEADS)))
    w_out2 = o_w_out[0].astype(bf16)
    gate_bias = jnp.pad(o_b_if[0].reshape(1, 2 * M_HEADS), ((0, 0), (0, LANES - 2 * M_HEADS)))
    gmn = o_g_mnorm[0].reshape(1, M_WIDTH)

    proj2_p, gates_p = mod_matmul(xp, g_norm_mix[1], m_p[0], m_p[1], w_in2, O_MAIN, seq, 1024, 512,
                                  w_gate=w_gate, name="o_in_p")
    proj2_s, gates_s = mod_matmul(xs, g_norm_mix[1], m_s[0], m_s[1], w_in2, O_MAIN, 1, bs, 512,
                                  w_gate=w_gate, name="o_in_s")

    oc_p = sb_prompt(proj2_p, o_sb_bias[0], bp, seq)
    oc_s = sb_step(proj2_s[:, :ATT_W], o_sb_bias[0], cache_sb_k[0], cache_sb_v[0], page_table)
    hm_p, mC_p, mn_p, mm_p = mlstm_prompt(proj2_p, gates_p, gate_bias, gmn, bp, seq)
    hm_s, mC_s, mn_s, mm_s = mlstm_step(proj2_s, gates_s, gate_bias, gmn,
                                        state_mlstm_C[0], state_mlstm_n[0], state_mlstm_m[0])

    p3 = proj2_p.reshape(bp, seq, O_MAIN)
    n_pg = seq // PAGE
    sb_k_p = p3[:, :, ATT_W:2 * ATT_W].reshape(bp, n_pg, PAGE, HEADS, DH)
    sb_v_p = p3[:, :, 2 * ATT_W:3 * ATT_W].reshape(bp, n_pg, PAGE, HEADS, DH)
    sb_k_s = proj2_s[:, ATT_W:2 * ATT_W].reshape(bs, 1, HEADS, DH)
    sb_v_s = proj2_s[:, 2 * ATT_W:3 * ATT_W].reshape(bs, 1, HEADS, DH)

    xp = out_proj(oc_p, hm_p, w_out2, xp, m_p[2], seq, 1024, 512, name="o_out_p")
    xs = out_proj(oc_s, hm_s, w_out2, xs, m_s[2], 1, bs, 512, name="o_out_s")
    xp, xs = peer_layer(1, xp, xs, m_p, m_s, True)

    y_prompt = xp.reshape(bp, seq, d)
    y_sample = xs.reshape(bs, 1, d)
    st = lambda a: a[None]
    return (y_prompt, y_sample, st(conv_p), st(conv_s), st(h_p.reshape(bp, RG_WIDTH)), st(h_s),
            st(swa_k_p), st(swa_k_s), st(swa_v_p), st(swa_v_s),
            st(sb_k_p), st(sb_k_s), st(sb_v_p), st(sb_v_s),
            st(mC_p), st(mC_s), st(mn_p), st(mn_s), st(mm_p), st(mm_s))
```

```python
import functools
import math

import jax
import jax.numpy as jnp
from jax import lax
from jax.experimental import pallas as pl
from jax.experimental.pallas import tpu as pltpu

f32 = jnp.float32
bf16 = jnp.bfloat16

D_MODEL = 2048
PAST_LEN = 16384
PAGE = 128
RG_WIDTH = 1024
RG_BLOCKS = 8
CONV_W = 4
RG_C = 8.0
HEADS = 8
DH = 128
ATT_W = HEADS * DH
DIL_PATTERNS = ((128, 1), (512, 4), (2048, 16))
ROT_DIMS = 32
ROPE_THETA = 500000.0
M_HEADS = 4
M_DH = 256
M_WIDTH = M_HEADS * M_DH
CHUNK = 128
E_IN = 2 * RG_WIDTH + 3 * ATT_W
O_MAIN = 3 * ATT_W + 4 * M_WIDTH
PEER_HEADS = 8
N_KEYS = 128
N_EXPERTS = N_KEYS * N_KEYS
PEER_TOPK = 16
EPS = 1e-6
LANES = 128
VMEM_LIMIT = 56 * 1024 * 1024
NEG_INF = float("-inf")


def _cparams(*sem):
    return pltpu.CompilerParams(dimension_semantics=sem, vmem_limit_bytes=VMEM_LIMIT)


def _dot(a, b):
    return jnp.dot(a, b, preferred_element_type=f32)


def _dot_nt(a, b):
    return lax.dot_general(a, b, (((1,), (1,)), ((), ())), preferred_element_type=f32)


def _split3(x):
    hi = x.astype(bf16)
    r = x - hi.astype(f32)
    mid = r.astype(bf16)
    lo = (r - mid.astype(f32)).astype(bf16)
    return hi, mid, lo


def _sigmoid(x):
    return 1.0 / (1.0 + jnp.exp(-x))


def _log_sigmoid_pair(z):
    l1p = jnp.log1p(jnp.exp(-jnp.abs(z)))
    return jnp.minimum(z, 0.0) - l1p, -jnp.maximum(z, 0.0) - l1p


def _softplus(z):
    return jnp.maximum(z, 0.0) + jnp.log(1.0 + jnp.exp(-jnp.abs(z)))


def _gelu(x):
    c = math.sqrt(2.0 / math.pi)
    return 0.5 * x * (1.0 + jnp.tanh(c * (x + 0.044715 * (x * x * x))))


def _iota(shape, dim):
    return lax.broadcasted_iota(jnp.int32, shape, dim)


def _rowsum_bcast(x):
    ones = jnp.ones((LANES, LANES), bf16)
    hi = x.astype(bf16)
    lo = (x - hi.astype(f32)).astype(bf16)
    return _dot(hi, ones) + _dot(lo, ones)


def _adaln_kernel(c_ref, w_ref, b_ref, o_ref):
    c = c_ref[...]
    s = c * _sigmoid(c)
    w = w_ref[0]
    s_hi = s.astype(bf16)
    s_lo = (s - s_hi.astype(f32)).astype(bf16)
    w_hi = w.astype(bf16)
    w_lo = (w - w_hi.astype(f32)).astype(bf16)
    o_ref[0] = _dot(s_hi, w_hi) + _dot(s_hi, w_lo) + _dot(s_lo, w_hi) + b_ref[0]


def adaln_all(c_all, w_ada, b_ada):
    depth, d, n = w_ada.shape
    rows = c_all.shape[0]
    tn = 1024
    return pl.pallas_call(
        _adaln_kernel,
        grid=(depth, n // tn),
        in_specs=[
            pl.BlockSpec((rows, d), lambda l, j: (0, 0)),
            pl.BlockSpec((1, d, tn), lambda l, j: (l, 0, j)),
            pl.BlockSpec((1, 1, tn), lambda l, j: (l, 0, j)),
        ],
        out_specs=pl.BlockSpec((1, rows, tn), lambda l, j: (l, 0, j)),
        out_shape=jax.ShapeDtypeStruct((depth, rows, n), f32),
        compiler_params=_cparams("arbitrary", "arbitrary"),
        name="adaln",
    )(c_all, w_ada, b_ada.reshape(depth, 1, n))


def _modulated(x, g, sh, sc):
    ms = jnp.mean(x * x, axis=-1, keepdims=True)
    y = x * lax.rsqrt(ms + EPS) * g
    return y * (1.0 + sc) + sh


def _modmm_kernel(x_ref, g_ref, sh_ref, sc_ref, w_ref, o_ref, xn_ref):
    @pl.when(pl.program_id(1) == 0)
    def _():
        xn_ref[...] = _modulated(x_ref[...], g_ref[...], sh_ref[0], sc_ref[0]).astype(bf16)

    o_ref[...] = _dot(xn_ref[...], w_ref[...])


def _modmm_gate_kernel(x_ref, g_ref, sh_ref, sc_ref, w_ref, wg_ref, o_ref, og_ref, xn_ref):
    @pl.when(pl.program_id(1) == 0)
    def _():
        xn = _modulated(x_ref[...], g_ref[...], sh_ref[0], sc_ref[0]).astype(bf16)
        xn_ref[...] = xn
        og_ref[...] = _dot(xn, wg_ref[...])

    o_ref[...] = _dot(xn_ref[...], w_ref[...])


def _mod_specs(mod, m, tm, k, rows_per_batch):
    if rows_per_batch >= tm:
        assert rows_per_batch % tm == 0
        per = rows_per_batch // tm
        return mod.reshape(-1, 1, k), pl.BlockSpec((1, 1, k), lambda i, j: (i // per, 0, 0))
    assert rows_per_batch == 1
    rows = mod
    if rows.shape[0] < m:
        rows = jnp.pad(rows, ((0, m - rows.shape[0]), (0, 0)))
    return rows.reshape(1, m, k), pl.BlockSpec((1, tm, k), lambda i, j: (0, i, 0))


def mod_matmul(x, g, shift, scale, w, n_out, rows_per_batch, tm, tn, w_gate=None, name="modmm"):
    m, k = x.shape
    sh, sh_spec = _mod_specs(shift, m, tm, k, rows_per_batch)
    sc, sc_spec = _mod_specs(scale, m, tm, k, rows_per_batch)
    in_specs = [
        pl.BlockSpec((tm, k), lambda i, j: (i, 0)),
        pl.BlockSpec((1, k), lambda i, j: (0, 0)),
        sh_spec,
        sc_spec,
        pl.BlockSpec((k, tn), lambda i, j: (0, j)),
    ]
    args = [x, g.reshape(1, k), sh, sc, w]
    out_specs = pl.BlockSpec((tm, tn), lambda i, j: (i, j))
    out_shape = jax.ShapeDtypeStruct((m, n_out), f32)
    kern = _modmm_kernel
    if w_gate is not None:
        in_specs.append(pl.BlockSpec((k, LANES), lambda i, j: (0, 0)))
        args.append(w_gate)
        out_specs = [out_specs, pl.BlockSpec((tm, LANES), lambda i, j: (i, 0))]
        out_shape = [out_shape, jax.ShapeDtypeStruct((m, LANES), f32)]
        kern = _modmm_gate_kernel
    return pl.pallas_call(
        kern,
        grid=(m // tm, n_out // tn),
        in_specs=in_specs,
        out_specs=out_specs,
        out_shape=out_shape,
        scratch_shapes=[pltpu.VMEM((tm, k), bf16)],
        compiler_params=_cparams("arbitrary", "arbitrary"),
        name=name,
    )(*args)


def _outproj_kernel(a1_ref, a2_ref, w1_ref, w2_ref, x_ref, gt_ref, o_ref):
    y = _dot(a1_ref[...].astype(bf16), w1_ref[...]) + _dot(a2_ref[...].astype(bf16), w2_ref[...])
    o_ref[...] = x_ref[...] + gt_ref[0] * y


def out_proj(a1, a2, w, x, gate, rows_per_batch, tm, tn, name="outproj"):
    m, k1 = a1.shape
    k2 = a2.shape[1]
    n = w.shape[1]
    gt, gt_spec = _mod_specs(gate, m, tm, n, rows_per_batch)
    if gt.shape[1] == 1:
        per = rows_per_batch // tm
        gt_spec = pl.BlockSpec((1, 1, tn), lambda i, j: (i // per, 0, j))
    else:
        gt_spec = pl.BlockSpec((1, tm, tn), lambda i, j: (0, i, j))
    return pl.pallas_call(
        _outproj_kernel,
        grid=(m // tm, n // tn),
        in_specs=[
            pl.BlockSpec((tm, k1), lambda i, j: (i, 0)),
            pl.BlockSpec((tm, k2), lambda i, j: (i, 0)),
            pl.BlockSpec((k1, tn), lambda i, j: (0, j)),
            pl.BlockSpec((k2, tn), lambda i, j: (k1 // k2, j)),
            pl.BlockSpec((tm, tn), lambda i, j: (i, j)),
            gt_spec,
        ],
        out_specs=pl.BlockSpec((tm, tn), lambda i, j: (i, j)),
        out_shape=jax.ShapeDtypeStruct((m, n), f32),
        compiler_params=_cparams("arbitrary", "arbitrary"),
        name=name,
    )(a1, a2, w, w, x, gt)


def _rglru_gates(xc, wr_ref, br, wi_ref, bi, lam):
    xb = xc.astype(bf16)
    rs, gs = [], []
    for hb in range(RG_BLOCKS):
        sl = slice(hb * LANES, (hb + 1) * LANES)
        rs.append(_dot(xb[:, sl], wr_ref[hb]))
        gs.append(_dot(xb[:, sl], wi_ref[hb]))
    r = _sigmoid(jnp.concatenate(rs, axis=1) + br)
    ig = _sigmoid(jnp.concatenate(gs, axis=1) + bi)
    softplus_neg_lam = jnp.maximum(-lam, 0.0) + jnp.log1p(jnp.exp(-jnp.abs(lam)))
    log_a = -RG_C * r * softplus_neg_lam
    a = jnp.exp(log_a)
    u = jnp.sqrt(-jnp.tanh(log_a) * (a * a + 1.0)) * ig * xc
    return a, u


def _rglru_kernel(xa_ref, ga_ref, cw_ref, cb_ref, wr_ref, br_ref, wi_ref, bi_ref, lam_ref,
                  ya_ref, hl_ref, xprev_ref, hc_ref):
    t_idx = pl.program_id(1)
    tt = xa_ref.shape[0]

    @pl.when(t_idx == 0)
    def _():
        xprev_ref[...] = jnp.zeros_like(xprev_ref)
        hc_ref[...] = jnp.zeros_like(hc_ref)

    xa = xa_ref[...]
    xprev = xprev_ref[...]
    row8 = _iota((8, RG_WIDTH), 0)
    xc = cb_ref[...] + cw_ref[CONV_W - 1:CONV_W, :] * xa
    for k in range(1, CONV_W):
        rolled = pltpu.roll(xa, k, 0)
        head = jnp.where(row8 < k, pltpu.roll(xprev, k, 0), rolled[0:8])
        shifted = jnp.concatenate([head, rolled[8:]], axis=0)
        xc = xc + cw_ref[CONV_W - 1 - k:CONV_W - k, :] * shifted
    xprev_ref[...] = xa[tt - 8:tt]

    a, u = _rglru_gates(xc, wr_ref, br_ref[...], wi_ref, bi_ref[...], lam_ref[...])
    row = _iota((tt, RG_WIDTH), 0)
    s = 1
    while s < tt:
        a_sh = pltpu.roll(a, s, 0)
        u_sh = pltpu.roll(u, s, 0)
        ok = row >= s
        u = jnp.where(ok, a * u_sh + u, u)
        a = jnp.where(ok, a * a_sh, a)
        s *= 2
    h = a * hc_ref[...] + u
    hc_ref[...] = h[tt - 1:tt]
    hl_ref[0] = h[tt - 1:tt]
    ya_ref[...] = h * _gelu(ga_ref[...])


def rglru_prompt(proj, bsz, seq, cw, cb, wr, br, wi, bi, lam, tt=256):
    nt = seq // tt
    vec = lambda: pl.BlockSpec((1, RG_WIDTH), lambda b, t: (0, 0))
    return pl.pallas_call(
        _rglru_kernel,
        grid=(bsz, nt),
        in_specs=[
            pl.BlockSpec((tt, RG_WIDTH), lambda b, t: (b * nt + t, 0)),
            pl.BlockSpec((tt, RG_WIDTH), lambda b, t: (b * nt + t, 1)),
            pl.BlockSpec((CONV_W, RG_WIDTH), lambda b, t: (0, 0)),
            vec(),
            pl.BlockSpec((RG_BLOCKS, LANES, LANES), lambda b, t: (0, 0, 0)),
            vec(),
            pl.BlockSpec((RG_BLOCKS, LANES, LANES), lambda b, t: (0, 0, 0)),
            vec(),
            vec(),
        ],
        out_specs=[
            pl.BlockSpec((tt, RG_WIDTH), lambda b, t: (b * nt + t, 0)),
            pl.BlockSpec((1, 1, RG_WIDTH), lambda b, t: (b, 0, 0)),
        ],
        out_shape=[
            jax.ShapeDtypeStruct((bsz * seq, RG_WIDTH), f32),
            jax.ShapeDtypeStruct((bsz, 1, RG_WIDTH), f32),
        ],
        scratch_shapes=[pltpu.VMEM((8, RG_WIDTH), f32), pltpu.VMEM((1, RG_WIDTH), f32)],
        compiler_params=_cparams("arbitrary", "arbitrary"),
        name="rglru_prompt",
    )(proj, proj, cw, cb, wr, br, wi, bi, lam)


def _rglru_step_kernel(xa_ref, ga_ref, b0_ref, b1_ref, b2_ref, h0_ref, cw_ref, cb_ref,
                       wr_ref, br_ref, wi_ref, bi_ref, lam_ref, ya_ref, h_ref):
    xa = xa_ref[...]
    xc = (cb_ref[...] + cw_ref[0:1, :] * b0_ref[...] + cw_ref[1:2, :] * b1_ref[...]
          + cw_ref[2:3, :] * b2_ref[...] + cw_ref[3:4, :] * xa)
    a, u = _rglru_gates(xc, wr_ref, br_ref[...], wi_ref, bi_ref[...], lam_ref[...])
    h = a * h0_ref[...] + u
    h_ref[...] = h
    ya_ref[...] = h * _gelu(ga_ref[...])


def rglru_step(proj_s, conv_state, h0, cw, cb, wr, br, wi, bi, lam):
    n = proj_s.shape[0]
    full = lambda shape: pl.BlockSpec(shape, lambda i: tuple(0 for _ in shape))
    return pl.pallas_call(
        _rglru_step_kernel,
        grid=(1,),
        in_specs=[
            pl.BlockSpec((n, RG_WIDTH), lambda i: (0, 0)),
            pl.BlockSpec((n, RG_WIDTH), lambda i: (0, 1)),
            full((n, RG_WIDTH)), full((n, RG_WIDTH)), full((n, RG_WIDTH)), full((n, RG_WIDTH)),
            full((CONV_W, RG_WIDTH)), full((1, RG_WIDTH)),
            full((RG_BLOCKS, LANES, LANES)), full((1, RG_WIDTH)),
            full((RG_BLOCKS, LANES, LANES)), full((1, RG_WIDTH)), full((1, RG_WIDTH)),
        ],
        out_specs=[full((n, RG_WIDTH)), full((n, RG_WIDTH))],
        out_shape=[jax.ShapeDtypeStruct((n, RG_WIDTH), f32)] * 2,
        compiler_params=_cparams("arbitrary"),
        name="rglru_step",
    )(proj_s, proj_s, conv_state[:, 0], conv_state[:, 1], conv_state[:, 2], h0,
      cw, cb, wr, br, wi, bi, lam)


def _rope_tables(pos):
    half = ROT_DIMS // 2
    inv = ROPE_THETA ** (-jnp.arange(half, dtype=f32) / half)
    ang = pos.astype(f32)[:, None] * inv[None, :]
    cos, sin = jnp.cos(ang), jnp.sin(ang)
    n = pos.shape[0]
    ctab = jnp.concatenate([cos, cos, jnp.ones((n, DH - ROT_DIMS), f32)], axis=1)
    stab = jnp.concatenate([-sin, sin, jnp.zeros((n, DH - ROT_DIMS), f32)], axis=1)
    return ctab, stab


def _rope_head(xh, ctab, stab, lane):
    half = ROT_DIMS // 2
    partner = jnp.where(lane < half, pltpu.roll(xh, DH - half, 1), pltpu.roll(xh, half, 1))
    return xh * ctab + partner * stab


def _rope_kernel(q_ref, k_ref, v_ref, c_ref, s_ref, qo_ref, ko_ref, vo_ref):
    ctab, stab = c_ref[...], s_ref[...]
    lane = _iota((q_ref.shape[0], DH), 1)
    for hb in range(HEADS):
        sl = slice(hb * DH, (hb + 1) * DH)
        qo_ref[:, hb, :] = _rope_head(q_ref[:, sl], ctab, stab, lane) * (DH ** -0.5)
        ko_ref[:, hb, :] = _rope_head(k_ref[:, sl], ctab, stab, lane)
        vo_ref[:, hb, :] = v_ref[:, sl]


def rope_qkv(proj, ctab, stab, rows_per_seq, tt):
    m = proj.shape[0]
    nt = max(rows_per_seq // tt, 1)
    if ctab.shape[0] == 1:
        tab_spec = pl.BlockSpec((1, DH), lambda i: (0, 0))
    else:
        tab_spec = pl.BlockSpec((tt, DH), lambda i: (i % nt, 0))
    return pl.pallas_call(
        _rope_kernel,
        grid=(m // tt,),
        in_specs=[
            pl.BlockSpec((tt, ATT_W), lambda i: (i, 2)),
            pl.BlockSpec((tt, ATT_W), lambda i: (i, 3)),
            pl.BlockSpec((tt, ATT_W), lambda i: (i, 4)),
            tab_spec, tab_spec,
        ],
        out_specs=[pl.BlockSpec((tt, HEADS, DH), lambda i: (i, 0, 0))] * 3,
        out_shape=[jax.ShapeDtypeStruct((m, HEADS, DH), f32)] * 3,
        compiler_params=_cparams("arbitrary"),
        name="rope",
    )(proj, proj, proj, ctab, stab)


def _dil_kernel(*refs, span, first, last):
    if first:
        q_ref, kc_ref, kp_ref, vc_ref, vp_ref = refs[:5]
        outs = refs[5:]
    else:
        q_ref, kc_ref, kp_ref, vc_ref, vp_ref, op_ref, lp_ref = refs[:7]
        outs = refs[7:]
    o_ref = outs[0]
    tq = q_ref.shape[1]
    blk = pl.program_id(2)
    qi = _iota((tq, tq), 0)
    ki = _iota((tq, tq), 1)
    rel_c = qi - ki
    ok_c = (rel_c >= 0) & (rel_c <= span)
    rel_p = rel_c + tq
    ok_p = (rel_p <= span) & (blk > 0)
    for hb in range(HEADS):
        qh = q_ref[0, :, 0, hb, :].astype(bf16)
        s_c = jnp.where(ok_c, _dot_nt(qh, kc_ref[0, :, 0, hb, :].astype(bf16)), NEG_INF)
        s_p = jnp.where(ok_p, _dot_nt(qh, kp_ref[0, :, 0, hb, :].astype(bf16)), NEG_INF)
        mx = jnp.maximum(jnp.max(s_c, axis=1, keepdims=True), jnp.max(s_p, axis=1, keepdims=True))
        p_c = jnp.exp(s_c - mx)
        p_p = jnp.exp(s_p - mx)
        den = jnp.sum(p_c, axis=1, keepdims=True) + jnp.sum(p_p, axis=1, keepdims=True)
        num = (_dot(p_c.astype(bf16), vc_ref[0, :, 0, hb, :].astype(bf16))
               + _dot(p_p.astype(bf16), vp_ref[0, :, 0, hb, :].astype(bf16)))
        o_g = num / den
        lse_g = jnp.broadcast_to(mx + jnp.log(den), (tq, DH))
        if first:
            o_new, lse_new = o_g, lse_g
        else:
            lse_prev = lp_ref[0, :, 0, hb, :]
            top = jnp.maximum(lse_prev, lse_g)
            e_prev = jnp.exp(lse_prev - top)
            e_g = jnp.exp(lse_g - top)
            tot = e_prev + e_g
            o_new = (op_ref[0, :, 0, hb, :] * e_prev + o_g * e_g) / tot
            lse_new = top + jnp.log(tot)
        o_ref[0, :, 0, hb, :] = o_new
        if not last:
            outs[1][0, :, 0, hb, :] = lse_new


def dilated_prompt(q4, k4, v4, bsz, seq, tq=128):
    o = lse = None
    n_pat = len(DIL_PATTERNS)
    for gi, (w, d) in enumerate(DIL_PATTERNS):
        first, last = gi == 0, gi == n_pat - 1
        sd = seq // d
        nq = sd // tq
        span = w // d
        view = lambda a: a.reshape(bsz, sd, d, HEADS, DH)
        cur = lambda b, r, i: (b, i, r, 0, 0)
        prev = lambda b, r, i: (b, jnp.maximum(i - 1, 0), r, 0, 0)
        blk = (1, tq, 1, HEADS, DH)
        in_specs = [pl.BlockSpec(blk, cur), pl.BlockSpec(blk, cur), pl.BlockSpec(blk, prev),
                    pl.BlockSpec(blk, cur), pl.BlockSpec(blk, prev)]
        args = [view(q4), view(k4), view(k4), view(v4), view(v4)]
        if not first:
            in_specs += [pl.BlockSpec(blk, cur), pl.BlockSpec(blk, cur)]
            args += [view(o), view(lse)]
        out_specs = [pl.BlockSpec(blk, cur)]
        out_shape = [jax.ShapeDtypeStruct((bsz, sd, d, HEADS, DH), f32)]
        if not last:
            out_specs.append(pl.BlockSpec(blk, cur))
            out_shape.append(jax.ShapeDtypeStruct((bsz, sd, d, HEADS, DH), f32))
        res = pl.pallas_call(
            functools.partial(_dil_kernel, span=span, first=first, last=last),
            grid=(bsz, d, nq),
            in_specs=in_specs,
            out_specs=out_specs,
            out_shape=out_shape,
            compiler_params=_cparams("arbitrary", "arbitrary", "arbitrary"),
            name=f"dilattn_d{d}",
        )(*args)
        o = res[0]
        if not last:
            lse = res[1]
    return o.reshape(bsz * seq, ATT_W)


def _dil_step_kernel(q_ref, kn_ref, vn_ref, k1_ref, k4_ref, k16_ref, v1_ref, v4_ref, v16_ref, o_ref):
    q = q_ref[0]
    kn, vn = kn_ref[0], vn_ref[0]
    s_self = _rowsum_bcast(q * kn)
    o_gs, lse_gs = [], []
    for k_ref, v_ref in ((k1_ref, v1_ref), (k4_ref, v4_ref), (k16_ref, v16_ref)):
        k3 = k_ref[0, :, 0]
        nk = k3.shape[0]
        s = _rowsum_bcast((k3 * q[None]).reshape(nk * HEADS, DH)).reshape(nk, HEADS, DH)
        mx = jnp.maximum(jnp.max(s, axis=0), s_self)
        p = jnp.exp(s - mx[None])
        p_self = jnp.exp(s_self - mx)
        den = jnp.sum(p, axis=0) + p_self
        num = jnp.sum(p * v_ref[0, :, 0], axis=0) + p_self * vn
        o_gs.append(num / den)
        lse_gs.append(mx + jnp.log(den))
    top = jnp.maximum(jnp.maximum(lse_gs[0], lse_gs[1]), lse_gs[2])
    es = [jnp.exp(l - top) for l in lse_gs]
    tot = es[0] + es[1] + es[2]
    o_ref[0] = (o_gs[0] * es[0] + o_gs[1] * es[1] + o_gs[2] * es[2]) / tot


def dilated_step(q4, k4, v4, cache_k, cache_v):
    n, win = cache_k.shape[0], cache_k.shape[1]
    one = pl.BlockSpec((1, HEADS, DH), lambda b: (b, 0, 0))
    args = [q4, k4, v4]
    in_specs = [one, one, one]
    for cache in (cache_k, cache_v):
        for (w, d) in DIL_PATTERNS:
            nkeys = w // d
            assert win % d == 0 and (win // d) % nkeys == 0 and (win - w) % (d * nkeys) == 0
            args.append(cache.reshape(n, win // d, d, HEADS, DH))
            in_specs.append(pl.BlockSpec((1, nkeys, 1, HEADS, DH),
                                         lambda b, blk=(win - w) // d // nkeys: (b, blk, 0, 0, 0)))
    return pl.pallas_call(
        _dil_step_kernel,
        grid=(n,),
        in_specs=in_specs,
        out_specs=one,
        out_shape=jax.ShapeDtypeStruct((n, HEADS, DH), f32),
        compiler_params=_cparams("arbitrary"),
        name="dilattn_step",
    )(*args).reshape(n, ATT_W)


def _sb_kernel(bias_ref, q_ref, k_ref, v_ref, o_ref, qs_ref, acc_ref, carry_ref):
    tq = q_ref.shape[1]
    nh = q_ref.shape[2] // DH
    sub = PAGE
    hg = pl.program_id(1)
    qb = pl.program_id(2)
    rr = _iota((sub, 2 * sub), 0)
    cc = _iota((sub, 2 * sub), 1)
    tri = jnp.where((rr > cc) | (cc >= sub), 1.0, 0.0).astype(bf16)
    for h in range(nh):
        qs_ref[h] = (q_ref[0, :, h * DH:(h + 1) * DH] * (DH ** -0.5)).astype(bf16)
    acc_ref[...] = jnp.zeros_like(acc_ref)
    carry_ref[...] = jnp.zeros_like(carry_ref)
    nd = tq // sub

    def tile(start, q_minus_k0):
        for h in range(nh):
            hs = slice(h * DH, (h + 1) * DH)
            kk = k_ref[0, pl.ds(start, sub), hs].astype(bf16)
            vv = v_ref[0, pl.ds(start, sub), hs].astype(bf16)
            z = _dot_nt(qs_ref[h], kk) + bias_ref[hg * nh + h]
            sp = _softplus(z)
            if q_minus_k0 is not None:
                ok = _iota((tq, sub), 1) < _iota((tq, sub), 0) + q_minus_k0
                sp = jnp.where(ok, sp, 0.0)
            cs = _dot(sp.astype(bf16), tri)
            carry = carry_ref[h]
            w = jnp.exp(z - sp - cs[:, :sub] - carry)
            if q_minus_k0 is not None:
                w = jnp.where(ok, w, 0.0)
            acc_ref[h] += _dot(w.astype(bf16), vv)
            carry_ref[h] = carry + cs[:, sub:]

    for dd in range(nd - 1, -1, -1):
        tile(pl.multiple_of(qb * tq + dd * sub, sub), -dd * sub)

    def body(it, _):
        sb = qb * nd - 1 - it
        tile(pl.multiple_of(sb * sub, sub), None)
        return 0

    lax.fori_loop(0, qb * nd, body, 0)
    for h in range(nh):
        o_ref[0, :, h * DH:(h + 1) * DH] = acc_ref[h]


def sb_prompt(proj2, bias, bsz, seq, tq=256, nh=4):
    p3 = proj2.reshape(bsz, seq, proj2.shape[1])
    nq = seq // tq
    ng = HEADS // nh
    wd = nh * DH
    return pl.pallas_call(
        _sb_kernel,
        grid=(bsz, ng, nq),
        in_specs=[
            pl.BlockSpec(memory_space=pltpu.SMEM),
            pl.BlockSpec((1, tq, wd), lambda b, g, i: (b, i, g)),
            pl.BlockSpec((1, seq, wd), lambda b, g, i: (b, 0, ng + g)),
            pl.BlockSpec((1, seq, wd), lambda b, g, i: (b, 0, 2 * ng + g)),
        ],
        out_specs=pl.BlockSpec((1, tq, wd), lambda b, g, i: (b, i, g)),
        out_shape=jax.ShapeDtypeStruct((bsz, seq, ATT_W), f32),
        scratch_shapes=[pltpu.VMEM((nh, tq, DH), bf16), pltpu.VMEM((nh, tq, DH), f32),
                        pltpu.VMEM((nh, tq, PAGE), f32)],
        compiler_params=_cparams("arbitrary", "arbitrary", "arbitrary"),
        name="sb_prompt",
    )(bias, p3, p3, p3).reshape(bsz * seq, ATT_W)


def _sb_step_kernel(pt_ref, q_ref, bias_ref, *refs, npg):
    k_refs, v_refs = refs[:npg], refs[npg:2 * npg]
    o_ref, acc_ref, carry_ref = refs[2 * npg:]
    j = pl.program_id(1)

    @pl.when(j == 0)
    def _():
        acc_ref[...] = jnp.zeros_like(acc_ref)
        carry_ref[...] = jnp.zeros_like(carry_ref)

    q = q_ref[0]
    bias = bias_ref[...]
    acc = acc_ref[...]
    run = carry_ref[...]
    for p in range(npg):
        k3 = k_refs[p][0]
        z = _rowsum_bcast((k3 * q[None]).reshape(PAGE * HEADS, DH)).reshape(PAGE, HEADS, DH) + bias[None]
        sp = _softplus(z)
        later = []
        for key in range(PAGE - 1, -1, -1):
            later.append(run)
            run = run + sp[key]
        suffix = jnp.stack(later[::-1], axis=0)
        w = jnp.exp(z - sp - suffix)
        acc = acc + jnp.sum(w * v_refs[p][0], axis=0)
    acc_ref[...] = acc
    carry_ref[...] = run

    @pl.when(j == pl.num_programs(1) - 1)
    def _():
        o_ref[0] = acc


def sb_step(q_s, bias, cache_k, cache_v, page_table, npg=4):
    n, n_pages = page_table.shape
    q4 = (q_s * (DH ** -0.5)).reshape(n, HEADS, DH)
    bias4 = jnp.broadcast_to(bias[:, None], (HEADS, DH))

    def page(p):
        return lambda b, j, pt: (pt[b, n_pages - 1 - (j * npg + p)], 0, 0, 0)

    kv_specs = [pl.BlockSpec((1, PAGE, HEADS, DH), page(p)) for p in range(npg)]
    grid_spec = pltpu.PrefetchScalarGridSpec(
        num_scalar_prefetch=1,
        grid=(n, n_pages // npg),
        in_specs=[
            pl.BlockSpec((1, HEADS, DH), lambda b, j, pt: (b, 0, 0)),
            pl.BlockSpec((HEADS, DH), lambda b, j, pt: (0, 0)),
        ] + kv_specs + kv_specs,
        out_specs=pl.BlockSpec((1, HEADS, DH), lambda b, j, pt: (b, 0, 0)),
        scratch_shapes=[pltpu.VMEM((HEADS, DH), f32), pltpu.VMEM((HEADS, DH), f32)],
    )
    return pl.pallas_call(
        functools.partial(_sb_step_kernel, npg=npg),
        grid_spec=grid_spec,
        out_shape=jax.ShapeDtypeStruct((n, HEADS, DH), f32),
        compiler_params=_cparams("arbitrary", "arbitrary"),
        name="sb_step",
    )(page_table, q4, bias4, *([cache_k] * npg), *([cache_v] * npg)).reshape(n, ATT_W)


def _mlstm_kernel(q_ref, k_ref, v_ref, og_ref, gate_ref, gb_ref, gn_ref,
                  h_ref, c_out, n_out, m_out, c_s, n_s, m_s):
    ci = pl.program_id(1)
    L = CHUNK

    @pl.when(ci == 0)
    def _():
        c_s[...] = jnp.zeros_like(c_s)
        n_s[...] = jnp.zeros_like(n_s)
        m_s[...] = jnp.zeros_like(m_s)

    gt = gate_ref[...] + gb_ref[...]
    gt_t = gt.T
    ri = _iota((L, L), 0)
    li = _iota((L, L), 1)
    causal = li <= ri
    tri_incl = jnp.where(causal, 1.0, 0.0).astype(bf16)
    tri_incl_t = jnp.where(ri <= li, 1.0, 0.0).astype(bf16)
    for h in range(M_HEADS):
        sl = slice(h * M_DH, (h + 1) * M_DH)
        ig_col = gt[:, h:h + 1]
        ig_row = gt_t[h:h + 1, :]
        lf_col = _log_sigmoid_pair(gt[:, M_HEADS + h:M_HEADS + h + 1])[0]
        lf_row = _log_sigmoid_pair(gt_t[M_HEADS + h:M_HEADS + h + 1, :])[0]
        c_hi, c_mid, c_lo = _split3(jnp.broadcast_to(lf_col, (L, L)))
        bcum_col = _dot(tri_incl, c_hi) + _dot(tri_incl, c_mid) + _dot(tri_incl, c_lo)
        r_hi, r_mid, r_lo = _split3(jnp.broadcast_to(lf_row, (L, L)))
        bcum_row = _dot(r_hi, tri_incl_t) + _dot(r_mid, tri_incl_t) + _dot(r_lo, tri_incl_t)
        m_prev = m_s[h:h + 1, :]
        dlog = jnp.where(causal, bcum_col - bcum_row + ig_row, NEG_INF)
        inter = bcum_col + m_prev
        m_t = jnp.maximum(inter, jnp.max(dlog, axis=1, keepdims=True))
        dw = jnp.exp(dlog - m_t)
        iw = jnp.exp(inter - m_t)
        qh = q_ref[:, sl]
        kh = k_ref[:, sl] * (M_DH ** -0.5)
        vh = v_ref[:, sl]
        qb, kb, vb = qh.astype(bf16), kh.astype(bf16), vh.astype(bf16)
        sw = dw * _dot_nt(qb, kb)
        c_prev = c_s[h]
        n_prev = n_s[h:h + 1, :]
        iw_col = iw[:, 0:1]
        num = _dot(sw.astype(bf16), vb) + iw_col * _dot_nt(qb, c_prev.astype(bf16))
        qn = jnp.sum(qb.astype(f32) * n_prev.astype(bf16).astype(f32), axis=1, keepdims=True)
        den = jnp.sum(sw, axis=1, keepdims=True) + iw_col * qn
        m_col = m_t[:, 0:1]
        hout = num / jnp.maximum(jnp.abs(den), jnp.exp(-m_col))
        m_last = m_t[L - 1:L, :]
        b_last = bcum_col[L - 1:L, :]
        wl_col = jnp.exp(b_last[:, 0:1] - bcum_col[:, 0:1] + ig_col - m_last[:, 0:1])
        wl_row = jnp.exp(b_last - bcum_row[0:1, :] + ig_row - m_last)
        decay = jnp.exp(b_last + m_prev - m_last)
        dsc = decay[:, 0:1]
        c_s[h] = dsc * c_prev + _dot((vh * wl_col).T.astype(bf16), kb)
        wl8 = jnp.broadcast_to(wl_row, (8, L)).astype(bf16)
        n_s[h:h + 1, :] = dsc * n_prev + _dot(wl8, kb)[0:1, :]
        m_s[h:h + 1, :] = m_last
        hn = hout * lax.rsqrt(jnp.mean(hout * hout, axis=1, keepdims=True) + EPS)
        h_ref[:, sl] = hn * gn_ref[:, sl] * _sigmoid(og_ref[:, sl])

    @pl.when(ci == pl.num_programs(1) - 1)
    def _():
        c_out[0] = c_s[...]
        n_out[0] = n_s[...]
        m_out[0] = m_s[...]


def mlstm_prompt(proj2, gates, gate_bias, g_mnorm, bsz, seq):
    nc = seq // CHUNK
    col = lambda c: pl.BlockSpec((CHUNK, M_WIDTH), lambda b, i, c=c: (b * nc + i, c))
    hm, c1, n1, m1 = pl.pallas_call(
        _mlstm_kernel,
        grid=(bsz, nc),
        in_specs=[
            col(3), col(4), col(5), col(6),
            pl.BlockSpec((CHUNK, LANES), lambda b, i: (b * nc + i, 0)),
            pl.BlockSpec((1, LANES), lambda b, i: (0, 0)),
            pl.BlockSpec((1, M_WIDTH), lambda b, i: (0, 0)),
        ],
        out_specs=[
            pl.BlockSpec((CHUNK, M_WIDTH), lambda b, i: (b * nc + i, 0)),
            pl.BlockSpec((1, M_HEADS, M_DH, M_DH), lambda b, i: (b, 0, 0, 0)),
            pl.BlockSpec((1, M_HEADS, M_DH), lambda b, i: (b, 0, 0)),
            pl.BlockSpec((1, M_HEADS, LANES), lambda b, i: (b, 0, 0)),
        ],
        out_shape=[
            jax.ShapeDtypeStruct((bsz * seq, M_WIDTH), f32),
            jax.ShapeDtypeStruct((bsz, M_HEADS, M_DH, M_DH), f32),
            jax.ShapeDtypeStruct((bsz, M_HEADS, M_DH), f32),
            jax.ShapeDtypeStruct((bsz, M_HEADS, LANES), f32),
        ],
        scratch_shapes=[
            pltpu.VMEM((M_HEADS, M_DH, M_DH), f32),
            pltpu.VMEM((M_HEADS, M_DH), f32),
            pltpu.VMEM((M_HEADS, LANES), f32),
        ],
        compiler_params=_cparams("arbitrary", "arbitrary"),
        name="mlstm_prompt",
    )(proj2, proj2, proj2, proj2, gates, gate_bias, g_mnorm)
    return hm, c1, n1, m1[:, :, 0]


def _mlstm_step_kernel(q_ref, k_ref, v_ref, og_ref, gn_ref, ig_ref, fg_ref, c_ref, n_ref, m_ref,
                       h_ref, c_out, n_out, m_out):
    q = q_ref[0, 0]
    k = k_ref[0, 0] * (M_DH ** -0.5)
    v = v_ref[0, 0]
    ig = ig_ref[0, 0]
    lf = _log_sigmoid_pair(fg_ref[0, 0])[0]
    m0 = m_ref[0, 0]
    c0 = c_ref[0, 0]
    n0 = n_ref[0, 0]
    inter = lf + m0
    m_t = jnp.maximum(inter, ig)
    dw = jnp.exp(ig - m_t)
    iw = jnp.exp(inter - m_t)
    rnd = lambda a: a.astype(bf16).astype(f32)
    qr, kr, vr = rnd(q), rnd(k), rnd(v)
    qk = jnp.sum(qr * kr, axis=1, keepdims=True)
    sw = dw * qk
    cq = jnp.sum(rnd(c0) * qr, axis=1, keepdims=True)
    num = rnd(sw) * vr + iw * cq
    den = sw + iw * jnp.sum(rnd(n0) * qr, axis=1, keepdims=True)
    hout = num / jnp.maximum(jnp.abs(den), jnp.exp(-m_t))
    c_out[0, 0] = iw * c0 + rnd(dw * v) * kr
    n_out[0, 0] = iw * n0 + rnd(dw) * kr
    m_out[0, 0] = m_t
    hn = hout * lax.rsqrt(jnp.mean(hout * hout, axis=0, keepdims=True) + EPS)
    h_ref[0, 0] = hn * gn_ref[0] * _sigmoid(og_ref[0, 0])


def mlstm_step(proj2_s, gates_s, gate_bias, g_mnorm, c0, n0, m0):
    n = proj2_s.shape[0]
    base = 3 * ATT_W
    seg = lambda i: proj2_s[:, base + i * M_WIDTH: base + (i + 1) * M_WIDTH].reshape(n, M_HEADS, M_DH)
    g = gates_s + gate_bias
    rowb = pl.BlockSpec((1, 1, 1, M_DH), lambda b, h: (b, h, 0, 0))
    colb = pl.BlockSpec((1, 1, M_DH, 1), lambda b, h: (b, h, 0, 0))
    scal = pl.BlockSpec((1, 1, 1, 1), lambda b, h: (b, h, 0, 0))
    hcol, c1, n1, m1 = pl.pallas_call(
        _mlstm_step_kernel,
        grid=(n, M_HEADS),
        in_specs=[
            rowb, rowb, colb, colb,
            pl.BlockSpec((1, M_DH, 1), lambda b, h: (h, 0, 0)),
            scal, scal,
            pl.BlockSpec((1, 1, M_DH, M_DH), lambda b, h: (b, h, 0, 0)),
            rowb, scal,
        ],
        out_specs=[colb, pl.BlockSpec((1, 1, M_DH, M_DH), lambda b, h: (b, h, 0, 0)), rowb, scal],
        out_shape=[
            jax.ShapeDtypeStruct((n, M_HEADS, M_DH, 1), f32),
            jax.ShapeDtypeStruct((n, M_HEADS, M_DH, M_DH), f32),
            jax.ShapeDtypeStruct((n, M_HEADS, 1, M_DH), f32),
            jax.ShapeDtypeStruct((n, M_HEADS, 1, 1), f32),
        ],
        compiler_params=_cparams("arbitrary", "arbitrary"),
        name="mlstm_step",
    )(seg(0)[:, :, None, :], seg(1)[:, :, None, :], seg(2)[..., None], seg(3)[..., None],
      g_mnorm.reshape(M_HEADS, M_DH, 1),
      g[:, 0:M_HEADS].reshape(n, M_HEADS, 1, 1), g[:, M_HEADS:2 * M_HEADS].reshape(n, M_HEADS, 1, 1),
      c0, n0[:, :, None, :], m0.reshape(n, M_HEADS, 1, 1))
    return hcol.reshape(n, M_WIDTH), c1, n1.reshape(n, M_HEADS, M_DH), m1.reshape(n, M_HEADS)


def _peerq_kernel(x_ref, g_ref, sh_ref, sc_ref, wt_ref, qt_ref, xmt_ref, xs_ref):
    @pl.when(pl.program_id(1) == 0)
    def _():
        xm = _modulated(x_ref[...], g_ref[...], sh_ref[0], sc_ref[0])
        xt = xm.T.astype(bf16)
        xs_ref[...] = xt
        xmt_ref[...] = xt

    qt_ref[...] = _dot(wt_ref[...], xs_ref[...])


def peer_query(x, g, shift, scale, wq_t, rows_per_batch, tm, tn=512):
    m, k = x.shape
    n = wq_t.shape[0]
    sh, sh_spec = _mod_specs(shift, m, tm, k, rows_per_batch)
    sc, sc_spec = _mod_specs(scale, m, tm, k, rows_per_batch)
    return pl.pallas_call(
        _peerq_kernel,
        grid=(m // tm, n // tn),
        in_specs=[
            pl.BlockSpec((tm, k), lambda i, j: (i, 0)),
            pl.BlockSpec((1, k), lambda i, j: (0, 0)),
            sh_spec, sc_spec,
            pl.BlockSpec((tn, k), lambda i, j: (j, 0)),
        ],
        out_specs=[pl.BlockSpec((tn, tm), lambda i, j: (j, i)),
                   pl.BlockSpec((k, tm), lambda i, j: (0, i))],
        out_shape=[jax.ShapeDtypeStruct((n, m), f32), jax.ShapeDtypeStruct((k, m), bf16)],
        scratch_shapes=[pltpu.VMEM((k, tm), bf16)],
        compiler_params=_cparams("arbitrary", "arbitrary"),
        name="peer_query",
    )(x, g.reshape(1, k), sh, sc, wq_t)


_N_TOP = PEER_TOPK + 1
_CAND_PAIRS = [(a, b) for a in range(_N_TOP) for b in range(_N_TOP) if (a + 1) * (b + 1) <= _N_TOP]
_N_CAND = -(-len(_CAND_PAIRS) // 8) * 8


def _extract_top(cur, ridx, n):
    vals = []
    big = cur.shape[0]
    for _ in range(n):
        mx = jnp.max(cur, axis=0, keepdims=True)
        first = jnp.min(jnp.where(cur == mx, ridx, big), axis=0, keepdims=True)
        cur = jnp.where(ridx == first, NEG_INF, cur)
        vals.append(mx)
    return vals


def _router_kernel(qt_ref, keys_ref, e1_ref, e2_ref, th_ref, cand_ref):
    ridx = _iota((N_KEYS, LANES), 0)
    cidx = _iota((_N_CAND, LANES), 0)
    k0 = keys_ref[0].astype(bf16)
    k1 = keys_ref[1].astype(bf16)
    cand_ref[...] = jnp.full(cand_ref.shape, NEG_INF, f32)

    def body(h, _):
        r0 = pl.multiple_of(h * 2 * N_KEYS, 2 * N_KEYS)
        s1 = _dot(k0, qt_ref[pl.ds(r0, N_KEYS), :].astype(bf16))
        s2 = _dot(k1, qt_ref[pl.ds(r0 + N_KEYS, N_KEYS), :].astype(bf16))
        top_a = _extract_top(s1, ridx, _N_TOP)
        top_b = _extract_top(s2, ridx, _N_TOP)
        for ci, (a, b) in enumerate(_CAND_PAIRS):
            cand_ref[ci:ci + 1, :] = top_a[a] + top_b[b]
        cs = _extract_top(cand_ref[...], cidx, _N_TOP)
        z = jnp.ones_like(cs[0])
        for r in range(1, PEER_TOPK):
            z = z + jnp.exp(cs[r] - cs[0])
        inv_z = 1.0 / z
        mid = 0.5 * (cs[PEER_TOPK - 1] + cs[PEER_TOPK])
        o0 = pl.multiple_of(h * N_KEYS, N_KEYS)
        e1 = jnp.exp(s1 - top_a[0]) * inv_z
        e1_ref[:, h] = e1.reshape(N_KEYS // 8, 8, LANES)
        e2_ref[pl.ds(o0, N_KEYS), :] = jnp.exp(s2 - top_b[0])
        th_ref[pl.ds(h, 1), :] = jnp.exp(mid - cs[0]) * inv_z
        return 0

    lax.fori_loop(0, PEER_HEADS, body, 0)


def peer_router(qt, keys):
    n, m = qt.shape
    half = PEER_HEADS * N_KEYS
    return pl.pallas_call(
        _router_kernel,
        grid=(m // LANES,),
        in_specs=[pl.BlockSpec((n, LANES), lambda i: (0, i)),
                  pl.BlockSpec((2, N_KEYS, N_KEYS), lambda i: (0, 0, 0))],
        out_specs=[pl.BlockSpec((N_KEYS // 8, PEER_HEADS, 8, LANES), lambda i: (0, 0, 0, i)),
                   pl.BlockSpec((half, LANES), lambda i: (0, i)),
                   pl.BlockSpec((PEER_HEADS, LANES), lambda i: (0, i))],
        out_shape=[jax.ShapeDtypeStruct((N_KEYS // 8, PEER_HEADS, 8, m), f32),
                   jax.ShapeDtypeStruct((half, m), f32),
                   jax.ShapeDtypeStruct((PEER_HEADS, m), f32)],
        scratch_shapes=[pltpu.VMEM((_N_CAND, LANES), f32)],
        compiler_params=_cparams("arbitrary"),
        name="peer_router",
    )(qt, keys)


def _experts_kernel(xmt_ref, e1_ref, e2_ref, th_ref, u_ref, v_ref, x_ref, gt_ref, gf_ref,
                    o_ref, wt_ref, *, final_norm):
    c = pl.program_id(1)
    te, tm = wt_ref.shape
    ni = te // N_KEYS

    @pl.when(c == 0)
    def _():
        o_ref[...] = jnp.zeros_like(o_ref)

    act = _gelu(_dot(u_ref[...], xmt_ref[...]))
    for ii in range(ni):
        for lc in range(tm // LANES):
            ls = slice(lc * LANES, (lc + 1) * LANES)
            g = jnp.zeros((N_KEYS, LANES), f32)
            for h in range(PEER_HEADS):
                row = e1_ref[ii // 8, h, ii % 8:ii % 8 + 1, ls]
                p = e2_ref[h * N_KEYS:(h + 1) * N_KEYS, ls] * row
                g = g + jnp.where(p >= th_ref[h:h + 1, ls], p, 0.0)
            wt_ref[ii * N_KEYS:(ii + 1) * N_KEYS, ls] = g * act[ii * N_KEYS:(ii + 1) * N_KEYS, ls]
    o_ref[...] += _dot(wt_ref[...].T.astype(bf16), v_ref[...])

    @pl.when(c == pl.num_programs(1) - 1)
    def _():
        y = x_ref[...] + gt_ref[0] * o_ref[...]
        if final_norm:
            ms = jnp.mean(y * y, axis=-1, keepdims=True)
            y = y * lax.rsqrt(ms + EPS) * gf_ref[...]
        o_ref[...] = y


def peer_experts(xmt, e1t, e2t, th, u, v, x, gate, g_final, rows_per_batch, tm, te, final_norm):
    m, d = x.shape
    gt, gt_spec = _mod_specs(gate, m, tm, d, rows_per_batch)
    half = PEER_HEADS * N_KEYS
    return pl.pallas_call(
        functools.partial(_experts_kernel, final_norm=final_norm),
        grid=(m // tm, N_EXPERTS // te),
        in_specs=[
            pl.BlockSpec((d, tm), lambda i, c: (0, i)),
            pl.BlockSpec((te // N_KEYS // 8, PEER_HEADS, 8, tm), lambda i, c: (c, 0, 0, i)),
            pl.BlockSpec((half, tm), lambda i, c: (0, i)),
            pl.BlockSpec((PEER_HEADS, tm), lambda i, c: (0, i)),
            pl.BlockSpec((te, d), lambda i, c: (c, 0)),
            pl.BlockSpec((te, d), lambda i, c: (c, 0)),
            pl.BlockSpec((tm, d), lambda i, c: (i, 0)),
            gt_spec,
            pl.BlockSpec((1, d), lambda i, c: (0, 0)),
        ],
        out_specs=pl.BlockSpec((tm, d), lambda i, c: (i, 0)),
        out_shape=jax.ShapeDtypeStruct((m, d), f32),
        scratch_shapes=[pltpu.VMEM((te, tm), f32)],
        compiler_params=_cparams("arbitrary", "arbitrary"),
        name="peer_experts",
    )(xmt, e1t, e2t, th, u, v, x, gt, g_final.reshape(1, d))


def peer_block(x, g, shift, scale, gate, wq_t, keys, u, v, g_final, rows_per_batch, tm, te, final_norm):
    qt, xmt = peer_query(x, g, shift, scale, wq_t, rows_per_batch, tm)
    e1t, e2t, th = peer_router(qt, keys)
    return peer_experts(xmt, e1t, e2t, th, u, v, x, gate, g_final, rows_per_batch, tm, te, final_norm)


def kernel(x_prompt, x_sample, c_prompt, c_sample, state_rglru_conv, state_rglru_h, cache_swa_k, cache_swa_v, cache_sb_k, cache_sb_v, state_mlstm_C, state_mlstm_n, state_mlstm_m, page_table, w_ada, b_ada, g_norm_mix, g_norm_ffn, e_w_in, e_conv_w, e_conv_b, e_w_r, e_b_r, e_w_i, e_b_i, e_lambda, e_w_out, o_w_in, o_b_if, o_sb_bias, o_g_mnorm, o_w_out, peer_w_q, peer_keys, peer_u, peer_v, g_final):
    bp, seq, d = x_prompt.shape
    bs = x_sample.shape[0]
    mp = bp * seq
    pad_s = LANES
    xp = x_prompt.reshape(mp, d)
    xs = x_sample.reshape(bs, d)

    c_rows = 16
    c_all = jnp.concatenate([c_prompt, c_sample, jnp.zeros((c_rows - bp - bs, d), f32)], axis=0)
    mod = adaln_all(c_all, w_ada, b_ada)

    def mods(layer):
        parts = [mod[layer, :, i * d:(i + 1) * d] for i in range(6)]
        return [p[:bp] for p in parts], [p[bp:bp + bs] for p in parts]

    ctab_p, stab_p = _rope_tables(jnp.arange(seq, dtype=jnp.int32))
    ctab_s, stab_s = _rope_tables(jnp.full((1,), PAST_LEN, jnp.int32))

    TM = 512
    m_p, m_s = mods(0)
    w_in = e_w_in[0].astype(bf16)
    w_out = e_w_out[0].astype(bf16)
    cw, cb = e_conv_w[0], e_conv_b[0].reshape(1, -1)
    wr, wi = e_w_r[0].astype(bf16), e_w_i[0].astype(bf16)
    br, bi, lam = e_b_r[0].reshape(1, -1), e_b_i[0].reshape(1, -1), e_lambda[0].reshape(1, -1)

    proj_p = mod_matmul(xp, g_norm_mix[0], m_p[0], m_p[1], w_in, E_IN, seq, 1024, 512, name="e_in_p")
    proj_s = mod_matmul(xs, g_norm_mix[0], m_s[0], m_s[1], w_in, E_IN, 1, bs, 512, name="e_in_s")

    ya_p, h_p = rglru_prompt(proj_p, bp, seq, cw, cb, wr, br, wi, bi, lam)
    ya_s, h_s = rglru_step(proj_s, state_rglru_conv[0], state_rglru_h[0], cw, cb, wr, br, wi, bi, lam)
    conv_p = proj_p.reshape(bp, seq, E_IN)[:, seq - (CONV_W - 1):, :RG_WIDTH]
    conv_s = jnp.concatenate([state_rglru_conv[0][:, 1:], proj_s[:, None, :RG_WIDTH]], axis=1)

    q_p, k_p, v_p = rope_qkv(proj_p, ctab_p, stab_p, seq, 512)
    q_s, k_s, v_s = rope_qkv(proj_s, ctab_s, stab_s, 1, bs)
    o_p = dilated_prompt(q_p, k_p, v_p, bp, seq)
    win = cache_swa_k.shape[2]
    o_s = dilated_step(q_s, k_s, v_s, cache_swa_k.reshape(-1, win, HEADS, DH)[:bs],
                       cache_swa_v.reshape(-1, win, HEADS, DH)[:bs])
    wl = min(2048, seq)
    swa_k_p = k_p.reshape(bp, seq, HEADS, DH)[:, seq - wl:]
    swa_v_p = v_p.reshape(bp, seq, HEADS, DH)[:, seq - wl:]
    swa_k_s = k_s.reshape(bs, 1, HEADS, DH)
    swa_v_s = v_s.reshape(bs, 1, HEADS, DH)

    xp = out_proj(ya_p, o_p, w_out, xp, m_p[2], seq, 1024, 512, name="e_out_p")
    xs = out_proj(ya_s, o_s, w_out, xs, m_s[2], 1, bs, 512, name="e_out_s")

    def peer_layer(layer, xp, xs, m_p, m_s, final_norm):
        wq_t = peer_w_q[layer].T.astype(bf16)
        u = peer_u[layer].astype(bf16)
        v = peer_v[layer].astype(bf16)
        xp = peer_block(xp, g_norm_ffn[layer], m_p[3], m_p[4], m_p[5], wq_t, peer_keys[layer], u, v,
                        g_final, seq, TM, 1024, final_norm)
        xs_pad = jnp.pad(xs, ((0, pad_s - bs), (0, 0)))
        xs_new = peer_block(xs_pad, g_norm_ffn[layer], m_s[3], m_s[4], m_s[5], wq_t, peer_keys[layer],
                            u, v, g_final, 1, pad_s, 1024, final_norm)
        return xp, xs_new[:bs]

    xp, xs = peer_layer(0, xp, xs, m_p, m_s, False)

    m_p, m_s = mods(1)
    w_in2 = o_w_in[0].astype(bf16)
    w_gate = jnp.pad(w_in2[:, O_MAIN:], ((0, 0), (0, LANES - 2 * M_HEADS)))
    w_out2 = o_w_out[0].astype(bf16)
    gate_bias = jnp.pad(o_b_if[0].reshape(1, 2 * M_HEADS), ((0, 0), (0, LANES - 2 * M_HEADS)))
    gmn = o_g_mnorm[0].reshape(1, M_WIDTH)

    proj2_p, gates_p = mod_matmul(xp, g_norm_mix[1], m_p[0], m_p[1], w_in2, O_MAIN, seq, 1024, 512,
                                  w_gate=w_gate, name="o_in_p")
    proj2_s, gates_s = mod_matmul(xs, g_norm_mix[1], m_s[0], m_s[1], w_in2, O_MAIN, 1, bs, 512,
                                  w_gate=w_gate, name="o_in_s")

    oc_p = sb_prompt(proj2_p, o_sb_bias[0], bp, seq)
    oc_s = sb_step(proj2_s[:, :ATT_W], o_sb_bias[0], cache_sb_k.reshape(-1, PAGE, HEADS, DH),
                   cache_sb_v.reshape(-1, PAGE, HEADS, DH), page_table)
    hm_p, mC_p, mn_p, mm_p = mlstm_prompt(proj2_p, gates_p, gate_bias, gmn, bp, seq)
    hm_s, mC_s, mn_s, mm_s = mlstm_step(proj2_s, gates_s, gate_bias, gmn,
                                        state_mlstm_C.reshape(-1, M_HEADS, M_DH, M_DH)[:bs],
                                        state_mlstm_n[0], state_mlstm_m[0])

    p3 = proj2_p.reshape(bp, seq, O_MAIN)
    n_pg = seq // PAGE
    sb_k_p = p3[:, :, ATT_W:2 * ATT_W].reshape(bp, n_pg, PAGE, HEADS, DH)
    sb_v_p = p3[:, :, 2 * ATT_W:3 * ATT_W].reshape(bp, n_pg, PAGE, HEADS, DH)
    sb_k_s = proj2_s[:, ATT_W:2 * ATT_W].reshape(bs, 1, HEADS, DH)
    sb_v_s = proj2_s[:, 2 * ATT_W:3 * ATT_W].reshape(bs, 1, HEADS, DH)

    xp = out_proj(oc_p, hm_p, w_out2, xp, m_p[2], seq, 1024, 512, name="o_out_p")
    xs = out_proj(oc_s, hm_s, w_out2, xs, m_s[2], 1, bs, 512, name="o_out_s")
    xp, xs = peer_layer(1, xp, xs, m_p, m_s, True)

    y_prompt = xp.reshape(bp, seq, d)
    y_sample = xs.reshape(bs, 1, d)
    st = lambda a: a[None]
    return (y_prompt, y_sample, st(conv_p), st(conv_s), st(h_p.reshape(bp, RG_WIDTH)), st(h_s),
            st(swa_k_p), st(swa_k_s), st(swa_v_p), st(swa_v_s),
            st(sb_k_p), st(sb_k_s), st(sb_v_p), st(sb_v_s),
            st(mC_p), st(mC_s), st(mn_p), st(mn_s), st(mm_p), st(mm_s))
```

```python
import functools
import math

import jax
import jax.numpy as jnp
from jax import lax
from jax.experimental import pallas as pl
from jax.experimental.pallas import tpu as pltpu

f32 = jnp.float32
bf16 = jnp.bfloat16

D_MODEL = 2048
PAST_LEN = 16384
PAGE = 128
RG_WIDTH = 1024
RG_BLOCKS = 8
CONV_W = 4
RG_C = 8.0
HEADS = 8
DH = 128
ATT_W = HEADS * DH
DIL_PATTERNS = ((128, 1), (512, 4), (2048, 16))
ROT_DIMS = 32
ROPE_THETA = 500000.0
M_HEADS = 4
M_DH = 256
M_WIDTH = M_HEADS * M_DH
CHUNK = 128
E_IN = 2 * RG_WIDTH + 3 * ATT_W
O_MAIN = 3 * ATT_W + 4 * M_WIDTH
PEER_HEADS = 8
N_KEYS = 128
N_EXPERTS = N_KEYS * N_KEYS
PEER_TOPK = 16
EPS = 1e-6
LANES = 128
VMEM_LIMIT = 56 * 1024 * 1024
NEG_INF = float("-inf")


def _cparams(*sem):
    return pltpu.CompilerParams(dimension_semantics=sem, vmem_limit_bytes=VMEM_LIMIT)


def _dot(a, b):
    return jnp.dot(a, b, preferred_element_type=f32)


def _dot_nt(a, b):
    return lax.dot_general(a, b, (((1,), (1,)), ((), ())), preferred_element_type=f32)


def _split3(x):
    hi = x.astype(bf16)
    r = x - hi.astype(f32)
    mid = r.astype(bf16)
    lo = (r - mid.astype(f32)).astype(bf16)
    return hi, mid, lo


def _sigmoid(x):
    return 1.0 / (1.0 + jnp.exp(-x))


def _log_sigmoid_pair(z):
    l1p = jnp.log1p(jnp.exp(-jnp.abs(z)))
    return jnp.minimum(z, 0.0) - l1p, -jnp.maximum(z, 0.0) - l1p


def _softplus(z):
    return jnp.maximum(z, 0.0) + jnp.log(1.0 + jnp.exp(-jnp.abs(z)))


def _gelu(x):
    c = math.sqrt(2.0 / math.pi)
    return 0.5 * x * (1.0 + jnp.tanh(c * (x + 0.044715 * (x * x * x))))


def _iota(shape, dim):
    return lax.broadcasted_iota(jnp.int32, shape, dim)


def _rowsum_bcast(x):
    ones = jnp.ones((LANES, LANES), bf16)
    hi = x.astype(bf16)
    lo = (x - hi.astype(f32)).astype(bf16)
    return _dot(hi, ones) + _dot(lo, ones)


def _adaln_kernel(c_ref, w_ref, b_ref, o_ref):
    c = c_ref[...]
    s = c * _sigmoid(c)
    w = w_ref[0]
    s_hi = s.astype(bf16)
    s_lo = (s - s_hi.astype(f32)).astype(bf16)
    w_hi = w.astype(bf16)
    w_lo = (w - w_hi.astype(f32)).astype(bf16)
    o_ref[0] = _dot(s_hi, w_hi) + _dot(s_hi, w_lo) + _dot(s_lo, w_hi) + b_ref[0]


def adaln_all(c_all, w_ada, b_ada):
    depth, d, n = w_ada.shape
    rows = c_all.shape[0]
    tn = 1024
    return pl.pallas_call(
        _adaln_kernel,
        grid=(depth, n // tn),
        in_specs=[
            pl.BlockSpec((rows, d), lambda l, j: (0, 0)),
            pl.BlockSpec((1, d, tn), lambda l, j: (l, 0, j)),
            pl.BlockSpec((1, 1, tn), lambda l, j: (l, 0, j)),
        ],
        out_specs=pl.BlockSpec((1, rows, tn), lambda l, j: (l, 0, j)),
        out_shape=jax.ShapeDtypeStruct((depth, rows, n), f32),
        compiler_params=_cparams("arbitrary", "arbitrary"),
        name="adaln",
    )(c_all, w_ada, b_ada.reshape(depth, 1, n))


def _modulated(x, g, sh, sc):
    ms = jnp.mean(x * x, axis=-1, keepdims=True)
    y = x * lax.rsqrt(ms + EPS) * g
    return y * (1.0 + sc) + sh


def _modmm_kernel(x_ref, g_ref, sh_ref, sc_ref, w_ref, o_ref, xn_ref):
    @pl.when(pl.program_id(1) == 0)
    def _():
        xn_ref[...] = _modulated(x_ref[...], g_ref[...], sh_ref[0], sc_ref[0]).astype(bf16)

    o_ref[...] = _dot(xn_ref[...], w_ref[...])


def _modmm_gate_kernel(x_ref, g_ref, sh_ref, sc_ref, w_ref, wg_ref, o_ref, og_ref, xn_ref):
    @pl.when(pl.program_id(1) == 0)
    def _():
        xn = _modulated(x_ref[...], g_ref[...], sh_ref[0], sc_ref[0]).astype(bf16)
        xn_ref[...] = xn
        og_ref[...] = _dot(xn, wg_ref[...])

    o_ref[...] = _dot(xn_ref[...], w_ref[...])


def _mod_specs(mod, m, tm, k, rows_per_batch):
    if rows_per_batch >= tm:
        assert rows_per_batch % tm == 0
        per = rows_per_batch // tm
        return mod.reshape(-1, 1, k), pl.BlockSpec((1, 1, k), lambda i, j: (i // per, 0, 0))
    assert rows_per_batch == 1
    rows = mod
    if rows.shape[0] < m:
        rows = jnp.pad(rows, ((0, m - rows.shape[0]), (0, 0)))
    return rows.reshape(1, m, k), pl.BlockSpec((1, tm, k), lambda i, j: (0, i, 0))


def mod_matmul(x, g, shift, scale, w, n_out, rows_per_batch, tm, tn, w_gate=None, name="modmm"):
    m, k = x.shape
    sh, sh_spec = _mod_specs(shift, m, tm, k, rows_per_batch)
    sc, sc_spec = _mod_specs(scale, m, tm, k, rows_per_batch)
    in_specs = [
        pl.BlockSpec((tm, k), lambda i, j: (i, 0)),
        pl.BlockSpec((1, k), lambda i, j: (0, 0)),
        sh_spec,
        sc_spec,
        pl.BlockSpec((k, tn), lambda i, j: (0, j)),
    ]
    args = [x, g.reshape(1, k), sh, sc, w]
    out_specs = pl.BlockSpec((tm, tn), lambda i, j: (i, j))
    out_shape = jax.ShapeDtypeStruct((m, n_out), f32)
    kern = _modmm_kernel
    if w_gate is not None:
        in_specs.append(pl.BlockSpec((k, LANES), lambda i, j: (0, 0)))
        args.append(w_gate)
        out_specs = [out_specs, pl.BlockSpec((tm, LANES), lambda i, j: (i, 0))]
        out_shape = [out_shape, jax.ShapeDtypeStruct((m, LANES), f32)]
        kern = _modmm_gate_kernel
    return pl.pallas_call(
        kern,
        grid=(m // tm, n_out // tn),
        in_specs=in_specs,
        out_specs=out_specs,
        out_shape=out_shape,
        scratch_shapes=[pltpu.VMEM((tm, k), bf16)],
        compiler_params=_cparams("arbitrary", "arbitrary"),
        name=name,
    )(*args)


def _outproj_kernel(a1_ref, a2_ref, w1_ref, w2_ref, x_ref, gt_ref, o_ref):
    y = _dot(a1_ref[...].astype(bf16), w1_ref[...]) + _dot(a2_ref[...].astype(bf16), w2_ref[...])
    o_ref[...] = x_ref[...] + gt_ref[0] * y


def out_proj(a1, a2, w, x, gate, rows_per_batch, tm, tn, name="outproj"):
    m, k1 = a1.shape
    k2 = a2.shape[1]
    n = w.shape[1]
    gt, gt_spec = _mod_specs(gate, m, tm, n, rows_per_batch)
    if gt.shape[1] == 1:
        per = rows_per_batch // tm
        gt_spec = pl.BlockSpec((1, 1, tn), lambda i, j: (i // per, 0, j))
    else:
        gt_spec = pl.BlockSpec((1, tm, tn), lambda i, j: (0, i, j))
    return pl.pallas_call(
        _outproj_kernel,
        grid=(m // tm, n // tn),
        in_specs=[
            pl.BlockSpec((tm, k1), lambda i, j: (i, 0)),
            pl.BlockSpec((tm, k2), lambda i, j: (i, 0)),
            pl.BlockSpec((k1, tn), lambda i, j: (0, j)),
            pl.BlockSpec((k2, tn), lambda i, j: (k1 // k2, j)),
            pl.BlockSpec((tm, tn), lambda i, j: (i, j)),
            gt_spec,
        ],
        out_specs=pl.BlockSpec((tm, tn), lambda i, j: (i, j)),
        out_shape=jax.ShapeDtypeStruct((m, n), f32),
        compiler_params=_cparams("arbitrary", "arbitrary"),
        name=name,
    )(a1, a2, w, w, x, gt)


def _rglru_gates(xc, wr_ref, br, wi_ref, bi, lam):
    xb = xc.astype(bf16)
    rs, gs = [], []
    for hb in range(RG_BLOCKS):
        sl = slice(hb * LANES, (hb + 1) * LANES)
        rs.append(_dot(xb[:, sl], wr_ref[hb]))
        gs.append(_dot(xb[:, sl], wi_ref[hb]))
    r = _sigmoid(jnp.concatenate(rs, axis=1) + br)
    ig = _sigmoid(jnp.concatenate(gs, axis=1) + bi)
    softplus_neg_lam = jnp.maximum(-lam, 0.0) + jnp.log1p(jnp.exp(-jnp.abs(lam)))
    log_a = -RG_C * r * softplus_neg_lam
    a = jnp.exp(log_a)
    u = jnp.sqrt(-jnp.tanh(log_a) * (a * a + 1.0)) * ig * xc
    return a, u


def _rglru_kernel(xa_ref, ga_ref, cw_ref, cb_ref, wr_ref, br_ref, wi_ref, bi_ref, lam_ref,
                  ya_ref, hl_ref, xprev_ref, hc_ref):
    t_idx = pl.program_id(1)
    tt = xa_ref.shape[0]

    @pl.when(t_idx == 0)
    def _():
        xprev_ref[...] = jnp.zeros_like(xprev_ref)
        hc_ref[...] = jnp.zeros_like(hc_ref)

    xa = xa_ref[...]
    xprev = xprev_ref[...]
    row8 = _iota((8, RG_WIDTH), 0)
    xc = cb_ref[...] + cw_ref[CONV_W - 1:CONV_W, :] * xa
    for k in range(1, CONV_W):
        rolled = pltpu.roll(xa, k, 0)
        head = jnp.where(row8 < k, pltpu.roll(xprev, k, 0), rolled[0:8])
        shifted = jnp.concatenate([head, rolled[8:]], axis=0)
        xc = xc + cw_ref[CONV_W - 1 - k:CONV_W - k, :] * shifted
    xprev_ref[...] = xa[tt - 8:tt]

    a, u = _rglru_gates(xc, wr_ref, br_ref[...], wi_ref, bi_ref[...], lam_ref[...])
    row = _iota((tt, RG_WIDTH), 0)
    s = 1
    while s < tt:
        a_sh = pltpu.roll(a, s, 0)
        u_sh = pltpu.roll(u, s, 0)
        ok = row >= s
        u = jnp.where(ok, a * u_sh + u, u)
        a = jnp.where(ok, a * a_sh, a)
        s *= 2
    h = a * hc_ref[...] + u
    hc_ref[...] = h[tt - 1:tt]
    hl_ref[0] = h[tt - 1:tt]
    ya_ref[...] = h * _gelu(ga_ref[...])


def rglru_prompt(proj, bsz, seq, cw, cb, wr, br, wi, bi, lam, tt=256):
    nt = seq // tt
    vec = lambda: pl.BlockSpec((1, RG_WIDTH), lambda b, t: (0, 0))
    return pl.pallas_call(
        _rglru_kernel,
        grid=(bsz, nt),
        in_specs=[
            pl.BlockSpec((tt, RG_WIDTH), lambda b, t: (b * nt + t, 0)),
            pl.BlockSpec((tt, RG_WIDTH), lambda b, t: (b * nt + t, 1)),
            pl.BlockSpec((CONV_W, RG_WIDTH), lambda b, t: (0, 0)),
            vec(),
            pl.BlockSpec((RG_BLOCKS, LANES, LANES), lambda b, t: (0, 0, 0)),
            vec(),
            pl.BlockSpec((RG_BLOCKS, LANES, LANES), lambda b, t: (0, 0, 0)),
            vec(),
            vec(),
        ],
        out_specs=[
            pl.BlockSpec((tt, RG_WIDTH), lambda b, t: (b * nt + t, 0)),
            pl.BlockSpec((1, 1, RG_WIDTH), lambda b, t: (b, 0, 0)),
        ],
        out_shape=[
            jax.ShapeDtypeStruct((bsz * seq, RG_WIDTH), f32),
            jax.ShapeDtypeStruct((bsz, 1, RG_WIDTH), f32),
        ],
        scratch_shapes=[pltpu.VMEM((8, RG_WIDTH), f32), pltpu.VMEM((1, RG_WIDTH), f32)],
        compiler_params=_cparams("arbitrary", "arbitrary"),
        name="rglru_prompt",
    )(proj, proj, cw, cb, wr, br, wi, bi, lam)


def _rglru_step_kernel(xa_ref, ga_ref, b0_ref, b1_ref, b2_ref, h0_ref, cw_ref, cb_ref,
                       wr_ref, br_ref, wi_ref, bi_ref, lam_ref, ya_ref, h_ref):
    xa = xa_ref[...]
    xc = (cb_ref[...] + cw_ref[0:1, :] * b0_ref[...] + cw_ref[1:2, :] * b1_ref[...]
          + cw_ref[2:3, :] * b2_ref[...] + cw_ref[3:4, :] * xa)
    a, u = _rglru_gates(xc, wr_ref, br_ref[...], wi_ref, bi_ref[...], lam_ref[...])
    h = a * h0_ref[...] + u
    h_ref[...] = h
    ya_ref[...] = h * _gelu(ga_ref[...])


def rglru_step(proj_s, conv_state, h0, cw, cb, wr, br, wi, bi, lam):
    n = proj_s.shape[0]
    full = lambda shape: pl.BlockSpec(shape, lambda i: tuple(0 for _ in shape))
    return pl.pallas_call(
        _rglru_step_kernel,
        grid=(1,),
        in_specs=[
            pl.BlockSpec((n, RG_WIDTH), lambda i: (0, 0)),
            pl.BlockSpec((n, RG_WIDTH), lambda i: (0, 1)),
            full((n, RG_WIDTH)), full((n, RG_WIDTH)), full((n, RG_WIDTH)), full((n, RG_WIDTH)),
            full((CONV_W, RG_WIDTH)), full((1, RG_WIDTH)),
            full((RG_BLOCKS, LANES, LANES)), full((1, RG_WIDTH)),
            full((RG_BLOCKS, LANES, LANES)), full((1, RG_WIDTH)), full((1, RG_WIDTH)),
        ],
        out_specs=[full((n, RG_WIDTH)), full((n, RG_WIDTH))],
        out_shape=[jax.ShapeDtypeStruct((n, RG_WIDTH), f32)] * 2,
        compiler_params=_cparams("arbitrary"),
        name="rglru_step",
    )(proj_s, proj_s, conv_state[:, 0], conv_state[:, 1], conv_state[:, 2], h0,
      cw, cb, wr, br, wi, bi, lam)


def _rope_tables(pos):
    half = ROT_DIMS // 2
    inv = ROPE_THETA ** (-jnp.arange(half, dtype=f32) / half)
    ang = pos.astype(f32)[:, None] * inv[None, :]
    cos, sin = jnp.cos(ang), jnp.sin(ang)
    n = pos.shape[0]
    ctab = jnp.concatenate([cos, cos, jnp.ones((n, DH - ROT_DIMS), f32)], axis=1)
    stab = jnp.concatenate([-sin, sin, jnp.zeros((n, DH - ROT_DIMS), f32)], axis=1)
    return ctab, stab


def _rope_head(xh, ctab, stab, lane):
    half = ROT_DIMS // 2
    partner = jnp.where(lane < half, pltpu.roll(xh, DH - half, 1), pltpu.roll(xh, half, 1))
    return xh * ctab + partner * stab


def _rope_kernel(q_ref, k_ref, v_ref, c_ref, s_ref, qo_ref, ko_ref, vo_ref):
    ctab, stab = c_ref[...], s_ref[...]
    lane = _iota((q_ref.shape[0], DH), 1)
    for hb in range(HEADS):
        sl = slice(hb * DH, (hb + 1) * DH)
        qo_ref[:, hb, :] = _rope_head(q_ref[:, sl], ctab, stab, lane) * (DH ** -0.5)
        ko_ref[:, hb, :] = _rope_head(k_ref[:, sl], ctab, stab, lane)
        vo_ref[:, hb, :] = v_ref[:, sl]


def rope_qkv(proj, ctab, stab, rows_per_seq, tt):
    m = proj.shape[0]
    nt = max(rows_per_seq // tt, 1)
    if ctab.shape[0] == 1:
        tab_spec = pl.BlockSpec((1, DH), lambda i: (0, 0))
    else:
        tab_spec = pl.BlockSpec((tt, DH), lambda i: (i % nt, 0))
    return pl.pallas_call(
        _rope_kernel,
        grid=(m // tt,),
        in_specs=[
            pl.BlockSpec((tt, ATT_W), lambda i: (i, 2)),
            pl.BlockSpec((tt, ATT_W), lambda i: (i, 3)),
            pl.BlockSpec((tt, ATT_W), lambda i: (i, 4)),
            tab_spec, tab_spec,
        ],
        out_specs=[pl.BlockSpec((tt, HEADS, DH), lambda i: (i, 0, 0))] * 3,
        out_shape=[jax.ShapeDtypeStruct((m, HEADS, DH), f32)] * 3,
        compiler_params=_cparams("arbitrary"),
        name="rope",
    )(proj, proj, proj, ctab, stab)


def _dil_kernel(*refs, span, first, last):
    if first:
        q_ref, kc_ref, kp_ref, vc_ref, vp_ref = refs[:5]
        outs = refs[5:]
    else:
        q_ref, kc_ref, kp_ref, vc_ref, vp_ref, op_ref, lp_ref = refs[:7]
        outs = refs[7:]
    o_ref = outs[0]
    s_ref, p_ref, den_ref, lse_ref = outs[-4:]
    tq = q_ref.shape[1]
    blk = pl.program_id(2)
    qi = _iota((tq, 2 * tq), 0)
    col = _iota((tq, 2 * tq), 1)
    rel = jnp.where(col < tq, qi - col, qi - col + 2 * tq)
    ok = (rel >= 0) & (rel <= span) & ((col < tq) | (blk > 0))
    for hb in range(HEADS):
        qh = q_ref[0, :, 0, hb, :].astype(bf16)
        s_ref[hb, :, :tq] = _dot_nt(qh, kc_ref[0, :, 0, hb, :].astype(bf16))
        s_ref[hb, :, tq:] = _dot_nt(qh, kp_ref[0, :, 0, hb, :].astype(bf16))
    for hb in range(HEADS):
        s = jnp.where(ok, s_ref[hb], NEG_INF)
        mx = jnp.max(s, axis=1, keepdims=True)
        p = jnp.exp(s - mx)
        den = jnp.sum(p, axis=1, keepdims=True)
        p_ref[hb] = p.astype(bf16)
        den_ref[hb] = jnp.broadcast_to(den, (tq, DH))
        lse_ref[hb] = jnp.broadcast_to(mx + jnp.log(den), (tq, DH))
    for hb in range(HEADS):
        num = (_dot(p_ref[hb, :, :tq], vc_ref[0, :, 0, hb, :].astype(bf16))
               + _dot(p_ref[hb, :, tq:], vp_ref[0, :, 0, hb, :].astype(bf16)))
        o_g = num / den_ref[hb]
        lse_g = lse_ref[hb]
        if first:
            o_new, lse_new = o_g, lse_g
        else:
            lse_prev = lp_ref[0, :, 0, hb, :]
            top = jnp.maximum(lse_prev, lse_g)
            e_prev = jnp.exp(lse_prev - top)
            e_g = jnp.exp(lse_g - top)
            tot = e_prev + e_g
            o_new = (op_ref[0, :, 0, hb, :] * e_prev + o_g * e_g) / tot
            lse_new = top + jnp.log(tot)
        o_ref[0, :, 0, hb, :] = o_new
        if not last:
            outs[1][0, :, 0, hb, :] = lse_new


def dilated_prompt(q4, k4, v4, bsz, seq, tq=128):
    o = lse = None
    n_pat = len(DIL_PATTERNS)
    for gi, (w, d) in enumerate(DIL_PATTERNS):
        first, last = gi == 0, gi == n_pat - 1
        sd = seq // d
        nq = sd // tq
        span = w // d
        view = lambda a: a.reshape(bsz, sd, d, HEADS, DH)
        cur = lambda b, r, i: (b, i, r, 0, 0)
        prev = lambda b, r, i: (b, jnp.maximum(i - 1, 0), r, 0, 0)
        blk = (1, tq, 1, HEADS, DH)
        in_specs = [pl.BlockSpec(blk, cur), pl.BlockSpec(blk, cur), pl.BlockSpec(blk, prev),
                    pl.BlockSpec(blk, cur), pl.BlockSpec(blk, prev)]
        args = [view(q4), view(k4), view(k4), view(v4), view(v4)]
        if not first:
            in_specs += [pl.BlockSpec(blk, cur), pl.BlockSpec(blk, cur)]
            args += [view(o), view(lse)]
        out_specs = [pl.BlockSpec(blk, cur)]
        out_shape = [jax.ShapeDtypeStruct((bsz, sd, d, HEADS, DH), f32)]
        if not last:
            out_specs.append(pl.BlockSpec(blk, cur))
            out_shape.append(jax.ShapeDtypeStruct((bsz, sd, d, HEADS, DH), f32))
        res = pl.pallas_call(
            functools.partial(_dil_kernel, span=span, first=first, last=last),
            grid=(bsz, d, nq),
            in_specs=in_specs,
            out_specs=out_specs,
            out_shape=out_shape,
            scratch_shapes=[pltpu.VMEM((HEADS, tq, 2 * tq), f32), pltpu.VMEM((HEADS, tq, 2 * tq), bf16),
                            pltpu.VMEM((HEADS, tq, DH), f32), pltpu.VMEM((HEADS, tq, DH), f32)],
            compiler_params=_cparams("arbitrary", "arbitrary", "arbitrary"),
            name=f"dilattn_d{d}",
        )(*args)
        o = res[0]
        if not last:
            lse = res[1]
    return o.reshape(bsz * seq, ATT_W)


def _dil_step_kernel(q_ref, kn_ref, vn_ref, k1_ref, k4_ref, k16_ref, v1_ref, v4_ref, v16_ref, o_ref):
    q = q_ref[0]
    kn, vn = kn_ref[0], vn_ref[0]
    s_self = _rowsum_bcast(q * kn)
    o_gs, lse_gs = [], []
    for k_ref, v_ref in ((k1_ref, v1_ref), (k4_ref, v4_ref), (k16_ref, v16_ref)):
        k3 = k_ref[0, :, 0]
        nk = k3.shape[0]
        s = _rowsum_bcast((k3 * q[None]).reshape(nk * HEADS, DH)).reshape(nk, HEADS, DH)
        mx = jnp.maximum(jnp.max(s, axis=0), s_self)
        p = jnp.exp(s - mx[None])
        p_self = jnp.exp(s_self - mx)
        den = jnp.sum(p, axis=0) + p_self
        num = jnp.sum(p * v_ref[0, :, 0], axis=0) + p_self * vn
        o_gs.append(num / den)
        lse_gs.append(mx + jnp.log(den))
    top = jnp.maximum(jnp.maximum(lse_gs[0], lse_gs[1]), lse_gs[2])
    es = [jnp.exp(l - top) for l in lse_gs]
    tot = es[0] + es[1] + es[2]
    o_ref[0] = (o_gs[0] * es[0] + o_gs[1] * es[1] + o_gs[2] * es[2]) / tot


def dilated_step(q4, k4, v4, cache_k, cache_v):
    n, win = cache_k.shape[0], cache_k.shape[1]
    one = pl.BlockSpec((1, HEADS, DH), lambda b: (b, 0, 0))
    args = [q4, k4, v4]
    in_specs = [one, one, one]
    for cache in (cache_k, cache_v):
        for (w, d) in DIL_PATTERNS:
            nkeys = w // d
            assert win % d == 0 and (win // d) % nkeys == 0 and (win - w) % (d * nkeys) == 0
            args.append(cache.reshape(n, win // d, d, HEADS, DH))
            in_specs.append(pl.BlockSpec((1, nkeys, 1, HEADS, DH),
                                         lambda b, blk=(win - w) // d // nkeys: (b, blk, 0, 0, 0)))
    return pl.pallas_call(
        _dil_step_kernel,
        grid=(n,),
        in_specs=in_specs,
        out_specs=one,
        out_shape=jax.ShapeDtypeStruct((n, HEADS, DH), f32),
        compiler_params=_cparams("arbitrary"),
        name="dilattn_step",
    )(*args).reshape(n, ATT_W)


def _sb_kernel(bias_ref, q_ref, k_ref, v_ref, o_ref, kb_ref, vb_ref, qs_ref, t_ref, spb_ref, wb_ref,
               acc_ref, run_ref):
    tq = q_ref.shape[1]
    nh = q_ref.shape[2] // DH
    kt_w = t_ref.shape[2]
    sub = PAGE
    hg = pl.program_id(1)
    qb = pl.program_id(2)

    @pl.when(qb == 0)
    def _():
        kb_ref[...] = k_ref[0].astype(bf16)
        vb_ref[...] = v_ref[0].astype(bf16)

    rr = _iota((sub, 2 * sub), 0)
    cc = _iota((sub, 2 * sub), 1)
    tri = jnp.where((rr > cc) | (cc >= sub), 1.0, 0.0).astype(bf16)
    for h in range(nh):
        qs_ref[h] = (q_ref[0, :, h * DH:(h + 1) * DH] * (DH ** -0.5)).astype(bf16)
    acc_ref[...] = jnp.zeros_like(acc_ref)
    run_ref[...] = jnp.zeros_like(run_ref)

    def macro(start, masked):
        for h in range(nh):
            hs = slice(h * DH, (h + 1) * DH)
            z = _dot_nt(qs_ref[h], kb_ref[pl.ds(start, kt_w), hs]) + bias_ref[hg * nh + h]
            sp = _softplus(z)
            t = z - sp
            if masked:
                ok = start + _iota((tq, kt_w), 1) < qb * tq + _iota((tq, kt_w), 0)
                sp = jnp.where(ok, sp, 0.0)
                t = jnp.where(ok, t, NEG_INF)
            t_ref[h] = t
            spb_ref[h] = sp.astype(bf16)
        for h in range(nh):
            run = run_ref[h]
            for kt in range(kt_w // sub - 1, -1, -1):
                ks = slice(kt * sub, (kt + 1) * sub)
                cs = _dot(spb_ref[h, :, ks], tri)
                wb_ref[h, :, ks] = jnp.exp(t_ref[h, :, ks] - cs[:, :sub] - run).astype(bf16)
                run = run + cs[:, sub:]
            run_ref[h] = run
        for h in range(nh):
            hs = slice(h * DH, (h + 1) * DH)
            acc_ref[h] += _dot(wb_ref[h], vb_ref[pl.ds(start, kt_w), hs])

    top = (qb * tq) // kt_w
    macro(pl.multiple_of(top * kt_w, kt_w), True)

    def body(it, _):
        macro(pl.multiple_of((top - 1 - it) * kt_w, kt_w), False)
        return 0

    lax.fori_loop(0, top, body, 0)
    for h in range(nh):
        o_ref[0, :, h * DH:(h + 1) * DH] = acc_ref[h]


def sb_prompt(proj2, bias, bsz, seq, tq=256, nh=4, kt_w=512):
    assert kt_w % tq == 0 and seq % kt_w == 0
    p3 = proj2.reshape(bsz, seq, proj2.shape[1])
    nq = seq // tq
    ng = HEADS // nh
    wd = nh * DH
    return pl.pallas_call(
        _sb_kernel,
        grid=(bsz, ng, nq),
        in_specs=[
            pl.BlockSpec(memory_space=pltpu.SMEM),
            pl.BlockSpec((1, tq, wd), lambda b, g, i: (b, i, g)),
            pl.BlockSpec((1, seq, wd), lambda b, g, i: (b, 0, ng + g)),
            pl.BlockSpec((1, seq, wd), lambda b, g, i: (b, 0, 2 * ng + g)),
        ],
        out_specs=pl.BlockSpec((1, tq, wd), lambda b, g, i: (b, i, g)),
        out_shape=jax.ShapeDtypeStruct((bsz, seq, ATT_W), f32),
        scratch_shapes=[
            pltpu.VMEM((seq, wd), bf16), pltpu.VMEM((seq, wd), bf16),
            pltpu.VMEM((nh, tq, DH), bf16),
            pltpu.VMEM((nh, tq, kt_w), f32), pltpu.VMEM((nh, tq, kt_w), bf16),
            pltpu.VMEM((nh, tq, kt_w), bf16),
            pltpu.VMEM((nh, tq, DH), f32), pltpu.VMEM((nh, tq, PAGE), f32),
        ],
        compiler_params=_cparams("arbitrary", "arbitrary", "arbitrary"),
        name="sb_prompt",
    )(bias, p3, p3, p3).reshape(bsz * seq, ATT_W)


def _sb_step_kernel(pt_ref, q_ref, bias_ref, *refs, npg):
    k_refs, v_refs = refs[:npg], refs[npg:2 * npg]
    o_ref, acc_ref, carry_ref = refs[2 * npg:]
    j = pl.program_id(1)

    @pl.when(j == 0)
    def _():
        acc_ref[...] = jnp.zeros_like(acc_ref)
        carry_ref[...] = jnp.zeros_like(carry_ref)

    q = q_ref[0]
    bias = bias_ref[...]
    acc = acc_ref[...]
    run = carry_ref[...]
    for p in range(npg):
        k3 = k_refs[p][0]
        z = _rowsum_bcast((k3 * q[None]).reshape(PAGE * HEADS, DH)).reshape(PAGE, HEADS, DH) + bias[None]
        sp = _softplus(z)
        later = []
        for key in range(PAGE - 1, -1, -1):
            later.append(run)
            run = run + sp[key]
        suffix = jnp.stack(later[::-1], axis=0)
        w = jnp.exp(z - sp - suffix)
        acc = acc + jnp.sum(w * v_refs[p][0], axis=0)
    acc_ref[...] = acc
    carry_ref[...] = run

    @pl.when(j == pl.num_programs(1) - 1)
    def _():
        o_ref[0] = acc


def sb_step(q_s, bias, cache_k, cache_v, page_table, npg=4):
    n, n_pages = page_table.shape
    q4 = (q_s * (DH ** -0.5)).reshape(n, HEADS, DH)
    bias4 = jnp.broadcast_to(bias[:, None], (HEADS, DH))

    def page(p):
        return lambda b, j, pt: (pt[b, n_pages - 1 - (j * npg + p)], 0, 0, 0)

    kv_specs = [pl.BlockSpec((1, PAGE, HEADS, DH), page(p)) for p in range(npg)]
    grid_spec = pltpu.PrefetchScalarGridSpec(
        num_scalar_prefetch=1,
        grid=(n, n_pages // npg),
        in_specs=[
            pl.BlockSpec((1, HEADS, DH), lambda b, j, pt: (b, 0, 0)),
            pl.BlockSpec((HEADS, DH), lambda b, j, pt: (0, 0)),
        ] + kv_specs + kv_specs,
        out_specs=pl.BlockSpec((1, HEADS, DH), lambda b, j, pt: (b, 0, 0)),
        scratch_shapes=[pltpu.VMEM((HEADS, DH), f32), pltpu.VMEM((HEADS, DH), f32)],
    )
    return pl.pallas_call(
        functools.partial(_sb_step_kernel, npg=npg),
        grid_spec=grid_spec,
        out_shape=jax.ShapeDtypeStruct((n, HEADS, DH), f32),
        compiler_params=_cparams("arbitrary", "arbitrary"),
        name="sb_step",
    )(page_table, q4, bias4, *([cache_k] * npg), *([cache_v] * npg)).reshape(n, ATT_W)


def _mlstm_kernel(q_ref, k_ref, v_ref, og_ref, gate_ref, gb_ref, gn_ref,
                  h_ref, c_out, n_out, m_out, c_s, n_s, m_s):
    ci = pl.program_id(1)
    L = CHUNK

    @pl.when(ci == 0)
    def _():
        c_s[...] = jnp.zeros_like(c_s)
        n_s[...] = jnp.zeros_like(n_s)
        m_s[...] = jnp.zeros_like(m_s)

    gt = gate_ref[...] + gb_ref[...]
    gt_t = gt.T
    ri = _iota((L, L), 0)
    li = _iota((L, L), 1)
    causal = li <= ri
    tri_incl = jnp.where(causal, 1.0, 0.0).astype(bf16)
    tri_incl_t = jnp.where(ri <= li, 1.0, 0.0).astype(bf16)
    for h in range(M_HEADS):
        sl = slice(h * M_DH, (h + 1) * M_DH)
        ig_col = gt[:, h:h + 1]
        ig_row = gt_t[h:h + 1, :]
        lf_col = _log_sigmoid_pair(gt[:, M_HEADS + h:M_HEADS + h + 1])[0]
        lf_row = _log_sigmoid_pair(gt_t[M_HEADS + h:M_HEADS + h + 1, :])[0]
        c_hi, c_mid, c_lo = _split3(jnp.broadcast_to(lf_col, (L, L)))
        bcum_col = _dot(tri_incl, c_hi) + _dot(tri_incl, c_mid) + _dot(tri_incl, c_lo)
        r_hi, r_mid, r_lo = _split3(jnp.broadcast_to(lf_row, (L, L)))
        bcum_row = _dot(r_hi, tri_incl_t) + _dot(r_mid, tri_incl_t) + _dot(r_lo, tri_incl_t)
        m_prev = m_s[h:h + 1, :]
        dlog = jnp.where(causal, bcum_col - bcum_row + ig_row, NEG_INF)
        inter = bcum_col + m_prev
        m_t = jnp.maximum(inter, jnp.max(dlog, axis=1, keepdims=True))
        dw = jnp.exp(dlog - m_t)
        iw = jnp.exp(inter - m_t)
        qh = q_ref[:, sl]
        kh = k_ref[:, sl] * (M_DH ** -0.5)
        vh = v_ref[:, sl]
        qb, kb, vb = qh.astype(bf16), kh.astype(bf16), vh.astype(bf16)
        sw = dw * _dot_nt(qb, kb)
        c_prev = c_s[h]
        n_prev = n_s[h:h + 1, :]
        iw_col = iw[:, 0:1]
        num = _dot(sw.astype(bf16), vb) + iw_col * _dot_nt(qb, c_prev.astype(bf16))
        qn = jnp.sum(qb.astype(f32) * n_prev.astype(bf16).astype(f32), axis=1, keepdims=True)
        den = jnp.sum(sw, axis=1, keepdims=True) + iw_col * qn
        m_col = m_t[:, 0:1]
        hout = num / jnp.maximum(jnp.abs(den), jnp.exp(-m_col))
        m_last = m_t[L - 1:L, :]
        b_last = bcum_col[L - 1:L, :]
        wl_col = jnp.exp(b_last[:, 0:1] - bcum_col[:, 0:1] + ig_col - m_last[:, 0:1])
        wl_row = jnp.exp(b_last - bcum_row[0:1, :] + ig_row - m_last)
        decay = jnp.exp(b_last + m_prev - m_last)
        dsc = decay[:, 0:1]
        c_s[h] = dsc * c_prev + _dot((vh * wl_col).T.astype(bf16), kb)
        wl8 = jnp.broadcast_to(wl_row, (8, L)).astype(bf16)
        n_s[h:h + 1, :] = dsc * n_prev + _dot(wl8, kb)[0:1, :]
        m_s[h:h + 1, :] = m_last
        hn = hout * lax.rsqrt(jnp.mean(hout * hout, axis=1, keepdims=True) + EPS)
        h_ref[:, sl] = hn * gn_ref[:, sl] * _sigmoid(og_ref[:, sl])

    @pl.when(ci == pl.num_programs(1) - 1)
    def _():
        c_out[0] = c_s[...]
        n_out[0] = n_s[...]
        m_out[0] = m_s[...]


def mlstm_prompt(proj2, gates, gate_bias, g_mnorm, bsz, seq):
    nc = seq // CHUNK
    col = lambda c: pl.BlockSpec((CHUNK, M_WIDTH), lambda b, i, c=c: (b * nc + i, c))
    hm, c1, n1, m1 = pl.pallas_call(
        _mlstm_kernel,
        grid=(bsz, nc),
        in_specs=[
            col(3), col(4), col(5), col(6),
            pl.BlockSpec((CHUNK, LANES), lambda b, i: (b * nc + i, 0)),
            pl.BlockSpec((1, LANES), lambda b, i: (0, 0)),
            pl.BlockSpec((1, M_WIDTH), lambda b, i: (0, 0)),
        ],
        out_specs=[
            pl.BlockSpec((CHUNK, M_WIDTH), lambda b, i: (b * nc + i, 0)),
            pl.BlockSpec((1, M_HEADS, M_DH, M_DH), lambda b, i: (b, 0, 0, 0)),
            pl.BlockSpec((1, M_HEADS, M_DH), lambda b, i: (b, 0, 0)),
            pl.BlockSpec((1, M_HEADS, LANES), lambda b, i: (b, 0, 0)),
        ],
        out_shape=[
            jax.ShapeDtypeStruct((bsz * seq, M_WIDTH), f32),
            jax.ShapeDtypeStruct((bsz, M_HEADS, M_DH, M_DH), f32),
            jax.ShapeDtypeStruct((bsz, M_HEADS, M_DH), f32),
            jax.ShapeDtypeStruct((bsz, M_HEADS, LANES), f32),
        ],
        scratch_shapes=[
            pltpu.VMEM((M_HEADS, M_DH, M_DH), f32),
            pltpu.VMEM((M_HEADS, M_DH), f32),
            pltpu.VMEM((M_HEADS, LANES), f32),
        ],
        compiler_params=_cparams("arbitrary", "arbitrary"),
        name="mlstm_prompt",
    )(proj2, proj2, proj2, proj2, gates, gate_bias, g_mnorm)
    return hm, c1, n1, m1[:, :, 0]


def _mlstm_step_kernel(q_ref, k_ref, v_ref, og_ref, gn_ref, ig_ref, fg_ref, c_ref, n_ref, m_ref,
                       h_ref, c_out, n_out, m_out):
    q = q_ref[0, 0]
    k = k_ref[0, 0] * (M_DH ** -0.5)
    v = v_ref[0, 0]
    ig = ig_ref[0, 0]
    lf = _log_sigmoid_pair(fg_ref[0, 0])[0]
    m0 = m_ref[0, 0]
    c0 = c_ref[0, 0]
    n0 = n_ref[0, 0]
    inter = lf + m0
    m_t = jnp.maximum(inter, ig)
    dw = jnp.exp(ig - m_t)
    iw = jnp.exp(inter - m_t)
    rnd = lambda a: a.astype(bf16).astype(f32)
    qr, kr, vr = rnd(q), rnd(k), rnd(v)
    qk = jnp.sum(qr * kr, axis=1, keepdims=True)
    sw = dw * qk
    cq = jnp.sum(rnd(c0) * qr, axis=1, keepdims=True)
    num = rnd(sw) * vr + iw * cq
    den = sw + iw * jnp.sum(rnd(n0) * qr, axis=1, keepdims=True)
    hout = num / jnp.maximum(jnp.abs(den), jnp.exp(-m_t))
    c_out[0, 0] = iw * c0 + rnd(dw * v) * kr
    n_out[0, 0] = iw * n0 + rnd(dw) * kr
    m_out[0, 0] = m_t
    hn = hout * lax.rsqrt(jnp.mean(hout * hout, axis=0, keepdims=True) + EPS)
    h_ref[0, 0] = hn * gn_ref[0] * _sigmoid(og_ref[0, 0])


def mlstm_step(proj2_s, gates_s, gate_bias, g_mnorm, c0, n0, m0):
    n = proj2_s.shape[0]
    base = 3 * ATT_W
    seg = lambda i: proj2_s[:, base + i * M_WIDTH: base + (i + 1) * M_WIDTH].reshape(n, M_HEADS, M_DH)
    g = gates_s + gate_bias
    rowb = pl.BlockSpec((1, 1, 1, M_DH), lambda b, h: (b, h, 0, 0))
    colb = pl.BlockSpec((1, 1, M_DH, 1), lambda b, h: (b, h, 0, 0))
    scal = pl.BlockSpec((1, 1, 1, 1), lambda b, h: (b, h, 0, 0))
    hcol, c1, n1, m1 = pl.pallas_call(
        _mlstm_step_kernel,
        grid=(n, M_HEADS),
        in_specs=[
            rowb, rowb, colb, colb,
            pl.BlockSpec((1, M_DH, 1), lambda b, h: (h, 0, 0)),
            scal, scal,
            pl.BlockSpec((1, 1, M_DH, M_DH), lambda b, h: (b, h, 0, 0)),
            rowb, scal,
        ],
        out_specs=[colb, pl.BlockSpec((1, 1, M_DH, M_DH), lambda b, h: (b, h, 0, 0)), rowb, scal],
        out_shape=[
            jax.ShapeDtypeStruct((n, M_HEADS, M_DH, 1), f32),
            jax.ShapeDtypeStruct((n, M_HEADS, M_DH, M_DH), f32),
            jax.ShapeDtypeStruct((n, M_HEADS, 1, M_DH), f32),
            jax.ShapeDtypeStruct((n, M_HEADS, 1, 1), f32),
        ],
        compiler_params=_cparams("arbitrary", "arbitrary"),
        name="mlstm_step",
    )(seg(0)[:, :, None, :], seg(1)[:, :, None, :], seg(2)[..., None], seg(3)[..., None],
      g_mnorm.reshape(M_HEADS, M_DH, 1),
      g[:, 0:M_HEADS].reshape(n, M_HEADS, 1, 1), g[:, M_HEADS:2 * M_HEADS].reshape(n, M_HEADS, 1, 1),
      c0, n0[:, :, None, :], m0.reshape(n, M_HEADS, 1, 1))
    return hcol.reshape(n, M_WIDTH), c1, n1.reshape(n, M_HEADS, M_DH), m1.reshape(n, M_HEADS)


def _peerq_kernel(x_ref, g_ref, sh_ref, sc_ref, wt_ref, qt_ref, xmt_ref, xs_ref):
    @pl.when(pl.program_id(1) == 0)
    def _():
        xm = _modulated(x_ref[...], g_ref[...], sh_ref[0], sc_ref[0])
        xt = xm.T.astype(bf16)
        xs_ref[...] = xt
        xmt_ref[...] = xt

    qt_ref[...] = _dot(wt_ref[...], xs_ref[...])


def peer_query(x, g, shift, scale, wq_t, rows_per_batch, tm, tn=512):
    m, k = x.shape
    n = wq_t.shape[0]
    sh, sh_spec = _mod_specs(shift, m, tm, k, rows_per_batch)
    sc, sc_spec = _mod_specs(scale, m, tm, k, rows_per_batch)
    return pl.pallas_call(
        _peerq_kernel,
        grid=(m // tm, n // tn),
        in_specs=[
            pl.BlockSpec((tm, k), lambda i, j: (i, 0)),
            pl.BlockSpec((1, k), lambda i, j: (0, 0)),
            sh_spec, sc_spec,
            pl.BlockSpec((tn, k), lambda i, j: (j, 0)),
        ],
        out_specs=[pl.BlockSpec((tn, tm), lambda i, j: (j, i)),
                   pl.BlockSpec((k, tm), lambda i, j: (0, i))],
        out_shape=[jax.ShapeDtypeStruct((n, m), f32), jax.ShapeDtypeStruct((k, m), bf16)],
        scratch_shapes=[pltpu.VMEM((k, tm), bf16)],
        compiler_params=_cparams("arbitrary", "arbitrary"),
        name="peer_query",
    )(x, g.reshape(1, k), sh, sc, wq_t)


_N_TOP = PEER_TOPK + 1
_CAND_PAIRS = [(a, b) for a in range(_N_TOP) for b in range(_N_TOP) if (a + 1) * (b + 1) <= _N_TOP]
_N_CAND = -(-len(_CAND_PAIRS) // 8) * 8


def _extract_top(cur, ridx, n):
    vals = []
    big = float(cur.shape[0])
    for _ in range(n):
        mx = jnp.max(cur, axis=0, keepdims=True)
        first = jnp.min(jnp.where(cur == mx, ridx, big), axis=0, keepdims=True)
        cur = jnp.where(ridx == first, NEG_INF, cur)
        vals.append(mx)
    return vals


_ROUTER_UNROLL = 2


def _router_kernel(qt_ref, keys_ref, e1_ref, e2_ref, th_ref, cand_ref):
    ridx = _iota((N_KEYS, LANES), 0).astype(f32)
    cidx = _iota((_N_CAND, LANES), 0).astype(f32)
    k0 = keys_ref[0].astype(bf16)
    k1 = keys_ref[1].astype(bf16)
    cand_ref[...] = jnp.full(cand_ref.shape, NEG_INF, f32)

    def one_head(h, slot):
        r0 = pl.multiple_of(h * 2 * N_KEYS, 2 * N_KEYS)
        s1 = _dot(k0, qt_ref[pl.ds(r0, N_KEYS), :].astype(bf16))
        s2 = _dot(k1, qt_ref[pl.ds(r0 + N_KEYS, N_KEYS), :].astype(bf16))
        top_a = _extract_top(s1, ridx, _N_TOP)
        top_b = _extract_top(s2, ridx, _N_TOP)
        for ci, (a, b) in enumerate(_CAND_PAIRS):
            cand_ref[slot, ci:ci + 1, :] = top_a[a] + top_b[b]
        cs = _extract_top(cand_ref[slot], cidx, _N_TOP)
        z = jnp.ones_like(cs[0])
        for r in range(1, PEER_TOPK):
            z = z + jnp.exp(cs[r] - cs[0])
        inv_z = 1.0 / z
        mid = 0.5 * (cs[PEER_TOPK - 1] + cs[PEER_TOPK])
        o0 = pl.multiple_of(h * N_KEYS, N_KEYS)
        e1 = jnp.exp(s1 - top_a[0]) * inv_z
        e1_ref[:, h] = e1.reshape(N_KEYS // 8, 8, LANES)
        e2_ref[pl.ds(o0, N_KEYS), :] = jnp.exp(s2 - top_b[0])
        th_ref[pl.ds(h, 1), :] = jnp.exp(mid - cs[0]) * inv_z

    def body(hp, _):
        for slot in range(_ROUTER_UNROLL):
            one_head(hp * _ROUTER_UNROLL + slot, slot)
        return 0

    lax.fori_loop(0, PEER_HEADS // _ROUTER_UNROLL, body, 0)


def peer_router(qt, keys):
    n, m = qt.shape
    half = PEER_HEADS * N_KEYS
    return pl.pallas_call(
        _router_kernel,
        grid=(m // LANES,),
        in_specs=[pl.BlockSpec((n, LANES), lambda i: (0, i)),
                  pl.BlockSpec((2, N_KEYS, N_KEYS), lambda i: (0, 0, 0))],
        out_specs=[pl.BlockSpec((N_KEYS // 8, PEER_HEADS, 8, LANES), lambda i: (0, 0, 0, i)),
                   pl.BlockSpec((half, LANES), lambda i: (0, i)),
                   pl.BlockSpec((PEER_HEADS, LANES), lambda i: (0, i))],
        out_shape=[jax.ShapeDtypeStruct((N_KEYS // 8, PEER_HEADS, 8, m), f32),
                   jax.ShapeDtypeStruct((half, m), f32),
                   jax.ShapeDtypeStruct((PEER_HEADS, m), f32)],
        scratch_shapes=[pltpu.VMEM((_ROUTER_UNROLL, _N_CAND, LANES), f32)],
        compiler_params=_cparams("arbitrary"),
        name="peer_router",
    )(qt, keys)


def _experts_kernel(xmt_ref, e1_ref, e2_ref, th_ref, u_ref, v_ref, x_ref, gt_ref, gf_ref,
                    o_ref, w_ref, *, final_norm):
    c = pl.program_id(1)
    tm, te = w_ref.shape
    ni = te // N_KEYS

    @pl.when(c == 0)
    def _():
        o_ref[...] = jnp.zeros_like(o_ref)

    act = _gelu(_dot(u_ref[...], xmt_ref[...]))
    for ii in range(ni):
        es = slice(ii * N_KEYS, (ii + 1) * N_KEYS)
        for lc in range(tm // LANES):
            ls = slice(lc * LANES, (lc + 1) * LANES)
            g = jnp.zeros((N_KEYS, LANES), f32)
            for h in range(PEER_HEADS):
                row = e1_ref[ii // 8, h, ii % 8:ii % 8 + 1, ls]
                pr = e2_ref[h * N_KEYS:(h + 1) * N_KEYS, ls] * row
                g = g + jnp.where(pr >= th_ref[h:h + 1, ls], pr, 0.0)
            w_ref[ls, es] = (g * act[es, ls]).T.astype(bf16)
    o_ref[...] += _dot(w_ref[...], v_ref[...])

    @pl.when(c == pl.num_programs(1) - 1)
    def _():
        y = x_ref[...] + gt_ref[0] * o_ref[...]
        if final_norm:
            ms = jnp.mean(y * y, axis=-1, keepdims=True)
            y = y * lax.rsqrt(ms + EPS) * gf_ref[...]
        o_ref[...] = y


def peer_experts(xmt, e1t, e2t, th, u, v, x, gate, g_final, rows_per_batch, tm, te, final_norm):
    m, d = x.shape
    gt, gt_spec = _mod_specs(gate, m, tm, d, rows_per_batch)
    half = PEER_HEADS * N_KEYS
    return pl.pallas_call(
        functools.partial(_experts_kernel, final_norm=final_norm),
        grid=(m // tm, N_EXPERTS // te),
        in_specs=[
            pl.BlockSpec((d, tm), lambda i, c: (0, i)),
            pl.BlockSpec((te // N_KEYS // 8, PEER_HEADS, 8, tm), lambda i, c: (c, 0, 0, i)),
            pl.BlockSpec((half, tm), lambda i, c: (0, i)),
            pl.BlockSpec((PEER_HEADS, tm), lambda i, c: (0, i)),
            pl.BlockSpec((te, d), lambda i, c: (c, 0)),
            pl.BlockSpec((te, d), lambda i, c: (c, 0)),
            pl.BlockSpec((tm, d), lambda i, c: (i, 0)),
            gt_spec,
            pl.BlockSpec((1, d), lambda i, c: (0, 0)),
        ],
        out_specs=pl.BlockSpec((tm, d), lambda i, c: (i, 0)),
        out_shape=jax.ShapeDtypeStruct((m, d), f32),
        scratch_shapes=[pltpu.VMEM((tm, te), bf16)],
        compiler_params=_cparams("arbitrary", "arbitrary"),
        name="peer_experts",
    )(xmt, e1t, e2t, th, u, v, x, gt, g_final.reshape(1, d))


def peer_block(x, g, shift, scale, gate, wq_t, keys, u, v, g_final, rows_per_batch, tm, te, final_norm):
    qt, xmt = peer_query(x, g, shift, scale, wq_t, rows_per_batch, tm)
    e1t, e2t, th = peer_router(qt, keys)
    return peer_experts(xmt, e1t, e2t, th, u, v, x, gate, g_final, rows_per_batch, tm, te, final_norm)


def kernel(x_prompt, x_sample, c_prompt, c_sample, state_rglru_conv, state_rglru_h, cache_swa_k, cache_swa_v, cache_sb_k, cache_sb_v, state_mlstm_C, state_mlstm_n, state_mlstm_m, page_table, w_ada, b_ada, g_norm_mix, g_norm_ffn, e_w_in, e_conv_w, e_conv_b, e_w_r, e_b_r, e_w_i, e_b_i, e_lambda, e_w_out, o_w_in, o_b_if, o_sb_bias, o_g_mnorm, o_w_out, peer_w_q, peer_keys, peer_u, peer_v, g_final):
    bp, seq, d = x_prompt.shape
    bs = x_sample.shape[0]
    mp = bp * seq
    pad_s = LANES
    xp = x_prompt.reshape(mp, d)
    xs = x_sample.reshape(bs, d)

    c_rows = 16
    c_all = jnp.concatenate([c_prompt, c_sample, jnp.zeros((c_rows - bp - bs, d), f32)], axis=0)
    mod = adaln_all(c_all, w_ada, b_ada)

    def mods(layer):
        parts = [mod[layer, :, i * d:(i + 1) * d] for i in range(6)]
        return [p[:bp] for p in parts], [p[bp:bp + bs] for p in parts]

    ctab_p, stab_p = _rope_tables(jnp.arange(seq, dtype=jnp.int32))
    ctab_s, stab_s = _rope_tables(jnp.full((1,), PAST_LEN, jnp.int32))

    TM = 512
    m_p, m_s = mods(0)
    w_in = e_w_in[0].astype(bf16)
    w_out = e_w_out[0].astype(bf16)
    cw, cb = e_conv_w[0], e_conv_b[0].reshape(1, -1)
    wr, wi = e_w_r[0].astype(bf16), e_w_i[0].astype(bf16)
    br, bi, lam = e_b_r[0].reshape(1, -1), e_b_i[0].reshape(1, -1), e_lambda[0].reshape(1, -1)

    proj_p = mod_matmul(xp, g_norm_mix[0], m_p[0], m_p[1], w_in, E_IN, seq, 1024, 1024, name="e_in_p")
    proj_s = mod_matmul(xs, g_norm_mix[0], m_s[0], m_s[1], w_in, E_IN, 1, bs, 512, name="e_in_s")

    ya_p, h_p = rglru_prompt(proj_p, bp, seq, cw, cb, wr, br, wi, bi, lam)
    ya_s, h_s = rglru_step(proj_s, state_rglru_conv[0], state_rglru_h[0], cw, cb, wr, br, wi, bi, lam)
    conv_p = proj_p.reshape(bp, seq, E_IN)[:, seq - (CONV_W - 1):, :RG_WIDTH]
    conv_s = jnp.concatenate([state_rglru_conv[0][:, 1:], proj_s[:, None, :RG_WIDTH]], axis=1)

    q_p, k_p, v_p = rope_qkv(proj_p, ctab_p, stab_p, seq, 512)
    q_s, k_s, v_s = rope_qkv(proj_s, ctab_s, stab_s, 1, bs)
    o_p = dilated_prompt(q_p, k_p, v_p, bp, seq)
    win = cache_swa_k.shape[2]
    o_s = dilated_step(q_s, k_s, v_s, cache_swa_k.reshape(-1, win, HEADS, DH)[:bs],
                       cache_swa_v.reshape(-1, win, HEADS, DH)[:bs])
    wl = min(2048, seq)
    swa_k_p = k_p.reshape(bp, seq, HEADS, DH)[:, seq - wl:]
    swa_v_p = v_p.reshape(bp, seq, HEADS, DH)[:, seq - wl:]
    swa_k_s = k_s.reshape(bs, 1, HEADS, DH)
    swa_v_s = v_s.reshape(bs, 1, HEADS, DH)

    xp = out_proj(ya_p, o_p, w_out, xp, m_p[2], seq, 1024, 1024, name="e_out_p")
    xs = out_proj(ya_s, o_s, w_out, xs, m_s[2], 1, bs, 512, name="e_out_s")

    def peer_layer(layer, xp, xs, m_p, m_s, final_norm):
        wq_t = peer_w_q[layer].T.astype(bf16)
        u = peer_u[layer].astype(bf16)
        v = peer_v[layer].astype(bf16)
        xp = peer_block(xp, g_norm_ffn[layer], m_p[3], m_p[4], m_p[5], wq_t, peer_keys[layer], u, v,
                        g_final, seq, TM, 1024, final_norm)
        xs_pad = jnp.pad(xs, ((0, pad_s - bs), (0, 0)))
        xs_new = peer_block(xs_pad, g_norm_ffn[layer], m_s[3], m_s[4], m_s[5], wq_t, peer_keys[layer],
                            u, v, g_final, 1, pad_s, 1024, final_norm)
        return xp, xs_new[:bs]

    xp, xs = peer_layer(0, xp, xs, m_p, m_s, False)

    m_p, m_s = mods(1)
    w_in2 = o_w_in[0].astype(bf16)
    w_gate = jnp.pad(w_in2[:, O_MAIN:], ((0, 0), (0, LANES - 2 * M_HEADS)))
    w_out2 = o_w_out[0].astype(bf16)
    gate_bias = jnp.pad(o_b_if[0].reshape(1, 2 * M_HEADS), ((0, 0), (0, LANES - 2 * M_HEADS)))
    gmn = o_g_mnorm[0].reshape(1, M_WIDTH)

    proj2_p, gates_p = mod_matmul(xp, g_norm_mix[1], m_p[0], m_p[1], w_in2, O_MAIN, seq, 1024, 1024,
                                  w_gate=w_gate, name="o_in_p")
    proj2_s, gates_s = mod_matmul(xs, g_norm_mix[1], m_s[0], m_s[1], w_in2, O_MAIN, 1, bs, 512,
                                  w_gate=w_gate, name="o_in_s")

    oc_p = sb_prompt(proj2_p, o_sb_bias[0], bp, seq)
    oc_s = sb_step(proj2_s[:, :ATT_W], o_sb_bias[0], cache_sb_k.reshape(-1, PAGE, HEADS, DH),
                   cache_sb_v.reshape(-1, PAGE, HEADS, DH), page_table)
    hm_p, mC_p, mn_p, mm_p = mlstm_prompt(proj2_p, gates_p, gate_bias, gmn, bp, seq)
    hm_s, mC_s, mn_s, mm_s = mlstm_step(proj2_s, gates_s, gate_bias, gmn,
                                        state_mlstm_C.reshape(-1, M_HEADS, M_DH, M_DH)[:bs],
                                        state_mlstm_n[0], state_mlstm_m[0])

    p3 = proj2_p.reshape(bp, seq, O_MAIN)
    n_pg = seq // PAGE
    sb_k_p = p3[:, :, ATT_W:2 * ATT_W].reshape(bp, n_pg, PAGE, HEADS, DH)
    sb_v_p = p3[:, :, 2 * ATT_W:3 * ATT_W].reshape(bp, n_pg, PAGE, HEADS, DH)
    sb_k_s = proj2_s[:, ATT_W:2 * ATT_W].reshape(bs, 1, HEADS, DH)
    sb_v_s = proj2_s[:, 2 * ATT_W:3 * ATT_W].reshape(bs, 1, HEADS, DH)

    xp = out_proj(oc_p, hm_p, w_out2, xp, m_p[2], seq, 1024, 1024, name="o_out_p")
    xs = out_proj(oc_s, hm_s, w_out2, xs, m_s[2], 1, bs, 512, name="o_out_s")
    xp, xs = peer_layer(1, xp, xs, m_p, m_s, True)

    y_prompt = xp.reshape(bp, seq, d)
    y_sample = xs.reshape(bs, 1, d)
    st = lambda a: a[None]
    return (y_prompt, y_sample, st(conv_p), st(conv_s), st(h_p.reshape(bp, RG_WIDTH)), st(h_s),
            st(swa_k_p), st(swa_k_s), st(swa_v_p), st(swa_v_s),
            st(sb_k_p), st(sb_k_s), st(sb_v_p), st(sb_v_s),
            st(mC_p), st(mC_s), st(mn_p), st(mn_s), st(mm_p), st(mm_s))
```

```python
import functools
import math

import jax
import jax.numpy as jnp
from jax import lax
from jax.experimental import pallas as pl
from jax.experimental.pallas import tpu as pltpu

f32 = jnp.float32
bf16 = jnp.bfloat16

D_MODEL = 2048
PAST_LEN = 16384
PAGE = 128
RG_WIDTH = 1024
RG_BLOCKS = 8
CONV_W = 4
RG_C = 8.0
HEADS = 8
DH = 128
ATT_W = HEADS * DH
DIL_PATTERNS = ((128, 1), (512, 4), (2048, 16))
ROT_DIMS = 32
ROPE_THETA = 500000.0
M_HEADS = 4
M_DH = 256
M_WIDTH = M_HEADS * M_DH
CHUNK = 128
E_IN = 2 * RG_WIDTH + 3 * ATT_W
O_MAIN = 3 * ATT_W + 4 * M_WIDTH
PEER_HEADS = 8
N_KEYS = 128
N_EXPERTS = N_KEYS * N_KEYS
PEER_TOPK = 16
EPS = 1e-6
LANES = 128
VMEM_LIMIT = 56 * 1024 * 1024
NEG_INF = float("-inf")


def _cparams(*sem):
    return pltpu.CompilerParams(dimension_semantics=sem, vmem_limit_bytes=VMEM_LIMIT)


def _dot(a, b):
    return jnp.dot(a, b, preferred_element_type=f32)


def _dot_nt(a, b):
    return lax.dot_general(a, b, (((1,), (1,)), ((), ())), preferred_element_type=f32)


def _split3(x):
    hi = x.astype(bf16)
    r = x - hi.astype(f32)
    mid = r.astype(bf16)
    lo = (r - mid.astype(f32)).astype(bf16)
    return hi, mid, lo


def _sigmoid(x):
    return 1.0 / (1.0 + jnp.exp(-x))


def _log_sigmoid_pair(z):
    l1p = jnp.log1p(jnp.exp(-jnp.abs(z)))
    return jnp.minimum(z, 0.0) - l1p, -jnp.maximum(z, 0.0) - l1p


def _softplus(z):
    return jnp.maximum(z, 0.0) + jnp.log(1.0 + jnp.exp(-jnp.abs(z)))


def _gelu(x):
    c = math.sqrt(2.0 / math.pi)
    return 0.5 * x * (1.0 + jnp.tanh(c * (x + 0.044715 * (x * x * x))))


def _iota(shape, dim):
    return lax.broadcasted_iota(jnp.int32, shape, dim)


def _rowsum_bcast(x):
    ones = jnp.ones((LANES, LANES), bf16)
    hi = x.astype(bf16)
    lo = (x - hi.astype(f32)).astype(bf16)
    return _dot(hi, ones) + _dot(lo, ones)


def _adaln_kernel(c_ref, w_ref, b_ref, o_ref):
    c = c_ref[...]
    s = c * _sigmoid(c)
    w = w_ref[0]
    s_hi = s.astype(bf16)
    s_lo = (s - s_hi.astype(f32)).astype(bf16)
    w_hi = w.astype(bf16)
    w_lo = (w - w_hi.astype(f32)).astype(bf16)
    o_ref[0] = _dot(s_hi, w_hi) + _dot(s_hi, w_lo) + _dot(s_lo, w_hi) + b_ref[0]


def adaln_all(c_all, w_ada, b_ada):
    depth, d, n = w_ada.shape
    rows = c_all.shape[0]
    tn = 1024
    return pl.pallas_call(
        _adaln_kernel,
        grid=(depth, n // tn),
        in_specs=[
            pl.BlockSpec((rows, d), lambda l, j: (0, 0)),
            pl.BlockSpec((1, d, tn), lambda l, j: (l, 0, j)),
            pl.BlockSpec((1, 1, tn), lambda l, j: (l, 0, j)),
        ],
        out_specs=pl.BlockSpec((1, rows, tn), lambda l, j: (l, 0, j)),
        out_shape=jax.ShapeDtypeStruct((depth, rows, n), f32),
        compiler_params=_cparams("arbitrary", "arbitrary"),
        name="adaln",
    )(c_all, w_ada, b_ada.reshape(depth, 1, n))


def _modulated(x, g, sh, sc):
    ms = jnp.mean(x * x, axis=-1, keepdims=True)
    y = x * lax.rsqrt(ms + EPS) * g
    return y * (1.0 + sc) + sh


def _modmm_kernel(x_ref, g_ref, sh_ref, sc_ref, w_ref, o_ref, xn_ref):
    @pl.when(pl.program_id(1) == 0)
    def _():
        xn_ref[...] = _modulated(x_ref[...], g_ref[...], sh_ref[0], sc_ref[0]).astype(bf16)

    o_ref[...] = _dot(xn_ref[...], w_ref[...])


def _modmm_gate_kernel(x_ref, g_ref, sh_ref, sc_ref, w_ref, wg_ref, o_ref, og_ref, xn_ref):
    @pl.when(pl.program_id(1) == 0)
    def _():
        xn = _modulated(x_ref[...], g_ref[...], sh_ref[0], sc_ref[0]).astype(bf16)
        xn_ref[...] = xn
        og_ref[...] = _dot(xn, wg_ref[...])

    o_ref[...] = _dot(xn_ref[...], w_ref[...])


def _mod_specs(mod, m, tm, k, rows_per_batch):
    if rows_per_batch >= tm:
        assert rows_per_batch % tm == 0
        per = rows_per_batch // tm
        return mod.reshape(-1, 1, k), pl.BlockSpec((1, 1, k), lambda i, j: (i // per, 0, 0))
    assert rows_per_batch == 1
    rows = mod
    if rows.shape[0] < m:
        rows = jnp.pad(rows, ((0, m - rows.shape[0]), (0, 0)))
    return rows.reshape(1, m, k), pl.BlockSpec((1, tm, k), lambda i, j: (0, i, 0))


def mod_matmul(x, g, shift, scale, w, n_out, rows_per_batch, tm, tn, w_gate=None, name="modmm"):
    m, k = x.shape
    sh, sh_spec = _mod_specs(shift, m, tm, k, rows_per_batch)
    sc, sc_spec = _mod_specs(scale, m, tm, k, rows_per_batch)
    in_specs = [
        pl.BlockSpec((tm, k), lambda i, j: (i, 0)),
        pl.BlockSpec((1, k), lambda i, j: (0, 0)),
        sh_spec,
        sc_spec,
        pl.BlockSpec((k, tn), lambda i, j: (0, j)),
    ]
    args = [x, g.reshape(1, k), sh, sc, w]
    out_specs = pl.BlockSpec((tm, tn), lambda i, j: (i, j))
    out_shape = jax.ShapeDtypeStruct((m, n_out), f32)
    kern = _modmm_kernel
    if w_gate is not None:
        in_specs.append(pl.BlockSpec((k, LANES), lambda i, j: (0, 0)))
        args.append(w_gate)
        out_specs = [out_specs, pl.BlockSpec((tm, LANES), lambda i, j: (i, 0))]
        out_shape = [out_shape, jax.ShapeDtypeStruct((m, LANES), f32)]
        kern = _modmm_gate_kernel
    return pl.pallas_call(
        kern,
        grid=(m // tm, n_out // tn),
        in_specs=in_specs,
        out_specs=out_specs,
        out_shape=out_shape,
        scratch_shapes=[pltpu.VMEM((tm, k), bf16)],
        compiler_params=_cparams("arbitrary", "arbitrary"),
        name=name,
    )(*args)


def _outproj_kernel(a1_ref, a2_ref, w1_ref, w2_ref, x_ref, gt_ref, o_ref):
    y = _dot(a1_ref[...].astype(bf16), w1_ref[...]) + _dot(a2_ref[...].astype(bf16), w2_ref[...])
    o_ref[...] = x_ref[...] + gt_ref[0] * y


def out_proj(a1, a2, w, x, gate, rows_per_batch, tm, tn, name="outproj"):
    m, k1 = a1.shape
    k2 = a2.shape[1]
    n = w.shape[1]
    gt, gt_spec = _mod_specs(gate, m, tm, n, rows_per_batch)
    if gt.shape[1] == 1:
        per = rows_per_batch // tm
        gt_spec = pl.BlockSpec((1, 1, tn), lambda i, j: (i // per, 0, j))
    else:
        gt_spec = pl.BlockSpec((1, tm, tn), lambda i, j: (0, i, j))
    return pl.pallas_call(
        _outproj_kernel,
        grid=(m // tm, n // tn),
        in_specs=[
            pl.BlockSpec((tm, k1), lambda i, j: (i, 0)),
            pl.BlockSpec((tm, k2), lambda i, j: (i, 0)),
            pl.BlockSpec((k1, tn), lambda i, j: (0, j)),
            pl.BlockSpec((k2, tn), lambda i, j: (k1 // k2, j)),
            pl.BlockSpec((tm, tn), lambda i, j: (i, j)),
            gt_spec,
        ],
        out_specs=pl.BlockSpec((tm, tn), lambda i, j: (i, j)),
        out_shape=jax.ShapeDtypeStruct((m, n), f32),
        compiler_params=_cparams("arbitrary", "arbitrary"),
        name=name,
    )(a1, a2, w, w, x, gt)


def _rglru_gates(xc, wr_ref, br, wi_ref, bi, lam):
    xb = xc.astype(bf16)
    rs, gs = [], []
    for hb in range(RG_BLOCKS):
        sl = slice(hb * LANES, (hb + 1) * LANES)
        rs.append(_dot(xb[:, sl], wr_ref[hb]))
        gs.append(_dot(xb[:, sl], wi_ref[hb]))
    r = _sigmoid(jnp.concatenate(rs, axis=1) + br)
    ig = _sigmoid(jnp.concatenate(gs, axis=1) + bi)
    softplus_neg_lam = jnp.maximum(-lam, 0.0) + jnp.log1p(jnp.exp(-jnp.abs(lam)))
    log_a = -RG_C * r * softplus_neg_lam
    a = jnp.exp(log_a)
    u = jnp.sqrt(-jnp.tanh(log_a) * (a * a + 1.0)) * ig * xc
    return a, u


def _rglru_kernel(xa_ref, ga_ref, cw_ref, cb_ref, wr_ref, br_ref, wi_ref, bi_ref, lam_ref,
                  ya_ref, hl_ref, xprev_ref, hc_ref):
    t_idx = pl.program_id(1)
    tt = xa_ref.shape[0]

    @pl.when(t_idx == 0)
    def _():
        xprev_ref[...] = jnp.zeros_like(xprev_ref)
        hc_ref[...] = jnp.zeros_like(hc_ref)

    xa = xa_ref[...]
    xprev = xprev_ref[...]
    row8 = _iota((8, RG_WIDTH), 0)
    xc = cb_ref[...] + cw_ref[CONV_W - 1:CONV_W, :] * xa
    for k in range(1, CONV_W):
        rolled = pltpu.roll(xa, k, 0)
        head = jnp.where(row8 < k, pltpu.roll(xprev, k, 0), rolled[0:8])
        shifted = jnp.concatenate([head, rolled[8:]], axis=0)
        xc = xc + cw_ref[CONV_W - 1 - k:CONV_W - k, :] * shifted
    xprev_ref[...] = xa[tt - 8:tt]

    a, u = _rglru_gates(xc, wr_ref, br_ref[...], wi_ref, bi_ref[...], lam_ref[...])
    row = _iota((tt, RG_WIDTH), 0)
    s = 1
    while s < tt:
        a_sh = pltpu.roll(a, s, 0)
        u_sh = pltpu.roll(u, s, 0)
        ok = row >= s
        u = jnp.where(ok, a * u_sh + u, u)
        a = jnp.where(ok, a * a_sh, a)
        s *= 2
    h = a * hc_ref[...] + u
    hc_ref[...] = h[tt - 1:tt]
    hl_ref[0] = h[tt - 1:tt]
    ya_ref[...] = h * _gelu(ga_ref[...])


def rglru_prompt(proj, bsz, seq, cw, cb, wr, br, wi, bi, lam, tt=256):
    nt = seq // tt
    vec = lambda: pl.BlockSpec((1, RG_WIDTH), lambda b, t: (0, 0))
    return pl.pallas_call(
        _rglru_kernel,
        grid=(bsz, nt),
        in_specs=[
            pl.BlockSpec((tt, RG_WIDTH), lambda b, t: (b * nt + t, 0)),
            pl.BlockSpec((tt, RG_WIDTH), lambda b, t: (b * nt + t, 1)),
            pl.BlockSpec((CONV_W, RG_WIDTH), lambda b, t: (0, 0)),
            vec(),
            pl.BlockSpec((RG_BLOCKS, LANES, LANES), lambda b, t: (0, 0, 0)),
            vec(),
            pl.BlockSpec((RG_BLOCKS, LANES, LANES), lambda b, t: (0, 0, 0)),
            vec(),
            vec(),
        ],
        out_specs=[
            pl.BlockSpec((tt, RG_WIDTH), lambda b, t: (b * nt + t, 0)),
            pl.BlockSpec((1, 1, RG_WIDTH), lambda b, t: (b, 0, 0)),
        ],
        out_shape=[
            jax.ShapeDtypeStruct((bsz * seq, RG_WIDTH), f32),
            jax.ShapeDtypeStruct((bsz, 1, RG_WIDTH), f32),
        ],
        scratch_shapes=[pltpu.VMEM((8, RG_WIDTH), f32), pltpu.VMEM((1, RG_WIDTH), f32)],
        compiler_params=_cparams("arbitrary", "arbitrary"),
        name="rglru_prompt",
    )(proj, proj, cw, cb, wr, br, wi, bi, lam)


def _rglru_step_kernel(xa_ref, ga_ref, b0_ref, b1_ref, b2_ref, h0_ref, cw_ref, cb_ref,
                       wr_ref, br_ref, wi_ref, bi_ref, lam_ref, ya_ref, h_ref):
    xa = xa_ref[...]
    xc = (cb_ref[...] + cw_ref[0:1, :] * b0_ref[...] + cw_ref[1:2, :] * b1_ref[...]
          + cw_ref[2:3, :] * b2_ref[...] + cw_ref[3:4, :] * xa)
    a, u = _rglru_gates(xc, wr_ref, br_ref[...], wi_ref, bi_ref[...], lam_ref[...])
    h = a * h0_ref[...] + u
    h_ref[...] = h
    ya_ref[...] = h * _gelu(ga_ref[...])


def rglru_step(proj_s, conv_state, h0, cw, cb, wr, br, wi, bi, lam):
    n = proj_s.shape[0]
    full = lambda shape: pl.BlockSpec(shape, lambda i: tuple(0 for _ in shape))
    return pl.pallas_call(
        _rglru_step_kernel,
        grid=(1,),
        in_specs=[
            pl.BlockSpec((n, RG_WIDTH), lambda i: (0, 0)),
            pl.BlockSpec((n, RG_WIDTH), lambda i: (0, 1)),
            full((n, RG_WIDTH)), full((n, RG_WIDTH)), full((n, RG_WIDTH)), full((n, RG_WIDTH)),
            full((CONV_W, RG_WIDTH)), full((1, RG_WIDTH)),
            full((RG_BLOCKS, LANES, LANES)), full((1, RG_WIDTH)),
            full((RG_BLOCKS, LANES, LANES)), full((1, RG_WIDTH)), full((1, RG_WIDTH)),
        ],
        out_specs=[full((n, RG_WIDTH)), full((n, RG_WIDTH))],
        out_shape=[jax.ShapeDtypeStruct((n, RG_WIDTH), f32)] * 2,
        compiler_params=_cparams("arbitrary"),
        name="rglru_step",
    )(proj_s, proj_s, conv_state[:, 0], conv_state[:, 1], conv_state[:, 2], h0,
      cw, cb, wr, br, wi, bi, lam)


def _rope_tables(pos):
    half = ROT_DIMS // 2
    inv = ROPE_THETA ** (-jnp.arange(half, dtype=f32) / half)
    ang = pos.astype(f32)[:, None] * inv[None, :]
    cos, sin = jnp.cos(ang), jnp.sin(ang)
    n = pos.shape[0]
    ctab = jnp.concatenate([cos, cos, jnp.ones((n, DH - ROT_DIMS), f32)], axis=1)
    stab = jnp.concatenate([-sin, sin, jnp.zeros((n, DH - ROT_DIMS), f32)], axis=1)
    return ctab, stab


def _rope_head(xh, ctab, stab, lane):
    half = ROT_DIMS // 2
    partner = jnp.where(lane < half, pltpu.roll(xh, DH - half, 1), pltpu.roll(xh, half, 1))
    return xh * ctab + partner * stab


def _rope_kernel(q_ref, k_ref, v_ref, c_ref, s_ref, qo_ref, ko_ref, vo_ref):
    ctab, stab = c_ref[...], s_ref[...]
    lane = _iota((q_ref.shape[0], DH), 1)
    for hb in range(HEADS):
        sl = slice(hb * DH, (hb + 1) * DH)
        qo_ref[:, hb, :] = _rope_head(q_ref[:, sl], ctab, stab, lane) * (DH ** -0.5)
        ko_ref[:, hb, :] = _rope_head(k_ref[:, sl], ctab, stab, lane)
        vo_ref[:, hb, :] = v_ref[:, sl]


def rope_qkv(proj, ctab, stab, rows_per_seq, tt):
    m = proj.shape[0]
    nt = max(rows_per_seq // tt, 1)
    if ctab.shape[0] == 1:
        tab_spec = pl.BlockSpec((1, DH), lambda i: (0, 0))
    else:
        tab_spec = pl.BlockSpec((tt, DH), lambda i: (i % nt, 0))
    return pl.pallas_call(
        _rope_kernel,
        grid=(m // tt,),
        in_specs=[
            pl.BlockSpec((tt, ATT_W), lambda i: (i, 2)),
            pl.BlockSpec((tt, ATT_W), lambda i: (i, 3)),
            pl.BlockSpec((tt, ATT_W), lambda i: (i, 4)),
            tab_spec, tab_spec,
        ],
        out_specs=[pl.BlockSpec((tt, HEADS, DH), lambda i: (i, 0, 0))] * 3,
        out_shape=[jax.ShapeDtypeStruct((m, HEADS, DH), f32)] * 3,
        compiler_params=_cparams("arbitrary"),
        name="rope",
    )(proj, proj, proj, ctab, stab)


def _rope_split_kernel(q_ref, k_ref, v_ref, c_ref, s_ref, *refs):
    n_pat = len(DIL_PATTERNS)
    outs, (kf_ref, qs_ref, ks_ref) = refs[:3 * n_pat], refs[3 * n_pat:]
    tt = q_ref.shape[0]
    ctab, stab = c_ref[...], s_ref[...]
    lane = _iota((tt, DH), 1)
    qs_ref[...] = _rope_head(q_ref[...], ctab, stab, lane) * (DH ** -0.5)
    k_rot = _rope_head(k_ref[...], ctab, stab, lane)
    ks_ref[...] = k_rot
    kf_ref[...] = k_rot
    for gi, (_, d) in enumerate(DIL_PATTERNS):
        rows = tt // d
        for src, dst in ((qs_ref, outs[3 * gi]), (ks_ref, outs[3 * gi + 1]), (v_ref, outs[3 * gi + 2])):
            for r in range(d):
                dst[0, 0, r] = src[pl.ds(r, rows, stride=d), :].astype(bf16)


def rope_split(proj, ctab, stab, bsz, seq, tt=512):
    m = proj.shape[0]
    nt = seq // tt
    col = lambda c: pl.BlockSpec((tt, DH), lambda i, h, c=c: (i, c * HEADS + h))
    tab = pl.BlockSpec((tt, DH), lambda i, h: (i % nt, 0))
    out_specs, out_shape = [], []
    for (_, d) in DIL_PATTERNS:
        for _ in range(3):
            out_specs.append(pl.BlockSpec((1, 1, d, tt // d, DH), lambda i, h: (i // nt, h, 0, i % nt, 0)))
            out_shape.append(jax.ShapeDtypeStruct((bsz, HEADS, d, seq // d, DH), bf16))
    out_specs.append(pl.BlockSpec((tt, DH), lambda i, h: (i, h)))
    out_shape.append(jax.ShapeDtypeStruct((m, ATT_W), f32))
    res = pl.pallas_call(
        _rope_split_kernel,
        grid=(m // tt, HEADS),
        in_specs=[col(2), col(3), col(4), tab, tab],
        out_specs=out_specs,
        out_shape=out_shape,
        scratch_shapes=[pltpu.VMEM((tt, DH), f32), pltpu.VMEM((tt, DH), f32)],
        compiler_params=_cparams("arbitrary", "arbitrary"),
        name="rope_split",
    )(proj, proj, proj, ctab, stab)
    return [res[3 * gi:3 * gi + 3] for gi in range(len(DIL_PATTERNS))], res[-1]


def _dil_kernel(q_ref, kc_ref, kp_ref, vc_ref, vp_ref, o_ref, l_ref, s_ref, p_ref, *, span):
    tq = q_ref.shape[3]
    blk = pl.program_id(2)
    qi = _iota((tq, 2 * tq), 0)
    col = _iota((tq, 2 * tq), 1)
    rel = jnp.where(col < tq, qi - col, qi - col + 2 * tq)
    ok = (rel >= 0) & (rel <= span) & ((col < tq) | (blk > 0))
    for hb in range(HEADS):
        qh = q_ref[0, hb, 0]
        s_ref[hb, :, :tq] = _dot_nt(qh, kc_ref[0, hb, 0])
        s_ref[hb, :, tq:] = _dot_nt(qh, kp_ref[0, hb, 0])
    for hb in range(HEADS):
        s = jnp.where(ok, s_ref[hb], NEG_INF)
        mx = jnp.max(s, axis=1, keepdims=True)
        p = jnp.exp(s - mx)
        den = jnp.sum(p, axis=1, keepdims=True)
        p_ref[hb] = (p * (1.0 / den)).astype(bf16)
        l_ref[0, hb, 0] = jnp.broadcast_to(mx + jnp.log(den), (tq, DH))
    for hb in range(HEADS):
        o_ref[0, hb, 0] = (_dot(p_ref[hb, :, :tq], vc_ref[0, hb, 0])
                           + _dot(p_ref[hb, :, tq:], vp_ref[0, hb, 0]))


def _dil_merge_kernel(*refs):
    n_pat = len(DIL_PATTERNS)
    ins, o_ref, scr = refs[:2 * n_pat], refs[2 * n_pat], refs[2 * n_pat + 1:]
    tt = o_ref.shape[0]
    for gi, (_, d) in enumerate(DIL_PATTERNS):
        rows = tt // d
        for src, dst in ((ins[2 * gi], scr[2 * gi]), (ins[2 * gi + 1], scr[2 * gi + 1])):
            for r in range(d):
                dst[pl.ds(r, rows, stride=d), :] = src[0, 0, r]
    lses = [scr[2 * gi + 1][...] for gi in range(n_pat)]
    top = functools.reduce(jnp.maximum, lses)
    es = [jnp.exp(l - top) for l in lses]
    num = sum(scr[2 * gi][...] * es[gi] for gi in range(n_pat))
    o_ref[...] = num / sum(es)


def dilated_prompt(qkv_by_pattern, bsz, seq, tq=128, tt=512):
    partial = []
    for (w, d), (qd, kd, vd) in zip(DIL_PATTERNS, qkv_by_pattern):
        sd = seq // d
        blk = (1, HEADS, 1, tq, DH)
        cur = pl.BlockSpec(blk, lambda b, r, i: (b, 0, r, i, 0))
        prev = pl.BlockSpec(blk, lambda b, r, i: (b, 0, r, jnp.maximum(i - 1, 0), 0))
        partial += pl.pallas_call(
            functools.partial(_dil_kernel, span=w // d),
            grid=(bsz, d, sd // tq),
            in_specs=[cur, cur, prev, cur, prev],
            out_specs=[cur, cur],
            out_shape=[jax.ShapeDtypeStruct((bsz, HEADS, d, sd, DH), f32)] * 2,
            scratch_shapes=[pltpu.VMEM((HEADS, tq, 2 * tq), f32), pltpu.VMEM((HEADS, tq, 2 * tq), bf16)],
            compiler_params=_cparams("arbitrary", "arbitrary", "arbitrary"),
            name=f"dilattn_d{d}",
        )(qd, kd, kd, vd, vd)
    nt = seq // tt
    in_specs = []
    for (_, d) in DIL_PATTERNS:
        in_specs += [pl.BlockSpec((1, 1, d, tt // d, DH), lambda i, h: (i // nt, h, 0, i % nt, 0))] * 2
    return pl.pallas_call(
        _dil_merge_kernel,
        grid=(bsz * nt, HEADS),
        in_specs=in_specs,
        out_specs=pl.BlockSpec((tt, DH), lambda i, h: (i, h)),
        out_shape=jax.ShapeDtypeStruct((bsz * seq, ATT_W), f32),
        scratch_shapes=[pltpu.VMEM((tt, DH), f32)] * (2 * len(DIL_PATTERNS)),
        compiler_params=_cparams("arbitrary", "arbitrary"),
        name="dilattn_merge",
    )(*partial)


def _dil_step_kernel(q_ref, kn_ref, vn_ref, k1_ref, k4_ref, k16_ref, v1_ref, v4_ref, v16_ref, o_ref):
    q = q_ref[0]
    kn, vn = kn_ref[0], vn_ref[0]
    s_self = _rowsum_bcast(q * kn)
    o_gs, lse_gs = [], []
    for k_ref, v_ref in ((k1_ref, v1_ref), (k4_ref, v4_ref), (k16_ref, v16_ref)):
        k3 = k_ref[0, :, 0]
        nk = k3.shape[0]
        s = _rowsum_bcast((k3 * q[None]).reshape(nk * HEADS, DH)).reshape(nk, HEADS, DH)
        mx = jnp.maximum(jnp.max(s, axis=0), s_self)
        p = jnp.exp(s - mx[None])
        p_self = jnp.exp(s_self - mx)
        den = jnp.sum(p, axis=0) + p_self
        num = jnp.sum(p * v_ref[0, :, 0], axis=0) + p_self * vn
        o_gs.append(num / den)
        lse_gs.append(mx + jnp.log(den))
    top = jnp.maximum(jnp.maximum(lse_gs[0], lse_gs[1]), lse_gs[2])
    es = [jnp.exp(l - top) for l in lse_gs]
    tot = es[0] + es[1] + es[2]
    o_ref[0] = (o_gs[0] * es[0] + o_gs[1] * es[1] + o_gs[2] * es[2]) / tot


def dilated_step(q4, k4, v4, cache_k, cache_v):
    n, win = cache_k.shape[0], cache_k.shape[1]
    one = pl.BlockSpec((1, HEADS, DH), lambda b: (b, 0, 0))
    args = [q4, k4, v4]
    in_specs = [one, one, one]
    for cache in (cache_k, cache_v):
        for (w, d) in DIL_PATTERNS:
            nkeys = w // d
            assert win % d == 0 and (win // d) % nkeys == 0 and (win - w) % (d * nkeys) == 0
            args.append(cache.reshape(n, win // d, d, HEADS, DH))
            in_specs.append(pl.BlockSpec((1, nkeys, 1, HEADS, DH),
                                         lambda b, blk=(win - w) // d // nkeys: (b, blk, 0, 0, 0)))
    return pl.pallas_call(
        _dil_step_kernel,
        grid=(n,),
        in_specs=in_specs,
        out_specs=one,
        out_shape=jax.ShapeDtypeStruct((n, HEADS, DH), f32),
        compiler_params=_cparams("arbitrary"),
        name="dilattn_step",
    )(*args).reshape(n, ATT_W)


def _sb_kernel(bias_ref, q_ref, k_ref, v_ref, o_ref, kb_ref, vb_ref, qs_ref, t_ref, spb_ref, wb_ref,
               acc_ref, run_ref):
    tq = q_ref.shape[1]
    nh = q_ref.shape[2] // DH
    kt_w = t_ref.shape[2]
    sub = PAGE
    hg = pl.program_id(1)
    qb = pl.program_id(2)

    @pl.when(qb == 0)
    def _():
        kb_ref[...] = k_ref[0].astype(bf16)
        vb_ref[...] = v_ref[0].astype(bf16)

    rr = _iota((sub, 2 * sub), 0)
    cc = _iota((sub, 2 * sub), 1)
    tri = jnp.where((rr > cc) | (cc >= sub), 1.0, 0.0).astype(bf16)
    for h in range(nh):
        qs_ref[h] = (q_ref[0, :, h * DH:(h + 1) * DH] * (DH ** -0.5)).astype(bf16)
    acc_ref[...] = jnp.zeros_like(acc_ref)
    run_ref[...] = jnp.zeros_like(run_ref)

    def macro(start, masked):
        for h in range(nh):
            hs = slice(h * DH, (h + 1) * DH)
            z = _dot_nt(qs_ref[h], kb_ref[pl.ds(start, kt_w), hs]) + bias_ref[hg * nh + h]
            sp = _softplus(z)
            t = z - sp
            if masked:
                ok = start + _iota((tq, kt_w), 1) < qb * tq + _iota((tq, kt_w), 0)
                sp = jnp.where(ok, sp, 0.0)
                t = jnp.where(ok, t, NEG_INF)
            t_ref[h] = t
            spb_ref[h] = sp.astype(bf16)
        for h in range(nh):
            run = run_ref[h]
            for kt in range(kt_w // sub - 1, -1, -1):
                ks = slice(kt * sub, (kt + 1) * sub)
                cs = _dot(spb_ref[h, :, ks], tri)
                wb_ref[h, :, ks] = jnp.exp(t_ref[h, :, ks] - cs[:, :sub] - run).astype(bf16)
                run = run + cs[:, sub:]
            run_ref[h] = run
        for h in range(nh):
            hs = slice(h * DH, (h + 1) * DH)
            acc_ref[h] += _dot(wb_ref[h], vb_ref[pl.ds(start, kt_w), hs])

    top = (qb * tq) // kt_w
    macro(pl.multiple_of(top * kt_w, kt_w), True)

    def body(it, _):
        macro(pl.multiple_of((top - 1 - it) * kt_w, kt_w), False)
        return 0

    lax.fori_loop(0, top, body, 0)
    for h in range(nh):
        o_ref[0, :, h * DH:(h + 1) * DH] = acc_ref[h]


def sb_prompt(proj2, bias, bsz, seq, tq=256, nh=4, kt_w=512):
    assert kt_w % tq == 0 and seq % kt_w == 0
    p3 = proj2.reshape(bsz, seq, proj2.shape[1])
    nq = seq // tq
    ng = HEADS // nh
    wd = nh * DH
    return pl.pallas_call(
        _sb_kernel,
        grid=(bsz, ng, nq),
        in_specs=[
            pl.BlockSpec(memory_space=pltpu.SMEM),
            pl.BlockSpec((1, tq, wd), lambda b, g, i: (b, i, g)),
            pl.BlockSpec((1, seq, wd), lambda b, g, i: (b, 0, ng + g)),
            pl.BlockSpec((1, seq, wd), lambda b, g, i: (b, 0, 2 * ng + g)),
        ],
        out_specs=pl.BlockSpec((1, tq, wd), lambda b, g, i: (b, i, g)),
        out_shape=jax.ShapeDtypeStruct((bsz, seq, ATT_W), f32),
        scratch_shapes=[
            pltpu.VMEM((seq, wd), bf16), pltpu.VMEM((seq, wd), bf16),
            pltpu.VMEM((nh, tq, DH), bf16),
            pltpu.VMEM((nh, tq, kt_w), f32), pltpu.VMEM((nh, tq, kt_w), bf16),
            pltpu.VMEM((nh, tq, kt_w), bf16),
            pltpu.VMEM((nh, tq, DH), f32), pltpu.VMEM((nh, tq, PAGE), f32),
        ],
        compiler_params=_cparams("arbitrary", "arbitrary", "arbitrary"),
        name="sb_prompt",
    )(bias, p3, p3, p3).reshape(bsz * seq, ATT_W)


def _sb_step_kernel(pt_ref, q_ref, bias_ref, *refs, npg):
    k_refs, v_refs = refs[:npg], refs[npg:2 * npg]
    o_ref, acc_ref, carry_ref = refs[2 * npg:]
    j = pl.program_id(1)

    @pl.when(j == 0)
    def _():
        acc_ref[...] = jnp.zeros_like(acc_ref)
        carry_ref[...] = jnp.zeros_like(carry_ref)

    q = q_ref[0]
    bias = bias_ref[...]
    acc = acc_ref[...]
    run = carry_ref[...]
    for p in range(npg):
        k3 = k_refs[p][0]
        z = _rowsum_bcast((k3 * q[None]).reshape(PAGE * HEADS, DH)).reshape(PAGE, HEADS, DH) + bias[None]
        sp = _softplus(z)
        later = []
        for key in range(PAGE - 1, -1, -1):
            later.append(run)
            run = run + sp[key]
        suffix = jnp.stack(later[::-1], axis=0)
        w = jnp.exp(z - sp - suffix)
        acc = acc + jnp.sum(w * v_refs[p][0], axis=0)
    acc_ref[...] = acc
    carry_ref[...] = run

    @pl.when(j == pl.num_programs(1) - 1)
    def _():
        o_ref[0] = acc


def sb_step(q_s, bias, cache_k, cache_v, page_table, npg=4):
    n, n_pages = page_table.shape
    q4 = (q_s * (DH ** -0.5)).reshape(n, HEADS, DH)
    bias4 = jnp.broadcast_to(bias[:, None], (HEADS, DH))

    def page(p):
        return lambda b, j, pt: (pt[b, n_pages - 1 - (j * npg + p)], 0, 0, 0)

    kv_specs = [pl.BlockSpec((1, PAGE, HEADS, DH), page(p)) for p in range(npg)]
    grid_spec = pltpu.PrefetchScalarGridSpec(
        num_scalar_prefetch=1,
        grid=(n, n_pages // npg),
        in_specs=[
            pl.BlockSpec((1, HEADS, DH), lambda b, j, pt: (b, 0, 0)),
            pl.BlockSpec((HEADS, DH), lambda b, j, pt: (0, 0)),
        ] + kv_specs + kv_specs,
        out_specs=pl.BlockSpec((1, HEADS, DH), lambda b, j, pt: (b, 0, 0)),
        scratch_shapes=[pltpu.VMEM((HEADS, DH), f32), pltpu.VMEM((HEADS, DH), f32)],
    )
    return pl.pallas_call(
        functools.partial(_sb_step_kernel, npg=npg),
        grid_spec=grid_spec,
        out_shape=jax.ShapeDtypeStruct((n, HEADS, DH), f32),
        compiler_params=_cparams("arbitrary", "arbitrary"),
        name="sb_step",
    )(page_table, q4, bias4, *([cache_k] * npg), *([cache_v] * npg)).reshape(n, ATT_W)


def _mlstm_kernel(q_ref, k_ref, v_ref, og_ref, gate_ref, gb_ref, gn_ref,
                  h_ref, c_out, n_out, m_out, c_s, n_s, m_s):
    ci = pl.program_id(1)
    L = CHUNK

    @pl.when(ci == 0)
    def _():
        c_s[...] = jnp.zeros_like(c_s)
        n_s[...] = jnp.zeros_like(n_s)
        m_s[...] = jnp.zeros_like(m_s)

    gt = gate_ref[...] + gb_ref[...]
    gt_t = gt.T
    ri = _iota((L, L), 0)
    li = _iota((L, L), 1)
    causal = li <= ri
    tri_incl = jnp.where(causal, 1.0, 0.0).astype(bf16)
    tri_incl_t = jnp.where(ri <= li, 1.0, 0.0).astype(bf16)
    for h in range(M_HEADS):
        sl = slice(h * M_DH, (h + 1) * M_DH)
        ig_col = gt[:, h:h + 1]
        ig_row = gt_t[h:h + 1, :]
        lf_col = _log_sigmoid_pair(gt[:, M_HEADS + h:M_HEADS + h + 1])[0]
        lf_row = _log_sigmoid_pair(gt_t[M_HEADS + h:M_HEADS + h + 1, :])[0]
        c_hi, c_mid, c_lo = _split3(jnp.broadcast_to(lf_col, (L, L)))
        bcum_col = _dot(tri_incl, c_hi) + _dot(tri_incl, c_mid) + _dot(tri_incl, c_lo)
        r_hi, r_mid, r_lo = _split3(jnp.broadcast_to(lf_row, (L, L)))
        bcum_row = _dot(r_hi, tri_incl_t) + _dot(r_mid, tri_incl_t) + _dot(r_lo, tri_incl_t)
        m_prev = m_s[h:h + 1, :]
        dlog = jnp.where(causal, bcum_col - bcum_row + ig_row, NEG_INF)
        inter = bcum_col + m_prev
        m_t = jnp.maximum(inter, jnp.max(dlog, axis=1, keepdims=True))
        dw = jnp.exp(dlog - m_t)
        iw = jnp.exp(inter - m_t)
        qh = q_ref[:, sl]
        kh = k_ref[:, sl] * (M_DH ** -0.5)
        vh = v_ref[:, sl]
        qb, kb, vb = qh.astype(bf16), kh.astype(bf16), vh.astype(bf16)
        sw = dw * _dot_nt(qb, kb)
        c_prev = c_s[h]
        n_prev = n_s[h:h + 1, :]
        iw_col = iw[:, 0:1]
        num = _dot(sw.astype(bf16), vb) + iw_col * _dot_nt(qb, c_prev.astype(bf16))
        qn = jnp.sum(qb.astype(f32) * n_prev.astype(bf16).astype(f32), axis=1, keepdims=True)
        den = jnp.sum(sw, axis=1, keepdims=True) + iw_col * qn
        m_col = m_t[:, 0:1]
        hout = num / jnp.maximum(jnp.abs(den), jnp.exp(-m_col))
        m_last = m_t[L - 1:L, :]
        b_last = bcum_col[L - 1:L, :]
        wl_col = jnp.exp(b_last[:, 0:1] - bcum_col[:, 0:1] + ig_col - m_last[:, 0:1])
        wl_row = jnp.exp(b_last - bcum_row[0:1, :] + ig_row - m_last)
        decay = jnp.exp(b_last + m_prev - m_last)
        dsc = decay[:, 0:1]
        c_s[h] = dsc * c_prev + _dot((vh * wl_col).T.astype(bf16), kb)
        wl8 = jnp.broadcast_to(wl_row, (8, L)).astype(bf16)
        n_s[h:h + 1, :] = dsc * n_prev + _dot(wl8, kb)[0:1, :]
        m_s[h:h + 1, :] = m_last
        hn = hout * lax.rsqrt(jnp.mean(hout * hout, axis=1, keepdims=True) + EPS)
        h_ref[:, sl] = hn * gn_ref[:, sl] * _sigmoid(og_ref[:, sl])

    @pl.when(ci == pl.num_programs(1) - 1)
    def _():
        c_out[0] = c_s[...]
        n_out[0] = n_s[...]
        m_out[0] = m_s[...]


def mlstm_prompt(proj2, gates, gate_bias, g_mnorm, bsz, seq):
    nc = seq // CHUNK
    col = lambda c: pl.BlockSpec((CHUNK, M_WIDTH), lambda b, i, c=c: (b * nc + i, c))
    hm, c1, n1, m1 = pl.pallas_call(
        _mlstm_kernel,
        grid=(bsz, nc),
        in_specs=[
            col(3), col(4), col(5), col(6),
            pl.BlockSpec((CHUNK, LANES), lambda b, i: (b * nc + i, 0)),
            pl.BlockSpec((1, LANES), lambda b, i: (0, 0)),
            pl.BlockSpec((1, M_WIDTH), lambda b, i: (0, 0)),
        ],
        out_specs=[
            pl.BlockSpec((CHUNK, M_WIDTH), lambda b, i: (b * nc + i, 0)),
            pl.BlockSpec((1, M_HEADS, M_DH, M_DH), lambda b, i: (b, 0, 0, 0)),
            pl.BlockSpec((1, M_HEADS, M_DH), lambda b, i: (b, 0, 0)),
            pl.BlockSpec((1, M_HEADS, LANES), lambda b, i: (b, 0, 0)),
        ],
        out_shape=[
            jax.ShapeDtypeStruct((bsz * seq, M_WIDTH), f32),
            jax.ShapeDtypeStruct((bsz, M_HEADS, M_DH, M_DH), f32),
            jax.ShapeDtypeStruct((bsz, M_HEADS, M_DH), f32),
            jax.ShapeDtypeStruct((bsz, M_HEADS, LANES), f32),
        ],
        scratch_shapes=[
            pltpu.VMEM((M_HEADS, M_DH, M_DH), f32),
            pltpu.VMEM((M_HEADS, M_DH), f32),
            pltpu.VMEM((M_HEADS, LANES), f32),
        ],
        compiler_params=_cparams("arbitrary", "arbitrary"),
        name="mlstm_prompt",
    )(proj2, proj2, proj2, proj2, gates, gate_bias, g_mnorm)
    return hm, c1, n1, m1[:, :, 0]


def _mlstm_step_kernel(q_ref, k_ref, v_ref, og_ref, gn_ref, ig_ref, fg_ref, c_ref, n_ref, m_ref,
                       h_ref, c_out, n_out, m_out):
    q = q_ref[0, 0]
    k = k_ref[0, 0] * (M_DH ** -0.5)
    v = v_ref[0, 0]
    ig = ig_ref[0, 0]
    lf = _log_sigmoid_pair(fg_ref[0, 0])[0]
    m0 = m_ref[0, 0]
    c0 = c_ref[0, 0]
    n0 = n_ref[0, 0]
    inter = lf + m0
    m_t = jnp.maximum(inter, ig)
    dw = jnp.exp(ig - m_t)
    iw = jnp.exp(inter - m_t)
    rnd = lambda a: a.astype(bf16).astype(f32)
    qr, kr, vr = rnd(q), rnd(k), rnd(v)
    qk = jnp.sum(qr * kr, axis=1, keepdims=True)
    sw = dw * qk
    cq = jnp.sum(rnd(c0) * qr, axis=1, keepdims=True)
    num = rnd(sw) * vr + iw * cq
    den = sw + iw * jnp.sum(rnd(n0) * qr, axis=1, keepdims=True)
    hout = num / jnp.maximum(jnp.abs(den), jnp.exp(-m_t))
    c_out[0, 0] = iw * c0 + rnd(dw * v) * kr
    n_out[0, 0] = iw * n0 + rnd(dw) * kr
    m_out[0, 0] = m_t
    hn = hout * lax.rsqrt(jnp.mean(hout * hout, axis=0, keepdims=True) + EPS)
    h_ref[0, 0] = hn * gn_ref[0] * _sigmoid(og_ref[0, 0])


def mlstm_step(proj2_s, gates_s, gate_bias, g_mnorm, c0, n0, m0):
    n = proj2_s.shape[0]
    base = 3 * ATT_W
    seg = lambda i: proj2_s[:, base + i * M_WIDTH: base + (i + 1) * M_WIDTH].reshape(n, M_HEADS, M_DH)
    g = gates_s + gate_bias
    rowb = pl.BlockSpec((1, 1, 1, M_DH), lambda b, h: (b, h, 0, 0))
    colb = pl.BlockSpec((1, 1, M_DH, 1), lambda b, h: (b, h, 0, 0))
    scal = pl.BlockSpec((1, 1, 1, 1), lambda b, h: (b, h, 0, 0))
    hcol, c1, n1, m1 = pl.pallas_call(
        _mlstm_step_kernel,
        grid=(n, M_HEADS),
        in_specs=[
            rowb, rowb, colb, colb,
            pl.BlockSpec((1, M_DH, 1), lambda b, h: (h, 0, 0)),
            scal, scal,
            pl.BlockSpec((1, 1, M_DH, M_DH), lambda b, h: (b, h, 0, 0)),
            rowb, scal,
        ],
        out_specs=[colb, pl.BlockSpec((1, 1, M_DH, M_DH), lambda b, h: (b, h, 0, 0)), rowb, scal],
        out_shape=[
            jax.ShapeDtypeStruct((n, M_HEADS, M_DH, 1), f32),
            jax.ShapeDtypeStruct((n, M_HEADS, M_DH, M_DH), f32),
            jax.ShapeDtypeStruct((n, M_HEADS, 1, M_DH), f32),
            jax.ShapeDtypeStruct((n, M_HEADS, 1, 1), f32),
        ],
        compiler_params=_cparams("arbitrary", "arbitrary"),
        name="mlstm_step",
    )(seg(0)[:, :, None, :], seg(1)[:, :, None, :], seg(2)[..., None], seg(3)[..., None],
      g_mnorm.reshape(M_HEADS, M_DH, 1),
      g[:, 0:M_HEADS].reshape(n, M_HEADS, 1, 1), g[:, M_HEADS:2 * M_HEADS].reshape(n, M_HEADS, 1, 1),
      c0, n0[:, :, None, :], m0.reshape(n, M_HEADS, 1, 1))
    return hcol.reshape(n, M_WIDTH), c1, n1.reshape(n, M_HEADS, M_DH), m1.reshape(n, M_HEADS)


def _peerq_kernel(x_ref, g_ref, sh_ref, sc_ref, wt_ref, qt_ref, xmt_ref, xs_ref):
    @pl.when(pl.program_id(1) == 0)
    def _():
        xm = _modulated(x_ref[...], g_ref[...], sh_ref[0], sc_ref[0])
        xt = xm.T.astype(bf16)
        xs_ref[...] = xt
        xmt_ref[...] = xt

    qt_ref[...] = _dot(wt_ref[...], xs_ref[...])


def peer_query(x, g, shift, scale, wq_t, rows_per_batch, tm, tn=512):
    m, k = x.shape
    n = wq_t.shape[0]
    sh, sh_spec = _mod_specs(shift, m, tm, k, rows_per_batch)
    sc, sc_spec = _mod_specs(scale, m, tm, k, rows_per_batch)
    return pl.pallas_call(
        _peerq_kernel,
        grid=(m // tm, n // tn),
        in_specs=[
            pl.BlockSpec((tm, k), lambda i, j: (i, 0)),
            pl.BlockSpec((1, k), lambda i, j: (0, 0)),
            sh_spec, sc_spec,
            pl.BlockSpec((tn, k), lambda i, j: (j, 0)),
        ],
        out_specs=[pl.BlockSpec((tn, tm), lambda i, j: (j, i)),
                   pl.BlockSpec((k, tm), lambda i, j: (0, i))],
        out_shape=[jax.ShapeDtypeStruct((n, m), f32), jax.ShapeDtypeStruct((k, m), bf16)],
        scratch_shapes=[pltpu.VMEM((k, tm), bf16)],
        compiler_params=_cparams("arbitrary", "arbitrary"),
        name="peer_query",
    )(x, g.reshape(1, k), sh, sc, wq_t)


_N_TOP = PEER_TOPK + 1
_CAND_PAIRS = [(a, b) for a in range(_N_TOP) for b in range(_N_TOP) if (a + 1) * (b + 1) <= _N_TOP]
_N_CAND = -(-len(_CAND_PAIRS) // 8) * 8


def _extract_top(cur, ridx, n):
    vals = []
    big = float(cur.shape[0])
    for _ in range(n):
        mx = jnp.max(cur, axis=0, keepdims=True)
        first = jnp.min(jnp.where(cur == mx, ridx, big), axis=0, keepdims=True)
        cur = jnp.where(ridx == first, NEG_INF, cur)
        vals.append(mx)
    return vals


_ROUTER_UNROLL = 4


def _router_kernel(qt_ref, keys_ref, e1_ref, e2_ref, th_ref, cand_ref):
    ridx = _iota((N_KEYS, LANES), 0).astype(f32)
    cidx = _iota((_N_CAND, LANES), 0).astype(f32)
    k0 = keys_ref[0].astype(bf16)
    k1 = keys_ref[1].astype(bf16)
    cand_ref[...] = jnp.full(cand_ref.shape, NEG_INF, f32)

    def one_head(h, slot):
        r0 = pl.multiple_of(h * 2 * N_KEYS, 2 * N_KEYS)
        s1 = _dot(k0, qt_ref[pl.ds(r0, N_KEYS), :].astype(bf16))
        s2 = _dot(k1, qt_ref[pl.ds(r0 + N_KEYS, N_KEYS), :].astype(bf16))
        top_a = _extract_top(s1, ridx, _N_TOP)
        top_b = _extract_top(s2, ridx, _N_TOP)
        for ci, (a, b) in enumerate(_CAND_PAIRS):
            cand_ref[slot, ci:ci + 1, :] = top_a[a] + top_b[b]
        cs = _extract_top(cand_ref[slot], cidx, _N_TOP)
        z = jnp.ones_like(cs[0])
        for r in range(1, PEER_TOPK):
            z = z + jnp.exp(cs[r] - cs[0])
        inv_z = 1.0 / z
        mid = 0.5 * (cs[PEER_TOPK - 1] + cs[PEER_TOPK])
        o0 = pl.multiple_of(h * N_KEYS, N_KEYS)
        e1 = jnp.exp(s1 - top_a[0]) * inv_z
        e1_ref[:, h] = e1.reshape(N_KEYS // 8, 8, LANES)
        e2_ref[pl.ds(o0, N_KEYS), :] = jnp.exp(s2 - top_b[0])
        th_ref[pl.ds(h, 1), :] = jnp.exp(mid - cs[0]) * inv_z

    def body(hp, _):
        for slot in range(_ROUTER_UNROLL):
            one_head(hp * _ROUTER_UNROLL + slot, slot)
        return 0

    lax.fori_loop(0, PEER_HEADS // _ROUTER_UNROLL, body, 0)


def peer_router(qt, keys):
    n, m = qt.shape
    half = PEER_HEADS * N_KEYS
    return pl.pallas_call(
        _router_kernel,
        grid=(m // LANES,),
        in_specs=[pl.BlockSpec((n, LANES), lambda i: (0, i)),
                  pl.BlockSpec((2, N_KEYS, N_KEYS), lambda i: (0, 0, 0))],
        out_specs=[pl.BlockSpec((N_KEYS // 8, PEER_HEADS, 8, LANES), lambda i: (0, 0, 0, i)),
                   pl.BlockSpec((half, LANES), lambda i: (0, i)),
                   pl.BlockSpec((PEER_HEADS, LANES), lambda i: (0, i))],
        out_shape=[jax.ShapeDtypeStruct((N_KEYS // 8, PEER_HEADS, 8, m), f32),
                   jax.ShapeDtypeStruct((half, m), f32),
                   jax.ShapeDtypeStruct((PEER_HEADS, m), f32)],
        scratch_shapes=[pltpu.VMEM((_ROUTER_UNROLL, _N_CAND, LANES), f32)],
        compiler_params=_cparams("arbitrary"),
        name="peer_router",
    )(qt, keys)


def _experts_kernel(xmt_ref, e1_ref, e2_ref, th_ref, u_ref, v_ref, x_ref, gt_ref, gf_ref,
                    o_ref, w_ref, *, final_norm):
    c = pl.program_id(1)
    tm, te = w_ref.shape
    ni = te // N_KEYS

    @pl.when(c == 0)
    def _():
        o_ref[...] = jnp.zeros_like(o_ref)

    act = _gelu(_dot(u_ref[...], xmt_ref[...]))
    for ii in range(ni):
        es = slice(ii * N_KEYS, (ii + 1) * N_KEYS)
        for lc in range(tm // LANES):
            ls = slice(lc * LANES, (lc + 1) * LANES)
            g = jnp.zeros((N_KEYS, LANES), f32)
            for h in range(PEER_HEADS):
                row = e1_ref[ii // 8, h, ii % 8:ii % 8 + 1, ls]
                pr = e2_ref[h * N_KEYS:(h + 1) * N_KEYS, ls] * row
                g = g + jnp.where(pr >= th_ref[h:h + 1, ls], pr, 0.0)
            w_ref[ls, es] = (g * act[es, ls]).T.astype(bf16)
    o_ref[...] += _dot(w_ref[...], v_ref[...])

    @pl.when(c == pl.num_programs(1) - 1)
    def _():
        y = x_ref[...] + gt_ref[0] * o_ref[...]
        if final_norm:
            ms = jnp.mean(y * y, axis=-1, keepdims=True)
            y = y * lax.rsqrt(ms + EPS) * gf_ref[...]
        o_ref[...] = y


def peer_experts(xmt, e1t, e2t, th, u, v, x, gate, g_final, rows_per_batch, tm, te, final_norm):
    m, d = x.shape
    gt, gt_spec = _mod_specs(gate, m, tm, d, rows_per_batch)
    half = PEER_HEADS * N_KEYS
    return pl.pallas_call(
        functools.partial(_experts_kernel, final_norm=final_norm),
        grid=(m // tm, N_EXPERTS // te),
        in_specs=[
            pl.BlockSpec((d, tm), lambda i, c: (0, i)),
            pl.BlockSpec((te // N_KEYS // 8, PEER_HEADS, 8, tm), lambda i, c: (c, 0, 0, i)),
            pl.BlockSpec((half, tm), lambda i, c: (0, i)),
            pl.BlockSpec((PEER_HEADS, tm), lambda i, c: (0, i)),
            pl.BlockSpec((te, d), lambda i, c: (c, 0)),
            pl.BlockSpec((te, d), lambda i, c: (c, 0)),
            pl.BlockSpec((tm, d), lambda i, c: (i, 0)),
            gt_spec,
            pl.BlockSpec((1, d), lambda i, c: (0, 0)),
        ],
        out_specs=pl.BlockSpec((tm, d), lambda i, c: (i, 0)),
        out_shape=jax.ShapeDtypeStruct((m, d), f32),
        scratch_shapes=[pltpu.VMEM((tm, te), bf16)],
        compiler_params=_cparams("arbitrary", "arbitrary"),
        name="peer_experts",
    )(xmt, e1t, e2t, th, u, v, x, gt, g_final.reshape(1, d))


def peer_block(x, g, shift, scale, gate, wq_t, keys, u, v, g_final, rows_per_batch, tm, te, final_norm):
    qt, xmt = peer_query(x, g, shift, scale, wq_t, rows_per_batch, tm)
    e1t, e2t, th = peer_router(qt, keys)
    return peer_experts(xmt, e1t, e2t, th, u, v, x, gate, g_final, rows_per_batch, tm, te, final_norm)


def kernel(x_prompt, x_sample, c_prompt, c_sample, state_rglru_conv, state_rglru_h, cache_swa_k, cache_swa_v, cache_sb_k, cache_sb_v, state_mlstm_C, state_mlstm_n, state_mlstm_m, page_table, w_ada, b_ada, g_norm_mix, g_norm_ffn, e_w_in, e_conv_w, e_conv_b, e_w_r, e_b_r, e_w_i, e_b_i, e_lambda, e_w_out, o_w_in, o_b_if, o_sb_bias, o_g_mnorm, o_w_out, peer_w_q, peer_keys, peer_u, peer_v, g_final):
    bp, seq, d = x_prompt.shape
    bs = x_sample.shape[0]
    mp = bp * seq
    pad_s = LANES
    xp = x_prompt.reshape(mp, d)
    xs = x_sample.reshape(bs, d)

    c_rows = 16
    c_all = jnp.concatenate([c_prompt, c_sample, jnp.zeros((c_rows - bp - bs, d), f32)], axis=0)
    mod = adaln_all(c_all, w_ada, b_ada)

    def mods(layer):
        parts = [mod[layer, :, i * d:(i + 1) * d] for i in range(6)]
        return [p[:bp] for p in parts], [p[bp:bp + bs] for p in parts]

    ctab_p, stab_p = _rope_tables(jnp.arange(seq, dtype=jnp.int32))
    ctab_s, stab_s = _rope_tables(jnp.full((1,), PAST_LEN, jnp.int32))

    TM = 512
    m_p, m_s = mods(0)
    w_in = e_w_in[0].astype(bf16)
    w_out = e_w_out[0].astype(bf16)
    cw, cb = e_conv_w[0], e_conv_b[0].reshape(1, -1)
    wr, wi = e_w_r[0].astype(bf16), e_w_i[0].astype(bf16)
    br, bi, lam = e_b_r[0].reshape(1, -1), e_b_i[0].reshape(1, -1), e_lambda[0].reshape(1, -1)

    proj_p = mod_matmul(xp, g_norm_mix[0], m_p[0], m_p[1], w_in, E_IN, seq, 1024, 1024, name="e_in_p")
    proj_s = mod_matmul(xs, g_norm_mix[0], m_s[0], m_s[1], w_in, E_IN, 1, bs, 512, name="e_in_s")

    ya_p, h_p = rglru_prompt(proj_p, bp, seq, cw, cb, wr, br, wi, bi, lam)
    ya_s, h_s = rglru_step(proj_s, state_rglru_conv[0], state_rglru_h[0], cw, cb, wr, br, wi, bi, lam)
    conv_p = proj_p.reshape(bp, seq, E_IN)[:, seq - (CONV_W - 1):, :RG_WIDTH]
    conv_s = jnp.concatenate([state_rglru_conv[0][:, 1:], proj_s[:, None, :RG_WIDTH]], axis=1)

    qkv_p, k_p = rope_split(proj_p, ctab_p, stab_p, bp, seq)
    q_s, k_s, v_s = rope_qkv(proj_s, ctab_s, stab_s, 1, bs)
    o_p = dilated_prompt(qkv_p, bp, seq)
    win = cache_swa_k.shape[2]
    o_s = dilated_step(q_s, k_s, v_s, cache_swa_k.reshape(-1, win, HEADS, DH)[:bs],
                       cache_swa_v.reshape(-1, win, HEADS, DH)[:bs])
    wl = min(2048, seq)
    swa_k_p = k_p.reshape(bp, seq, ATT_W)[:, seq - wl:].reshape(bp, wl, HEADS, DH)
    swa_v_p = proj_p.reshape(bp, seq, E_IN)[:, seq - wl:, E_IN - ATT_W:].reshape(bp, wl, HEADS, DH)
    swa_k_s = k_s.reshape(bs, 1, HEADS, DH)
    swa_v_s = v_s.reshape(bs, 1, HEADS, DH)

    xp = out_proj(ya_p, o_p, w_out, xp, m_p[2], seq, 1024, 1024, name="e_out_p")
    xs = out_proj(ya_s, o_s, w_out, xs, m_s[2], 1, bs, 512, name="e_out_s")

    def peer_layer(layer, xp, xs, m_p, m_s, final_norm):
        wq_t = peer_w_q[layer].T.astype(bf16)
        u = peer_u[layer].astype(bf16)
        v = peer_v[layer].astype(bf16)
        xp = peer_block(xp, g_norm_ffn[layer], m_p[3], m_p[4], m_p[5], wq_t, peer_keys[layer], u, v,
                        g_final, seq, TM, 1024, final_norm)
        xs_pad = jnp.pad(xs, ((0, pad_s - bs), (0, 0)))
        xs_new = peer_block(xs_pad, g_norm_ffn[layer], m_s[3], m_s[4], m_s[5], wq_t, peer_keys[layer],
                            u, v, g_final, 1, pad_s, 1024, final_norm)
        return xp, xs_new[:bs]

    xp, xs = peer_layer(0, xp, xs, m_p, m_s, False)

    m_p, m_s = mods(1)
    w_in2 = o_w_in[0].astype(bf16)
    w_gate = jnp.pad(w_in2[:, O_MAIN:], ((0, 0), (0, LANES - 2 * M_HEADS)))
    w_out2 = o_w_out[0].astype(bf16)
    gate_bias = jnp.pad(o_b_if[0].reshape(1, 2 * M_HEADS), ((0, 0), (0, LANES - 2 * M_HEADS)))
    gmn = o_g_mnorm[0].reshape(1, M_WIDTH)

    proj2_p, gates_p = mod_matmul(xp, g_norm_mix[1], m_p[0], m_p[1], w_in2, O_MAIN, seq, 1024, 1024,
                                  w_gate=w_gate, name="o_in_p")
    proj2_s, gates_s = mod_matmul(xs, g_norm_mix[1], m_s[0], m_s[1], w_in2, O_MAIN, 1, bs, 512,
                                  w_gate=w_gate, name="o_in_s")

    oc_p = sb_prompt(proj2_p, o_sb_bias[0], bp, seq)
    oc_s = sb_step(proj2_s[:, :ATT_W], o_sb_bias[0], cache_sb_k.reshape(-1, PAGE, HEADS, DH),
                   cache_sb_v.reshape(-1, PAGE, HEADS, DH), page_table)
    hm_p, mC_p, mn_p, mm_p = mlstm_prompt(proj2_p, gates_p, gate_bias, gmn, bp, seq)
    hm_s, mC_s, mn_s, mm_s = mlstm_step(proj2_s, gates_s, gate_bias, gmn,
                                        state_mlstm_C.reshape(-1, M_HEADS, M_DH, M_DH)[:bs],
                                        state_mlstm_n[0], state_mlstm_m[0])

    p3 = proj2_p.reshape(bp, seq, O_MAIN)
    n_pg = seq // PAGE
    sb_k_p = p3[:, :, ATT_W:2 * ATT_W].reshape(bp, n_pg, PAGE, HEADS, DH)
    sb_v_p = p3[:, :, 2 * ATT_W:3 * ATT_W].reshape(bp, n_pg, PAGE, HEADS, DH)
    sb_k_s = proj2_s[:, ATT_W:2 * ATT_W].reshape(bs, 1, HEADS, DH)
    sb_v_s = proj2_s[:, 2 * ATT_W:3 * ATT_W].reshape(bs, 1, HEADS, DH)

    xp = out_proj(oc_p, hm_p, w_out2, xp, m_p[2], seq, 1024, 1024, name="o_out_p")
    xs = out_proj(oc_s, hm_s, w_out2, xs, m_s[2], 1, bs, 512, name="o_out_s")
    xp, xs = peer_layer(1, xp, xs, m_p, m_s, True)

    y_prompt = xp.reshape(bp, seq, d)
    y_sample = xs.reshape(bs, 1, d)
    st = lambda a: a[None]
    return (y_prompt, y_sample, st(conv_p), st(conv_s), st(h_p.reshape(bp, RG_WIDTH)), st(h_s),
            st(swa_k_p), st(swa_k_s), st(swa_v_p), st(swa_v_s),
            st(sb_k_p), st(sb_k_s), st(sb_v_p), st(sb_v_s),
            st(mC_p), st(mC_s), st(mn_p), st(mn_s), st(mm_p), st(mm_s))
```

```python
import functools
import math

import jax
import jax.numpy as jnp
from jax import lax
from jax.experimental import pallas as pl
from jax.experimental.pallas import tpu as pltpu

f32 = jnp.float32
bf16 = jnp.bfloat16

D_MODEL = 2048
PAST_LEN = 16384
PAGE = 128
RG_WIDTH = 1024
RG_BLOCKS = 8
CONV_W = 4
RG_C = 8.0
HEADS = 8
DH = 128
ATT_W = HEADS * DH
DIL_PATTERNS = ((128, 1), (512, 4), (2048, 16))
ROT_DIMS = 32
ROPE_THETA = 500000.0
M_HEADS = 4
M_DH = 256
M_WIDTH = M_HEADS * M_DH
CHUNK = 128
E_IN = 2 * RG_WIDTH + 3 * ATT_W
O_MAIN = 3 * ATT_W + 4 * M_WIDTH
PEER_HEADS = 8
N_KEYS = 128
N_EXPERTS = N_KEYS * N_KEYS
PEER_TOPK = 16
EPS = 1e-6
LANES = 128
VMEM_LIMIT = 56 * 1024 * 1024
NEG_INF = float("-inf")


def _cparams(*sem):
    return pltpu.CompilerParams(dimension_semantics=sem, vmem_limit_bytes=VMEM_LIMIT)


def _dot(a, b):
    return jnp.dot(a, b, preferred_element_type=f32)


def _dot_nt(a, b):
    return lax.dot_general(a, b, (((1,), (1,)), ((), ())), preferred_element_type=f32)


def _split3(x):
    hi = x.astype(bf16)
    r = x - hi.astype(f32)
    mid = r.astype(bf16)
    lo = (r - mid.astype(f32)).astype(bf16)
    return hi, mid, lo


def _sigmoid(x):
    return 1.0 / (1.0 + jnp.exp(-x))


def _log_sigmoid_pair(z):
    l1p = jnp.log1p(jnp.exp(-jnp.abs(z)))
    return jnp.minimum(z, 0.0) - l1p, -jnp.maximum(z, 0.0) - l1p


def _softplus(z):
    return jnp.maximum(z, 0.0) + jnp.log(1.0 + jnp.exp(-jnp.abs(z)))


def _gelu(x):
    c = math.sqrt(2.0 / math.pi)
    return 0.5 * x * (1.0 + jnp.tanh(c * (x + 0.044715 * (x * x * x))))


def _iota(shape, dim):
    return lax.broadcasted_iota(jnp.int32, shape, dim)


def _rowsum_bcast(x):
    ones = jnp.ones((LANES, LANES), bf16)
    hi = x.astype(bf16)
    lo = (x - hi.astype(f32)).astype(bf16)
    return _dot(hi, ones) + _dot(lo, ones)


def _adaln_kernel(c_ref, w_ref, b_ref, o_ref):
    c = c_ref[...]
    s = c * _sigmoid(c)
    w = w_ref[0]
    s_hi = s.astype(bf16)
    s_lo = (s - s_hi.astype(f32)).astype(bf16)
    w_hi = w.astype(bf16)
    w_lo = (w - w_hi.astype(f32)).astype(bf16)
    o_ref[0] = _dot(s_hi, w_hi) + _dot(s_hi, w_lo) + _dot(s_lo, w_hi) + b_ref[0]


def adaln_all(c_all, w_ada, b_ada):
    depth, d, n = w_ada.shape
    rows = c_all.shape[0]
    tn = 1024
    return pl.pallas_call(
        _adaln_kernel,
        grid=(depth, n // tn),
        in_specs=[
            pl.BlockSpec((rows, d), lambda l, j: (0, 0)),
            pl.BlockSpec((1, d, tn), lambda l, j: (l, 0, j)),
            pl.BlockSpec((1, 1, tn), lambda l, j: (l, 0, j)),
        ],
        out_specs=pl.BlockSpec((1, rows, tn), lambda l, j: (l, 0, j)),
        out_shape=jax.ShapeDtypeStruct((depth, rows, n), f32),
        compiler_params=_cparams("arbitrary", "arbitrary"),
        name="adaln",
    )(c_all, w_ada, b_ada.reshape(depth, 1, n))


def _modulated(x, g, sh, sc):
    ms = jnp.mean(x * x, axis=-1, keepdims=True)
    y = x * lax.rsqrt(ms + EPS) * g
    return y * (1.0 + sc) + sh


def _modmm_kernel(x_ref, g_ref, sh_ref, sc_ref, w_ref, o_ref, xn_ref):
    @pl.when(pl.program_id(1) == 0)
    def _():
        xn_ref[...] = _modulated(x_ref[...], g_ref[...], sh_ref[0], sc_ref[0]).astype(bf16)

    o_ref[...] = _dot(xn_ref[...], w_ref[...])


def _modmm_gate_kernel(x_ref, g_ref, sh_ref, sc_ref, w_ref, wg_ref, o_ref, og_ref, xn_ref):
    @pl.when(pl.program_id(1) == 0)
    def _():
        xn = _modulated(x_ref[...], g_ref[...], sh_ref[0], sc_ref[0]).astype(bf16)
        xn_ref[...] = xn
        og_ref[...] = _dot(xn, wg_ref[...])

    o_ref[...] = _dot(xn_ref[...], w_ref[...])


def _mod_specs(mod, m, tm, k, rows_per_batch):
    if rows_per_batch >= tm:
        assert rows_per_batch % tm == 0
        per = rows_per_batch // tm
        return mod.reshape(-1, 1, k), pl.BlockSpec((1, 1, k), lambda i, j: (i // per, 0, 0))
    assert rows_per_batch == 1
    rows = mod
    if rows.shape[0] < m:
        rows = jnp.pad(rows, ((0, m - rows.shape[0]), (0, 0)))
    return rows.reshape(1, m, k), pl.BlockSpec((1, tm, k), lambda i, j: (0, i, 0))


def mod_matmul(x, g, shift, scale, w, n_out, rows_per_batch, tm, tn, w_gate=None, name="modmm"):
    m, k = x.shape
    sh, sh_spec = _mod_specs(shift, m, tm, k, rows_per_batch)
    sc, sc_spec = _mod_specs(scale, m, tm, k, rows_per_batch)
    in_specs = [
        pl.BlockSpec((tm, k), lambda i, j: (i, 0)),
        pl.BlockSpec((1, k), lambda i, j: (0, 0)),
        sh_spec,
        sc_spec,
        pl.BlockSpec((k, tn), lambda i, j: (0, j)),
    ]
    args = [x, g.reshape(1, k), sh, sc, w]
    out_specs = pl.BlockSpec((tm, tn), lambda i, j: (i, j))
    out_shape = jax.ShapeDtypeStruct((m, n_out), f32)
    kern = _modmm_kernel
    if w_gate is not None:
        in_specs.append(pl.BlockSpec((k, LANES), lambda i, j: (0, 0)))
        args.append(w_gate)
        out_specs = [out_specs, pl.BlockSpec((tm, LANES), lambda i, j: (i, 0))]
        out_shape = [out_shape, jax.ShapeDtypeStruct((m, LANES), f32)]
        kern = _modmm_gate_kernel
    return pl.pallas_call(
        kern,
        grid=(m // tm, n_out // tn),
        in_specs=in_specs,
        out_specs=out_specs,
        out_shape=out_shape,
        scratch_shapes=[pltpu.VMEM((tm, k), bf16)],
        compiler_params=_cparams("arbitrary", "arbitrary"),
        name=name,
    )(*args)


def _outproj_kernel(a1_ref, a2_ref, w1_ref, w2_ref, x_ref, gt_ref, o_ref):
    y = _dot(a1_ref[...].astype(bf16), w1_ref[...]) + _dot(a2_ref[...].astype(bf16), w2_ref[...])
    o_ref[...] = x_ref[...] + gt_ref[0] * y


def out_proj(a1, a2, w, x, gate, rows_per_batch, tm, tn, name="outproj"):
    m, k1 = a1.shape
    k2 = a2.shape[1]
    n = w.shape[1]
    gt, gt_spec = _mod_specs(gate, m, tm, n, rows_per_batch)
    if gt.shape[1] == 1:
        per = rows_per_batch // tm
        gt_spec = pl.BlockSpec((1, 1, tn), lambda i, j: (i // per, 0, j))
    else:
        gt_spec = pl.BlockSpec((1, tm, tn), lambda i, j: (0, i, j))
    return pl.pallas_call(
        _outproj_kernel,
        grid=(m // tm, n // tn),
        in_specs=[
            pl.BlockSpec((tm, k1), lambda i, j: (i, 0)),
            pl.BlockSpec((tm, k2), lambda i, j: (i, 0)),
            pl.BlockSpec((k1, tn), lambda i, j: (0, j)),
            pl.BlockSpec((k2, tn), lambda i, j: (k1 // k2, j)),
            pl.BlockSpec((tm, tn), lambda i, j: (i, j)),
            gt_spec,
        ],
        out_specs=pl.BlockSpec((tm, tn), lambda i, j: (i, j)),
        out_shape=jax.ShapeDtypeStruct((m, n), f32),
        compiler_params=_cparams("arbitrary", "arbitrary"),
        name=name,
    )(a1, a2, w, w, x, gt)


def _rglru_gates(xc, wr_ref, br, wi_ref, bi, lam):
    xb = xc.astype(bf16)
    rs, gs = [], []
    for hb in range(RG_BLOCKS):
        sl = slice(hb * LANES, (hb + 1) * LANES)
        rs.append(_dot(xb[:, sl], wr_ref[hb]))
        gs.append(_dot(xb[:, sl], wi_ref[hb]))
    r = _sigmoid(jnp.concatenate(rs, axis=1) + br)
    ig = _sigmoid(jnp.concatenate(gs, axis=1) + bi)
    softplus_neg_lam = jnp.maximum(-lam, 0.0) + jnp.log1p(jnp.exp(-jnp.abs(lam)))
    log_a = -RG_C * r * softplus_neg_lam
    a = jnp.exp(log_a)
    u = jnp.sqrt(-jnp.tanh(log_a) * (a * a + 1.0)) * ig * xc
    return a, u


def _rglru_kernel(xa_ref, ga_ref, cw_ref, cb_ref, wr_ref, br_ref, wi_ref, bi_ref, lam_ref,
                  ya_ref, hl_ref, xprev_ref, hc_ref):
    t_idx = pl.program_id(1)
    tt = xa_ref.shape[0]

    @pl.when(t_idx == 0)
    def _():
        xprev_ref[...] = jnp.zeros_like(xprev_ref)
        hc_ref[...] = jnp.zeros_like(hc_ref)

    xa = xa_ref[...]
    xprev = xprev_ref[...]
    row8 = _iota((8, RG_WIDTH), 0)
    xc = cb_ref[...] + cw_ref[CONV_W - 1:CONV_W, :] * xa
    for k in range(1, CONV_W):
        rolled = pltpu.roll(xa, k, 0)
        head = jnp.where(row8 < k, pltpu.roll(xprev, k, 0), rolled[0:8])
        shifted = jnp.concatenate([head, rolled[8:]], axis=0)
        xc = xc + cw_ref[CONV_W - 1 - k:CONV_W - k, :] * shifted
    xprev_ref[...] = xa[tt - 8:tt]

    a, u = _rglru_gates(xc, wr_ref, br_ref[...], wi_ref, bi_ref[...], lam_ref[...])
    row = _iota((tt, RG_WIDTH), 0)
    s = 1
    while s < tt:
        a_sh = pltpu.roll(a, s, 0)
        u_sh = pltpu.roll(u, s, 0)
        ok = row >= s
        u = jnp.where(ok, a * u_sh + u, u)
        a = jnp.where(ok, a * a_sh, a)
        s *= 2
    h = a * hc_ref[...] + u
    hc_ref[...] = h[tt - 1:tt]
    hl_ref[0] = h[tt - 1:tt]
    ya_ref[...] = h * _gelu(ga_ref[...])


def rglru_prompt(proj, bsz, seq, cw, cb, wr, br, wi, bi, lam, tt=256):
    nt = seq // tt
    vec = lambda: pl.BlockSpec((1, RG_WIDTH), lambda b, t: (0, 0))
    return pl.pallas_call(
        _rglru_kernel,
        grid=(bsz, nt),
        in_specs=[
            pl.BlockSpec((tt, RG_WIDTH), lambda b, t: (b * nt + t, 0)),
            pl.BlockSpec((tt, RG_WIDTH), lambda b, t: (b * nt + t, 1)),
            pl.BlockSpec((CONV_W, RG_WIDTH), lambda b, t: (0, 0)),
            vec(),
            pl.BlockSpec((RG_BLOCKS, LANES, LANES), lambda b, t: (0, 0, 0)),
            vec(),
            pl.BlockSpec((RG_BLOCKS, LANES, LANES), lambda b, t: (0, 0, 0)),
            vec(),
            vec(),
        ],
        out_specs=[
            pl.BlockSpec((tt, RG_WIDTH), lambda b, t: (b * nt + t, 0)),
            pl.BlockSpec((1, 1, RG_WIDTH), lambda b, t: (b, 0, 0)),
        ],
        out_shape=[
            jax.ShapeDtypeStruct((bsz * seq, RG_WIDTH), f32),
            jax.ShapeDtypeStruct((bsz, 1, RG_WIDTH), f32),
        ],
        scratch_shapes=[pltpu.VMEM((8, RG_WIDTH), f32), pltpu.VMEM((1, RG_WIDTH), f32)],
        compiler_params=_cparams("arbitrary", "arbitrary"),
        name="rglru_prompt",
    )(proj, proj, cw, cb, wr, br, wi, bi, lam)


def _rglru_step_kernel(xa_ref, ga_ref, b0_ref, b1_ref, b2_ref, h0_ref, cw_ref, cb_ref,
                       wr_ref, br_ref, wi_ref, bi_ref, lam_ref, ya_ref, h_ref):
    xa = xa_ref[...]
    xc = (cb_ref[...] + cw_ref[0:1, :] * b0_ref[...] + cw_ref[1:2, :] * b1_ref[...]
          + cw_ref[2:3, :] * b2_ref[...] + cw_ref[3:4, :] * xa)
    a, u = _rglru_gates(xc, wr_ref, br_ref[...], wi_ref, bi_ref[...], lam_ref[...])
    h = a * h0_ref[...] + u
    h_ref[...] = h
    ya_ref[...] = h * _gelu(ga_ref[...])


def rglru_step(proj_s, conv_state, h0, cw, cb, wr, br, wi, bi, lam):
    n = proj_s.shape[0]
    full = lambda shape: pl.BlockSpec(shape, lambda i: tuple(0 for _ in shape))
    return pl.pallas_call(
        _rglru_step_kernel,
        grid=(1,),
        in_specs=[
            pl.BlockSpec((n, RG_WIDTH), lambda i: (0, 0)),
            pl.BlockSpec((n, RG_WIDTH), lambda i: (0, 1)),
            full((n, RG_WIDTH)), full((n, RG_WIDTH)), full((n, RG_WIDTH)), full((n, RG_WIDTH)),
            full((CONV_W, RG_WIDTH)), full((1, RG_WIDTH)),
            full((RG_BLOCKS, LANES, LANES)), full((1, RG_WIDTH)),
            full((RG_BLOCKS, LANES, LANES)), full((1, RG_WIDTH)), full((1, RG_WIDTH)),
        ],
        out_specs=[full((n, RG_WIDTH)), full((n, RG_WIDTH))],
        out_shape=[jax.ShapeDtypeStruct((n, RG_WIDTH), f32)] * 2,
        compiler_params=_cparams("arbitrary"),
        name="rglru_step",
    )(proj_s, proj_s, conv_state[:, 0], conv_state[:, 1], conv_state[:, 2], h0,
      cw, cb, wr, br, wi, bi, lam)


def _rope_tables(pos):
    half = ROT_DIMS // 2
    inv = ROPE_THETA ** (-jnp.arange(half, dtype=f32) / half)
    ang = pos.astype(f32)[:, None] * inv[None, :]
    cos, sin = jnp.cos(ang), jnp.sin(ang)
    n = pos.shape[0]
    ctab = jnp.concatenate([cos, cos, jnp.ones((n, DH - ROT_DIMS), f32)], axis=1)
    stab = jnp.concatenate([-sin, sin, jnp.zeros((n, DH - ROT_DIMS), f32)], axis=1)
    return ctab, stab


def _rope_head(xh, ctab, stab, lane):
    half = ROT_DIMS // 2
    partner = jnp.where(lane < half, pltpu.roll(xh, DH - half, 1), pltpu.roll(xh, half, 1))
    return xh * ctab + partner * stab


def _rope_kernel(q_ref, k_ref, v_ref, c_ref, s_ref, qo_ref, ko_ref, vo_ref):
    ctab, stab = c_ref[...], s_ref[...]
    lane = _iota((q_ref.shape[0], DH), 1)
    for hb in range(HEADS):
        sl = slice(hb * DH, (hb + 1) * DH)
        qo_ref[:, hb, :] = _rope_head(q_ref[:, sl], ctab, stab, lane) * (DH ** -0.5)
        ko_ref[:, hb, :] = _rope_head(k_ref[:, sl], ctab, stab, lane)
        vo_ref[:, hb, :] = v_ref[:, sl]


def rope_qkv(proj, ctab, stab, rows_per_seq, tt):
    m = proj.shape[0]
    nt = max(rows_per_seq // tt, 1)
    if ctab.shape[0] == 1:
        tab_spec = pl.BlockSpec((1, DH), lambda i: (0, 0))
    else:
        tab_spec = pl.BlockSpec((tt, DH), lambda i: (i % nt, 0))
    return pl.pallas_call(
        _rope_kernel,
        grid=(m // tt,),
        in_specs=[
            pl.BlockSpec((tt, ATT_W), lambda i: (i, 2)),
            pl.BlockSpec((tt, ATT_W), lambda i: (i, 3)),
            pl.BlockSpec((tt, ATT_W), lambda i: (i, 4)),
            tab_spec, tab_spec,
        ],
        out_specs=[pl.BlockSpec((tt, HEADS, DH), lambda i: (i, 0, 0))] * 3,
        out_shape=[jax.ShapeDtypeStruct((m, HEADS, DH), f32)] * 3,
        compiler_params=_cparams("arbitrary"),
        name="rope",
    )(proj, proj, proj, ctab, stab)


def _rope_split_kernel(q_ref, k_ref, v_ref, c_ref, s_ref, *refs):
    n_pat = len(DIL_PATTERNS)
    outs, (kf_ref, qs_ref, ks_ref) = refs[:3 * n_pat], refs[3 * n_pat:]
    tt = q_ref.shape[0]
    ctab, stab = c_ref[...], s_ref[...]
    lane = _iota((tt, DH), 1)
    qs_ref[...] = _rope_head(q_ref[...], ctab, stab, lane) * (DH ** -0.5)
    k_rot = _rope_head(k_ref[...], ctab, stab, lane)
    ks_ref[...] = k_rot
    kf_ref[...] = k_rot
    for gi, (_, d) in enumerate(DIL_PATTERNS):
        rows = tt // d
        for src, dst in ((qs_ref, outs[3 * gi]), (ks_ref, outs[3 * gi + 1]), (v_ref, outs[3 * gi + 2])):
            for r in range(d):
                dst[0, 0, r] = src[pl.ds(r, rows, stride=d), :].astype(bf16)


def rope_split(proj, ctab, stab, bsz, seq, tt=512):
    m = proj.shape[0]
    nt = seq // tt
    col = lambda c: pl.BlockSpec((tt, DH), lambda i, h, c=c: (i, c * HEADS + h))
    tab = pl.BlockSpec((tt, DH), lambda i, h: (i % nt, 0))
    out_specs, out_shape = [], []
    for (_, d) in DIL_PATTERNS:
        for _ in range(3):
            out_specs.append(pl.BlockSpec((1, 1, d, tt // d, DH), lambda i, h: (i // nt, h, 0, i % nt, 0)))
            out_shape.append(jax.ShapeDtypeStruct((bsz, HEADS, d, seq // d, DH), bf16))
    out_specs.append(pl.BlockSpec((tt, DH), lambda i, h: (i, h)))
    out_shape.append(jax.ShapeDtypeStruct((m, ATT_W), f32))
    res = pl.pallas_call(
        _rope_split_kernel,
        grid=(m // tt, HEADS),
        in_specs=[col(2), col(3), col(4), tab, tab],
        out_specs=out_specs,
        out_shape=out_shape,
        scratch_shapes=[pltpu.VMEM((tt, DH), f32), pltpu.VMEM((tt, DH), f32)],
        compiler_params=_cparams("arbitrary", "arbitrary"),
        name="rope_split",
    )(proj, proj, proj, ctab, stab)
    return [res[3 * gi:3 * gi + 3] for gi in range(len(DIL_PATTERNS))], res[-1]


def _dil_kernel(q_ref, kc_ref, kp_ref, vc_ref, vp_ref, o_ref, l_ref, s_ref, p_ref, *, span):
    tq = q_ref.shape[3]
    blk = pl.program_id(2)
    qi = _iota((tq, 2 * tq), 0)
    col = _iota((tq, 2 * tq), 1)
    rel = jnp.where(col < tq, qi - col, qi - col + 2 * tq)
    ok = (rel >= 0) & (rel <= span) & ((col < tq) | (blk > 0))
    for hb in range(HEADS):
        qh = q_ref[0, hb, 0]
        s_ref[hb, :, :tq] = _dot_nt(qh, kc_ref[0, hb, 0])
        s_ref[hb, :, tq:] = _dot_nt(qh, kp_ref[0, hb, 0])
    for hb in range(HEADS):
        s = jnp.where(ok, s_ref[hb], NEG_INF)
        mx = jnp.max(s, axis=1, keepdims=True)
        p = jnp.exp(s - mx)
        den = jnp.sum(p, axis=1, keepdims=True)
        p_ref[hb] = (p * (1.0 / den)).astype(bf16)
        l_ref[0, hb, 0] = jnp.broadcast_to(mx + jnp.log(den), (tq, DH))
    for hb in range(HEADS):
        o_ref[0, hb, 0] = (_dot(p_ref[hb, :, :tq], vc_ref[0, hb, 0])
                           + _dot(p_ref[hb, :, tq:], vp_ref[0, hb, 0]))


def _dil_merge_kernel(*refs):
    n_pat = len(DIL_PATTERNS)
    ins, o_ref, scr = refs[:2 * n_pat], refs[2 * n_pat], refs[2 * n_pat + 1:]
    tt = o_ref.shape[0]
    for gi, (_, d) in enumerate(DIL_PATTERNS):
        rows = tt // d
        for src, dst in ((ins[2 * gi], scr[2 * gi]), (ins[2 * gi + 1], scr[2 * gi + 1])):
            for r in range(d):
                dst[pl.ds(r, rows, stride=d), :] = src[0, 0, r]
    lses = [scr[2 * gi + 1][...] for gi in range(n_pat)]
    top = functools.reduce(jnp.maximum, lses)
    es = [jnp.exp(l - top) for l in lses]
    num = sum(scr[2 * gi][...] * es[gi] for gi in range(n_pat))
    o_ref[...] = num / sum(es)


def dilated_prompt(qkv_by_pattern, bsz, seq, tq=128, tt=512):
    partial = []
    for (w, d), (qd, kd, vd) in zip(DIL_PATTERNS, qkv_by_pattern):
        sd = seq // d
        blk = (1, HEADS, 1, tq, DH)
        cur = pl.BlockSpec(blk, lambda b, r, i: (b, 0, r, i, 0))
        prev = pl.BlockSpec(blk, lambda b, r, i: (b, 0, r, jnp.maximum(i - 1, 0), 0))
        partial += pl.pallas_call(
            functools.partial(_dil_kernel, span=w // d),
            grid=(bsz, d, sd // tq),
            in_specs=[cur, cur, prev, cur, prev],
            out_specs=[cur, cur],
            out_shape=[jax.ShapeDtypeStruct((bsz, HEADS, d, sd, DH), f32)] * 2,
            scratch_shapes=[pltpu.VMEM((HEADS, tq, 2 * tq), f32), pltpu.VMEM((HEADS, tq, 2 * tq), bf16)],
            compiler_params=_cparams("arbitrary", "arbitrary", "arbitrary"),
            name=f"dilattn_d{d}",
        )(qd, kd, kd, vd, vd)
    nt = seq // tt
    in_specs = []
    for (_, d) in DIL_PATTERNS:
        in_specs += [pl.BlockSpec((1, 1, d, tt // d, DH), lambda i, h: (i // nt, h, 0, i % nt, 0))] * 2
    return pl.pallas_call(
        _dil_merge_kernel,
        grid=(bsz * nt, HEADS),
        in_specs=in_specs,
        out_specs=pl.BlockSpec((tt, DH), lambda i, h: (i, h)),
        out_shape=jax.ShapeDtypeStruct((bsz * seq, ATT_W), f32),
        scratch_shapes=[pltpu.VMEM((tt, DH), f32)] * (2 * len(DIL_PATTERNS)),
        compiler_params=_cparams("arbitrary", "arbitrary"),
        name="dilattn_merge",
    )(*partial)


def _dil_step_kernel(q_ref, kn_ref, vn_ref, k1_ref, k4_ref, k16_ref, v1_ref, v4_ref, v16_ref, o_ref):
    q = q_ref[0]
    kn, vn = kn_ref[0], vn_ref[0]
    s_self = _rowsum_bcast(q * kn)
    o_gs, lse_gs = [], []
    for k_ref, v_ref in ((k1_ref, v1_ref), (k4_ref, v4_ref), (k16_ref, v16_ref)):
        k3 = k_ref[0, :, 0]
        nk = k3.shape[0]
        s = _rowsum_bcast((k3 * q[None]).reshape(nk * HEADS, DH)).reshape(nk, HEADS, DH)
        mx = jnp.maximum(jnp.max(s, axis=0), s_self)
        p = jnp.exp(s - mx[None])
        p_self = jnp.exp(s_self - mx)
        den = jnp.sum(p, axis=0) + p_self
        num = jnp.sum(p * v_ref[0, :, 0], axis=0) + p_self * vn
        o_gs.append(num / den)
        lse_gs.append(mx + jnp.log(den))
    top = jnp.maximum(jnp.maximum(lse_gs[0], lse_gs[1]), lse_gs[2])
    es = [jnp.exp(l - top) for l in lse_gs]
    tot = es[0] + es[1] + es[2]
    o_ref[0] = (o_gs[0] * es[0] + o_gs[1] * es[1] + o_gs[2] * es[2]) / tot


def dilated_step(q4, k4, v4, cache_k, cache_v):
    n, win = cache_k.shape[0], cache_k.shape[1]
    one = pl.BlockSpec((1, HEADS, DH), lambda b: (b, 0, 0))
    args = [q4, k4, v4]
    in_specs = [one, one, one]
    for cache in (cache_k, cache_v):
        for (w, d) in DIL_PATTERNS:
            nkeys = w // d
            assert win % d == 0 and (win // d) % nkeys == 0 and (win - w) % (d * nkeys) == 0
            args.append(cache.reshape(n, win // d, d, HEADS, DH))
            in_specs.append(pl.BlockSpec((1, nkeys, 1, HEADS, DH),
                                         lambda b, blk=(win - w) // d // nkeys: (b, blk, 0, 0, 0)))
    return pl.pallas_call(
        _dil_step_kernel,
        grid=(n,),
        in_specs=in_specs,
        out_specs=one,
        out_shape=jax.ShapeDtypeStruct((n, HEADS, DH), f32),
        compiler_params=_cparams("arbitrary"),
        name="dilattn_step",
    )(*args).reshape(n, ATT_W)


def _sb_kernel(bias_ref, q_ref, k_ref, v_ref, o_ref, kb_ref, vb_ref, qs_ref, t_ref, spb_ref, wb_ref,
               acc_ref, run_ref):
    tq = q_ref.shape[1]
    nh = q_ref.shape[2] // DH
    kt_w = t_ref.shape[2]
    sub = PAGE
    hg = pl.program_id(1)
    qb = pl.program_id(2)

    @pl.when(qb == 0)
    def _():
        kb_ref[...] = k_ref[0].astype(bf16)
        vb_ref[...] = v_ref[0].astype(bf16)

    rr = _iota((sub, 2 * sub), 0)
    cc = _iota((sub, 2 * sub), 1)
    tri = jnp.where((rr > cc) | (cc >= sub), 1.0, 0.0).astype(bf16)
    for h in range(nh):
        qs_ref[h] = (q_ref[0, :, h * DH:(h + 1) * DH] * (DH ** -0.5)).astype(bf16)
    acc_ref[...] = jnp.zeros_like(acc_ref)
    run_ref[...] = jnp.zeros_like(run_ref)

    def macro(start, masked):
        for h in range(nh):
            hs = slice(h * DH, (h + 1) * DH)
            z = _dot_nt(qs_ref[h], kb_ref[pl.ds(start, kt_w), hs]) + bias_ref[hg * nh + h]
            sp = _softplus(z)
            t = z - sp
            if masked:
                ok = start + _iota((tq, kt_w), 1) < qb * tq + _iota((tq, kt_w), 0)
                sp = jnp.where(ok, sp, 0.0)
                t = jnp.where(ok, t, NEG_INF)
            t_ref[h] = t
            spb_ref[h] = sp.astype(bf16)
        for h in range(nh):
            run = run_ref[h]
            for kt in range(kt_w // sub - 1, -1, -1):
                ks = slice(kt * sub, (kt + 1) * sub)
                cs = _dot(spb_ref[h, :, ks], tri)
                wb_ref[h, :, ks] = jnp.exp(t_ref[h, :, ks] - cs[:, :sub] - run).astype(bf16)
                run = run + cs[:, sub:]
            run_ref[h] = run
        for h in range(nh):
            hs = slice(h * DH, (h + 1) * DH)
            acc_ref[h] += _dot(wb_ref[h], vb_ref[pl.ds(start, kt_w), hs])

    top = (qb * tq) // kt_w
    macro(pl.multiple_of(top * kt_w, kt_w), True)

    def body(it, _):
        macro(pl.multiple_of((top - 1 - it) * kt_w, kt_w), False)
        return 0

    lax.fori_loop(0, top, body, 0)
    for h in range(nh):
        o_ref[0, :, h * DH:(h + 1) * DH] = acc_ref[h]


def sb_prompt(proj2, bias, bsz, seq, tq=256, nh=4, kt_w=512):
    assert kt_w % tq == 0 and seq % kt_w == 0
    p3 = proj2.reshape(bsz, seq, proj2.shape[1])
    nq = seq // tq
    ng = HEADS // nh
    wd = nh * DH
    return pl.pallas_call(
        _sb_kernel,
        grid=(bsz, ng, nq),
        in_specs=[
            pl.BlockSpec(memory_space=pltpu.SMEM),
            pl.BlockSpec((1, tq, wd), lambda b, g, i: (b, i, g)),
            pl.BlockSpec((1, seq, wd), lambda b, g, i: (b, 0, ng + g)),
            pl.BlockSpec((1, seq, wd), lambda b, g, i: (b, 0, 2 * ng + g)),
        ],
        out_specs=pl.BlockSpec((1, tq, wd), lambda b, g, i: (b, i, g)),
        out_shape=jax.ShapeDtypeStruct((bsz, seq, ATT_W), f32),
        scratch_shapes=[
            pltpu.VMEM((seq, wd), bf16), pltpu.VMEM((seq, wd), bf16),
            pltpu.VMEM((nh, tq, DH), bf16),
            pltpu.VMEM((nh, tq, kt_w), f32), pltpu.VMEM((nh, tq, kt_w), bf16),
            pltpu.VMEM((nh, tq, kt_w), bf16),
            pltpu.VMEM((nh, tq, DH), f32), pltpu.VMEM((nh, tq, PAGE), f32),
        ],
        compiler_params=_cparams("arbitrary", "arbitrary", "arbitrary"),
        name="sb_prompt",
    )(bias, p3, p3, p3).reshape(bsz * seq, ATT_W)


def _sb_step_kernel(pt_ref, q_ref, bias_ref, eye_ref, *refs, npg):
    k_refs, v_refs = refs[:npg], refs[npg:2 * npg]
    o_ref, acc_ref, carry_ref = refs[2 * npg:]
    j = pl.program_id(1)

    @pl.when(j == 0)
    def _():
        acc_ref[...] = jnp.zeros_like(acc_ref)
        carry_ref[...] = jnp.zeros_like(carry_ref)

    q = q_ref[0]
    bias = bias_ref[...]
    ones = jnp.ones((DH, DH), bf16)
    rr = _iota((PAGE, 2 * PAGE), 0)
    cc = _iota((PAGE, 2 * PAGE), 1)
    tri = jnp.where((rr > cc) | (cc >= PAGE), 1.0, 0.0).astype(bf16)
    acc = acc_ref[...]
    run = carry_ref[...]
    for p in range(npg):
        k3 = k_refs[p][0]
        prod = (k3 * q[None]).reshape(PAGE * HEADS, DH).astype(bf16)
        zb = _dot(prod, ones).reshape(PAGE, HEADS, DH)
        z = jnp.sum(zb * eye_ref[...], axis=0) + bias
        sp = _softplus(z)
        hi, mid, lo = _split3(sp)
        cs = _dot(hi, tri) + _dot(mid, tri) + _dot(lo, tri)
        w = jnp.exp(z - sp - cs[:, :PAGE] - run)
        run = run + cs[:, PAGE:]
        wsel = (eye_ref[...] * w[None]).reshape(PAGE * HEADS, DH).astype(bf16)
        wb = _dot(wsel, ones).reshape(PAGE, HEADS, DH)
        acc = acc + jnp.sum(wb * v_refs[p][0], axis=0)
    acc_ref[...] = acc
    carry_ref[...] = run

    @pl.when(j == pl.num_programs(1) - 1)
    def _():
        o_ref[0] = acc


def sb_step(q_s, bias, cache_k, cache_v, page_table, npg=4):
    n, n_pages = page_table.shape
    q4 = (q_s * (DH ** -0.5)).reshape(n, HEADS, DH)
    bias4 = jnp.broadcast_to(bias[:, None], (HEADS, DH))
    eye3 = jnp.broadcast_to(jnp.eye(PAGE, DH, dtype=f32)[:, None, :], (PAGE, HEADS, DH))

    def page(p):
        return lambda b, j, pt: (pt[b, n_pages - 1 - (j * npg + p)], 0, 0, 0)

    kv_specs = [pl.BlockSpec((1, PAGE, HEADS, DH), page(p)) for p in range(npg)]
    grid_spec = pltpu.PrefetchScalarGridSpec(
        num_scalar_prefetch=1,
        grid=(n, n_pages // npg),
        in_specs=[
            pl.BlockSpec((1, HEADS, DH), lambda b, j, pt: (b, 0, 0)),
            pl.BlockSpec((HEADS, DH), lambda b, j, pt: (0, 0)),
            pl.BlockSpec((PAGE, HEADS, DH), lambda b, j, pt: (0, 0, 0)),
        ] + kv_specs + kv_specs,
        out_specs=pl.BlockSpec((1, HEADS, DH), lambda b, j, pt: (b, 0, 0)),
        scratch_shapes=[pltpu.VMEM((HEADS, DH), f32), pltpu.VMEM((HEADS, DH), f32)],
    )
    return pl.pallas_call(
        functools.partial(_sb_step_kernel, npg=npg),
        grid_spec=grid_spec,
        out_shape=jax.ShapeDtypeStruct((n, HEADS, DH), f32),
        compiler_params=_cparams("arbitrary", "arbitrary"),
        name="sb_step",
    )(page_table, q4, bias4, eye3, *([cache_k] * npg), *([cache_v] * npg)).reshape(n, ATT_W)


def _mlstm_kernel(q_ref, k_ref, v_ref, og_ref, gate_ref, gb_ref, gn_ref,
                  h_ref, c_out, n_out, m_out, c_s, n_s, m_s):
    ci = pl.program_id(1)
    L = CHUNK

    @pl.when(ci == 0)
    def _():
        c_s[...] = jnp.zeros_like(c_s)
        n_s[...] = jnp.zeros_like(n_s)
        m_s[...] = jnp.zeros_like(m_s)

    gt = gate_ref[...] + gb_ref[...]
    gt_t = gt.T
    ri = _iota((L, L), 0)
    li = _iota((L, L), 1)
    causal = li <= ri
    tri_incl = jnp.where(causal, 1.0, 0.0).astype(bf16)
    tri_incl_t = jnp.where(ri <= li, 1.0, 0.0).astype(bf16)
    for h in range(M_HEADS):
        sl = slice(h * M_DH, (h + 1) * M_DH)
        ig_col = gt[:, h:h + 1]
        ig_row = gt_t[h:h + 1, :]
        lf_col = _log_sigmoid_pair(gt[:, M_HEADS + h:M_HEADS + h + 1])[0]
        lf_row = _log_sigmoid_pair(gt_t[M_HEADS + h:M_HEADS + h + 1, :])[0]
        c_hi, c_mid, c_lo = _split3(jnp.broadcast_to(lf_col, (L, L)))
        bcum_col = _dot(tri_incl, c_hi) + _dot(tri_incl, c_mid) + _dot(tri_incl, c_lo)
        r_hi, r_mid, r_lo = _split3(jnp.broadcast_to(lf_row, (L, L)))
        bcum_row = _dot(r_hi, tri_incl_t) + _dot(r_mid, tri_incl_t) + _dot(r_lo, tri_incl_t)
        m_prev = m_s[h:h + 1, :]
        dlog = jnp.where(causal, bcum_col - bcum_row + ig_row, NEG_INF)
        inter = bcum_col + m_prev
        m_t = jnp.maximum(inter, jnp.max(dlog, axis=1, keepdims=True))
        dw = jnp.exp(dlog - m_t)
        iw = jnp.exp(inter - m_t)
        qh = q_ref[:, sl]
        kh = k_ref[:, sl] * (M_DH ** -0.5)
        vh = v_ref[:, sl]
        qb, kb, vb = qh.astype(bf16), kh.astype(bf16), vh.astype(bf16)
        sw = dw * _dot_nt(qb, kb)
        c_prev = c_s[h]
        n_prev = n_s[h:h + 1, :]
        iw_col = iw[:, 0:1]
        num = _dot(sw.astype(bf16), vb) + iw_col * _dot_nt(qb, c_prev.astype(bf16))
        qn = jnp.sum(qb.astype(f32) * n_prev.astype(bf16).astype(f32), axis=1, keepdims=True)
        den = jnp.sum(sw, axis=1, keepdims=True) + iw_col * qn
        m_col = m_t[:, 0:1]
        hout = num / jnp.maximum(jnp.abs(den), jnp.exp(-m_col))
        m_last = m_t[L - 1:L, :]
        b_last = bcum_col[L - 1:L, :]
        wl_col = jnp.exp(b_last[:, 0:1] - bcum_col[:, 0:1] + ig_col - m_last[:, 0:1])
        wl_row = jnp.exp(b_last - bcum_row[0:1, :] + ig_row - m_last)
        decay = jnp.exp(b_last + m_prev - m_last)
        dsc = decay[:, 0:1]
        c_s[h] = dsc * c_prev + _dot((vh * wl_col).T.astype(bf16), kb)
        wl8 = jnp.broadcast_to(wl_row, (8, L)).astype(bf16)
        n_s[h:h + 1, :] = dsc * n_prev + _dot(wl8, kb)[0:1, :]
        m_s[h:h + 1, :] = m_last
        hn = hout * lax.rsqrt(jnp.mean(hout * hout, axis=1, keepdims=True) + EPS)
        h_ref[:, sl] = hn * gn_ref[:, sl] * _sigmoid(og_ref[:, sl])

    @pl.when(ci == pl.num_programs(1) - 1)
    def _():
        c_out[0] = c_s[...]
        n_out[0] = n_s[...]
        m_out[0] = m_s[...]


def mlstm_prompt(proj2, gates, gate_bias, g_mnorm, bsz, seq):
    nc = seq // CHUNK
    col = lambda c: pl.BlockSpec((CHUNK, M_WIDTH), lambda b, i, c=c: (b * nc + i, c))
    hm, c1, n1, m1 = pl.pallas_call(
        _mlstm_kernel,
        grid=(bsz, nc),
        in_specs=[
            col(3), col(4), col(5), col(6),
            pl.BlockSpec((CHUNK, LANES), lambda b, i: (b * nc + i, 0)),
            pl.BlockSpec((1, LANES), lambda b, i: (0, 0)),
            pl.BlockSpec((1, M_WIDTH), lambda b, i: (0, 0)),
        ],
        out_specs=[
            pl.BlockSpec((CHUNK, M_WIDTH), lambda b, i: (b * nc + i, 0)),
            pl.BlockSpec((1, M_HEADS, M_DH, M_DH), lambda b, i: (b, 0, 0, 0)),
            pl.BlockSpec((1, M_HEADS, M_DH), lambda b, i: (b, 0, 0)),
            pl.BlockSpec((1, M_HEADS, LANES), lambda b, i: (b, 0, 0)),
        ],
        out_shape=[
            jax.ShapeDtypeStruct((bsz * seq, M_WIDTH), f32),
            jax.ShapeDtypeStruct((bsz, M_HEADS, M_DH, M_DH), f32),
            jax.ShapeDtypeStruct((bsz, M_HEADS, M_DH), f32),
            jax.ShapeDtypeStruct((bsz, M_HEADS, LANES), f32),
        ],
        scratch_shapes=[
            pltpu.VMEM((M_HEADS, M_DH, M_DH), f32),
            pltpu.VMEM((M_HEADS, M_DH), f32),
            pltpu.VMEM((M_HEADS, LANES), f32),
        ],
        compiler_params=_cparams("arbitrary", "arbitrary"),
        name="mlstm_prompt",
    )(proj2, proj2, proj2, proj2, gates, gate_bias, g_mnorm)
    return hm, c1, n1, m1[:, :, 0]


def _mlstm_step_kernel(q_ref, k_ref, v_ref, og_ref, gn_ref, ig_ref, fg_ref, c_ref, n_ref, m_ref,
                       h_ref, c_out, n_out, m_out):
    q = q_ref[0, 0]
    k = k_ref[0, 0] * (M_DH ** -0.5)
    v = v_ref[0, 0]
    ig = ig_ref[0, 0]
    lf = _log_sigmoid_pair(fg_ref[0, 0])[0]
    m0 = m_ref[0, 0]
    c0 = c_ref[0, 0]
    n0 = n_ref[0, 0]
    inter = lf + m0
    m_t = jnp.maximum(inter, ig)
    dw = jnp.exp(ig - m_t)
    iw = jnp.exp(inter - m_t)
    rnd = lambda a: a.astype(bf16).astype(f32)
    qr, kr, vr = rnd(q), rnd(k), rnd(v)
    qk = jnp.sum(qr * kr, axis=1, keepdims=True)
    sw = dw * qk
    cq = jnp.sum(rnd(c0) * qr, axis=1, keepdims=True)
    num = rnd(sw) * vr + iw * cq
    den = sw + iw * jnp.sum(rnd(n0) * qr, axis=1, keepdims=True)
    hout = num / jnp.maximum(jnp.abs(den), jnp.exp(-m_t))
    c_out[0, 0] = iw * c0 + rnd(dw * v) * kr
    n_out[0, 0] = iw * n0 + rnd(dw) * kr
    m_out[0, 0] = m_t
    hn = hout * lax.rsqrt(jnp.mean(hout * hout, axis=0, keepdims=True) + EPS)
    h_ref[0, 0] = hn * gn_ref[0] * _sigmoid(og_ref[0, 0])


def mlstm_step(proj2_s, gates_s, gate_bias, g_mnorm, c0, n0, m0):
    n = proj2_s.shape[0]
    base = 3 * ATT_W
    seg = lambda i: proj2_s[:, base + i * M_WIDTH: base + (i + 1) * M_WIDTH].reshape(n, M_HEADS, M_DH)
    g = gates_s + gate_bias
    rowb = pl.BlockSpec((1, 1, 1, M_DH), lambda b, h: (b, h, 0, 0))
    colb = pl.BlockSpec((1, 1, M_DH, 1), lambda b, h: (b, h, 0, 0))
    scal = pl.BlockSpec((1, 1, 1, 1), lambda b, h: (b, h, 0, 0))
    hcol, c1, n1, m1 = pl.pallas_call(
        _mlstm_step_kernel,
        grid=(n, M_HEADS),
        in_specs=[
            rowb, rowb, colb, colb,
            pl.BlockSpec((1, M_DH, 1), lambda b, h: (h, 0, 0)),
            scal, scal,
            pl.BlockSpec((1, 1, M_DH, M_DH), lambda b, h: (b, h, 0, 0)),
            rowb, scal,
        ],
        out_specs=[colb, pl.BlockSpec((1, 1, M_DH, M_DH), lambda b, h: (b, h, 0, 0)), rowb, scal],
        out_shape=[
            jax.ShapeDtypeStruct((n, M_HEADS, M_DH, 1), f32),
            jax.ShapeDtypeStruct((n, M_HEADS, M_DH, M_DH), f32),
            jax.ShapeDtypeStruct((n, M_HEADS, 1, M_DH), f32),
            jax.ShapeDtypeStruct((n, M_HEADS, 1, 1), f32),
        ],
        compiler_params=_cparams("arbitrary", "arbitrary"),
        name="mlstm_step",
    )(seg(0)[:, :, None, :], seg(1)[:, :, None, :], seg(2)[..., None], seg(3)[..., None],
      g_mnorm.reshape(M_HEADS, M_DH, 1),
      g[:, 0:M_HEADS].reshape(n, M_HEADS, 1, 1), g[:, M_HEADS:2 * M_HEADS].reshape(n, M_HEADS, 1, 1),
      c0, n0[:, :, None, :], m0.reshape(n, M_HEADS, 1, 1))
    return hcol.reshape(n, M_WIDTH), c1, n1.reshape(n, M_HEADS, M_DH), m1.reshape(n, M_HEADS)


def _peerq_kernel(x_ref, g_ref, sh_ref, sc_ref, wt_ref, qt_ref, xmt_ref, xs_ref):
    @pl.when(pl.program_id(1) == 0)
    def _():
        xm = _modulated(x_ref[...], g_ref[...], sh_ref[0], sc_ref[0])
        xt = xm.T.astype(bf16)
        xs_ref[...] = xt
        xmt_ref[...] = xt

    qt_ref[...] = _dot(wt_ref[...], xs_ref[...])


def peer_query(x, g, shift, scale, wq_t, rows_per_batch, tm, tn=512):
    m, k = x.shape
    n = wq_t.shape[0]
    sh, sh_spec = _mod_specs(shift, m, tm, k, rows_per_batch)
    sc, sc_spec = _mod_specs(scale, m, tm, k, rows_per_batch)
    return pl.pallas_call(
        _peerq_kernel,
        grid=(m // tm, n // tn),
        in_specs=[
            pl.BlockSpec((tm, k), lambda i, j: (i, 0)),
            pl.BlockSpec((1, k), lambda i, j: (0, 0)),
            sh_spec, sc_spec,
            pl.BlockSpec((tn, k), lambda i, j: (j, 0)),
        ],
        out_specs=[pl.BlockSpec((tn, tm), lambda i, j: (j, i)),
                   pl.BlockSpec((k, tm), lambda i, j: (0, i))],
        out_shape=[jax.ShapeDtypeStruct((n, m), f32), jax.ShapeDtypeStruct((k, m), bf16)],
        scratch_shapes=[pltpu.VMEM((k, tm), bf16)],
        compiler_params=_cparams("arbitrary", "arbitrary"),
        name="peer_query",
    )(x, g.reshape(1, k), sh, sc, wq_t)


def _heads_layout_kernel(*refs):
    n = len(refs) // 2
    for x_ref, o_ref in zip(refs[:n], refs[n:]):
        for hb in range(HEADS):
            o_ref[:, hb, :] = x_ref[:, hb * DH:(hb + 1) * DH]


def heads_layout(srcs, bsz, seq, tail, tt=512):
    nt, first = tail // tt, (seq - tail) // tt
    rows = lambda b, i: b * (seq // tt) + first + i
    return pl.pallas_call(
        _heads_layout_kernel,
        grid=(bsz, nt),
        in_specs=[pl.BlockSpec((tt, ATT_W), lambda b, i, c=c: (rows(b, i), c)) for _, c in srcs],
        out_specs=[pl.BlockSpec((tt, HEADS, DH), lambda b, i: (b * nt + i, 0, 0))] * len(srcs),
        out_shape=[jax.ShapeDtypeStruct((bsz * tail, HEADS, DH), f32)] * len(srcs),
        compiler_params=_cparams("arbitrary", "arbitrary"),
        name="heads_layout",
    )(*[x for x, _ in srcs])


def _cast_kernel(w_ref, o_ref):
    o_ref[...] = w_ref[0].astype(bf16)


def cast_layer_bf16(w, layer, tr=1024):
    _, rows, cols = w.shape
    return pl.pallas_call(
        _cast_kernel,
        grid=(rows // tr,),
        in_specs=[pl.BlockSpec((1, tr, cols), lambda i: (layer, i, 0))],
        out_specs=pl.BlockSpec((tr, cols), lambda i: (i, 0)),
        out_shape=jax.ShapeDtypeStruct((rows, cols), bf16),
        compiler_params=_cparams("arbitrary"),
        name="cast_bf16",
    )(w)


_N_TOP = PEER_TOPK + 1
_CAND_PAIRS = [(a, b) for a in range(_N_TOP) for b in range(_N_TOP) if (a + 1) * (b + 1) <= _N_TOP]
_N_CAND = -(-len(_CAND_PAIRS) // 8) * 8


def _extract_top(cur, ridx, n):
    vals = []
    big = float(cur.shape[0])
    for _ in range(n):
        mx = jnp.max(cur, axis=0, keepdims=True)
        first = jnp.min(jnp.where(cur == mx, ridx, big), axis=0, keepdims=True)
        cur = jnp.where(ridx == first, NEG_INF, cur)
        vals.append(mx)
    return vals


_ROUTER_UNROLL = 4


def _router_kernel(qt_ref, keys_ref, e1_ref, e2_ref, th_ref, cand_ref):
    ridx = _iota((N_KEYS, LANES), 0).astype(f32)
    cidx = _iota((_N_CAND, LANES), 0).astype(f32)
    k0 = keys_ref[0].astype(bf16)
    k1 = keys_ref[1].astype(bf16)
    cand_ref[...] = jnp.full(cand_ref.shape, NEG_INF, f32)

    def one_head(h, slot):
        r0 = pl.multiple_of(h * 2 * N_KEYS, 2 * N_KEYS)
        s1 = _dot(k0, qt_ref[pl.ds(r0, N_KEYS), :].astype(bf16))
        s2 = _dot(k1, qt_ref[pl.ds(r0 + N_KEYS, N_KEYS), :].astype(bf16))
        top_a = _extract_top(s1, ridx, _N_TOP)
        top_b = _extract_top(s2, ridx, _N_TOP)
        for ci, (a, b) in enumerate(_CAND_PAIRS):
            cand_ref[slot, ci:ci + 1, :] = top_a[a] + top_b[b]
        cs = _extract_top(cand_ref[slot], cidx, _N_TOP)
        z = jnp.ones_like(cs[0])
        for r in range(1, PEER_TOPK):
            z = z + jnp.exp(cs[r] - cs[0])
        inv_z = 1.0 / z
        mid = 0.5 * (cs[PEER_TOPK - 1] + cs[PEER_TOPK])
        o0 = pl.multiple_of(h * N_KEYS, N_KEYS)
        e1 = jnp.exp(s1 - top_a[0]) * inv_z
        e1_ref[:, h] = e1.reshape(N_KEYS // 8, 8, LANES)
        e2_ref[pl.ds(o0, N_KEYS), :] = jnp.exp(s2 - top_b[0])
        th_ref[pl.ds(h, 1), :] = jnp.exp(mid - cs[0]) * inv_z

    def body(hp, _):
        for slot in range(_ROUTER_UNROLL):
            one_head(hp * _ROUTER_UNROLL + slot, slot)
        return 0

    lax.fori_loop(0, PEER_HEADS // _ROUTER_UNROLL, body, 0)


def peer_router(qt, keys):
    n, m = qt.shape
    half = PEER_HEADS * N_KEYS
    return pl.pallas_call(
        _router_kernel,
        grid=(m // LANES,),
        in_specs=[pl.BlockSpec((n, LANES), lambda i: (0, i)),
                  pl.BlockSpec((2, N_KEYS, N_KEYS), lambda i: (0, 0, 0))],
        out_specs=[pl.BlockSpec((N_KEYS // 8, PEER_HEADS, 8, LANES), lambda i: (0, 0, 0, i)),
                   pl.BlockSpec((half, LANES), lambda i: (0, i)),
                   pl.BlockSpec((PEER_HEADS, LANES), lambda i: (0, i))],
        out_shape=[jax.ShapeDtypeStruct((N_KEYS // 8, PEER_HEADS, 8, m), f32),
                   jax.ShapeDtypeStruct((half, m), f32),
                   jax.ShapeDtypeStruct((PEER_HEADS, m), f32)],
        scratch_shapes=[pltpu.VMEM((_ROUTER_UNROLL, _N_CAND, LANES), f32)],
        compiler_params=_cparams("arbitrary"),
        name="peer_router",
    )(qt, keys)


def _experts_kernel(xmt_ref, e1_ref, e2_ref, th_ref, u_ref, v_ref, x_ref, gt_ref, gf_ref,
                    o_ref, w_ref, *, final_norm):
    c = pl.program_id(1)
    tm, te = w_ref.shape
    ni = te // N_KEYS

    @pl.when(c == 0)
    def _():
        o_ref[...] = jnp.zeros_like(o_ref)

    act = _gelu(_dot(u_ref[...], xmt_ref[...]))
    for ii in range(ni):
        es = slice(ii * N_KEYS, (ii + 1) * N_KEYS)
        for lc in range(tm // LANES):
            ls = slice(lc * LANES, (lc + 1) * LANES)
            g = jnp.zeros((N_KEYS, LANES), f32)
            for h in range(PEER_HEADS):
                row = e1_ref[ii // 8, h, ii % 8:ii % 8 + 1, ls]
                pr = e2_ref[h * N_KEYS:(h + 1) * N_KEYS, ls] * row
                g = g + jnp.where(pr >= th_ref[h:h + 1, ls], pr, 0.0)
            w_ref[ls, es] = (g * act[es, ls]).T.astype(bf16)
    o_ref[...] += _dot(w_ref[...], v_ref[...])

    @pl.when(c == pl.num_programs(1) - 1)
    def _():
        y = x_ref[...] + gt_ref[0] * o_ref[...]
        if final_norm:
            ms = jnp.mean(y * y, axis=-1, keepdims=True)
            y = y * lax.rsqrt(ms + EPS) * gf_ref[...]
        o_ref[...] = y


def peer_experts(xmt, e1t, e2t, th, u, v, x, gate, g_final, rows_per_batch, tm, te, final_norm):
    m, d = x.shape
    gt, gt_spec = _mod_specs(gate, m, tm, d, rows_per_batch)
    half = PEER_HEADS * N_KEYS
    return pl.pallas_call(
        functools.partial(_experts_kernel, final_norm=final_norm),
        grid=(m // tm, N_EXPERTS // te),
        in_specs=[
            pl.BlockSpec((d, tm), lambda i, c: (0, i)),
            pl.BlockSpec((te // N_KEYS // 8, PEER_HEADS, 8, tm), lambda i, c: (c, 0, 0, i)),
            pl.BlockSpec((half, tm), lambda i, c: (0, i)),
            pl.BlockSpec((PEER_HEADS, tm), lambda i, c: (0, i)),
            pl.BlockSpec((te, d), lambda i, c: (c, 0)),
            pl.BlockSpec((te, d), lambda i, c: (c, 0)),
            pl.BlockSpec((tm, d), lambda i, c: (i, 0)),
            gt_spec,
            pl.BlockSpec((1, d), lambda i, c: (0, 0)),
        ],
        out_specs=pl.BlockSpec((tm, d), lambda i, c: (i, 0)),
        out_shape=jax.ShapeDtypeStruct((m, d), f32),
        scratch_shapes=[pltpu.VMEM((tm, te), bf16)],
        compiler_params=_cparams("arbitrary", "arbitrary"),
        name="peer_experts",
    )(xmt, e1t, e2t, th, u, v, x, gt, g_final.reshape(1, d))


def peer_block(x, g, shift, scale, gate, wq_t, keys, u, v, g_final, rows_per_batch, tm, te, final_norm):
    qt, xmt = peer_query(x, g, shift, scale, wq_t, rows_per_batch, tm)
    e1t, e2t, th = peer_router(qt, keys)
    return peer_experts(xmt, e1t, e2t, th, u, v, x, gate, g_final, rows_per_batch, tm, te, final_norm)


def kernel(x_prompt, x_sample, c_prompt, c_sample, state_rglru_conv, state_rglru_h, cache_swa_k, cache_swa_v, cache_sb_k, cache_sb_v, state_mlstm_C, state_mlstm_n, state_mlstm_m, page_table, w_ada, b_ada, g_norm_mix, g_norm_ffn, e_w_in, e_conv_w, e_conv_b, e_w_r, e_b_r, e_w_i, e_b_i, e_lambda, e_w_out, o_w_in, o_b_if, o_sb_bias, o_g_mnorm, o_w_out, peer_w_q, peer_keys, peer_u, peer_v, g_final):
    bp, seq, d = x_prompt.shape
    bs = x_sample.shape[0]
    mp = bp * seq
    pad_s = LANES
    xp = x_prompt.reshape(mp, d)
    xs = x_sample.reshape(bs, d)

    c_rows = 16
    c_all = jnp.concatenate([c_prompt, c_sample, jnp.zeros((c_rows - bp - bs, d), f32)], axis=0)
    mod = adaln_all(c_all, w_ada, b_ada)

    def mods(layer):
        parts = [mod[layer, :, i * d:(i + 1) * d] for i in range(6)]
        return [p[:bp] for p in parts], [p[bp:bp + bs] for p in parts]

    ctab_p, stab_p = _rope_tables(jnp.arange(seq, dtype=jnp.int32))
    ctab_s, stab_s = _rope_tables(jnp.full((1,), PAST_LEN, jnp.int32))

    TM = 512
    m_p, m_s = mods(0)
    w_in = e_w_in[0].astype(bf16)
    w_out = e_w_out[0].astype(bf16)
    cw, cb = e_conv_w[0], e_conv_b[0].reshape(1, -1)
    wr, wi = e_w_r[0].astype(bf16), e_w_i[0].astype(bf16)
    br, bi, lam = e_b_r[0].reshape(1, -1), e_b_i[0].reshape(1, -1), e_lambda[0].reshape(1, -1)

    proj_p = mod_matmul(xp, g_norm_mix[0], m_p[0], m_p[1], w_in, E_IN, seq, 1024, 1024, name="e_in_p")
    proj_s = mod_matmul(xs, g_norm_mix[0], m_s[0], m_s[1], w_in, E_IN, 1, bs, 512, name="e_in_s")

    ya_p, h_p = rglru_prompt(proj_p, bp, seq, cw, cb, wr, br, wi, bi, lam)
    ya_s, h_s = rglru_step(proj_s, state_rglru_conv[0], state_rglru_h[0], cw, cb, wr, br, wi, bi, lam)
    conv_p = proj_p.reshape(bp, seq, E_IN)[:, seq - (CONV_W - 1):, :RG_WIDTH]
    conv_s = jnp.concatenate([state_rglru_conv[0][:, 1:], proj_s[:, None, :RG_WIDTH]], axis=1)

    qkv_p, k_p = rope_split(proj_p, ctab_p, stab_p, bp, seq)
    q_s, k_s, v_s = rope_qkv(proj_s, ctab_s, stab_s, 1, bs)
    o_p = dilated_prompt(qkv_p, bp, seq)
    win = cache_swa_k.shape[2]
    o_s = dilated_step(q_s, k_s, v_s, cache_swa_k.reshape(-1, win, HEADS, DH)[:bs],
                       cache_swa_v.reshape(-1, win, HEADS, DH)[:bs])
    wl = min(2048, seq)
    swa_k_p, swa_v_p = [a.reshape(bp, wl, HEADS, DH)
                        for a in heads_layout([(k_p, 0), (proj_p, E_IN // ATT_W - 1)], bp, seq, wl)]
    swa_k_s = k_s.reshape(bs, 1, HEADS, DH)
    swa_v_s = v_s.reshape(bs, 1, HEADS, DH)

    xp = out_proj(ya_p, o_p, w_out, xp, m_p[2], seq, 1024, 1024, name="e_out_p")
    xs = out_proj(ya_s, o_s, w_out, xs, m_s[2], 1, bs, 512, name="e_out_s")

    def peer_layer(layer, xp, xs, m_p, m_s, final_norm):
        wq_t = peer_w_q[layer].T.astype(bf16)
        u = cast_layer_bf16(peer_u, layer)
        v = cast_layer_bf16(peer_v, layer)
        xp = peer_block(xp, g_norm_ffn[layer], m_p[3], m_p[4], m_p[5], wq_t, peer_keys[layer], u, v,
                        g_final, seq, TM, 1024, final_norm)
        xs_pad = jnp.pad(xs, ((0, pad_s - bs), (0, 0)))
        xs_new = peer_block(xs_pad, g_norm_ffn[layer], m_s[3], m_s[4], m_s[5], wq_t, peer_keys[layer],
                            u, v, g_final, 1, pad_s, 1024, final_norm)
        return xp, xs_new[:bs]

    xp, xs = peer_layer(0, xp, xs, m_p, m_s, False)

    m_p, m_s = mods(1)
    w_in2 = o_w_in[0].astype(bf16)
    w_gate = jnp.pad(w_in2[:, O_MAIN:], ((0, 0), (0, LANES - 2 * M_HEADS)))
    w_out2 = o_w_out[0].astype(bf16)
    gate_bias = jnp.pad(o_b_if[0].reshape(1, 2 * M_HEADS), ((0, 0), (0, LANES - 2 * M_HEADS)))
    gmn = o_g_mnorm[0].reshape(1, M_WIDTH)

    proj2_p, gates_p = mod_matmul(xp, g_norm_mix[1], m_p[0], m_p[1], w_in2, O_MAIN, seq, 1024, 1024,
                                  w_gate=w_gate, name="o_in_p")
    proj2_s, gates_s = mod_matmul(xs, g_norm_mix[1], m_s[0], m_s[1], w_in2, O_MAIN, 1, bs, 512,
                                  w_gate=w_gate, name="o_in_s")

    oc_p = sb_prompt(proj2_p, o_sb_bias[0], bp, seq)
    oc_s = sb_step(proj2_s[:, :ATT_W], o_sb_bias[0], cache_sb_k.reshape(-1, PAGE, HEADS, DH),
                   cache_sb_v.reshape(-1, PAGE, HEADS, DH), page_table)
    hm_p, mC_p, mn_p, mm_p = mlstm_prompt(proj2_p, gates_p, gate_bias, gmn, bp, seq)
    hm_s, mC_s, mn_s, mm_s = mlstm_step(proj2_s, gates_s, gate_bias, gmn,
                                        state_mlstm_C.reshape(-1, M_HEADS, M_DH, M_DH)[:bs],
                                        state_mlstm_n[0], state_mlstm_m[0])

    n_pg = seq // PAGE
    sb_k_p, sb_v_p = [a.reshape(bp, n_pg, PAGE, HEADS, DH)
                      for a in heads_layout([(proj2_p, 1), (proj2_p, 2)], bp, seq, seq)]
    sb_k_s = proj2_s[:, ATT_W:2 * ATT_W].reshape(bs, 1, HEADS, DH)
    sb_v_s = proj2_s[:, 2 * ATT_W:3 * ATT_W].reshape(bs, 1, HEADS, DH)

    xp = out_proj(oc_p, hm_p, w_out2, xp, m_p[2], seq, 1024, 1024, name="o_out_p")
    xs = out_proj(oc_s, hm_s, w_out2, xs, m_s[2], 1, bs, 512, name="o_out_s")
    xp, xs = peer_layer(1, xp, xs, m_p, m_s, True)

    y_prompt = xp.reshape(bp, seq, d)
    y_sample = xs.reshape(bs, 1, d)
    st = lambda a: a[None]
    return (y_prompt, y_sample, st(conv_p), st(conv_s), st(h_p.reshape(bp, RG_WIDTH)), st(h_s),
            st(swa_k_p), st(swa_k_s), st(swa_v_p), st(swa_v_s),
            st(sb_k_p), st(sb_k_s), st(sb_v_p), st(sb_v_s),
            st(mC_p), st(mC_s), st(mn_p), st(mn_s), st(mm_p), st(mm_s))
```

```python
import functools
import math

import jax
import jax.numpy as jnp
from jax import lax
from jax.experimental import pallas as pl
from jax.experimental.pallas import tpu as pltpu

f32 = jnp.float32
bf16 = jnp.bfloat16

D_MODEL = 2048
PAST_LEN = 16384
PAGE = 128
RG_WIDTH = 1024
RG_BLOCKS = 8
CONV_W = 4
RG_C = 8.0
HEADS = 8
DH = 128
ATT_W = HEADS * DH
DIL_PATTERNS = ((128, 1), (512, 4), (2048, 16))
ROT_DIMS = 32
ROPE_THETA = 500000.0
M_HEADS = 4
M_DH = 256
M_WIDTH = M_HEADS * M_DH
CHUNK = 128
E_IN = 2 * RG_WIDTH + 3 * ATT_W
O_MAIN = 3 * ATT_W + 4 * M_WIDTH
PEER_HEADS = 8
N_KEYS = 128
N_EXPERTS = N_KEYS * N_KEYS
PEER_TOPK = 16
EPS = 1e-6
LANES = 128
VMEM_LIMIT = 56 * 1024 * 1024
NEG_INF = float("-inf")


def _cparams(*sem):
    return pltpu.CompilerParams(dimension_semantics=sem, vmem_limit_bytes=VMEM_LIMIT)


def _dot(a, b):
    return jnp.dot(a, b, preferred_element_type=f32)


def _dot_nt(a, b):
    return lax.dot_general(a, b, (((1,), (1,)), ((), ())), preferred_element_type=f32)


def _split3(x):
    hi = x.astype(bf16)
    r = x - hi.astype(f32)
    mid = r.astype(bf16)
    lo = (r - mid.astype(f32)).astype(bf16)
    return hi, mid, lo


def _sigmoid(x):
    return 1.0 / (1.0 + jnp.exp(-x))


def _log_sigmoid_pair(z):
    l1p = jnp.log1p(jnp.exp(-jnp.abs(z)))
    return jnp.minimum(z, 0.0) - l1p, -jnp.maximum(z, 0.0) - l1p


def _softplus(z):
    return jnp.maximum(z, 0.0) + jnp.log(1.0 + jnp.exp(-jnp.abs(z)))


def _gelu(x):
    c = math.sqrt(2.0 / math.pi)
    return 0.5 * x * (1.0 + jnp.tanh(c * (x + 0.044715 * (x * x * x))))


def _iota(shape, dim):
    return lax.broadcasted_iota(jnp.int32, shape, dim)


def _rowsum_bcast(x):
    ones = jnp.ones((LANES, LANES), bf16)
    hi = x.astype(bf16)
    lo = (x - hi.astype(f32)).astype(bf16)
    return _dot(hi, ones) + _dot(lo, ones)


def _adaln_kernel(c_ref, w_ref, b_ref, o_ref):
    c = c_ref[...]
    s = c * _sigmoid(c)
    w = w_ref[0]
    s_hi = s.astype(bf16)
    s_lo = (s - s_hi.astype(f32)).astype(bf16)
    w_hi = w.astype(bf16)
    w_lo = (w - w_hi.astype(f32)).astype(bf16)
    o_ref[0] = _dot(s_hi, w_hi) + _dot(s_hi, w_lo) + _dot(s_lo, w_hi) + b_ref[0]


def adaln_all(c_all, w_ada, b_ada):
    depth, d, n = w_ada.shape
    rows = c_all.shape[0]
    tn = 1024
    return pl.pallas_call(
        _adaln_kernel,
        grid=(depth, n // tn),
        in_specs=[
            pl.BlockSpec((rows, d), lambda l, j: (0, 0)),
            pl.BlockSpec((1, d, tn), lambda l, j: (l, 0, j)),
            pl.BlockSpec((1, 1, tn), lambda l, j: (l, 0, j)),
        ],
        out_specs=pl.BlockSpec((1, rows, tn), lambda l, j: (l, 0, j)),
        out_shape=jax.ShapeDtypeStruct((depth, rows, n), f32),
        compiler_params=_cparams("arbitrary", "arbitrary"),
        name="adaln",
    )(c_all, w_ada, b_ada.reshape(depth, 1, n))


def _modulated(x, g, sh, sc):
    ms = jnp.mean(x * x, axis=-1, keepdims=True)
    y = x * lax.rsqrt(ms + EPS) * g
    return y * (1.0 + sc) + sh


def _modmm_kernel(x_ref, g_ref, sh_ref, sc_ref, w_ref, o_ref, xn_ref):
    @pl.when(pl.program_id(1) == 0)
    def _():
        xn_ref[...] = _modulated(x_ref[...], g_ref[...], sh_ref[0], sc_ref[0]).astype(bf16)

    o_ref[...] = _dot(xn_ref[...], w_ref[...])


def _modmm_gate_kernel(x_ref, g_ref, sh_ref, sc_ref, w_ref, wg_ref, o_ref, og_ref, xn_ref):
    @pl.when(pl.program_id(1) == 0)
    def _():
        xn = _modulated(x_ref[...], g_ref[...], sh_ref[0], sc_ref[0]).astype(bf16)
        xn_ref[...] = xn
        og_ref[...] = _dot(xn, wg_ref[...])

    o_ref[...] = _dot(xn_ref[...], w_ref[...])


def _mod_specs(mod, m, tm, k, rows_per_batch):
    if rows_per_batch >= tm:
        assert rows_per_batch % tm == 0
        per = rows_per_batch // tm
        return mod.reshape(-1, 1, k), pl.BlockSpec((1, 1, k), lambda i, j: (i // per, 0, 0))
    assert rows_per_batch == 1
    rows = mod
    if rows.shape[0] < m:
        rows = jnp.pad(rows, ((0, m - rows.shape[0]), (0, 0)))
    return rows.reshape(1, m, k), pl.BlockSpec((1, tm, k), lambda i, j: (0, i, 0))


def mod_matmul(x, g, shift, scale, w, n_out, rows_per_batch, tm, tn, w_gate=None, name="modmm"):
    m, k = x.shape
    sh, sh_spec = _mod_specs(shift, m, tm, k, rows_per_batch)
    sc, sc_spec = _mod_specs(scale, m, tm, k, rows_per_batch)
    in_specs = [
        pl.BlockSpec((tm, k), lambda i, j: (i, 0)),
        pl.BlockSpec((1, k), lambda i, j: (0, 0)),
        sh_spec,
        sc_spec,
        pl.BlockSpec((k, tn), lambda i, j: (0, j)),
    ]
    args = [x, g.reshape(1, k), sh, sc, w]
    out_specs = pl.BlockSpec((tm, tn), lambda i, j: (i, j))
    out_shape = jax.ShapeDtypeStruct((m, n_out), f32)
    kern = _modmm_kernel
    if w_gate is not None:
        in_specs.append(pl.BlockSpec((k, LANES), lambda i, j: (0, 0)))
        args.append(w_gate)
        out_specs = [out_specs, pl.BlockSpec((tm, LANES), lambda i, j: (i, 0))]
        out_shape = [out_shape, jax.ShapeDtypeStruct((m, LANES), f32)]
        kern = _modmm_gate_kernel
    return pl.pallas_call(
        kern,
        grid=(m // tm, n_out // tn),
        in_specs=in_specs,
        out_specs=out_specs,
        out_shape=out_shape,
        scratch_shapes=[pltpu.VMEM((tm, k), bf16)],
        compiler_params=_cparams("arbitrary", "arbitrary"),
        name=name,
    )(*args)


def _outproj_kernel(a1_ref, a2_ref, w1_ref, w2_ref, x_ref, gt_ref, o_ref):
    y = _dot(a1_ref[...].astype(bf16), w1_ref[...]) + _dot(a2_ref[...].astype(bf16), w2_ref[...])
    o_ref[...] = x_ref[...] + gt_ref[0] * y


def out_proj(a1, a2, w, x, gate, rows_per_batch, tm, tn, name="outproj"):
    m, k1 = a1.shape
    k2 = a2.shape[1]
    n = w.shape[1]
    gt, gt_spec = _mod_specs(gate, m, tm, n, rows_per_batch)
    if gt.shape[1] == 1:
        per = rows_per_batch // tm
        gt_spec = pl.BlockSpec((1, 1, tn), lambda i, j: (i // per, 0, j))
    else:
        gt_spec = pl.BlockSpec((1, tm, tn), lambda i, j: (0, i, j))
    return pl.pallas_call(
        _outproj_kernel,
        grid=(m // tm, n // tn),
        in_specs=[
            pl.BlockSpec((tm, k1), lambda i, j: (i, 0)),
            pl.BlockSpec((tm, k2), lambda i, j: (i, 0)),
            pl.BlockSpec((k1, tn), lambda i, j: (0, j)),
            pl.BlockSpec((k2, tn), lambda i, j: (k1 // k2, j)),
            pl.BlockSpec((tm, tn), lambda i, j: (i, j)),
            gt_spec,
        ],
        out_specs=pl.BlockSpec((tm, tn), lambda i, j: (i, j)),
        out_shape=jax.ShapeDtypeStruct((m, n), f32),
        compiler_params=_cparams("arbitrary", "arbitrary"),
        name=name,
    )(a1, a2, w, w, x, gt)


def _rglru_gates(xc, wr_ref, br, wi_ref, bi, lam):
    xb = xc.astype(bf16)
    rs, gs = [], []
    for hb in range(RG_BLOCKS):
        sl = slice(hb * LANES, (hb + 1) * LANES)
        rs.append(_dot(xb[:, sl], wr_ref[hb]))
        gs.append(_dot(xb[:, sl], wi_ref[hb]))
    r = _sigmoid(jnp.concatenate(rs, axis=1) + br)
    ig = _sigmoid(jnp.concatenate(gs, axis=1) + bi)
    softplus_neg_lam = jnp.maximum(-lam, 0.0) + jnp.log1p(jnp.exp(-jnp.abs(lam)))
    log_a = -RG_C * r * softplus_neg_lam
    a = jnp.exp(log_a)
    u = jnp.sqrt(-jnp.tanh(log_a) * (a * a + 1.0)) * ig * xc
    return a, u


def _rglru_kernel(xa_ref, ga_ref, cw_ref, cb_ref, wr_ref, br_ref, wi_ref, bi_ref, lam_ref,
                  ya_ref, hl_ref, xprev_ref, hc_ref):
    t_idx = pl.program_id(1)
    tt = xa_ref.shape[0]

    @pl.when(t_idx == 0)
    def _():
        xprev_ref[...] = jnp.zeros_like(xprev_ref)
        hc_ref[...] = jnp.zeros_like(hc_ref)

    xa = xa_ref[...]
    xprev = xprev_ref[...]
    row8 = _iota((8, RG_WIDTH), 0)
    xc = cb_ref[...] + cw_ref[CONV_W - 1:CONV_W, :] * xa
    for k in range(1, CONV_W):
        rolled = pltpu.roll(xa, k, 0)
        head = jnp.where(row8 < k, pltpu.roll(xprev, k, 0), rolled[0:8])
        shifted = jnp.concatenate([head, rolled[8:]], axis=0)
        xc = xc + cw_ref[CONV_W - 1 - k:CONV_W - k, :] * shifted
    xprev_ref[...] = xa[tt - 8:tt]

    a, u = _rglru_gates(xc, wr_ref, br_ref[...], wi_ref, bi_ref[...], lam_ref[...])
    row = _iota((tt, RG_WIDTH), 0)
    s = 1
    while s < tt:
        a_sh = pltpu.roll(a, s, 0)
        u_sh = pltpu.roll(u, s, 0)
        ok = row >= s
        u = jnp.where(ok, a * u_sh + u, u)
        a = jnp.where(ok, a * a_sh, a)
        s *= 2
    h = a * hc_ref[...] + u
    hc_ref[...] = h[tt - 1:tt]
    hl_ref[0] = h[tt - 1:tt]
    ya_ref[...] = h * _gelu(ga_ref[...])


def rglru_prompt(proj, bsz, seq, cw, cb, wr, br, wi, bi, lam, tt=256):
    nt = seq // tt
    vec = lambda: pl.BlockSpec((1, RG_WIDTH), lambda b, t: (0, 0))
    return pl.pallas_call(
        _rglru_kernel,
        grid=(bsz, nt),
        in_specs=[
            pl.BlockSpec((tt, RG_WIDTH), lambda b, t: (b * nt + t, 0)),
            pl.BlockSpec((tt, RG_WIDTH), lambda b, t: (b * nt + t, 1)),
            pl.BlockSpec((CONV_W, RG_WIDTH), lambda b, t: (0, 0)),
            vec(),
            pl.BlockSpec((RG_BLOCKS, LANES, LANES), lambda b, t: (0, 0, 0)),
            vec(),
            pl.BlockSpec((RG_BLOCKS, LANES, LANES), lambda b, t: (0, 0, 0)),
            vec(),
            vec(),
        ],
        out_specs=[
            pl.BlockSpec((tt, RG_WIDTH), lambda b, t: (b * nt + t, 0)),
            pl.BlockSpec((1, 1, RG_WIDTH), lambda b, t: (b, 0, 0)),
        ],
        out_shape=[
            jax.ShapeDtypeStruct((bsz * seq, RG_WIDTH), f32),
            jax.ShapeDtypeStruct((bsz, 1, RG_WIDTH), f32),
        ],
        scratch_shapes=[pltpu.VMEM((8, RG_WIDTH), f32), pltpu.VMEM((1, RG_WIDTH), f32)],
        compiler_params=_cparams("arbitrary", "arbitrary"),
        name="rglru_prompt",
    )(proj, proj, cw, cb, wr, br, wi, bi, lam)


def _rglru_step_kernel(xa_ref, ga_ref, b0_ref, b1_ref, b2_ref, h0_ref, cw_ref, cb_ref,
                       wr_ref, br_ref, wi_ref, bi_ref, lam_ref, ya_ref, h_ref):
    xa = xa_ref[...]
    xc = (cb_ref[...] + cw_ref[0:1, :] * b0_ref[...] + cw_ref[1:2, :] * b1_ref[...]
          + cw_ref[2:3, :] * b2_ref[...] + cw_ref[3:4, :] * xa)
    a, u = _rglru_gates(xc, wr_ref, br_ref[...], wi_ref, bi_ref[...], lam_ref[...])
    h = a * h0_ref[...] + u
    h_ref[...] = h
    ya_ref[...] = h * _gelu(ga_ref[...])


def rglru_step(proj_s, conv_state, h0, cw, cb, wr, br, wi, bi, lam):
    n = proj_s.shape[0]
    full = lambda shape: pl.BlockSpec(shape, lambda i: tuple(0 for _ in shape))
    return pl.pallas_call(
        _rglru_step_kernel,
        grid=(1,),
        in_specs=[
            pl.BlockSpec((n, RG_WIDTH), lambda i: (0, 0)),
            pl.BlockSpec((n, RG_WIDTH), lambda i: (0, 1)),
            full((n, RG_WIDTH)), full((n, RG_WIDTH)), full((n, RG_WIDTH)), full((n, RG_WIDTH)),
            full((CONV_W, RG_WIDTH)), full((1, RG_WIDTH)),
            full((RG_BLOCKS, LANES, LANES)), full((1, RG_WIDTH)),
            full((RG_BLOCKS, LANES, LANES)), full((1, RG_WIDTH)), full((1, RG_WIDTH)),
        ],
        out_specs=[full((n, RG_WIDTH)), full((n, RG_WIDTH))],
        out_shape=[jax.ShapeDtypeStruct((n, RG_WIDTH), f32)] * 2,
        compiler_params=_cparams("arbitrary"),
        name="rglru_step",
    )(proj_s, proj_s, conv_state[:, 0], conv_state[:, 1], conv_state[:, 2], h0,
      cw, cb, wr, br, wi, bi, lam)


def _rope_tables(pos):
    half = ROT_DIMS // 2
    inv = ROPE_THETA ** (-jnp.arange(half, dtype=f32) / half)
    ang = pos.astype(f32)[:, None] * inv[None, :]
    cos, sin = jnp.cos(ang), jnp.sin(ang)
    n = pos.shape[0]
    ctab = jnp.concatenate([cos, cos, jnp.ones((n, DH - ROT_DIMS), f32)], axis=1)
    stab = jnp.concatenate([-sin, sin, jnp.zeros((n, DH - ROT_DIMS), f32)], axis=1)
    return ctab, stab


def _rope_head(xh, ctab, stab, lane):
    half = ROT_DIMS // 2
    partner = jnp.where(lane < half, pltpu.roll(xh, DH - half, 1), pltpu.roll(xh, half, 1))
    return xh * ctab + partner * stab


def _rope_kernel(q_ref, k_ref, v_ref, c_ref, s_ref, qo_ref, ko_ref, vo_ref):
    ctab, stab = c_ref[...], s_ref[...]
    lane = _iota((q_ref.shape[0], DH), 1)
    for hb in range(HEADS):
        sl = slice(hb * DH, (hb + 1) * DH)
        qo_ref[:, hb, :] = _rope_head(q_ref[:, sl], ctab, stab, lane) * (DH ** -0.5)
        ko_ref[:, hb, :] = _rope_head(k_ref[:, sl], ctab, stab, lane)
        vo_ref[:, hb, :] = v_ref[:, sl]


def rope_qkv(proj, ctab, stab, rows_per_seq, tt):
    m = proj.shape[0]
    nt = max(rows_per_seq // tt, 1)
    if ctab.shape[0] == 1:
        tab_spec = pl.BlockSpec((1, DH), lambda i: (0, 0))
    else:
        tab_spec = pl.BlockSpec((tt, DH), lambda i: (i % nt, 0))
    return pl.pallas_call(
        _rope_kernel,
        grid=(m // tt,),
        in_specs=[
            pl.BlockSpec((tt, ATT_W), lambda i: (i, 2)),
            pl.BlockSpec((tt, ATT_W), lambda i: (i, 3)),
            pl.BlockSpec((tt, ATT_W), lambda i: (i, 4)),
            tab_spec, tab_spec,
        ],
        out_specs=[pl.BlockSpec((tt, HEADS, DH), lambda i: (i, 0, 0))] * 3,
        out_shape=[jax.ShapeDtypeStruct((m, HEADS, DH), f32)] * 3,
        compiler_params=_cparams("arbitrary"),
        name="rope",
    )(proj, proj, proj, ctab, stab)


def _rope_split_kernel(q_ref, k_ref, v_ref, c_ref, s_ref, *refs):
    n_pat = len(DIL_PATTERNS)
    outs, (kf_ref, qs_ref, ks_ref) = refs[:3 * n_pat], refs[3 * n_pat:]
    tt = q_ref.shape[0]
    ctab, stab = c_ref[...], s_ref[...]
    lane = _iota((tt, DH), 1)
    qs_ref[...] = _rope_head(q_ref[...], ctab, stab, lane) * (DH ** -0.5)
    k_rot = _rope_head(k_ref[...], ctab, stab, lane)
    ks_ref[...] = k_rot
    kf_ref[...] = k_rot
    for gi, (_, d) in enumerate(DIL_PATTERNS):
        rows = tt // d
        for src, dst in ((qs_ref, outs[3 * gi]), (ks_ref, outs[3 * gi + 1]), (v_ref, outs[3 * gi + 2])):
            for r in range(d):
                dst[0, 0, r] = src[pl.ds(r, rows, stride=d), :].astype(bf16)


def rope_split(proj, ctab, stab, bsz, seq, tt=512):
    m = proj.shape[0]
    nt = seq // tt
    col = lambda c: pl.BlockSpec((tt, DH), lambda i, h, c=c: (i, c * HEADS + h))
    tab = pl.BlockSpec((tt, DH), lambda i, h: (i % nt, 0))
    out_specs, out_shape = [], []
    for (_, d) in DIL_PATTERNS:
        for _ in range(3):
            out_specs.append(pl.BlockSpec((1, 1, d, tt // d, DH), lambda i, h: (i // nt, h, 0, i % nt, 0)))
            out_shape.append(jax.ShapeDtypeStruct((bsz, HEADS, d, seq // d, DH), bf16))
    out_specs.append(pl.BlockSpec((tt, DH), lambda i, h: (i, h)))
    out_shape.append(jax.ShapeDtypeStruct((m, ATT_W), f32))
    res = pl.pallas_call(
        _rope_split_kernel,
        grid=(m // tt, HEADS),
        in_specs=[col(2), col(3), col(4), tab, tab],
        out_specs=out_specs,
        out_shape=out_shape,
        scratch_shapes=[pltpu.VMEM((tt, DH), f32), pltpu.VMEM((tt, DH), f32)],
        compiler_params=_cparams("arbitrary", "arbitrary"),
        name="rope_split",
    )(proj, proj, proj, ctab, stab)
    return [res[3 * gi:3 * gi + 3] for gi in range(len(DIL_PATTERNS))], res[-1]


def _dil_kernel(q_ref, kc_ref, kp_ref, vc_ref, vp_ref, o_ref, l_ref, s_ref, p_ref, *, span):
    tq = q_ref.shape[3]
    blk = pl.program_id(2)
    qi = _iota((tq, 2 * tq), 0)
    col = _iota((tq, 2 * tq), 1)
    rel = jnp.where(col < tq, qi - col, qi - col + 2 * tq)
    ok = (rel >= 0) & (rel <= span) & ((col < tq) | (blk > 0))
    for hb in range(HEADS):
        qh = q_ref[0, hb, 0]
        s_ref[hb, :, :tq] = _dot_nt(qh, kc_ref[0, hb, 0])
        s_ref[hb, :, tq:] = _dot_nt(qh, kp_ref[0, hb, 0])
    for hb in range(HEADS):
        s = jnp.where(ok, s_ref[hb], NEG_INF)
        mx = jnp.max(s, axis=1, keepdims=True)
        p = jnp.exp(s - mx)
        den = jnp.sum(p, axis=1, keepdims=True)
        p_ref[hb] = (p * (1.0 / den)).astype(bf16)
        l_ref[0, hb, 0] = jnp.broadcast_to(mx + jnp.log(den), (tq, DH))
    for hb in range(HEADS):
        o_ref[0, hb, 0] = (_dot(p_ref[hb, :, :tq], vc_ref[0, hb, 0])
                           + _dot(p_ref[hb, :, tq:], vp_ref[0, hb, 0]))


def _dil_merge_kernel(*refs):
    n_pat = len(DIL_PATTERNS)
    ins, o_ref, scr = refs[:2 * n_pat], refs[2 * n_pat], refs[2 * n_pat + 1:]
    tt = o_ref.shape[0]
    for gi, (_, d) in enumerate(DIL_PATTERNS):
        rows = tt // d
        for src, dst in ((ins[2 * gi], scr[2 * gi]), (ins[2 * gi + 1], scr[2 * gi + 1])):
            for r in range(d):
                dst[pl.ds(r, rows, stride=d), :] = src[0, 0, r]
    lses = [scr[2 * gi + 1][...] for gi in range(n_pat)]
    top = functools.reduce(jnp.maximum, lses)
    es = [jnp.exp(l - top) for l in lses]
    num = sum(scr[2 * gi][...] * es[gi] for gi in range(n_pat))
    o_ref[...] = num / sum(es)


def dilated_prompt(qkv_by_pattern, bsz, seq, tq=128, tt=512):
    partial = []
    for (w, d), (qd, kd, vd) in zip(DIL_PATTERNS, qkv_by_pattern):
        sd = seq // d
        blk = (1, HEADS, 1, tq, DH)
        cur = pl.BlockSpec(blk, lambda b, r, i: (b, 0, r, i, 0))
        prev = pl.BlockSpec(blk, lambda b, r, i: (b, 0, r, jnp.maximum(i - 1, 0), 0))
        partial += pl.pallas_call(
            functools.partial(_dil_kernel, span=w // d),
            grid=(bsz, d, sd // tq),
            in_specs=[cur, cur, prev, cur, prev],
            out_specs=[cur, cur],
            out_shape=[jax.ShapeDtypeStruct((bsz, HEADS, d, sd, DH), f32)] * 2,
            scratch_shapes=[pltpu.VMEM((HEADS, tq, 2 * tq), f32), pltpu.VMEM((HEADS, tq, 2 * tq), bf16)],
            compiler_params=_cparams("arbitrary", "arbitrary", "arbitrary"),
            name=f"dilattn_d{d}",
        )(qd, kd, kd, vd, vd)
    nt = seq // tt
    in_specs = []
    for (_, d) in DIL_PATTERNS:
        in_specs += [pl.BlockSpec((1, 1, d, tt // d, DH), lambda i, h: (i // nt, h, 0, i % nt, 0))] * 2
    return pl.pallas_call(
        _dil_merge_kernel,
        grid=(bsz * nt, HEADS),
        in_specs=in_specs,
        out_specs=pl.BlockSpec((tt, DH), lambda i, h: (i, h)),
        out_shape=jax.ShapeDtypeStruct((bsz * seq, ATT_W), f32),
        scratch_shapes=[pltpu.VMEM((tt, DH), f32)] * (2 * len(DIL_PATTERNS)),
        compiler_params=_cparams("arbitrary", "arbitrary"),
        name="dilattn_merge",
    )(*partial)


def _dil_step_kernel(q_ref, kn_ref, vn_ref, k1_ref, k4_ref, k16_ref, v1_ref, v4_ref, v16_ref, o_ref):
    q = q_ref[0]
    kn, vn = kn_ref[0], vn_ref[0]
    s_self = _rowsum_bcast(q * kn)
    o_gs, lse_gs = [], []
    for k_ref, v_ref in ((k1_ref, v1_ref), (k4_ref, v4_ref), (k16_ref, v16_ref)):
        k3 = k_ref[0, :, 0]
        nk = k3.shape[0]
        s = _rowsum_bcast((k3 * q[None]).reshape(nk * HEADS, DH)).reshape(nk, HEADS, DH)
        mx = jnp.maximum(jnp.max(s, axis=0), s_self)
        p = jnp.exp(s - mx[None])
        p_self = jnp.exp(s_self - mx)
        den = jnp.sum(p, axis=0) + p_self
        num = jnp.sum(p * v_ref[0, :, 0], axis=0) + p_self * vn
        o_gs.append(num / den)
        lse_gs.append(mx + jnp.log(den))
    top = jnp.maximum(jnp.maximum(lse_gs[0], lse_gs[1]), lse_gs[2])
    es = [jnp.exp(l - top) for l in lse_gs]
    tot = es[0] + es[1] + es[2]
    o_ref[0] = (o_gs[0] * es[0] + o_gs[1] * es[1] + o_gs[2] * es[2]) / tot


def dilated_step(q4, k4, v4, cache_k, cache_v):
    n, win = cache_k.shape[0], cache_k.shape[1]
    one = pl.BlockSpec((1, HEADS, DH), lambda b: (b, 0, 0))
    args = [q4, k4, v4]
    in_specs = [one, one, one]
    for cache in (cache_k, cache_v):
        for (w, d) in DIL_PATTERNS:
            nkeys = w // d
            assert win % d == 0 and (win // d) % nkeys == 0 and (win - w) % (d * nkeys) == 0
            args.append(cache.reshape(n, win // d, d, HEADS, DH))
            in_specs.append(pl.BlockSpec((1, nkeys, 1, HEADS, DH),
                                         lambda b, blk=(win - w) // d // nkeys: (b, blk, 0, 0, 0)))
    return pl.pallas_call(
        _dil_step_kernel,
        grid=(n,),
        in_specs=in_specs,
        out_specs=one,
        out_shape=jax.ShapeDtypeStruct((n, HEADS, DH), f32),
        compiler_params=_cparams("arbitrary"),
        name="dilattn_step",
    )(*args).reshape(n, ATT_W)


def _sb_kernel(bias_ref, q_ref, k_ref, v_ref, o_ref, kb_ref, vb_ref, qs_ref, t_ref, spb_ref, wb_ref,
               acc_ref, run_ref):
    tq = q_ref.shape[1]
    nh = q_ref.shape[2] // DH
    kt_w = t_ref.shape[2]
    sub = PAGE
    hg = pl.program_id(1)
    qb = pl.program_id(2)

    @pl.when(qb == 0)
    def _():
        kb_ref[...] = k_ref[0].astype(bf16)
        vb_ref[...] = v_ref[0].astype(bf16)

    rr = _iota((sub, 2 * sub), 0)
    cc = _iota((sub, 2 * sub), 1)
    tri = jnp.where((rr > cc) | (cc >= sub), 1.0, 0.0).astype(bf16)
    for h in range(nh):
        qs_ref[h] = (q_ref[0, :, h * DH:(h + 1) * DH] * (DH ** -0.5)).astype(bf16)
    acc_ref[...] = jnp.zeros_like(acc_ref)
    run_ref[...] = jnp.zeros_like(run_ref)

    def macro(start, masked):
        for h in range(nh):
            hs = slice(h * DH, (h + 1) * DH)
            z = _dot_nt(qs_ref[h], kb_ref[pl.ds(start, kt_w), hs]) + bias_ref[hg * nh + h]
            sp = _softplus(z)
            t = z - sp
            if masked:
                ok = start + _iota((tq, kt_w), 1) < qb * tq + _iota((tq, kt_w), 0)
                sp = jnp.where(ok, sp, 0.0)
                t = jnp.where(ok, t, NEG_INF)
            t_ref[h] = t
            spb_ref[h] = sp.astype(bf16)
        for h in range(nh):
            run = run_ref[h]
            for kt in range(kt_w // sub - 1, -1, -1):
                ks = slice(kt * sub, (kt + 1) * sub)
                cs = _dot(spb_ref[h, :, ks], tri)
                wb_ref[h, :, ks] = jnp.exp(t_ref[h, :, ks] - cs[:, :sub] - run).astype(bf16)
                run = run + cs[:, sub:]
            run_ref[h] = run
        for h in range(nh):
            hs = slice(h * DH, (h + 1) * DH)
            acc_ref[h] += _dot(wb_ref[h], vb_ref[pl.ds(start, kt_w), hs])

    top = (qb * tq) // kt_w
    macro(pl.multiple_of(top * kt_w, kt_w), True)

    def body(it, _):
        macro(pl.multiple_of((top - 1 - it) * kt_w, kt_w), False)
        return 0

    lax.fori_loop(0, top, body, 0)
    for h in range(nh):
        o_ref[0, :, h * DH:(h + 1) * DH] = acc_ref[h]


def sb_prompt(proj2, bias, bsz, seq, tq=256, nh=4, kt_w=512):
    assert kt_w % tq == 0 and seq % kt_w == 0
    p3 = proj2.reshape(bsz, seq, proj2.shape[1])
    nq = seq // tq
    ng = HEADS // nh
    wd = nh * DH
    return pl.pallas_call(
        _sb_kernel,
        grid=(bsz, ng, nq),
        in_specs=[
            pl.BlockSpec(memory_space=pltpu.SMEM),
            pl.BlockSpec((1, tq, wd), lambda b, g, i: (b, i, g)),
            pl.BlockSpec((1, seq, wd), lambda b, g, i: (b, 0, ng + g)),
            pl.BlockSpec((1, seq, wd), lambda b, g, i: (b, 0, 2 * ng + g)),
        ],
        out_specs=pl.BlockSpec((1, tq, wd), lambda b, g, i: (b, i, g)),
        out_shape=jax.ShapeDtypeStruct((bsz, seq, ATT_W), f32),
        scratch_shapes=[
            pltpu.VMEM((seq, wd), bf16), pltpu.VMEM((seq, wd), bf16),
            pltpu.VMEM((nh, tq, DH), bf16),
            pltpu.VMEM((nh, tq, kt_w), f32), pltpu.VMEM((nh, tq, kt_w), bf16),
            pltpu.VMEM((nh, tq, kt_w), bf16),
            pltpu.VMEM((nh, tq, DH), f32), pltpu.VMEM((nh, tq, PAGE), f32),
        ],
        compiler_params=_cparams("arbitrary", "arbitrary", "arbitrary"),
        name="sb_prompt",
    )(bias, p3, p3, p3).reshape(bsz * seq, ATT_W)


def _sb_step_kernel(pt_ref, q_ref, bias_ref, eye_ref, *refs, npg):
    k_refs, v_refs = refs[:npg], refs[npg:2 * npg]
    o_ref, acc_ref, carry_ref = refs[2 * npg:]
    j = pl.program_id(1)

    @pl.when(j == 0)
    def _():
        acc_ref[...] = jnp.zeros_like(acc_ref)
        carry_ref[...] = jnp.zeros_like(carry_ref)

    q = q_ref[0]
    bias = bias_ref[...]
    ones = jnp.ones((DH, DH), bf16)
    rr = _iota((PAGE, 2 * PAGE), 0)
    cc = _iota((PAGE, 2 * PAGE), 1)
    tri = jnp.where((rr > cc) | (cc >= PAGE), 1.0, 0.0).astype(bf16)
    acc = acc_ref[...]
    run = carry_ref[...]
    pages = range(npg)
    zbs = [_dot((k_refs[p][0] * q[None]).reshape(PAGE * HEADS, DH).astype(bf16), ones)
           .reshape(PAGE, HEADS, DH) for p in pages]
    zs = [jnp.sum(zb * eye_ref[...], axis=0) + bias for zb in zbs]
    sps = [_softplus(z) for z in zs]
    css = []
    for sp in sps:
        hi, mid, lo = _split3(sp)
        css.append(_dot(hi, tri) + _dot(mid, tri) + _dot(lo, tri))
    ws = []
    for z, sp, cs in zip(zs, sps, css):
        ws.append(jnp.exp(z - sp - cs[:, :PAGE] - run))
        run = run + cs[:, PAGE:]
    wbs = [_dot((eye_ref[...] * w[None]).reshape(PAGE * HEADS, DH).astype(bf16), ones)
           .reshape(PAGE, HEADS, DH) for w in ws]
    for p, wb in zip(pages, wbs):
        acc = acc + jnp.sum(wb * v_refs[p][0], axis=0)
    acc_ref[...] = acc
    carry_ref[...] = run

    @pl.when(j == pl.num_programs(1) - 1)
    def _():
        o_ref[0] = acc


def sb_step(q_s, bias, cache_k, cache_v, page_table, npg=4):
    n, n_pages = page_table.shape
    q4 = (q_s * (DH ** -0.5)).reshape(n, HEADS, DH)
    bias4 = jnp.broadcast_to(bias[:, None], (HEADS, DH))
    eye3 = jnp.broadcast_to(jnp.eye(PAGE, DH, dtype=f32)[:, None, :], (PAGE, HEADS, DH))

    def page(p):
        return lambda b, j, pt: (pt[b, n_pages - 1 - (j * npg + p)], 0, 0, 0)

    kv_specs = [pl.BlockSpec((1, PAGE, HEADS, DH), page(p)) for p in range(npg)]
    grid_spec = pltpu.PrefetchScalarGridSpec(
        num_scalar_prefetch=1,
        grid=(n, n_pages // npg),
        in_specs=[
            pl.BlockSpec((1, HEADS, DH), lambda b, j, pt: (b, 0, 0)),
            pl.BlockSpec((HEADS, DH), lambda b, j, pt: (0, 0)),
            pl.BlockSpec((PAGE, HEADS, DH), lambda b, j, pt: (0, 0, 0)),
        ] + kv_specs + kv_specs,
        out_specs=pl.BlockSpec((1, HEADS, DH), lambda b, j, pt: (b, 0, 0)),
        scratch_shapes=[pltpu.VMEM((HEADS, DH), f32), pltpu.VMEM((HEADS, DH), f32)],
    )
    return pl.pallas_call(
        functools.partial(_sb_step_kernel, npg=npg),
        grid_spec=grid_spec,
        out_shape=jax.ShapeDtypeStruct((n, HEADS, DH), f32),
        compiler_params=_cparams("arbitrary", "arbitrary"),
        name="sb_step",
    )(page_table, q4, bias4, eye3, *([cache_k] * npg), *([cache_v] * npg)).reshape(n, ATT_W)


def _mlstm_kernel(q_ref, k_ref, v_ref, og_ref, gate_ref, gb_ref, gn_ref,
                  h_ref, c_out, n_out, m_out, c_s, n_s, m_s):
    ci = pl.program_id(1)
    L = CHUNK

    @pl.when(ci == 0)
    def _():
        c_s[...] = jnp.zeros_like(c_s)
        n_s[...] = jnp.zeros_like(n_s)
        m_s[...] = jnp.zeros_like(m_s)

    gt = gate_ref[...] + gb_ref[...]
    gt_t = gt.T
    ri = _iota((L, L), 0)
    li = _iota((L, L), 1)
    causal = li <= ri
    tri_incl = jnp.where(causal, 1.0, 0.0).astype(bf16)
    tri_incl_t = jnp.where(ri <= li, 1.0, 0.0).astype(bf16)
    for h in range(M_HEADS):
        sl = slice(h * M_DH, (h + 1) * M_DH)
        ig_col = gt[:, h:h + 1]
        ig_row = gt_t[h:h + 1, :]
        lf_col = _log_sigmoid_pair(gt[:, M_HEADS + h:M_HEADS + h + 1])[0]
        lf_row = _log_sigmoid_pair(gt_t[M_HEADS + h:M_HEADS + h + 1, :])[0]
        c_hi, c_mid, c_lo = _split3(jnp.broadcast_to(lf_col, (L, L)))
        bcum_col = _dot(tri_incl, c_hi) + _dot(tri_incl, c_mid) + _dot(tri_incl, c_lo)
        r_hi, r_mid, r_lo = _split3(jnp.broadcast_to(lf_row, (L, L)))
        bcum_row = _dot(r_hi, tri_incl_t) + _dot(r_mid, tri_incl_t) + _dot(r_lo, tri_incl_t)
        m_prev = m_s[h:h + 1, :]
        dlog = jnp.where(causal, bcum_col - bcum_row + ig_row, NEG_INF)
        inter = bcum_col + m_prev
        m_t = jnp.maximum(inter, jnp.max(dlog, axis=1, keepdims=True))
        dw = jnp.exp(dlog - m_t)
        iw = jnp.exp(inter - m_t)
        qh = q_ref[:, sl]
        kh = k_ref[:, sl] * (M_DH ** -0.5)
        vh = v_ref[:, sl]
        qb, kb, vb = qh.astype(bf16), kh.astype(bf16), vh.astype(bf16)
        sw = dw * _dot_nt(qb, kb)
        c_prev = c_s[h]
        n_prev = n_s[h:h + 1, :]
        iw_col = iw[:, 0:1]
        num = _dot(sw.astype(bf16), vb) + iw_col * _dot_nt(qb, c_prev.astype(bf16))
        qn = jnp.sum(qb.astype(f32) * n_prev.astype(bf16).astype(f32), axis=1, keepdims=True)
        den = jnp.sum(sw, axis=1, keepdims=True) + iw_col * qn
        m_col = m_t[:, 0:1]
        hout = num / jnp.maximum(jnp.abs(den), jnp.exp(-m_col))
        m_last = m_t[L - 1:L, :]
        b_last = bcum_col[L - 1:L, :]
        wl_col = jnp.exp(b_last[:, 0:1] - bcum_col[:, 0:1] + ig_col - m_last[:, 0:1])
        wl_row = jnp.exp(b_last - bcum_row[0:1, :] + ig_row - m_last)
        decay = jnp.exp(b_last + m_prev - m_last)
        dsc = decay[:, 0:1]
        c_s[h] = dsc * c_prev + _dot((vh * wl_col).T.astype(bf16), kb)
        wl8 = jnp.broadcast_to(wl_row, (8, L)).astype(bf16)
        n_s[h:h + 1, :] = dsc * n_prev + _dot(wl8, kb)[0:1, :]
        m_s[h:h + 1, :] = m_last
        hn = hout * lax.rsqrt(jnp.mean(hout * hout, axis=1, keepdims=True) + EPS)
        h_ref[:, sl] = hn * gn_ref[:, sl] * _sigmoid(og_ref[:, sl])

    @pl.when(ci == pl.num_programs(1) - 1)
    def _():
        c_out[0] = c_s[...]
        n_out[0] = n_s[...]
        m_out[0] = m_s[...]


def mlstm_prompt(proj2, gates, gate_bias, g_mnorm, bsz, seq):
    nc = seq // CHUNK
    col = lambda c: pl.BlockSpec((CHUNK, M_WIDTH), lambda b, i, c=c: (b * nc + i, c))
    hm, c1, n1, m1 = pl.pallas_call(
        _mlstm_kernel,
        grid=(bsz, nc),
        in_specs=[
            col(3), col(4), col(5), col(6),
            pl.BlockSpec((CHUNK, LANES), lambda b, i: (b * nc + i, 0)),
            pl.BlockSpec((1, LANES), lambda b, i: (0, 0)),
            pl.BlockSpec((1, M_WIDTH), lambda b, i: (0, 0)),
        ],
        out_specs=[
            pl.BlockSpec((CHUNK, M_WIDTH), lambda b, i: (b * nc + i, 0)),
            pl.BlockSpec((1, M_HEADS, M_DH, M_DH), lambda b, i: (b, 0, 0, 0)),
            pl.BlockSpec((1, M_HEADS, M_DH), lambda b, i: (b, 0, 0)),
            pl.BlockSpec((1, M_HEADS, LANES), lambda b, i: (b, 0, 0)),
        ],
        out_shape=[
            jax.ShapeDtypeStruct((bsz * seq, M_WIDTH), f32),
            jax.ShapeDtypeStruct((bsz, M_HEADS, M_DH, M_DH), f32),
            jax.ShapeDtypeStruct((bsz, M_HEADS, M_DH), f32),
            jax.ShapeDtypeStruct((bsz, M_HEADS, LANES), f32),
        ],
        scratch_shapes=[
            pltpu.VMEM((M_HEADS, M_DH, M_DH), f32),
            pltpu.VMEM((M_HEADS, M_DH), f32),
            pltpu.VMEM((M_HEADS, LANES), f32),
        ],
        compiler_params=_cparams("arbitrary", "arbitrary"),
        name="mlstm_prompt",
    )(proj2, proj2, proj2, proj2, gates, gate_bias, g_mnorm)
    return hm, c1, n1, m1[:, :, 0]


def _mlstm_step_kernel(q_ref, k_ref, v_ref, og_ref, gn_ref, ig_ref, fg_ref, c_ref, n_ref, m_ref,
                       h_ref, c_out, n_out, m_out):
    q = q_ref[0, 0]
    k = k_ref[0, 0] * (M_DH ** -0.5)
    v = v_ref[0, 0]
    ig = ig_ref[0, 0]
    lf = _log_sigmoid_pair(fg_ref[0, 0])[0]
    m0 = m_ref[0, 0]
    c0 = c_ref[0, 0]
    n0 = n_ref[0, 0]
    inter = lf + m0
    m_t = jnp.maximum(inter, ig)
    dw = jnp.exp(ig - m_t)
    iw = jnp.exp(inter - m_t)
    rnd = lambda a: a.astype(bf16).astype(f32)
    qr, kr, vr = rnd(q), rnd(k), rnd(v)
    qk = jnp.sum(qr * kr, axis=1, keepdims=True)
    sw = dw * qk
    cq = jnp.sum(rnd(c0) * qr, axis=1, keepdims=True)
    num = rnd(sw) * vr + iw * cq
    den = sw + iw * jnp.sum(rnd(n0) * qr, axis=1, keepdims=True)
    hout = num / jnp.maximum(jnp.abs(den), jnp.exp(-m_t))
    c_out[0, 0] = iw * c0 + rnd(dw * v) * kr
    n_out[0, 0] = iw * n0 + rnd(dw) * kr
    m_out[0, 0] = m_t
    hn = hout * lax.rsqrt(jnp.mean(hout * hout, axis=0, keepdims=True) + EPS)
    h_ref[0, 0] = hn * gn_ref[0] * _sigmoid(og_ref[0, 0])


def mlstm_step(proj2_s, gates_s, gate_bias, g_mnorm, c0, n0, m0):
    n = proj2_s.shape[0]
    base = 3 * ATT_W
    seg = lambda i: proj2_s[:, base + i * M_WIDTH: base + (i + 1) * M_WIDTH].reshape(n, M_HEADS, M_DH)
    g = gates_s + gate_bias
    rowb = pl.BlockSpec((1, 1, 1, M_DH), lambda b, h: (b, h, 0, 0))
    colb = pl.BlockSpec((1, 1, M_DH, 1), lambda b, h: (b, h, 0, 0))
    scal = pl.BlockSpec((1, 1, 1, 1), lambda b, h: (b, h, 0, 0))
    hcol, c1, n1, m1 = pl.pallas_call(
        _mlstm_step_kernel,
        grid=(n, M_HEADS),
        in_specs=[
            rowb, rowb, colb, colb,
            pl.BlockSpec((1, M_DH, 1), lambda b, h: (h, 0, 0)),
            scal, scal,
            pl.BlockSpec((1, 1, M_DH, M_DH), lambda b, h: (b, h, 0, 0)),
            rowb, scal,
        ],
        out_specs=[colb, pl.BlockSpec((1, 1, M_DH, M_DH), lambda b, h: (b, h, 0, 0)), rowb, scal],
        out_shape=[
            jax.ShapeDtypeStruct((n, M_HEADS, M_DH, 1), f32),
            jax.ShapeDtypeStruct((n, M_HEADS, M_DH, M_DH), f32),
            jax.ShapeDtypeStruct((n, M_HEADS, 1, M_DH), f32),
            jax.ShapeDtypeStruct((n, M_HEADS, 1, 1), f32),
        ],
        compiler_params=_cparams("arbitrary", "arbitrary"),
        name="mlstm_step",
    )(seg(0)[:, :, None, :], seg(1)[:, :, None, :], seg(2)[..., None], seg(3)[..., None],
      g_mnorm.reshape(M_HEADS, M_DH, 1),
      g[:, 0:M_HEADS].reshape(n, M_HEADS, 1, 1), g[:, M_HEADS:2 * M_HEADS].reshape(n, M_HEADS, 1, 1),
      c0, n0[:, :, None, :], m0.reshape(n, M_HEADS, 1, 1))
    return hcol.reshape(n, M_WIDTH), c1, n1.reshape(n, M_HEADS, M_DH), m1.reshape(n, M_HEADS)


def _peerq_kernel(x_ref, g_ref, sh_ref, sc_ref, wt_ref, qt_ref, xmt_ref, xs_ref):
    @pl.when(pl.program_id(1) == 0)
    def _():
        xm = _modulated(x_ref[...], g_ref[...], sh_ref[0], sc_ref[0])
        xt = xm.T.astype(bf16)
        xs_ref[...] = xt
        xmt_ref[...] = xt

    qt_ref[...] = _dot(wt_ref[...], xs_ref[...])


def peer_query(x, g, shift, scale, wq_t, rows_per_batch, tm, tn=512):
    m, k = x.shape
    n = wq_t.shape[0]
    sh, sh_spec = _mod_specs(shift, m, tm, k, rows_per_batch)
    sc, sc_spec = _mod_specs(scale, m, tm, k, rows_per_batch)
    return pl.pallas_call(
        _peerq_kernel,
        grid=(m // tm, n // tn),
        in_specs=[
            pl.BlockSpec((tm, k), lambda i, j: (i, 0)),
            pl.BlockSpec((1, k), lambda i, j: (0, 0)),
            sh_spec, sc_spec,
            pl.BlockSpec((tn, k), lambda i, j: (j, 0)),
        ],
        out_specs=[pl.BlockSpec((tn, tm), lambda i, j: (j, i)),
                   pl.BlockSpec((k, tm), lambda i, j: (0, i))],
        out_shape=[jax.ShapeDtypeStruct((n, m), f32), jax.ShapeDtypeStruct((k, m), bf16)],
        scratch_shapes=[pltpu.VMEM((k, tm), bf16)],
        compiler_params=_cparams("arbitrary", "arbitrary"),
        name="peer_query",
    )(x, g.reshape(1, k), sh, sc, wq_t)


def _heads_layout_kernel(*refs):
    n = len(refs) // 2
    for x_ref, o_ref in zip(refs[:n], refs[n:]):
        for hb in range(HEADS):
            o_ref[:, hb, :] = x_ref[:, hb * DH:(hb + 1) * DH]


def heads_layout(srcs, bsz, seq, tail, tt=512):
    nt, first = tail // tt, (seq - tail) // tt
    rows = lambda b, i: b * (seq // tt) + first + i
    return pl.pallas_call(
        _heads_layout_kernel,
        grid=(bsz, nt),
        in_specs=[pl.BlockSpec((tt, ATT_W), lambda b, i, c=c: (rows(b, i), c)) for _, c in srcs],
        out_specs=[pl.BlockSpec((tt, HEADS, DH), lambda b, i: (b * nt + i, 0, 0))] * len(srcs),
        out_shape=[jax.ShapeDtypeStruct((bsz * tail, HEADS, DH), f32)] * len(srcs),
        compiler_params=_cparams("arbitrary", "arbitrary"),
        name="heads_layout",
    )(*[x for x, _ in srcs])


def _cast_kernel(w_ref, o_ref):
    o_ref[...] = w_ref[0].astype(bf16)


def cast_layer_bf16(w, layer, tr=1024):
    _, rows, cols = w.shape
    return pl.pallas_call(
        _cast_kernel,
        grid=(rows // tr,),
        in_specs=[pl.BlockSpec((1, tr, cols), lambda i: (layer, i, 0))],
        out_specs=pl.BlockSpec((tr, cols), lambda i: (i, 0)),
        out_shape=jax.ShapeDtypeStruct((rows, cols), bf16),
        compiler_params=_cparams("arbitrary"),
        name="cast_bf16",
    )(w)


_N_TOP = PEER_TOPK + 1
_CAND_PAIRS = [(a, b) for a in range(_N_TOP) for b in range(_N_TOP) if (a + 1) * (b + 1) <= _N_TOP]
_N_CAND = -(-len(_CAND_PAIRS) // 8) * 8


def _extract_top(cur, ridx, n):
    vals = []
    big = float(cur.shape[0])
    for _ in range(n):
        mx = jnp.max(cur, axis=0, keepdims=True)
        first = jnp.min(jnp.where(cur == mx, ridx, big), axis=0, keepdims=True)
        cur = jnp.where(ridx == first, NEG_INF, cur)
        vals.append(mx)
    return vals


_ROUTER_UNROLL = 8


def _router_kernel(qt_ref, keys_ref, e1_ref, e2_ref, th_ref, cand_ref):
    ridx = _iota((N_KEYS, LANES), 0).astype(f32)
    cidx = _iota((_N_CAND, LANES), 0).astype(f32)
    k0 = keys_ref[0].astype(bf16)
    k1 = keys_ref[1].astype(bf16)
    cand_ref[...] = jnp.full(cand_ref.shape, NEG_INF, f32)

    def one_head(h, slot):
        r0 = pl.multiple_of(h * 2 * N_KEYS, 2 * N_KEYS)
        s1 = _dot(k0, qt_ref[pl.ds(r0, N_KEYS), :].astype(bf16))
        s2 = _dot(k1, qt_ref[pl.ds(r0 + N_KEYS, N_KEYS), :].astype(bf16))
        top_a = _extract_top(s1, ridx, _N_TOP)
        top_b = _extract_top(s2, ridx, _N_TOP)
        for ci, (a, b) in enumerate(_CAND_PAIRS):
            cand_ref[slot, ci:ci + 1, :] = top_a[a] + top_b[b]
        cs = _extract_top(cand_ref[slot], cidx, _N_TOP)
        z = jnp.ones_like(cs[0])
        for r in range(1, PEER_TOPK):
            z = z + jnp.exp(cs[r] - cs[0])
        inv_z = 1.0 / z
        mid = 0.5 * (cs[PEER_TOPK - 1] + cs[PEER_TOPK])
        o0 = pl.multiple_of(h * N_KEYS, N_KEYS)
        e1 = jnp.exp(s1 - top_a[0]) * inv_z
        e1_ref[:, h] = e1.reshape(N_KEYS // 8, 8, LANES)
        e2_ref[pl.ds(o0, N_KEYS), :] = jnp.exp(s2 - top_b[0])
        th_ref[pl.ds(h, 1), :] = jnp.exp(mid - cs[0]) * inv_z

    def body(hp, _):
        for slot in range(_ROUTER_UNROLL):
            one_head(hp * _ROUTER_UNROLL + slot, slot)
        return 0

    lax.fori_loop(0, PEER_HEADS // _ROUTER_UNROLL, body, 0)


def peer_router(qt, keys):
    n, m = qt.shape
    half = PEER_HEADS * N_KEYS
    return pl.pallas_call(
        _router_kernel,
        grid=(m // LANES,),
        in_specs=[pl.BlockSpec((n, LANES), lambda i: (0, i)),
                  pl.BlockSpec((2, N_KEYS, N_KEYS), lambda i: (0, 0, 0))],
        out_specs=[pl.BlockSpec((N_KEYS // 8, PEER_HEADS, 8, LANES), lambda i: (0, 0, 0, i)),
                   pl.BlockSpec((half, LANES), lambda i: (0, i)),
                   pl.BlockSpec((PEER_HEADS, LANES), lambda i: (0, i))],
        out_shape=[jax.ShapeDtypeStruct((N_KEYS // 8, PEER_HEADS, 8, m), f32),
                   jax.ShapeDtypeStruct((half, m), f32),
                   jax.ShapeDtypeStruct((PEER_HEADS, m), f32)],
        scratch_shapes=[pltpu.VMEM((_ROUTER_UNROLL, _N_CAND, LANES), f32)],
        compiler_params=_cparams("arbitrary"),
        name="peer_router",
    )(qt, keys)


def _experts_kernel(xmt_ref, e1_ref, e2_ref, th_ref, u_ref, v_ref, x_ref, gt_ref, gf_ref,
                    o_ref, w_ref, *, final_norm):
    c = pl.program_id(1)
    tm, te = w_ref.shape
    ni = te // N_KEYS

    @pl.when(c == 0)
    def _():
        o_ref[...] = jnp.zeros_like(o_ref)

    act = _gelu(_dot(u_ref[...], xmt_ref[...]))
    for ii in range(ni):
        es = slice(ii * N_KEYS, (ii + 1) * N_KEYS)
        for lc in range(tm // LANES):
            ls = slice(lc * LANES, (lc + 1) * LANES)
            g = jnp.zeros((N_KEYS, LANES), f32)
            for h in range(PEER_HEADS):
                row = e1_ref[ii // 8, h, ii % 8:ii % 8 + 1, ls]
                pr = e2_ref[h * N_KEYS:(h + 1) * N_KEYS, ls] * row
                g = g + jnp.where(pr >= th_ref[h:h + 1, ls], pr, 0.0)
            w_ref[ls, es] = (g * act[es, ls]).T.astype(bf16)
    o_ref[...] += _dot(w_ref[...], v_ref[...])

    @pl.when(c == pl.num_programs(1) - 1)
    def _():
        y = x_ref[...] + gt_ref[0] * o_ref[...]
        if final_norm:
            ms = jnp.mean(y * y, axis=-1, keepdims=True)
            y = y * lax.rsqrt(ms + EPS) * gf_ref[...]
        o_ref[...] = y


def peer_experts(xmt, e1t, e2t, th, u, v, x, gate, g_final, rows_per_batch, tm, te, final_norm):
    m, d = x.shape
    gt, gt_spec = _mod_specs(gate, m, tm, d, rows_per_batch)
    half = PEER_HEADS * N_KEYS
    return pl.pallas_call(
        functools.partial(_experts_kernel, final_norm=final_norm),
        grid=(m // tm, N_EXPERTS // te),
        in_specs=[
            pl.BlockSpec((d, tm), lambda i, c: (0, i)),
            pl.BlockSpec((te // N_KEYS // 8, PEER_HEADS, 8, tm), lambda i, c: (c, 0, 0, i)),
            pl.BlockSpec((half, tm), lambda i, c: (0, i)),
            pl.BlockSpec((PEER_HEADS, tm), lambda i, c: (0, i)),
            pl.BlockSpec((te, d), lambda i, c: (c, 0)),
            pl.BlockSpec((te, d), lambda i, c: (c, 0)),
            pl.BlockSpec((tm, d), lambda i, c: (i, 0)),
            gt_spec,
            pl.BlockSpec((1, d), lambda i, c: (0, 0)),
        ],
        out_specs=pl.BlockSpec((tm, d), lambda i, c: (i, 0)),
        out_shape=jax.ShapeDtypeStruct((m, d), f32),
        scratch_shapes=[pltpu.VMEM((tm, te), bf16)],
        compiler_params=_cparams("arbitrary", "arbitrary"),
        name="peer_experts",
    )(xmt, e1t, e2t, th, u, v, x, gt, g_final.reshape(1, d))


def peer_block(x, g, shift, scale, gate, wq_t, keys, u, v, g_final, rows_per_batch, tm, te, final_norm):
    qt, xmt = peer_query(x, g, shift, scale, wq_t, rows_per_batch, tm)
    e1t, e2t, th = peer_router(qt, keys)
    return peer_experts(xmt, e1t, e2t, th, u, v, x, gate, g_final, rows_per_batch, tm, te, final_norm)


def kernel(x_prompt, x_sample, c_prompt, c_sample, state_rglru_conv, state_rglru_h, cache_swa_k, cache_swa_v, cache_sb_k, cache_sb_v, state_mlstm_C, state_mlstm_n, state_mlstm_m, page_table, w_ada, b_ada, g_norm_mix, g_norm_ffn, e_w_in, e_conv_w, e_conv_b, e_w_r, e_b_r, e_w_i, e_b_i, e_lambda, e_w_out, o_w_in, o_b_if, o_sb_bias, o_g_mnorm, o_w_out, peer_w_q, peer_keys, peer_u, peer_v, g_final):
    bp, seq, d = x_prompt.shape
    bs = x_sample.shape[0]
    mp = bp * seq
    pad_s = LANES
    xp = x_prompt.reshape(mp, d)
    xs = x_sample.reshape(bs, d)

    c_rows = 16
    c_all = jnp.concatenate([c_prompt, c_sample, jnp.zeros((c_rows - bp - bs, d), f32)], axis=0)
    mod = adaln_all(c_all, w_ada, b_ada)

    def mods(layer):
        parts = [mod[layer, :, i * d:(i + 1) * d] for i in range(6)]
        return [p[:bp] for p in parts], [p[bp:bp + bs] for p in parts]

    ctab_p, stab_p = _rope_tables(jnp.arange(seq, dtype=jnp.int32))
    ctab_s, stab_s = _rope_tables(jnp.full((1,), PAST_LEN, jnp.int32))

    TM = 512
    m_p, m_s = mods(0)
    w_in = e_w_in[0].astype(bf16)
    w_out = e_w_out[0].astype(bf16)
    cw, cb = e_conv_w[0], e_conv_b[0].reshape(1, -1)
    wr, wi = e_w_r[0].astype(bf16), e_w_i[0].astype(bf16)
    br, bi, lam = e_b_r[0].reshape(1, -1), e_b_i[0].reshape(1, -1), e_lambda[0].reshape(1, -1)

    proj_p = mod_matmul(xp, g_norm_mix[0], m_p[0], m_p[1], w_in, E_IN, seq, 1024, 1024, name="e_in_p")
    proj_s = mod_matmul(xs, g_norm_mix[0], m_s[0], m_s[1], w_in, E_IN, 1, bs, 512, name="e_in_s")

    ya_p, h_p = rglru_prompt(proj_p, bp, seq, cw, cb, wr, br, wi, bi, lam)
    ya_s, h_s = rglru_step(proj_s, state_rglru_conv[0], state_rglru_h[0], cw, cb, wr, br, wi, bi, lam)
    conv_p = proj_p.reshape(bp, seq, E_IN)[:, seq - (CONV_W - 1):, :RG_WIDTH]
    conv_s = jnp.concatenate([state_rglru_conv[0][:, 1:], proj_s[:, None, :RG_WIDTH]], axis=1)

    qkv_p, k_p = rope_split(proj_p, ctab_p, stab_p, bp, seq)
    q_s, k_s, v_s = rope_qkv(proj_s, ctab_s, stab_s, 1, bs)
    o_p = dilated_prompt(qkv_p, bp, seq)
    win = cache_swa_k.shape[2]
    o_s = dilated_step(q_s, k_s, v_s, cache_swa_k.reshape(-1, win, HEADS, DH)[:bs],
                       cache_swa_v.reshape(-1, win, HEADS, DH)[:bs])
    wl = min(2048, seq)
    swa_k_p, swa_v_p = [a.reshape(bp, wl, HEADS, DH)
                        for a in heads_layout([(k_p, 0), (proj_p, E_IN // ATT_W - 1)], bp, seq, wl)]
    swa_k_s = k_s.reshape(bs, 1, HEADS, DH)
    swa_v_s = v_s.reshape(bs, 1, HEADS, DH)

    xp = out_proj(ya_p, o_p, w_out, xp, m_p[2], seq, 1024, 1024, name="e_out_p")
    xs = out_proj(ya_s, o_s, w_out, xs, m_s[2], 1, bs, 512, name="e_out_s")

    def peer_layer(layer, xp, xs, m_p, m_s, final_norm):
        wq_t = peer_w_q[layer].T.astype(bf16)
        u = cast_layer_bf16(peer_u, layer)
        v = cast_layer_bf16(peer_v, layer)
        xp = peer_block(xp, g_norm_ffn[layer], m_p[3], m_p[4], m_p[5], wq_t, peer_keys[layer], u, v,
                        g_final, seq, TM, 1024, final_norm)
        xs_pad = jnp.pad(xs, ((0, pad_s - bs), (0, 0)))
        xs_new = peer_block(xs_pad, g_norm_ffn[layer], m_s[3], m_s[4], m_s[5], wq_t, peer_keys[layer],
                            u, v, g_final, 1, pad_s, 1024, final_norm)
        return xp, xs_new[:bs]

    xp, xs = peer_layer(0, xp, xs, m_p, m_s, False)

    m_p, m_s = mods(1)
    w_in2 = o_w_in[0].astype(bf16)
    w_gate = jnp.pad(w_in2[:, O_MAIN:], ((0, 0), (0, LANES - 2 * M_HEADS)))
    w_out2 = o_w_out[0].astype(bf16)
    gate_bias = jnp.pad(o_b_if[0].reshape(1, 2 * M_HEADS), ((0, 0), (0, LANES - 2 * M_HEADS)))
    gmn = o_g_mnorm[0].reshape(1, M_WIDTH)

    proj2_p, gates_p = mod_matmul(xp, g_norm_mix[1], m_p[0], m_p[1], w_in2, O_MAIN, seq, 1024, 1024,
                                  w_gate=w_gate, name="o_in_p")
    proj2_s, gates_s = mod_matmul(xs, g_norm_mix[1], m_s[0], m_s[1], w_in2, O_MAIN, 1, bs, 512,
                                  w_gate=w_gate, name="o_in_s")

    oc_p = sb_prompt(proj2_p, o_sb_bias[0], bp, seq)
    oc_s = sb_step(proj2_s[:, :ATT_W], o_sb_bias[0], cache_sb_k.reshape(-1, PAGE, HEADS, DH),
                   cache_sb_v.reshape(-1, PAGE, HEADS, DH), page_table)
    hm_p, mC_p, mn_p, mm_p = mlstm_prompt(proj2_p, gates_p, gate_bias, gmn, bp, seq)
    hm_s, mC_s, mn_s, mm_s = mlstm_step(proj2_s, gates_s, gate_bias, gmn,
                                        state_mlstm_C.reshape(-1, M_HEADS, M_DH, M_DH)[:bs],
                                        state_mlstm_n[0], state_mlstm_m[0])

    n_pg = seq // PAGE
    sb_k_p, sb_v_p = [a.reshape(bp, n_pg, PAGE, HEADS, DH)
                      for a in heads_layout([(proj2_p, 1), (proj2_p, 2)], bp, seq, seq)]
    sb_k_s = proj2_s[:, ATT_W:2 * ATT_W].reshape(bs, 1, HEADS, DH)
    sb_v_s = proj2_s[:, 2 * ATT_W:3 * ATT_W].reshape(bs, 1, HEADS, DH)

    xp = out_proj(oc_p, hm_p, w_out2, xp, m_p[2], seq, 1024, 1024, name="o_out_p")
    xs = out_proj(oc_s, hm_s, w_out2, xs, m_s[2], 1, bs, 512, name="o_out_s")
    xp, xs = peer_layer(1, xp, xs, m_p, m_s, True)

    y_prompt = xp.reshape(bp, seq, d)
    y_sample = xs.reshape(bs, 1, d)
    st = lambda a: a[None]
    return (y_prompt, y_sample, st(conv_p), st(conv_s), st(h_p.reshape(bp, RG_WIDTH)), st(h_s),
            st(swa_k_p), st(swa_k_s), st(swa_v_p), st(swa_v_s),
            st(sb_k_p), st(sb_k_s), st(sb_v_p), st(sb_v_s),
            st(mC_p), st(mC_s), st(mn_p), st(mn_s), st(mm_p), st(mm_s))
```

```python
import functools
import math

import jax
import jax.numpy as jnp
from jax import lax
from jax.experimental import pallas as pl
from jax.experimental.pallas import tpu as pltpu

f32 = jnp.float32
bf16 = jnp.bfloat16

D_MODEL = 2048
PAST_LEN = 16384
PAGE = 128
RG_WIDTH = 1024
RG_BLOCKS = 8
CONV_W = 4
RG_C = 8.0
HEADS = 8
DH = 128
ATT_W = HEADS * DH
DIL_PATTERNS = ((128, 1), (512, 4), (2048, 16))
ROT_DIMS = 32
ROPE_THETA = 500000.0
M_HEADS = 4
M_DH = 256
M_WIDTH = M_HEADS * M_DH
CHUNK = 128
E_IN = 2 * RG_WIDTH + 3 * ATT_W
O_MAIN = 3 * ATT_W + 4 * M_WIDTH
PEER_HEADS = 8
N_KEYS = 128
N_EXPERTS = N_KEYS * N_KEYS
PEER_TOPK = 16
EPS = 1e-6
LANES = 128
VMEM_LIMIT = 56 * 1024 * 1024
NEG_INF = float("-inf")


def _cparams(*sem):
    return pltpu.CompilerParams(dimension_semantics=sem, vmem_limit_bytes=VMEM_LIMIT)


def _dot(a, b):
    return jnp.dot(a, b, preferred_element_type=f32)


def _dot_nt(a, b):
    return lax.dot_general(a, b, (((1,), (1,)), ((), ())), preferred_element_type=f32)


def _split3(x):
    hi = x.astype(bf16)
    r = x - hi.astype(f32)
    mid = r.astype(bf16)
    lo = (r - mid.astype(f32)).astype(bf16)
    return hi, mid, lo


def _sigmoid(x):
    return 1.0 / (1.0 + jnp.exp(-x))


def _log_sigmoid_pair(z):
    l1p = jnp.log1p(jnp.exp(-jnp.abs(z)))
    return jnp.minimum(z, 0.0) - l1p, -jnp.maximum(z, 0.0) - l1p


def _softplus(z):
    return jnp.maximum(z, 0.0) + jnp.log(1.0 + jnp.exp(-jnp.abs(z)))


def _gelu(x):
    c = math.sqrt(2.0 / math.pi)
    return 0.5 * x * (1.0 + jnp.tanh(c * (x + 0.044715 * (x * x * x))))


def _iota(shape, dim):
    return lax.broadcasted_iota(jnp.int32, shape, dim)


def _rowsum_bcast(x):
    ones = jnp.ones((LANES, LANES), bf16)
    hi = x.astype(bf16)
    lo = (x - hi.astype(f32)).astype(bf16)
    return _dot(hi, ones) + _dot(lo, ones)


def _adaln_kernel(c_ref, w_ref, b_ref, o_ref):
    c = c_ref[...]
    s = c * _sigmoid(c)
    w = w_ref[0]
    s_hi = s.astype(bf16)
    s_lo = (s - s_hi.astype(f32)).astype(bf16)
    w_hi = w.astype(bf16)
    w_lo = (w - w_hi.astype(f32)).astype(bf16)
    o_ref[0] = _dot(s_hi, w_hi) + _dot(s_hi, w_lo) + _dot(s_lo, w_hi) + b_ref[0]


def adaln_all(c_all, w_ada, b_ada):
    depth, d, n = w_ada.shape
    rows = c_all.shape[0]
    tn = 1024
    return pl.pallas_call(
        _adaln_kernel,
        grid=(depth, n // tn),
        in_specs=[
            pl.BlockSpec((rows, d), lambda l, j: (0, 0)),
            pl.BlockSpec((1, d, tn), lambda l, j: (l, 0, j)),
            pl.BlockSpec((1, 1, tn), lambda l, j: (l, 0, j)),
        ],
        out_specs=pl.BlockSpec((1, rows, tn), lambda l, j: (l, 0, j)),
        out_shape=jax.ShapeDtypeStruct((depth, rows, n), f32),
        compiler_params=_cparams("arbitrary", "arbitrary"),
        name="adaln",
    )(c_all, w_ada, b_ada.reshape(depth, 1, n))


def _modulated(x, g, sh, sc):
    ms = jnp.mean(x * x, axis=-1, keepdims=True)
    y = x * lax.rsqrt(ms + EPS) * g
    return y * (1.0 + sc) + sh


def _modmm_kernel(x_ref, g_ref, sh_ref, sc_ref, w_ref, o_ref, xn_ref):
    @pl.when(pl.program_id(1) == 0)
    def _():
        xn_ref[...] = _modulated(x_ref[...], g_ref[...], sh_ref[0], sc_ref[0]).astype(bf16)

    o_ref[...] = _dot(xn_ref[...], w_ref[...])


def _modmm_gate_kernel(x_ref, g_ref, sh_ref, sc_ref, w_ref, wg_ref, o_ref, og_ref, xn_ref):
    @pl.when(pl.program_id(1) == 0)
    def _():
        xn = _modulated(x_ref[...], g_ref[...], sh_ref[0], sc_ref[0]).astype(bf16)
        xn_ref[...] = xn
        og_ref[...] = _dot(xn, wg_ref[...])

    o_ref[...] = _dot(xn_ref[...], w_ref[...])


def _mod_specs(mod, m, tm, k, rows_per_batch):
    if rows_per_batch >= tm:
        assert rows_per_batch % tm == 0
        per = rows_per_batch // tm
        return mod.reshape(-1, 1, k), pl.BlockSpec((1, 1, k), lambda i, j: (i // per, 0, 0))
    assert rows_per_batch == 1
    rows = mod
    if rows.shape[0] < m:
        rows = jnp.pad(rows, ((0, m - rows.shape[0]), (0, 0)))
    return rows.reshape(1, m, k), pl.BlockSpec((1, tm, k), lambda i, j: (0, i, 0))


def mod_matmul(x, g, shift, scale, w, n_out, rows_per_batch, tm, tn, w_gate=None, name="modmm"):
    m, k = x.shape
    sh, sh_spec = _mod_specs(shift, m, tm, k, rows_per_batch)
    sc, sc_spec = _mod_specs(scale, m, tm, k, rows_per_batch)
    in_specs = [
        pl.BlockSpec((tm, k), lambda i, j: (i, 0)),
        pl.BlockSpec((1, k), lambda i, j: (0, 0)),
        sh_spec,
        sc_spec,
        pl.BlockSpec((k, tn), lambda i, j: (0, j)),
    ]
    args = [x, g.reshape(1, k), sh, sc, w]
    out_specs = pl.BlockSpec((tm, tn), lambda i, j: (i, j))
    out_shape = jax.ShapeDtypeStruct((m, n_out), f32)
    kern = _modmm_kernel
    if w_gate is not None:
        in_specs.append(pl.BlockSpec((k, LANES), lambda i, j: (0, 0)))
        args.append(w_gate)
        out_specs = [out_specs, pl.BlockSpec((tm, LANES), lambda i, j: (i, 0))]
        out_shape = [out_shape, jax.ShapeDtypeStruct((m, LANES), f32)]
        kern = _modmm_gate_kernel
    return pl.pallas_call(
        kern,
        grid=(m // tm, n_out // tn),
        in_specs=in_specs,
        out_specs=out_specs,
        out_shape=out_shape,
        scratch_shapes=[pltpu.VMEM((tm, k), bf16)],
        compiler_params=_cparams("arbitrary", "arbitrary"),
        name=name,
    )(*args)


def _outproj_kernel(a1_ref, a2_ref, w1_ref, w2_ref, x_ref, gt_ref, o_ref):
    y = _dot(a1_ref[...].astype(bf16), w1_ref[...]) + _dot(a2_ref[...].astype(bf16), w2_ref[...])
    o_ref[...] = x_ref[...] + gt_ref[0] * y


def out_proj(a1, a2, w, x, gate, rows_per_batch, tm, tn, name="outproj"):
    m, k1 = a1.shape
    k2 = a2.shape[1]
    n = w.shape[1]
    gt, gt_spec = _mod_specs(gate, m, tm, n, rows_per_batch)
    if gt.shape[1] == 1:
        per = rows_per_batch // tm
        gt_spec = pl.BlockSpec((1, 1, tn), lambda i, j: (i // per, 0, j))
    else:
        gt_spec = pl.BlockSpec((1, tm, tn), lambda i, j: (0, i, j))
    return pl.pallas_call(
        _outproj_kernel,
        grid=(m // tm, n // tn),
        in_specs=[
            pl.BlockSpec((tm, k1), lambda i, j: (i, 0)),
            pl.BlockSpec((tm, k2), lambda i, j: (i, 0)),
            pl.BlockSpec((k1, tn), lambda i, j: (0, j)),
            pl.BlockSpec((k2, tn), lambda i, j: (k1 // k2, j)),
            pl.BlockSpec((tm, tn), lambda i, j: (i, j)),
            gt_spec,
        ],
        out_specs=pl.BlockSpec((tm, tn), lambda i, j: (i, j)),
        out_shape=jax.ShapeDtypeStruct((m, n), f32),
        compiler_params=_cparams("arbitrary", "arbitrary"),
        name=name,
    )(a1, a2, w, w, x, gt)


def _rglru_gates(xc, wr_ref, br, wi_ref, bi, lam):
    xb = xc.astype(bf16)
    rs, gs = [], []
    for hb in range(RG_BLOCKS):
        sl = slice(hb * LANES, (hb + 1) * LANES)
        rs.append(_dot(xb[:, sl], wr_ref[hb]))
        gs.append(_dot(xb[:, sl], wi_ref[hb]))
    r = _sigmoid(jnp.concatenate(rs, axis=1) + br)
    ig = _sigmoid(jnp.concatenate(gs, axis=1) + bi)
    softplus_neg_lam = jnp.maximum(-lam, 0.0) + jnp.log1p(jnp.exp(-jnp.abs(lam)))
    log_a = -RG_C * r * softplus_neg_lam
    a = jnp.exp(log_a)
    u = jnp.sqrt(-jnp.tanh(log_a) * (a * a + 1.0)) * ig * xc
    return a, u


def _rglru_kernel(xa_ref, ga_ref, cw_ref, cb_ref, wr_ref, br_ref, wi_ref, bi_ref, lam_ref,
                  ya_ref, hl_ref, xprev_ref, hc_ref):
    t_idx = pl.program_id(1)
    tt = xa_ref.shape[0]

    @pl.when(t_idx == 0)
    def _():
        xprev_ref[...] = jnp.zeros_like(xprev_ref)
        hc_ref[...] = jnp.zeros_like(hc_ref)

    xa = xa_ref[...]
    xprev = xprev_ref[...]
    row8 = _iota((8, RG_WIDTH), 0)
    xc = cb_ref[...] + cw_ref[CONV_W - 1:CONV_W, :] * xa
    for k in range(1, CONV_W):
        rolled = pltpu.roll(xa, k, 0)
        head = jnp.where(row8 < k, pltpu.roll(xprev, k, 0), rolled[0:8])
        shifted = jnp.concatenate([head, rolled[8:]], axis=0)
        xc = xc + cw_ref[CONV_W - 1 - k:CONV_W - k, :] * shifted
    xprev_ref[...] = xa[tt - 8:tt]

    a, u = _rglru_gates(xc, wr_ref, br_ref[...], wi_ref, bi_ref[...], lam_ref[...])
    row = _iota((tt, RG_WIDTH), 0)
    s = 1
    while s < tt:
        a_sh = pltpu.roll(a, s, 0)
        u_sh = pltpu.roll(u, s, 0)
        ok = row >= s
        u = jnp.where(ok, a * u_sh + u, u)
        a = jnp.where(ok, a * a_sh, a)
        s *= 2
    h = a * hc_ref[...] + u
    hc_ref[...] = h[tt - 1:tt]
    hl_ref[0] = h[tt - 1:tt]
    ya_ref[...] = h * _gelu(ga_ref[...])


def rglru_prompt(proj, bsz, seq, cw, cb, wr, br, wi, bi, lam, tt=256):
    nt = seq // tt
    vec = lambda: pl.BlockSpec((1, RG_WIDTH), lambda b, t: (0, 0))
    return pl.pallas_call(
        _rglru_kernel,
        grid=(bsz, nt),
        in_specs=[
            pl.BlockSpec((tt, RG_WIDTH), lambda b, t: (b * nt + t, 0)),
            pl.BlockSpec((tt, RG_WIDTH), lambda b, t: (b * nt + t, 1)),
            pl.BlockSpec((CONV_W, RG_WIDTH), lambda b, t: (0, 0)),
            vec(),
            pl.BlockSpec((RG_BLOCKS, LANES, LANES), lambda b, t: (0, 0, 0)),
            vec(),
            pl.BlockSpec((RG_BLOCKS, LANES, LANES), lambda b, t: (0, 0, 0)),
            vec(),
            vec(),
        ],
        out_specs=[
            pl.BlockSpec((tt, RG_WIDTH), lambda b, t: (b * nt + t, 0)),
            pl.BlockSpec((1, 1, RG_WIDTH), lambda b, t: (b, 0, 0)),
        ],
        out_shape=[
            jax.ShapeDtypeStruct((bsz * seq, RG_WIDTH), f32),
            jax.ShapeDtypeStruct((bsz, 1, RG_WIDTH), f32),
        ],
        scratch_shapes=[pltpu.VMEM((8, RG_WIDTH), f32), pltpu.VMEM((1, RG_WIDTH), f32)],
        compiler_params=_cparams("arbitrary", "arbitrary"),
        name="rglru_prompt",
    )(proj, proj, cw, cb, wr, br, wi, bi, lam)


def _rglru_step_kernel(xa_ref, ga_ref, b0_ref, b1_ref, b2_ref, h0_ref, cw_ref, cb_ref,
                       wr_ref, br_ref, wi_ref, bi_ref, lam_ref, ya_ref, h_ref):
    xa = xa_ref[...]
    xc = (cb_ref[...] + cw_ref[0:1, :] * b0_ref[...] + cw_ref[1:2, :] * b1_ref[...]
          + cw_ref[2:3, :] * b2_ref[...] + cw_ref[3:4, :] * xa)
    a, u = _rglru_gates(xc, wr_ref, br_ref[...], wi_ref, bi_ref[...], lam_ref[...])
    h = a * h0_ref[...] + u
    h_ref[...] = h
    ya_ref[...] = h * _gelu(ga_ref[...])


def rglru_step(proj_s, conv_state, h0, cw, cb, wr, br, wi, bi, lam):
    n = proj_s.shape[0]
    full = lambda shape: pl.BlockSpec(shape, lambda i: tuple(0 for _ in shape))
    return pl.pallas_call(
        _rglru_step_kernel,
        grid=(1,),
        in_specs=[
            pl.BlockSpec((n, RG_WIDTH), lambda i: (0, 0)),
            pl.BlockSpec((n, RG_WIDTH), lambda i: (0, 1)),
            full((n, RG_WIDTH)), full((n, RG_WIDTH)), full((n, RG_WIDTH)), full((n, RG_WIDTH)),
            full((CONV_W, RG_WIDTH)), full((1, RG_WIDTH)),
            full((RG_BLOCKS, LANES, LANES)), full((1, RG_WIDTH)),
            full((RG_BLOCKS, LANES, LANES)), full((1, RG_WIDTH)), full((1, RG_WIDTH)),
        ],
        out_specs=[full((n, RG_WIDTH)), full((n, RG_WIDTH))],
        out_shape=[jax.ShapeDtypeStruct((n, RG_WIDTH), f32)] * 2,
        compiler_params=_cparams("arbitrary"),
        name="rglru_step",
    )(proj_s, proj_s, conv_state[:, 0], conv_state[:, 1], conv_state[:, 2], h0,
      cw, cb, wr, br, wi, bi, lam)


def _rope_tables(pos):
    half = ROT_DIMS // 2
    inv = ROPE_THETA ** (-jnp.arange(half, dtype=f32) / half)
    ang = pos.astype(f32)[:, None] * inv[None, :]
    cos, sin = jnp.cos(ang), jnp.sin(ang)
    n = pos.shape[0]
    ctab = jnp.concatenate([cos, cos, jnp.ones((n, DH - ROT_DIMS), f32)], axis=1)
    stab = jnp.concatenate([-sin, sin, jnp.zeros((n, DH - ROT_DIMS), f32)], axis=1)
    return ctab, stab


def _rope_head(xh, ctab, stab, lane):
    half = ROT_DIMS // 2
    partner = jnp.where(lane < half, pltpu.roll(xh, DH - half, 1), pltpu.roll(xh, half, 1))
    return xh * ctab + partner * stab


def _rope_kernel(q_ref, k_ref, v_ref, c_ref, s_ref, qo_ref, ko_ref, vo_ref):
    ctab, stab = c_ref[...], s_ref[...]
    lane = _iota((q_ref.shape[0], DH), 1)
    for hb in range(HEADS):
        sl = slice(hb * DH, (hb + 1) * DH)
        qo_ref[:, hb, :] = _rope_head(q_ref[:, sl], ctab, stab, lane) * (DH ** -0.5)
        ko_ref[:, hb, :] = _rope_head(k_ref[:, sl], ctab, stab, lane)
        vo_ref[:, hb, :] = v_ref[:, sl]


def rope_qkv(proj, ctab, stab, rows_per_seq, tt):
    m = proj.shape[0]
    nt = max(rows_per_seq // tt, 1)
    if ctab.shape[0] == 1:
        tab_spec = pl.BlockSpec((1, DH), lambda i: (0, 0))
    else:
        tab_spec = pl.BlockSpec((tt, DH), lambda i: (i % nt, 0))
    return pl.pallas_call(
        _rope_kernel,
        grid=(m // tt,),
        in_specs=[
            pl.BlockSpec((tt, ATT_W), lambda i: (i, 2)),
            pl.BlockSpec((tt, ATT_W), lambda i: (i, 3)),
            pl.BlockSpec((tt, ATT_W), lambda i: (i, 4)),
            tab_spec, tab_spec,
        ],
        out_specs=[pl.BlockSpec((tt, HEADS, DH), lambda i: (i, 0, 0))] * 3,
        out_shape=[jax.ShapeDtypeStruct((m, HEADS, DH), f32)] * 3,
        compiler_params=_cparams("arbitrary"),
        name="rope",
    )(proj, proj, proj, ctab, stab)


def _rope_split_kernel(q_ref, k_ref, v_ref, c_ref, s_ref, *refs):
    n_pat = len(DIL_PATTERNS)
    outs, (kf_ref, qs_ref, ks_ref) = refs[:3 * n_pat], refs[3 * n_pat:]
    tt = q_ref.shape[0]
    ctab, stab = c_ref[...], s_ref[...]
    lane = _iota((tt, DH), 1)
    qs_ref[...] = _rope_head(q_ref[...], ctab, stab, lane) * (DH ** -0.5)
    k_rot = _rope_head(k_ref[...], ctab, stab, lane)
    ks_ref[...] = k_rot
    kf_ref[...] = k_rot
    for gi, (_, d) in enumerate(DIL_PATTERNS):
        rows = tt // d
        for src, dst in ((qs_ref, outs[3 * gi]), (ks_ref, outs[3 * gi + 1]), (v_ref, outs[3 * gi + 2])):
            for r in range(d):
                dst[0, 0, r] = src[pl.ds(r, rows, stride=d), :].astype(bf16)


def rope_split(proj, ctab, stab, bsz, seq, tt=512):
    m = proj.shape[0]
    nt = seq // tt
    col = lambda c: pl.BlockSpec((tt, DH), lambda i, h, c=c: (i, c * HEADS + h))
    tab = pl.BlockSpec((tt, DH), lambda i, h: (i % nt, 0))
    out_specs, out_shape = [], []
    for (_, d) in DIL_PATTERNS:
        for _ in range(3):
            out_specs.append(pl.BlockSpec((1, 1, d, tt // d, DH), lambda i, h: (i // nt, h, 0, i % nt, 0)))
            out_shape.append(jax.ShapeDtypeStruct((bsz, HEADS, d, seq // d, DH), bf16))
    out_specs.append(pl.BlockSpec((tt, DH), lambda i, h: (i, h)))
    out_shape.append(jax.ShapeDtypeStruct((m, ATT_W), f32))
    res = pl.pallas_call(
        _rope_split_kernel,
        grid=(m // tt, HEADS),
        in_specs=[col(2), col(3), col(4), tab, tab],
        out_specs=out_specs,
        out_shape=out_shape,
        scratch_shapes=[pltpu.VMEM((tt, DH), f32), pltpu.VMEM((tt, DH), f32)],
        compiler_params=_cparams("arbitrary", "arbitrary"),
        name="rope_split",
    )(proj, proj, proj, ctab, stab)
    return [res[3 * gi:3 * gi + 3] for gi in range(len(DIL_PATTERNS))], res[-1]


def _dil_kernel(q_ref, kc_ref, kp_ref, vc_ref, vp_ref, o_ref, l_ref, s_ref, p_ref, *, span):
    tq = q_ref.shape[3]
    blk = pl.program_id(2)
    qi = _iota((tq, 2 * tq), 0)
    col = _iota((tq, 2 * tq), 1)
    rel = jnp.where(col < tq, qi - col, qi - col + 2 * tq)
    ok = (rel >= 0) & (rel <= span) & ((col < tq) | (blk > 0))
    for hb in range(HEADS):
        qh = q_ref[0, hb, 0]
        s_ref[hb, :, :tq] = _dot_nt(qh, kc_ref[0, hb, 0])
        s_ref[hb, :, tq:] = _dot_nt(qh, kp_ref[0, hb, 0])
    for hb in range(HEADS):
        s = jnp.where(ok, s_ref[hb], NEG_INF)
        mx = jnp.max(s, axis=1, keepdims=True)
        p = jnp.exp(s - mx)
        den = jnp.sum(p, axis=1, keepdims=True)
        p_ref[hb] = (p * (1.0 / den)).astype(bf16)
        l_ref[0, hb, 0] = jnp.broadcast_to(mx + jnp.log(den), (tq, DH))
    for hb in range(HEADS):
        o_ref[0, hb, 0] = (_dot(p_ref[hb, :, :tq], vc_ref[0, hb, 0])
                           + _dot(p_ref[hb, :, tq:], vp_ref[0, hb, 0]))


def _dil_merge_kernel(*refs):
    n_pat = len(DIL_PATTERNS)
    ins, o_ref, scr = refs[:2 * n_pat], refs[2 * n_pat], refs[2 * n_pat + 1:]
    tt = o_ref.shape[0]
    for gi, (_, d) in enumerate(DIL_PATTERNS):
        rows = tt // d
        for src, dst in ((ins[2 * gi], scr[2 * gi]), (ins[2 * gi + 1], scr[2 * gi + 1])):
            for r in range(d):
                dst[pl.ds(r, rows, stride=d), :] = src[0, 0, r]
    lses = [scr[2 * gi + 1][...] for gi in range(n_pat)]
    top = functools.reduce(jnp.maximum, lses)
    es = [jnp.exp(l - top) for l in lses]
    num = sum(scr[2 * gi][...] * es[gi] for gi in range(n_pat))
    o_ref[...] = num / sum(es)


def dilated_prompt(qkv_by_pattern, bsz, seq, tq=128, tt=512):
    partial = []
    for (w, d), (qd, kd, vd) in zip(DIL_PATTERNS, qkv_by_pattern):
        sd = seq // d
        blk = (1, HEADS, 1, tq, DH)
        cur = pl.BlockSpec(blk, lambda b, r, i: (b, 0, r, i, 0))
        prev = pl.BlockSpec(blk, lambda b, r, i: (b, 0, r, jnp.maximum(i - 1, 0), 0))
        partial += pl.pallas_call(
            functools.partial(_dil_kernel, span=w // d),
            grid=(bsz, d, sd // tq),
            in_specs=[cur, cur, prev, cur, prev],
            out_specs=[cur, cur],
            out_shape=[jax.ShapeDtypeStruct((bsz, HEADS, d, sd, DH), f32)] * 2,
            scratch_shapes=[pltpu.VMEM((HEADS, tq, 2 * tq), f32), pltpu.VMEM((HEADS, tq, 2 * tq), bf16)],
            compiler_params=_cparams("arbitrary", "arbitrary", "arbitrary"),
            name=f"dilattn_d{d}",
        )(qd, kd, kd, vd, vd)
    nt = seq // tt
    in_specs = []
    for (_, d) in DIL_PATTERNS:
        in_specs += [pl.BlockSpec((1, 1, d, tt // d, DH), lambda i, h: (i // nt, h, 0, i % nt, 0))] * 2
    return pl.pallas_call(
        _dil_merge_kernel,
        grid=(bsz * nt, HEADS),
        in_specs=in_specs,
        out_specs=pl.BlockSpec((tt, DH), lambda i, h: (i, h)),
        out_shape=jax.ShapeDtypeStruct((bsz * seq, ATT_W), f32),
        scratch_shapes=[pltpu.VMEM((tt, DH), f32)] * (2 * len(DIL_PATTERNS)),
        compiler_params=_cparams("arbitrary", "arbitrary"),
        name="dilattn_merge",
    )(*partial)


def _dil_step_kernel(q_ref, kn_ref, vn_ref, k1_ref, k4_ref, k16_ref, v1_ref, v4_ref, v16_ref, o_ref):
    q = q_ref[0]
    kn, vn = kn_ref[0], vn_ref[0]
    s_self = _rowsum_bcast(q * kn)
    o_gs, lse_gs = [], []
    for k_ref, v_ref in ((k1_ref, v1_ref), (k4_ref, v4_ref), (k16_ref, v16_ref)):
        k3 = k_ref[0, :, 0]
        nk = k3.shape[0]
        s = _rowsum_bcast((k3 * q[None]).reshape(nk * HEADS, DH)).reshape(nk, HEADS, DH)
        mx = jnp.maximum(jnp.max(s, axis=0), s_self)
        p = jnp.exp(s - mx[None])
        p_self = jnp.exp(s_self - mx)
        den = jnp.sum(p, axis=0) + p_self
        num = jnp.sum(p * v_ref[0, :, 0], axis=0) + p_self * vn
        o_gs.append(num / den)
        lse_gs.append(mx + jnp.log(den))
    top = jnp.maximum(jnp.maximum(lse_gs[0], lse_gs[1]), lse_gs[2])
    es = [jnp.exp(l - top) for l in lse_gs]
    tot = es[0] + es[1] + es[2]
    o_ref[0] = (o_gs[0] * es[0] + o_gs[1] * es[1] + o_gs[2] * es[2]) / tot


def dilated_step(q4, k4, v4, cache_k, cache_v):
    n, win = cache_k.shape[0], cache_k.shape[1]
    one = pl.BlockSpec((1, HEADS, DH), lambda b: (b, 0, 0))
    args = [q4, k4, v4]
    in_specs = [one, one, one]
    for cache in (cache_k, cache_v):
        for (w, d) in DIL_PATTERNS:
            nkeys = w // d
            assert win % d == 0 and (win // d) % nkeys == 0 and (win - w) % (d * nkeys) == 0
            args.append(cache.reshape(n, win // d, d, HEADS, DH))
            in_specs.append(pl.BlockSpec((1, nkeys, 1, HEADS, DH),
                                         lambda b, blk=(win - w) // d // nkeys: (b, blk, 0, 0, 0)))
    return pl.pallas_call(
        _dil_step_kernel,
        grid=(n,),
        in_specs=in_specs,
        out_specs=one,
        out_shape=jax.ShapeDtypeStruct((n, HEADS, DH), f32),
        compiler_params=_cparams("arbitrary"),
        name="dilattn_step",
    )(*args).reshape(n, ATT_W)


def _sb_kernel(bias_ref, q_ref, k_ref, v_ref, o_ref, kb_ref, vb_ref, qs_ref, t_ref, spb_ref, wb_ref,
               acc_ref, run_ref):
    tq = q_ref.shape[1]
    nh = q_ref.shape[2] // DH
    kt_w = t_ref.shape[2]
    sub = PAGE
    hg = pl.program_id(1)
    qb = pl.program_id(2)

    @pl.when(qb == 0)
    def _():
        kb_ref[...] = k_ref[0].astype(bf16)
        vb_ref[...] = v_ref[0].astype(bf16)

    rr = _iota((sub, 2 * sub), 0)
    cc = _iota((sub, 2 * sub), 1)
    tri = jnp.where((rr > cc) | (cc >= sub), 1.0, 0.0).astype(bf16)
    for h in range(nh):
        qs_ref[h] = (q_ref[0, :, h * DH:(h + 1) * DH] * (DH ** -0.5)).astype(bf16)
    acc_ref[...] = jnp.zeros_like(acc_ref)
    run_ref[...] = jnp.zeros_like(run_ref)

    def macro(start, masked):
        for h in range(nh):
            hs = slice(h * DH, (h + 1) * DH)
            z = _dot_nt(qs_ref[h], kb_ref[pl.ds(start, kt_w), hs]) + bias_ref[hg * nh + h]
            sp = _softplus(z)
            t = z - sp
            if masked:
                ok = start + _iota((tq, kt_w), 1) < qb * tq + _iota((tq, kt_w), 0)
                sp = jnp.where(ok, sp, 0.0)
                t = jnp.where(ok, t, NEG_INF)
            t_ref[h] = t
            spb_ref[h] = sp.astype(bf16)
        for h in range(nh):
            run = run_ref[h]
            for kt in range(kt_w // sub - 1, -1, -1):
                ks = slice(kt * sub, (kt + 1) * sub)
                cs = _dot(spb_ref[h, :, ks], tri)
                wb_ref[h, :, ks] = jnp.exp(t_ref[h, :, ks] - cs[:, :sub] - run).astype(bf16)
                run = run + cs[:, sub:]
            run_ref[h] = run
        for h in range(nh):
            hs = slice(h * DH, (h + 1) * DH)
            acc_ref[h] += _dot(wb_ref[h], vb_ref[pl.ds(start, kt_w), hs])

    top = (qb * tq) // kt_w
    macro(pl.multiple_of(top * kt_w, kt_w), True)

    def body(it, _):
        macro(pl.multiple_of((top - 1 - it) * kt_w, kt_w), False)
        return 0

    lax.fori_loop(0, top, body, 0)
    for h in range(nh):
        o_ref[0, :, h * DH:(h + 1) * DH] = acc_ref[h]


def sb_prompt(proj2, bias, bsz, seq, tq=256, nh=4, kt_w=512):
    assert kt_w % tq == 0 and seq % kt_w == 0
    p3 = proj2.reshape(bsz, seq, proj2.shape[1])
    nq = seq // tq
    ng = HEADS // nh
    wd = nh * DH
    return pl.pallas_call(
        _sb_kernel,
        grid=(bsz, ng, nq),
        in_specs=[
            pl.BlockSpec(memory_space=pltpu.SMEM),
            pl.BlockSpec((1, tq, wd), lambda b, g, i: (b, i, g)),
            pl.BlockSpec((1, seq, wd), lambda b, g, i: (b, 0, ng + g)),
            pl.BlockSpec((1, seq, wd), lambda b, g, i: (b, 0, 2 * ng + g)),
        ],
        out_specs=pl.BlockSpec((1, tq, wd), lambda b, g, i: (b, i, g)),
        out_shape=jax.ShapeDtypeStruct((bsz, seq, ATT_W), f32),
        scratch_shapes=[
            pltpu.VMEM((seq, wd), bf16), pltpu.VMEM((seq, wd), bf16),
            pltpu.VMEM((nh, tq, DH), bf16),
            pltpu.VMEM((nh, tq, kt_w), f32), pltpu.VMEM((nh, tq, kt_w), bf16),
            pltpu.VMEM((nh, tq, kt_w), bf16),
            pltpu.VMEM((nh, tq, DH), f32), pltpu.VMEM((nh, tq, PAGE), f32),
        ],
        compiler_params=_cparams("arbitrary", "arbitrary", "arbitrary"),
        name="sb_prompt",
    )(bias, p3, p3, p3).reshape(bsz * seq, ATT_W)


def _sb_step_kernel(pt_ref, q_ref, bias_ref, eye_ref, *refs, npg):
    k_refs, v_refs = refs[:npg], refs[npg:2 * npg]
    o_ref, acc_ref, carry_ref = refs[2 * npg:]
    j = pl.program_id(1)

    @pl.when(j == 0)
    def _():
        acc_ref[...] = jnp.zeros_like(acc_ref)
        carry_ref[...] = jnp.zeros_like(carry_ref)

    q = q_ref[0]
    bias = bias_ref[...]
    ones = jnp.ones((DH, DH), bf16)
    rr = _iota((PAGE, 2 * PAGE), 0)
    cc = _iota((PAGE, 2 * PAGE), 1)
    tri = jnp.where((rr > cc) | (cc >= PAGE), 1.0, 0.0).astype(bf16)
    acc = acc_ref[...]
    run = carry_ref[...]
    pages = range(npg)
    zbs = [_dot((k_refs[p][0] * q[None]).reshape(PAGE * HEADS, DH).astype(bf16), ones)
           .reshape(PAGE, HEADS, DH) for p in pages]
    zs = [jnp.sum(zb * eye_ref[...], axis=0) + bias for zb in zbs]
    sps = [_softplus(z) for z in zs]
    css = []
    for sp in sps:
        hi, mid, lo = _split3(sp)
        css.append(_dot(hi, tri) + _dot(mid, tri) + _dot(lo, tri))
    ws = []
    for z, sp, cs in zip(zs, sps, css):
        ws.append(jnp.exp(z - sp - cs[:, :PAGE] - run))
        run = run + cs[:, PAGE:]
    wbs = [_dot((eye_ref[...] * w[None]).reshape(PAGE * HEADS, DH).astype(bf16), ones)
           .reshape(PAGE, HEADS, DH) for w in ws]
    for p, wb in zip(pages, wbs):
        acc = acc + jnp.sum(wb * v_refs[p][0], axis=0)
    acc_ref[...] = acc
    carry_ref[...] = run

    @pl.when(j == pl.num_programs(1) - 1)
    def _():
        o_ref[0] = acc


def sb_step(q_s, bias, cache_k, cache_v, page_table, npg=8):
    n, n_pages = page_table.shape
    q4 = (q_s * (DH ** -0.5)).reshape(n, HEADS, DH)
    bias4 = jnp.broadcast_to(bias[:, None], (HEADS, DH))
    eye3 = jnp.broadcast_to(jnp.eye(PAGE, DH, dtype=f32)[:, None, :], (PAGE, HEADS, DH))

    def page(p):
        return lambda b, j, pt: (pt[b, n_pages - 1 - (j * npg + p)], 0, 0, 0)

    kv_specs = [pl.BlockSpec((1, PAGE, HEADS, DH), page(p)) for p in range(npg)]
    grid_spec = pltpu.PrefetchScalarGridSpec(
        num_scalar_prefetch=1,
        grid=(n, n_pages // npg),
        in_specs=[
            pl.BlockSpec((1, HEADS, DH), lambda b, j, pt: (b, 0, 0)),
            pl.BlockSpec((HEADS, DH), lambda b, j, pt: (0, 0)),
            pl.BlockSpec((PAGE, HEADS, DH), lambda b, j, pt: (0, 0, 0)),
        ] + kv_specs + kv_specs,
        out_specs=pl.BlockSpec((1, HEADS, DH), lambda b, j, pt: (b, 0, 0)),
        scratch_shapes=[pltpu.VMEM((HEADS, DH), f32), pltpu.VMEM((HEADS, DH), f32)],
    )
    return pl.pallas_call(
        functools.partial(_sb_step_kernel, npg=npg),
        grid_spec=grid_spec,
        out_shape=jax.ShapeDtypeStruct((n, HEADS, DH), f32),
        compiler_params=_cparams("arbitrary", "arbitrary"),
        name="sb_step",
    )(page_table, q4, bias4, eye3, *([cache_k] * npg), *([cache_v] * npg)).reshape(n, ATT_W)


def _mlstm_kernel(q_ref, k_ref, v_ref, og_ref, gate_ref, gb_ref, gn_ref,
                  h_ref, c_out, n_out, m_out, c_s, n_s, m_s):
    ci = pl.program_id(1)
    L = CHUNK

    @pl.when(ci == 0)
    def _():
        c_s[...] = jnp.zeros_like(c_s)
        n_s[...] = jnp.zeros_like(n_s)
        m_s[...] = jnp.zeros_like(m_s)

    gt = gate_ref[...] + gb_ref[...]
    gt_t = gt.T
    ri = _iota((L, L), 0)
    li = _iota((L, L), 1)
    causal = li <= ri
    tri_incl = jnp.where(causal, 1.0, 0.0).astype(bf16)
    tri_incl_t = jnp.where(ri <= li, 1.0, 0.0).astype(bf16)
    for h in range(M_HEADS):
        sl = slice(h * M_DH, (h + 1) * M_DH)
        ig_col = gt[:, h:h + 1]
        ig_row = gt_t[h:h + 1, :]
        lf_col = _log_sigmoid_pair(gt[:, M_HEADS + h:M_HEADS + h + 1])[0]
        lf_row = _log_sigmoid_pair(gt_t[M_HEADS + h:M_HEADS + h + 1, :])[0]
        c_hi, c_mid, c_lo = _split3(jnp.broadcast_to(lf_col, (L, L)))
        bcum_col = _dot(tri_incl, c_hi) + _dot(tri_incl, c_mid) + _dot(tri_incl, c_lo)
        r_hi, r_mid, r_lo = _split3(jnp.broadcast_to(lf_row, (L, L)))
        bcum_row = _dot(r_hi, tri_incl_t) + _dot(r_mid, tri_incl_t) + _dot(r_lo, tri_incl_t)
        m_prev = m_s[h:h + 1, :]
        dlog = jnp.where(causal, bcum_col - bcum_row + ig_row, NEG_INF)
        inter = bcum_col + m_prev
        m_t = jnp.maximum(inter, jnp.max(dlog, axis=1, keepdims=True))
        dw = jnp.exp(dlog - m_t)
        iw = jnp.exp(inter - m_t)
        qh = q_ref[:, sl]
        kh = k_ref[:, sl] * (M_DH ** -0.5)
        vh = v_ref[:, sl]
        qb, kb, vb = qh.astype(bf16), kh.astype(bf16), vh.astype(bf16)
        sw = dw * _dot_nt(qb, kb)
        c_prev = c_s[h]
        n_prev = n_s[h:h + 1, :]
        iw_col = iw[:, 0:1]
        num = _dot(sw.astype(bf16), vb) + iw_col * _dot_nt(qb, c_prev.astype(bf16))
        qn = jnp.sum(qb.astype(f32) * n_prev.astype(bf16).astype(f32), axis=1, keepdims=True)
        den = jnp.sum(sw, axis=1, keepdims=True) + iw_col * qn
        m_col = m_t[:, 0:1]
        hout = num / jnp.maximum(jnp.abs(den), jnp.exp(-m_col))
        m_last = m_t[L - 1:L, :]
        b_last = bcum_col[L - 1:L, :]
        wl_col = jnp.exp(b_last[:, 0:1] - bcum_col[:, 0:1] + ig_col - m_last[:, 0:1])
        wl_row = jnp.exp(b_last - bcum_row[0:1, :] + ig_row - m_last)
        decay = jnp.exp(b_last + m_prev - m_last)
        dsc = decay[:, 0:1]
        c_s[h] = dsc * c_prev + _dot((vh * wl_col).T.astype(bf16), kb)
        wl8 = jnp.broadcast_to(wl_row, (8, L)).astype(bf16)
        n_s[h:h + 1, :] = dsc * n_prev + _dot(wl8, kb)[0:1, :]
        m_s[h:h + 1, :] = m_last
        hn = hout * lax.rsqrt(jnp.mean(hout * hout, axis=1, keepdims=True) + EPS)
        h_ref[:, sl] = hn * gn_ref[:, sl] * _sigmoid(og_ref[:, sl])

    @pl.when(ci == pl.num_programs(1) - 1)
    def _():
        c_out[0] = c_s[...]
        n_out[0] = n_s[...]
        m_out[0] = m_s[...]


def mlstm_prompt(proj2, gates, gate_bias, g_mnorm, bsz, seq):
    nc = seq // CHUNK
    col = lambda c: pl.BlockSpec((CHUNK, M_WIDTH), lambda b, i, c=c: (b * nc + i, c))
    hm, c1, n1, m1 = pl.pallas_call(
        _mlstm_kernel,
        grid=(bsz, nc),
        in_specs=[
            col(3), col(4), col(5), col(6),
            pl.BlockSpec((CHUNK, LANES), lambda b, i: (b * nc + i, 0)),
            pl.BlockSpec((1, LANES), lambda b, i: (0, 0)),
            pl.BlockSpec((1, M_WIDTH), lambda b, i: (0, 0)),
        ],
        out_specs=[
            pl.BlockSpec((CHUNK, M_WIDTH), lambda b, i: (b * nc + i, 0)),
            pl.BlockSpec((1, M_HEADS, M_DH, M_DH), lambda b, i: (b, 0, 0, 0)),
            pl.BlockSpec((1, M_HEADS, M_DH), lambda b, i: (b, 0, 0)),
            pl.BlockSpec((1, M_HEADS, LANES), lambda b, i: (b, 0, 0)),
        ],
        out_shape=[
            jax.ShapeDtypeStruct((bsz * seq, M_WIDTH), f32),
            jax.ShapeDtypeStruct((bsz, M_HEADS, M_DH, M_DH), f32),
            jax.ShapeDtypeStruct((bsz, M_HEADS, M_DH), f32),
            jax.ShapeDtypeStruct((bsz, M_HEADS, LANES), f32),
        ],
        scratch_shapes=[
            pltpu.VMEM((M_HEADS, M_DH, M_DH), f32),
            pltpu.VMEM((M_HEADS, M_DH), f32),
            pltpu.VMEM((M_HEADS, LANES), f32),
        ],
        compiler_params=_cparams("arbitrary", "arbitrary"),
        name="mlstm_prompt",
    )(proj2, proj2, proj2, proj2, gates, gate_bias, g_mnorm)
    return hm, c1, n1, m1[:, :, 0]


def _mlstm_step_kernel(q_ref, k_ref, v_ref, og_ref, gn_ref, ig_ref, fg_ref, c_ref, n_ref, m_ref,
                       h_ref, c_out, n_out, m_out):
    q = q_ref[0, 0]
    k = k_ref[0, 0] * (M_DH ** -0.5)
    v = v_ref[0, 0]
    ig = ig_ref[0, 0]
    lf = _log_sigmoid_pair(fg_ref[0, 0])[0]
    m0 = m_ref[0, 0]
    c0 = c_ref[0, 0]
    n0 = n_ref[0, 0]
    inter = lf + m0
    m_t = jnp.maximum(inter, ig)
    dw = jnp.exp(ig - m_t)
    iw = jnp.exp(inter - m_t)
    rnd = lambda a: a.astype(bf16).astype(f32)
    qr, kr, vr = rnd(q), rnd(k), rnd(v)
    qk = jnp.sum(qr * kr, axis=1, keepdims=True)
    sw = dw * qk
    cq = jnp.sum(rnd(c0) * qr, axis=1, keepdims=True)
    num = rnd(sw) * vr + iw * cq
    den = sw + iw * jnp.sum(rnd(n0) * qr, axis=1, keepdims=True)
    hout = num / jnp.maximum(jnp.abs(den), jnp.exp(-m_t))
    c_out[0, 0] = iw * c0 + rnd(dw * v) * kr
    n_out[0, 0] = iw * n0 + rnd(dw) * kr
    m_out[0, 0] = m_t
    hn = hout * lax.rsqrt(jnp.mean(hout * hout, axis=0, keepdims=True) + EPS)
    h_ref[0, 0] = hn * gn_ref[0] * _sigmoid(og_ref[0, 0])


def mlstm_step(proj2_s, gates_s, gate_bias, g_mnorm, c0, n0, m0):
    n = proj2_s.shape[0]
    base = 3 * ATT_W
    seg = lambda i: proj2_s[:, base + i * M_WIDTH: base + (i + 1) * M_WIDTH].reshape(n, M_HEADS, M_DH)
    g = gates_s + gate_bias
    rowb = pl.BlockSpec((1, 1, 1, M_DH), lambda b, h: (b, h, 0, 0))
    colb = pl.BlockSpec((1, 1, M_DH, 1), lambda b, h: (b, h, 0, 0))
    scal = pl.BlockSpec((1, 1, 1, 1), lambda b, h: (b, h, 0, 0))
    hcol, c1, n1, m1 = pl.pallas_call(
        _mlstm_step_kernel,
        grid=(n, M_HEADS),
        in_specs=[
            rowb, rowb, colb, colb,
            pl.BlockSpec((1, M_DH, 1), lambda b, h: (h, 0, 0)),
            scal, scal,
            pl.BlockSpec((1, 1, M_DH, M_DH), lambda b, h: (b, h, 0, 0)),
            rowb, scal,
        ],
        out_specs=[colb, pl.BlockSpec((1, 1, M_DH, M_DH), lambda b, h: (b, h, 0, 0)), rowb, scal],
        out_shape=[
            jax.ShapeDtypeStruct((n, M_HEADS, M_DH, 1), f32),
            jax.ShapeDtypeStruct((n, M_HEADS, M_DH, M_DH), f32),
            jax.ShapeDtypeStruct((n, M_HEADS, 1, M_DH), f32),
            jax.ShapeDtypeStruct((n, M_HEADS, 1, 1), f32),
        ],
        compiler_params=_cparams("arbitrary", "arbitrary"),
        name="mlstm_step",
    )(seg(0)[:, :, None, :], seg(1)[:, :, None, :], seg(2)[..., None], seg(3)[..., None],
      g_mnorm.reshape(M_HEADS, M_DH, 1),
      g[:, 0:M_HEADS].reshape(n, M_HEADS, 1, 1), g[:, M_HEADS:2 * M_HEADS].reshape(n, M_HEADS, 1, 1),
      c0, n0[:, :, None, :], m0.reshape(n, M_HEADS, 1, 1))
    return hcol.reshape(n, M_WIDTH), c1, n1.reshape(n, M_HEADS, M_DH), m1.reshape(n, M_HEADS)


def _peerq_kernel(x_ref, g_ref, sh_ref, sc_ref, wt_ref, qt_ref, xmt_ref, xs_ref):
    @pl.when(pl.program_id(1) == 0)
    def _():
        xm = _modulated(x_ref[...], g_ref[...], sh_ref[0], sc_ref[0])
        xt = xm.T.astype(bf16)
        xs_ref[...] = xt
        xmt_ref[...] = xt

    qt_ref[...] = _dot(wt_ref[...], xs_ref[...])


def peer_query(x, g, shift, scale, wq_t, rows_per_batch, tm, tn=512):
    m, k = x.shape
    n = wq_t.shape[0]
    sh, sh_spec = _mod_specs(shift, m, tm, k, rows_per_batch)
    sc, sc_spec = _mod_specs(scale, m, tm, k, rows_per_batch)
    return pl.pallas_call(
        _peerq_kernel,
        grid=(m // tm, n // tn),
        in_specs=[
            pl.BlockSpec((tm, k), lambda i, j: (i, 0)),
            pl.BlockSpec((1, k), lambda i, j: (0, 0)),
            sh_spec, sc_spec,
            pl.BlockSpec((tn, k), lambda i, j: (j, 0)),
        ],
        out_specs=[pl.BlockSpec((tn, tm), lambda i, j: (j, i)),
                   pl.BlockSpec((k, tm), lambda i, j: (0, i))],
        out_shape=[jax.ShapeDtypeStruct((n, m), f32), jax.ShapeDtypeStruct((k, m), bf16)],
        scratch_shapes=[pltpu.VMEM((k, tm), bf16)],
        compiler_params=_cparams("arbitrary", "arbitrary"),
        name="peer_query",
    )(x, g.reshape(1, k), sh, sc, wq_t)


def _heads_layout_kernel(*refs):
    n = len(refs) // 2
    for x_ref, o_ref in zip(refs[:n], refs[n:]):
        for hb in range(HEADS):
            o_ref[:, hb, :] = x_ref[:, hb * DH:(hb + 1) * DH]


def heads_layout(srcs, bsz, seq, tail, tt=512):
    nt, first = tail // tt, (seq - tail) // tt
    rows = lambda b, i: b * (seq // tt) + first + i
    return pl.pallas_call(
        _heads_layout_kernel,
        grid=(bsz, nt),
        in_specs=[pl.BlockSpec((tt, ATT_W), lambda b, i, c=c: (rows(b, i), c)) for _, c in srcs],
        out_specs=[pl.BlockSpec((tt, HEADS, DH), lambda b, i: (b * nt + i, 0, 0))] * len(srcs),
        out_shape=[jax.ShapeDtypeStruct((bsz * tail, HEADS, DH), f32)] * len(srcs),
        compiler_params=_cparams("arbitrary", "arbitrary"),
        name="heads_layout",
    )(*[x for x, _ in srcs])


def _cast_kernel(w_ref, o_ref):
    o_ref[...] = w_ref[0].astype(bf16)


def cast_layer_bf16(w, layer, tr=1024):
    _, rows, cols = w.shape
    return pl.pallas_call(
        _cast_kernel,
        grid=(rows // tr,),
        in_specs=[pl.BlockSpec((1, tr, cols), lambda i: (layer, i, 0))],
        out_specs=pl.BlockSpec((tr, cols), lambda i: (i, 0)),
        out_shape=jax.ShapeDtypeStruct((rows, cols), bf16),
        compiler_params=_cparams("arbitrary"),
        name="cast_bf16",
    )(w)


_N_TOP = PEER_TOPK + 1
_CAND_PAIRS = [(a, b) for a in range(_N_TOP) for b in range(_N_TOP) if (a + 1) * (b + 1) <= _N_TOP]
_N_CAND = -(-len(_CAND_PAIRS) // 8) * 8


def _extract_top(cur, ridx, n):
    vals = []
    big = float(cur.shape[0])
    for _ in range(n):
        mx = jnp.max(cur, axis=0, keepdims=True)
        first = jnp.min(jnp.where(cur == mx, ridx, big), axis=0, keepdims=True)
        cur = jnp.where(ridx == first, NEG_INF, cur)
        vals.append(mx)
    return vals


def _oddeven_mergesort_pairs(n):
    pairs = []

    def merge(lo, cnt, r):
        step = 2 * r
        if step < cnt:
            merge(lo, cnt, step)
            merge(lo + r, cnt, step)
            pairs.extend((i, i + r) for i in range(lo + r, lo + cnt - r, step))
        else:
            pairs.append((lo, lo + r))

    def sort(lo, cnt):
        if cnt > 1:
            sort(lo, cnt // 2)
            sort(lo + cnt // 2, cnt // 2)
            merge(lo, cnt, 1)

    sort(0, n)
    return pairs


_SORT16 = _oddeven_mergesort_pairs(PEER_TOPK)


def _exchange(y, i, j):
    y[i], y[j] = jnp.maximum(y[i], y[j]), jnp.minimum(y[i], y[j])


def _top17_network(s):
    n = PEER_TOPK
    y = [s[v * 8:(v + 1) * 8, :] for v in range(n)]
    for i, j in _SORT16:
        _exchange(y, i, j)
    dropped = None
    for shift in (4, 2, 1):
        other = [pltpu.roll(y[n - 1 - i], shift, 0) for i in range(n)]
        low = functools.reduce(jnp.maximum, [jnp.minimum(y[i], other[i]) for i in range(n)])
        dropped = low if dropped is None else jnp.maximum(dropped, low)
        y = [jnp.maximum(y[i], other[i]) for i in range(n)]
        dist = n // 2
        while dist:
            for i in range(n):
                if not i & dist:
                    _exchange(y, i, i + dist)
            dist //= 2
    return [v[0:1, :] for v in y] + [jnp.max(dropped, axis=0, keepdims=True)]


_ROUTER_UNROLL = 8


def _router_kernel(qt_ref, keys_ref, e1_ref, e2_ref, th_ref, cand_ref):
    cidx = _iota((_N_CAND, LANES), 0).astype(f32)
    k0 = keys_ref[0].astype(bf16)
    k1 = keys_ref[1].astype(bf16)
    cand_ref[...] = jnp.full(cand_ref.shape, NEG_INF, f32)

    def one_head(h, slot):
        r0 = pl.multiple_of(h * 2 * N_KEYS, 2 * N_KEYS)
        s1 = _dot(k0, qt_ref[pl.ds(r0, N_KEYS), :].astype(bf16))
        s2 = _dot(k1, qt_ref[pl.ds(r0 + N_KEYS, N_KEYS), :].astype(bf16))
        top_a = _top17_network(s1)
        top_b = _top17_network(s2)
        for ci, (a, b) in enumerate(_CAND_PAIRS):
            cand_ref[slot, ci:ci + 1, :] = top_a[a] + top_b[b]
        cs = _extract_top(cand_ref[slot], cidx, _N_TOP)
        z = jnp.ones_like(cs[0])
        for r in range(1, PEER_TOPK):
            z = z + jnp.exp(cs[r] - cs[0])
        inv_z = 1.0 / z
        mid = 0.5 * (cs[PEER_TOPK - 1] + cs[PEER_TOPK])
        o0 = pl.multiple_of(h * N_KEYS, N_KEYS)
        e1 = jnp.exp(s1 - top_a[0]) * inv_z
        e1_ref[:, h] = e1.reshape(N_KEYS // 8, 8, LANES)
        e2_ref[pl.ds(o0, N_KEYS), :] = jnp.exp(s2 - top_b[0])
        th_ref[pl.ds(h, 1), :] = jnp.exp(mid - cs[0]) * inv_z

    def body(hp, _):
        for slot in range(_ROUTER_UNROLL):
            one_head(hp * _ROUTER_UNROLL + slot, slot)
        return 0

    lax.fori_loop(0, PEER_HEADS // _ROUTER_UNROLL, body, 0)


def peer_router(qt, keys):
    n, m = qt.shape
    half = PEER_HEADS * N_KEYS
    return pl.pallas_call(
        _router_kernel,
        grid=(m // LANES,),
        in_specs=[pl.BlockSpec((n, LANES), lambda i: (0, i)),
                  pl.BlockSpec((2, N_KEYS, N_KEYS), lambda i: (0, 0, 0))],
        out_specs=[pl.BlockSpec((N_KEYS // 8, PEER_HEADS, 8, LANES), lambda i: (0, 0, 0, i)),
                   pl.BlockSpec((half, LANES), lambda i: (0, i)),
                   pl.BlockSpec((PEER_HEADS, LANES), lambda i: (0, i))],
        out_shape=[jax.ShapeDtypeStruct((N_KEYS // 8, PEER_HEADS, 8, m), f32),
                   jax.ShapeDtypeStruct((half, m), f32),
                   jax.ShapeDtypeStruct((PEER_HEADS, m), f32)],
        scratch_shapes=[pltpu.VMEM((_ROUTER_UNROLL, _N_CAND, LANES), f32)],
        compiler_params=_cparams("arbitrary"),
        name="peer_router",
    )(qt, keys)


def _experts_kernel(xmt_ref, e1_ref, e2_ref, th_ref, u_ref, v_ref, x_ref, gt_ref, gf_ref,
                    o_ref, w_ref, *, final_norm):
    c = pl.program_id(1)
    tm, te = w_ref.shape
    ni = te // N_KEYS

    @pl.when(c == 0)
    def _():
        o_ref[...] = jnp.zeros_like(o_ref)

    act = _gelu(_dot(u_ref[...], xmt_ref[...]))
    for ii in range(ni):
        es = slice(ii * N_KEYS, (ii + 1) * N_KEYS)
        for lc in range(tm // LANES):
            ls = slice(lc * LANES, (lc + 1) * LANES)
            g = jnp.zeros((N_KEYS, LANES), f32)
            for h in range(PEER_HEADS):
                row = e1_ref[ii // 8, h, ii % 8:ii % 8 + 1, ls]
                pr = e2_ref[h * N_KEYS:(h + 1) * N_KEYS, ls] * row
                g = g + jnp.where(pr >= th_ref[h:h + 1, ls], pr, 0.0)
            w_ref[ls, es] = (g * act[es, ls]).T.astype(bf16)
    o_ref[...] += _dot(w_ref[...], v_ref[...])

    @pl.when(c == pl.num_programs(1) - 1)
    def _():
        y = x_ref[...] + gt_ref[0] * o_ref[...]
        if final_norm:
            ms = jnp.mean(y * y, axis=-1, keepdims=True)
            y = y * lax.rsqrt(ms + EPS) * gf_ref[...]
        o_ref[...] = y


def peer_experts(xmt, e1t, e2t, th, u, v, x, gate, g_final, rows_per_batch, tm, te, final_norm):
    m, d = x.shape
    gt, gt_spec = _mod_specs(gate, m, tm, d, rows_per_batch)
    half = PEER_HEADS * N_KEYS
    return pl.pallas_call(
        functools.partial(_experts_kernel, final_norm=final_norm),
        grid=(m // tm, N_EXPERTS // te),
        in_specs=[
            pl.BlockSpec((d, tm), lambda i, c: (0, i)),
            pl.BlockSpec((te // N_KEYS // 8, PEER_HEADS, 8, tm), lambda i, c: (c, 0, 0, i)),
            pl.BlockSpec((half, tm), lambda i, c: (0, i)),
            pl.BlockSpec((PEER_HEADS, tm), lambda i, c: (0, i)),
            pl.BlockSpec((te, d), lambda i, c: (c, 0)),
            pl.BlockSpec((te, d), lambda i, c: (c, 0)),
            pl.BlockSpec((tm, d), lambda i, c: (i, 0)),
            gt_spec,
            pl.BlockSpec((1, d), lambda i, c: (0, 0)),
        ],
        out_specs=pl.BlockSpec((tm, d), lambda i, c: (i, 0)),
        out_shape=jax.ShapeDtypeStruct((m, d), f32),
        scratch_shapes=[pltpu.VMEM((tm, te), bf16)],
        compiler_params=_cparams("arbitrary", "arbitrary"),
        name="peer_experts",
    )(xmt, e1t, e2t, th, u, v, x, gt, g_final.reshape(1, d))


def peer_block(x, g, shift, scale, gate, wq_t, keys, u, v, g_final, rows_per_batch, tm, te, final_norm):
    tm_q = 1024 if x.shape[0] % 1024 == 0 and rows_per_batch % 1024 == 0 else tm
    qt, xmt = peer_query(x, g, shift, scale, wq_t, rows_per_batch, tm_q)
    e1t, e2t, th = peer_router(qt, keys)
    return peer_experts(xmt, e1t, e2t, th, u, v, x, gate, g_final, rows_per_batch, tm, te, final_norm)


def kernel(x_prompt, x_sample, c_prompt, c_sample, state_rglru_conv, state_rglru_h, cache_swa_k, cache_swa_v, cache_sb_k, cache_sb_v, state_mlstm_C, state_mlstm_n, state_mlstm_m, page_table, w_ada, b_ada, g_norm_mix, g_norm_ffn, e_w_in, e_conv_w, e_conv_b, e_w_r, e_b_r, e_w_i, e_b_i, e_lambda, e_w_out, o_w_in, o_b_if, o_sb_bias, o_g_mnorm, o_w_out, peer_w_q, peer_keys, peer_u, peer_v, g_final):
    bp, seq, d = x_prompt.shape
    bs = x_sample.shape[0]
    mp = bp * seq
    pad_s = LANES
    xp = x_prompt.reshape(mp, d)
    xs = x_sample.reshape(bs, d)

    c_rows = 16
    c_all = jnp.concatenate([c_prompt, c_sample, jnp.zeros((c_rows - bp - bs, d), f32)], axis=0)
    mod = adaln_all(c_all, w_ada, b_ada)

    def mods(layer):
        parts = [mod[layer, :, i * d:(i + 1) * d] for i in range(6)]
        return [p[:bp] for p in parts], [p[bp:bp + bs] for p in parts]

    ctab_p, stab_p = _rope_tables(jnp.arange(seq, dtype=jnp.int32))
    ctab_s, stab_s = _rope_tables(jnp.full((1,), PAST_LEN, jnp.int32))

    TM = 512
    m_p, m_s = mods(0)
    w_in = e_w_in[0].astype(bf16)
    w_out = e_w_out[0].astype(bf16)
    cw, cb = e_conv_w[0], e_conv_b[0].reshape(1, -1)
    wr, wi = e_w_r[0].astype(bf16), e_w_i[0].astype(bf16)
    br, bi, lam = e_b_r[0].reshape(1, -1), e_b_i[0].reshape(1, -1), e_lambda[0].reshape(1, -1)

    proj_p = mod_matmul(xp, g_norm_mix[0], m_p[0], m_p[1], w_in, E_IN, seq, 1024, 1024, name="e_in_p")
    proj_s = mod_matmul(xs, g_norm_mix[0], m_s[0], m_s[1], w_in, E_IN, 1, bs, 512, name="e_in_s")

    ya_p, h_p = rglru_prompt(proj_p, bp, seq, cw, cb, wr, br, wi, bi, lam)
    ya_s, h_s = rglru_step(proj_s, state_rglru_conv[0], state_rglru_h[0], cw, cb, wr, br, wi, bi, lam)
    conv_p = proj_p.reshape(bp, seq, E_IN)[:, seq - (CONV_W - 1):, :RG_WIDTH]
    conv_s = jnp.concatenate([state_rglru_conv[0][:, 1:], proj_s[:, None, :RG_WIDTH]], axis=1)

    qkv_p, k_p = rope_split(proj_p, ctab_p, stab_p, bp, seq)
    q_s, k_s, v_s = rope_qkv(proj_s, ctab_s, stab_s, 1, bs)
    o_p = dilated_prompt(qkv_p, bp, seq)
    win = cache_swa_k.shape[2]
    o_s = dilated_step(q_s, k_s, v_s, cache_swa_k.reshape(-1, win, HEADS, DH)[:bs],
                       cache_swa_v.reshape(-1, win, HEADS, DH)[:bs])
    wl = min(2048, seq)
    swa_k_p, swa_v_p = [a.reshape(bp, wl, HEADS, DH)
                        for a in heads_layout([(k_p, 0), (proj_p, E_IN // ATT_W - 1)], bp, seq, wl)]
    swa_k_s = k_s.reshape(bs, 1, HEADS, DH)
    swa_v_s = v_s.reshape(bs, 1, HEADS, DH)

    xp = out_proj(ya_p, o_p, w_out, xp, m_p[2], seq, 1024, 1024, name="e_out_p")
    xs = out_proj(ya_s, o_s, w_out, xs, m_s[2], 1, bs, 512, name="e_out_s")

    def peer_layer(layer, xp, xs, m_p, m_s, final_norm):
        wq_t = peer_w_q[layer].T.astype(bf16)
        u = cast_layer_bf16(peer_u, layer)
        v = cast_layer_bf16(peer_v, layer)
        xp = peer_block(xp, g_norm_ffn[layer], m_p[3], m_p[4], m_p[5], wq_t, peer_keys[layer], u, v,
                        g_final, seq, TM, 1024, final_norm)
        xs_pad = jnp.pad(xs, ((0, pad_s - bs), (0, 0)))
        xs_new = peer_block(xs_pad, g_norm_ffn[layer], m_s[3], m_s[4], m_s[5], wq_t, peer_keys[layer],
                            u, v, g_final, 1, pad_s, 1024, final_norm)
        return xp, xs_new[:bs]

    xp, xs = peer_layer(0, xp, xs, m_p, m_s, False)

    m_p, m_s = mods(1)
    w_in2 = o_w_in[0].astype(bf16)
    w_gate = jnp.pad(w_in2[:, O_MAIN:], ((0, 0), (0, LANES - 2 * M_HEADS)))
    w_out2 = o_w_out[0].astype(bf16)
    gate_bias = jnp.pad(o_b_if[0].reshape(1, 2 * M_HEADS), ((0, 0), (0, LANES - 2 * M_HEADS)))
    gmn = o_g_mnorm[0].reshape(1, M_WIDTH)

    proj2_p, gates_p = mod_matmul(xp, g_norm_mix[1], m_p[0], m_p[1], w_in2, O_MAIN, seq, 1024, 1024,
                                  w_gate=w_gate, name="o_in_p")
    proj2_s, gates_s = mod_matmul(xs, g_norm_mix[1], m_s[0], m_s[1], w_in2, O_MAIN, 1, bs, 512,
                                  w_gate=w_gate, name="o_in_s")

    oc_p = sb_prompt(proj2_p, o_sb_bias[0], bp, seq)
    oc_s = sb_step(proj2_s[:, :ATT_W], o_sb_bias[0], cache_sb_k.reshape(-1, PAGE, HEADS, DH),
                   cache_sb_v.reshape(-1, PAGE, HEADS, DH), page_table)
    hm_p, mC_p, mn_p, mm_p = mlstm_prompt(proj2_p, gates_p, gate_bias, gmn, bp, seq)
    hm_s, mC_s, mn_s, mm_s = mlstm_step(proj2_s, gates_s, gate_bias, gmn,
                                        state_mlstm_C.reshape(-1, M_HEADS, M_DH, M_DH)[:bs],
                                        state_mlstm_n[0], state_mlstm_m[0])

    n_pg = seq // PAGE
    sb_k_p, sb_v_p = [a.reshape(bp, n_pg, PAGE, HEADS, DH)
                      for a in heads_layout([(proj2_p, 1), (proj2_p, 2)], bp, seq, seq)]
    sb_k_s = proj2_s[:, ATT_W:2 * ATT_W].reshape(bs, 1, HEADS, DH)
    sb_v_s = proj2_s[:, 2 * ATT_W:3 * ATT_W].reshape(bs, 1, HEADS, DH)

    xp = out_proj(oc_p, hm_p, w_out2, xp, m_p[2], seq, 1024, 1024, name="o_out_p")
    xs = out_proj(oc_s, hm_s, w_out2, xs, m_s[2], 1, bs, 512, name="o_out_s")
    xp, xs = peer_layer(1, xp, xs, m_p, m_s, True)

    y_prompt = xp.reshape(bp, seq, d)
    y_sample = xs.reshape(bs, 1, d)
    st = lambda a: a[None]
    return (y_prompt, y_sample, st(conv_p), st(conv_s), st(h_p.reshape(bp, RG_WIDTH)), st(h_s),
            st(swa_k_p), st(swa_k_s), st(swa_v_p), st(swa_v_s),
            st(sb_k_p), st(sb_k_s), st(sb_v_p), st(sb_v_s),
            st(mC_p), st(mC_s), st(mn_p), st(mn_s), st(mm_p), st(mm_s))
```

```python
import functools
import math

import jax
import jax.numpy as jnp
from jax import lax
from jax.experimental import pallas as pl
from jax.experimental.pallas import tpu as pltpu

f32 = jnp.float32
bf16 = jnp.bfloat16

D_MODEL = 2048
PAST_LEN = 16384
PAGE = 128
RG_WIDTH = 1024
RG_BLOCKS = 8
CONV_W = 4
RG_C = 8.0
HEADS = 8
DH = 128
ATT_W = HEADS * DH
DIL_PATTERNS = ((128, 1), (512, 4), (2048, 16))
ROT_DIMS = 32
ROPE_THETA = 500000.0
M_HEADS = 4
M_DH = 256
M_WIDTH = M_HEADS * M_DH
CHUNK = 128
E_IN = 2 * RG_WIDTH + 3 * ATT_W
O_MAIN = 3 * ATT_W + 4 * M_WIDTH
PEER_HEADS = 8
N_KEYS = 128
N_EXPERTS = N_KEYS * N_KEYS
PEER_TOPK = 16
EPS = 1e-6
LANES = 128
VMEM_LIMIT = 56 * 1024 * 1024
NEG_INF = float("-inf")


def _cparams(*sem):
    return pltpu.CompilerParams(dimension_semantics=sem, vmem_limit_bytes=VMEM_LIMIT)


def _dot(a, b):
    return jnp.dot(a, b, preferred_element_type=f32)


def _dot_nt(a, b):
    return lax.dot_general(a, b, (((1,), (1,)), ((), ())), preferred_element_type=f32)


def _split3(x):
    hi = x.astype(bf16)
    r = x - hi.astype(f32)
    mid = r.astype(bf16)
    lo = (r - mid.astype(f32)).astype(bf16)
    return hi, mid, lo


def _sigmoid(x):
    return 1.0 / (1.0 + jnp.exp(-x))


def _log_sigmoid_pair(z):
    l1p = jnp.log1p(jnp.exp(-jnp.abs(z)))
    return jnp.minimum(z, 0.0) - l1p, -jnp.maximum(z, 0.0) - l1p


def _softplus(z):
    return jnp.maximum(z, 0.0) + jnp.log(1.0 + jnp.exp(-jnp.abs(z)))


def _gelu(x):
    c = math.sqrt(2.0 / math.pi)
    return 0.5 * x * (1.0 + jnp.tanh(c * (x + 0.044715 * (x * x * x))))


def _iota(shape, dim):
    return lax.broadcasted_iota(jnp.int32, shape, dim)


def _rowsum_bcast(x):
    ones = jnp.ones((LANES, LANES), bf16)
    hi = x.astype(bf16)
    lo = (x - hi.astype(f32)).astype(bf16)
    return _dot(hi, ones) + _dot(lo, ones)


def _adaln_kernel(c_ref, w_ref, b_ref, o_ref):
    c = c_ref[...]
    s = c * _sigmoid(c)
    w = w_ref[0]
    s_hi = s.astype(bf16)
    s_lo = (s - s_hi.astype(f32)).astype(bf16)
    w_hi = w.astype(bf16)
    w_lo = (w - w_hi.astype(f32)).astype(bf16)
    o_ref[0] = _dot(s_hi, w_hi) + _dot(s_hi, w_lo) + _dot(s_lo, w_hi) + b_ref[0]


def adaln_all(c_all, w_ada, b_ada):
    depth, d, n = w_ada.shape
    rows = c_all.shape[0]
    tn = 1024
    return pl.pallas_call(
        _adaln_kernel,
        grid=(depth, n // tn),
        in_specs=[
            pl.BlockSpec((rows, d), lambda l, j: (0, 0)),
            pl.BlockSpec((1, d, tn), lambda l, j: (l, 0, j)),
            pl.BlockSpec((1, 1, tn), lambda l, j: (l, 0, j)),
        ],
        out_specs=pl.BlockSpec((1, rows, tn), lambda l, j: (l, 0, j)),
        out_shape=jax.ShapeDtypeStruct((depth, rows, n), f32),
        compiler_params=_cparams("arbitrary", "arbitrary"),
        name="adaln",
    )(c_all, w_ada, b_ada.reshape(depth, 1, n))


def _modulated(x, g, sh, sc):
    ms = jnp.mean(x * x, axis=-1, keepdims=True)
    y = x * lax.rsqrt(ms + EPS) * g
    return y * (1.0 + sc) + sh


def _modmm_kernel(x_ref, g_ref, sh_ref, sc_ref, w_ref, o_ref, xn_ref):
    @pl.when(pl.program_id(1) == 0)
    def _():
        xn_ref[...] = _modulated(x_ref[...], g_ref[...], sh_ref[0], sc_ref[0]).astype(bf16)

    o_ref[...] = _dot(xn_ref[...], w_ref[...])


def _modmm_gate_kernel(x_ref, g_ref, sh_ref, sc_ref, w_ref, wg_ref, o_ref, og_ref, xn_ref):
    @pl.when(pl.program_id(1) == 0)
    def _():
        xn = _modulated(x_ref[...], g_ref[...], sh_ref[0], sc_ref[0]).astype(bf16)
        xn_ref[...] = xn
        og_ref[...] = _dot(xn, wg_ref[...])

    o_ref[...] = _dot(xn_ref[...], w_ref[...])


def _mod_specs(mod, m, tm, k, rows_per_batch):
    if rows_per_batch >= tm:
        assert rows_per_batch % tm == 0
        per = rows_per_batch // tm
        return mod.reshape(-1, 1, k), pl.BlockSpec((1, 1, k), lambda i, j: (i // per, 0, 0))
    assert rows_per_batch == 1
    rows = mod
    if rows.shape[0] < m:
        rows = jnp.pad(rows, ((0, m - rows.shape[0]), (0, 0)))
    return rows.reshape(1, m, k), pl.BlockSpec((1, tm, k), lambda i, j: (0, i, 0))


def mod_matmul(x, g, shift, scale, w, n_out, rows_per_batch, tm, tn, w_gate=None, name="modmm"):
    m, k = x.shape
    sh, sh_spec = _mod_specs(shift, m, tm, k, rows_per_batch)
    sc, sc_spec = _mod_specs(scale, m, tm, k, rows_per_batch)
    in_specs = [
        pl.BlockSpec((tm, k), lambda i, j: (i, 0)),
        pl.BlockSpec((1, k), lambda i, j: (0, 0)),
        sh_spec,
        sc_spec,
        pl.BlockSpec((k, tn), lambda i, j: (0, j)),
    ]
    args = [x, g.reshape(1, k), sh, sc, w]
    out_specs = pl.BlockSpec((tm, tn), lambda i, j: (i, j))
    out_shape = jax.ShapeDtypeStruct((m, n_out), f32)
    kern = _modmm_kernel
    if w_gate is not None:
        in_specs.append(pl.BlockSpec((k, LANES), lambda i, j: (0, 0)))
        args.append(w_gate)
        out_specs = [out_specs, pl.BlockSpec((tm, LANES), lambda i, j: (i, 0))]
        out_shape = [out_shape, jax.ShapeDtypeStruct((m, LANES), f32)]
        kern = _modmm_gate_kernel
    return pl.pallas_call(
        kern,
        grid=(m // tm, n_out // tn),
        in_specs=in_specs,
        out_specs=out_specs,
        out_shape=out_shape,
        scratch_shapes=[pltpu.VMEM((tm, k), bf16)],
        compiler_params=_cparams("arbitrary", "arbitrary"),
        name=name,
    )(*args)


def _outproj_kernel(a1_ref, a2_ref, w1_ref, w2_ref, x_ref, gt_ref, o_ref):
    y = _dot(a1_ref[...].astype(bf16), w1_ref[...]) + _dot(a2_ref[...].astype(bf16), w2_ref[...])
    o_ref[...] = x_ref[...] + gt_ref[0] * y


def out_proj(a1, a2, w, x, gate, rows_per_batch, tm, tn, name="outproj"):
    m, k1 = a1.shape
    k2 = a2.shape[1]
    n = w.shape[1]
    gt, gt_spec = _mod_specs(gate, m, tm, n, rows_per_batch)
    if gt.shape[1] == 1:
        per = rows_per_batch // tm
        gt_spec = pl.BlockSpec((1, 1, tn), lambda i, j: (i // per, 0, j))
    else:
        gt_spec = pl.BlockSpec((1, tm, tn), lambda i, j: (0, i, j))
    return pl.pallas_call(
        _outproj_kernel,
        grid=(m // tm, n // tn),
        in_specs=[
            pl.BlockSpec((tm, k1), lambda i, j: (i, 0)),
            pl.BlockSpec((tm, k2), lambda i, j: (i, 0)),
            pl.BlockSpec((k1, tn), lambda i, j: (0, j)),
            pl.BlockSpec((k2, tn), lambda i, j: (k1 // k2, j)),
            pl.BlockSpec((tm, tn), lambda i, j: (i, j)),
            gt_spec,
        ],
        out_specs=pl.BlockSpec((tm, tn), lambda i, j: (i, j)),
        out_shape=jax.ShapeDtypeStruct((m, n), f32),
        compiler_params=_cparams("arbitrary", "arbitrary"),
        name=name,
    )(a1, a2, w, w, x, gt)


def _rglru_gates(xc, wr_ref, br, wi_ref, bi, lam):
    xb = xc.astype(bf16)
    rs, gs = [], []
    for hb in range(RG_BLOCKS):
        sl = slice(hb * LANES, (hb + 1) * LANES)
        rs.append(_dot(xb[:, sl], wr_ref[hb]))
        gs.append(_dot(xb[:, sl], wi_ref[hb]))
    r = _sigmoid(jnp.concatenate(rs, axis=1) + br)
    ig = _sigmoid(jnp.concatenate(gs, axis=1) + bi)
    softplus_neg_lam = jnp.maximum(-lam, 0.0) + jnp.log1p(jnp.exp(-jnp.abs(lam)))
    log_a = -RG_C * r * softplus_neg_lam
    a = jnp.exp(log_a)
    u = jnp.sqrt(-jnp.tanh(log_a) * (a * a + 1.0)) * ig * xc
    return a, u


def _rglru_kernel(xa_ref, ga_ref, cw_ref, cb_ref, wr_ref, br_ref, wi_ref, bi_ref, lam_ref,
                  ya_ref, hl_ref, xprev_ref, hc_ref):
    t_idx = pl.program_id(1)
    tt = xa_ref.shape[0]

    @pl.when(t_idx == 0)
    def _():
        xprev_ref[...] = jnp.zeros_like(xprev_ref)
        hc_ref[...] = jnp.zeros_like(hc_ref)

    xa = xa_ref[...]
    xprev = xprev_ref[...]
    row8 = _iota((8, RG_WIDTH), 0)
    xc = cb_ref[...] + cw_ref[CONV_W - 1:CONV_W, :] * xa
    for k in range(1, CONV_W):
        rolled = pltpu.roll(xa, k, 0)
        head = jnp.where(row8 < k, pltpu.roll(xprev, k, 0), rolled[0:8])
        shifted = jnp.concatenate([head, rolled[8:]], axis=0)
        xc = xc + cw_ref[CONV_W - 1 - k:CONV_W - k, :] * shifted
    xprev_ref[...] = xa[tt - 8:tt]

    a, u = _rglru_gates(xc, wr_ref, br_ref[...], wi_ref, bi_ref[...], lam_ref[...])
    row = _iota((tt, RG_WIDTH), 0)
    s = 1
    while s < tt:
        a_sh = pltpu.roll(a, s, 0)
        u_sh = pltpu.roll(u, s, 0)
        ok = row >= s
        u = jnp.where(ok, a * u_sh + u, u)
        a = jnp.where(ok, a * a_sh, a)
        s *= 2
    h = a * hc_ref[...] + u
    hc_ref[...] = h[tt - 1:tt]
    hl_ref[0] = h[tt - 1:tt]
    ya_ref[...] = h * _gelu(ga_ref[...])


def rglru_prompt(proj, bsz, seq, cw, cb, wr, br, wi, bi, lam, tt=256):
    nt = seq // tt
    vec = lambda: pl.BlockSpec((1, RG_WIDTH), lambda b, t: (0, 0))
    return pl.pallas_call(
        _rglru_kernel,
        grid=(bsz, nt),
        in_specs=[
            pl.BlockSpec((tt, RG_WIDTH), lambda b, t: (b * nt + t, 0)),
            pl.BlockSpec((tt, RG_WIDTH), lambda b, t: (b * nt + t, 1)),
            pl.BlockSpec((CONV_W, RG_WIDTH), lambda b, t: (0, 0)),
            vec(),
            pl.BlockSpec((RG_BLOCKS, LANES, LANES), lambda b, t: (0, 0, 0)),
            vec(),
            pl.BlockSpec((RG_BLOCKS, LANES, LANES), lambda b, t: (0, 0, 0)),
            vec(),
            vec(),
        ],
        out_specs=[
            pl.BlockSpec((tt, RG_WIDTH), lambda b, t: (b * nt + t, 0)),
            pl.BlockSpec((1, 1, RG_WIDTH), lambda b, t: (b, 0, 0)),
        ],
        out_shape=[
            jax.ShapeDtypeStruct((bsz * seq, RG_WIDTH), f32),
            jax.ShapeDtypeStruct((bsz, 1, RG_WIDTH), f32),
        ],
        scratch_shapes=[pltpu.VMEM((8, RG_WIDTH), f32), pltpu.VMEM((1, RG_WIDTH), f32)],
        compiler_params=_cparams("arbitrary", "arbitrary"),
        name="rglru_prompt",
    )(proj, proj, cw, cb, wr, br, wi, bi, lam)


def _rglru_step_kernel(xa_ref, ga_ref, b0_ref, b1_ref, b2_ref, h0_ref, cw_ref, cb_ref,
                       wr_ref, br_ref, wi_ref, bi_ref, lam_ref, ya_ref, h_ref):
    xa = xa_ref[...]
    xc = (cb_ref[...] + cw_ref[0:1, :] * b0_ref[...] + cw_ref[1:2, :] * b1_ref[...]
          + cw_ref[2:3, :] * b2_ref[...] + cw_ref[3:4, :] * xa)
    a, u = _rglru_gates(xc, wr_ref, br_ref[...], wi_ref, bi_ref[...], lam_ref[...])
    h = a * h0_ref[...] + u
    h_ref[...] = h
    ya_ref[...] = h * _gelu(ga_ref[...])


def rglru_step(proj_s, conv_state, h0, cw, cb, wr, br, wi, bi, lam):
    n = proj_s.shape[0]
    full = lambda shape: pl.BlockSpec(shape, lambda i: tuple(0 for _ in shape))
    return pl.pallas_call(
        _rglru_step_kernel,
        grid=(1,),
        in_specs=[
            pl.BlockSpec((n, RG_WIDTH), lambda i: (0, 0)),
            pl.BlockSpec((n, RG_WIDTH), lambda i: (0, 1)),
            full((n, RG_WIDTH)), full((n, RG_WIDTH)), full((n, RG_WIDTH)), full((n, RG_WIDTH)),
            full((CONV_W, RG_WIDTH)), full((1, RG_WIDTH)),
            full((RG_BLOCKS, LANES, LANES)), full((1, RG_WIDTH)),
            full((RG_BLOCKS, LANES, LANES)), full((1, RG_WIDTH)), full((1, RG_WIDTH)),
        ],
        out_specs=[full((n, RG_WIDTH)), full((n, RG_WIDTH))],
        out_shape=[jax.ShapeDtypeStruct((n, RG_WIDTH), f32)] * 2,
        compiler_params=_cparams("arbitrary"),
        name="rglru_step",
    )(proj_s, proj_s, conv_state[:, 0], conv_state[:, 1], conv_state[:, 2], h0,
      cw, cb, wr, br, wi, bi, lam)


def _rope_tables(pos):
    half = ROT_DIMS // 2
    inv = ROPE_THETA ** (-jnp.arange(half, dtype=f32) / half)
    ang = pos.astype(f32)[:, None] * inv[None, :]
    cos, sin = jnp.cos(ang), jnp.sin(ang)
    n = pos.shape[0]
    ctab = jnp.concatenate([cos, cos, jnp.ones((n, DH - ROT_DIMS), f32)], axis=1)
    stab = jnp.concatenate([-sin, sin, jnp.zeros((n, DH - ROT_DIMS), f32)], axis=1)
    return ctab, stab


def _rope_head(xh, ctab, stab, lane):
    half = ROT_DIMS // 2
    partner = jnp.where(lane < half, pltpu.roll(xh, DH - half, 1), pltpu.roll(xh, half, 1))
    return xh * ctab + partner * stab


def _rope_kernel(q_ref, k_ref, v_ref, c_ref, s_ref, qo_ref, ko_ref, vo_ref):
    ctab, stab = c_ref[...], s_ref[...]
    lane = _iota((q_ref.shape[0], DH), 1)
    for hb in range(HEADS):
        sl = slice(hb * DH, (hb + 1) * DH)
        qo_ref[:, hb, :] = _rope_head(q_ref[:, sl], ctab, stab, lane) * (DH ** -0.5)
        ko_ref[:, hb, :] = _rope_head(k_ref[:, sl], ctab, stab, lane)
        vo_ref[:, hb, :] = v_ref[:, sl]


def rope_qkv(proj, ctab, stab, rows_per_seq, tt):
    m = proj.shape[0]
    nt = max(rows_per_seq // tt, 1)
    if ctab.shape[0] == 1:
        tab_spec = pl.BlockSpec((1, DH), lambda i: (0, 0))
    else:
        tab_spec = pl.BlockSpec((tt, DH), lambda i: (i % nt, 0))
    return pl.pallas_call(
        _rope_kernel,
        grid=(m // tt,),
        in_specs=[
            pl.BlockSpec((tt, ATT_W), lambda i: (i, 2)),
            pl.BlockSpec((tt, ATT_W), lambda i: (i, 3)),
            pl.BlockSpec((tt, ATT_W), lambda i: (i, 4)),
            tab_spec, tab_spec,
        ],
        out_specs=[pl.BlockSpec((tt, HEADS, DH), lambda i: (i, 0, 0))] * 3,
        out_shape=[jax.ShapeDtypeStruct((m, HEADS, DH), f32)] * 3,
        compiler_params=_cparams("arbitrary"),
        name="rope",
    )(proj, proj, proj, ctab, stab)


def _rope_split_kernel(q_ref, k_ref, v_ref, c_ref, s_ref, *refs):
    n_pat = len(DIL_PATTERNS)
    outs, (kf_ref, qs_ref, ks_ref) = refs[:3 * n_pat], refs[3 * n_pat:]
    tt = q_ref.shape[0]
    ctab, stab = c_ref[...], s_ref[...]
    lane = _iota((tt, DH), 1)
    qs_ref[...] = _rope_head(q_ref[...], ctab, stab, lane) * (DH ** -0.5)
    k_rot = _rope_head(k_ref[...], ctab, stab, lane)
    ks_ref[...] = k_rot
    kf_ref[...] = k_rot
    for gi, (_, d) in enumerate(DIL_PATTERNS):
        rows = tt // d
        for src, dst in ((qs_ref, outs[3 * gi]), (ks_ref, outs[3 * gi + 1]), (v_ref, outs[3 * gi + 2])):
            for r in range(d):
                dst[0, 0, r] = src[pl.ds(r, rows, stride=d), :].astype(bf16)


def rope_split(proj, ctab, stab, bsz, seq, tt=512):
    m = proj.shape[0]
    nt = seq // tt
    col = lambda c: pl.BlockSpec((tt, DH), lambda i, h, c=c: (i, c * HEADS + h))
    tab = pl.BlockSpec((tt, DH), lambda i, h: (i % nt, 0))
    out_specs, out_shape = [], []
    for (_, d) in DIL_PATTERNS:
        for _ in range(3):
            out_specs.append(pl.BlockSpec((1, 1, d, tt // d, DH), lambda i, h: (i // nt, h, 0, i % nt, 0)))
            out_shape.append(jax.ShapeDtypeStruct((bsz, HEADS, d, seq // d, DH), bf16))
    out_specs.append(pl.BlockSpec((tt, DH), lambda i, h: (i, h)))
    out_shape.append(jax.ShapeDtypeStruct((m, ATT_W), f32))
    res = pl.pallas_call(
        _rope_split_kernel,
        grid=(m // tt, HEADS),
        in_specs=[col(2), col(3), col(4), tab, tab],
        out_specs=out_specs,
        out_shape=out_shape,
        scratch_shapes=[pltpu.VMEM((tt, DH), f32), pltpu.VMEM((tt, DH), f32)],
        compiler_params=_cparams("arbitrary", "arbitrary"),
        name="rope_split",
    )(proj, proj, proj, ctab, stab)
    return [res[3 * gi:3 * gi + 3] for gi in range(len(DIL_PATTERNS))], res[-1]


def _dil_kernel(q_ref, kc_ref, kp_ref, vc_ref, vp_ref, o_ref, l_ref, s_ref, p_ref, *, span):
    tq = q_ref.shape[3]
    blk = pl.program_id(2)
    qi = _iota((tq, 2 * tq), 0)
    col = _iota((tq, 2 * tq), 1)
    rel = jnp.where(col < tq, qi - col, qi - col + 2 * tq)
    ok = (rel >= 0) & (rel <= span) & ((col < tq) | (blk > 0))
    for hb in range(HEADS):
        qh = q_ref[0, hb, 0]
        s_ref[hb, :, :tq] = _dot_nt(qh, kc_ref[0, hb, 0])
        s_ref[hb, :, tq:] = _dot_nt(qh, kp_ref[0, hb, 0])
    for hb in range(HEADS):
        s = jnp.where(ok, s_ref[hb], NEG_INF)
        mx = jnp.max(s, axis=1, keepdims=True)
        p = jnp.exp(s - mx)
        den = jnp.sum(p, axis=1, keepdims=True)
        p_ref[hb] = (p * (1.0 / den)).astype(bf16)
        l_ref[0, hb, 0] = jnp.broadcast_to(mx + jnp.log(den), (tq, DH))
    for hb in range(HEADS):
        o_ref[0, hb, 0] = (_dot(p_ref[hb, :, :tq], vc_ref[0, hb, 0])
                           + _dot(p_ref[hb, :, tq:], vp_ref[0, hb, 0]))


def _dil_merge_kernel(*refs):
    n_pat = len(DIL_PATTERNS)
    ins, o_ref, scr = refs[:2 * n_pat], refs[2 * n_pat], refs[2 * n_pat + 1:]
    tt = o_ref.shape[0]
    for gi, (_, d) in enumerate(DIL_PATTERNS):
        rows = tt // d
        for src, dst in ((ins[2 * gi], scr[2 * gi]), (ins[2 * gi + 1], scr[2 * gi + 1])):
            for r in range(d):
                dst[pl.ds(r, rows, stride=d), :] = src[0, 0, r]
    lses = [scr[2 * gi + 1][...] for gi in range(n_pat)]
    top = functools.reduce(jnp.maximum, lses)
    es = [jnp.exp(l - top) for l in lses]
    num = sum(scr[2 * gi][...] * es[gi] for gi in range(n_pat))
    o_ref[...] = num / sum(es)


def dilated_prompt(qkv_by_pattern, bsz, seq, tq=128, tt=512):
    partial = []
    for (w, d), (qd, kd, vd) in zip(DIL_PATTERNS, qkv_by_pattern):
        sd = seq // d
        blk = (1, HEADS, 1, tq, DH)
        cur = pl.BlockSpec(blk, lambda b, r, i: (b, 0, r, i, 0))
        prev = pl.BlockSpec(blk, lambda b, r, i: (b, 0, r, jnp.maximum(i - 1, 0), 0))
        partial += pl.pallas_call(
            functools.partial(_dil_kernel, span=w // d),
            grid=(bsz, d, sd // tq),
            in_specs=[cur, cur, prev, cur, prev],
            out_specs=[cur, cur],
            out_shape=[jax.ShapeDtypeStruct((bsz, HEADS, d, sd, DH), f32)] * 2,
            scratch_shapes=[pltpu.VMEM((HEADS, tq, 2 * tq), f32), pltpu.VMEM((HEADS, tq, 2 * tq), bf16)],
            compiler_params=_cparams("arbitrary", "arbitrary", "arbitrary"),
            name=f"dilattn_d{d}",
        )(qd, kd, kd, vd, vd)
    nt = seq // tt
    in_specs = []
    for (_, d) in DIL_PATTERNS:
        in_specs += [pl.BlockSpec((1, 1, d, tt // d, DH), lambda i, h: (i // nt, h, 0, i % nt, 0))] * 2
    return pl.pallas_call(
        _dil_merge_kernel,
        grid=(bsz * nt, HEADS),
        in_specs=in_specs,
        out_specs=pl.BlockSpec((tt, DH), lambda i, h: (i, h)),
        out_shape=jax.ShapeDtypeStruct((bsz * seq, ATT_W), f32),
        scratch_shapes=[pltpu.VMEM((tt, DH), f32)] * (2 * len(DIL_PATTERNS)),
        compiler_params=_cparams("arbitrary", "arbitrary"),
        name="dilattn_merge",
    )(*partial)


def _dil_step_kernel(q_ref, kn_ref, vn_ref, k1_ref, k4_ref, k16_ref, v1_ref, v4_ref, v16_ref, o_ref):
    q = q_ref[0]
    kn, vn = kn_ref[0], vn_ref[0]
    s_self = _rowsum_bcast(q * kn)
    o_gs, lse_gs = [], []
    for k_ref, v_ref in ((k1_ref, v1_ref), (k4_ref, v4_ref), (k16_ref, v16_ref)):
        k3 = k_ref[0, :, 0]
        nk = k3.shape[0]
        s = _rowsum_bcast((k3 * q[None]).reshape(nk * HEADS, DH)).reshape(nk, HEADS, DH)
        mx = jnp.maximum(jnp.max(s, axis=0), s_self)
        p = jnp.exp(s - mx[None])
        p_self = jnp.exp(s_self - mx)
        den = jnp.sum(p, axis=0) + p_self
        num = jnp.sum(p * v_ref[0, :, 0], axis=0) + p_self * vn
        o_gs.append(num / den)
        lse_gs.append(mx + jnp.log(den))
    top = jnp.maximum(jnp.maximum(lse_gs[0], lse_gs[1]), lse_gs[2])
    es = [jnp.exp(l - top) for l in lse_gs]
    tot = es[0] + es[1] + es[2]
    o_ref[0] = (o_gs[0] * es[0] + o_gs[1] * es[1] + o_gs[2] * es[2]) / tot


def dilated_step(q4, k4, v4, cache_k, cache_v):
    n, win = cache_k.shape[0], cache_k.shape[1]
    one = pl.BlockSpec((1, HEADS, DH), lambda b: (b, 0, 0))
    args = [q4, k4, v4]
    in_specs = [one, one, one]
    for cache in (cache_k, cache_v):
        for (w, d) in DIL_PATTERNS:
            nkeys = w // d
            assert win % d == 0 and (win // d) % nkeys == 0 and (win - w) % (d * nkeys) == 0
            args.append(cache.reshape(n, win // d, d, HEADS, DH))
            in_specs.append(pl.BlockSpec((1, nkeys, 1, HEADS, DH),
                                         lambda b, blk=(win - w) // d // nkeys: (b, blk, 0, 0, 0)))
    return pl.pallas_call(
        _dil_step_kernel,
        grid=(n,),
        in_specs=in_specs,
        out_specs=one,
        out_shape=jax.ShapeDtypeStruct((n, HEADS, DH), f32),
        compiler_params=_cparams("arbitrary"),
        name="dilattn_step",
    )(*args).reshape(n, ATT_W)


def _sb_kernel(bias_ref, q_ref, k_ref, v_ref, o_ref, kb_ref, vb_ref, qs_ref, t_ref, spb_ref, wb_ref,
               acc_ref, run_ref):
    tq = q_ref.shape[1]
    nh = q_ref.shape[2] // DH
    kt_w = t_ref.shape[2]
    sub = PAGE
    hg = pl.program_id(1)
    qb = pl.program_id(2)

    @pl.when(qb == 0)
    def _():
        kb_ref[...] = k_ref[0].astype(bf16)
        vb_ref[...] = v_ref[0].astype(bf16)

    rr = _iota((sub, 2 * sub), 0)
    cc = _iota((sub, 2 * sub), 1)
    tri = jnp.where((rr > cc) | (cc >= sub), 1.0, 0.0).astype(bf16)
    for h in range(nh):
        qs_ref[h] = (q_ref[0, :, h * DH:(h + 1) * DH] * (DH ** -0.5)).astype(bf16)
    acc_ref[...] = jnp.zeros_like(acc_ref)
    run_ref[...] = jnp.zeros_like(run_ref)

    def macro(start, masked):
        for h in range(nh):
            hs = slice(h * DH, (h + 1) * DH)
            z = _dot_nt(qs_ref[h], kb_ref[pl.ds(start, kt_w), hs]) + bias_ref[hg * nh + h]
            sp = _softplus(z)
            t = z - sp
            if masked:
                ok = start + _iota((tq, kt_w), 1) < qb * tq + _iota((tq, kt_w), 0)
                sp = jnp.where(ok, sp, 0.0)
                t = jnp.where(ok, t, NEG_INF)
            t_ref[h] = t
            spb_ref[h] = sp.astype(bf16)
        for h in range(nh):
            run = run_ref[h]
            for kt in range(kt_w // sub - 1, -1, -1):
                ks = slice(kt * sub, (kt + 1) * sub)
                cs = _dot(spb_ref[h, :, ks], tri)
                wb_ref[h, :, ks] = jnp.exp(t_ref[h, :, ks] - cs[:, :sub] - run).astype(bf16)
                run = run + cs[:, sub:]
            run_ref[h] = run
        for h in range(nh):
            hs = slice(h * DH, (h + 1) * DH)
            acc_ref[h] += _dot(wb_ref[h], vb_ref[pl.ds(start, kt_w), hs])

    top = (qb * tq) // kt_w
    macro(pl.multiple_of(top * kt_w, kt_w), True)

    def body(it, _):
        macro(pl.multiple_of((top - 1 - it) * kt_w, kt_w), False)
        return 0

    lax.fori_loop(0, top, body, 0)
    for h in range(nh):
        o_ref[0, :, h * DH:(h + 1) * DH] = acc_ref[h]


def sb_prompt(proj2, bias, bsz, seq, tq=256, nh=4, kt_w=512):
    assert kt_w % tq == 0 and seq % kt_w == 0
    p3 = proj2.reshape(bsz, seq, proj2.shape[1])
    nq = seq // tq
    ng = HEADS // nh
    wd = nh * DH
    return pl.pallas_call(
        _sb_kernel,
        grid=(bsz, ng, nq),
        in_specs=[
            pl.BlockSpec(memory_space=pltpu.SMEM),
            pl.BlockSpec((1, tq, wd), lambda b, g, i: (b, i, g)),
            pl.BlockSpec((1, seq, wd), lambda b, g, i: (b, 0, ng + g)),
            pl.BlockSpec((1, seq, wd), lambda b, g, i: (b, 0, 2 * ng + g)),
        ],
        out_specs=pl.BlockSpec((1, tq, wd), lambda b, g, i: (b, i, g)),
        out_shape=jax.ShapeDtypeStruct((bsz, seq, ATT_W), f32),
        scratch_shapes=[
            pltpu.VMEM((seq, wd), bf16), pltpu.VMEM((seq, wd), bf16),
            pltpu.VMEM((nh, tq, DH), bf16),
            pltpu.VMEM((nh, tq, kt_w), f32), pltpu.VMEM((nh, tq, kt_w), bf16),
            pltpu.VMEM((nh, tq, kt_w), bf16),
            pltpu.VMEM((nh, tq, DH), f32), pltpu.VMEM((nh, tq, PAGE), f32),
        ],
        compiler_params=_cparams("arbitrary", "arbitrary", "arbitrary"),
        name="sb_prompt",
    )(bias, p3, p3, p3).reshape(bsz * seq, ATT_W)


def _sb_step_kernel(pt_ref, q_ref, bias_ref, eye_ref, *refs, npg):
    k_refs, v_refs = refs[:npg], refs[npg:2 * npg]
    o_ref, acc_ref, carry_ref = refs[2 * npg:]
    j = pl.program_id(1)

    @pl.when(j == 0)
    def _():
        acc_ref[...] = jnp.zeros_like(acc_ref)
        carry_ref[...] = jnp.zeros_like(carry_ref)

    q = q_ref[0]
    bias = bias_ref[...]
    ones = jnp.ones((DH, DH), bf16)
    rr = _iota((PAGE, 2 * PAGE), 0)
    cc = _iota((PAGE, 2 * PAGE), 1)
    tri = jnp.where((rr > cc) | (cc >= PAGE), 1.0, 0.0).astype(bf16)
    acc = acc_ref[...]
    run = carry_ref[...]
    pages = range(npg)
    zbs = [_dot((k_refs[p][0] * q[None]).reshape(PAGE * HEADS, DH).astype(bf16), ones)
           .reshape(PAGE, HEADS, DH) for p in pages]
    zs = [jnp.sum(zb * eye_ref[...], axis=0) + bias for zb in zbs]
    sps = [_softplus(z) for z in zs]
    css = []
    for sp in sps:
        hi, mid, lo = _split3(sp)
        css.append(_dot(hi, tri) + _dot(mid, tri) + _dot(lo, tri))
    ws = []
    for z, sp, cs in zip(zs, sps, css):
        ws.append(jnp.exp(z - sp - cs[:, :PAGE] - run))
        run = run + cs[:, PAGE:]
    wbs = [_dot((eye_ref[...] * w[None]).reshape(PAGE * HEADS, DH).astype(bf16), ones)
           .reshape(PAGE, HEADS, DH) for w in ws]
    for p, wb in zip(pages, wbs):
        acc = acc + jnp.sum(wb * v_refs[p][0], axis=0)
    acc_ref[...] = acc
    carry_ref[...] = run

    @pl.when(j == pl.num_programs(1) - 1)
    def _():
        o_ref[0] = acc


def sb_step(q_s, bias, cache_k, cache_v, page_table, npg=8):
    n, n_pages = page_table.shape
    q4 = (q_s * (DH ** -0.5)).reshape(n, HEADS, DH)
    bias4 = jnp.broadcast_to(bias[:, None], (HEADS, DH))
    eye3 = jnp.broadcast_to(jnp.eye(PAGE, DH, dtype=f32)[:, None, :], (PAGE, HEADS, DH))

    def page(p):
        return lambda b, j, pt: (pt[b, n_pages - 1 - (j * npg + p)], 0, 0, 0)

    kv_specs = [pl.BlockSpec((1, PAGE, HEADS, DH), page(p)) for p in range(npg)]
    grid_spec = pltpu.PrefetchScalarGridSpec(
        num_scalar_prefetch=1,
        grid=(n, n_pages // npg),
        in_specs=[
            pl.BlockSpec((1, HEADS, DH), lambda b, j, pt: (b, 0, 0)),
            pl.BlockSpec((HEADS, DH), lambda b, j, pt: (0, 0)),
            pl.BlockSpec((PAGE, HEADS, DH), lambda b, j, pt: (0, 0, 0)),
        ] + kv_specs + kv_specs,
        out_specs=pl.BlockSpec((1, HEADS, DH), lambda b, j, pt: (b, 0, 0)),
        scratch_shapes=[pltpu.VMEM((HEADS, DH), f32), pltpu.VMEM((HEADS, DH), f32)],
    )
    return pl.pallas_call(
        functools.partial(_sb_step_kernel, npg=npg),
        grid_spec=grid_spec,
        out_shape=jax.ShapeDtypeStruct((n, HEADS, DH), f32),
        compiler_params=_cparams("arbitrary", "arbitrary"),
        name="sb_step",
    )(page_table, q4, bias4, eye3, *([cache_k] * npg), *([cache_v] * npg)).reshape(n, ATT_W)


def _mlstm_kernel(q_ref, k_ref, v_ref, og_ref, gate_ref, gb_ref, gn_ref,
                  h_ref, c_out, n_out, m_out, c_s, n_s, m_s):
    ci = pl.program_id(1)
    L = CHUNK

    @pl.when(ci == 0)
    def _():
        c_s[...] = jnp.zeros_like(c_s)
        n_s[...] = jnp.zeros_like(n_s)
        m_s[...] = jnp.zeros_like(m_s)

    gt = gate_ref[...] + gb_ref[...]
    gt_t = gt.T
    ri = _iota((L, L), 0)
    li = _iota((L, L), 1)
    causal = li <= ri
    tri_incl = jnp.where(causal, 1.0, 0.0).astype(bf16)
    tri_incl_t = jnp.where(ri <= li, 1.0, 0.0).astype(bf16)
    for h in range(M_HEADS):
        sl = slice(h * M_DH, (h + 1) * M_DH)
        ig_col = gt[:, h:h + 1]
        ig_row = gt_t[h:h + 1, :]
        lf_col = _log_sigmoid_pair(gt[:, M_HEADS + h:M_HEADS + h + 1])[0]
        lf_row = _log_sigmoid_pair(gt_t[M_HEADS + h:M_HEADS + h + 1, :])[0]
        c_hi, c_mid, c_lo = _split3(jnp.broadcast_to(lf_col, (L, L)))
        bcum_col = _dot(tri_incl, c_hi) + _dot(tri_incl, c_mid) + _dot(tri_incl, c_lo)
        r_hi, r_mid, r_lo = _split3(jnp.broadcast_to(lf_row, (L, L)))
        bcum_row = _dot(r_hi, tri_incl_t) + _dot(r_mid, tri_incl_t) + _dot(r_lo, tri_incl_t)
        m_prev = m_s[h:h + 1, :]
        dlog = jnp.where(causal, bcum_col - bcum_row + ig_row, NEG_INF)
        inter = bcum_col + m_prev
        m_t = jnp.maximum(inter, jnp.max(dlog, axis=1, keepdims=True))
        dw = jnp.exp(dlog - m_t)
        iw = jnp.exp(inter - m_t)
        qh = q_ref[:, sl]
        kh = k_ref[:, sl] * (M_DH ** -0.5)
        vh = v_ref[:, sl]
        qb, kb, vb = qh.astype(bf16), kh.astype(bf16), vh.astype(bf16)
        sw = dw * _dot_nt(qb, kb)
        c_prev = c_s[h]
        n_prev = n_s[h:h + 1, :]
        iw_col = iw[:, 0:1]
        num = _dot(sw.astype(bf16), vb) + iw_col * _dot_nt(qb, c_prev.astype(bf16))
        qn = jnp.sum(qb.astype(f32) * n_prev.astype(bf16).astype(f32), axis=1, keepdims=True)
        den = jnp.sum(sw, axis=1, keepdims=True) + iw_col * qn
        m_col = m_t[:, 0:1]
        hout = num / jnp.maximum(jnp.abs(den), jnp.exp(-m_col))
        m_last = m_t[L - 1:L, :]
        b_last = bcum_col[L - 1:L, :]
        wl_col = jnp.exp(b_last[:, 0:1] - bcum_col[:, 0:1] + ig_col - m_last[:, 0:1])
        wl_row = jnp.exp(b_last - bcum_row[0:1, :] + ig_row - m_last)
        decay = jnp.exp(b_last + m_prev - m_last)
        dsc = decay[:, 0:1]
        c_s[h] = dsc * c_prev + _dot((vh * wl_col).T.astype(bf16), kb)
        wl8 = jnp.broadcast_to(wl_row, (8, L)).astype(bf16)
        n_s[h:h + 1, :] = dsc * n_prev + _dot(wl8, kb)[0:1, :]
        m_s[h:h + 1, :] = m_last
        hn = hout * lax.rsqrt(jnp.mean(hout * hout, axis=1, keepdims=True) + EPS)
        h_ref[:, sl] = hn * gn_ref[:, sl] * _sigmoid(og_ref[:, sl])

    @pl.when(ci == pl.num_programs(1) - 1)
    def _():
        c_out[0] = c_s[...]
        n_out[0] = n_s[...]
        m_out[0] = m_s[...]


def mlstm_prompt(proj2, gates, gate_bias, g_mnorm, bsz, seq):
    nc = seq // CHUNK
    col = lambda c: pl.BlockSpec((CHUNK, M_WIDTH), lambda b, i, c=c: (b * nc + i, c))
    hm, c1, n1, m1 = pl.pallas_call(
        _mlstm_kernel,
        grid=(bsz, nc),
        in_specs=[
            col(3), col(4), col(5), col(6),
            pl.BlockSpec((CHUNK, LANES), lambda b, i: (b * nc + i, 0)),
            pl.BlockSpec((1, LANES), lambda b, i: (0, 0)),
            pl.BlockSpec((1, M_WIDTH), lambda b, i: (0, 0)),
        ],
        out_specs=[
            pl.BlockSpec((CHUNK, M_WIDTH), lambda b, i: (b * nc + i, 0)),
            pl.BlockSpec((1, M_HEADS, M_DH, M_DH), lambda b, i: (b, 0, 0, 0)),
            pl.BlockSpec((1, M_HEADS, M_DH), lambda b, i: (b, 0, 0)),
            pl.BlockSpec((1, M_HEADS, LANES), lambda b, i: (b, 0, 0)),
        ],
        out_shape=[
            jax.ShapeDtypeStruct((bsz * seq, M_WIDTH), f32),
            jax.ShapeDtypeStruct((bsz, M_HEADS, M_DH, M_DH), f32),
            jax.ShapeDtypeStruct((bsz, M_HEADS, M_DH), f32),
            jax.ShapeDtypeStruct((bsz, M_HEADS, LANES), f32),
        ],
        scratch_shapes=[
            pltpu.VMEM((M_HEADS, M_DH, M_DH), f32),
            pltpu.VMEM((M_HEADS, M_DH), f32),
            pltpu.VMEM((M_HEADS, LANES), f32),
        ],
        compiler_params=_cparams("arbitrary", "arbitrary"),
        name="mlstm_prompt",
    )(proj2, proj2, proj2, proj2, gates, gate_bias, g_mnorm)
    return hm, c1, n1, m1[:, :, 0]


def _mlstm_step_kernel(q_ref, k_ref, v_ref, og_ref, gn_ref, ig_ref, fg_ref, c_ref, n_ref, m_ref,
                       h_ref, c_out, n_out, m_out):
    q = q_ref[0, 0]
    k = k_ref[0, 0] * (M_DH ** -0.5)
    v = v_ref[0, 0]
    ig = ig_ref[0, 0]
    lf = _log_sigmoid_pair(fg_ref[0, 0])[0]
    m0 = m_ref[0, 0]
    c0 = c_ref[0, 0]
    n0 = n_ref[0, 0]
    inter = lf + m0
    m_t = jnp.maximum(inter, ig)
    dw = jnp.exp(ig - m_t)
    iw = jnp.exp(inter - m_t)
    rnd = lambda a: a.astype(bf16).astype(f32)
    qr, kr, vr = rnd(q), rnd(k), rnd(v)
    qk = jnp.sum(qr * kr, axis=1, keepdims=True)
    sw = dw * qk
    cq = jnp.sum(rnd(c0) * qr, axis=1, keepdims=True)
    num = rnd(sw) * vr + iw * cq
    den = sw + iw * jnp.sum(rnd(n0) * qr, axis=1, keepdims=True)
    hout = num / jnp.maximum(jnp.abs(den), jnp.exp(-m_t))
    c_out[0, 0] = iw * c0 + rnd(dw * v) * kr
    n_out[0, 0] = iw * n0 + rnd(dw) * kr
    m_out[0, 0] = m_t
    hn = hout * lax.rsqrt(jnp.mean(hout * hout, axis=0, keepdims=True) + EPS)
    h_ref[0, 0] = hn * gn_ref[0] * _sigmoid(og_ref[0, 0])


def mlstm_step(proj2_s, gates_s, gate_bias, g_mnorm, c0, n0, m0):
    n = proj2_s.shape[0]
    base = 3 * ATT_W
    seg = lambda i: proj2_s[:, base + i * M_WIDTH: base + (i + 1) * M_WIDTH].reshape(n, M_HEADS, M_DH)
    g = gates_s + gate_bias
    rowb = pl.BlockSpec((1, 1, 1, M_DH), lambda b, h: (b, h, 0, 0))
    colb = pl.BlockSpec((1, 1, M_DH, 1), lambda b, h: (b, h, 0, 0))
    scal = pl.BlockSpec((1, 1, 1, 1), lambda b, h: (b, h, 0, 0))
    hcol, c1, n1, m1 = pl.pallas_call(
        _mlstm_step_kernel,
        grid=(n, M_HEADS),
        in_specs=[
            rowb, rowb, colb, colb,
            pl.BlockSpec((1, M_DH, 1), lambda b, h: (h, 0, 0)),
            scal, scal,
            pl.BlockSpec((1, 1, M_DH, M_DH), lambda b, h: (b, h, 0, 0)),
            rowb, scal,
        ],
        out_specs=[colb, pl.BlockSpec((1, 1, M_DH, M_DH), lambda b, h: (b, h, 0, 0)), rowb, scal],
        out_shape=[
            jax.ShapeDtypeStruct((n, M_HEADS, M_DH, 1), f32),
            jax.ShapeDtypeStruct((n, M_HEADS, M_DH, M_DH), f32),
            jax.ShapeDtypeStruct((n, M_HEADS, 1, M_DH), f32),
            jax.ShapeDtypeStruct((n, M_HEADS, 1, 1), f32),
        ],
        compiler_params=_cparams("arbitrary", "arbitrary"),
        name="mlstm_step",
    )(seg(0)[:, :, None, :], seg(1)[:, :, None, :], seg(2)[..., None], seg(3)[..., None],
      g_mnorm.reshape(M_HEADS, M_DH, 1),
      g[:, 0:M_HEADS].reshape(n, M_HEADS, 1, 1), g[:, M_HEADS:2 * M_HEADS].reshape(n, M_HEADS, 1, 1),
      c0, n0[:, :, None, :], m0.reshape(n, M_HEADS, 1, 1))
    return hcol.reshape(n, M_WIDTH), c1, n1.reshape(n, M_HEADS, M_DH), m1.reshape(n, M_HEADS)


def _peerq_kernel(x_ref, g_ref, sh_ref, sc_ref, wt_ref, qt_ref, xmt_ref, xs_ref):
    @pl.when(pl.program_id(1) == 0)
    def _():
        xm = _modulated(x_ref[...], g_ref[...], sh_ref[0], sc_ref[0])
        xt = xm.T.astype(bf16)
        xs_ref[...] = xt
        xmt_ref[...] = xt

    qt_ref[...] = _dot(wt_ref[...], xs_ref[...])


def peer_query(x, g, shift, scale, wq_t, rows_per_batch, tm, tn=512):
    m, k = x.shape
    n = wq_t.shape[0]
    sh, sh_spec = _mod_specs(shift, m, tm, k, rows_per_batch)
    sc, sc_spec = _mod_specs(scale, m, tm, k, rows_per_batch)
    return pl.pallas_call(
        _peerq_kernel,
        grid=(m // tm, n // tn),
        in_specs=[
            pl.BlockSpec((tm, k), lambda i, j: (i, 0)),
            pl.BlockSpec((1, k), lambda i, j: (0, 0)),
            sh_spec, sc_spec,
            pl.BlockSpec((tn, k), lambda i, j: (j, 0)),
        ],
        out_specs=[pl.BlockSpec((tn, tm), lambda i, j: (j, i)),
                   pl.BlockSpec((k, tm), lambda i, j: (0, i))],
        out_shape=[jax.ShapeDtypeStruct((n, m), f32), jax.ShapeDtypeStruct((k, m), bf16)],
        scratch_shapes=[pltpu.VMEM((k, tm), bf16)],
        compiler_params=_cparams("arbitrary", "arbitrary"),
        name="peer_query",
    )(x, g.reshape(1, k), sh, sc, wq_t)


def _heads_layout_kernel(*refs):
    n = len(refs) // 2
    for x_ref, o_ref in zip(refs[:n], refs[n:]):
        for hb in range(HEADS):
            o_ref[:, hb, :] = x_ref[:, hb * DH:(hb + 1) * DH]


def heads_layout(srcs, bsz, seq, tail, tt=512):
    nt, first = tail // tt, (seq - tail) // tt
    rows = lambda b, i: b * (seq // tt) + first + i
    return pl.pallas_call(
        _heads_layout_kernel,
        grid=(bsz, nt),
        in_specs=[pl.BlockSpec((tt, ATT_W), lambda b, i, c=c: (rows(b, i), c)) for _, c in srcs],
        out_specs=[pl.BlockSpec((tt, HEADS, DH), lambda b, i: (b * nt + i, 0, 0))] * len(srcs),
        out_shape=[jax.ShapeDtypeStruct((bsz * tail, HEADS, DH), f32)] * len(srcs),
        compiler_params=_cparams("arbitrary", "arbitrary"),
        name="heads_layout",
    )(*[x for x, _ in srcs])


def _cast_kernel(w_ref, o_ref):
    o_ref[...] = w_ref[0].astype(bf16)


def cast_layer_bf16(w, layer, tr=1024):
    _, rows, cols = w.shape
    return pl.pallas_call(
        _cast_kernel,
        grid=(rows // tr,),
        in_specs=[pl.BlockSpec((1, tr, cols), lambda i: (layer, i, 0))],
        out_specs=pl.BlockSpec((tr, cols), lambda i: (i, 0)),
        out_shape=jax.ShapeDtypeStruct((rows, cols), bf16),
        compiler_params=_cparams("arbitrary"),
        name="cast_bf16",
    )(w)


_N_TOP = PEER_TOPK + 1
_CAND_PAIRS = [(a, b) for a in range(_N_TOP) for b in range(_N_TOP) if (a + 1) * (b + 1) <= _N_TOP]
_N_CAND = -(-len(_CAND_PAIRS) // 8) * 8


def _extract_top(cur, ridx, n):
    vals = []
    big = float(cur.shape[0])
    for _ in range(n):
        mx = jnp.max(cur, axis=0, keepdims=True)
        first = jnp.min(jnp.where(cur == mx, ridx, big), axis=0, keepdims=True)
        cur = jnp.where(ridx == first, NEG_INF, cur)
        vals.append(mx)
    return vals


def _oddeven_mergesort_pairs(n):
    pairs = []

    def merge(lo, cnt, r):
        step = 2 * r
        if step < cnt:
            merge(lo, cnt, step)
            merge(lo + r, cnt, step)
            pairs.extend((i, i + r) for i in range(lo + r, lo + cnt - r, step))
        else:
            pairs.append((lo, lo + r))

    def sort(lo, cnt):
        if cnt > 1:
            sort(lo, cnt // 2)
            sort(lo + cnt // 2, cnt // 2)
            merge(lo, cnt, 1)

    sort(0, n)
    return pairs


_SORT16 = _oddeven_mergesort_pairs(PEER_TOPK)


def _exchange(y, i, j):
    y[i], y[j] = jnp.maximum(y[i], y[j]), jnp.minimum(y[i], y[j])


def _top17_network(s):
    n = PEER_TOPK
    y = [s[v * 8:(v + 1) * 8, :] for v in range(n)]
    for i, j in _SORT16:
        _exchange(y, i, j)
    dropped = None
    for shift in (4, 2, 1):
        other = [pltpu.roll(y[n - 1 - i], shift, 0) for i in range(n)]
        low = functools.reduce(jnp.maximum, [jnp.minimum(y[i], other[i]) for i in range(n)])
        dropped = low if dropped is None else jnp.maximum(dropped, low)
        y = [jnp.maximum(y[i], other[i]) for i in range(n)]
        dist = n // 2
        while dist:
            for i in range(n):
                if not i & dist:
                    _exchange(y, i, i + dist)
            dist //= 2
    return [v[0:1, :] for v in y] + [jnp.max(dropped, axis=0, keepdims=True)]


_ROUTER_UNROLL = 8


def _router_kernel(qt_ref, keys_ref, e1_ref, e2_ref, th_ref, cand_ref):
    cidx = _iota((_N_CAND, LANES), 0).astype(f32)
    k0 = keys_ref[0].astype(bf16)
    k1 = keys_ref[1].astype(bf16)
    cand_ref[...] = jnp.full(cand_ref.shape, NEG_INF, f32)

    def one_head(h, slot):
        r0 = pl.multiple_of(h * 2 * N_KEYS, 2 * N_KEYS)
        s1 = _dot(k0, qt_ref[pl.ds(r0, N_KEYS), :].astype(bf16))
        s2 = _dot(k1, qt_ref[pl.ds(r0 + N_KEYS, N_KEYS), :].astype(bf16))
        top_a = _top17_network(s1)
        top_b = _top17_network(s2)
        for ci, (a, b) in enumerate(_CAND_PAIRS):
            cand_ref[slot, ci:ci + 1, :] = top_a[a] + top_b[b]
        cs = _extract_top(cand_ref[slot], cidx, _N_TOP)
        z = jnp.ones_like(cs[0])
        for r in range(1, PEER_TOPK):
            z = z + jnp.exp(cs[r] - cs[0])
        inv_z = 1.0 / z
        mid = 0.5 * (cs[PEER_TOPK - 1] + cs[PEER_TOPK])
        o0 = pl.multiple_of(h * N_KEYS, N_KEYS)
        e1 = jnp.exp(s1 - top_a[0]) * inv_z
        e1_ref[:, h] = e1.reshape(N_KEYS // 8, 8, LANES)
        e2_ref[pl.ds(o0, N_KEYS), :] = jnp.exp(s2 - top_b[0])
        th_ref[pl.ds(h, 1), :] = jnp.exp(mid - cs[0]) * inv_z

    def body(hp, _):
        for slot in range(_ROUTER_UNROLL):
            one_head(hp * _ROUTER_UNROLL + slot, slot)
        return 0

    lax.fori_loop(0, PEER_HEADS // _ROUTER_UNROLL, body, 0)


def peer_router(qt, keys):
    n, m = qt.shape
    half = PEER_HEADS * N_KEYS
    return pl.pallas_call(
        _router_kernel,
        grid=(m // LANES,),
        in_specs=[pl.BlockSpec((n, LANES), lambda i: (0, i)),
                  pl.BlockSpec((2, N_KEYS, N_KEYS), lambda i: (0, 0, 0))],
        out_specs=[pl.BlockSpec((N_KEYS // 8, PEER_HEADS, 8, LANES), lambda i: (0, 0, 0, i)),
                   pl.BlockSpec((half, LANES), lambda i: (0, i)),
                   pl.BlockSpec((PEER_HEADS, LANES), lambda i: (0, i))],
        out_shape=[jax.ShapeDtypeStruct((N_KEYS // 8, PEER_HEADS, 8, m), f32),
                   jax.ShapeDtypeStruct((half, m), f32),
                   jax.ShapeDtypeStruct((PEER_HEADS, m), f32)],
        scratch_shapes=[pltpu.VMEM((_ROUTER_UNROLL, _N_CAND, LANES), f32)],
        compiler_params=_cparams("arbitrary"),
        name="peer_router",
    )(qt, keys)


_G_CHAINS = 3


def _experts_kernel(xmt_ref, e1_ref, e2_ref, th_ref, u_ref, v_ref, x_ref, gt_ref, gf_ref, zero_ref,
                    o_ref, w_ref, *, final_norm):
    c = pl.program_id(1)
    tm, te = w_ref.shape
    ni = te // N_KEYS

    @pl.when(c == 0)
    def _():
        o_ref[...] = jnp.zeros_like(o_ref)

    act = _gelu(_dot(u_ref[...], xmt_ref[...]))
    deps = [None] * _G_CHAINS
    cnt = 0
    sub = 8
    for ii in range(ni):
        es = slice(ii * N_KEYS, (ii + 1) * N_KEYS)
        for lc in range(tm // LANES):
            ls = slice(lc * LANES, (lc + 1) * LANES)
            rows = [e1_ref[ii // 8, h, ii % 8:ii % 8 + 1, ls] for h in range(PEER_HEADS)]
            ths = [th_ref[h:h + 1, ls] for h in range(PEER_HEADS)]
            parts = []
            for gi in range(N_KEYS // sub):
                dep = deps[cnt % _G_CHAINS]
                g = jnp.zeros((sub, LANES), f32)
                for h in range(PEER_HEADS):
                    row = rows[h] if (dep is None or h) else rows[h] + dep
                    pr = e2_ref[h * N_KEYS + gi * sub:h * N_KEYS + (gi + 1) * sub, ls] * row
                    g = g + jnp.where(pr >= ths[h], pr, 0.0)
                bits = lax.bitcast_convert_type(g[0:1, :], jnp.int32) & zero_ref[0:1, :]
                deps[cnt % _G_CHAINS] = lax.bitcast_convert_type(bits, f32)
                cnt += 1
                parts.append(g)
            g_blk = jnp.concatenate(parts, axis=0)
            w_ref[ls, es] = (g_blk * act[es, ls]).T.astype(bf16)
    o_ref[...] += _dot(w_ref[...], v_ref[...])

    @pl.when(c == pl.num_programs(1) - 1)
    def _():
        y = x_ref[...] + gt_ref[0] * o_ref[...]
        if final_norm:
            ms = jnp.mean(y * y, axis=-1, keepdims=True)
            y = y * lax.rsqrt(ms + EPS) * gf_ref[...]
        o_ref[...] = y


def peer_experts(xmt, e1t, e2t, th, u, v, x, gate, g_final, rows_per_batch, tm, te, final_norm):
    m, d = x.shape
    gt, gt_spec = _mod_specs(gate, m, tm, d, rows_per_batch)
    half = PEER_HEADS * N_KEYS
    return pl.pallas_call(
        functools.partial(_experts_kernel, final_norm=final_norm),
        grid=(m // tm, N_EXPERTS // te),
        in_specs=[
            pl.BlockSpec((d, tm), lambda i, c: (0, i)),
            pl.BlockSpec((te // N_KEYS // 8, PEER_HEADS, 8, tm), lambda i, c: (c, 0, 0, i)),
            pl.BlockSpec((half, tm), lambda i, c: (0, i)),
            pl.BlockSpec((PEER_HEADS, tm), lambda i, c: (0, i)),
            pl.BlockSpec((te, d), lambda i, c: (c, 0)),
            pl.BlockSpec((te, d), lambda i, c: (c, 0)),
            pl.BlockSpec((tm, d), lambda i, c: (i, 0)),
            gt_spec,
            pl.BlockSpec((1, d), lambda i, c: (0, 0)),
            pl.BlockSpec((8, LANES), lambda i, c: (0, 0)),
        ],
        out_specs=pl.BlockSpec((tm, d), lambda i, c: (i, 0)),
        out_shape=jax.ShapeDtypeStruct((m, d), f32),
        scratch_shapes=[pltpu.VMEM((tm, te), bf16)],
        compiler_params=_cparams("arbitrary", "arbitrary"),
        name="peer_experts",
    )(xmt, e1t, e2t, th, u, v, x, gt, g_final.reshape(1, d), jnp.zeros((8, LANES), jnp.int32))


def peer_block(x, g, shift, scale, gate, wq_t, keys, u, v, g_final, rows_per_batch, tm, te, final_norm):
    tm_q = 1024 if x.shape[0] % 1024 == 0 and rows_per_batch % 1024 == 0 else tm
    qt, xmt = peer_query(x, g, shift, scale, wq_t, rows_per_batch, tm_q)
    e1t, e2t, th = peer_router(qt, keys)
    return peer_experts(xmt, e1t, e2t, th, u, v, x, gate, g_final, rows_per_batch, tm, te, final_norm)


def kernel(x_prompt, x_sample, c_prompt, c_sample, state_rglru_conv, state_rglru_h, cache_swa_k, cache_swa_v, cache_sb_k, cache_sb_v, state_mlstm_C, state_mlstm_n, state_mlstm_m, page_table, w_ada, b_ada, g_norm_mix, g_norm_ffn, e_w_in, e_conv_w, e_conv_b, e_w_r, e_b_r, e_w_i, e_b_i, e_lambda, e_w_out, o_w_in, o_b_if, o_sb_bias, o_g_mnorm, o_w_out, peer_w_q, peer_keys, peer_u, peer_v, g_final):
    bp, seq, d = x_prompt.shape
    bs = x_sample.shape[0]
    mp = bp * seq
    pad_s = LANES
    xp = x_prompt.reshape(mp, d)
    xs = x_sample.reshape(bs, d)

    c_rows = 16
    c_all = jnp.concatenate([c_prompt, c_sample, jnp.zeros((c_rows - bp - bs, d), f32)], axis=0)
    mod = adaln_all(c_all, w_ada, b_ada)

    def mods(layer):
        parts = [mod[layer, :, i * d:(i + 1) * d] for i in range(6)]
        return [p[:bp] for p in parts], [p[bp:bp + bs] for p in parts]

    ctab_p, stab_p = _rope_tables(jnp.arange(seq, dtype=jnp.int32))
    ctab_s, stab_s = _rope_tables(jnp.full((1,), PAST_LEN, jnp.int32))

    TM = 512
    m_p, m_s = mods(0)
    w_in = e_w_in[0].astype(bf16)
    w_out = e_w_out[0].astype(bf16)
    cw, cb = e_conv_w[0], e_conv_b[0].reshape(1, -1)
    wr, wi = e_w_r[0].astype(bf16), e_w_i[0].astype(bf16)
    br, bi, lam = e_b_r[0].reshape(1, -1), e_b_i[0].reshape(1, -1), e_lambda[0].reshape(1, -1)

    proj_p = mod_matmul(xp, g_norm_mix[0], m_p[0], m_p[1], w_in, E_IN, seq, 1024, 1024, name="e_in_p")
    proj_s = mod_matmul(xs, g_norm_mix[0], m_s[0], m_s[1], w_in, E_IN, 1, bs, 512, name="e_in_s")

    ya_p, h_p = rglru_prompt(proj_p, bp, seq, cw, cb, wr, br, wi, bi, lam)
    ya_s, h_s = rglru_step(proj_s, state_rglru_conv[0], state_rglru_h[0], cw, cb, wr, br, wi, bi, lam)
    conv_p = proj_p.reshape(bp, seq, E_IN)[:, seq - (CONV_W - 1):, :RG_WIDTH]
    conv_s = jnp.concatenate([state_rglru_conv[0][:, 1:], proj_s[:, None, :RG_WIDTH]], axis=1)

    qkv_p, k_p = rope_split(proj_p, ctab_p, stab_p, bp, seq)
    q_s, k_s, v_s = rope_qkv(proj_s, ctab_s, stab_s, 1, bs)
    o_p = dilated_prompt(qkv_p, bp, seq)
    win = cache_swa_k.shape[2]
    o_s = dilated_step(q_s, k_s, v_s, cache_swa_k.reshape(-1, win, HEADS, DH)[:bs],
                       cache_swa_v.reshape(-1, win, HEADS, DH)[:bs])
    wl = min(2048, seq)
    swa_k_p, swa_v_p = [a.reshape(bp, wl, HEADS, DH)
                        for a in heads_layout([(k_p, 0), (proj_p, E_IN // ATT_W - 1)], bp, seq, wl)]
    swa_k_s = k_s.reshape(bs, 1, HEADS, DH)
    swa_v_s = v_s.reshape(bs, 1, HEADS, DH)

    xp = out_proj(ya_p, o_p, w_out, xp, m_p[2], seq, 1024, 1024, name="e_out_p")
    xs = out_proj(ya_s, o_s, w_out, xs, m_s[2], 1, bs, 512, name="e_out_s")

    def peer_layer(layer, xp, xs, m_p, m_s, final_norm):
        wq_t = peer_w_q[layer].T.astype(bf16)
        u = cast_layer_bf16(peer_u, layer)
        v = cast_layer_bf16(peer_v, layer)
        xp = peer_block(xp, g_norm_ffn[layer], m_p[3], m_p[4], m_p[5], wq_t, peer_keys[layer], u, v,
                        g_final, seq, TM, 1024, final_norm)
        xs_pad = jnp.pad(xs, ((0, pad_s - bs), (0, 0)))
        xs_new = peer_block(xs_pad, g_norm_ffn[layer], m_s[3], m_s[4], m_s[5], wq_t, peer_keys[layer],
                            u, v, g_final, 1, pad_s, 1024, final_norm)
        return xp, xs_new[:bs]

    xp, xs = peer_layer(0, xp, xs, m_p, m_s, False)

    m_p, m_s = mods(1)
    w_in2 = o_w_in[0].astype(bf16)
    w_gate = jnp.pad(w_in2[:, O_MAIN:], ((0, 0), (0, LANES - 2 * M_HEADS)))
    w_out2 = o_w_out[0].astype(bf16)
    gate_bias = jnp.pad(o_b_if[0].reshape(1, 2 * M_HEADS), ((0, 0), (0, LANES - 2 * M_HEADS)))
    gmn = o_g_mnorm[0].reshape(1, M_WIDTH)

    proj2_p, gates_p = mod_matmul(xp, g_norm_mix[1], m_p[0], m_p[1], w_in2, O_MAIN, seq, 1024, 1024,
                                  w_gate=w_gate, name="o_in_p")
    proj2_s, gates_s = mod_matmul(xs, g_norm_mix[1], m_s[0], m_s[1], w_in2, O_MAIN, 1, bs, 512,
                                  w_gate=w_gate, name="o_in_s")

    oc_p = sb_prompt(proj2_p, o_sb_bias[0], bp, seq)
    oc_s = sb_step(proj2_s[:, :ATT_W], o_sb_bias[0], cache_sb_k.reshape(-1, PAGE, HEADS, DH),
                   cache_sb_v.reshape(-1, PAGE, HEADS, DH), page_table)
    hm_p, mC_p, mn_p, mm_p = mlstm_prompt(proj2_p, gates_p, gate_bias, gmn, bp, seq)
    hm_s, mC_s, mn_s, mm_s = mlstm_step(proj2_s, gates_s, gate_bias, gmn,
                                        state_mlstm_C.reshape(-1, M_HEADS, M_DH, M_DH)[:bs],
                                        state_mlstm_n[0], state_mlstm_m[0])

    n_pg = seq // PAGE
    sb_k_p, sb_v_p = [a.reshape(bp, n_pg, PAGE, HEADS, DH)
                      for a in heads_layout([(proj2_p, 1), (proj2_p, 2)], bp, seq, seq)]
    sb_k_s = proj2_s[:, ATT_W:2 * ATT_W].reshape(bs, 1, HEADS, DH)
    sb_v_s = proj2_s[:, 2 * ATT_W:3 * ATT_W].reshape(bs, 1, HEADS, DH)

    xp = out_proj(oc_p, hm_p, w_out2, xp, m_p[2], seq, 1024, 1024, name="o_out_p")
    xs = out_proj(oc_s, hm_s, w_out2, xs, m_s[2], 1, bs, 512, name="o_out_s")
    xp, xs = peer_layer(1, xp, xs, m_p, m_s, True)

    y_prompt = xp.reshape(bp, seq, d)
    y_sample = xs.reshape(bs, 1, d)
    st = lambda a: a[None]
    return (y_prompt, y_sample, st(conv_p), st(conv_s), st(h_p.reshape(bp, RG_WIDTH)), st(h_s),
            st(swa_k_p), st(swa_k_s), st(swa_v_p), st(swa_v_s),
            st(sb_k_p), st(sb_k_s), st(sb_v_p), st(sb_v_s),
            st(mC_p), st(mC_s), st(mn_p), st(mn_s), st(mm_p), st(mm_s))
```

```python
import functools
import math

import jax
import jax.numpy as jnp
from jax import lax
from jax.experimental import pallas as pl
from jax.experimental.pallas import tpu as pltpu

f32 = jnp.float32
bf16 = jnp.bfloat16

D_MODEL = 2048
PAST_LEN = 16384
PAGE = 128
RG_WIDTH = 1024
RG_BLOCKS = 8
CONV_W = 4
RG_C = 8.0
HEADS = 8
DH = 128
ATT_W = HEADS * DH
DIL_PATTERNS = ((128, 1), (512, 4), (2048, 16))
ROT_DIMS = 32
ROPE_THETA = 500000.0
M_HEADS = 4
M_DH = 256
M_WIDTH = M_HEADS * M_DH
CHUNK = 128
E_IN = 2 * RG_WIDTH + 3 * ATT_W
O_MAIN = 3 * ATT_W + 4 * M_WIDTH
PEER_HEADS = 8
N_KEYS = 128
N_EXPERTS = N_KEYS * N_KEYS
PEER_TOPK = 16
EPS = 1e-6
LANES = 128
VMEM_LIMIT = 56 * 1024 * 1024
NEG_INF = float("-inf")


def _cparams(*sem):
    return pltpu.CompilerParams(dimension_semantics=sem, vmem_limit_bytes=VMEM_LIMIT)


def _dot(a, b):
    return jnp.dot(a, b, preferred_element_type=f32)


def _dot_nt(a, b):
    return lax.dot_general(a, b, (((1,), (1,)), ((), ())), preferred_element_type=f32)


def _split3(x):
    hi = x.astype(bf16)
    r = x - hi.astype(f32)
    mid = r.astype(bf16)
    lo = (r - mid.astype(f32)).astype(bf16)
    return hi, mid, lo


def _sigmoid(x):
    return 1.0 / (1.0 + jnp.exp(-x))


def _log_sigmoid_pair(z):
    l1p = jnp.log1p(jnp.exp(-jnp.abs(z)))
    return jnp.minimum(z, 0.0) - l1p, -jnp.maximum(z, 0.0) - l1p


def _softplus(z):
    return jnp.maximum(z, 0.0) + jnp.log(1.0 + jnp.exp(-jnp.abs(z)))


def _gelu(x):
    c = math.sqrt(2.0 / math.pi)
    h = 0.5 * x
    return h + h * jnp.tanh(x * (c + (c * 0.044715) * (x * x)))


def _iota(shape, dim):
    return lax.broadcasted_iota(jnp.int32, shape, dim)


def _rowsum_bcast(x):
    ones = jnp.ones((LANES, LANES), bf16)
    hi = x.astype(bf16)
    lo = (x - hi.astype(f32)).astype(bf16)
    return _dot(hi, ones) + _dot(lo, ones)


def _adaln_kernel(c_ref, w_ref, b_ref, o_ref):
    c = c_ref[...]
    s = c * _sigmoid(c)
    w = w_ref[0]
    s_hi = s.astype(bf16)
    s_lo = (s - s_hi.astype(f32)).astype(bf16)
    w_hi = w.astype(bf16)
    w_lo = (w - w_hi.astype(f32)).astype(bf16)
    o_ref[0] = _dot(s_hi, w_hi) + _dot(s_hi, w_lo) + _dot(s_lo, w_hi) + b_ref[0]


def adaln_all(c_all, w_ada, b_ada):
    depth, d, n = w_ada.shape
    rows = c_all.shape[0]
    tn = 1024
    return pl.pallas_call(
        _adaln_kernel,
        grid=(depth, n // tn),
        in_specs=[
            pl.BlockSpec((rows, d), lambda l, j: (0, 0)),
            pl.BlockSpec((1, d, tn), lambda l, j: (l, 0, j)),
            pl.BlockSpec((1, 1, tn), lambda l, j: (l, 0, j)),
        ],
        out_specs=pl.BlockSpec((1, rows, tn), lambda l, j: (l, 0, j)),
        out_shape=jax.ShapeDtypeStruct((depth, rows, n), f32),
        compiler_params=_cparams("arbitrary", "arbitrary"),
        name="adaln",
    )(c_all, w_ada, b_ada.reshape(depth, 1, n))


def _modulated(x, g, sh, sc):
    ms = jnp.mean(x * x, axis=-1, keepdims=True)
    y = x * lax.rsqrt(ms + EPS) * g
    return y * (1.0 + sc) + sh


def _modmm_kernel(x_ref, g_ref, sh_ref, sc_ref, w_ref, o_ref, xn_ref):
    @pl.when(pl.program_id(1) == 0)
    def _():
        xn_ref[...] = _modulated(x_ref[...], g_ref[...], sh_ref[0], sc_ref[0]).astype(bf16)

    o_ref[...] = _dot(xn_ref[...], w_ref[...])


def _modmm_gate_kernel(x_ref, g_ref, sh_ref, sc_ref, w_ref, wg_ref, o_ref, og_ref, xn_ref):
    @pl.when(pl.program_id(1) == 0)
    def _():
        xn = _modulated(x_ref[...], g_ref[...], sh_ref[0], sc_ref[0]).astype(bf16)
        xn_ref[...] = xn
        og_ref[...] = _dot(xn, wg_ref[...])

    o_ref[...] = _dot(xn_ref[...], w_ref[...])


def _mod_specs(mod, m, tm, k, rows_per_batch):
    if rows_per_batch >= tm:
        assert rows_per_batch % tm == 0
        per = rows_per_batch // tm
        return mod.reshape(-1, 1, k), pl.BlockSpec((1, 1, k), lambda i, j: (i // per, 0, 0))
    assert rows_per_batch == 1
    rows = mod
    if rows.shape[0] < m:
        rows = jnp.pad(rows, ((0, m - rows.shape[0]), (0, 0)))
    return rows.reshape(1, m, k), pl.BlockSpec((1, tm, k), lambda i, j: (0, i, 0))


def mod_matmul(x, g, shift, scale, w, n_out, rows_per_batch, tm, tn, w_gate=None, name="modmm"):
    m, k = x.shape
    sh, sh_spec = _mod_specs(shift, m, tm, k, rows_per_batch)
    sc, sc_spec = _mod_specs(scale, m, tm, k, rows_per_batch)
    in_specs = [
        pl.BlockSpec((tm, k), lambda i, j: (i, 0)),
        pl.BlockSpec((1, k), lambda i, j: (0, 0)),
        sh_spec,
        sc_spec,
        pl.BlockSpec((k, tn), lambda i, j: (0, j)),
    ]
    args = [x, g.reshape(1, k), sh, sc, w]
    out_specs = pl.BlockSpec((tm, tn), lambda i, j: (i, j))
    out_shape = jax.ShapeDtypeStruct((m, n_out), f32)
    kern = _modmm_kernel
    if w_gate is not None:
        in_specs.append(pl.BlockSpec((k, LANES), lambda i, j: (0, 0)))
        args.append(w_gate)
        out_specs = [out_specs, pl.BlockSpec((tm, LANES), lambda i, j: (i, 0))]
        out_shape = [out_shape, jax.ShapeDtypeStruct((m, LANES), f32)]
        kern = _modmm_gate_kernel
    return pl.pallas_call(
        kern,
        grid=(m // tm, n_out // tn),
        in_specs=in_specs,
        out_specs=out_specs,
        out_shape=out_shape,
        scratch_shapes=[pltpu.VMEM((tm, k), bf16)],
        compiler_params=_cparams("arbitrary", "arbitrary"),
        name=name,
    )(*args)


def _outproj_kernel(a1_ref, a2_ref, w1_ref, w2_ref, x_ref, gt_ref, o_ref):
    y = _dot(a1_ref[...].astype(bf16), w1_ref[...]) + _dot(a2_ref[...].astype(bf16), w2_ref[...])
    o_ref[...] = x_ref[...] + gt_ref[0] * y


def out_proj(a1, a2, w, x, gate, rows_per_batch, tm, tn, name="outproj"):
    m, k1 = a1.shape
    k2 = a2.shape[1]
    n = w.shape[1]
    gt, gt_spec = _mod_specs(gate, m, tm, n, rows_per_batch)
    if gt.shape[1] == 1:
        per = rows_per_batch // tm
        gt_spec = pl.BlockSpec((1, 1, tn), lambda i, j: (i // per, 0, j))
    else:
        gt_spec = pl.BlockSpec((1, tm, tn), lambda i, j: (0, i, j))
    return pl.pallas_call(
        _outproj_kernel,
        grid=(m // tm, n // tn),
        in_specs=[
            pl.BlockSpec((tm, k1), lambda i, j: (i, 0)),
            pl.BlockSpec((tm, k2), lambda i, j: (i, 0)),
            pl.BlockSpec((k1, tn), lambda i, j: (0, j)),
            pl.BlockSpec((k2, tn), lambda i, j: (k1 // k2, j)),
            pl.BlockSpec((tm, tn), lambda i, j: (i, j)),
            gt_spec,
        ],
        out_specs=pl.BlockSpec((tm, tn), lambda i, j: (i, j)),
        out_shape=jax.ShapeDtypeStruct((m, n), f32),
        compiler_params=_cparams("arbitrary", "arbitrary"),
        name=name,
    )(a1, a2, w, w, x, gt)


def _rglru_gates(xc, wr_ref, br, wi_ref, bi, lam):
    xb = xc.astype(bf16)
    rs, gs = [], []
    for hb in range(RG_BLOCKS):
        sl = slice(hb * LANES, (hb + 1) * LANES)
        rs.append(_dot(xb[:, sl], wr_ref[hb]))
        gs.append(_dot(xb[:, sl], wi_ref[hb]))
    r = _sigmoid(jnp.concatenate(rs, axis=1) + br)
    ig = _sigmoid(jnp.concatenate(gs, axis=1) + bi)
    softplus_neg_lam = jnp.maximum(-lam, 0.0) + jnp.log1p(jnp.exp(-jnp.abs(lam)))
    log_a = -RG_C * r * softplus_neg_lam
    a = jnp.exp(log_a)
    u = jnp.sqrt(-jnp.tanh(log_a) * (a * a + 1.0)) * ig * xc
    return a, u


def _rglru_kernel(xa_ref, ga_ref, cw_ref, cb_ref, wr_ref, br_ref, wi_ref, bi_ref, lam_ref,
                  ya_ref, hl_ref, xprev_ref, hc_ref):
    t_idx = pl.program_id(1)
    tt = xa_ref.shape[0]

    @pl.when(t_idx == 0)
    def _():
        xprev_ref[...] = jnp.zeros_like(xprev_ref)
        hc_ref[...] = jnp.zeros_like(hc_ref)

    xa = xa_ref[...]
    xprev = xprev_ref[...]
    row8 = _iota((8, RG_WIDTH), 0)
    xc = cb_ref[...] + cw_ref[CONV_W - 1:CONV_W, :] * xa
    for k in range(1, CONV_W):
        rolled = pltpu.roll(xa, k, 0)
        head = jnp.where(row8 < k, pltpu.roll(xprev, k, 0), rolled[0:8])
        shifted = jnp.concatenate([head, rolled[8:]], axis=0)
        xc = xc + cw_ref[CONV_W - 1 - k:CONV_W - k, :] * shifted
    xprev_ref[...] = xa[tt - 8:tt]

    a, u = _rglru_gates(xc, wr_ref, br_ref[...], wi_ref, bi_ref[...], lam_ref[...])
    row = _iota((tt, RG_WIDTH), 0)
    s = 1
    while s < tt:
        a_sh = pltpu.roll(a, s, 0)
        u_sh = pltpu.roll(u, s, 0)
        ok = row >= s
        u = jnp.where(ok, a * u_sh + u, u)
        a = jnp.where(ok, a * a_sh, a)
        s *= 2
    h = a * hc_ref[...] + u
    hc_ref[...] = h[tt - 1:tt]
    hl_ref[0] = h[tt - 1:tt]
    ya_ref[...] = h * _gelu(ga_ref[...])


def rglru_prompt(proj, bsz, seq, cw, cb, wr, br, wi, bi, lam, tt=256):
    nt = seq // tt
    vec = lambda: pl.BlockSpec((1, RG_WIDTH), lambda b, t: (0, 0))
    return pl.pallas_call(
        _rglru_kernel,
        grid=(bsz, nt),
        in_specs=[
            pl.BlockSpec((tt, RG_WIDTH), lambda b, t: (b * nt + t, 0)),
            pl.BlockSpec((tt, RG_WIDTH), lambda b, t: (b * nt + t, 1)),
            pl.BlockSpec((CONV_W, RG_WIDTH), lambda b, t: (0, 0)),
            vec(),
            pl.BlockSpec((RG_BLOCKS, LANES, LANES), lambda b, t: (0, 0, 0)),
            vec(),
            pl.BlockSpec((RG_BLOCKS, LANES, LANES), lambda b, t: (0, 0, 0)),
            vec(),
            vec(),
        ],
        out_specs=[
            pl.BlockSpec((tt, RG_WIDTH), lambda b, t: (b * nt + t, 0)),
            pl.BlockSpec((1, 1, RG_WIDTH), lambda b, t: (b, 0, 0)),
        ],
        out_shape=[
            jax.ShapeDtypeStruct((bsz * seq, RG_WIDTH), f32),
            jax.ShapeDtypeStruct((bsz, 1, RG_WIDTH), f32),
        ],
        scratch_shapes=[pltpu.VMEM((8, RG_WIDTH), f32), pltpu.VMEM((1, RG_WIDTH), f32)],
        compiler_params=_cparams("arbitrary", "arbitrary"),
        name="rglru_prompt",
    )(proj, proj, cw, cb, wr, br, wi, bi, lam)


def _rglru_step_kernel(xa_ref, ga_ref, b0_ref, b1_ref, b2_ref, h0_ref, cw_ref, cb_ref,
                       wr_ref, br_ref, wi_ref, bi_ref, lam_ref, ya_ref, h_ref):
    xa = xa_ref[...]
    xc = (cb_ref[...] + cw_ref[0:1, :] * b0_ref[...] + cw_ref[1:2, :] * b1_ref[...]
          + cw_ref[2:3, :] * b2_ref[...] + cw_ref[3:4, :] * xa)
    a, u = _rglru_gates(xc, wr_ref, br_ref[...], wi_ref, bi_ref[...], lam_ref[...])
    h = a * h0_ref[...] + u
    h_ref[...] = h
    ya_ref[...] = h * _gelu(ga_ref[...])


def rglru_step(proj_s, conv_state, h0, cw, cb, wr, br, wi, bi, lam):
    n = proj_s.shape[0]
    full = lambda shape: pl.BlockSpec(shape, lambda i: tuple(0 for _ in shape))
    return pl.pallas_call(
        _rglru_step_kernel,
        grid=(1,),
        in_specs=[
            pl.BlockSpec((n, RG_WIDTH), lambda i: (0, 0)),
            pl.BlockSpec((n, RG_WIDTH), lambda i: (0, 1)),
            full((n, RG_WIDTH)), full((n, RG_WIDTH)), full((n, RG_WIDTH)), full((n, RG_WIDTH)),
            full((CONV_W, RG_WIDTH)), full((1, RG_WIDTH)),
            full((RG_BLOCKS, LANES, LANES)), full((1, RG_WIDTH)),
            full((RG_BLOCKS, LANES, LANES)), full((1, RG_WIDTH)), full((1, RG_WIDTH)),
        ],
        out_specs=[full((n, RG_WIDTH)), full((n, RG_WIDTH))],
        out_shape=[jax.ShapeDtypeStruct((n, RG_WIDTH), f32)] * 2,
        compiler_params=_cparams("arbitrary"),
        name="rglru_step",
    )(proj_s, proj_s, conv_state[:, 0], conv_state[:, 1], conv_state[:, 2], h0,
      cw, cb, wr, br, wi, bi, lam)


def _rope_tables(pos):
    half = ROT_DIMS // 2
    inv = ROPE_THETA ** (-jnp.arange(half, dtype=f32) / half)
    ang = pos.astype(f32)[:, None] * inv[None, :]
    cos, sin = jnp.cos(ang), jnp.sin(ang)
    n = pos.shape[0]
    ctab = jnp.concatenate([cos, cos, jnp.ones((n, DH - ROT_DIMS), f32)], axis=1)
    stab = jnp.concatenate([-sin, sin, jnp.zeros((n, DH - ROT_DIMS), f32)], axis=1)
    return ctab, stab


def _rope_head(xh, ctab, stab, lane):
    half = ROT_DIMS // 2
    partner = jnp.where(lane < half, pltpu.roll(xh, DH - half, 1), pltpu.roll(xh, half, 1))
    return xh * ctab + partner * stab


def _rope_kernel(q_ref, k_ref, v_ref, c_ref, s_ref, qo_ref, ko_ref, vo_ref):
    ctab, stab = c_ref[...], s_ref[...]
    lane = _iota((q_ref.shape[0], DH), 1)
    for hb in range(HEADS):
        sl = slice(hb * DH, (hb + 1) * DH)
        qo_ref[:, hb, :] = _rope_head(q_ref[:, sl], ctab, stab, lane) * (DH ** -0.5)
        ko_ref[:, hb, :] = _rope_head(k_ref[:, sl], ctab, stab, lane)
        vo_ref[:, hb, :] = v_ref[:, sl]


def rope_qkv(proj, ctab, stab, rows_per_seq, tt):
    m = proj.shape[0]
    nt = max(rows_per_seq // tt, 1)
    if ctab.shape[0] == 1:
        tab_spec = pl.BlockSpec((1, DH), lambda i: (0, 0))
    else:
        tab_spec = pl.BlockSpec((tt, DH), lambda i: (i % nt, 0))
    return pl.pallas_call(
        _rope_kernel,
        grid=(m // tt,),
        in_specs=[
            pl.BlockSpec((tt, ATT_W), lambda i: (i, 2)),
            pl.BlockSpec((tt, ATT_W), lambda i: (i, 3)),
            pl.BlockSpec((tt, ATT_W), lambda i: (i, 4)),
            tab_spec, tab_spec,
        ],
        out_specs=[pl.BlockSpec((tt, HEADS, DH), lambda i: (i, 0, 0))] * 3,
        out_shape=[jax.ShapeDtypeStruct((m, HEADS, DH), f32)] * 3,
        compiler_params=_cparams("arbitrary"),
        name="rope",
    )(proj, proj, proj, ctab, stab)


def _rope_split_kernel(q_ref, k_ref, v_ref, c_ref, s_ref, *refs):
    n_pat = len(DIL_PATTERNS)
    outs, (kf_ref, qs_ref, ks_ref) = refs[:3 * n_pat], refs[3 * n_pat:]
    tt = q_ref.shape[0]
    ctab, stab = c_ref[...], s_ref[...]
    lane = _iota((tt, DH), 1)
    qs_ref[...] = _rope_head(q_ref[...], ctab, stab, lane) * (DH ** -0.5)
    k_rot = _rope_head(k_ref[...], ctab, stab, lane)
    ks_ref[...] = k_rot
    kf_ref[...] = k_rot
    for gi, (_, d) in enumerate(DIL_PATTERNS):
        rows = tt // d
        for src, dst in ((qs_ref, outs[3 * gi]), (ks_ref, outs[3 * gi + 1]), (v_ref, outs[3 * gi + 2])):
            for r in range(d):
                dst[0, 0, r] = src[pl.ds(r, rows, stride=d), :].astype(bf16)


def rope_split(proj, ctab, stab, bsz, seq, tt=512):
    m = proj.shape[0]
    nt = seq // tt
    col = lambda c: pl.BlockSpec((tt, DH), lambda i, h, c=c: (i, c * HEADS + h))
    tab = pl.BlockSpec((tt, DH), lambda i, h: (i % nt, 0))
    out_specs, out_shape = [], []
    for (_, d) in DIL_PATTERNS:
        for _ in range(3):
            out_specs.append(pl.BlockSpec((1, 1, d, tt // d, DH), lambda i, h: (i // nt, h, 0, i % nt, 0)))
            out_shape.append(jax.ShapeDtypeStruct((bsz, HEADS, d, seq // d, DH), bf16))
    out_specs.append(pl.BlockSpec((tt, DH), lambda i, h: (i, h)))
    out_shape.append(jax.ShapeDtypeStruct((m, ATT_W), f32))
    res = pl.pallas_call(
        _rope_split_kernel,
        grid=(m // tt, HEADS),
        in_specs=[col(2), col(3), col(4), tab, tab],
        out_specs=out_specs,
        out_shape=out_shape,
        scratch_shapes=[pltpu.VMEM((tt, DH), f32), pltpu.VMEM((tt, DH), f32)],
        compiler_params=_cparams("arbitrary", "arbitrary"),
        name="rope_split",
    )(proj, proj, proj, ctab, stab)
    return [res[3 * gi:3 * gi + 3] for gi in range(len(DIL_PATTERNS))], res[-1]


def _dil_kernel(q_ref, kc_ref, kp_ref, vc_ref, vp_ref, o_ref, l_ref, s_ref, p_ref, *, span):
    tq = q_ref.shape[3]
    blk = pl.program_id(2)
    qi = _iota((tq, 2 * tq), 0)
    col = _iota((tq, 2 * tq), 1)
    rel = jnp.where(col < tq, qi - col, qi - col + 2 * tq)
    ok = (rel >= 0) & (rel <= span) & ((col < tq) | (blk > 0))
    for hb in range(HEADS):
        qh = q_ref[0, hb, 0]
        s_ref[hb, :, :tq] = _dot_nt(qh, kc_ref[0, hb, 0])
        s_ref[hb, :, tq:] = _dot_nt(qh, kp_ref[0, hb, 0])
    for hb in range(HEADS):
        s = jnp.where(ok, s_ref[hb], NEG_INF)
        mx = jnp.max(s, axis=1, keepdims=True)
        p = jnp.exp(s - mx)
        den = jnp.sum(p, axis=1, keepdims=True)
        p_ref[hb] = (p * (1.0 / den)).astype(bf16)
        l_ref[0, hb, 0] = jnp.broadcast_to(mx + jnp.log(den), (tq, DH))
    for hb in range(HEADS):
        o_ref[0, hb, 0] = (_dot(p_ref[hb, :, :tq], vc_ref[0, hb, 0])
                           + _dot(p_ref[hb, :, tq:], vp_ref[0, hb, 0]))


def _dil_merge_kernel(*refs):
    n_pat = len(DIL_PATTERNS)
    ins, o_ref, scr = refs[:2 * n_pat], refs[2 * n_pat], refs[2 * n_pat + 1:]
    tt = o_ref.shape[0]
    for gi, (_, d) in enumerate(DIL_PATTERNS):
        rows = tt // d
        for src, dst in ((ins[2 * gi], scr[2 * gi]), (ins[2 * gi + 1], scr[2 * gi + 1])):
            for r in range(d):
                dst[pl.ds(r, rows, stride=d), :] = src[0, 0, r]
    lses = [scr[2 * gi + 1][...] for gi in range(n_pat)]
    top = functools.reduce(jnp.maximum, lses)
    es = [jnp.exp(l - top) for l in lses]
    num = sum(scr[2 * gi][...] * es[gi] for gi in range(n_pat))
    o_ref[...] = num / sum(es)


def dilated_prompt(qkv_by_pattern, bsz, seq, tq=128, tt=512):
    partial = []
    for (w, d), (qd, kd, vd) in zip(DIL_PATTERNS, qkv_by_pattern):
        sd = seq // d
        blk = (1, HEADS, 1, tq, DH)
        cur = pl.BlockSpec(blk, lambda b, r, i: (b, 0, r, i, 0))
        prev = pl.BlockSpec(blk, lambda b, r, i: (b, 0, r, jnp.maximum(i - 1, 0), 0))
        partial += pl.pallas_call(
            functools.partial(_dil_kernel, span=w // d),
            grid=(bsz, d, sd // tq),
            in_specs=[cur, cur, prev, cur, prev],
            out_specs=[cur, cur],
            out_shape=[jax.ShapeDtypeStruct((bsz, HEADS, d, sd, DH), f32)] * 2,
            scratch_shapes=[pltpu.VMEM((HEADS, tq, 2 * tq), f32), pltpu.VMEM((HEADS, tq, 2 * tq), bf16)],
            compiler_params=_cparams("arbitrary", "arbitrary", "arbitrary"),
            name=f"dilattn_d{d}",
        )(qd, kd, kd, vd, vd)
    nt = seq // tt
    in_specs = []
    for (_, d) in DIL_PATTERNS:
        in_specs += [pl.BlockSpec((1, 1, d, tt // d, DH), lambda i, h: (i // nt, h, 0, i % nt, 0))] * 2
    return pl.pallas_call(
        _dil_merge_kernel,
        grid=(bsz * nt, HEADS),
        in_specs=in_specs,
        out_specs=pl.BlockSpec((tt, DH), lambda i, h: (i, h)),
        out_shape=jax.ShapeDtypeStruct((bsz * seq, ATT_W), f32),
        scratch_shapes=[pltpu.VMEM((tt, DH), f32)] * (2 * len(DIL_PATTERNS)),
        compiler_params=_cparams("arbitrary", "arbitrary"),
        name="dilattn_merge",
    )(*partial)


def _dil_step_kernel(q_ref, kn_ref, vn_ref, k1_ref, k4_ref, k16_ref, v1_ref, v4_ref, v16_ref, o_ref):
    q = q_ref[0]
    kn, vn = kn_ref[0], vn_ref[0]
    s_self = _rowsum_bcast(q * kn)
    o_gs, lse_gs = [], []
    for k_ref, v_ref in ((k1_ref, v1_ref), (k4_ref, v4_ref), (k16_ref, v16_ref)):
        k3 = k_ref[0, :, 0]
        nk = k3.shape[0]
        s = _rowsum_bcast((k3 * q[None]).reshape(nk * HEADS, DH)).reshape(nk, HEADS, DH)
        mx = jnp.maximum(jnp.max(s, axis=0), s_self)
        p = jnp.exp(s - mx[None])
        p_self = jnp.exp(s_self - mx)
        den = jnp.sum(p, axis=0) + p_self
        num = jnp.sum(p * v_ref[0, :, 0], axis=0) + p_self * vn
        o_gs.append(num / den)
        lse_gs.append(mx + jnp.log(den))
    top = jnp.maximum(jnp.maximum(lse_gs[0], lse_gs[1]), lse_gs[2])
    es = [jnp.exp(l - top) for l in lse_gs]
    tot = es[0] + es[1] + es[2]
    o_ref[0] = (o_gs[0] * es[0] + o_gs[1] * es[1] + o_gs[2] * es[2]) / tot


def dilated_step(q4, k4, v4, cache_k, cache_v):
    n, win = cache_k.shape[0], cache_k.shape[1]
    one = pl.BlockSpec((1, HEADS, DH), lambda b: (b, 0, 0))
    args = [q4, k4, v4]
    in_specs = [one, one, one]
    for cache in (cache_k, cache_v):
        for (w, d) in DIL_PATTERNS:
            nkeys = w // d
            assert win % d == 0 and (win // d) % nkeys == 0 and (win - w) % (d * nkeys) == 0
            args.append(cache.reshape(n, win // d, d, HEADS, DH))
            in_specs.append(pl.BlockSpec((1, nkeys, 1, HEADS, DH),
                                         lambda b, blk=(win - w) // d // nkeys: (b, blk, 0, 0, 0)))
    return pl.pallas_call(
        _dil_step_kernel,
        grid=(n,),
        in_specs=in_specs,
        out_specs=one,
        out_shape=jax.ShapeDtypeStruct((n, HEADS, DH), f32),
        compiler_params=_cparams("arbitrary"),
        name="dilattn_step",
    )(*args).reshape(n, ATT_W)


def _sb_kernel(bias_ref, q_ref, k_ref, v_ref, o_ref, kb_ref, vb_ref, qs_ref, t_ref, spb_ref, wb_ref,
               acc_ref, run_ref):
    tq = q_ref.shape[1]
    nh = q_ref.shape[2] // DH
    kt_w = t_ref.shape[2]
    sub = PAGE
    hg = pl.program_id(1)
    qb = pl.program_id(2)

    @pl.when(qb == 0)
    def _():
        kb_ref[...] = k_ref[0].astype(bf16)
        vb_ref[...] = v_ref[0].astype(bf16)

    rr = _iota((sub, 2 * sub), 0)
    cc = _iota((sub, 2 * sub), 1)
    tri = jnp.where((rr > cc) | (cc >= sub), 1.0, 0.0).astype(bf16)
    for h in range(nh):
        qs_ref[h] = (q_ref[0, :, h * DH:(h + 1) * DH] * (DH ** -0.5)).astype(bf16)
    acc_ref[...] = jnp.zeros_like(acc_ref)
    run_ref[...] = jnp.zeros_like(run_ref)

    def macro(start, masked):
        for h in range(nh):
            hs = slice(h * DH, (h + 1) * DH)
            z = _dot_nt(qs_ref[h], kb_ref[pl.ds(start, kt_w), hs]) + bias_ref[hg * nh + h]
            sp = _softplus(z)
            t = z - sp
            if masked:
                ok = start + _iota((tq, kt_w), 1) < qb * tq + _iota((tq, kt_w), 0)
                sp = jnp.where(ok, sp, 0.0)
                t = jnp.where(ok, t, NEG_INF)
            t_ref[h] = t
            spb_ref[h] = sp.astype(bf16)
        for h in range(nh):
            run = run_ref[h]
            for kt in range(kt_w // sub - 1, -1, -1):
                ks = slice(kt * sub, (kt + 1) * sub)
                cs = _dot(spb_ref[h, :, ks], tri)
                wb_ref[h, :, ks] = jnp.exp(t_ref[h, :, ks] - cs[:, :sub] - run).astype(bf16)
                run = run + cs[:, sub:]
            run_ref[h] = run
        for h in range(nh):
            hs = slice(h * DH, (h + 1) * DH)
            acc_ref[h] += _dot(wb_ref[h], vb_ref[pl.ds(start, kt_w), hs])

    top = (qb * tq) // kt_w
    macro(pl.multiple_of(top * kt_w, kt_w), True)

    def body(it, _):
        macro(pl.multiple_of((top - 1 - it) * kt_w, kt_w), False)
        return 0

    lax.fori_loop(0, top, body, 0)
    for h in range(nh):
        o_ref[0, :, h * DH:(h + 1) * DH] = acc_ref[h]


def sb_prompt(proj2, bias, bsz, seq, tq=256, nh=4, kt_w=512):
    assert kt_w % tq == 0 and seq % kt_w == 0
    p3 = proj2.reshape(bsz, seq, proj2.shape[1])
    nq = seq // tq
    ng = HEADS // nh
    wd = nh * DH
    return pl.pallas_call(
        _sb_kernel,
        grid=(bsz, ng, nq),
        in_specs=[
            pl.BlockSpec(memory_space=pltpu.SMEM),
            pl.BlockSpec((1, tq, wd), lambda b, g, i: (b, i, g)),
            pl.BlockSpec((1, seq, wd), lambda b, g, i: (b, 0, ng + g)),
            pl.BlockSpec((1, seq, wd), lambda b, g, i: (b, 0, 2 * ng + g)),
        ],
        out_specs=pl.BlockSpec((1, tq, wd), lambda b, g, i: (b, i, g)),
        out_shape=jax.ShapeDtypeStruct((bsz, seq, ATT_W), f32),
        scratch_shapes=[
            pltpu.VMEM((seq, wd), bf16), pltpu.VMEM((seq, wd), bf16),
            pltpu.VMEM((nh, tq, DH), bf16),
            pltpu.VMEM((nh, tq, kt_w), f32), pltpu.VMEM((nh, tq, kt_w), bf16),
            pltpu.VMEM((nh, tq, kt_w), bf16),
            pltpu.VMEM((nh, tq, DH), f32), pltpu.VMEM((nh, tq, PAGE), f32),
        ],
        compiler_params=_cparams("arbitrary", "arbitrary", "arbitrary"),
        name="sb_prompt",
    )(bias, p3, p3, p3).reshape(bsz * seq, ATT_W)


def _sb_step_kernel(pt_ref, q_ref, bias_ref, eye_ref, *refs, npg):
    k_refs, v_refs = refs[:npg], refs[npg:2 * npg]
    o_ref, acc_ref, carry_ref = refs[2 * npg:]
    j = pl.program_id(1)

    @pl.when(j == 0)
    def _():
        acc_ref[...] = jnp.zeros_like(acc_ref)
        carry_ref[...] = jnp.zeros_like(carry_ref)

    q = q_ref[0]
    bias = bias_ref[...]
    ones = jnp.ones((DH, DH), bf16)
    rr = _iota((PAGE, 2 * PAGE), 0)
    cc = _iota((PAGE, 2 * PAGE), 1)
    tri = jnp.where((rr > cc) | (cc >= PAGE), 1.0, 0.0).astype(bf16)
    acc = acc_ref[...]
    run = carry_ref[...]
    pages = range(npg)
    zbs = [_dot((k_refs[p][0] * q[None]).reshape(PAGE * HEADS, DH).astype(bf16), ones)
           .reshape(PAGE, HEADS, DH) for p in pages]
    zs = [jnp.sum(zb * eye_ref[...], axis=0) + bias for zb in zbs]
    sps = [_softplus(z) for z in zs]
    css = []
    for sp in sps:
        hi, mid, lo = _split3(sp)
        css.append(_dot(hi, tri) + _dot(mid, tri) + _dot(lo, tri))
    ws = []
    for z, sp, cs in zip(zs, sps, css):
        ws.append(jnp.exp(z - sp - cs[:, :PAGE] - run))
        run = run + cs[:, PAGE:]
    wbs = [_dot((eye_ref[...] * w[None]).reshape(PAGE * HEADS, DH).astype(bf16), ones)
           .reshape(PAGE, HEADS, DH) for w in ws]
    for p, wb in zip(pages, wbs):
        acc = acc + jnp.sum(wb * v_refs[p][0], axis=0)
    acc_ref[...] = acc
    carry_ref[...] = run

    @pl.when(j == pl.num_programs(1) - 1)
    def _():
        o_ref[0] = acc


def sb_step(q_s, bias, cache_k, cache_v, page_table, npg=8):
    n, n_pages = page_table.shape
    q4 = (q_s * (DH ** -0.5)).reshape(n, HEADS, DH)
    bias4 = jnp.broadcast_to(bias[:, None], (HEADS, DH))
    eye3 = jnp.broadcast_to(jnp.eye(PAGE, DH, dtype=f32)[:, None, :], (PAGE, HEADS, DH))

    def page(p):
        return lambda b, j, pt: (pt[b, n_pages - 1 - (j * npg + p)], 0, 0, 0)

    kv_specs = [pl.BlockSpec((1, PAGE, HEADS, DH), page(p)) for p in range(npg)]
    grid_spec = pltpu.PrefetchScalarGridSpec(
        num_scalar_prefetch=1,
        grid=(n, n_pages // npg),
        in_specs=[
            pl.BlockSpec((1, HEADS, DH), lambda b, j, pt: (b, 0, 0)),
            pl.BlockSpec((HEADS, DH), lambda b, j, pt: (0, 0)),
            pl.BlockSpec((PAGE, HEADS, DH), lambda b, j, pt: (0, 0, 0)),
        ] + kv_specs + kv_specs,
        out_specs=pl.BlockSpec((1, HEADS, DH), lambda b, j, pt: (b, 0, 0)),
        scratch_shapes=[pltpu.VMEM((HEADS, DH), f32), pltpu.VMEM((HEADS, DH), f32)],
    )
    return pl.pallas_call(
        functools.partial(_sb_step_kernel, npg=npg),
        grid_spec=grid_spec,
        out_shape=jax.ShapeDtypeStruct((n, HEADS, DH), f32),
        compiler_params=_cparams("arbitrary", "arbitrary"),
        name="sb_step",
    )(page_table, q4, bias4, eye3, *([cache_k] * npg), *([cache_v] * npg)).reshape(n, ATT_W)


def _mlstm_kernel(q_ref, k_ref, v_ref, og_ref, gate_ref, gb_ref, gn_ref,
                  h_ref, c_out, n_out, m_out, c_s, n_s, m_s):
    ci = pl.program_id(1)
    L = CHUNK

    @pl.when(ci == 0)
    def _():
        c_s[...] = jnp.zeros_like(c_s)
        n_s[...] = jnp.zeros_like(n_s)
        m_s[...] = jnp.zeros_like(m_s)

    gt = gate_ref[...] + gb_ref[...]
    gt_t = gt.T
    ri = _iota((L, L), 0)
    li = _iota((L, L), 1)
    causal = li <= ri
    tri_incl = jnp.where(causal, 1.0, 0.0).astype(bf16)
    tri_incl_t = jnp.where(ri <= li, 1.0, 0.0).astype(bf16)
    for h in range(M_HEADS):
        sl = slice(h * M_DH, (h + 1) * M_DH)
        ig_col = gt[:, h:h + 1]
        ig_row = gt_t[h:h + 1, :]
        lf_col = _log_sigmoid_pair(gt[:, M_HEADS + h:M_HEADS + h + 1])[0]
        lf_row = _log_sigmoid_pair(gt_t[M_HEADS + h:M_HEADS + h + 1, :])[0]
        c_hi, c_mid, c_lo = _split3(jnp.broadcast_to(lf_col, (L, L)))
        bcum_col = _dot(tri_incl, c_hi) + _dot(tri_incl, c_mid) + _dot(tri_incl, c_lo)
        r_hi, r_mid, r_lo = _split3(jnp.broadcast_to(lf_row, (L, L)))
        bcum_row = _dot(r_hi, tri_incl_t) + _dot(r_mid, tri_incl_t) + _dot(r_lo, tri_incl_t)
        m_prev = m_s[h:h + 1, :]
        dlog = jnp.where(causal, bcum_col - bcum_row + ig_row, NEG_INF)
        inter = bcum_col + m_prev
        m_t = jnp.maximum(inter, jnp.max(dlog, axis=1, keepdims=True))
        dw = jnp.exp(dlog - m_t)
        iw = jnp.exp(inter - m_t)
        qh = q_ref[:, sl]
        kh = k_ref[:, sl] * (M_DH ** -0.5)
        vh = v_ref[:, sl]
        qb, kb, vb = qh.astype(bf16), kh.astype(bf16), vh.astype(bf16)
        sw = dw * _dot_nt(qb, kb)
        c_prev = c_s[h]
        n_prev = n_s[h:h + 1, :]
        iw_col = iw[:, 0:1]
        num = _dot(sw.astype(bf16), vb) + iw_col * _dot_nt(qb, c_prev.astype(bf16))
        qn = jnp.sum(qb.astype(f32) * n_prev.astype(bf16).astype(f32), axis=1, keepdims=True)
        den = jnp.sum(sw, axis=1, keepdims=True) + iw_col * qn
        m_col = m_t[:, 0:1]
        hout = num / jnp.maximum(jnp.abs(den), jnp.exp(-m_col))
        m_last = m_t[L - 1:L, :]
        b_last = bcum_col[L - 1:L, :]
        wl_col = jnp.exp(b_last[:, 0:1] - bcum_col[:, 0:1] + ig_col - m_last[:, 0:1])
        wl_row = jnp.exp(b_last - bcum_row[0:1, :] + ig_row - m_last)
        decay = jnp.exp(b_last + m_prev - m_last)
        dsc = decay[:, 0:1]
        c_s[h] = dsc * c_prev + _dot((vh * wl_col).T.astype(bf16), kb)
        wl8 = jnp.broadcast_to(wl_row, (8, L)).astype(bf16)
        n_s[h:h + 1, :] = dsc * n_prev + _dot(wl8, kb)[0:1, :]
        m_s[h:h + 1, :] = m_last
        hn = hout * lax.rsqrt(jnp.mean(hout * hout, axis=1, keepdims=True) + EPS)
        h_ref[:, sl] = hn * gn_ref[:, sl] * _sigmoid(og_ref[:, sl])

    @pl.when(ci == pl.num_programs(1) - 1)
    def _():
        c_out[0] = c_s[...]
        n_out[0] = n_s[...]
        m_out[0] = m_s[...]


def mlstm_prompt(proj2, gates, gate_bias, g_mnorm, bsz, seq):
    nc = seq // CHUNK
    col = lambda c: pl.BlockSpec((CHUNK, M_WIDTH), lambda b, i, c=c: (b * nc + i, c))
    hm, c1, n1, m1 = pl.pallas_call(
        _mlstm_kernel,
        grid=(bsz, nc),
        in_specs=[
            col(3), col(4), col(5), col(6),
            pl.BlockSpec((CHUNK, LANES), lambda b, i: (b * nc + i, 0)),
            pl.BlockSpec((1, LANES), lambda b, i: (0, 0)),
            pl.BlockSpec((1, M_WIDTH), lambda b, i: (0, 0)),
        ],
        out_specs=[
            pl.BlockSpec((CHUNK, M_WIDTH), lambda b, i: (b * nc + i, 0)),
            pl.BlockSpec((1, M_HEADS, M_DH, M_DH), lambda b, i: (b, 0, 0, 0)),
            pl.BlockSpec((1, M_HEADS, M_DH), lambda b, i: (b, 0, 0)),
            pl.BlockSpec((1, M_HEADS, LANES), lambda b, i: (b, 0, 0)),
        ],
        out_shape=[
            jax.ShapeDtypeStruct((bsz * seq, M_WIDTH), f32),
            jax.ShapeDtypeStruct((bsz, M_HEADS, M_DH, M_DH), f32),
            jax.ShapeDtypeStruct((bsz, M_HEADS, M_DH), f32),
            jax.ShapeDtypeStruct((bsz, M_HEADS, LANES), f32),
        ],
        scratch_shapes=[
            pltpu.VMEM((M_HEADS, M_DH, M_DH), f32),
            pltpu.VMEM((M_HEADS, M_DH), f32),
            pltpu.VMEM((M_HEADS, LANES), f32),
        ],
        compiler_params=_cparams("arbitrary", "arbitrary"),
        name="mlstm_prompt",
    )(proj2, proj2, proj2, proj2, gates, gate_bias, g_mnorm)
    return hm, c1, n1, m1[:, :, 0]


def _mlstm_step_kernel(q_ref, k_ref, v_ref, og_ref, gn_ref, ig_ref, fg_ref, c_ref, n_ref, m_ref,
                       h_ref, c_out, n_out, m_out):
    q = q_ref[0, 0]
    k = k_ref[0, 0] * (M_DH ** -0.5)
    v = v_ref[0, 0]
    ig = ig_ref[0, 0]
    lf = _log_sigmoid_pair(fg_ref[0, 0])[0]
    m0 = m_ref[0, 0]
    c0 = c_ref[0, 0]
    n0 = n_ref[0, 0]
    inter = lf + m0
    m_t = jnp.maximum(inter, ig)
    dw = jnp.exp(ig - m_t)
    iw = jnp.exp(inter - m_t)
    rnd = lambda a: a.astype(bf16).astype(f32)
    qr, kr, vr = rnd(q), rnd(k), rnd(v)
    qk = jnp.sum(qr * kr, axis=1, keepdims=True)
    sw = dw * qk
    cq = jnp.sum(rnd(c0) * qr, axis=1, keepdims=True)
    num = rnd(sw) * vr + iw * cq
    den = sw + iw * jnp.sum(rnd(n0) * qr, axis=1, keepdims=True)
    hout = num / jnp.maximum(jnp.abs(den), jnp.exp(-m_t))
    c_out[0, 0] = iw * c0 + rnd(dw * v) * kr
    n_out[0, 0] = iw * n0 + rnd(dw) * kr
    m_out[0, 0] = m_t
    hn = hout * lax.rsqrt(jnp.mean(hout * hout, axis=0, keepdims=True) + EPS)
    h_ref[0, 0] = hn * gn_ref[0] * _sigmoid(og_ref[0, 0])


def mlstm_step(proj2_s, gates_s, gate_bias, g_mnorm, c0, n0, m0):
    n = proj2_s.shape[0]
    base = 3 * ATT_W
    seg = lambda i: proj2_s[:, base + i * M_WIDTH: base + (i + 1) * M_WIDTH].reshape(n, M_HEADS, M_DH)
    g = gates_s + gate_bias
    rowb = pl.BlockSpec((1, 1, 1, M_DH), lambda b, h: (b, h, 0, 0))
    colb = pl.BlockSpec((1, 1, M_DH, 1), lambda b, h: (b, h, 0, 0))
    scal = pl.BlockSpec((1, 1, 1, 1), lambda b, h: (b, h, 0, 0))
    hcol, c1, n1, m1 = pl.pallas_call(
        _mlstm_step_kernel,
        grid=(n, M_HEADS),
        in_specs=[
            rowb, rowb, colb, colb,
            pl.BlockSpec((1, M_DH, 1), lambda b, h: (h, 0, 0)),
            scal, scal,
            pl.BlockSpec((1, 1, M_DH, M_DH), lambda b, h: (b, h, 0, 0)),
            rowb, scal,
        ],
        out_specs=[colb, pl.BlockSpec((1, 1, M_DH, M_DH), lambda b, h: (b, h, 0, 0)), rowb, scal],
        out_shape=[
            jax.ShapeDtypeStruct((n, M_HEADS, M_DH, 1), f32),
            jax.ShapeDtypeStruct((n, M_HEADS, M_DH, M_DH), f32),
            jax.ShapeDtypeStruct((n, M_HEADS, 1, M_DH), f32),
            jax.ShapeDtypeStruct((n, M_HEADS, 1, 1), f32),
        ],
        compiler_params=_cparams("arbitrary", "arbitrary"),
        name="mlstm_step",
    )(seg(0)[:, :, None, :], seg(1)[:, :, None, :], seg(2)[..., None], seg(3)[..., None],
      g_mnorm.reshape(M_HEADS, M_DH, 1),
      g[:, 0:M_HEADS].reshape(n, M_HEADS, 1, 1), g[:, M_HEADS:2 * M_HEADS].reshape(n, M_HEADS, 1, 1),
      c0, n0[:, :, None, :], m0.reshape(n, M_HEADS, 1, 1))
    return hcol.reshape(n, M_WIDTH), c1, n1.reshape(n, M_HEADS, M_DH), m1.reshape(n, M_HEADS)


def _peerq_kernel(x_ref, g_ref, sh_ref, sc_ref, wt_ref, qt_ref, xmt_ref, xs_ref):
    @pl.when(pl.program_id(1) == 0)
    def _():
        xm = _modulated(x_ref[...], g_ref[...], sh_ref[0], sc_ref[0])
        xt = xm.T.astype(bf16)
        xs_ref[...] = xt
        xmt_ref[...] = xt

    qt_ref[...] = _dot(wt_ref[...], xs_ref[...])


def peer_query(x, g, shift, scale, wq_t, rows_per_batch, tm, tn=512):
    m, k = x.shape
    n = wq_t.shape[0]
    sh, sh_spec = _mod_specs(shift, m, tm, k, rows_per_batch)
    sc, sc_spec = _mod_specs(scale, m, tm, k, rows_per_batch)
    return pl.pallas_call(
        _peerq_kernel,
        grid=(m // tm, n // tn),
        in_specs=[
            pl.BlockSpec((tm, k), lambda i, j: (i, 0)),
            pl.BlockSpec((1, k), lambda i, j: (0, 0)),
            sh_spec, sc_spec,
            pl.BlockSpec((tn, k), lambda i, j: (j, 0)),
        ],
        out_specs=[pl.BlockSpec((tn, tm), lambda i, j: (j, i)),
                   pl.BlockSpec((k, tm), lambda i, j: (0, i))],
        out_shape=[jax.ShapeDtypeStruct((n, m), f32), jax.ShapeDtypeStruct((k, m), bf16)],
        scratch_shapes=[pltpu.VMEM((k, tm), bf16)],
        compiler_params=_cparams("arbitrary", "arbitrary"),
        name="peer_query",
    )(x, g.reshape(1, k), sh, sc, wq_t)


def _heads_layout_kernel(*refs):
    n = len(refs) // 2
    for x_ref, o_ref in zip(refs[:n], refs[n:]):
        for hb in range(HEADS):
            o_ref[:, hb, :] = x_ref[:, hb * DH:(hb + 1) * DH]


def heads_layout(srcs, bsz, seq, tail, tt=512):
    nt, first = tail // tt, (seq - tail) // tt
    rows = lambda b, i: b * (seq // tt) + first + i
    return pl.pallas_call(
        _heads_layout_kernel,
        grid=(bsz, nt),
        in_specs=[pl.BlockSpec((tt, ATT_W), lambda b, i, c=c: (rows(b, i), c)) for _, c in srcs],
        out_specs=[pl.BlockSpec((tt, HEADS, DH), lambda b, i: (b * nt + i, 0, 0))] * len(srcs),
        out_shape=[jax.ShapeDtypeStruct((bsz * tail, HEADS, DH), f32)] * len(srcs),
        compiler_params=_cparams("arbitrary", "arbitrary"),
        name="heads_layout",
    )(*[x for x, _ in srcs])


def _cast_kernel(w_ref, o_ref):
    o_ref[...] = w_ref[0].astype(bf16)


def cast_layer_bf16(w, layer, tr=1024):
    _, rows, cols = w.shape
    return pl.pallas_call(
        _cast_kernel,
        grid=(rows // tr,),
        in_specs=[pl.BlockSpec((1, tr, cols), lambda i: (layer, i, 0))],
        out_specs=pl.BlockSpec((tr, cols), lambda i: (i, 0)),
        out_shape=jax.ShapeDtypeStruct((rows, cols), bf16),
        compiler_params=_cparams("arbitrary"),
        name="cast_bf16",
    )(w)


_N_TOP = PEER_TOPK + 1
_CAND_PAIRS = [(a, b) for a in range(_N_TOP) for b in range(_N_TOP) if (a + 1) * (b + 1) <= _N_TOP]
_N_CAND = -(-len(_CAND_PAIRS) // 8) * 8


def _extract_top(cur, ridx, n):
    vals = []
    big = float(cur.shape[0])
    for _ in range(n):
        mx = jnp.max(cur, axis=0, keepdims=True)
        first = jnp.min(jnp.where(cur == mx, ridx, big), axis=0, keepdims=True)
        cur = jnp.where(ridx == first, NEG_INF, cur)
        vals.append(mx)
    return vals


def _oddeven_mergesort_pairs(n):
    pairs = []

    def merge(lo, cnt, r):
        step = 2 * r
        if step < cnt:
            merge(lo, cnt, step)
            merge(lo + r, cnt, step)
            pairs.extend((i, i + r) for i in range(lo + r, lo + cnt - r, step))
        else:
            pairs.append((lo, lo + r))

    def sort(lo, cnt):
        if cnt > 1:
            sort(lo, cnt // 2)
            sort(lo + cnt // 2, cnt // 2)
            merge(lo, cnt, 1)

    sort(0, n)
    return pairs


_SORT16 = _oddeven_mergesort_pairs(PEER_TOPK)


def _exchange(y, i, j):
    y[i], y[j] = jnp.maximum(y[i], y[j]), jnp.minimum(y[i], y[j])


def _top17_network(s):
    n = PEER_TOPK
    y = [s[v * 8:(v + 1) * 8, :] for v in range(n)]
    for i, j in _SORT16:
        _exchange(y, i, j)
    dropped = None
    for shift in (4, 2, 1):
        other = [pltpu.roll(y[n - 1 - i], shift, 0) for i in range(n)]
        low = functools.reduce(jnp.maximum, [jnp.minimum(y[i], other[i]) for i in range(n)])
        dropped = low if dropped is None else jnp.maximum(dropped, low)
        y = [jnp.maximum(y[i], other[i]) for i in range(n)]
        dist = n // 2
        while dist:
            for i in range(n):
                if not i & dist:
                    _exchange(y, i, i + dist)
            dist //= 2
    return [v[0:1, :] for v in y] + [jnp.max(dropped, axis=0, keepdims=True)]


_ROUTER_UNROLL = 8


def _router_kernel(qt_ref, keys_ref, e1_ref, e2_ref, th_ref, cand_ref):
    cidx = _iota((_N_CAND, LANES), 0).astype(f32)
    k0 = keys_ref[0].astype(bf16)
    k1 = keys_ref[1].astype(bf16)
    cand_ref[...] = jnp.full(cand_ref.shape, NEG_INF, f32)

    def one_head(h, slot):
        r0 = pl.multiple_of(h * 2 * N_KEYS, 2 * N_KEYS)
        s1 = _dot(k0, qt_ref[pl.ds(r0, N_KEYS), :].astype(bf16))
        s2 = _dot(k1, qt_ref[pl.ds(r0 + N_KEYS, N_KEYS), :].astype(bf16))
        top_a = _top17_network(s1)
        top_b = _top17_network(s2)
        for ci, (a, b) in enumerate(_CAND_PAIRS):
            cand_ref[slot, ci:ci + 1, :] = top_a[a] + top_b[b]
        cs = _extract_top(cand_ref[slot], cidx, _N_TOP)
        z = jnp.ones_like(cs[0])
        for r in range(1, PEER_TOPK):
            z = z + jnp.exp(cs[r] - cs[0])
        inv_z = 1.0 / z
        mid = 0.5 * (cs[PEER_TOPK - 1] + cs[PEER_TOPK])
        o0 = pl.multiple_of(h * N_KEYS, N_KEYS)
        e1 = jnp.exp(s1 - top_a[0]) * inv_z
        e1_ref[:, h] = e1.reshape(N_KEYS // 8, 8, LANES)
        e2_ref[pl.ds(o0, N_KEYS), :] = jnp.exp(s2 - top_b[0])
        th_ref[pl.ds(h, 1), :] = jnp.exp(mid - cs[0]) * inv_z

    def body(hp, _):
        for slot in range(_ROUTER_UNROLL):
            one_head(hp * _ROUTER_UNROLL + slot, slot)
        return 0

    lax.fori_loop(0, PEER_HEADS // _ROUTER_UNROLL, body, 0)


def peer_router(qt, keys):
    n, m = qt.shape
    half = PEER_HEADS * N_KEYS
    return pl.pallas_call(
        _router_kernel,
        grid=(m // LANES,),
        in_specs=[pl.BlockSpec((n, LANES), lambda i: (0, i)),
                  pl.BlockSpec((2, N_KEYS, N_KEYS), lambda i: (0, 0, 0))],
        out_specs=[pl.BlockSpec((N_KEYS // 8, PEER_HEADS, 8, LANES), lambda i: (0, 0, 0, i)),
                   pl.BlockSpec((half, LANES), lambda i: (0, i)),
                   pl.BlockSpec((PEER_HEADS, LANES), lambda i: (0, i))],
        out_shape=[jax.ShapeDtypeStruct((N_KEYS // 8, PEER_HEADS, 8, m), f32),
                   jax.ShapeDtypeStruct((half, m), f32),
                   jax.ShapeDtypeStruct((PEER_HEADS, m), f32)],
        scratch_shapes=[pltpu.VMEM((_ROUTER_UNROLL, _N_CAND, LANES), f32)],
        compiler_params=_cparams("arbitrary"),
        name="peer_router",
    )(qt, keys)


_G_CHAINS = 3
_PIECE = 256


class _Chains:
    def __init__(self, n, zero_ref):
        self.deps, self.cnt, self.zero_ref = [None] * n, 0, zero_ref

    def take(self):
        return self.deps[self.cnt % len(self.deps)]

    def put(self, g):
        bits = lax.bitcast_convert_type(g[0:1, :], jnp.int32) & self.zero_ref[0:1, :]
        self.deps[self.cnt % len(self.deps)] = lax.bitcast_convert_type(bits, f32)
        self.cnt += 1


def _routing_block(e1_ref, e2_ref, th_ref, ii, ls, chains):
    sub = 8
    rows = [e1_ref[ii // 8, h, ii % 8:ii % 8 + 1, ls] for h in range(PEER_HEADS)]
    ths = [th_ref[h:h + 1, ls] for h in range(PEER_HEADS)]
    parts = []
    for gi in range(N_KEYS // sub):
        dep = chains.take()
        g = jnp.zeros((sub, LANES), f32)
        for h in range(PEER_HEADS):
            row = rows[h] if (dep is None or h) else rows[h] + dep
            pr = e2_ref[h * N_KEYS + gi * sub:h * N_KEYS + (gi + 1) * sub, ls] * row
            g = g + jnp.where(pr >= ths[h], pr, 0.0)
        chains.put(g)
        parts.append(g)
    return jnp.concatenate(parts, axis=0)


def _experts_kernel(xmt_ref, e1_ref, e2_ref, th_ref, u_ref, v_ref, x_ref, gt_ref, gf_ref, zero_ref,
                    o_ref, w_ref, *, final_norm):
    c = pl.program_id(1)
    tm, te = w_ref.shape

    @pl.when(c == 0)
    def _():
        o_ref[...] = jnp.zeros_like(o_ref)

    chains = _Chains(_G_CHAINS, zero_ref)
    for p in range(te // _PIECE):
        act = _gelu(_dot(u_ref[p * _PIECE:(p + 1) * _PIECE, :], xmt_ref[...]))
        for il in range(_PIECE // N_KEYS):
            ii = p * (_PIECE // N_KEYS) + il
            es = slice(ii * N_KEYS, (ii + 1) * N_KEYS)
            for lc in range(tm // LANES):
                ls = slice(lc * LANES, (lc + 1) * LANES)
                g_blk = _routing_block(e1_ref, e2_ref, th_ref, ii, ls, chains)
                a_blk = act[il * N_KEYS:(il + 1) * N_KEYS, ls]
                w_ref[ls, es] = (g_blk * a_blk).T.astype(bf16)
    o_ref[...] += _dot(w_ref[...], v_ref[...])

    @pl.when(c == pl.num_programs(1) - 1)
    def _():
        y = x_ref[...] + gt_ref[0] * o_ref[...]
        if final_norm:
            ms = jnp.mean(y * y, axis=-1, keepdims=True)
            y = y * lax.rsqrt(ms + EPS) * gf_ref[...]
        o_ref[...] = y


def peer_experts(xmt, e1t, e2t, th, u, v, x, gate, g_final, rows_per_batch, tm, te, final_norm):
    m, d = x.shape
    gt, gt_spec = _mod_specs(gate, m, tm, d, rows_per_batch)
    half = PEER_HEADS * N_KEYS
    return pl.pallas_call(
        functools.partial(_experts_kernel, final_norm=final_norm),
        grid=(m // tm, N_EXPERTS // te),
        in_specs=[
            pl.BlockSpec((d, tm), lambda i, c: (0, i)),
            pl.BlockSpec((te // N_KEYS // 8, PEER_HEADS, 8, tm), lambda i, c: (c, 0, 0, i)),
            pl.BlockSpec((half, tm), lambda i, c: (0, i)),
            pl.BlockSpec((PEER_HEADS, tm), lambda i, c: (0, i)),
            pl.BlockSpec((te, d), lambda i, c: (c, 0)),
            pl.BlockSpec((te, d), lambda i, c: (c, 0)),
            pl.BlockSpec((tm, d), lambda i, c: (i, 0)),
            gt_spec,
            pl.BlockSpec((1, d), lambda i, c: (0, 0)),
            pl.BlockSpec((8, LANES), lambda i, c: (0, 0)),
        ],
        out_specs=pl.BlockSpec((tm, d), lambda i, c: (i, 0)),
        out_shape=jax.ShapeDtypeStruct((m, d), f32),
        scratch_shapes=[pltpu.VMEM((tm, te), bf16)],
        compiler_params=_cparams("arbitrary", "arbitrary"),
        name="peer_experts",
    )(xmt, e1t, e2t, th, u, v, x, gt, g_final.reshape(1, d), jnp.zeros((8, LANES), jnp.int32))


def peer_block(x, g, shift, scale, gate, wq_t, keys, u, v, g_final, rows_per_batch, tm, te, final_norm):
    tm_q = 1024 if x.shape[0] % 1024 == 0 and rows_per_batch % 1024 == 0 else tm
    qt, xmt = peer_query(x, g, shift, scale, wq_t, rows_per_batch, tm_q)
    e1t, e2t, th = peer_router(qt, keys)
    return peer_experts(xmt, e1t, e2t, th, u, v, x, gate, g_final, rows_per_batch, tm, te, final_norm)


def kernel(x_prompt, x_sample, c_prompt, c_sample, state_rglru_conv, state_rglru_h, cache_swa_k, cache_swa_v, cache_sb_k, cache_sb_v, state_mlstm_C, state_mlstm_n, state_mlstm_m, page_table, w_ada, b_ada, g_norm_mix, g_norm_ffn, e_w_in, e_conv_w, e_conv_b, e_w_r, e_b_r, e_w_i, e_b_i, e_lambda, e_w_out, o_w_in, o_b_if, o_sb_bias, o_g_mnorm, o_w_out, peer_w_q, peer_keys, peer_u, peer_v, g_final):
    bp, seq, d = x_prompt.shape
    bs = x_sample.shape[0]
    mp = bp * seq
    pad_s = LANES
    xp = x_prompt.reshape(mp, d)
    xs = x_sample.reshape(bs, d)

    c_rows = 16
    c_all = jnp.concatenate([c_prompt, c_sample, jnp.zeros((c_rows - bp - bs, d), f32)], axis=0)
    mod = adaln_all(c_all, w_ada, b_ada)

    def mods(layer):
        parts = [mod[layer, :, i * d:(i + 1) * d] for i in range(6)]
        return [p[:bp] for p in parts], [p[bp:bp + bs] for p in parts]

    ctab_p, stab_p = _rope_tables(jnp.arange(seq, dtype=jnp.int32))
    ctab_s, stab_s = _rope_tables(jnp.full((1,), PAST_LEN, jnp.int32))

    TM = 512
    m_p, m_s = mods(0)
    w_in = e_w_in[0].astype(bf16)
    w_out = e_w_out[0].astype(bf16)
    cw, cb = e_conv_w[0], e_conv_b[0].reshape(1, -1)
    wr, wi = e_w_r[0].astype(bf16), e_w_i[0].astype(bf16)
    br, bi, lam = e_b_r[0].reshape(1, -1), e_b_i[0].reshape(1, -1), e_lambda[0].reshape(1, -1)

    proj_p = mod_matmul(xp, g_norm_mix[0], m_p[0], m_p[1], w_in, E_IN, seq, 1024, 1024, name="e_in_p")
    proj_s = mod_matmul(xs, g_norm_mix[0], m_s[0], m_s[1], w_in, E_IN, 1, bs, 512, name="e_in_s")

    ya_p, h_p = rglru_prompt(proj_p, bp, seq, cw, cb, wr, br, wi, bi, lam)
    ya_s, h_s = rglru_step(proj_s, state_rglru_conv[0], state_rglru_h[0], cw, cb, wr, br, wi, bi, lam)
    conv_p = proj_p.reshape(bp, seq, E_IN)[:, seq - (CONV_W - 1):, :RG_WIDTH]
    conv_s = jnp.concatenate([state_rglru_conv[0][:, 1:], proj_s[:, None, :RG_WIDTH]], axis=1)

    qkv_p, k_p = rope_split(proj_p, ctab_p, stab_p, bp, seq)
    q_s, k_s, v_s = rope_qkv(proj_s, ctab_s, stab_s, 1, bs)
    o_p = dilated_prompt(qkv_p, bp, seq)
    win = cache_swa_k.shape[2]
    o_s = dilated_step(q_s, k_s, v_s, cache_swa_k.reshape(-1, win, HEADS, DH)[:bs],
                       cache_swa_v.reshape(-1, win, HEADS, DH)[:bs])
    wl = min(2048, seq)
    swa_k_p, swa_v_p = [a.reshape(bp, wl, HEADS, DH)
                        for a in heads_layout([(k_p, 0), (proj_p, E_IN // ATT_W - 1)], bp, seq, wl)]
    swa_k_s = k_s.reshape(bs, 1, HEADS, DH)
    swa_v_s = v_s.reshape(bs, 1, HEADS, DH)

    xp = out_proj(ya_p, o_p, w_out, xp, m_p[2], seq, 1024, 1024, name="e_out_p")
    xs = out_proj(ya_s, o_s, w_out, xs, m_s[2], 1, bs, 512, name="e_out_s")

    def peer_layer(layer, xp, xs, m_p, m_s, final_norm):
        wq_t = peer_w_q[layer].T.astype(bf16)
        u = cast_layer_bf16(peer_u, layer)
        v = cast_layer_bf16(peer_v, layer)
        xp = peer_block(xp, g_norm_ffn[layer], m_p[3], m_p[4], m_p[5], wq_t, peer_keys[layer], u, v,
                        g_final, seq, TM, 1024, final_norm)
        xs_pad = jnp.pad(xs, ((0, pad_s - bs), (0, 0)))
        xs_new = peer_block(xs_pad, g_norm_ffn[layer], m_s[3], m_s[4], m_s[5], wq_t, peer_keys[layer],
                            u, v, g_final, 1, pad_s, 1024, final_norm)
        return xp, xs_new[:bs]

    xp, xs = peer_layer(0, xp, xs, m_p, m_s, False)

    m_p, m_s = mods(1)
    w_in2 = o_w_in[0].astype(bf16)
    w_gate = jnp.pad(w_in2[:, O_MAIN:], ((0, 0), (0, LANES - 2 * M_HEADS)))
    w_out2 = o_w_out[0].astype(bf16)
    gate_bias = jnp.pad(o_b_if[0].reshape(1, 2 * M_HEADS), ((0, 0), (0, LANES - 2 * M_HEADS)))
    gmn = o_g_mnorm[0].reshape(1, M_WIDTH)

    proj2_p, gates_p = mod_matmul(xp, g_norm_mix[1], m_p[0], m_p[1], w_in2, O_MAIN, seq, 1024, 1024,
                                  w_gate=w_gate, name="o_in_p")
    proj2_s, gates_s = mod_matmul(xs, g_norm_mix[1], m_s[0], m_s[1], w_in2, O_MAIN, 1, bs, 512,
                                  w_gate=w_gate, name="o_in_s")

    oc_p = sb_prompt(proj2_p, o_sb_bias[0], bp, seq)
    oc_s = sb_step(proj2_s[:, :ATT_W], o_sb_bias[0], cache_sb_k.reshape(-1, PAGE, HEADS, DH),
                   cache_sb_v.reshape(-1, PAGE, HEADS, DH), page_table)
    hm_p, mC_p, mn_p, mm_p = mlstm_prompt(proj2_p, gates_p, gate_bias, gmn, bp, seq)
    hm_s, mC_s, mn_s, mm_s = mlstm_step(proj2_s, gates_s, gate_bias, gmn,
                                        state_mlstm_C.reshape(-1, M_HEADS, M_DH, M_DH)[:bs],
                                        state_mlstm_n[0], state_mlstm_m[0])

    n_pg = seq // PAGE
    sb_k_p, sb_v_p = [a.reshape(bp, n_pg, PAGE, HEADS, DH)
                      for a in heads_layout([(proj2_p, 1), (proj2_p, 2)], bp, seq, seq)]
    sb_k_s = proj2_s[:, ATT_W:2 * ATT_W].reshape(bs, 1, HEADS, DH)
    sb_v_s = proj2_s[:, 2 * ATT_W:3 * ATT_W].reshape(bs, 1, HEADS, DH)

    xp = out_proj(oc_p, hm_p, w_out2, xp, m_p[2], seq, 1024, 1024, name="o_out_p")
    xs = out_proj(oc_s, hm_s, w_out2, xs, m_s[2], 1, bs, 512, name="o_out_s")
    xp, xs = peer_layer(1, xp, xs, m_p, m_s, True)

    y_prompt = xp.reshape(bp, seq, d)
    y_sample = xs.reshape(bs, 1, d)
    st = lambda a: a[None]
    return (y_prompt, y_sample, st(conv_p), st(conv_s), st(h_p.reshape(bp, RG_WIDTH)), st(h_s),
            st(swa_k_p), st(swa_k_s), st(swa_v_p), st(swa_v_s),
            st(sb_k_p), st(sb_k_s), st(sb_v_p), st(sb_v_s),
            st(mC_p), st(mC_s), st(mn_p), st(mn_s), st(mm_p), st(mm_s))
```

```python
import functools
import math

import jax
import jax.numpy as jnp
from jax import lax
from jax.experimental import pallas as pl
from jax.experimental.pallas import tpu as pltpu

f32 = jnp.float32
bf16 = jnp.bfloat16

D_MODEL = 2048
PAST_LEN = 16384
PAGE = 128
RG_WIDTH = 1024
RG_BLOCKS = 8
CONV_W = 4
RG_C = 8.0
HEADS = 8
DH = 128
ATT_W = HEADS * DH
DIL_PATTERNS = ((128, 1), (512, 4), (2048, 16))
ROT_DIMS = 32
ROPE_THETA = 500000.0
M_HEADS = 4
M_DH = 256
M_WIDTH = M_HEADS * M_DH
CHUNK = 128
E_IN = 2 * RG_WIDTH + 3 * ATT_W
O_MAIN = 3 * ATT_W + 4 * M_WIDTH
PEER_HEADS = 8
N_KEYS = 128
N_EXPERTS = N_KEYS * N_KEYS
PEER_TOPK = 16
EPS = 1e-6
LANES = 128
VMEM_LIMIT = 56 * 1024 * 1024
NEG_INF = float("-inf")


def _cparams(*sem):
    return pltpu.CompilerParams(dimension_semantics=sem, vmem_limit_bytes=VMEM_LIMIT)


def _dot(a, b):
    return jnp.dot(a, b, preferred_element_type=f32)


def _dot_nt(a, b):
    return lax.dot_general(a, b, (((1,), (1,)), ((), ())), preferred_element_type=f32)


def _split3(x):
    hi = x.astype(bf16)
    r = x - hi.astype(f32)
    mid = r.astype(bf16)
    lo = (r - mid.astype(f32)).astype(bf16)
    return hi, mid, lo


def _sigmoid(x):
    return 1.0 / (1.0 + jnp.exp(-x))


def _log_sigmoid_pair(z):
    l1p = jnp.log1p(jnp.exp(-jnp.abs(z)))
    return jnp.minimum(z, 0.0) - l1p, -jnp.maximum(z, 0.0) - l1p


def _softplus(z):
    return jnp.maximum(z, 0.0) + jnp.log(1.0 + jnp.exp(-jnp.abs(z)))


def _gelu(x):
    c = math.sqrt(2.0 / math.pi)
    h = 0.5 * x
    return h + h * jnp.tanh(x * (c + (c * 0.044715) * (x * x)))


def _iota(shape, dim):
    return lax.broadcasted_iota(jnp.int32, shape, dim)


def _rowsum_bcast(x):
    ones = jnp.ones((LANES, LANES), bf16)
    hi = x.astype(bf16)
    lo = (x - hi.astype(f32)).astype(bf16)
    return _dot(hi, ones) + _dot(lo, ones)


def _adaln_kernel(c_ref, w_ref, b_ref, o_ref):
    c = c_ref[...]
    s = c * _sigmoid(c)
    w = w_ref[0]
    s_hi = s.astype(bf16)
    s_lo = (s - s_hi.astype(f32)).astype(bf16)
    w_hi = w.astype(bf16)
    w_lo = (w - w_hi.astype(f32)).astype(bf16)
    o_ref[0] = _dot(s_hi, w_hi) + _dot(s_hi, w_lo) + _dot(s_lo, w_hi) + b_ref[0]


def adaln_all(c_all, w_ada, b_ada):
    depth, d, n = w_ada.shape
    rows = c_all.shape[0]
    tn = 1024
    return pl.pallas_call(
        _adaln_kernel,
        grid=(depth, n // tn),
        in_specs=[
            pl.BlockSpec((rows, d), lambda l, j: (0, 0)),
            pl.BlockSpec((1, d, tn), lambda l, j: (l, 0, j)),
            pl.BlockSpec((1, 1, tn), lambda l, j: (l, 0, j)),
        ],
        out_specs=pl.BlockSpec((1, rows, tn), lambda l, j: (l, 0, j)),
        out_shape=jax.ShapeDtypeStruct((depth, rows, n), f32),
        compiler_params=_cparams("arbitrary", "arbitrary"),
        name="adaln",
    )(c_all, w_ada, b_ada.reshape(depth, 1, n))


def _modulated(x, g, sh, sc):
    ms = jnp.mean(x * x, axis=-1, keepdims=True)
    y = x * lax.rsqrt(ms + EPS) * g
    return y * (1.0 + sc) + sh


def _modmm_kernel(x_ref, g_ref, sh_ref, sc_ref, w_ref, o_ref, xn_ref):
    @pl.when(pl.program_id(1) == 0)
    def _():
        xn_ref[...] = _modulated(x_ref[...], g_ref[...], sh_ref[0], sc_ref[0]).astype(bf16)

    o_ref[...] = _dot(xn_ref[...], w_ref[...])


def _modmm_gate_kernel(x_ref, g_ref, sh_ref, sc_ref, w_ref, wg_ref, o_ref, og_ref, xn_ref):
    @pl.when(pl.program_id(1) == 0)
    def _():
        xn = _modulated(x_ref[...], g_ref[...], sh_ref[0], sc_ref[0]).astype(bf16)
        xn_ref[...] = xn
        og_ref[...] = _dot(xn, wg_ref[...])

    o_ref[...] = _dot(xn_ref[...], w_ref[...])


def _mod_specs(mod, m, tm, k, rows_per_batch):
    if rows_per_batch >= tm:
        assert rows_per_batch % tm == 0
        per = rows_per_batch // tm
        return mod.reshape(-1, 1, k), pl.BlockSpec((1, 1, k), lambda i, j: (i // per, 0, 0))
    assert rows_per_batch == 1
    rows = mod
    if rows.shape[0] < m:
        rows = jnp.pad(rows, ((0, m - rows.shape[0]), (0, 0)))
    return rows.reshape(1, m, k), pl.BlockSpec((1, tm, k), lambda i, j: (0, i, 0))


def mod_matmul(x, g, shift, scale, w, n_out, rows_per_batch, tm, tn, w_gate=None, name="modmm"):
    m, k = x.shape
    sh, sh_spec = _mod_specs(shift, m, tm, k, rows_per_batch)
    sc, sc_spec = _mod_specs(scale, m, tm, k, rows_per_batch)
    in_specs = [
        pl.BlockSpec((tm, k), lambda i, j: (i, 0)),
        pl.BlockSpec((1, k), lambda i, j: (0, 0)),
        sh_spec,
        sc_spec,
        pl.BlockSpec((k, tn), lambda i, j: (0, j)),
    ]
    args = [x, g.reshape(1, k), sh, sc, w]
    out_specs = pl.BlockSpec((tm, tn), lambda i, j: (i, j))
    out_shape = jax.ShapeDtypeStruct((m, n_out), f32)
    kern = _modmm_kernel
    if w_gate is not None:
        in_specs.append(pl.BlockSpec((k, LANES), lambda i, j: (0, 0)))
        args.append(w_gate)
        out_specs = [out_specs, pl.BlockSpec((tm, LANES), lambda i, j: (i, 0))]
        out_shape = [out_shape, jax.ShapeDtypeStruct((m, LANES), f32)]
        kern = _modmm_gate_kernel
    return pl.pallas_call(
        kern,
        grid=(m // tm, n_out // tn),
        in_specs=in_specs,
        out_specs=out_specs,
        out_shape=out_shape,
        scratch_shapes=[pltpu.VMEM((tm, k), bf16)],
        compiler_params=_cparams("arbitrary", "arbitrary"),
        name=name,
    )(*args)


def _outproj_kernel(a1_ref, a2_ref, w1_ref, w2_ref, x_ref, gt_ref, o_ref):
    y = _dot(a1_ref[...].astype(bf16), w1_ref[...]) + _dot(a2_ref[...].astype(bf16), w2_ref[...])
    o_ref[...] = x_ref[...] + gt_ref[0] * y


def out_proj(a1, a2, w, x, gate, rows_per_batch, tm, tn, name="outproj"):
    m, k1 = a1.shape
    k2 = a2.shape[1]
    n = w.shape[1]
    gt, gt_spec = _mod_specs(gate, m, tm, n, rows_per_batch)
    if gt.shape[1] == 1:
        per = rows_per_batch // tm
        gt_spec = pl.BlockSpec((1, 1, tn), lambda i, j: (i // per, 0, j))
    else:
        gt_spec = pl.BlockSpec((1, tm, tn), lambda i, j: (0, i, j))
    return pl.pallas_call(
        _outproj_kernel,
        grid=(m // tm, n // tn),
        in_specs=[
            pl.BlockSpec((tm, k1), lambda i, j: (i, 0)),
            pl.BlockSpec((tm, k2), lambda i, j: (i, 0)),
            pl.BlockSpec((k1, tn), lambda i, j: (0, j)),
            pl.BlockSpec((k2, tn), lambda i, j: (k1 // k2, j)),
            pl.BlockSpec((tm, tn), lambda i, j: (i, j)),
            gt_spec,
        ],
        out_specs=pl.BlockSpec((tm, tn), lambda i, j: (i, j)),
        out_shape=jax.ShapeDtypeStruct((m, n), f32),
        compiler_params=_cparams("arbitrary", "arbitrary"),
        name=name,
    )(a1, a2, w, w, x, gt)


def _rglru_gates(xc, wr_ref, br, wi_ref, bi, lam):
    xb = xc.astype(bf16)
    rs, gs = [], []
    for hb in range(RG_BLOCKS):
        sl = slice(hb * LANES, (hb + 1) * LANES)
        rs.append(_dot(xb[:, sl], wr_ref[hb]))
        gs.append(_dot(xb[:, sl], wi_ref[hb]))
    r = _sigmoid(jnp.concatenate(rs, axis=1) + br)
    ig = _sigmoid(jnp.concatenate(gs, axis=1) + bi)
    softplus_neg_lam = jnp.maximum(-lam, 0.0) + jnp.log1p(jnp.exp(-jnp.abs(lam)))
    log_a = -RG_C * r * softplus_neg_lam
    a = jnp.exp(log_a)
    u = jnp.sqrt(-jnp.tanh(log_a) * (a * a + 1.0)) * ig * xc
    return a, u


def _rglru_kernel(xa_ref, ga_ref, cw_ref, cb_ref, wr_ref, br_ref, wi_ref, bi_ref, lam_ref,
                  ya_ref, hl_ref, xprev_ref, hc_ref):
    t_idx = pl.program_id(1)
    tt = xa_ref.shape[0]

    @pl.when(t_idx == 0)
    def _():
        xprev_ref[...] = jnp.zeros_like(xprev_ref)
        hc_ref[...] = jnp.zeros_like(hc_ref)

    xa = xa_ref[...]
    xprev = xprev_ref[...]
    row8 = _iota((8, RG_WIDTH), 0)
    xc = cb_ref[...] + cw_ref[CONV_W - 1:CONV_W, :] * xa
    for k in range(1, CONV_W):
        rolled = pltpu.roll(xa, k, 0)
        head = jnp.where(row8 < k, pltpu.roll(xprev, k, 0), rolled[0:8])
        shifted = jnp.concatenate([head, rolled[8:]], axis=0)
        xc = xc + cw_ref[CONV_W - 1 - k:CONV_W - k, :] * shifted
    xprev_ref[...] = xa[tt - 8:tt]

    a, u = _rglru_gates(xc, wr_ref, br_ref[...], wi_ref, bi_ref[...], lam_ref[...])
    row = _iota((tt, RG_WIDTH), 0)
    s = 1
    while s < tt:
        a_sh = pltpu.roll(a, s, 0)
        u_sh = pltpu.roll(u, s, 0)
        ok = row >= s
        u = jnp.where(ok, a * u_sh + u, u)
        a = jnp.where(ok, a * a_sh, a)
        s *= 2
    h = a * hc_ref[...] + u
    hc_ref[...] = h[tt - 1:tt]
    hl_ref[0] = h[tt - 1:tt]
    ya_ref[...] = h * _gelu(ga_ref[...])


def rglru_prompt(proj, bsz, seq, cw, cb, wr, br, wi, bi, lam, tt=256):
    nt = seq // tt
    vec = lambda: pl.BlockSpec((1, RG_WIDTH), lambda b, t: (0, 0))
    return pl.pallas_call(
        _rglru_kernel,
        grid=(bsz, nt),
        in_specs=[
            pl.BlockSpec((tt, RG_WIDTH), lambda b, t: (b * nt + t, 0)),
            pl.BlockSpec((tt, RG_WIDTH), lambda b, t: (b * nt + t, 1)),
            pl.BlockSpec((CONV_W, RG_WIDTH), lambda b, t: (0, 0)),
            vec(),
            pl.BlockSpec((RG_BLOCKS, LANES, LANES), lambda b, t: (0, 0, 0)),
            vec(),
            pl.BlockSpec((RG_BLOCKS, LANES, LANES), lambda b, t: (0, 0, 0)),
            vec(),
            vec(),
        ],
        out_specs=[
            pl.BlockSpec((tt, RG_WIDTH), lambda b, t: (b * nt + t, 0)),
            pl.BlockSpec((1, 1, RG_WIDTH), lambda b, t: (b, 0, 0)),
        ],
        out_shape=[
            jax.ShapeDtypeStruct((bsz * seq, RG_WIDTH), f32),
            jax.ShapeDtypeStruct((bsz, 1, RG_WIDTH), f32),
        ],
        scratch_shapes=[pltpu.VMEM((8, RG_WIDTH), f32), pltpu.VMEM((1, RG_WIDTH), f32)],
        compiler_params=_cparams("arbitrary", "arbitrary"),
        name="rglru_prompt",
    )(proj, proj, cw, cb, wr, br, wi, bi, lam)


def _rglru_step_kernel(xa_ref, ga_ref, b0_ref, b1_ref, b2_ref, h0_ref, cw_ref, cb_ref,
                       wr_ref, br_ref, wi_ref, bi_ref, lam_ref, ya_ref, h_ref):
    xa = xa_ref[...]
    xc = (cb_ref[...] + cw_ref[0:1, :] * b0_ref[...] + cw_ref[1:2, :] * b1_ref[...]
          + cw_ref[2:3, :] * b2_ref[...] + cw_ref[3:4, :] * xa)
    a, u = _rglru_gates(xc, wr_ref, br_ref[...], wi_ref, bi_ref[...], lam_ref[...])
    h = a * h0_ref[...] + u
    h_ref[...] = h
    ya_ref[...] = h * _gelu(ga_ref[...])


def rglru_step(proj_s, conv_state, h0, cw, cb, wr, br, wi, bi, lam):
    n = proj_s.shape[0]
    full = lambda shape: pl.BlockSpec(shape, lambda i: tuple(0 for _ in shape))
    return pl.pallas_call(
        _rglru_step_kernel,
        grid=(1,),
        in_specs=[
            pl.BlockSpec((n, RG_WIDTH), lambda i: (0, 0)),
            pl.BlockSpec((n, RG_WIDTH), lambda i: (0, 1)),
            full((n, RG_WIDTH)), full((n, RG_WIDTH)), full((n, RG_WIDTH)), full((n, RG_WIDTH)),
            full((CONV_W, RG_WIDTH)), full((1, RG_WIDTH)),
            full((RG_BLOCKS, LANES, LANES)), full((1, RG_WIDTH)),
            full((RG_BLOCKS, LANES, LANES)), full((1, RG_WIDTH)), full((1, RG_WIDTH)),
        ],
        out_specs=[full((n, RG_WIDTH)), full((n, RG_WIDTH))],
        out_shape=[jax.ShapeDtypeStruct((n, RG_WIDTH), f32)] * 2,
        compiler_params=_cparams("arbitrary"),
        name="rglru_step",
    )(proj_s, proj_s, conv_state[:, 0], conv_state[:, 1], conv_state[:, 2], h0,
      cw, cb, wr, br, wi, bi, lam)


def _rope_tables(pos):
    half = ROT_DIMS // 2
    inv = ROPE_THETA ** (-jnp.arange(half, dtype=f32) / half)
    ang = pos.astype(f32)[:, None] * inv[None, :]
    cos, sin = jnp.cos(ang), jnp.sin(ang)
    n = pos.shape[0]
    ctab = jnp.concatenate([cos, cos, jnp.ones((n, DH - ROT_DIMS), f32)], axis=1)
    stab = jnp.concatenate([-sin, sin, jnp.zeros((n, DH - ROT_DIMS), f32)], axis=1)
    return ctab, stab


def _rope_head(xh, ctab, stab, lane):
    half = ROT_DIMS // 2
    partner = jnp.where(lane < half, pltpu.roll(xh, DH - half, 1), pltpu.roll(xh, half, 1))
    return xh * ctab + partner * stab


def _rope_kernel(q_ref, k_ref, v_ref, c_ref, s_ref, qo_ref, ko_ref, vo_ref):
    ctab, stab = c_ref[...], s_ref[...]
    lane = _iota((q_ref.shape[0], DH), 1)
    for hb in range(HEADS):
        sl = slice(hb * DH, (hb + 1) * DH)
        qo_ref[:, hb, :] = _rope_head(q_ref[:, sl], ctab, stab, lane) * (DH ** -0.5)
        ko_ref[:, hb, :] = _rope_head(k_ref[:, sl], ctab, stab, lane)
        vo_ref[:, hb, :] = v_ref[:, sl]


def rope_qkv(proj, ctab, stab, rows_per_seq, tt):
    m = proj.shape[0]
    nt = max(rows_per_seq // tt, 1)
    if ctab.shape[0] == 1:
        tab_spec = pl.BlockSpec((1, DH), lambda i: (0, 0))
    else:
        tab_spec = pl.BlockSpec((tt, DH), lambda i: (i % nt, 0))
    return pl.pallas_call(
        _rope_kernel,
        grid=(m // tt,),
        in_specs=[
            pl.BlockSpec((tt, ATT_W), lambda i: (i, 2)),
            pl.BlockSpec((tt, ATT_W), lambda i: (i, 3)),
            pl.BlockSpec((tt, ATT_W), lambda i: (i, 4)),
            tab_spec, tab_spec,
        ],
        out_specs=[pl.BlockSpec((tt, HEADS, DH), lambda i: (i, 0, 0))] * 3,
        out_shape=[jax.ShapeDtypeStruct((m, HEADS, DH), f32)] * 3,
        compiler_params=_cparams("arbitrary"),
        name="rope",
    )(proj, proj, proj, ctab, stab)


def _rope_split_kernel(q_ref, k_ref, v_ref, c_ref, s_ref, *refs):
    n_pat = len(DIL_PATTERNS)
    outs, (kf_ref, qs_ref, ks_ref) = refs[:3 * n_pat], refs[3 * n_pat:]
    tt = q_ref.shape[0]
    ctab, stab = c_ref[...], s_ref[...]
    lane = _iota((tt, DH), 1)
    qs_ref[...] = _rope_head(q_ref[...], ctab, stab, lane) * (DH ** -0.5)
    k_rot = _rope_head(k_ref[...], ctab, stab, lane)
    ks_ref[...] = k_rot
    kf_ref[...] = k_rot
    for gi, (_, d) in enumerate(DIL_PATTERNS):
        rows = tt // d
        for src, dst in ((qs_ref, outs[3 * gi]), (ks_ref, outs[3 * gi + 1]), (v_ref, outs[3 * gi + 2])):
            for r in range(d):
                dst[0, 0, r] = src[pl.ds(r, rows, stride=d), :].astype(bf16)


def rope_split(proj, ctab, stab, bsz, seq, tt=1024):
    m = proj.shape[0]
    nt = seq // tt
    col = lambda c: pl.BlockSpec((tt, DH), lambda i, h, c=c: (i, c * HEADS + h))
    tab = pl.BlockSpec((tt, DH), lambda i, h: (i % nt, 0))
    out_specs, out_shape = [], []
    for (_, d) in DIL_PATTERNS:
        for _ in range(3):
            out_specs.append(pl.BlockSpec((1, 1, d, tt // d, DH), lambda i, h: (i // nt, h, 0, i % nt, 0)))
            out_shape.append(jax.ShapeDtypeStruct((bsz, HEADS, d, seq // d, DH), bf16))
    out_specs.append(pl.BlockSpec((tt, DH), lambda i, h: (i, h)))
    out_shape.append(jax.ShapeDtypeStruct((m, ATT_W), f32))
    res = pl.pallas_call(
        _rope_split_kernel,
        grid=(m // tt, HEADS),
        in_specs=[col(2), col(3), col(4), tab, tab],
        out_specs=out_specs,
        out_shape=out_shape,
        scratch_shapes=[pltpu.VMEM((tt, DH), f32), pltpu.VMEM((tt, DH), f32)],
        compiler_params=_cparams("arbitrary", "arbitrary"),
        name="rope_split",
    )(proj, proj, proj, ctab, stab)
    return [res[3 * gi:3 * gi + 3] for gi in range(len(DIL_PATTERNS))], res[-1]


def _dil_kernel(q_ref, kc_ref, kp_ref, vc_ref, vp_ref, o_ref, l_ref, s_ref, p_ref, *, span):
    tq = q_ref.shape[3]
    blk = pl.program_id(2)
    qi = _iota((tq, 2 * tq), 0)
    col = _iota((tq, 2 * tq), 1)
    rel = jnp.where(col < tq, qi - col, qi - col + 2 * tq)
    ok = (rel >= 0) & (rel <= span) & ((col < tq) | (blk > 0))
    for hb in range(HEADS):
        qh = q_ref[0, hb, 0]
        s_ref[hb, :, :tq] = _dot_nt(qh, kc_ref[0, hb, 0])
        s_ref[hb, :, tq:] = _dot_nt(qh, kp_ref[0, hb, 0])
    for hb in range(HEADS):
        s = jnp.where(ok, s_ref[hb], NEG_INF)
        mx = jnp.max(s, axis=1, keepdims=True)
        p = jnp.exp(s - mx)
        den = jnp.sum(p, axis=1, keepdims=True)
        p_ref[hb] = (p * (1.0 / den)).astype(bf16)
        l_ref[0, hb, 0] = jnp.broadcast_to(mx + jnp.log(den), (tq, DH))
    for hb in range(HEADS):
        o_ref[0, hb, 0] = (_dot(p_ref[hb, :, :tq], vc_ref[0, hb, 0])
                           + _dot(p_ref[hb, :, tq:], vp_ref[0, hb, 0]))


def _dil_merge_kernel(*refs):
    n_pat = len(DIL_PATTERNS)
    ins, o_ref, scr = refs[:2 * n_pat], refs[2 * n_pat], refs[2 * n_pat + 1:]
    tt = o_ref.shape[0]
    for gi, (_, d) in enumerate(DIL_PATTERNS):
        rows = tt // d
        for src, dst in ((ins[2 * gi], scr[2 * gi]), (ins[2 * gi + 1], scr[2 * gi + 1])):
            for r in range(d):
                dst[pl.ds(r, rows, stride=d), :] = src[0, 0, r]
    lses = [scr[2 * gi + 1][...] for gi in range(n_pat)]
    top = functools.reduce(jnp.maximum, lses)
    es = [jnp.exp(l - top) for l in lses]
    num = sum(scr[2 * gi][...] * es[gi] for gi in range(n_pat))
    o_ref[...] = num / sum(es)


def dilated_prompt(qkv_by_pattern, bsz, seq, tq=128, tt=1024):
    partial = []
    for (w, d), (qd, kd, vd) in zip(DIL_PATTERNS, qkv_by_pattern):
        sd = seq // d
        blk = (1, HEADS, 1, tq, DH)
        cur = pl.BlockSpec(blk, lambda b, r, i: (b, 0, r, i, 0))
        prev = pl.BlockSpec(blk, lambda b, r, i: (b, 0, r, jnp.maximum(i - 1, 0), 0))
        partial += pl.pallas_call(
            functools.partial(_dil_kernel, span=w // d),
            grid=(bsz, d, sd // tq),
            in_specs=[cur, cur, prev, cur, prev],
            out_specs=[cur, cur],
            out_shape=[jax.ShapeDtypeStruct((bsz, HEADS, d, sd, DH), f32)] * 2,
            scratch_shapes=[pltpu.VMEM((HEADS, tq, 2 * tq), f32), pltpu.VMEM((HEADS, tq, 2 * tq), bf16)],
            compiler_params=_cparams("arbitrary", "arbitrary", "arbitrary"),
            name=f"dilattn_d{d}",
        )(qd, kd, kd, vd, vd)
    nt = seq // tt
    in_specs = []
    for (_, d) in DIL_PATTERNS:
        in_specs += [pl.BlockSpec((1, 1, d, tt // d, DH), lambda i, h: (i // nt, h, 0, i % nt, 0))] * 2
    return pl.pallas_call(
        _dil_merge_kernel,
        grid=(bsz * nt, HEADS),
        in_specs=in_specs,
        out_specs=pl.BlockSpec((tt, DH), lambda i, h: (i, h)),
        out_shape=jax.ShapeDtypeStruct((bsz * seq, ATT_W), f32),
        scratch_shapes=[pltpu.VMEM((tt, DH), f32)] * (2 * len(DIL_PATTERNS)),
        compiler_params=_cparams("arbitrary", "arbitrary"),
        name="dilattn_merge",
    )(*partial)


def _dil_step_kernel(q_ref, kn_ref, vn_ref, k1_ref, k4_ref, k16_ref, v1_ref, v4_ref, v16_ref, o_ref):
    q = q_ref[0]
    kn, vn = kn_ref[0], vn_ref[0]
    s_self = _rowsum_bcast(q * kn)
    o_gs, lse_gs = [], []
    for k_ref, v_ref in ((k1_ref, v1_ref), (k4_ref, v4_ref), (k16_ref, v16_ref)):
        k3 = k_ref[0, :, 0]
        nk = k3.shape[0]
        s = _rowsum_bcast((k3 * q[None]).reshape(nk * HEADS, DH)).reshape(nk, HEADS, DH)
        mx = jnp.maximum(jnp.max(s, axis=0), s_self)
        p = jnp.exp(s - mx[None])
        p_self = jnp.exp(s_self - mx)
        den = jnp.sum(p, axis=0) + p_self
        num = jnp.sum(p * v_ref[0, :, 0], axis=0) + p_self * vn
        o_gs.append(num / den)
        lse_gs.append(mx + jnp.log(den))
    top = jnp.maximum(jnp.maximum(lse_gs[0], lse_gs[1]), lse_gs[2])
    es = [jnp.exp(l - top) for l in lse_gs]
    tot = es[0] + es[1] + es[2]
    o_ref[0] = (o_gs[0] * es[0] + o_gs[1] * es[1] + o_gs[2] * es[2]) / tot


def dilated_step(q4, k4, v4, cache_k, cache_v):
    n, win = cache_k.shape[0], cache_k.shape[1]
    one = pl.BlockSpec((1, HEADS, DH), lambda b: (b, 0, 0))
    args = [q4, k4, v4]
    in_specs = [one, one, one]
    for cache in (cache_k, cache_v):
        for (w, d) in DIL_PATTERNS:
            nkeys = w // d
            assert win % d == 0 and (win // d) % nkeys == 0 and (win - w) % (d * nkeys) == 0
            args.append(cache.reshape(n, win // d, d, HEADS, DH))
            in_specs.append(pl.BlockSpec((1, nkeys, 1, HEADS, DH),
                                         lambda b, blk=(win - w) // d // nkeys: (b, blk, 0, 0, 0)))
    return pl.pallas_call(
        _dil_step_kernel,
        grid=(n,),
        in_specs=in_specs,
        out_specs=one,
        out_shape=jax.ShapeDtypeStruct((n, HEADS, DH), f32),
        compiler_params=_cparams("arbitrary"),
        name="dilattn_step",
    )(*args).reshape(n, ATT_W)


def _sb_kernel(bias_ref, q_ref, k_ref, v_ref, o_ref, kb_ref, vb_ref, qs_ref, t_ref, spb_ref, wb_ref,
               acc_ref, run_ref):
    tq = q_ref.shape[1]
    nh = q_ref.shape[2] // DH
    kt_w = t_ref.shape[2]
    sub = PAGE
    hg = pl.program_id(1)
    qb = pl.program_id(2)

    @pl.when(qb == 0)
    def _():
        kb_ref[...] = k_ref[0].astype(bf16)
        vb_ref[...] = v_ref[0].astype(bf16)

    rr = _iota((sub, 2 * sub), 0)
    cc = _iota((sub, 2 * sub), 1)
    tri = jnp.where((rr > cc) | (cc >= sub), 1.0, 0.0).astype(bf16)
    for h in range(nh):
        qs_ref[h] = (q_ref[0, :, h * DH:(h + 1) * DH] * (DH ** -0.5)).astype(bf16)
    acc_ref[...] = jnp.zeros_like(acc_ref)
    run_ref[...] = jnp.zeros_like(run_ref)

    def macro(start, masked):
        for h in range(nh):
            hs = slice(h * DH, (h + 1) * DH)
            z = _dot_nt(qs_ref[h], kb_ref[pl.ds(start, kt_w), hs]) + bias_ref[hg * nh + h]
            sp = _softplus(z)
            t = z - sp
            if masked:
                ok = start + _iota((tq, kt_w), 1) < qb * tq + _iota((tq, kt_w), 0)
                sp = jnp.where(ok, sp, 0.0)
                t = jnp.where(ok, t, NEG_INF)
            t_ref[h] = t
            spb_ref[h] = sp.astype(bf16)
        for h in range(nh):
            run = run_ref[h]
            for kt in range(kt_w // sub - 1, -1, -1):
                ks = slice(kt * sub, (kt + 1) * sub)
                cs = _dot(spb_ref[h, :, ks], tri)
                wb_ref[h, :, ks] = jnp.exp(t_ref[h, :, ks] - cs[:, :sub] - run).astype(bf16)
                run = run + cs[:, sub:]
            run_ref[h] = run
        for h in range(nh):
            hs = slice(h * DH, (h + 1) * DH)
            acc_ref[h] += _dot(wb_ref[h], vb_ref[pl.ds(start, kt_w), hs])

    top = (qb * tq) // kt_w
    macro(pl.multiple_of(top * kt_w, kt_w), True)

    def body(it, _):
        macro(pl.multiple_of((top - 1 - it) * kt_w, kt_w), False)
        return 0

    lax.fori_loop(0, top, body, 0)
    for h in range(nh):
        o_ref[0, :, h * DH:(h + 1) * DH] = acc_ref[h]


def sb_prompt(proj2, bias, bsz, seq, tq=256, nh=4, kt_w=512):
    assert kt_w % tq == 0 and seq % kt_w == 0
    p3 = proj2.reshape(bsz, seq, proj2.shape[1])
    nq = seq // tq
    ng = HEADS // nh
    wd = nh * DH
    return pl.pallas_call(
        _sb_kernel,
        grid=(bsz, ng, nq),
        in_specs=[
            pl.BlockSpec(memory_space=pltpu.SMEM),
            pl.BlockSpec((1, tq, wd), lambda b, g, i: (b, i, g)),
            pl.BlockSpec((1, seq, wd), lambda b, g, i: (b, 0, ng + g)),
            pl.BlockSpec((1, seq, wd), lambda b, g, i: (b, 0, 2 * ng + g)),
        ],
        out_specs=pl.BlockSpec((1, tq, wd), lambda b, g, i: (b, i, g)),
        out_shape=jax.ShapeDtypeStruct((bsz, seq, ATT_W), f32),
        scratch_shapes=[
            pltpu.VMEM((seq, wd), bf16), pltpu.VMEM((seq, wd), bf16),
            pltpu.VMEM((nh, tq, DH), bf16),
            pltpu.VMEM((nh, tq, kt_w), f32), pltpu.VMEM((nh, tq, kt_w), bf16),
            pltpu.VMEM((nh, tq, kt_w), bf16),
            pltpu.VMEM((nh, tq, DH), f32), pltpu.VMEM((nh, tq, PAGE), f32),
        ],
        compiler_params=_cparams("arbitrary", "arbitrary", "arbitrary"),
        name="sb_prompt",
    )(bias, p3, p3, p3).reshape(bsz * seq, ATT_W)


def _sb_step_kernel(pt_ref, q_ref, bias_ref, eye_ref, *refs, npg):
    k_refs, v_refs = refs[:npg], refs[npg:2 * npg]
    o_ref, acc_ref, carry_ref = refs[2 * npg:]
    j = pl.program_id(1)

    @pl.when(j == 0)
    def _():
        acc_ref[...] = jnp.zeros_like(acc_ref)
        carry_ref[...] = jnp.zeros_like(carry_ref)

    q = q_ref[0]
    bias = bias_ref[...]
    ones = jnp.ones((DH, DH), bf16)
    rr = _iota((PAGE, 2 * PAGE), 0)
    cc = _iota((PAGE, 2 * PAGE), 1)
    tri = jnp.where((rr > cc) | (cc >= PAGE), 1.0, 0.0).astype(bf16)
    acc = acc_ref[...]
    run = carry_ref[...]
    pages = range(npg)
    zbs = [_dot((k_refs[p][0] * q[None]).reshape(PAGE * HEADS, DH).astype(bf16), ones)
           .reshape(PAGE, HEADS, DH) for p in pages]
    zs = [jnp.sum(zb * eye_ref[...], axis=0) + bias for zb in zbs]
    sps = [_softplus(z) for z in zs]
    css = []
    for sp in sps:
        hi, mid, lo = _split3(sp)
        css.append(_dot(hi, tri) + _dot(mid, tri) + _dot(lo, tri))
    ws = []
    for z, sp, cs in zip(zs, sps, css):
        ws.append(jnp.exp(z - sp - cs[:, :PAGE] - run))
        run = run + cs[:, PAGE:]
    wbs = [_dot((eye_ref[...] * w[None]).reshape(PAGE * HEADS, DH).astype(bf16), ones)
           .reshape(PAGE, HEADS, DH) for w in ws]
    for p, wb in zip(pages, wbs):
        acc = acc + jnp.sum(wb * v_refs[p][0], axis=0)
    acc_ref[...] = acc
    carry_ref[...] = run

    @pl.when(j == pl.num_programs(1) - 1)
    def _():
        o_ref[0] = acc


def sb_step(q_s, bias, cache_k, cache_v, page_table, npg=8):
    n, n_pages = page_table.shape
    q4 = (q_s * (DH ** -0.5)).reshape(n, HEADS, DH)
    bias4 = jnp.broadcast_to(bias[:, None], (HEADS, DH))
    eye3 = jnp.broadcast_to(jnp.eye(PAGE, DH, dtype=f32)[:, None, :], (PAGE, HEADS, DH))

    def page(p):
        return lambda b, j, pt: (pt[b, n_pages - 1 - (j * npg + p)], 0, 0, 0)

    kv_specs = [pl.BlockSpec((1, PAGE, HEADS, DH), page(p)) for p in range(npg)]
    grid_spec = pltpu.PrefetchScalarGridSpec(
        num_scalar_prefetch=1,
        grid=(n, n_pages // npg),
        in_specs=[
            pl.BlockSpec((1, HEADS, DH), lambda b, j, pt: (b, 0, 0)),
            pl.BlockSpec((HEADS, DH), lambda b, j, pt: (0, 0)),
            pl.BlockSpec((PAGE, HEADS, DH), lambda b, j, pt: (0, 0, 0)),
        ] + kv_specs + kv_specs,
        out_specs=pl.BlockSpec((1, HEADS, DH), lambda b, j, pt: (b, 0, 0)),
        scratch_shapes=[pltpu.VMEM((HEADS, DH), f32), pltpu.VMEM((HEADS, DH), f32)],
    )
    return pl.pallas_call(
        functools.partial(_sb_step_kernel, npg=npg),
        grid_spec=grid_spec,
        out_shape=jax.ShapeDtypeStruct((n, HEADS, DH), f32),
        compiler_params=_cparams("arbitrary", "arbitrary"),
        name="sb_step",
    )(page_table, q4, bias4, eye3, *([cache_k] * npg), *([cache_v] * npg)).reshape(n, ATT_W)


def _mlstm_kernel(q_ref, k_ref, v_ref, og_ref, gate_ref, gb_ref, gn_ref,
                  h_ref, c_out, n_out, m_out, c_s, n_s, m_s):
    ci = pl.program_id(1)
    L = CHUNK

    @pl.when(ci == 0)
    def _():
        c_s[...] = jnp.zeros_like(c_s)
        n_s[...] = jnp.zeros_like(n_s)
        m_s[...] = jnp.zeros_like(m_s)

    gt = gate_ref[...] + gb_ref[...]
    gt_t = gt.T
    ri = _iota((L, L), 0)
    li = _iota((L, L), 1)
    causal = li <= ri
    tri_incl = jnp.where(causal, 1.0, 0.0).astype(bf16)
    tri_incl_t = jnp.where(ri <= li, 1.0, 0.0).astype(bf16)
    for h in range(M_HEADS):
        sl = slice(h * M_DH, (h + 1) * M_DH)
        ig_col = gt[:, h:h + 1]
        ig_row = gt_t[h:h + 1, :]
        lf_col = _log_sigmoid_pair(gt[:, M_HEADS + h:M_HEADS + h + 1])[0]
        lf_row = _log_sigmoid_pair(gt_t[M_HEADS + h:M_HEADS + h + 1, :])[0]
        c_hi, c_mid, c_lo = _split3(jnp.broadcast_to(lf_col, (L, L)))
        bcum_col = _dot(tri_incl, c_hi) + _dot(tri_incl, c_mid) + _dot(tri_incl, c_lo)
        r_hi, r_mid, r_lo = _split3(jnp.broadcast_to(lf_row, (L, L)))
        bcum_row = _dot(r_hi, tri_incl_t) + _dot(r_mid, tri_incl_t) + _dot(r_lo, tri_incl_t)
        m_prev = m_s[h:h + 1, :]
        dlog = jnp.where(causal, bcum_col - bcum_row + ig_row, NEG_INF)
        inter = bcum_col + m_prev
        m_t = jnp.maximum(inter, jnp.max(dlog, axis=1, keepdims=True))
        dw = jnp.exp(dlog - m_t)
        iw = jnp.exp(inter - m_t)
        qh = q_ref[:, sl]
        kh = k_ref[:, sl] * (M_DH ** -0.5)
        vh = v_ref[:, sl]
        qb, kb, vb = qh.astype(bf16), kh.astype(bf16), vh.astype(bf16)
        sw = dw * _dot_nt(qb, kb)
        c_prev = c_s[h]
        n_prev = n_s[h:h + 1, :]
        iw_col = iw[:, 0:1]
        num = _dot(sw.astype(bf16), vb) + iw_col * _dot_nt(qb, c_prev.astype(bf16))
        qn = jnp.sum(qb.astype(f32) * n_prev.astype(bf16).astype(f32), axis=1, keepdims=True)
        den = jnp.sum(sw, axis=1, keepdims=True) + iw_col * qn
        m_col = m_t[:, 0:1]
        hout = num / jnp.maximum(jnp.abs(den), jnp.exp(-m_col))
        m_last = m_t[L - 1:L, :]
        b_last = bcum_col[L - 1:L, :]
        wl_col = jnp.exp(b_last[:, 0:1] - bcum_col[:, 0:1] + ig_col - m_last[:, 0:1])
        wl_row = jnp.exp(b_last - bcum_row[0:1, :] + ig_row - m_last)
        decay = jnp.exp(b_last + m_prev - m_last)
        dsc = decay[:, 0:1]
        c_s[h] = dsc * c_prev + _dot((vh * wl_col).T.astype(bf16), kb)
        wl8 = jnp.broadcast_to(wl_row, (8, L)).astype(bf16)
        n_s[h:h + 1, :] = dsc * n_prev + _dot(wl8, kb)[0:1, :]
        m_s[h:h + 1, :] = m_last
        hn = hout * lax.rsqrt(jnp.mean(hout * hout, axis=1, keepdims=True) + EPS)
        h_ref[:, sl] = hn * gn_ref[:, sl] * _sigmoid(og_ref[:, sl])

    @pl.when(ci == pl.num_programs(1) - 1)
    def _():
        c_out[0] = c_s[...]
        n_out[0] = n_s[...]
        m_out[0] = m_s[...]


def mlstm_prompt(proj2, gates, gate_bias, g_mnorm, bsz, seq):
    nc = seq // CHUNK
    col = lambda c: pl.BlockSpec((CHUNK, M_WIDTH), lambda b, i, c=c: (b * nc + i, c))
    hm, c1, n1, m1 = pl.pallas_call(
        _mlstm_kernel,
        grid=(bsz, nc),
        in_specs=[
            col(3), col(4), col(5), col(6),
            pl.BlockSpec((CHUNK, LANES), lambda b, i: (b * nc + i, 0)),
            pl.BlockSpec((1, LANES), lambda b, i: (0, 0)),
            pl.BlockSpec((1, M_WIDTH), lambda b, i: (0, 0)),
        ],
        out_specs=[
            pl.BlockSpec((CHUNK, M_WIDTH), lambda b, i: (b * nc + i, 0)),
            pl.BlockSpec((1, M_HEADS, M_DH, M_DH), lambda b, i: (b, 0, 0, 0)),
            pl.BlockSpec((1, M_HEADS, M_DH), lambda b, i: (b, 0, 0)),
            pl.BlockSpec((1, M_HEADS, LANES), lambda b, i: (b, 0, 0)),
        ],
        out_shape=[
            jax.ShapeDtypeStruct((bsz * seq, M_WIDTH), f32),
            jax.ShapeDtypeStruct((bsz, M_HEADS, M_DH, M_DH), f32),
            jax.ShapeDtypeStruct((bsz, M_HEADS, M_DH), f32),
            jax.ShapeDtypeStruct((bsz, M_HEADS, LANES), f32),
        ],
        scratch_shapes=[
            pltpu.VMEM((M_HEADS, M_DH, M_DH), f32),
            pltpu.VMEM((M_HEADS, M_DH), f32),
            pltpu.VMEM((M_HEADS, LANES), f32),
        ],
        compiler_params=_cparams("arbitrary", "arbitrary"),
        name="mlstm_prompt",
    )(proj2, proj2, proj2, proj2, gates, gate_bias, g_mnorm)
    return hm, c1, n1, m1[:, :, 0]


def _mlstm_step_kernel(q_ref, k_ref, v_ref, og_ref, gn_ref, ig_ref, fg_ref, c_ref, n_ref, m_ref,
                       h_ref, c_out, n_out, m_out):
    q = q_ref[0, 0]
    k = k_ref[0, 0] * (M_DH ** -0.5)
    v = v_ref[0, 0]
    ig = ig_ref[0, 0]
    lf = _log_sigmoid_pair(fg_ref[0, 0])[0]
    m0 = m_ref[0, 0]
    c0 = c_ref[0, 0]
    n0 = n_ref[0, 0]
    inter = lf + m0
    m_t = jnp.maximum(inter, ig)
    dw = jnp.exp(ig - m_t)
    iw = jnp.exp(inter - m_t)
    rnd = lambda a: a.astype(bf16).astype(f32)
    qr, kr, vr = rnd(q), rnd(k), rnd(v)
    qk = jnp.sum(qr * kr, axis=1, keepdims=True)
    sw = dw * qk
    cq = jnp.sum(rnd(c0) * qr, axis=1, keepdims=True)
    num = rnd(sw) * vr + iw * cq
    den = sw + iw * jnp.sum(rnd(n0) * qr, axis=1, keepdims=True)
    hout = num / jnp.maximum(jnp.abs(den), jnp.exp(-m_t))
    c_out[0, 0] = iw * c0 + rnd(dw * v) * kr
    n_out[0, 0] = iw * n0 + rnd(dw) * kr
    m_out[0, 0] = m_t
    hn = hout * lax.rsqrt(jnp.mean(hout * hout, axis=0, keepdims=True) + EPS)
    h_ref[0, 0] = hn * gn_ref[0] * _sigmoid(og_ref[0, 0])


def mlstm_step(proj2_s, gates_s, gate_bias, g_mnorm, c0, n0, m0):
    n = proj2_s.shape[0]
    base = 3 * ATT_W
    seg = lambda i: proj2_s[:, base + i * M_WIDTH: base + (i + 1) * M_WIDTH].reshape(n, M_HEADS, M_DH)
    g = gates_s + gate_bias
    rowb = pl.BlockSpec((1, 1, 1, M_DH), lambda b, h: (b, h, 0, 0))
    colb = pl.BlockSpec((1, 1, M_DH, 1), lambda b, h: (b, h, 0, 0))
    scal = pl.BlockSpec((1, 1, 1, 1), lambda b, h: (b, h, 0, 0))
    hcol, c1, n1, m1 = pl.pallas_call(
        _mlstm_step_kernel,
        grid=(n, M_HEADS),
        in_specs=[
            rowb, rowb, colb, colb,
            pl.BlockSpec((1, M_DH, 1), lambda b, h: (h, 0, 0)),
            scal, scal,
            pl.BlockSpec((1, 1, M_DH, M_DH), lambda b, h: (b, h, 0, 0)),
            rowb, scal,
        ],
        out_specs=[colb, pl.BlockSpec((1, 1, M_DH, M_DH), lambda b, h: (b, h, 0, 0)), rowb, scal],
        out_shape=[
            jax.ShapeDtypeStruct((n, M_HEADS, M_DH, 1), f32),
            jax.ShapeDtypeStruct((n, M_HEADS, M_DH, M_DH), f32),
            jax.ShapeDtypeStruct((n, M_HEADS, 1, M_DH), f32),
            jax.ShapeDtypeStruct((n, M_HEADS, 1, 1), f32),
        ],
        compiler_params=_cparams("arbitrary", "arbitrary"),
        name="mlstm_step",
    )(seg(0)[:, :, None, :], seg(1)[:, :, None, :], seg(2)[..., None], seg(3)[..., None],
      g_mnorm.reshape(M_HEADS, M_DH, 1),
      g[:, 0:M_HEADS].reshape(n, M_HEADS, 1, 1), g[:, M_HEADS:2 * M_HEADS].reshape(n, M_HEADS, 1, 1),
      c0, n0[:, :, None, :], m0.reshape(n, M_HEADS, 1, 1))
    return hcol.reshape(n, M_WIDTH), c1, n1.reshape(n, M_HEADS, M_DH), m1.reshape(n, M_HEADS)


def _peerq_kernel(x_ref, g_ref, sh_ref, sc_ref, wt_ref, qt_ref, xmt_ref, xs_ref):
    @pl.when(pl.program_id(1) == 0)
    def _():
        xm = _modulated(x_ref[...], g_ref[...], sh_ref[0], sc_ref[0])
        xt = xm.T.astype(bf16)
        xs_ref[...] = xt
        xmt_ref[...] = xt

    qt_ref[...] = _dot(wt_ref[...], xs_ref[...])


def peer_query(x, g, shift, scale, wq_t, rows_per_batch, tm, tn=512):
    m, k = x.shape
    n = wq_t.shape[0]
    sh, sh_spec = _mod_specs(shift, m, tm, k, rows_per_batch)
    sc, sc_spec = _mod_specs(scale, m, tm, k, rows_per_batch)
    return pl.pallas_call(
        _peerq_kernel,
        grid=(m // tm, n // tn),
        in_specs=[
            pl.BlockSpec((tm, k), lambda i, j: (i, 0)),
            pl.BlockSpec((1, k), lambda i, j: (0, 0)),
            sh_spec, sc_spec,
            pl.BlockSpec((tn, k), lambda i, j: (j, 0)),
        ],
        out_specs=[pl.BlockSpec((tn, tm), lambda i, j: (j, i)),
                   pl.BlockSpec((k, tm), lambda i, j: (0, i))],
        out_shape=[jax.ShapeDtypeStruct((n, m), f32), jax.ShapeDtypeStruct((k, m), bf16)],
        scratch_shapes=[pltpu.VMEM((k, tm), bf16)],
        compiler_params=_cparams("arbitrary", "arbitrary"),
        name="peer_query",
    )(x, g.reshape(1, k), sh, sc, wq_t)


def _heads_layout_kernel(*refs):
    n = len(refs) // 2
    for x_ref, o_ref in zip(refs[:n], refs[n:]):
        for hb in range(HEADS):
            o_ref[:, hb, :] = x_ref[:, hb * DH:(hb + 1) * DH]


def heads_layout(srcs, bsz, seq, tail, tt=512):
    nt, first = tail // tt, (seq - tail) // tt
    rows = lambda b, i: b * (seq // tt) + first + i
    return pl.pallas_call(
        _heads_layout_kernel,
        grid=(bsz, nt),
        in_specs=[pl.BlockSpec((tt, ATT_W), lambda b, i, c=c: (rows(b, i), c)) for _, c in srcs],
        out_specs=[pl.BlockSpec((tt, HEADS, DH), lambda b, i: (b * nt + i, 0, 0))] * len(srcs),
        out_shape=[jax.ShapeDtypeStruct((bsz * tail, HEADS, DH), f32)] * len(srcs),
        compiler_params=_cparams("arbitrary", "arbitrary"),
        name="heads_layout",
    )(*[x for x, _ in srcs])


def _cast_kernel(w_ref, o_ref):
    o_ref[...] = w_ref[0].astype(bf16)


def cast_layer_bf16(w, layer, tr=1024):
    _, rows, cols = w.shape
    return pl.pallas_call(
        _cast_kernel,
        grid=(rows // tr,),
        in_specs=[pl.BlockSpec((1, tr, cols), lambda i: (layer, i, 0))],
        out_specs=pl.BlockSpec((tr, cols), lambda i: (i, 0)),
        out_shape=jax.ShapeDtypeStruct((rows, cols), bf16),
        compiler_params=_cparams("arbitrary"),
        name="cast_bf16",
    )(w)


_N_TOP = PEER_TOPK + 1
_CAND_PAIRS = [(a, b) for a in range(_N_TOP) for b in range(_N_TOP) if (a + 1) * (b + 1) <= _N_TOP]
_N_CAND = -(-len(_CAND_PAIRS) // 8) * 8


def _extract_top(cur, ridx, n):
    vals = []
    big = float(cur.shape[0])
    for _ in range(n):
        mx = jnp.max(cur, axis=0, keepdims=True)
        first = jnp.min(jnp.where(cur == mx, ridx, big), axis=0, keepdims=True)
        cur = jnp.where(ridx == first, NEG_INF, cur)
        vals.append(mx)
    return vals


def _oddeven_mergesort_pairs(n):
    pairs = []

    def merge(lo, cnt, r):
        step = 2 * r
        if step < cnt:
            merge(lo, cnt, step)
            merge(lo + r, cnt, step)
            pairs.extend((i, i + r) for i in range(lo + r, lo + cnt - r, step))
        else:
            pairs.append((lo, lo + r))

    def sort(lo, cnt):
        if cnt > 1:
            sort(lo, cnt // 2)
            sort(lo + cnt // 2, cnt // 2)
            merge(lo, cnt, 1)

    sort(0, n)
    return pairs


_SORT16 = _oddeven_mergesort_pairs(PEER_TOPK)


def _exchange(y, i, j):
    y[i], y[j] = jnp.maximum(y[i], y[j]), jnp.minimum(y[i], y[j])


def _top17_network(s):
    n = PEER_TOPK
    y = [s[v * 8:(v + 1) * 8, :] for v in range(n)]
    for i, j in _SORT16:
        _exchange(y, i, j)
    dropped = None
    for shift in (4, 2, 1):
        other = [pltpu.roll(y[n - 1 - i], shift, 0) for i in range(n)]
        low = functools.reduce(jnp.maximum, [jnp.minimum(y[i], other[i]) for i in range(n)])
        dropped = low if dropped is None else jnp.maximum(dropped, low)
        y = [jnp.maximum(y[i], other[i]) for i in range(n)]
        dist = n // 2
        while dist:
            for i in range(n):
                if not i & dist:
                    _exchange(y, i, i + dist)
            dist //= 2
    return [v[0:1, :] for v in y] + [jnp.max(dropped, axis=0, keepdims=True)]


_ROUTER_UNROLL = 8


def _router_kernel(qt_ref, keys_ref, e1_ref, e2_ref, th_ref, cand_ref):
    cidx = _iota((_N_CAND, LANES), 0).astype(f32)
    k0 = keys_ref[0].astype(bf16)
    k1 = keys_ref[1].astype(bf16)
    cand_ref[...] = jnp.full(cand_ref.shape, NEG_INF, f32)

    def one_head(h, slot):
        r0 = pl.multiple_of(h * 2 * N_KEYS, 2 * N_KEYS)
        s1 = _dot(k0, qt_ref[pl.ds(r0, N_KEYS), :].astype(bf16))
        s2 = _dot(k1, qt_ref[pl.ds(r0 + N_KEYS, N_KEYS), :].astype(bf16))
        top_a = _top17_network(s1)
        top_b = _top17_network(s2)
        for ci, (a, b) in enumerate(_CAND_PAIRS):
            cand_ref[slot, ci:ci + 1, :] = top_a[a] + top_b[b]
        cs = _extract_top(cand_ref[slot], cidx, _N_TOP)
        z = jnp.ones_like(cs[0])
        for r in range(1, PEER_TOPK):
            z = z + jnp.exp(cs[r] - cs[0])
        inv_z = 1.0 / z
        mid = 0.5 * (cs[PEER_TOPK - 1] + cs[PEER_TOPK])
        o0 = pl.multiple_of(h * N_KEYS, N_KEYS)
        e1 = jnp.exp(s1 - top_a[0]) * inv_z
        e1_ref[:, h] = e1.reshape(N_KEYS // 8, 8, LANES)
        e2_ref[pl.ds(o0, N_KEYS), :] = jnp.exp(s2 - top_b[0])
        th_ref[pl.ds(h, 1), :] = jnp.exp(mid - cs[0]) * inv_z

    def body(hp, _):
        for slot in range(_ROUTER_UNROLL):
            one_head(hp * _ROUTER_UNROLL + slot, slot)
        return 0

    lax.fori_loop(0, PEER_HEADS // _ROUTER_UNROLL, body, 0)


def peer_router(qt, keys):
    n, m = qt.shape
    half = PEER_HEADS * N_KEYS
    return pl.pallas_call(
        _router_kernel,
        grid=(m // LANES,),
        in_specs=[pl.BlockSpec((n, LANES), lambda i: (0, i)),
                  pl.BlockSpec((2, N_KEYS, N_KEYS), lambda i: (0, 0, 0))],
        out_specs=[pl.BlockSpec((N_KEYS // 8, PEER_HEADS, 8, LANES), lambda i: (0, 0, 0, i)),
                   pl.BlockSpec((half, LANES), lambda i: (0, i)),
                   pl.BlockSpec((PEER_HEADS, LANES), lambda i: (0, i))],
        out_shape=[jax.ShapeDtypeStruct((N_KEYS // 8, PEER_HEADS, 8, m), f32),
                   jax.ShapeDtypeStruct((half, m), f32),
                   jax.ShapeDtypeStruct((PEER_HEADS, m), f32)],
        scratch_shapes=[pltpu.VMEM((_ROUTER_UNROLL, _N_CAND, LANES), f32)],
        compiler_params=_cparams("arbitrary"),
        name="peer_router",
    )(qt, keys)


_G_CHAINS = 3
_PIECE = 1024


class _Chains:
    def __init__(self, n, zero_ref):
        self.deps, self.cnt, self.zero_ref = [None] * n, 0, zero_ref

    def take(self):
        return self.deps[self.cnt % len(self.deps)]

    def put(self, g):
        bits = lax.bitcast_convert_type(g[0:1, :], jnp.int32) & self.zero_ref[0:1, :]
        self.deps[self.cnt % len(self.deps)] = lax.bitcast_convert_type(bits, f32)
        self.cnt += 1


def _routing_block(e1_ref, e2_ref, th_ref, ii, ls, chains):
    sub = 8
    rows = [e1_ref[ii // 8, h, ii % 8:ii % 8 + 1, ls] for h in range(PEER_HEADS)]
    ths = [th_ref[h:h + 1, ls] for h in range(PEER_HEADS)]
    parts = []
    for gi in range(N_KEYS // sub):
        dep = chains.take()
        g = jnp.zeros((sub, LANES), f32)
        for h in range(PEER_HEADS):
            row = rows[h] if (dep is None or h) else rows[h] + dep
            pr = e2_ref[h * N_KEYS + gi * sub:h * N_KEYS + (gi + 1) * sub, ls] * row
            g = g + jnp.where(pr >= ths[h], pr, 0.0)
        chains.put(g)
        parts.append(g)
    return jnp.concatenate(parts, axis=0)


def _experts_kernel(xmt_ref, e1_ref, e2_ref, th_ref, u_ref, v_ref, x_ref, gt_ref, gf_ref, zero_ref,
                    o_ref, w_ref, *, final_norm):
    c = pl.program_id(1)
    tm, te = w_ref.shape

    @pl.when(c == 0)
    def _():
        o_ref[...] = jnp.zeros_like(o_ref)

    chains = _Chains(_G_CHAINS, zero_ref)
    for p in range(te // _PIECE):
        act = _gelu(_dot(u_ref[p * _PIECE:(p + 1) * _PIECE, :], xmt_ref[...]))
        for il in range(_PIECE // N_KEYS):
            ii = p * (_PIECE // N_KEYS) + il
            es = slice(ii * N_KEYS, (ii + 1) * N_KEYS)
            for lc in range(tm // LANES):
                ls = slice(lc * LANES, (lc + 1) * LANES)
                g_blk = _routing_block(e1_ref, e2_ref, th_ref, ii, ls, chains)
                a_blk = act[il * N_KEYS:(il + 1) * N_KEYS, ls]
                w_ref[ls, es] = (g_blk * a_blk).T.astype(bf16)
    o_ref[...] += _dot(w_ref[...], v_ref[...])

    @pl.when(c == pl.num_programs(1) - 1)
    def _():
        y = x_ref[...] + gt_ref[0] * o_ref[...]
        if final_norm:
            ms = jnp.mean(y * y, axis=-1, keepdims=True)
            y = y * lax.rsqrt(ms + EPS) * gf_ref[...]
        o_ref[...] = y


def peer_experts(xmt, e1t, e2t, th, u, v, x, gate, g_final, rows_per_batch, tm, te, final_norm):
    m, d = x.shape
    gt, gt_spec = _mod_specs(gate, m, tm, d, rows_per_batch)
    half = PEER_HEADS * N_KEYS
    return pl.pallas_call(
        functools.partial(_experts_kernel, final_norm=final_norm),
        grid=(m // tm, N_EXPERTS // te),
        in_specs=[
            pl.BlockSpec((d, tm), lambda i, c: (0, i)),
            pl.BlockSpec((te // N_KEYS // 8, PEER_HEADS, 8, tm), lambda i, c: (c, 0, 0, i)),
            pl.BlockSpec((half, tm), lambda i, c: (0, i)),
            pl.BlockSpec((PEER_HEADS, tm), lambda i, c: (0, i)),
            pl.BlockSpec((te, d), lambda i, c: (c, 0)),
            pl.BlockSpec((te, d), lambda i, c: (c, 0)),
            pl.BlockSpec((tm, d), lambda i, c: (i, 0)),
            gt_spec,
            pl.BlockSpec((1, d), lambda i, c: (0, 0)),
            pl.BlockSpec((8, LANES), lambda i, c: (0, 0)),
        ],
        out_specs=pl.BlockSpec((tm, d), lambda i, c: (i, 0)),
        out_shape=jax.ShapeDtypeStruct((m, d), f32),
        scratch_shapes=[pltpu.VMEM((tm, te), bf16)],
        compiler_params=_cparams("arbitrary", "arbitrary"),
        name="peer_experts",
    )(xmt, e1t, e2t, th, u, v, x, gt, g_final.reshape(1, d), jnp.zeros((8, LANES), jnp.int32))


def peer_block(x, g, shift, scale, gate, wq_t, keys, u, v, g_final, rows_per_batch, tm, te, final_norm):
    tm_q = 1024 if x.shape[0] % 1024 == 0 and rows_per_batch % 1024 == 0 else tm
    qt, xmt = peer_query(x, g, shift, scale, wq_t, rows_per_batch, tm_q)
    e1t, e2t, th = peer_router(qt, keys)
    return peer_experts(xmt, e1t, e2t, th, u, v, x, gate, g_final, rows_per_batch, tm, te, final_norm)


def kernel(x_prompt, x_sample, c_prompt, c_sample, state_rglru_conv, state_rglru_h, cache_swa_k, cache_swa_v, cache_sb_k, cache_sb_v, state_mlstm_C, state_mlstm_n, state_mlstm_m, page_table, w_ada, b_ada, g_norm_mix, g_norm_ffn, e_w_in, e_conv_w, e_conv_b, e_w_r, e_b_r, e_w_i, e_b_i, e_lambda, e_w_out, o_w_in, o_b_if, o_sb_bias, o_g_mnorm, o_w_out, peer_w_q, peer_keys, peer_u, peer_v, g_final):
    bp, seq, d = x_prompt.shape
    bs = x_sample.shape[0]
    mp = bp * seq
    pad_s = LANES
    xp = x_prompt.reshape(mp, d)
    xs = x_sample.reshape(bs, d)

    c_rows = 16
    c_all = jnp.concatenate([c_prompt, c_sample, jnp.zeros((c_rows - bp - bs, d), f32)], axis=0)
    mod = adaln_all(c_all, w_ada, b_ada)

    def mods(layer):
        parts = [mod[layer, :, i * d:(i + 1) * d] for i in range(6)]
        return [p[:bp] for p in parts], [p[bp:bp + bs] for p in parts]

    ctab_p, stab_p = _rope_tables(jnp.arange(seq, dtype=jnp.int32))
    ctab_s, stab_s = _rope_tables(jnp.full((1,), PAST_LEN, jnp.int32))

    TM = 512
    m_p, m_s = mods(0)
    w_in = e_w_in[0].astype(bf16)
    w_out = e_w_out[0].astype(bf16)
    cw, cb = e_conv_w[0], e_conv_b[0].reshape(1, -1)
    wr, wi = e_w_r[0].astype(bf16), e_w_i[0].astype(bf16)
    br, bi, lam = e_b_r[0].reshape(1, -1), e_b_i[0].reshape(1, -1), e_lambda[0].reshape(1, -1)

    proj_p = mod_matmul(xp, g_norm_mix[0], m_p[0], m_p[1], w_in, E_IN, seq, 1024, 1024, name="e_in_p")
    proj_s = mod_matmul(xs, g_norm_mix[0], m_s[0], m_s[1], w_in, E_IN, 1, bs, 512, name="e_in_s")

    ya_p, h_p = rglru_prompt(proj_p, bp, seq, cw, cb, wr, br, wi, bi, lam)
    ya_s, h_s = rglru_step(proj_s, state_rglru_conv[0], state_rglru_h[0], cw, cb, wr, br, wi, bi, lam)
    conv_p = proj_p.reshape(bp, seq, E_IN)[:, seq - (CONV_W - 1):, :RG_WIDTH]
    conv_s = jnp.concatenate([state_rglru_conv[0][:, 1:], proj_s[:, None, :RG_WIDTH]], axis=1)

    qkv_p, k_p = rope_split(proj_p, ctab_p, stab_p, bp, seq)
    q_s, k_s, v_s = rope_qkv(proj_s, ctab_s, stab_s, 1, bs)
    o_p = dilated_prompt(qkv_p, bp, seq)
    win = cache_swa_k.shape[2]
    o_s = dilated_step(q_s, k_s, v_s, cache_swa_k.reshape(-1, win, HEADS, DH)[:bs],
                       cache_swa_v.reshape(-1, win, HEADS, DH)[:bs])
    wl = min(2048, seq)
    swa_k_p, swa_v_p = [a.reshape(bp, wl, HEADS, DH)
                        for a in heads_layout([(k_p, 0), (proj_p, E_IN // ATT_W - 1)], bp, seq, wl)]
    swa_k_s = k_s.reshape(bs, 1, HEADS, DH)
    swa_v_s = v_s.reshape(bs, 1, HEADS, DH)

    xp = out_proj(ya_p, o_p, w_out, xp, m_p[2], seq, 1024, 1024, name="e_out_p")
    xs = out_proj(ya_s, o_s, w_out, xs, m_s[2], 1, bs, 512, name="e_out_s")

    def peer_layer(layer, xp, xs, m_p, m_s, final_norm):
        wq_t = peer_w_q[layer].T.astype(bf16)
        u = cast_layer_bf16(peer_u, layer)
        v = cast_layer_bf16(peer_v, layer)
        xp = peer_block(xp, g_norm_ffn[layer], m_p[3], m_p[4], m_p[5], wq_t, peer_keys[layer], u, v,
                        g_final, seq, TM, 1024, final_norm)
        xs_pad = jnp.pad(xs, ((0, pad_s - bs), (0, 0)))
        xs_new = peer_block(xs_pad, g_norm_ffn[layer], m_s[3], m_s[4], m_s[5], wq_t, peer_keys[layer],
                            u, v, g_final, 1, pad_s, 1024, final_norm)
        return xp, xs_new[:bs]

    xp, xs = peer_layer(0, xp, xs, m_p, m_s, False)

    m_p, m_s = mods(1)
    w_in2 = o_w_in[0].astype(bf16)
    w_gate = jnp.pad(w_in2[:, O_MAIN:], ((0, 0), (0, LANES - 2 * M_HEADS)))
    w_out2 = o_w_out[0].astype(bf16)
    gate_bias = jnp.pad(o_b_if[0].reshape(1, 2 * M_HEADS), ((0, 0), (0, LANES - 2 * M_HEADS)))
    gmn = o_g_mnorm[0].reshape(1, M_WIDTH)

    proj2_p, gates_p = mod_matmul(xp, g_norm_mix[1], m_p[0], m_p[1], w_in2, O_MAIN, seq, 1024, 1024,
                                  w_gate=w_gate, name="o_in_p")
    proj2_s, gates_s = mod_matmul(xs, g_norm_mix[1], m_s[0], m_s[1], w_in2, O_MAIN, 1, bs, 512,
                                  w_gate=w_gate, name="o_in_s")

    oc_p = sb_prompt(proj2_p, o_sb_bias[0], bp, seq)
    oc_s = sb_step(proj2_s[:, :ATT_W], o_sb_bias[0], cache_sb_k.reshape(-1, PAGE, HEADS, DH),
                   cache_sb_v.reshape(-1, PAGE, HEADS, DH), page_table)
    hm_p, mC_p, mn_p, mm_p = mlstm_prompt(proj2_p, gates_p, gate_bias, gmn, bp, seq)
    hm_s, mC_s, mn_s, mm_s = mlstm_step(proj2_s, gates_s, gate_bias, gmn,
                                        state_mlstm_C.reshape(-1, M_HEADS, M_DH, M_DH)[:bs],
                                        state_mlstm_n[0], state_mlstm_m[0])

    n_pg = seq // PAGE
    sb_k_p, sb_v_p = [a.reshape(bp, n_pg, PAGE, HEADS, DH)
                      for a in heads_layout([(proj2_p, 1), (proj2_p, 2)], bp, seq, seq)]
    sb_k_s = proj2_s[:, ATT_W:2 * ATT_W].reshape(bs, 1, HEADS, DH)
    sb_v_s = proj2_s[:, 2 * ATT_W:3 * ATT_W].reshape(bs, 1, HEADS, DH)

    xp = out_proj(oc_p, hm_p, w_out2, xp, m_p[2], seq, 1024, 1024, name="o_out_p")
    xs = out_proj(oc_s, hm_s, w_out2, xs, m_s[2], 1, bs, 512, name="o_out_s")
    xp, xs = peer_layer(1, xp, xs, m_p, m_s, True)

    y_prompt = xp.reshape(bp, seq, d)
    y_sample = xs.reshape(bs, 1, d)
    st = lambda a: a[None]
    return (y_prompt, y_sample, st(conv_p), st(conv_s), st(h_p.reshape(bp, RG_WIDTH)), st(h_s),
            st(swa_k_p), st(swa_k_s), st(swa_v_p), st(swa_v_s),
            st(sb_k_p), st(sb_k_s), st(sb_v_p), st(sb_v_s),
            st(mC_p), st(mC_s), st(mn_p), st(mn_s), st(mm_p), st(mm_s))
```

```python
import functools
import math

import jax
import jax.numpy as jnp
from jax import lax
from jax.experimental import pallas as pl
from jax.experimental.pallas import tpu as pltpu

f32 = jnp.float32
bf16 = jnp.bfloat16

D_MODEL = 2048
PAST_LEN = 16384
PAGE = 128
RG_WIDTH = 1024
RG_BLOCKS = 8
CONV_W = 4
RG_C = 8.0
HEADS = 8
DH = 128
ATT_W = HEADS * DH
DIL_PATTERNS = ((128, 1), (512, 4), (2048, 16))
ROT_DIMS = 32
ROPE_THETA = 500000.0
M_HEADS = 4
M_DH = 256
M_WIDTH = M_HEADS * M_DH
CHUNK = 128
E_IN = 2 * RG_WIDTH + 3 * ATT_W
O_MAIN = 3 * ATT_W + 4 * M_WIDTH
PEER_HEADS = 8
N_KEYS = 128
N_EXPERTS = N_KEYS * N_KEYS
PEER_TOPK = 16
EPS = 1e-6
LANES = 128
SUBLANES = 8
VMEM_LIMIT = 56 * 1024 * 1024
NEG_INF = float("-inf")


def _cparams(*sem):
    return pltpu.CompilerParams(dimension_semantics=sem, vmem_limit_bytes=VMEM_LIMIT)


def _dot(a, b):
    return jnp.dot(a, b, preferred_element_type=f32)


def _dot_nt(a, b):
    return lax.dot_general(a, b, (((1,), (1,)), ((), ())), preferred_element_type=f32)


def _split3(x):
    hi = x.astype(bf16)
    r = x - hi.astype(f32)
    mid = r.astype(bf16)
    lo = (r - mid.astype(f32)).astype(bf16)
    return hi, mid, lo


def _sigmoid(x):
    return 1.0 / (1.0 + jnp.exp(-x))


def _log_sigmoid_pair(z):
    l1p = jnp.log1p(jnp.exp(-jnp.abs(z)))
    return jnp.minimum(z, 0.0) - l1p, -jnp.maximum(z, 0.0) - l1p


def _softplus(z):
    return jnp.maximum(z, 0.0) + jnp.log(1.0 + jnp.exp(-jnp.abs(z)))


def _gelu(x):
    c = math.sqrt(2.0 / math.pi)
    h = 0.5 * x
    return h + h * jnp.tanh(x * (c + (c * 0.044715) * (x * x)))


def _iota(shape, dim):
    return lax.broadcasted_iota(jnp.int32, shape, dim)


def _rowsum_bcast(x):
    ones = jnp.ones((LANES, LANES), bf16)
    hi = x.astype(bf16)
    lo = (x - hi.astype(f32)).astype(bf16)
    return _dot(hi, ones) + _dot(lo, ones)


def _adaln_kernel(c_ref, w_ref, b_ref, o_ref):
    c = c_ref[...]
    s = c * _sigmoid(c)
    w = w_ref[0]
    s_hi = s.astype(bf16)
    s_lo = (s - s_hi.astype(f32)).astype(bf16)
    w_hi = w.astype(bf16)
    w_lo = (w - w_hi.astype(f32)).astype(bf16)
    o_ref[0] = _dot(s_hi, w_hi) + _dot(s_hi, w_lo) + _dot(s_lo, w_hi) + b_ref[0]


def adaln_all(c_all, w_ada, b_ada):
    depth, d, n = w_ada.shape
    rows = c_all.shape[0]
    tn = 1024
    return pl.pallas_call(
        _adaln_kernel,
        grid=(depth, n // tn),
        in_specs=[
            pl.BlockSpec((rows, d), lambda l, j: (0, 0)),
            pl.BlockSpec((1, d, tn), lambda l, j: (l, 0, j)),
            pl.BlockSpec((1, 1, tn), lambda l, j: (l, 0, j)),
        ],
        out_specs=pl.BlockSpec((1, rows, tn), lambda l, j: (l, 0, j)),
        out_shape=jax.ShapeDtypeStruct((depth, rows, n), f32),
        compiler_params=_cparams("arbitrary", "arbitrary"),
        name="adaln",
    )(c_all, w_ada, b_ada.reshape(depth, 1, n))


def _modulated(x, g, sh, sc):
    ms = jnp.mean(x * x, axis=-1, keepdims=True)
    y = x * lax.rsqrt(ms + EPS) * g
    return y * (1.0 + sc) + sh


def _modmm_kernel(x_ref, g_ref, sh_ref, sc_ref, w_ref, o_ref, xn_ref):
    @pl.when(pl.program_id(1) == 0)
    def _():
        xn_ref[...] = _modulated(x_ref[...], g_ref[...], sh_ref[0], sc_ref[0]).astype(bf16)

    o_ref[...] = _dot(xn_ref[...], w_ref[...])


def _modmm_gate_kernel(x_ref, g_ref, sh_ref, sc_ref, w_ref, wg_ref, o_ref, og_ref, xn_ref):
    @pl.when(pl.program_id(1) == 0)
    def _():
        xn = _modulated(x_ref[...], g_ref[...], sh_ref[0], sc_ref[0]).astype(bf16)
        xn_ref[...] = xn
        og_ref[...] = _dot(xn, wg_ref[...])

    o_ref[...] = _dot(xn_ref[...], w_ref[...])


def _mod_specs(mod, m, tm, k, rows_per_batch):
    if rows_per_batch >= tm:
        assert rows_per_batch % tm == 0
        per = rows_per_batch // tm
        return mod.reshape(-1, 1, k), pl.BlockSpec((1, 1, k), lambda i, j: (i // per, 0, 0))
    assert rows_per_batch == 1
    rows = mod
    if rows.shape[0] < m:
        rows = jnp.pad(rows, ((0, m - rows.shape[0]), (0, 0)))
    return rows.reshape(1, m, k), pl.BlockSpec((1, tm, k), lambda i, j: (0, i, 0))


def mod_matmul(x, g, shift, scale, w, n_out, rows_per_batch, tm, tn, w_gate=None, name="modmm"):
    m, k = x.shape
    sh, sh_spec = _mod_specs(shift, m, tm, k, rows_per_batch)
    sc, sc_spec = _mod_specs(scale, m, tm, k, rows_per_batch)
    in_specs = [
        pl.BlockSpec((tm, k), lambda i, j: (i, 0)),
        pl.BlockSpec((1, k), lambda i, j: (0, 0)),
        sh_spec,
        sc_spec,
        pl.BlockSpec((k, tn), lambda i, j: (0, j)),
    ]
    args = [x, g.reshape(1, k), sh, sc, w]
    out_specs = pl.BlockSpec((tm, tn), lambda i, j: (i, j))
    out_shape = jax.ShapeDtypeStruct((m, n_out), f32)
    kern = _modmm_kernel
    if w_gate is not None:
        in_specs.append(pl.BlockSpec((k, LANES), lambda i, j: (0, 0)))
        args.append(w_gate)
        out_specs = [out_specs, pl.BlockSpec((tm, LANES), lambda i, j: (i, 0))]
        out_shape = [out_shape, jax.ShapeDtypeStruct((m, LANES), f32)]
        kern = _modmm_gate_kernel
    return pl.pallas_call(
        kern,
        grid=(m // tm, n_out // tn),
        in_specs=in_specs,
        out_specs=out_specs,
        out_shape=out_shape,
        scratch_shapes=[pltpu.VMEM((tm, k), bf16)],
        compiler_params=_cparams("arbitrary", "arbitrary"),
        name=name,
    )(*args)


def _outproj_kernel(a1_ref, a2_ref, w1_ref, w2_ref, x_ref, gt_ref, o_ref):
    y = _dot(a1_ref[...].astype(bf16), w1_ref[...]) + _dot(a2_ref[...].astype(bf16), w2_ref[...])
    o_ref[...] = x_ref[...] + gt_ref[0] * y


def out_proj(a1, a2, w, x, gate, rows_per_batch, tm, tn, name="outproj"):
    m, k1 = a1.shape
    k2 = a2.shape[1]
    n = w.shape[1]
    gt, gt_spec = _mod_specs(gate, m, tm, n, rows_per_batch)
    if gt.shape[1] == 1:
        per = rows_per_batch // tm
        gt_spec = pl.BlockSpec((1, 1, tn), lambda i, j: (i // per, 0, j))
    else:
        gt_spec = pl.BlockSpec((1, tm, tn), lambda i, j: (0, i, j))
    return pl.pallas_call(
        _outproj_kernel,
        grid=(m // tm, n // tn),
        in_specs=[
            pl.BlockSpec((tm, k1), lambda i, j: (i, 0)),
            pl.BlockSpec((tm, k2), lambda i, j: (i, 0)),
            pl.BlockSpec((k1, tn), lambda i, j: (0, j)),
            pl.BlockSpec((k2, tn), lambda i, j: (k1 // k2, j)),
            pl.BlockSpec((tm, tn), lambda i, j: (i, j)),
            gt_spec,
        ],
        out_specs=pl.BlockSpec((tm, tn), lambda i, j: (i, j)),
        out_shape=jax.ShapeDtypeStruct((m, n), f32),
        compiler_params=_cparams("arbitrary", "arbitrary"),
        name=name,
    )(a1, a2, w, w, x, gt)


def _rglru_gates(xc, wr_ref, br, wi_ref, bi, lam):
    xb = xc.astype(bf16)
    rs, gs = [], []
    for hb in range(RG_BLOCKS):
        sl = slice(hb * LANES, (hb + 1) * LANES)
        rs.append(_dot(xb[:, sl], wr_ref[hb]))
        gs.append(_dot(xb[:, sl], wi_ref[hb]))
    r = _sigmoid(jnp.concatenate(rs, axis=1) + br)
    ig = _sigmoid(jnp.concatenate(gs, axis=1) + bi)
    softplus_neg_lam = jnp.maximum(-lam, 0.0) + jnp.log1p(jnp.exp(-jnp.abs(lam)))
    log_a = -RG_C * r * softplus_neg_lam
    a = jnp.exp(log_a)
    u = jnp.sqrt(-jnp.tanh(log_a) * (a * a + 1.0)) * ig * xc
    return a, u


def _rglru_kernel(xa_ref, ga_ref, cw_ref, cb_ref, wr_ref, br_ref, wi_ref, bi_ref, lam_ref,
                  ya_ref, hl_ref, xprev_ref, hc_ref):
    t_idx = pl.program_id(1)
    tt = xa_ref.shape[0]

    @pl.when(t_idx == 0)
    def _():
        xprev_ref[...] = jnp.zeros_like(xprev_ref)
        hc_ref[...] = jnp.zeros_like(hc_ref)

    xa = xa_ref[...]
    xprev = xprev_ref[...]
    row8 = _iota((8, RG_WIDTH), 0)
    xc = cb_ref[...] + cw_ref[CONV_W - 1:CONV_W, :] * xa
    for k in range(1, CONV_W):
        rolled = pltpu.roll(xa, k, 0)
        head = jnp.where(row8 < k, pltpu.roll(xprev, k, 0), rolled[0:8])
        shifted = jnp.concatenate([head, rolled[8:]], axis=0)
        xc = xc + cw_ref[CONV_W - 1 - k:CONV_W - k, :] * shifted
    xprev_ref[...] = xa[tt - 8:tt]

    a, u = _rglru_gates(xc, wr_ref, br_ref[...], wi_ref, bi_ref[...], lam_ref[...])
    row = _iota((tt, RG_WIDTH), 0)
    s = 1
    while s < tt:
        a_sh = pltpu.roll(a, s, 0)
        u_sh = pltpu.roll(u, s, 0)
        ok = row >= s
        u = jnp.where(ok, a * u_sh + u, u)
        a = jnp.where(ok, a * a_sh, a)
        s *= 2
    h = a * hc_ref[...] + u
    hc_ref[...] = h[tt - 1:tt]
    hl_ref[0] = h[tt - 1:tt]
    ya_ref[...] = h * _gelu(ga_ref[...])


def rglru_prompt(proj, bsz, seq, cw, cb, wr, br, wi, bi, lam, tt=256):
    nt = seq // tt
    vec = lambda: pl.BlockSpec((1, RG_WIDTH), lambda b, t: (0, 0))
    return pl.pallas_call(
        _rglru_kernel,
        grid=(bsz, nt),
        in_specs=[
            pl.BlockSpec((tt, RG_WIDTH), lambda b, t: (b * nt + t, 0)),
            pl.BlockSpec((tt, RG_WIDTH), lambda b, t: (b * nt + t, 1)),
            pl.BlockSpec((CONV_W, RG_WIDTH), lambda b, t: (0, 0)),
            vec(),
            pl.BlockSpec((RG_BLOCKS, LANES, LANES), lambda b, t: (0, 0, 0)),
            vec(),
            pl.BlockSpec((RG_BLOCKS, LANES, LANES), lambda b, t: (0, 0, 0)),
            vec(),
            vec(),
        ],
        out_specs=[
            pl.BlockSpec((tt, RG_WIDTH), lambda b, t: (b * nt + t, 0)),
            pl.BlockSpec((1, 1, RG_WIDTH), lambda b, t: (b, 0, 0)),
        ],
        out_shape=[
            jax.ShapeDtypeStruct((bsz * seq, RG_WIDTH), f32),
            jax.ShapeDtypeStruct((bsz, 1, RG_WIDTH), f32),
        ],
        scratch_shapes=[pltpu.VMEM((8, RG_WIDTH), f32), pltpu.VMEM((1, RG_WIDTH), f32)],
        compiler_params=_cparams("arbitrary", "arbitrary"),
        name="rglru_prompt",
    )(proj, proj, cw, cb, wr, br, wi, bi, lam)


def _rglru_step_kernel(xa_ref, ga_ref, b0_ref, b1_ref, b2_ref, h0_ref, cw_ref, cb_ref,
                       wr_ref, br_ref, wi_ref, bi_ref, lam_ref, ya_ref, h_ref):
    xa = xa_ref[...]
    xc = (cb_ref[...] + cw_ref[0:1, :] * b0_ref[...] + cw_ref[1:2, :] * b1_ref[...]
          + cw_ref[2:3, :] * b2_ref[...] + cw_ref[3:4, :] * xa)
    a, u = _rglru_gates(xc, wr_ref, br_ref[...], wi_ref, bi_ref[...], lam_ref[...])
    h = a * h0_ref[...] + u
    h_ref[...] = h
    ya_ref[...] = h * _gelu(ga_ref[...])


def rglru_step(proj_s, conv_state, h0, cw, cb, wr, br, wi, bi, lam):
    n = proj_s.shape[0]
    full = lambda shape: pl.BlockSpec(shape, lambda i: tuple(0 for _ in shape))
    return pl.pallas_call(
        _rglru_step_kernel,
        grid=(1,),
        in_specs=[
            pl.BlockSpec((n, RG_WIDTH), lambda i: (0, 0)),
            pl.BlockSpec((n, RG_WIDTH), lambda i: (0, 1)),
            full((n, RG_WIDTH)), full((n, RG_WIDTH)), full((n, RG_WIDTH)), full((n, RG_WIDTH)),
            full((CONV_W, RG_WIDTH)), full((1, RG_WIDTH)),
            full((RG_BLOCKS, LANES, LANES)), full((1, RG_WIDTH)),
            full((RG_BLOCKS, LANES, LANES)), full((1, RG_WIDTH)), full((1, RG_WIDTH)),
        ],
        out_specs=[full((n, RG_WIDTH)), full((n, RG_WIDTH))],
        out_shape=[jax.ShapeDtypeStruct((n, RG_WIDTH), f32)] * 2,
        compiler_params=_cparams("arbitrary"),
        name="rglru_step",
    )(proj_s, proj_s, conv_state[:, 0], conv_state[:, 1], conv_state[:, 2], h0,
      cw, cb, wr, br, wi, bi, lam)


def _rope_tables(pos):
    half = ROT_DIMS // 2
    inv = ROPE_THETA ** (-jnp.arange(half, dtype=f32) / half)
    ang = pos.astype(f32)[:, None] * inv[None, :]
    cos, sin = jnp.cos(ang), jnp.sin(ang)
    n = pos.shape[0]
    ctab = jnp.concatenate([cos, cos, jnp.ones((n, DH - ROT_DIMS), f32)], axis=1)
    stab = jnp.concatenate([-sin, sin, jnp.zeros((n, DH - ROT_DIMS), f32)], axis=1)
    return ctab, stab


def _rope_head(xh, ctab, stab, lane):
    half = ROT_DIMS // 2
    partner = jnp.where(lane < half, pltpu.roll(xh, DH - half, 1), pltpu.roll(xh, half, 1))
    return xh * ctab + partner * stab


def _rope_kernel(q_ref, k_ref, v_ref, c_ref, s_ref, qo_ref, ko_ref, vo_ref):
    ctab, stab = c_ref[...], s_ref[...]
    lane = _iota((q_ref.shape[0], DH), 1)
    for hb in range(HEADS):
        sl = slice(hb * DH, (hb + 1) * DH)
        qo_ref[:, hb, :] = _rope_head(q_ref[:, sl], ctab, stab, lane) * (DH ** -0.5)
        ko_ref[:, hb, :] = _rope_head(k_ref[:, sl], ctab, stab, lane)
        vo_ref[:, hb, :] = v_ref[:, sl]


def rope_qkv(proj, ctab, stab, rows_per_seq, tt):
    m = proj.shape[0]
    nt = max(rows_per_seq // tt, 1)
    if ctab.shape[0] == 1:
        tab_spec = pl.BlockSpec((1, DH), lambda i: (0, 0))
    else:
        tab_spec = pl.BlockSpec((tt, DH), lambda i: (i % nt, 0))
    return pl.pallas_call(
        _rope_kernel,
        grid=(m // tt,),
        in_specs=[
            pl.BlockSpec((tt, ATT_W), lambda i: (i, 2)),
            pl.BlockSpec((tt, ATT_W), lambda i: (i, 3)),
            pl.BlockSpec((tt, ATT_W), lambda i: (i, 4)),
            tab_spec, tab_spec,
        ],
        out_specs=[pl.BlockSpec((tt, HEADS, DH), lambda i: (i, 0, 0))] * 3,
        out_shape=[jax.ShapeDtypeStruct((m, HEADS, DH), f32)] * 3,
        compiler_params=_cparams("arbitrary"),
        name="rope",
    )(proj, proj, proj, ctab, stab)


def _rope_split_kernel(q_ref, k_ref, v_ref, c_ref, s_ref, *refs):
    n_pat = len(DIL_PATTERNS)
    outs, (kf_ref, qs_ref, ks_ref) = refs[:3 * n_pat], refs[3 * n_pat:]
    tt = q_ref.shape[0]
    ctab, stab = c_ref[...], s_ref[...]
    lane = _iota((tt, DH), 1)
    qs_ref[...] = _rope_head(q_ref[...], ctab, stab, lane) * (DH ** -0.5)
    k_rot = _rope_head(k_ref[...], ctab, stab, lane)
    ks_ref[...] = k_rot
    kf_ref[...] = k_rot
    for gi, (_, d) in enumerate(DIL_PATTERNS):
        rows = tt // d
        for src, dst in ((qs_ref, outs[3 * gi]), (ks_ref, outs[3 * gi + 1]), (v_ref, outs[3 * gi + 2])):
            for r in range(d):
                dst[0, 0, r] = src[pl.ds(r, rows, stride=d), :].astype(bf16)


def rope_split(proj, ctab, stab, bsz, seq, tt=1024):
    m = proj.shape[0]
    nt = seq // tt
    col = lambda c: pl.BlockSpec((tt, DH), lambda i, h, c=c: (i, c * HEADS + h))
    tab = pl.BlockSpec((tt, DH), lambda i, h: (i % nt, 0))
    out_specs, out_shape = [], []
    for (_, d) in DIL_PATTERNS:
        for _ in range(3):
            out_specs.append(pl.BlockSpec((1, 1, d, tt // d, DH), lambda i, h: (i // nt, h, 0, i % nt, 0)))
            out_shape.append(jax.ShapeDtypeStruct((bsz, HEADS, d, seq // d, DH), bf16))
    out_specs.append(pl.BlockSpec((tt, DH), lambda i, h: (i, h)))
    out_shape.append(jax.ShapeDtypeStruct((m, ATT_W), f32))
    res = pl.pallas_call(
        _rope_split_kernel,
        grid=(m // tt, HEADS),
        in_specs=[col(2), col(3), col(4), tab, tab],
        out_specs=out_specs,
        out_shape=out_shape,
        scratch_shapes=[pltpu.VMEM((tt, DH), f32), pltpu.VMEM((tt, DH), f32)],
        compiler_params=_cparams("arbitrary", "arbitrary"),
        name="rope_split",
    )(proj, proj, proj, ctab, stab)
    return [res[3 * gi:3 * gi + 3] for gi in range(len(DIL_PATTERNS))], res[-1]


def _dil_kernel(q_ref, kc_ref, kp_ref, vc_ref, vp_ref, o_ref, l_ref, s_ref, p_ref, *, span):
    tq = q_ref.shape[3]
    blk = pl.program_id(2)
    qi = _iota((tq, 2 * tq), 0)
    col = _iota((tq, 2 * tq), 1)
    rel = jnp.where(col < tq, qi - col, qi - col + 2 * tq)
    ok = (rel >= 0) & (rel <= span) & ((col < tq) | (blk > 0))
    for hb in range(HEADS):
        qh = q_ref[0, hb, 0]
        s_ref[hb, :, :tq] = _dot_nt(qh, kc_ref[0, hb, 0])
        s_ref[hb, :, tq:] = _dot_nt(qh, kp_ref[0, hb, 0])
    for hb in range(HEADS):
        s = jnp.where(ok, s_ref[hb], NEG_INF)
        mx = jnp.max(s, axis=1, keepdims=True)
        p = jnp.exp(s - mx)
        den = jnp.sum(p, axis=1, keepdims=True)
        p_ref[hb] = (p * (1.0 / den)).astype(bf16)
        l_ref[0, hb, 0] = jnp.broadcast_to(mx + jnp.log(den), (tq, DH))
    for hb in range(HEADS):
        o_ref[0, hb, 0] = (_dot(p_ref[hb, :, :tq], vc_ref[0, hb, 0])
                           + _dot(p_ref[hb, :, tq:], vp_ref[0, hb, 0]))


def _dil_merge_kernel(*refs):
    n_pat = len(DIL_PATTERNS)
    ins, o_ref, scr = refs[:2 * n_pat], refs[2 * n_pat], refs[2 * n_pat + 1:]
    tt = o_ref.shape[0]
    for gi, (_, d) in enumerate(DIL_PATTERNS):
        rows = tt // d
        for src, dst in ((ins[2 * gi], scr[2 * gi]), (ins[2 * gi + 1], scr[2 * gi + 1])):
            for r in range(d):
                dst[pl.ds(r, rows, stride=d), :] = src[0, 0, r]
    lses = [scr[2 * gi + 1][...] for gi in range(n_pat)]
    top = functools.reduce(jnp.maximum, lses)
    es = [jnp.exp(l - top) for l in lses]
    num = sum(scr[2 * gi][...] * es[gi] for gi in range(n_pat))
    o_ref[...] = num / sum(es)


def dilated_prompt(qkv_by_pattern, bsz, seq, tq=128, tt=1024):
    partial = []
    for (w, d), (qd, kd, vd) in zip(DIL_PATTERNS, qkv_by_pattern):
        sd = seq // d
        blk = (1, HEADS, 1, tq, DH)
        cur = pl.BlockSpec(blk, lambda b, r, i: (b, 0, r, i, 0))
        prev = pl.BlockSpec(blk, lambda b, r, i: (b, 0, r, jnp.maximum(i - 1, 0), 0))
        partial += pl.pallas_call(
            functools.partial(_dil_kernel, span=w // d),
            grid=(bsz, d, sd // tq),
            in_specs=[cur, cur, prev, cur, prev],
            out_specs=[cur, cur],
            out_shape=[jax.ShapeDtypeStruct((bsz, HEADS, d, sd, DH), f32)] * 2,
            scratch_shapes=[pltpu.VMEM((HEADS, tq, 2 * tq), f32), pltpu.VMEM((HEADS, tq, 2 * tq), bf16)],
            compiler_params=_cparams("arbitrary", "arbitrary", "arbitrary"),
            name=f"dilattn_d{d}",
        )(qd, kd, kd, vd, vd)
    nt = seq // tt
    in_specs = []
    for (_, d) in DIL_PATTERNS:
        in_specs += [pl.BlockSpec((1, 1, d, tt // d, DH), lambda i, h: (i // nt, h, 0, i % nt, 0))] * 2
    return pl.pallas_call(
        _dil_merge_kernel,
        grid=(bsz * nt, HEADS),
        in_specs=in_specs,
        out_specs=pl.BlockSpec((tt, DH), lambda i, h: (i, h)),
        out_shape=jax.ShapeDtypeStruct((bsz * seq, ATT_W), f32),
        scratch_shapes=[pltpu.VMEM((tt, DH), f32)] * (2 * len(DIL_PATTERNS)),
        compiler_params=_cparams("arbitrary", "arbitrary"),
        name="dilattn_merge",
    )(*partial)


def _dil_step_kernel(q_ref, kn_ref, vn_ref, k1_ref, k4_ref, k16_ref, v1_ref, v4_ref, v16_ref, o_ref):
    q = q_ref[0]
    kn, vn = kn_ref[0], vn_ref[0]
    s_self = _rowsum_bcast(q * kn)
    o_gs, lse_gs = [], []
    for k_ref, v_ref in ((k1_ref, v1_ref), (k4_ref, v4_ref), (k16_ref, v16_ref)):
        k3 = k_ref[0, :, 0]
        nk = k3.shape[0]
        s = _rowsum_bcast((k3 * q[None]).reshape(nk * HEADS, DH)).reshape(nk, HEADS, DH)
        mx = jnp.maximum(jnp.max(s, axis=0), s_self)
        p = jnp.exp(s - mx[None])
        p_self = jnp.exp(s_self - mx)
        den = jnp.sum(p, axis=0) + p_self
        num = jnp.sum(p * v_ref[0, :, 0], axis=0) + p_self * vn
        o_gs.append(num / den)
        lse_gs.append(mx + jnp.log(den))
    top = jnp.maximum(jnp.maximum(lse_gs[0], lse_gs[1]), lse_gs[2])
    es = [jnp.exp(l - top) for l in lse_gs]
    tot = es[0] + es[1] + es[2]
    o_ref[0] = (o_gs[0] * es[0] + o_gs[1] * es[1] + o_gs[2] * es[2]) / tot


def dilated_step(q4, k4, v4, cache_k, cache_v):
    n, win = cache_k.shape[0], cache_k.shape[1]
    one = pl.BlockSpec((1, HEADS, DH), lambda b: (b, 0, 0))
    args = [q4, k4, v4]
    in_specs = [one, one, one]
    for cache in (cache_k, cache_v):
        for (w, d) in DIL_PATTERNS:
            nkeys = w // d
            assert win % d == 0 and (win // d) % nkeys == 0 and (win - w) % (d * nkeys) == 0
            args.append(cache.reshape(n, win // d, d, HEADS, DH))
            in_specs.append(pl.BlockSpec((1, nkeys, 1, HEADS, DH),
                                         lambda b, blk=(win - w) // d // nkeys: (b, blk, 0, 0, 0)))
    return pl.pallas_call(
        _dil_step_kernel,
        grid=(n,),
        in_specs=in_specs,
        out_specs=one,
        out_shape=jax.ShapeDtypeStruct((n, HEADS, DH), f32),
        compiler_params=_cparams("arbitrary"),
        name="dilattn_step",
    )(*args).reshape(n, ATT_W)


def _sb_kernel(bias_ref, q_ref, k_ref, v_ref, o_ref, kb_ref, vb_ref, qs_ref, t_ref, spb_ref, wb_ref,
               acc_ref, run_ref):
    tq = q_ref.shape[1]
    nh = q_ref.shape[2] // DH
    kt_w = t_ref.shape[2]
    sub = PAGE
    hg = pl.program_id(1)
    qb = pl.program_id(2)

    @pl.when(qb == 0)
    def _():
        kb_ref[...] = k_ref[0].astype(bf16)
        vb_ref[...] = v_ref[0].astype(bf16)

    rr = _iota((sub, 2 * sub), 0)
    cc = _iota((sub, 2 * sub), 1)
    tri = jnp.where((rr > cc) | (cc >= sub), 1.0, 0.0).astype(bf16)
    for h in range(nh):
        qs_ref[h] = (q_ref[0, :, h * DH:(h + 1) * DH] * (DH ** -0.5)).astype(bf16)
    acc_ref[...] = jnp.zeros_like(acc_ref)
    run_ref[...] = jnp.zeros_like(run_ref)

    def macro(start, masked):
        for h in range(nh):
            hs = slice(h * DH, (h + 1) * DH)
            z = _dot_nt(qs_ref[h], kb_ref[pl.ds(start, kt_w), hs]) + bias_ref[hg * nh + h]
            sp = _softplus(z)
            t = z - sp
            if masked:
                ok = start + _iota((tq, kt_w), 1) < qb * tq + _iota((tq, kt_w), 0)
                sp = jnp.where(ok, sp, 0.0)
                t = jnp.where(ok, t, NEG_INF)
            t_ref[h] = t
            spb_ref[h] = sp.astype(bf16)
        for h in range(nh):
            run = run_ref[h]
            for kt in range(kt_w // sub - 1, -1, -1):
                ks = slice(kt * sub, (kt + 1) * sub)
                cs = _dot(spb_ref[h, :, ks], tri)
                wb_ref[h, :, ks] = jnp.exp(t_ref[h, :, ks] - cs[:, :sub] - run).astype(bf16)
                run = run + cs[:, sub:]
            run_ref[h] = run
        for h in range(nh):
            hs = slice(h * DH, (h + 1) * DH)
            acc_ref[h] += _dot(wb_ref[h], vb_ref[pl.ds(start, kt_w), hs])

    top = (qb * tq) // kt_w
    macro(pl.multiple_of(top * kt_w, kt_w), True)

    def body(it, _):
        macro(pl.multiple_of((top - 1 - it) * kt_w, kt_w), False)
        return 0

    lax.fori_loop(0, top, body, 0)
    for h in range(nh):
        o_ref[0, :, h * DH:(h + 1) * DH] = acc_ref[h]


def sb_prompt(proj2, bias, bsz, seq, tq=256, nh=4, kt_w=512):
    assert kt_w % tq == 0 and seq % kt_w == 0
    p3 = proj2.reshape(bsz, seq, proj2.shape[1])
    nq = seq // tq
    ng = HEADS // nh
    wd = nh * DH
    return pl.pallas_call(
        _sb_kernel,
        grid=(bsz, ng, nq),
        in_specs=[
            pl.BlockSpec(memory_space=pltpu.SMEM),
            pl.BlockSpec((1, tq, wd), lambda b, g, i: (b, i, g)),
            pl.BlockSpec((1, seq, wd), lambda b, g, i: (b, 0, ng + g)),
            pl.BlockSpec((1, seq, wd), lambda b, g, i: (b, 0, 2 * ng + g)),
        ],
        out_specs=pl.BlockSpec((1, tq, wd), lambda b, g, i: (b, i, g)),
        out_shape=jax.ShapeDtypeStruct((bsz, seq, ATT_W), f32),
        scratch_shapes=[
            pltpu.VMEM((seq, wd), bf16), pltpu.VMEM((seq, wd), bf16),
            pltpu.VMEM((nh, tq, DH), bf16),
            pltpu.VMEM((nh, tq, kt_w), f32), pltpu.VMEM((nh, tq, kt_w), bf16),
            pltpu.VMEM((nh, tq, kt_w), bf16),
            pltpu.VMEM((nh, tq, DH), f32), pltpu.VMEM((nh, tq, PAGE), f32),
        ],
        compiler_params=_cparams("arbitrary", "arbitrary", "arbitrary"),
        name="sb_prompt",
    )(bias, p3, p3, p3).reshape(bsz * seq, ATT_W)


def _sb_step_kernel(pt_ref, q_ref, bias_ref, eye_ref, *refs, npg):
    k_refs, v_refs = refs[:npg], refs[npg:2 * npg]
    o_ref, acc_ref, carry_ref = refs[2 * npg:]
    j = pl.program_id(1)

    @pl.when(j == 0)
    def _():
        acc_ref[...] = jnp.zeros_like(acc_ref)
        carry_ref[...] = jnp.zeros_like(carry_ref)

    q = q_ref[0]
    bias = bias_ref[...]
    ones = jnp.ones((DH, DH), bf16)
    rr = _iota((PAGE, 2 * PAGE), 0)
    cc = _iota((PAGE, 2 * PAGE), 1)
    tri = jnp.where((rr > cc) | (cc >= PAGE), 1.0, 0.0).astype(bf16)
    acc = acc_ref[...]
    run = carry_ref[...]
    pages = range(npg)
    zbs = [_dot((k_refs[p][0] * q[None]).reshape(PAGE * HEADS, DH).astype(bf16), ones)
           .reshape(PAGE, HEADS, DH) for p in pages]
    zs = [jnp.sum(zb * eye_ref[...], axis=0) + bias for zb in zbs]
    sps = [_softplus(z) for z in zs]
    css = []
    for sp in sps:
        hi, mid, lo = _split3(sp)
        css.append(_dot(hi, tri) + _dot(mid, tri) + _dot(lo, tri))
    ws = []
    for z, sp, cs in zip(zs, sps, css):
        ws.append(jnp.exp(z - sp - cs[:, :PAGE] - run))
        run = run + cs[:, PAGE:]
    wbs = [_dot((eye_ref[...] * w[None]).reshape(PAGE * HEADS, DH).astype(bf16), ones)
           .reshape(PAGE, HEADS, DH) for w in ws]
    for p, wb in zip(pages, wbs):
        acc = acc + jnp.sum(wb * v_refs[p][0], axis=0)
    acc_ref[...] = acc
    carry_ref[...] = run

    @pl.when(j == pl.num_programs(1) - 1)
    def _():
        o_ref[0] = acc


def sb_step(q_s, bias, cache_k, cache_v, page_table, npg=8):
    n, n_pages = page_table.shape
    q4 = (q_s * (DH ** -0.5)).reshape(n, HEADS, DH)
    bias4 = jnp.broadcast_to(bias[:, None], (HEADS, DH))
    eye3 = jnp.broadcast_to(jnp.eye(PAGE, DH, dtype=f32)[:, None, :], (PAGE, HEADS, DH))

    def page(p):
        return lambda b, j, pt: (pt[b, n_pages - 1 - (j * npg + p)], 0, 0, 0)

    kv_specs = [pl.BlockSpec((1, PAGE, HEADS, DH), page(p)) for p in range(npg)]
    grid_spec = pltpu.PrefetchScalarGridSpec(
        num_scalar_prefetch=1,
        grid=(n, n_pages // npg),
        in_specs=[
            pl.BlockSpec((1, HEADS, DH), lambda b, j, pt: (b, 0, 0)),
            pl.BlockSpec((HEADS, DH), lambda b, j, pt: (0, 0)),
            pl.BlockSpec((PAGE, HEADS, DH), lambda b, j, pt: (0, 0, 0)),
        ] + kv_specs + kv_specs,
        out_specs=pl.BlockSpec((1, HEADS, DH), lambda b, j, pt: (b, 0, 0)),
        scratch_shapes=[pltpu.VMEM((HEADS, DH), f32), pltpu.VMEM((HEADS, DH), f32)],
    )
    return pl.pallas_call(
        functools.partial(_sb_step_kernel, npg=npg),
        grid_spec=grid_spec,
        out_shape=jax.ShapeDtypeStruct((n, HEADS, DH), f32),
        compiler_params=_cparams("arbitrary", "arbitrary"),
        name="sb_step",
    )(page_table, q4, bias4, eye3, *([cache_k] * npg), *([cache_v] * npg)).reshape(n, ATT_W)


def _mlstm_kernel(q_ref, k_ref, v_ref, og_ref, gate_ref, gb_ref, gn_ref,
                  h_ref, c_out, n_out, m_out, c_s, n_s, m_s):
    ci = pl.program_id(1)
    L = CHUNK

    @pl.when(ci == 0)
    def _():
        c_s[...] = jnp.zeros_like(c_s)
        n_s[...] = jnp.zeros_like(n_s)
        m_s[...] = jnp.zeros_like(m_s)

    gt = gate_ref[...] + gb_ref[...]
    gt_t = gt.T
    ri = _iota((L, L), 0)
    li = _iota((L, L), 1)
    causal = li <= ri
    tri_incl = jnp.where(causal, 1.0, 0.0).astype(bf16)
    tri_incl_t = jnp.where(ri <= li, 1.0, 0.0).astype(bf16)
    for h in range(M_HEADS):
        sl = slice(h * M_DH, (h + 1) * M_DH)
        ig_col = gt[:, h:h + 1]
        ig_row = gt_t[h:h + 1, :]
        lf_col = _log_sigmoid_pair(gt[:, M_HEADS + h:M_HEADS + h + 1])[0]
        lf_row = _log_sigmoid_pair(gt_t[M_HEADS + h:M_HEADS + h + 1, :])[0]
        c_hi, c_mid, c_lo = _split3(jnp.broadcast_to(lf_col, (L, L)))
        bcum_col = _dot(tri_incl, c_hi) + _dot(tri_incl, c_mid) + _dot(tri_incl, c_lo)
        r_hi, r_mid, r_lo = _split3(jnp.broadcast_to(lf_row, (L, L)))
        bcum_row = _dot(r_hi, tri_incl_t) + _dot(r_mid, tri_incl_t) + _dot(r_lo, tri_incl_t)
        m_prev = m_s[h:h + 1, :]
        dlog = jnp.where(causal, bcum_col - bcum_row + ig_row, NEG_INF)
        inter = bcum_col + m_prev
        m_t = jnp.maximum(inter, jnp.max(dlog, axis=1, keepdims=True))
        dw = jnp.exp(dlog - m_t)
        iw = jnp.exp(inter - m_t)
        qh = q_ref[:, sl]
        kh = k_ref[:, sl] * (M_DH ** -0.5)
        vh = v_ref[:, sl]
        qb, kb, vb = qh.astype(bf16), kh.astype(bf16), vh.astype(bf16)
        sw = dw * _dot_nt(qb, kb)
        c_prev = c_s[h]
        n_prev = n_s[h:h + 1, :]
        iw_col = iw[:, 0:1]
        num = _dot(sw.astype(bf16), vb) + iw_col * _dot_nt(qb, c_prev.astype(bf16))
        qn = jnp.sum(qb.astype(f32) * n_prev.astype(bf16).astype(f32), axis=1, keepdims=True)
        den = jnp.sum(sw, axis=1, keepdims=True) + iw_col * qn
        m_col = m_t[:, 0:1]
        hout = num / jnp.maximum(jnp.abs(den), jnp.exp(-m_col))
        m_last = m_t[L - 1:L, :]
        b_last = bcum_col[L - 1:L, :]
        wl_col = jnp.exp(b_last[:, 0:1] - bcum_col[:, 0:1] + ig_col - m_last[:, 0:1])
        wl_row = jnp.exp(b_last - bcum_row[0:1, :] + ig_row - m_last)
        decay = jnp.exp(b_last + m_prev - m_last)
        dsc = decay[:, 0:1]
        c_s[h] = dsc * c_prev + _dot((vh * wl_col).T.astype(bf16), kb)
        wl8 = jnp.broadcast_to(wl_row, (8, L)).astype(bf16)
        n_s[h:h + 1, :] = dsc * n_prev + _dot(wl8, kb)[0:1, :]
        m_s[h:h + 1, :] = m_last
        hn = hout * lax.rsqrt(jnp.mean(hout * hout, axis=1, keepdims=True) + EPS)
        h_ref[:, sl] = hn * gn_ref[:, sl] * _sigmoid(og_ref[:, sl])

    @pl.when(ci == pl.num_programs(1) - 1)
    def _():
        c_out[0] = c_s[...]
        n_out[0] = n_s[...]
        m_out[0] = m_s[...]


def mlstm_prompt(proj2, gates, gate_bias, g_mnorm, bsz, seq):
    nc = seq // CHUNK
    col = lambda c: pl.BlockSpec((CHUNK, M_WIDTH), lambda b, i, c=c: (b * nc + i, c))
    hm, c1, n1, m1 = pl.pallas_call(
        _mlstm_kernel,
        grid=(bsz, nc),
        in_specs=[
            col(3), col(4), col(5), col(6),
            pl.BlockSpec((CHUNK, LANES), lambda b, i: (b * nc + i, 0)),
            pl.BlockSpec((1, LANES), lambda b, i: (0, 0)),
            pl.BlockSpec((1, M_WIDTH), lambda b, i: (0, 0)),
        ],
        out_specs=[
            pl.BlockSpec((CHUNK, M_WIDTH), lambda b, i: (b * nc + i, 0)),
            pl.BlockSpec((1, M_HEADS, M_DH, M_DH), lambda b, i: (b, 0, 0, 0)),
            pl.BlockSpec((1, M_HEADS, M_DH), lambda b, i: (b, 0, 0)),
            pl.BlockSpec((1, M_HEADS, LANES), lambda b, i: (b, 0, 0)),
        ],
        out_shape=[
            jax.ShapeDtypeStruct((bsz * seq, M_WIDTH), f32),
            jax.ShapeDtypeStruct((bsz, M_HEADS, M_DH, M_DH), f32),
            jax.ShapeDtypeStruct((bsz, M_HEADS, M_DH), f32),
            jax.ShapeDtypeStruct((bsz, M_HEADS, LANES), f32),
        ],
        scratch_shapes=[
            pltpu.VMEM((M_HEADS, M_DH, M_DH), f32),
            pltpu.VMEM((M_HEADS, M_DH), f32),
            pltpu.VMEM((M_HEADS, LANES), f32),
        ],
        compiler_params=_cparams("arbitrary", "arbitrary"),
        name="mlstm_prompt",
    )(proj2, proj2, proj2, proj2, gates, gate_bias, g_mnorm)
    return hm, c1, n1, m1[:, :, 0]


def _mlstm_step_kernel(q_ref, k_ref, v_ref, og_ref, gn_ref, ig_ref, fg_ref, c_ref, n_ref, m_ref,
                       h_ref, c_out, n_out, m_out):
    q = q_ref[0, 0]
    k = k_ref[0, 0] * (M_DH ** -0.5)
    v = v_ref[0, 0]
    ig = ig_ref[0, 0]
    lf = _log_sigmoid_pair(fg_ref[0, 0])[0]
    m0 = m_ref[0, 0]
    c0 = c_ref[0, 0]
    n0 = n_ref[0, 0]
    inter = lf + m0
    m_t = jnp.maximum(inter, ig)
    dw = jnp.exp(ig - m_t)
    iw = jnp.exp(inter - m_t)
    rnd = lambda a: a.astype(bf16).astype(f32)
    qr, kr, vr = rnd(q), rnd(k), rnd(v)
    qk = jnp.sum(qr * kr, axis=1, keepdims=True)
    sw = dw * qk
    cq = jnp.sum(rnd(c0) * qr, axis=1, keepdims=True)
    num = rnd(sw) * vr + iw * cq
    den = sw + iw * jnp.sum(rnd(n0) * qr, axis=1, keepdims=True)
    hout = num / jnp.maximum(jnp.abs(den), jnp.exp(-m_t))
    c_out[0, 0] = iw * c0 + rnd(dw * v) * kr
    n_out[0, 0] = iw * n0 + rnd(dw) * kr
    m_out[0, 0] = m_t
    hn = hout * lax.rsqrt(jnp.mean(hout * hout, axis=0, keepdims=True) + EPS)
    h_ref[0, 0] = hn * gn_ref[0] * _sigmoid(og_ref[0, 0])


def mlstm_step(proj2_s, gates_s, gate_bias, g_mnorm, c0, n0, m0):
    n = proj2_s.shape[0]
    base = 3 * ATT_W
    seg = lambda i: proj2_s[:, base + i * M_WIDTH: base + (i + 1) * M_WIDTH].reshape(n, M_HEADS, M_DH)
    g = gates_s + gate_bias
    rowb = pl.BlockSpec((1, 1, 1, M_DH), lambda b, h: (b, h, 0, 0))
    colb = pl.BlockSpec((1, 1, M_DH, 1), lambda b, h: (b, h, 0, 0))
    scal = pl.BlockSpec((1, 1, 1, 1), lambda b, h: (b, h, 0, 0))
    hcol, c1, n1, m1 = pl.pallas_call(
        _mlstm_step_kernel,
        grid=(n, M_HEADS),
        in_specs=[
            rowb, rowb, colb, colb,
            pl.BlockSpec((1, M_DH, 1), lambda b, h: (h, 0, 0)),
            scal, scal,
            pl.BlockSpec((1, 1, M_DH, M_DH), lambda b, h: (b, h, 0, 0)),
            rowb, scal,
        ],
        out_specs=[colb, pl.BlockSpec((1, 1, M_DH, M_DH), lambda b, h: (b, h, 0, 0)), rowb, scal],
        out_shape=[
            jax.ShapeDtypeStruct((n, M_HEADS, M_DH, 1), f32),
            jax.ShapeDtypeStruct((n, M_HEADS, M_DH, M_DH), f32),
            jax.ShapeDtypeStruct((n, M_HEADS, 1, M_DH), f32),
            jax.ShapeDtypeStruct((n, M_HEADS, 1, 1), f32),
        ],
        compiler_params=_cparams("arbitrary", "arbitrary"),
        name="mlstm_step",
    )(seg(0)[:, :, None, :], seg(1)[:, :, None, :], seg(2)[..., None], seg(3)[..., None],
      g_mnorm.reshape(M_HEADS, M_DH, 1),
      g[:, 0:M_HEADS].reshape(n, M_HEADS, 1, 1), g[:, M_HEADS:2 * M_HEADS].reshape(n, M_HEADS, 1, 1),
      c0, n0[:, :, None, :], m0.reshape(n, M_HEADS, 1, 1))
    return hcol.reshape(n, M_WIDTH), c1, n1.reshape(n, M_HEADS, M_DH), m1.reshape(n, M_HEADS)


def _peerq_kernel(x_ref, g_ref, sh_ref, sc_ref, wt_ref, qt_ref, xmt_ref, xs_ref):
    @pl.when(pl.program_id(1) == 0)
    def _():
        xm = _modulated(x_ref[...], g_ref[...], sh_ref[0], sc_ref[0])
        xt = xm.T.astype(bf16)
        xs_ref[...] = xt
        xmt_ref[...] = xt

    qt_ref[...] = _dot(wt_ref[...], xs_ref[...])


def peer_query(x, g, shift, scale, wq_t, rows_per_batch, tm, tn=512):
    m, k = x.shape
    n = wq_t.shape[0]
    sh, sh_spec = _mod_specs(shift, m, tm, k, rows_per_batch)
    sc, sc_spec = _mod_specs(scale, m, tm, k, rows_per_batch)
    return pl.pallas_call(
        _peerq_kernel,
        grid=(m // tm, n // tn),
        in_specs=[
            pl.BlockSpec((tm, k), lambda i, j: (i, 0)),
            pl.BlockSpec((1, k), lambda i, j: (0, 0)),
            sh_spec, sc_spec,
            pl.BlockSpec((tn, k), lambda i, j: (j, 0)),
        ],
        out_specs=[pl.BlockSpec((tn, tm), lambda i, j: (j, i)),
                   pl.BlockSpec((k, tm), lambda i, j: (0, i))],
        out_shape=[jax.ShapeDtypeStruct((n, m), f32), jax.ShapeDtypeStruct((k, m), bf16)],
        scratch_shapes=[pltpu.VMEM((k, tm), bf16)],
        compiler_params=_cparams("arbitrary", "arbitrary"),
        name="peer_query",
    )(x, g.reshape(1, k), sh, sc, wq_t)


def _heads_layout_kernel(*refs):
    n = len(refs) // 2
    for x_ref, o_ref in zip(refs[:n], refs[n:]):
        for hb in range(HEADS):
            o_ref[:, hb, :] = x_ref[:, hb * DH:(hb + 1) * DH]


def heads_layout(srcs, bsz, seq, tail, tt=512):
    nt, first = tail // tt, (seq - tail) // tt
    rows = lambda b, i: b * (seq // tt) + first + i
    return pl.pallas_call(
        _heads_layout_kernel,
        grid=(bsz, nt),
        in_specs=[pl.BlockSpec((tt, ATT_W), lambda b, i, c=c: (rows(b, i), c)) for _, c in srcs],
        out_specs=[pl.BlockSpec((tt, HEADS, DH), lambda b, i: (b * nt + i, 0, 0))] * len(srcs),
        out_shape=[jax.ShapeDtypeStruct((bsz * tail, HEADS, DH), f32)] * len(srcs),
        compiler_params=_cparams("arbitrary", "arbitrary"),
        name="heads_layout",
    )(*[x for x, _ in srcs])


def _cast_kernel(w_ref, o_ref):
    o_ref[...] = w_ref[0].astype(bf16)


def cast_layer_bf16(w, layer, tr=1024):
    _, rows, cols = w.shape
    return pl.pallas_call(
        _cast_kernel,
        grid=(rows // tr,),
        in_specs=[pl.BlockSpec((1, tr, cols), lambda i: (layer, i, 0))],
        out_specs=pl.BlockSpec((tr, cols), lambda i: (i, 0)),
        out_shape=jax.ShapeDtypeStruct((rows, cols), bf16),
        compiler_params=_cparams("arbitrary"),
        name="cast_bf16",
    )(w)


_N_TOP = PEER_TOPK + 1
_CAND_PAIRS = [(a, b) for a in range(_N_TOP) for b in range(_N_TOP) if (a + 1) * (b + 1) <= _N_TOP]
_N_CAND = -(-len(_CAND_PAIRS) // 8) * 8


def _extract_top(cur, ridx, n):
    vals = []
    big = float(cur.shape[0])
    for _ in range(n):
        mx = jnp.max(cur, axis=0, keepdims=True)
        first = jnp.min(jnp.where(cur == mx, ridx, big), axis=0, keepdims=True)
        cur = jnp.where(ridx == first, NEG_INF, cur)
        vals.append(mx)
    return vals


def _oddeven_mergesort_pairs(n):
    pairs = []

    def merge(lo, cnt, r):
        step = 2 * r
        if step < cnt:
            merge(lo, cnt, step)
            merge(lo + r, cnt, step)
            pairs.extend((i, i + r) for i in range(lo + r, lo + cnt - r, step))
        else:
            pairs.append((lo, lo + r))

    def sort(lo, cnt):
        if cnt > 1:
            sort(lo, cnt // 2)
            sort(lo + cnt // 2, cnt // 2)
            merge(lo, cnt, 1)

    sort(0, n)
    return pairs


_SORT16 = _oddeven_mergesort_pairs(PEER_TOPK)


def _exchange(y, i, j):
    y[i], y[j] = jnp.maximum(y[i], y[j]), jnp.minimum(y[i], y[j])


def _top17_network(s):
    n = PEER_TOPK
    y = [s[v * 8:(v + 1) * 8, :] for v in range(n)]
    for i, j in _SORT16:
        _exchange(y, i, j)
    dropped = None
    for shift in (4, 2, 1):
        other = [pltpu.roll(y[n - 1 - i], shift, 0) for i in range(n)]
        low = functools.reduce(jnp.maximum, [jnp.minimum(y[i], other[i]) for i in range(n)])
        dropped = low if dropped is None else jnp.maximum(dropped, low)
        y = [jnp.maximum(y[i], other[i]) for i in range(n)]
        dist = n // 2
        while dist:
            for i in range(n):
                if not i & dist:
                    _exchange(y, i, i + dist)
            dist //= 2
    return [v[0:1, :] for v in y] + [jnp.max(dropped, axis=0, keepdims=True)]


_ROUTER_UNROLL = 8


def _router_kernel(qt_ref, keys_ref, e1_ref, e2_ref, th_ref, cand_ref):
    cidx = _iota((_N_CAND, LANES), 0).astype(f32)
    k0 = keys_ref[0].astype(bf16)
    k1 = keys_ref[1].astype(bf16)
    cand_ref[...] = jnp.full(cand_ref.shape, NEG_INF, f32)

    def one_head(h, slot):
        r0 = pl.multiple_of(h * 2 * N_KEYS, 2 * N_KEYS)
        s1 = _dot(k0, qt_ref[pl.ds(r0, N_KEYS), :].astype(bf16))
        s2 = _dot(k1, qt_ref[pl.ds(r0 + N_KEYS, N_KEYS), :].astype(bf16))
        top_a = _top17_network(s1)
        top_b = _top17_network(s2)
        for ci, (a, b) in enumerate(_CAND_PAIRS):
            cand_ref[slot, ci:ci + 1, :] = top_a[a] + top_b[b]
        cs = _extract_top(cand_ref[slot], cidx, _N_TOP)
        z = jnp.ones_like(cs[0])
        for r in range(1, PEER_TOPK):
            z = z + jnp.exp(cs[r] - cs[0])
        inv_z = 1.0 / z
        mid = 0.5 * (cs[PEER_TOPK - 1] + cs[PEER_TOPK])
        o0 = pl.multiple_of(h * N_KEYS, N_KEYS)
        e1 = jnp.exp(s1 - top_a[0]) * inv_z
        e1_ref[:, h] = e1.reshape(N_KEYS // SUBLANES, SUBLANES, LANES)
        e2_ref[pl.ds(o0, N_KEYS), :] = jnp.exp(s2 - top_b[0])
        th_ref[pl.ds(h, 1), :] = jnp.exp(mid - cs[0]) * inv_z

    def body(hp, _):
        for slot in range(_ROUTER_UNROLL):
            one_head(hp * _ROUTER_UNROLL + slot, slot)
        return 0

    lax.fori_loop(0, PEER_HEADS // _ROUTER_UNROLL, body, 0)


def peer_router(qt, keys):
    n, m = qt.shape
    half = PEER_HEADS * N_KEYS
    return pl.pallas_call(
        _router_kernel,
        grid=(m // LANES,),
        in_specs=[pl.BlockSpec((n, LANES), lambda i: (0, i)),
                  pl.BlockSpec((2, N_KEYS, N_KEYS), lambda i: (0, 0, 0))],
        out_specs=[pl.BlockSpec((N_KEYS // SUBLANES, PEER_HEADS, SUBLANES, LANES), lambda i: (0, 0, 0, i)),
                   pl.BlockSpec((half, LANES), lambda i: (0, i)),
                   pl.BlockSpec((PEER_HEADS, LANES), lambda i: (0, i))],
        out_shape=[jax.ShapeDtypeStruct((N_KEYS // SUBLANES, PEER_HEADS, SUBLANES, m), f32),
                   jax.ShapeDtypeStruct((half, m), f32),
                   jax.ShapeDtypeStruct((PEER_HEADS, m), f32)],
        scratch_shapes=[pltpu.VMEM((_ROUTER_UNROLL, _N_CAND, LANES), f32)],
        compiler_params=_cparams("arbitrary"),
        name="peer_router",
    )(qt, keys)


_G_CHAINS = 3
_G_ROWS = SUBLANES


class _Chains:
    def __init__(self, n, zero_ref):
        self.deps, self.cnt, self.zero_ref = [None] * n, 0, zero_ref

    def take(self):
        return self.deps[self.cnt % len(self.deps)]

    def put(self, g):
        bits = lax.bitcast_convert_type(g[0:1, :], jnp.int32) & self.zero_ref[0:1, :]
        self.deps[self.cnt % len(self.deps)] = lax.bitcast_convert_type(bits, f32)
        self.cnt += 1


def _routing_block(e1_ref, e2_ref, th_ref, ii, ls, chains):
    sub = _G_ROWS
    blk, r = divmod(ii, SUBLANES)
    rows = [e1_ref[blk, h, r:r + 1, ls] for h in range(PEER_HEADS)]
    ths = [th_ref[h:h + 1, ls] for h in range(PEER_HEADS)]
    parts = []
    for gi in range(N_KEYS // sub):
        dep = chains.take()
        g = jnp.zeros((sub, LANES), f32)
        for h in range(PEER_HEADS):
            row = rows[h] if (dep is None or h) else rows[h] + dep
            pr = e2_ref[h * N_KEYS + gi * sub:h * N_KEYS + (gi + 1) * sub, ls] * row
            g = g + jnp.where(pr >= ths[h], pr, 0.0)
        chains.put(g)
        parts.append(g)
    return jnp.concatenate(parts, axis=0)


def _experts_kernel(xmt_ref, e1_ref, e2_ref, th_ref, u_ref, v_ref, x_ref, gt_ref, gf_ref, zero_ref,
                    o_ref, w_ref, *, final_norm):
    c = pl.program_id(1)
    tm, te = w_ref.shape

    @pl.when(c == 0)
    def _():
        o_ref[...] = jnp.zeros_like(o_ref)

    chains = _Chains(_G_CHAINS, zero_ref)
    act = _gelu(_dot(u_ref[...], xmt_ref[...]))
    for ii in range(te // N_KEYS):
        es = slice(ii * N_KEYS, (ii + 1) * N_KEYS)
        for lc in range(tm // LANES):
            ls = slice(lc * LANES, (lc + 1) * LANES)
            g_blk = _routing_block(e1_ref, e2_ref, th_ref, ii, ls, chains)
            w_ref[ls, es] = (g_blk * act[es, ls]).T.astype(bf16)
    o_ref[...] += _dot(w_ref[...], v_ref[...])

    @pl.when(c == pl.num_programs(1) - 1)
    def _():
        y = x_ref[...] + gt_ref[0] * o_ref[...]
        if final_norm:
            ms = jnp.mean(y * y, axis=-1, keepdims=True)
            y = y * lax.rsqrt(ms + EPS) * gf_ref[...]
        o_ref[...] = y


def peer_experts(xmt, e1t, e2t, th, u, v, x, gate, g_final, rows_per_batch, tm, te, final_norm):
    m, d = x.shape
    gt, gt_spec = _mod_specs(gate, m, tm, d, rows_per_batch)
    half = PEER_HEADS * N_KEYS
    return pl.pallas_call(
        functools.partial(_experts_kernel, final_norm=final_norm),
        grid=(m // tm, N_EXPERTS // te),
        in_specs=[
            pl.BlockSpec((d, tm), lambda i, c: (0, i)),
            pl.BlockSpec((te // N_KEYS // SUBLANES, PEER_HEADS, SUBLANES, tm), lambda i, c: (c, 0, 0, i)),
            pl.BlockSpec((half, tm), lambda i, c: (0, i)),
            pl.BlockSpec((PEER_HEADS, tm), lambda i, c: (0, i)),
            pl.BlockSpec((te, d), lambda i, c: (c, 0)),
            pl.BlockSpec((te, d), lambda i, c: (c, 0)),
            pl.BlockSpec((tm, d), lambda i, c: (i, 0)),
            gt_spec,
            pl.BlockSpec((1, d), lambda i, c: (0, 0)),
            pl.BlockSpec((SUBLANES, LANES), lambda i, c: (0, 0)),
        ],
        out_specs=pl.BlockSpec((tm, d), lambda i, c: (i, 0)),
        out_shape=jax.ShapeDtypeStruct((m, d), f32),
        scratch_shapes=[pltpu.VMEM((tm, te), bf16)],
        compiler_params=_cparams("arbitrary", "arbitrary"),
        name="peer_experts",
    )(xmt, e1t, e2t, th, u, v, x, gt, g_final.reshape(1, d), jnp.zeros((SUBLANES, LANES), jnp.int32))


def peer_block(x, g, shift, scale, gate, wq_t, keys, u, v, g_final, rows_per_batch, tm, te, final_norm):
    tm_q = 1024 if x.shape[0] % 1024 == 0 and rows_per_batch % 1024 == 0 else tm
    qt, xmt = peer_query(x, g, shift, scale, wq_t, rows_per_batch, tm_q)
    e1t, e2t, th = peer_router(qt, keys)
    return peer_experts(xmt, e1t, e2t, th, u, v, x, gate, g_final, rows_per_batch, tm, te, final_norm)


def kernel(x_prompt, x_sample, c_prompt, c_sample, state_rglru_conv, state_rglru_h, cache_swa_k, cache_swa_v, cache_sb_k, cache_sb_v, state_mlstm_C, state_mlstm_n, state_mlstm_m, page_table, w_ada, b_ada, g_norm_mix, g_norm_ffn, e_w_in, e_conv_w, e_conv_b, e_w_r, e_b_r, e_w_i, e_b_i, e_lambda, e_w_out, o_w_in, o_b_if, o_sb_bias, o_g_mnorm, o_w_out, peer_w_q, peer_keys, peer_u, peer_v, g_final):
    bp, seq, d = x_prompt.shape
    bs = x_sample.shape[0]
    mp = bp * seq
    pad_s = LANES
    xp = x_prompt.reshape(mp, d)
    xs = x_sample.reshape(bs, d)

    c_rows = 16
    c_all = jnp.concatenate([c_prompt, c_sample, jnp.zeros((c_rows - bp - bs, d), f32)], axis=0)
    mod = adaln_all(c_all, w_ada, b_ada)

    def mods(layer):
        parts = [mod[layer, :, i * d:(i + 1) * d] for i in range(6)]
        return [p[:bp] for p in parts], [p[bp:bp + bs] for p in parts]

    ctab_p, stab_p = _rope_tables(jnp.arange(seq, dtype=jnp.int32))
    ctab_s, stab_s = _rope_tables(jnp.full((1,), PAST_LEN, jnp.int32))

    TM = 512
    m_p, m_s = mods(0)
    w_in = e_w_in[0].astype(bf16)
    w_out = e_w_out[0].astype(bf16)
    cw, cb = e_conv_w[0], e_conv_b[0].reshape(1, -1)
    wr, wi = e_w_r[0].astype(bf16), e_w_i[0].astype(bf16)
    br, bi, lam = e_b_r[0].reshape(1, -1), e_b_i[0].reshape(1, -1), e_lambda[0].reshape(1, -1)

    proj_p = mod_matmul(xp, g_norm_mix[0], m_p[0], m_p[1], w_in, E_IN, seq, 1024, 1024, name="e_in_p")
    proj_s = mod_matmul(xs, g_norm_mix[0], m_s[0], m_s[1], w_in, E_IN, 1, bs, 512, name="e_in_s")

    ya_p, h_p = rglru_prompt(proj_p, bp, seq, cw, cb, wr, br, wi, bi, lam)
    ya_s, h_s = rglru_step(proj_s, state_rglru_conv[0], state_rglru_h[0], cw, cb, wr, br, wi, bi, lam)
    conv_p = proj_p.reshape(bp, seq, E_IN)[:, seq - (CONV_W - 1):, :RG_WIDTH]
    conv_s = jnp.concatenate([state_rglru_conv[0][:, 1:], proj_s[:, None, :RG_WIDTH]], axis=1)

    qkv_p, k_p = rope_split(proj_p, ctab_p, stab_p, bp, seq)
    q_s, k_s, v_s = rope_qkv(proj_s, ctab_s, stab_s, 1, bs)
    o_p = dilated_prompt(qkv_p, bp, seq)
    win = cache_swa_k.shape[2]
    o_s = dilated_step(q_s, k_s, v_s, cache_swa_k.reshape(-1, win, HEADS, DH)[:bs],
                       cache_swa_v.reshape(-1, win, HEADS, DH)[:bs])
    wl = min(2048, seq)
    swa_k_p, swa_v_p = [a.reshape(bp, wl, HEADS, DH)
                        for a in heads_layout([(k_p, 0), (proj_p, E_IN // ATT_W - 1)], bp, seq, wl)]
    swa_k_s = k_s.reshape(bs, 1, HEADS, DH)
    swa_v_s = v_s.reshape(bs, 1, HEADS, DH)

    xp = out_proj(ya_p, o_p, w_out, xp, m_p[2], seq, 1024, 1024, name="e_out_p")
    xs = out_proj(ya_s, o_s, w_out, xs, m_s[2], 1, bs, 512, name="e_out_s")

    def peer_layer(layer, xp, xs, m_p, m_s, final_norm):
        wq_t = peer_w_q[layer].T.astype(bf16)
        u = cast_layer_bf16(peer_u, layer)
        v = cast_layer_bf16(peer_v, layer)
        xp = peer_block(xp, g_norm_ffn[layer], m_p[3], m_p[4], m_p[5], wq_t, peer_keys[layer], u, v,
                        g_final, seq, TM, 1024, final_norm)
        xs_pad = jnp.pad(xs, ((0, pad_s - bs), (0, 0)))
        xs_new = peer_block(xs_pad, g_norm_ffn[layer], m_s[3], m_s[4], m_s[5], wq_t, peer_keys[layer],
                            u, v, g_final, 1, pad_s, 1024, final_norm)
        return xp, xs_new[:bs]

    xp, xs = peer_layer(0, xp, xs, m_p, m_s, False)

    m_p, m_s = mods(1)
    w_in2 = o_w_in[0].astype(bf16)
    w_gate = jnp.pad(w_in2[:, O_MAIN:], ((0, 0), (0, LANES - 2 * M_HEADS)))
    w_out2 = o_w_out[0].astype(bf16)
    gate_bias = jnp.pad(o_b_if[0].reshape(1, 2 * M_HEADS), ((0, 0), (0, LANES - 2 * M_HEADS)))
    gmn = o_g_mnorm[0].reshape(1, M_WIDTH)

    proj2_p, gates_p = mod_matmul(xp, g_norm_mix[1], m_p[0], m_p[1], w_in2, O_MAIN, seq, 1024, 1024,
                                  w_gate=w_gate, name="o_in_p")
    proj2_s, gates_s = mod_matmul(xs, g_norm_mix[1], m_s[0], m_s[1], w_in2, O_MAIN, 1, bs, 512,
                                  w_gate=w_gate, name="o_in_s")

    oc_p = sb_prompt(proj2_p, o_sb_bias[0], bp, seq)
    oc_s = sb_step(proj2_s[:, :ATT_W], o_sb_bias[0], cache_sb_k.reshape(-1, PAGE, HEADS, DH),
                   cache_sb_v.reshape(-1, PAGE, HEADS, DH), page_table)
    hm_p, mC_p, mn_p, mm_p = mlstm_prompt(proj2_p, gates_p, gate_bias, gmn, bp, seq)
    hm_s, mC_s, mn_s, mm_s = mlstm_step(proj2_s, gates_s, gate_bias, gmn,
                                        state_mlstm_C.reshape(-1, M_HEADS, M_DH, M_DH)[:bs],
                                        state_mlstm_n[0], state_mlstm_m[0])

    n_pg = seq // PAGE
    sb_k_p, sb_v_p = [a.reshape(bp, n_pg, PAGE, HEADS, DH)
                      for a in heads_layout([(proj2_p, 1), (proj2_p, 2)], bp, seq, seq)]
    sb_k_s = proj2_s[:, ATT_W:2 * ATT_W].reshape(bs, 1, HEADS, DH)
    sb_v_s = proj2_s[:, 2 * ATT_W:3 * ATT_W].reshape(bs, 1, HEADS, DH)

    xp = out_proj(oc_p, hm_p, w_out2, xp, m_p[2], seq, 1024, 1024, name="o_out_p")
    xs = out_proj(oc_s, hm_s, w_out2, xs, m_s[2], 1, bs, 512, name="o_out_s")
    xp, xs = peer_layer(1, xp, xs, m_p, m_s, True)

    y_prompt = xp.reshape(bp, seq, d)
    y_sample = xs.reshape(bs, 1, d)
    st = lambda a: a[None]
    return (y_prompt, y_sample, st(conv_p), st(conv_s), st(h_p.reshape(bp, RG_WIDTH)), st(h_s),
            st(swa_k_p), st(swa_k_s), st(swa_v_p), st(swa_v_s),
            st(sb_k_p), st(sb_k_s), st(sb_v_p), st(sb_v_s),
            st(mC_p), st(mC_s), st(mn_p), st(mn_s), st(mm_p), st(mm_s))
```

```python
import functools
import math

import jax
import jax.numpy as jnp
from jax import lax
from jax.experimental import pallas as pl
from jax.experimental.pallas import tpu as pltpu

f32 = jnp.float32
bf16 = jnp.bfloat16

D_MODEL = 2048
PAST_LEN = 16384
PAGE = 128
RG_WIDTH = 1024
RG_BLOCKS = 8
CONV_W = 4
RG_C = 8.0
HEADS = 8
DH = 128
ATT_W = HEADS * DH
DIL_PATTERNS = ((128, 1), (512, 4), (2048, 16))
ROT_DIMS = 32
ROPE_THETA = 500000.0
M_HEADS = 4
M_DH = 256
M_WIDTH = M_HEADS * M_DH
CHUNK = 128
E_IN = 2 * RG_WIDTH + 3 * ATT_W
O_MAIN = 3 * ATT_W + 4 * M_WIDTH
PEER_HEADS = 8
N_KEYS = 128
N_EXPERTS = N_KEYS * N_KEYS
PEER_TOPK = 16
EPS = 1e-6
LANES = 128
SUBLANES = 8
VMEM_LIMIT = 56 * 1024 * 1024
NEG_INF = float("-inf")


def _cparams(*sem):
    return pltpu.CompilerParams(dimension_semantics=sem, vmem_limit_bytes=VMEM_LIMIT)


def _dot(a, b):
    return jnp.dot(a, b, preferred_element_type=f32)


def _dot_nt(a, b):
    return lax.dot_general(a, b, (((1,), (1,)), ((), ())), preferred_element_type=f32)


def _split3(x):
    hi = x.astype(bf16)
    r = x - hi.astype(f32)
    mid = r.astype(bf16)
    lo = (r - mid.astype(f32)).astype(bf16)
    return hi, mid, lo


def _sigmoid(x):
    return 1.0 / (1.0 + jnp.exp(-x))


def _log_sigmoid_pair(z):
    l1p = jnp.log1p(jnp.exp(-jnp.abs(z)))
    return jnp.minimum(z, 0.0) - l1p, -jnp.maximum(z, 0.0) - l1p


def _softplus(z):
    return jnp.maximum(z, 0.0) + jnp.log(1.0 + jnp.exp(-jnp.abs(z)))


def _gelu(x):
    c = math.sqrt(2.0 / math.pi)
    h = 0.5 * x
    return h + h * jnp.tanh(x * (c + (c * 0.044715) * (x * x)))


def _iota(shape, dim):
    return lax.broadcasted_iota(jnp.int32, shape, dim)


def _rowsum_bcast(x):
    ones = jnp.ones((LANES, LANES), bf16)
    hi = x.astype(bf16)
    lo = (x - hi.astype(f32)).astype(bf16)
    return _dot(hi, ones) + _dot(lo, ones)


def _adaln_kernel(c_ref, w_ref, b_ref, o_ref):
    c = c_ref[...]
    s = c * _sigmoid(c)
    w = w_ref[0]
    s_hi = s.astype(bf16)
    s_lo = (s - s_hi.astype(f32)).astype(bf16)
    w_hi = w.astype(bf16)
    w_lo = (w - w_hi.astype(f32)).astype(bf16)
    o_ref[0] = _dot(s_hi, w_hi) + _dot(s_hi, w_lo) + _dot(s_lo, w_hi) + b_ref[0]


def adaln_all(c_all, w_ada, b_ada):
    depth, d, n = w_ada.shape
    rows = c_all.shape[0]
    tn = 1024
    return pl.pallas_call(
        _adaln_kernel,
        grid=(depth, n // tn),
        in_specs=[
            pl.BlockSpec((rows, d), lambda l, j: (0, 0)),
            pl.BlockSpec((1, d, tn), lambda l, j: (l, 0, j)),
            pl.BlockSpec((1, 1, tn), lambda l, j: (l, 0, j)),
        ],
        out_specs=pl.BlockSpec((1, rows, tn), lambda l, j: (l, 0, j)),
        out_shape=jax.ShapeDtypeStruct((depth, rows, n), f32),
        compiler_params=_cparams("arbitrary", "arbitrary"),
        name="adaln",
    )(c_all, w_ada, b_ada.reshape(depth, 1, n))


def _modulated(x, g, sh, sc):
    ms = jnp.mean(x * x, axis=-1, keepdims=True)
    y = x * lax.rsqrt(ms + EPS) * g
    return y * (1.0 + sc) + sh


def _modmm_kernel(x_ref, g_ref, sh_ref, sc_ref, w_ref, o_ref, xn_ref):
    @pl.when(pl.program_id(1) == 0)
    def _():
        xn_ref[...] = _modulated(x_ref[...], g_ref[...], sh_ref[0], sc_ref[0]).astype(bf16)

    o_ref[...] = _dot(xn_ref[...], w_ref[...])


def _modmm_gate_kernel(x_ref, g_ref, sh_ref, sc_ref, w_ref, wg_ref, o_ref, og_ref, xn_ref):
    @pl.when(pl.program_id(1) == 0)
    def _():
        xn = _modulated(x_ref[...], g_ref[...], sh_ref[0], sc_ref[0]).astype(bf16)
        xn_ref[...] = xn
        og_ref[...] = _dot(xn, wg_ref[...])

    o_ref[...] = _dot(xn_ref[...], w_ref[...])


def _mod_specs(mod, m, tm, k, rows_per_batch):
    if rows_per_batch >= tm:
        assert rows_per_batch % tm == 0
        per = rows_per_batch // tm
        return mod.reshape(-1, 1, k), pl.BlockSpec((1, 1, k), lambda i, j: (i // per, 0, 0))
    assert rows_per_batch == 1
    rows = mod
    if rows.shape[0] < m:
        rows = jnp.pad(rows, ((0, m - rows.shape[0]), (0, 0)))
    return rows.reshape(1, m, k), pl.BlockSpec((1, tm, k), lambda i, j: (0, i, 0))


def mod_matmul(x, g, shift, scale, w, n_out, rows_per_batch, tm, tn, w_gate=None, name="modmm"):
    m, k = x.shape
    sh, sh_spec = _mod_specs(shift, m, tm, k, rows_per_batch)
    sc, sc_spec = _mod_specs(scale, m, tm, k, rows_per_batch)
    in_specs = [
        pl.BlockSpec((tm, k), lambda i, j: (i, 0)),
        pl.BlockSpec((1, k), lambda i, j: (0, 0)),
        sh_spec,
        sc_spec,
        pl.BlockSpec((k, tn), lambda i, j: (0, j)),
    ]
    args = [x, g.reshape(1, k), sh, sc, w]
    out_specs = pl.BlockSpec((tm, tn), lambda i, j: (i, j))
    out_shape = jax.ShapeDtypeStruct((m, n_out), f32)
    kern = _modmm_kernel
    if w_gate is not None:
        in_specs.append(pl.BlockSpec((k, LANES), lambda i, j: (0, 0)))
        args.append(w_gate)
        out_specs = [out_specs, pl.BlockSpec((tm, LANES), lambda i, j: (i, 0))]
        out_shape = [out_shape, jax.ShapeDtypeStruct((m, LANES), f32)]
        kern = _modmm_gate_kernel
    return pl.pallas_call(
        kern,
        grid=(m // tm, n_out // tn),
        in_specs=in_specs,
        out_specs=out_specs,
        out_shape=out_shape,
        scratch_shapes=[pltpu.VMEM((tm, k), bf16)],
        compiler_params=_cparams("arbitrary", "arbitrary"),
        name=name,
    )(*args)


def _outproj_kernel(a1_ref, a2_ref, w1_ref, w2_ref, x_ref, gt_ref, o_ref):
    y = _dot(a1_ref[...].astype(bf16), w1_ref[...]) + _dot(a2_ref[...].astype(bf16), w2_ref[...])
    o_ref[...] = x_ref[...] + gt_ref[0] * y


def out_proj(a1, a2, w, x, gate, rows_per_batch, tm, tn, name="outproj"):
    m, k1 = a1.shape
    k2 = a2.shape[1]
    n = w.shape[1]
    gt, gt_spec = _mod_specs(gate, m, tm, n, rows_per_batch)
    if gt.shape[1] == 1:
        per = rows_per_batch // tm
        gt_spec = pl.BlockSpec((1, 1, tn), lambda i, j: (i // per, 0, j))
    else:
        gt_spec = pl.BlockSpec((1, tm, tn), lambda i, j: (0, i, j))
    return pl.pallas_call(
        _outproj_kernel,
        grid=(m // tm, n // tn),
        in_specs=[
            pl.BlockSpec((tm, k1), lambda i, j: (i, 0)),
            pl.BlockSpec((tm, k2), lambda i, j: (i, 0)),
            pl.BlockSpec((k1, tn), lambda i, j: (0, j)),
            pl.BlockSpec((k2, tn), lambda i, j: (k1 // k2, j)),
            pl.BlockSpec((tm, tn), lambda i, j: (i, j)),
            gt_spec,
        ],
        out_specs=pl.BlockSpec((tm, tn), lambda i, j: (i, j)),
        out_shape=jax.ShapeDtypeStruct((m, n), f32),
        compiler_params=_cparams("arbitrary", "arbitrary"),
        name=name,
    )(a1, a2, w, w, x, gt)


def _rglru_gates(xc, wr_ref, br, wi_ref, bi, lam):
    xb = xc.astype(bf16)
    rs, gs = [], []
    for hb in range(RG_BLOCKS):
        sl = slice(hb * LANES, (hb + 1) * LANES)
        rs.append(_dot(xb[:, sl], wr_ref[hb]))
        gs.append(_dot(xb[:, sl], wi_ref[hb]))
    r = _sigmoid(jnp.concatenate(rs, axis=1) + br)
    ig = _sigmoid(jnp.concatenate(gs, axis=1) + bi)
    softplus_neg_lam = jnp.maximum(-lam, 0.0) + jnp.log1p(jnp.exp(-jnp.abs(lam)))
    log_a = -RG_C * r * softplus_neg_lam
    a = jnp.exp(log_a)
    u = jnp.sqrt(-jnp.tanh(log_a) * (a * a + 1.0)) * ig * xc
    return a, u


def _rglru_kernel(xa_ref, ga_ref, cw_ref, cb_ref, wr_ref, br_ref, wi_ref, bi_ref, lam_ref,
                  ya_ref, hl_ref, xprev_ref, hc_ref):
    t_idx = pl.program_id(1)
    tt = xa_ref.shape[0]

    @pl.when(t_idx == 0)
    def _():
        xprev_ref[...] = jnp.zeros_like(xprev_ref)
        hc_ref[...] = jnp.zeros_like(hc_ref)

    xa = xa_ref[...]
    xprev = xprev_ref[...]
    row8 = _iota((8, RG_WIDTH), 0)
    xc = cb_ref[...] + cw_ref[CONV_W - 1:CONV_W, :] * xa
    for k in range(1, CONV_W):
        rolled = pltpu.roll(xa, k, 0)
        head = jnp.where(row8 < k, pltpu.roll(xprev, k, 0), rolled[0:8])
        shifted = jnp.concatenate([head, rolled[8:]], axis=0)
        xc = xc + cw_ref[CONV_W - 1 - k:CONV_W - k, :] * shifted
    xprev_ref[...] = xa[tt - 8:tt]

    a, u = _rglru_gates(xc, wr_ref, br_ref[...], wi_ref, bi_ref[...], lam_ref[...])
    row = _iota((tt, RG_WIDTH), 0)
    s = 1
    while s < tt:
        a_sh = pltpu.roll(a, s, 0)
        u_sh = pltpu.roll(u, s, 0)
        ok = row >= s
        u = jnp.where(ok, a * u_sh + u, u)
        a = jnp.where(ok, a * a_sh, a)
        s *= 2
    h = a * hc_ref[...] + u
    hc_ref[...] = h[tt - 1:tt]
    hl_ref[0] = h[tt - 1:tt]
    ya_ref[...] = h * _gelu(ga_ref[...])


def rglru_prompt(proj, bsz, seq, cw, cb, wr, br, wi, bi, lam, tt=256):
    nt = seq // tt
    vec = lambda: pl.BlockSpec((1, RG_WIDTH), lambda b, t: (0, 0))
    return pl.pallas_call(
        _rglru_kernel,
        grid=(bsz, nt),
        in_specs=[
            pl.BlockSpec((tt, RG_WIDTH), lambda b, t: (b * nt + t, 0)),
            pl.BlockSpec((tt, RG_WIDTH), lambda b, t: (b * nt + t, 1)),
            pl.BlockSpec((CONV_W, RG_WIDTH), lambda b, t: (0, 0)),
            vec(),
            pl.BlockSpec((RG_BLOCKS, LANES, LANES), lambda b, t: (0, 0, 0)),
            vec(),
            pl.BlockSpec((RG_BLOCKS, LANES, LANES), lambda b, t: (0, 0, 0)),
            vec(),
            vec(),
        ],
        out_specs=[
            pl.BlockSpec((tt, RG_WIDTH), lambda b, t: (b * nt + t, 0)),
            pl.BlockSpec((1, 1, RG_WIDTH), lambda b, t: (b, 0, 0)),
        ],
        out_shape=[
            jax.ShapeDtypeStruct((bsz * seq, RG_WIDTH), f32),
            jax.ShapeDtypeStruct((bsz, 1, RG_WIDTH), f32),
        ],
        scratch_shapes=[pltpu.VMEM((8, RG_WIDTH), f32), pltpu.VMEM((1, RG_WIDTH), f32)],
        compiler_params=_cparams("arbitrary", "arbitrary"),
        name="rglru_prompt",
    )(proj, proj, cw, cb, wr, br, wi, bi, lam)


def _rglru_step_kernel(xa_ref, ga_ref, b0_ref, b1_ref, b2_ref, h0_ref, cw_ref, cb_ref,
                       wr_ref, br_ref, wi_ref, bi_ref, lam_ref, ya_ref, h_ref):
    xa = xa_ref[...]
    xc = (cb_ref[...] + cw_ref[0:1, :] * b0_ref[...] + cw_ref[1:2, :] * b1_ref[...]
          + cw_ref[2:3, :] * b2_ref[...] + cw_ref[3:4, :] * xa)
    a, u = _rglru_gates(xc, wr_ref, br_ref[...], wi_ref, bi_ref[...], lam_ref[...])
    h = a * h0_ref[...] + u
    h_ref[...] = h
    ya_ref[...] = h * _gelu(ga_ref[...])


def rglru_step(proj_s, conv_state, h0, cw, cb, wr, br, wi, bi, lam):
    n = proj_s.shape[0]
    full = lambda shape: pl.BlockSpec(shape, lambda i: tuple(0 for _ in shape))
    return pl.pallas_call(
        _rglru_step_kernel,
        grid=(1,),
        in_specs=[
            pl.BlockSpec((n, RG_WIDTH), lambda i: (0, 0)),
            pl.BlockSpec((n, RG_WIDTH), lambda i: (0, 1)),
            full((n, RG_WIDTH)), full((n, RG_WIDTH)), full((n, RG_WIDTH)), full((n, RG_WIDTH)),
            full((CONV_W, RG_WIDTH)), full((1, RG_WIDTH)),
            full((RG_BLOCKS, LANES, LANES)), full((1, RG_WIDTH)),
            full((RG_BLOCKS, LANES, LANES)), full((1, RG_WIDTH)), full((1, RG_WIDTH)),
        ],
        out_specs=[full((n, RG_WIDTH)), full((n, RG_WIDTH))],
        out_shape=[jax.ShapeDtypeStruct((n, RG_WIDTH), f32)] * 2,
        compiler_params=_cparams("arbitrary"),
        name="rglru_step",
    )(proj_s, proj_s, conv_state[:, 0], conv_state[:, 1], conv_state[:, 2], h0,
      cw, cb, wr, br, wi, bi, lam)


def _rope_tables(pos):
    half = ROT_DIMS // 2
    inv = ROPE_THETA ** (-jnp.arange(half, dtype=f32) / half)
    ang = pos.astype(f32)[:, None] * inv[None, :]
    cos, sin = jnp.cos(ang), jnp.sin(ang)
    n = pos.shape[0]
    ctab = jnp.concatenate([cos, cos, jnp.ones((n, DH - ROT_DIMS), f32)], axis=1)
    stab = jnp.concatenate([-sin, sin, jnp.zeros((n, DH - ROT_DIMS), f32)], axis=1)
    return ctab, stab


def _rope_head(xh, ctab, stab, lane):
    half = ROT_DIMS // 2
    partner = jnp.where(lane < half, pltpu.roll(xh, DH - half, 1), pltpu.roll(xh, half, 1))
    return xh * ctab + partner * stab


def _rope_kernel(q_ref, k_ref, v_ref, c_ref, s_ref, qo_ref, ko_ref, vo_ref):
    ctab, stab = c_ref[...], s_ref[...]
    lane = _iota((q_ref.shape[0], DH), 1)
    for hb in range(HEADS):
        sl = slice(hb * DH, (hb + 1) * DH)
        qo_ref[:, hb, :] = _rope_head(q_ref[:, sl], ctab, stab, lane) * (DH ** -0.5)
        ko_ref[:, hb, :] = _rope_head(k_ref[:, sl], ctab, stab, lane)
        vo_ref[:, hb, :] = v_ref[:, sl]


def rope_qkv(proj, ctab, stab, rows_per_seq, tt):
    m = proj.shape[0]
    nt = max(rows_per_seq // tt, 1)
    if ctab.shape[0] == 1:
        tab_spec = pl.BlockSpec((1, DH), lambda i: (0, 0))
    else:
        tab_spec = pl.BlockSpec((tt, DH), lambda i: (i % nt, 0))
    return pl.pallas_call(
        _rope_kernel,
        grid=(m // tt,),
        in_specs=[
            pl.BlockSpec((tt, ATT_W), lambda i: (i, 2)),
            pl.BlockSpec((tt, ATT_W), lambda i: (i, 3)),
            pl.BlockSpec((tt, ATT_W), lambda i: (i, 4)),
            tab_spec, tab_spec,
        ],
        out_specs=[pl.BlockSpec((tt, HEADS, DH), lambda i: (i, 0, 0))] * 3,
        out_shape=[jax.ShapeDtypeStruct((m, HEADS, DH), f32)] * 3,
        compiler_params=_cparams("arbitrary"),
        name="rope",
    )(proj, proj, proj, ctab, stab)


def _rope_split_kernel(q_ref, k_ref, v_ref, c_ref, s_ref, *refs):
    n_pat = len(DIL_PATTERNS)
    outs, (kf_ref, qs_ref, ks_ref) = refs[:3 * n_pat], refs[3 * n_pat:]
    tt = q_ref.shape[0]
    ctab, stab = c_ref[...], s_ref[...]
    lane = _iota((tt, DH), 1)
    qs_ref[...] = _rope_head(q_ref[...], ctab, stab, lane) * (DH ** -0.5)
    k_rot = _rope_head(k_ref[...], ctab, stab, lane)
    ks_ref[...] = k_rot
    kf_ref[...] = k_rot
    for gi, (_, d) in enumerate(DIL_PATTERNS):
        rows = tt // d
        for src, dst in ((qs_ref, outs[3 * gi]), (ks_ref, outs[3 * gi + 1]), (v_ref, outs[3 * gi + 2])):
            for r in range(d):
                dst[0, 0, r] = src[pl.ds(r, rows, stride=d), :].astype(bf16)


def rope_split(proj, ctab, stab, bsz, seq, tt=1024):
    m = proj.shape[0]
    nt = seq // tt
    col = lambda c: pl.BlockSpec((tt, DH), lambda i, h, c=c: (i, c * HEADS + h))
    tab = pl.BlockSpec((tt, DH), lambda i, h: (i % nt, 0))
    out_specs, out_shape = [], []
    for (_, d) in DIL_PATTERNS:
        for _ in range(3):
            out_specs.append(pl.BlockSpec((1, 1, d, tt // d, DH), lambda i, h: (i // nt, h, 0, i % nt, 0)))
            out_shape.append(jax.ShapeDtypeStruct((bsz, HEADS, d, seq // d, DH), bf16))
    out_specs.append(pl.BlockSpec((tt, DH), lambda i, h: (i, h)))
    out_shape.append(jax.ShapeDtypeStruct((m, ATT_W), f32))
    res = pl.pallas_call(
        _rope_split_kernel,
        grid=(m // tt, HEADS),
        in_specs=[col(2), col(3), col(4), tab, tab],
        out_specs=out_specs,
        out_shape=out_shape,
        scratch_shapes=[pltpu.VMEM((tt, DH), f32), pltpu.VMEM((tt, DH), f32)],
        compiler_params=_cparams("arbitrary", "arbitrary"),
        name="rope_split",
    )(proj, proj, proj, ctab, stab)
    return [res[3 * gi:3 * gi + 3] for gi in range(len(DIL_PATTERNS))], res[-1]


def _dil_kernel(q_ref, kc_ref, kp_ref, vc_ref, vp_ref, o_ref, l_ref, s_ref, p_ref, *, span):
    tq = q_ref.shape[3]
    blk = pl.program_id(2)
    qi = _iota((tq, 2 * tq), 0)
    col = _iota((tq, 2 * tq), 1)
    rel = jnp.where(col < tq, qi - col, qi - col + 2 * tq)
    ok = (rel >= 0) & (rel <= span) & ((col < tq) | (blk > 0))
    for hb in range(HEADS):
        qh = q_ref[0, hb, 0]
        s_ref[hb, :, :tq] = _dot_nt(qh, kc_ref[0, hb, 0])
        s_ref[hb, :, tq:] = _dot_nt(qh, kp_ref[0, hb, 0])
    for hb in range(HEADS):
        s = jnp.where(ok, s_ref[hb], NEG_INF)
        mx = jnp.max(s, axis=1, keepdims=True)
        p = jnp.exp(s - mx)
        den = jnp.sum(p, axis=1, keepdims=True)
        p_ref[hb] = (p * (1.0 / den)).astype(bf16)
        l_ref[0, hb, 0] = jnp.broadcast_to(mx + jnp.log(den), (tq, DH))
    for hb in range(HEADS):
        o_ref[0, hb, 0] = (_dot(p_ref[hb, :, :tq], vc_ref[0, hb, 0])
                           + _dot(p_ref[hb, :, tq:], vp_ref[0, hb, 0]))


def _dil_merge_kernel(*refs):
    n_pat = len(DIL_PATTERNS)
    ins, o_ref, scr = refs[:2 * n_pat], refs[2 * n_pat], refs[2 * n_pat + 1:]
    tt = o_ref.shape[0]
    for gi, (_, d) in enumerate(DIL_PATTERNS):
        rows = tt // d
        for src, dst in ((ins[2 * gi], scr[2 * gi]), (ins[2 * gi + 1], scr[2 * gi + 1])):
            for r in range(d):
                dst[pl.ds(r, rows, stride=d), :] = src[0, 0, r]
    lses = [scr[2 * gi + 1][...] for gi in range(n_pat)]
    top = functools.reduce(jnp.maximum, lses)
    es = [jnp.exp(l - top) for l in lses]
    num = sum(scr[2 * gi][...] * es[gi] for gi in range(n_pat))
    o_ref[...] = num / sum(es)


def dilated_prompt(qkv_by_pattern, bsz, seq, tq=256, tt=1024):
    partial = []
    for (w, d), (qd, kd, vd) in zip(DIL_PATTERNS, qkv_by_pattern):
        sd = seq // d
        blk = (1, HEADS, 1, tq, DH)
        cur = pl.BlockSpec(blk, lambda b, r, i: (b, 0, r, i, 0))
        prev = pl.BlockSpec(blk, lambda b, r, i: (b, 0, r, jnp.maximum(i - 1, 0), 0))
        partial += pl.pallas_call(
            functools.partial(_dil_kernel, span=w // d),
            grid=(bsz, d, sd // tq),
            in_specs=[cur, cur, prev, cur, prev],
            out_specs=[cur, cur],
            out_shape=[jax.ShapeDtypeStruct((bsz, HEADS, d, sd, DH), f32)] * 2,
            scratch_shapes=[pltpu.VMEM((HEADS, tq, 2 * tq), f32), pltpu.VMEM((HEADS, tq, 2 * tq), bf16)],
            compiler_params=_cparams("arbitrary", "arbitrary", "arbitrary"),
            name=f"dilattn_d{d}",
        )(qd, kd, kd, vd, vd)
    nt = seq // tt
    in_specs = []
    for (_, d) in DIL_PATTERNS:
        in_specs += [pl.BlockSpec((1, 1, d, tt // d, DH), lambda i, h: (i // nt, h, 0, i % nt, 0))] * 2
    return pl.pallas_call(
        _dil_merge_kernel,
        grid=(bsz * nt, HEADS),
        in_specs=in_specs,
        out_specs=pl.BlockSpec((tt, DH), lambda i, h: (i, h)),
        out_shape=jax.ShapeDtypeStruct((bsz * seq, ATT_W), f32),
        scratch_shapes=[pltpu.VMEM((tt, DH), f32)] * (2 * len(DIL_PATTERNS)),
        compiler_params=_cparams("arbitrary", "arbitrary"),
        name="dilattn_merge",
    )(*partial)


def _dil_step_kernel(q_ref, kn_ref, vn_ref, k1_ref, k4_ref, k16_ref, v1_ref, v4_ref, v16_ref, o_ref):
    q = q_ref[0]
    kn, vn = kn_ref[0], vn_ref[0]
    s_self = _rowsum_bcast(q * kn)
    o_gs, lse_gs = [], []
    for k_ref, v_ref in ((k1_ref, v1_ref), (k4_ref, v4_ref), (k16_ref, v16_ref)):
        k3 = k_ref[0, :, 0]
        nk = k3.shape[0]
        s = _rowsum_bcast((k3 * q[None]).reshape(nk * HEADS, DH)).reshape(nk, HEADS, DH)
        mx = jnp.maximum(jnp.max(s, axis=0), s_self)
        p = jnp.exp(s - mx[None])
        p_self = jnp.exp(s_self - mx)
        den = jnp.sum(p, axis=0) + p_self
        num = jnp.sum(p * v_ref[0, :, 0], axis=0) + p_self * vn
        o_gs.append(num / den)
        lse_gs.append(mx + jnp.log(den))
    top = jnp.maximum(jnp.maximum(lse_gs[0], lse_gs[1]), lse_gs[2])
    es = [jnp.exp(l - top) for l in lse_gs]
    tot = es[0] + es[1] + es[2]
    o_ref[0] = (o_gs[0] * es[0] + o_gs[1] * es[1] + o_gs[2] * es[2]) / tot


def dilated_step(q4, k4, v4, cache_k, cache_v):
    n, win = cache_k.shape[0], cache_k.shape[1]
    one = pl.BlockSpec((1, HEADS, DH), lambda b: (b, 0, 0))
    args = [q4, k4, v4]
    in_specs = [one, one, one]
    for cache in (cache_k, cache_v):
        for (w, d) in DIL_PATTERNS:
            nkeys = w // d
            assert win % d == 0 and (win // d) % nkeys == 0 and (win - w) % (d * nkeys) == 0
            args.append(cache.reshape(n, win // d, d, HEADS, DH))
            in_specs.append(pl.BlockSpec((1, nkeys, 1, HEADS, DH),
                                         lambda b, blk=(win - w) // d // nkeys: (b, blk, 0, 0, 0)))
    return pl.pallas_call(
        _dil_step_kernel,
        grid=(n,),
        in_specs=in_specs,
        out_specs=one,
        out_shape=jax.ShapeDtypeStruct((n, HEADS, DH), f32),
        compiler_params=_cparams("arbitrary"),
        name="dilattn_step",
    )(*args).reshape(n, ATT_W)


def _sb_kernel(bias_ref, q_ref, k_ref, v_ref, o_ref, kb_ref, vb_ref, qs_ref, t_ref, spb_ref, wb_ref,
               acc_ref, run_ref):
    tq = q_ref.shape[1]
    nh = q_ref.shape[2] // DH
    kt_w = t_ref.shape[2]
    sub = PAGE
    hg = pl.program_id(1)
    qb = pl.program_id(2)

    @pl.when(qb == 0)
    def _():
        kb_ref[...] = k_ref[0].astype(bf16)
        vb_ref[...] = v_ref[0].astype(bf16)

    rr = _iota((sub, 2 * sub), 0)
    cc = _iota((sub, 2 * sub), 1)
    tri = jnp.where((rr > cc) | (cc >= sub), 1.0, 0.0).astype(bf16)
    for h in range(nh):
        qs_ref[h] = (q_ref[0, :, h * DH:(h + 1) * DH] * (DH ** -0.5)).astype(bf16)
    acc_ref[...] = jnp.zeros_like(acc_ref)
    run_ref[...] = jnp.zeros_like(run_ref)

    def macro(start, width, masked):
        start = pl.multiple_of(start, width)
        for h in range(nh):
            hs = slice(h * DH, (h + 1) * DH)
            z = _dot_nt(qs_ref[h], kb_ref[pl.ds(start, width), hs]) + bias_ref[hg * nh + h]
            sp = _softplus(z)
            t = z - sp
            if masked:
                ok = start + _iota((tq, width), 1) < qb * tq + _iota((tq, width), 0)
                sp = jnp.where(ok, sp, 0.0)
                t = jnp.where(ok, t, NEG_INF)
            t_ref[h, :, :width] = t
            spb_ref[h, :, :width] = sp.astype(bf16)
        for h in range(nh):
            run = run_ref[h]
            for kt in range(width // sub - 1, -1, -1):
                ks = slice(kt * sub, (kt + 1) * sub)
                cs = _dot(spb_ref[h, :, ks], tri)
                wb_ref[h, :, ks] = jnp.exp(t_ref[h, :, ks] - cs[:, :sub] - run).astype(bf16)
                run = run + cs[:, sub:]
            run_ref[h] = run
        for h in range(nh):
            hs = slice(h * DH, (h + 1) * DH)
            acc_ref[h] += _dot(wb_ref[h, :, :width], vb_ref[pl.ds(start, width), hs])

    macro(qb * tq, tq, True)
    n_part = (qb * tq % kt_w) // tq
    for j in range(1, kt_w // tq):
        @pl.when(n_part >= j)
        def _():
            macro((qb - j) * tq, tq, False)

    top = (qb * tq) // kt_w

    def body(it, _):
        macro((top - 1 - it) * kt_w, kt_w, False)
        return 0

    lax.fori_loop(0, top, body, 0)
    for h in range(nh):
        o_ref[0, :, h * DH:(h + 1) * DH] = acc_ref[h]


def sb_prompt(proj2, bias, bsz, seq, tq=256, nh=4, kt_w=512):
    assert kt_w % tq == 0 and seq % kt_w == 0
    p3 = proj2.reshape(bsz, seq, proj2.shape[1])
    nq = seq // tq
    ng = HEADS // nh
    wd = nh * DH
    return pl.pallas_call(
        _sb_kernel,
        grid=(bsz, ng, nq),
        in_specs=[
            pl.BlockSpec(memory_space=pltpu.SMEM),
            pl.BlockSpec((1, tq, wd), lambda b, g, i: (b, i, g)),
            pl.BlockSpec((1, seq, wd), lambda b, g, i: (b, 0, ng + g)),
            pl.BlockSpec((1, seq, wd), lambda b, g, i: (b, 0, 2 * ng + g)),
        ],
        out_specs=pl.BlockSpec((1, tq, wd), lambda b, g, i: (b, i, g)),
        out_shape=jax.ShapeDtypeStruct((bsz, seq, ATT_W), f32),
        scratch_shapes=[
            pltpu.VMEM((seq, wd), bf16), pltpu.VMEM((seq, wd), bf16),
            pltpu.VMEM((nh, tq, DH), bf16),
            pltpu.VMEM((nh, tq, kt_w), f32), pltpu.VMEM((nh, tq, kt_w), bf16),
            pltpu.VMEM((nh, tq, kt_w), bf16),
            pltpu.VMEM((nh, tq, DH), f32), pltpu.VMEM((nh, tq, PAGE), f32),
        ],
        compiler_params=_cparams("arbitrary", "arbitrary", "arbitrary"),
        name="sb_prompt",
    )(bias, p3, p3, p3).reshape(bsz * seq, ATT_W)


def _sb_step_kernel(pt_ref, q_ref, bias_ref, eye_ref, *refs, npg):
    k_refs, v_refs = refs[:npg], refs[npg:2 * npg]
    o_ref, acc_ref, carry_ref = refs[2 * npg:]
    j = pl.program_id(1)

    @pl.when(j == 0)
    def _():
        acc_ref[...] = jnp.zeros_like(acc_ref)
        carry_ref[...] = jnp.zeros_like(carry_ref)

    q = q_ref[0]
    bias = bias_ref[...]
    ones = jnp.ones((DH, DH), bf16)
    rr = _iota((PAGE, 2 * PAGE), 0)
    cc = _iota((PAGE, 2 * PAGE), 1)
    tri = jnp.where((rr > cc) | (cc >= PAGE), 1.0, 0.0).astype(bf16)
    acc = acc_ref[...]
    run = carry_ref[...]
    pages = range(npg)
    zbs = [_dot((k_refs[p][0] * q[None]).reshape(PAGE * HEADS, DH).astype(bf16), ones)
           .reshape(PAGE, HEADS, DH) for p in pages]
    zs = [jnp.sum(zb * eye_ref[...], axis=0) + bias for zb in zbs]
    sps = [_softplus(z) for z in zs]
    css = []
    for sp in sps:
        hi, mid, lo = _split3(sp)
        css.append(_dot(hi, tri) + _dot(mid, tri) + _dot(lo, tri))
    ws = []
    for z, sp, cs in zip(zs, sps, css):
        ws.append(jnp.exp(z - sp - cs[:, :PAGE] - run))
        run = run + cs[:, PAGE:]
    wbs = [_dot((eye_ref[...] * w[None]).reshape(PAGE * HEADS, DH).astype(bf16), ones)
           .reshape(PAGE, HEADS, DH) for w in ws]
    for p, wb in zip(pages, wbs):
        acc = acc + jnp.sum(wb * v_refs[p][0], axis=0)
    acc_ref[...] = acc
    carry_ref[...] = run

    @pl.when(j == pl.num_programs(1) - 1)
    def _():
        o_ref[0] = acc


def sb_step(q_s, bias, cache_k, cache_v, page_table, npg=8):
    n, n_pages = page_table.shape
    q4 = (q_s * (DH ** -0.5)).reshape(n, HEADS, DH)
    bias4 = jnp.broadcast_to(bias[:, None], (HEADS, DH))
    eye3 = jnp.broadcast_to(jnp.eye(PAGE, DH, dtype=f32)[:, None, :], (PAGE, HEADS, DH))

    def page(p):
        return lambda b, j, pt: (pt[b, n_pages - 1 - (j * npg + p)], 0, 0, 0)

    kv_specs = [pl.BlockSpec((1, PAGE, HEADS, DH), page(p)) for p in range(npg)]
    grid_spec = pltpu.PrefetchScalarGridSpec(
        num_scalar_prefetch=1,
        grid=(n, n_pages // npg),
        in_specs=[
            pl.BlockSpec((1, HEADS, DH), lambda b, j, pt: (b, 0, 0)),
            pl.BlockSpec((HEADS, DH), lambda b, j, pt: (0, 0)),
            pl.BlockSpec((PAGE, HEADS, DH), lambda b, j, pt: (0, 0, 0)),
        ] + kv_specs + kv_specs,
        out_specs=pl.BlockSpec((1, HEADS, DH), lambda b, j, pt: (b, 0, 0)),
        scratch_shapes=[pltpu.VMEM((HEADS, DH), f32), pltpu.VMEM((HEADS, DH), f32)],
    )
    return pl.pallas_call(
        functools.partial(_sb_step_kernel, npg=npg),
        grid_spec=grid_spec,
        out_shape=jax.ShapeDtypeStruct((n, HEADS, DH), f32),
        compiler_params=_cparams("arbitrary", "arbitrary"),
        name="sb_step",
    )(page_table, q4, bias4, eye3, *([cache_k] * npg), *([cache_v] * npg)).reshape(n, ATT_W)


def _mlstm_kernel(q_ref, k_ref, v_ref, og_ref, gate_ref, gb_ref, gn_ref,
                  h_ref, c_out, n_out, m_out, c_s, n_s, m_s):
    ci = pl.program_id(1)
    L = CHUNK

    @pl.when(ci == 0)
    def _():
        c_s[...] = jnp.zeros_like(c_s)
        n_s[...] = jnp.zeros_like(n_s)
        m_s[...] = jnp.zeros_like(m_s)

    gt = gate_ref[...] + gb_ref[...]
    gt_t = gt.T
    ri = _iota((L, L), 0)
    li = _iota((L, L), 1)
    causal = li <= ri
    tri_incl = jnp.where(causal, 1.0, 0.0).astype(bf16)
    tri_incl_t = jnp.where(ri <= li, 1.0, 0.0).astype(bf16)
    for h in range(M_HEADS):
        sl = slice(h * M_DH, (h + 1) * M_DH)
        ig_col = gt[:, h:h + 1]
        ig_row = gt_t[h:h + 1, :]
        lf_col = _log_sigmoid_pair(gt[:, M_HEADS + h:M_HEADS + h + 1])[0]
        lf_row = _log_sigmoid_pair(gt_t[M_HEADS + h:M_HEADS + h + 1, :])[0]
        c_hi, c_mid, c_lo = _split3(jnp.broadcast_to(lf_col, (L, L)))
        bcum_col = _dot(tri_incl, c_hi) + _dot(tri_incl, c_mid) + _dot(tri_incl, c_lo)
        r_hi, r_mid, r_lo = _split3(jnp.broadcast_to(lf_row, (L, L)))
        bcum_row = _dot(r_hi, tri_incl_t) + _dot(r_mid, tri_incl_t) + _dot(r_lo, tri_incl_t)
        m_prev = m_s[h:h + 1, :]
        dlog = jnp.where(causal, bcum_col - bcum_row + ig_row, NEG_INF)
        inter = bcum_col + m_prev
        m_t = jnp.maximum(inter, jnp.max(dlog, axis=1, keepdims=True))
        dw = jnp.exp(dlog - m_t)
        iw = jnp.exp(inter - m_t)
        qh = q_ref[:, sl]
        kh = k_ref[:, sl] * (M_DH ** -0.5)
        vh = v_ref[:, sl]
        qb, kb, vb = qh.astype(bf16), kh.astype(bf16), vh.astype(bf16)
        sw = dw * _dot_nt(qb, kb)
        c_prev = c_s[h]
        n_prev = n_s[h:h + 1, :]
        iw_col = iw[:, 0:1]
        num = _dot(sw.astype(bf16), vb) + iw_col * _dot_nt(qb, c_prev.astype(bf16))
        qn = jnp.sum(qb.astype(f32) * n_prev.astype(bf16).astype(f32), axis=1, keepdims=True)
        den = jnp.sum(sw, axis=1, keepdims=True) + iw_col * qn
        m_col = m_t[:, 0:1]
        hout = num / jnp.maximum(jnp.abs(den), jnp.exp(-m_col))
        m_last = m_t[L - 1:L, :]
        b_last = bcum_col[L - 1:L, :]
        wl_col = jnp.exp(b_last[:, 0:1] - bcum_col[:, 0:1] + ig_col - m_last[:, 0:1])
        wl_row = jnp.exp(b_last - bcum_row[0:1, :] + ig_row - m_last)
        decay = jnp.exp(b_last + m_prev - m_last)
        dsc = decay[:, 0:1]
        c_s[h] = dsc * c_prev + _dot((vh * wl_col).T.astype(bf16), kb)
        wl8 = jnp.broadcast_to(wl_row, (8, L)).astype(bf16)
        n_s[h:h + 1, :] = dsc * n_prev + _dot(wl8, kb)[0:1, :]
        m_s[h:h + 1, :] = m_last
        hn = hout * lax.rsqrt(jnp.mean(hout * hout, axis=1, keepdims=True) + EPS)
        h_ref[:, sl] = hn * gn_ref[:, sl] * _sigmoid(og_ref[:, sl])

    @pl.when(ci == pl.num_programs(1) - 1)
    def _():
        c_out[0] = c_s[...]
        n_out[0] = n_s[...]
        m_out[0] = m_s[...]


def mlstm_prompt(proj2, gates, gate_bias, g_mnorm, bsz, seq):
    nc = seq // CHUNK
    col = lambda c: pl.BlockSpec((CHUNK, M_WIDTH), lambda b, i, c=c: (b * nc + i, c))
    hm, c1, n1, m1 = pl.pallas_call(
        _mlstm_kernel,
        grid=(bsz, nc),
        in_specs=[
            col(3), col(4), col(5), col(6),
            pl.BlockSpec((CHUNK, LANES), lambda b, i: (b * nc + i, 0)),
            pl.BlockSpec((1, LANES), lambda b, i: (0, 0)),
            pl.BlockSpec((1, M_WIDTH), lambda b, i: (0, 0)),
        ],
        out_specs=[
            pl.BlockSpec((CHUNK, M_WIDTH), lambda b, i: (b * nc + i, 0)),
            pl.BlockSpec((1, M_HEADS, M_DH, M_DH), lambda b, i: (b, 0, 0, 0)),
            pl.BlockSpec((1, M_HEADS, M_DH), lambda b, i: (b, 0, 0)),
            pl.BlockSpec((1, M_HEADS, LANES), lambda b, i: (b, 0, 0)),
        ],
        out_shape=[
            jax.ShapeDtypeStruct((bsz * seq, M_WIDTH), f32),
            jax.ShapeDtypeStruct((bsz, M_HEADS, M_DH, M_DH), f32),
            jax.ShapeDtypeStruct((bsz, M_HEADS, M_DH), f32),
            jax.ShapeDtypeStruct((bsz, M_HEADS, LANES), f32),
        ],
        scratch_shapes=[
            pltpu.VMEM((M_HEADS, M_DH, M_DH), f32),
            pltpu.VMEM((M_HEADS, M_DH), f32),
            pltpu.VMEM((M_HEADS, LANES), f32),
        ],
        compiler_params=_cparams("arbitrary", "arbitrary"),
        name="mlstm_prompt",
    )(proj2, proj2, proj2, proj2, gates, gate_bias, g_mnorm)
    return hm, c1, n1, m1[:, :, 0]


def _mlstm_step_kernel(q_ref, k_ref, v_ref, og_ref, gn_ref, ig_ref, fg_ref, c_ref, n_ref, m_ref,
                       h_ref, c_out, n_out, m_out):
    q = q_ref[0, 0]
    k = k_ref[0, 0] * (M_DH ** -0.5)
    v = v_ref[0, 0]
    ig = ig_ref[0, 0]
    lf = _log_sigmoid_pair(fg_ref[0, 0])[0]
    m0 = m_ref[0, 0]
    c0 = c_ref[0, 0]
    n0 = n_ref[0, 0]
    inter = lf + m0
    m_t = jnp.maximum(inter, ig)
    dw = jnp.exp(ig - m_t)
    iw = jnp.exp(inter - m_t)
    rnd = lambda a: a.astype(bf16).astype(f32)
    qr, kr, vr = rnd(q), rnd(k), rnd(v)
    qk = jnp.sum(qr * kr, axis=1, keepdims=True)
    sw = dw * qk
    cq = jnp.sum(rnd(c0) * qr, axis=1, keepdims=True)
    num = rnd(sw) * vr + iw * cq
    den = sw + iw * jnp.sum(rnd(n0) * qr, axis=1, keepdims=True)
    hout = num / jnp.maximum(jnp.abs(den), jnp.exp(-m_t))
    c_out[0, 0] = iw * c0 + rnd(dw * v) * kr
    n_out[0, 0] = iw * n0 + rnd(dw) * kr
    m_out[0, 0] = m_t
    hn = hout * lax.rsqrt(jnp.mean(hout * hout, axis=0, keepdims=True) + EPS)
    h_ref[0, 0] = hn * gn_ref[0] * _sigmoid(og_ref[0, 0])


def mlstm_step(proj2_s, gates_s, gate_bias, g_mnorm, c0, n0, m0):
    n = proj2_s.shape[0]
    base = 3 * ATT_W
    seg = lambda i: proj2_s[:, base + i * M_WIDTH: base + (i + 1) * M_WIDTH].reshape(n, M_HEADS, M_DH)
    g = gates_s + gate_bias
    rowb = pl.BlockSpec((1, 1, 1, M_DH), lambda b, h: (b, h, 0, 0))
    colb = pl.BlockSpec((1, 1, M_DH, 1), lambda b, h: (b, h, 0, 0))
    scal = pl.BlockSpec((1, 1, 1, 1), lambda b, h: (b, h, 0, 0))
    hcol, c1, n1, m1 = pl.pallas_call(
        _mlstm_step_kernel,
        grid=(n, M_HEADS),
        in_specs=[
            rowb, rowb, colb, colb,
            pl.BlockSpec((1, M_DH, 1), lambda b, h: (h, 0, 0)),
            scal, scal,
            pl.BlockSpec((1, 1, M_DH, M_DH), lambda b, h: (b, h, 0, 0)),
            rowb, scal,
        ],
        out_specs=[colb, pl.BlockSpec((1, 1, M_DH, M_DH), lambda b, h: (b, h, 0, 0)), rowb, scal],
        out_shape=[
            jax.ShapeDtypeStruct((n, M_HEADS, M_DH, 1), f32),
            jax.ShapeDtypeStruct((n, M_HEADS, M_DH, M_DH), f32),
            jax.ShapeDtypeStruct((n, M_HEADS, 1, M_DH), f32),
            jax.ShapeDtypeStruct((n, M_HEADS, 1, 1), f32),
        ],
        compiler_params=_cparams("arbitrary", "arbitrary"),
        name="mlstm_step",
    )(seg(0)[:, :, None, :], seg(1)[:, :, None, :], seg(2)[..., None], seg(3)[..., None],
      g_mnorm.reshape(M_HEADS, M_DH, 1),
      g[:, 0:M_HEADS].reshape(n, M_HEADS, 1, 1), g[:, M_HEADS:2 * M_HEADS].reshape(n, M_HEADS, 1, 1),
      c0, n0[:, :, None, :], m0.reshape(n, M_HEADS, 1, 1))
    return hcol.reshape(n, M_WIDTH), c1, n1.reshape(n, M_HEADS, M_DH), m1.reshape(n, M_HEADS)


def _peerq_kernel(x_ref, g_ref, sh_ref, sc_ref, wt_ref, qt_ref, xmt_ref, xs_ref):
    @pl.when(pl.program_id(1) == 0)
    def _():
        xm = _modulated(x_ref[...], g_ref[...], sh_ref[0], sc_ref[0])
        xt = xm.T.astype(bf16)
        xs_ref[...] = xt
        xmt_ref[...] = xt

    qt_ref[...] = _dot(wt_ref[...], xs_ref[...])


def peer_query(x, g, shift, scale, wq_t, rows_per_batch, tm, tn=512):
    m, k = x.shape
    n = wq_t.shape[0]
    sh, sh_spec = _mod_specs(shift, m, tm, k, rows_per_batch)
    sc, sc_spec = _mod_specs(scale, m, tm, k, rows_per_batch)
    return pl.pallas_call(
        _peerq_kernel,
        grid=(m // tm, n // tn),
        in_specs=[
            pl.BlockSpec((tm, k), lambda i, j: (i, 0)),
            pl.BlockSpec((1, k), lambda i, j: (0, 0)),
            sh_spec, sc_spec,
            pl.BlockSpec((tn, k), lambda i, j: (j, 0)),
        ],
        out_specs=[pl.BlockSpec((tn, tm), lambda i, j: (j, i)),
                   pl.BlockSpec((k, tm), lambda i, j: (0, i))],
        out_shape=[jax.ShapeDtypeStruct((n, m), f32), jax.ShapeDtypeStruct((k, m), bf16)],
        scratch_shapes=[pltpu.VMEM((k, tm), bf16)],
        compiler_params=_cparams("arbitrary", "arbitrary"),
        name="peer_query",
    )(x, g.reshape(1, k), sh, sc, wq_t)


def _cast_t_kernel(w_ref, o_ref):
    o_ref[...] = w_ref[0].T.astype(bf16)


def cast_layer_bf16_t(w, layer, tr=512):
    _, rows, cols = w.shape
    return pl.pallas_call(
        _cast_t_kernel,
        grid=(rows // tr,),
        in_specs=[pl.BlockSpec((1, tr, cols), lambda i: (layer, i, 0))],
        out_specs=pl.BlockSpec((cols, tr), lambda i: (0, i)),
        out_shape=jax.ShapeDtypeStruct((cols, rows), bf16),
        compiler_params=_cparams("arbitrary"),
        name="cast_bf16_t",
    )(w)


def _heads_layout_kernel(*refs):
    n = len(refs) // 2
    for x_ref, o_ref in zip(refs[:n], refs[n:]):
        for hb in range(HEADS):
            o_ref[:, hb, :] = x_ref[:, hb * DH:(hb + 1) * DH]


def heads_layout(srcs, bsz, seq, tail, tt=512):
    nt, first = tail // tt, (seq - tail) // tt
    rows = lambda b, i: b * (seq // tt) + first + i
    return pl.pallas_call(
        _heads_layout_kernel,
        grid=(bsz, nt),
        in_specs=[pl.BlockSpec((tt, ATT_W), lambda b, i, c=c: (rows(b, i), c)) for _, c in srcs],
        out_specs=[pl.BlockSpec((tt, HEADS, DH), lambda b, i: (b * nt + i, 0, 0))] * len(srcs),
        out_shape=[jax.ShapeDtypeStruct((bsz * tail, HEADS, DH), f32)] * len(srcs),
        compiler_params=_cparams("arbitrary", "arbitrary"),
        name="heads_layout",
    )(*[x for x, _ in srcs])


def _cast_kernel(w_ref, o_ref):
    o_ref[...] = w_ref[0].astype(bf16)


def cast_layer_bf16(w, layer, tr=1024):
    _, rows, cols = w.shape
    return pl.pallas_call(
        _cast_kernel,
        grid=(rows // tr,),
        in_specs=[pl.BlockSpec((1, tr, cols), lambda i: (layer, i, 0))],
        out_specs=pl.BlockSpec((tr, cols), lambda i: (i, 0)),
        out_shape=jax.ShapeDtypeStruct((rows, cols), bf16),
        compiler_params=_cparams("arbitrary"),
        name="cast_bf16",
    )(w)


_N_TOP = PEER_TOPK + 1
_CAND_PAIRS = [(a, b) for a in range(_N_TOP) for b in range(_N_TOP) if (a + 1) * (b + 1) <= _N_TOP]
_N_CAND = -(-len(_CAND_PAIRS) // 8) * 8


def _extract_top(cur, ridx, n):
    vals = []
    big = float(cur.shape[0])
    for _ in range(n):
        mx = jnp.max(cur, axis=0, keepdims=True)
        first = jnp.min(jnp.where(cur == mx, ridx, big), axis=0, keepdims=True)
        cur = jnp.where(ridx == first, NEG_INF, cur)
        vals.append(mx)
    return vals


def _oddeven_mergesort_pairs(n):
    pairs = []

    def merge(lo, cnt, r):
        step = 2 * r
        if step < cnt:
            merge(lo, cnt, step)
            merge(lo + r, cnt, step)
            pairs.extend((i, i + r) for i in range(lo + r, lo + cnt - r, step))
        else:
            pairs.append((lo, lo + r))

    def sort(lo, cnt):
        if cnt > 1:
            sort(lo, cnt // 2)
            sort(lo + cnt // 2, cnt // 2)
            merge(lo, cnt, 1)

    sort(0, n)
    return pairs


_SORT16 = _oddeven_mergesort_pairs(PEER_TOPK)


def _exchange(y, i, j):
    y[i], y[j] = jnp.maximum(y[i], y[j]), jnp.minimum(y[i], y[j])


def _top17_network(s):
    n = PEER_TOPK
    y = [s[v * 8:(v + 1) * 8, :] for v in range(n)]
    for i, j in _SORT16:
        _exchange(y, i, j)
    dropped = None
    for shift in (4, 2, 1):
        other = [pltpu.roll(y[n - 1 - i], shift, 0) for i in range(n)]
        low = functools.reduce(jnp.maximum, [jnp.minimum(y[i], other[i]) for i in range(n)])
        dropped = low if dropped is None else jnp.maximum(dropped, low)
        y = [jnp.maximum(y[i], other[i]) for i in range(n)]
        dist = n // 2
        while dist:
            for i in range(n):
                if not i & dist:
                    _exchange(y, i, i + dist)
            dist //= 2
    return [v[0:1, :] for v in y] + [jnp.max(dropped, axis=0, keepdims=True)]


_ROUTER_UNROLL = 8


def _router_kernel(qt_ref, keys_ref, e1_ref, e2_ref, th_ref, cand_ref):
    cidx = _iota((_N_CAND, LANES), 0).astype(f32)
    k0 = keys_ref[0].astype(bf16)
    k1 = keys_ref[1].astype(bf16)
    cand_ref[...] = jnp.full(cand_ref.shape, NEG_INF, f32)

    def one_head(h, slot):
        r0 = pl.multiple_of(h * 2 * N_KEYS, 2 * N_KEYS)
        s1 = _dot(k0, qt_ref[pl.ds(r0, N_KEYS), :].astype(bf16))
        s2 = _dot(k1, qt_ref[pl.ds(r0 + N_KEYS, N_KEYS), :].astype(bf16))
        top_a = _top17_network(s1)
        top_b = _top17_network(s2)
        for ci, (a, b) in enumerate(_CAND_PAIRS):
            cand_ref[slot, ci:ci + 1, :] = top_a[a] + top_b[b]
        cs = _extract_top(cand_ref[slot], cidx, _N_TOP)
        z = jnp.ones_like(cs[0])
        for r in range(1, PEER_TOPK):
            z = z + jnp.exp(cs[r] - cs[0])
        inv_z = 1.0 / z
        mid = 0.5 * (cs[PEER_TOPK - 1] + cs[PEER_TOPK])
        o0 = pl.multiple_of(h * N_KEYS, N_KEYS)
        e1 = jnp.exp(s1 - top_a[0]) * inv_z
        e1_ref[:, h] = e1.reshape(N_KEYS // SUBLANES, SUBLANES, LANES)
        e2_ref[pl.ds(o0, N_KEYS), :] = jnp.exp(s2 - top_b[0])
        th_ref[pl.ds(h, 1), :] = jnp.exp(mid - cs[0]) * inv_z

    def body(hp, _):
        for slot in range(_ROUTER_UNROLL):
            one_head(hp * _ROUTER_UNROLL + slot, slot)
        return 0

    lax.fori_loop(0, PEER_HEADS // _ROUTER_UNROLL, body, 0)


def peer_router(qt, keys):
    n, m = qt.shape
    half = PEER_HEADS * N_KEYS
    return pl.pallas_call(
        _router_kernel,
        grid=(m // LANES,),
        in_specs=[pl.BlockSpec((n, LANES), lambda i: (0, i)),
                  pl.BlockSpec((2, N_KEYS, N_KEYS), lambda i: (0, 0, 0))],
        out_specs=[pl.BlockSpec((N_KEYS // SUBLANES, PEER_HEADS, SUBLANES, LANES), lambda i: (0, 0, 0, i)),
                   pl.BlockSpec((half, LANES), lambda i: (0, i)),
                   pl.BlockSpec((PEER_HEADS, LANES), lambda i: (0, i))],
        out_shape=[jax.ShapeDtypeStruct((N_KEYS // SUBLANES, PEER_HEADS, SUBLANES, m), f32),
                   jax.ShapeDtypeStruct((half, m), f32),
                   jax.ShapeDtypeStruct((PEER_HEADS, m), f32)],
        scratch_shapes=[pltpu.VMEM((_ROUTER_UNROLL, _N_CAND, LANES), f32)],
        compiler_params=_cparams("arbitrary"),
        name="peer_router",
    )(qt, keys)


_G_CHAINS = 3
_G_ROWS = SUBLANES


class _Chains:
    def __init__(self, n, zero_ref):
        self.deps, self.cnt, self.zero_ref = [None] * n, 0, zero_ref

    def take(self):
        return self.deps[self.cnt % len(self.deps)]

    def put(self, g):
        bits = lax.bitcast_convert_type(g[0:1, :], jnp.int32) & self.zero_ref[0:1, :]
        self.deps[self.cnt % len(self.deps)] = lax.bitcast_convert_type(bits, f32)
        self.cnt += 1


def _routing_block(e1_ref, e2_ref, th_ref, ii, ls, chains):
    sub = _G_ROWS
    blk, r = divmod(ii, SUBLANES)
    rows = [e1_ref[blk, h, r:r + 1, ls] for h in range(PEER_HEADS)]
    ths = [th_ref[h:h + 1, ls] for h in range(PEER_HEADS)]
    parts = []
    for gi in range(N_KEYS // sub):
        dep = chains.take()
        g = jnp.zeros((sub, LANES), f32)
        for h in range(PEER_HEADS):
            row = rows[h] if (dep is None or h) else rows[h] + dep
            pr = e2_ref[h * N_KEYS + gi * sub:h * N_KEYS + (gi + 1) * sub, ls] * row
            g = g + jnp.where(pr >= ths[h], pr, 0.0)
        chains.put(g)
        parts.append(g)
    return jnp.concatenate(parts, axis=0)


def _experts_kernel(xmt_ref, e1_ref, e2_ref, th_ref, u_ref, v_ref, x_ref, gt_ref, gf_ref, zero_ref,
                    o_ref, w_ref, *, final_norm):
    c = pl.program_id(1)
    tm, te = w_ref.shape

    @pl.when(c == 0)
    def _():
        o_ref[...] = jnp.zeros_like(o_ref)

    chains = _Chains(_G_CHAINS, zero_ref)
    act = _gelu(_dot(u_ref[...], xmt_ref[...]))
    for ii in range(te // N_KEYS):
        es = slice(ii * N_KEYS, (ii + 1) * N_KEYS)
        for lc in range(tm // LANES):
            ls = slice(lc * LANES, (lc + 1) * LANES)
            g_blk = _routing_block(e1_ref, e2_ref, th_ref, ii, ls, chains)
            w_ref[ls, es] = (g_blk * act[es, ls]).T.astype(bf16)
    o_ref[...] += _dot(w_ref[...], v_ref[...])

    @pl.when(c == pl.num_programs(1) - 1)
    def _():
        y = x_ref[...] + gt_ref[0] * o_ref[...]
        if final_norm:
            ms = jnp.mean(y * y, axis=-1, keepdims=True)
            y = y * lax.rsqrt(ms + EPS) * gf_ref[...]
        o_ref[...] = y


def peer_experts(xmt, e1t, e2t, th, u, v, x, gate, g_final, rows_per_batch, tm, te, final_norm):
    m, d = x.shape
    gt, gt_spec = _mod_specs(gate, m, tm, d, rows_per_batch)
    half = PEER_HEADS * N_KEYS
    return pl.pallas_call(
        functools.partial(_experts_kernel, final_norm=final_norm),
        grid=(m // tm, N_EXPERTS // te),
        in_specs=[
            pl.BlockSpec((d, tm), lambda i, c: (0, i)),
            pl.BlockSpec((te // N_KEYS // SUBLANES, PEER_HEADS, SUBLANES, tm), lambda i, c: (c, 0, 0, i)),
            pl.BlockSpec((half, tm), lambda i, c: (0, i)),
            pl.BlockSpec((PEER_HEADS, tm), lambda i, c: (0, i)),
            pl.BlockSpec((te, d), lambda i, c: (c, 0)),
            pl.BlockSpec((te, d), lambda i, c: (c, 0)),
            pl.BlockSpec((tm, d), lambda i, c: (i, 0)),
            gt_spec,
            pl.BlockSpec((1, d), lambda i, c: (0, 0)),
            pl.BlockSpec((SUBLANES, LANES), lambda i, c: (0, 0)),
        ],
        out_specs=pl.BlockSpec((tm, d), lambda i, c: (i, 0)),
        out_shape=jax.ShapeDtypeStruct((m, d), f32),
        scratch_shapes=[pltpu.VMEM((tm, te), bf16)],
        compiler_params=_cparams("arbitrary", "arbitrary"),
        name="peer_experts",
    )(xmt, e1t, e2t, th, u, v, x, gt, g_final.reshape(1, d), jnp.zeros((SUBLANES, LANES), jnp.int32))


def peer_block(x, g, shift, scale, gate, wq_t, keys, u, v, g_final, rows_per_batch, tm, te, final_norm):
    tm_q = 1024 if x.shape[0] % 1024 == 0 and rows_per_batch % 1024 == 0 else tm
    qt, xmt = peer_query(x, g, shift, scale, wq_t, rows_per_batch, tm_q)
    e1t, e2t, th = peer_router(qt, keys)
    return peer_experts(xmt, e1t, e2t, th, u, v, x, gate, g_final, rows_per_batch, tm, te, final_norm)


def kernel(x_prompt, x_sample, c_prompt, c_sample, state_rglru_conv, state_rglru_h, cache_swa_k, cache_swa_v, cache_sb_k, cache_sb_v, state_mlstm_C, state_mlstm_n, state_mlstm_m, page_table, w_ada, b_ada, g_norm_mix, g_norm_ffn, e_w_in, e_conv_w, e_conv_b, e_w_r, e_b_r, e_w_i, e_b_i, e_lambda, e_w_out, o_w_in, o_b_if, o_sb_bias, o_g_mnorm, o_w_out, peer_w_q, peer_keys, peer_u, peer_v, g_final):
    bp, seq, d = x_prompt.shape
    bs = x_sample.shape[0]
    mp = bp * seq
    pad_s = LANES
    xp = x_prompt.reshape(mp, d)
    xs = x_sample.reshape(bs, d)

    c_rows = 16
    c_all = jnp.concatenate([c_prompt, c_sample, jnp.zeros((c_rows - bp - bs, d), f32)], axis=0)
    mod = adaln_all(c_all, w_ada, b_ada)

    def mods(layer):
        parts = [mod[layer, :, i * d:(i + 1) * d] for i in range(6)]
        return [p[:bp] for p in parts], [p[bp:bp + bs] for p in parts]

    ctab_p, stab_p = _rope_tables(jnp.arange(seq, dtype=jnp.int32))
    ctab_s, stab_s = _rope_tables(jnp.full((1,), PAST_LEN, jnp.int32))

    TM = 512
    m_p, m_s = mods(0)
    w_in = e_w_in[0].astype(bf16)
    w_out = e_w_out[0].astype(bf16)
    cw, cb = e_conv_w[0], e_conv_b[0].reshape(1, -1)
    wr, wi = e_w_r[0].astype(bf16), e_w_i[0].astype(bf16)
    br, bi, lam = e_b_r[0].reshape(1, -1), e_b_i[0].reshape(1, -1), e_lambda[0].reshape(1, -1)

    proj_p = mod_matmul(xp, g_norm_mix[0], m_p[0], m_p[1], w_in, E_IN, seq, 1024, 1024, name="e_in_p")
    proj_s = mod_matmul(xs, g_norm_mix[0], m_s[0], m_s[1], w_in, E_IN, 1, bs, 512, name="e_in_s")

    ya_p, h_p = rglru_prompt(proj_p, bp, seq, cw, cb, wr, br, wi, bi, lam)
    ya_s, h_s = rglru_step(proj_s, state_rglru_conv[0], state_rglru_h[0], cw, cb, wr, br, wi, bi, lam)
    conv_p = proj_p.reshape(bp, seq, E_IN)[:, seq - (CONV_W - 1):, :RG_WIDTH]
    conv_s = jnp.concatenate([state_rglru_conv[0][:, 1:], proj_s[:, None, :RG_WIDTH]], axis=1)

    qkv_p, k_p = rope_split(proj_p, ctab_p, stab_p, bp, seq)
    q_s, k_s, v_s = rope_qkv(proj_s, ctab_s, stab_s, 1, bs)
    o_p = dilated_prompt(qkv_p, bp, seq)
    win = cache_swa_k.shape[2]
    o_s = dilated_step(q_s, k_s, v_s, cache_swa_k.reshape(-1, win, HEADS, DH)[:bs],
                       cache_swa_v.reshape(-1, win, HEADS, DH)[:bs])
    wl = min(2048, seq)
    swa_k_p, swa_v_p = [a.reshape(bp, wl, HEADS, DH)
                        for a in heads_layout([(k_p, 0), (proj_p, E_IN // ATT_W - 1)], bp, seq, wl)]
    swa_k_s = k_s.reshape(bs, 1, HEADS, DH)
    swa_v_s = v_s.reshape(bs, 1, HEADS, DH)

    xp = out_proj(ya_p, o_p, w_out, xp, m_p[2], seq, 1024, 1024, name="e_out_p")
    xs = out_proj(ya_s, o_s, w_out, xs, m_s[2], 1, bs, 512, name="e_out_s")

    def peer_layer(layer, xp, xs, m_p, m_s, final_norm):
        wq_t = cast_layer_bf16_t(peer_w_q, layer)
        u = cast_layer_bf16(peer_u, layer)
        v = cast_layer_bf16(peer_v, layer)
        xp = peer_block(xp, g_norm_ffn[layer], m_p[3], m_p[4], m_p[5], wq_t, peer_keys[layer], u, v,
                        g_final, seq, TM, 1024, final_norm)
        xs_pad = jnp.pad(xs, ((0, pad_s - bs), (0, 0)))
        xs_new = peer_block(xs_pad, g_norm_ffn[layer], m_s[3], m_s[4], m_s[5], wq_t, peer_keys[layer],
                            u, v, g_final, 1, pad_s, 1024, final_norm)
        return xp, xs_new[:bs]

    xp, xs = peer_layer(0, xp, xs, m_p, m_s, False)

    m_p, m_s = mods(1)
    w_in2 = o_w_in[0].astype(bf16)
    w_gate = jnp.pad(w_in2[:, O_MAIN:], ((0, 0), (0, LANES - 2 * M_HEADS)))
    w_out2 = o_w_out[0].astype(bf16)
    gate_bias = jnp.pad(o_b_if[0].reshape(1, 2 * M_HEADS), ((0, 0), (0, LANES - 2 * M_HEADS)))
    gmn = o_g_mnorm[0].reshape(1, M_WIDTH)

    proj2_p, gates_p = mod_matmul(xp, g_norm_mix[1], m_p[0], m_p[1], w_in2, O_MAIN, seq, 1024, 1024,
                                  w_gate=w_gate, name="o_in_p")
    proj2_s, gates_s = mod_matmul(xs, g_norm_mix[1], m_s[0], m_s[1], w_in2, O_MAIN, 1, bs, 512,
                                  w_gate=w_gate, name="o_in_s")

    oc_p = sb_prompt(proj2_p, o_sb_bias[0], bp, seq)
    oc_s = sb_step(proj2_s[:, :ATT_W], o_sb_bias[0], cache_sb_k.reshape(-1, PAGE, HEADS, DH),
                   cache_sb_v.reshape(-1, PAGE, HEADS, DH), page_table)
    hm_p, mC_p, mn_p, mm_p = mlstm_prompt(proj2_p, gates_p, gate_bias, gmn, bp, seq)
    hm_s, mC_s, mn_s, mm_s = mlstm_step(proj2_s, gates_s, gate_bias, gmn,
                                        state_mlstm_C.reshape(-1, M_HEADS, M_DH, M_DH)[:bs],
                                        state_mlstm_n[0], state_mlstm_m[0])

    n_pg = seq // PAGE
    sb_k_p, sb_v_p = [a.reshape(bp, n_pg, PAGE, HEADS, DH)
                      for a in heads_layout([(proj2_p, 1), (proj2_p, 2)], bp, seq, seq)]
    sb_k_s = proj2_s[:, ATT_W:2 * ATT_W].reshape(bs, 1, HEADS, DH)
    sb_v_s = proj2_s[:, 2 * ATT_W:3 * ATT_W].reshape(bs, 1, HEADS, DH)

    xp = out_proj(oc_p, hm_p, w_out2, xp, m_p[2], seq, 1024, 1024, name="o_out_p")
    xs = out_proj(oc_s, hm_s, w_out2, xs, m_s[2], 1, bs, 512, name="o_out_s")
    xp, xs = peer_layer(1, xp, xs, m_p, m_s, True)

    y_prompt = xp.reshape(bp, seq, d)
    y_sample = xs.reshape(bs, 1, d)
    st = lambda a: a[None]
    return (y_prompt, y_sample, st(conv_p), st(conv_s), st(h_p.reshape(bp, RG_WIDTH)), st(h_s),
            st(swa_k_p), st(swa_k_s), st(swa_v_p), st(swa_v_s),
            st(sb_k_p), st(sb_k_s), st(sb_v_p), st(sb_v_s),
            st(mC_p), st(mC_s), st(mn_p), st(mn_s), st(mm_p), st(mm_s))
```

```python
import functools
import math

import jax
import jax.numpy as jnp
from jax import lax
from jax.experimental import pallas as pl
from jax.experimental.pallas import tpu as pltpu

f32 = jnp.float32
bf16 = jnp.bfloat16

D_MODEL = 2048
PAST_LEN = 16384
PAGE = 128
RG_WIDTH = 1024
RG_BLOCKS = 8
CONV_W = 4
RG_C = 8.0
HEADS = 8
DH = 128
ATT_W = HEADS * DH
DIL_PATTERNS = ((128, 1), (512, 4), (2048, 16))
ROT_DIMS = 32
ROPE_THETA = 500000.0
M_HEADS = 4
M_DH = 256
M_WIDTH = M_HEADS * M_DH
CHUNK = 128
E_IN = 2 * RG_WIDTH + 3 * ATT_W
O_MAIN = 3 * ATT_W + 4 * M_WIDTH
PEER_HEADS = 8
N_KEYS = 128
N_EXPERTS = N_KEYS * N_KEYS
PEER_TOPK = 16
EPS = 1e-6
LANES = 128
SUBLANES = 8
VMEM_LIMIT = 56 * 1024 * 1024
NEG_INF = float("-inf")


def _cparams(*sem):
    return pltpu.CompilerParams(dimension_semantics=sem, vmem_limit_bytes=VMEM_LIMIT)


def _dot(a, b):
    return jnp.dot(a, b, preferred_element_type=f32)


def _dot_nt(a, b):
    return lax.dot_general(a, b, (((1,), (1,)), ((), ())), preferred_element_type=f32)


def _split3(x):
    hi = x.astype(bf16)
    r = x - hi.astype(f32)
    mid = r.astype(bf16)
    lo = (r - mid.astype(f32)).astype(bf16)
    return hi, mid, lo


def _sigmoid(x):
    return 1.0 / (1.0 + jnp.exp(-x))


def _log_sigmoid_pair(z):
    l1p = jnp.log1p(jnp.exp(-jnp.abs(z)))
    return jnp.minimum(z, 0.0) - l1p, -jnp.maximum(z, 0.0) - l1p


def _softplus(z):
    return jnp.maximum(z, 0.0) + jnp.log(1.0 + jnp.exp(-jnp.abs(z)))


def _gelu(x):
    c = math.sqrt(2.0 / math.pi)
    h = 0.5 * x
    return h + h * jnp.tanh(x * (c + (c * 0.044715) * (x * x)))


def _iota(shape, dim):
    return lax.broadcasted_iota(jnp.int32, shape, dim)


def _rowsum_bcast(x):
    ones = jnp.ones((LANES, LANES), bf16)
    hi = x.astype(bf16)
    lo = (x - hi.astype(f32)).astype(bf16)
    return _dot(hi, ones) + _dot(lo, ones)


def _adaln_kernel(c_ref, w_ref, b_ref, o_ref):
    c = c_ref[...]
    s = c * _sigmoid(c)
    w = w_ref[0]
    s_hi = s.astype(bf16)
    s_lo = (s - s_hi.astype(f32)).astype(bf16)
    w_hi = w.astype(bf16)
    w_lo = (w - w_hi.astype(f32)).astype(bf16)
    o_ref[0] = _dot(s_hi, w_hi) + _dot(s_hi, w_lo) + _dot(s_lo, w_hi) + b_ref[0]


def adaln_all(c_all, w_ada, b_ada):
    depth, d, n = w_ada.shape
    rows = c_all.shape[0]
    tn = 1024
    return pl.pallas_call(
        _adaln_kernel,
        grid=(depth, n // tn),
        in_specs=[
            pl.BlockSpec((rows, d), lambda l, j: (0, 0)),
            pl.BlockSpec((1, d, tn), lambda l, j: (l, 0, j)),
            pl.BlockSpec((1, 1, tn), lambda l, j: (l, 0, j)),
        ],
        out_specs=pl.BlockSpec((1, rows, tn), lambda l, j: (l, 0, j)),
        out_shape=jax.ShapeDtypeStruct((depth, rows, n), f32),
        compiler_params=_cparams("arbitrary", "arbitrary"),
        name="adaln",
    )(c_all, w_ada, b_ada.reshape(depth, 1, n))


def _modulated(x, g, sh, sc):
    ms = jnp.mean(x * x, axis=-1, keepdims=True)
    y = x * lax.rsqrt(ms + EPS) * g
    return y * (1.0 + sc) + sh


def _modmm_kernel(x_ref, g_ref, sh_ref, sc_ref, w_ref, o_ref, xn_ref):
    @pl.when(pl.program_id(1) == 0)
    def _():
        xn_ref[...] = _modulated(x_ref[...], g_ref[...], sh_ref[0], sc_ref[0]).astype(bf16)

    o_ref[...] = _dot(xn_ref[...], w_ref[...])


def _modmm_gate_kernel(x_ref, g_ref, sh_ref, sc_ref, w_ref, wg_ref, o_ref, og_ref, xn_ref):
    @pl.when(pl.program_id(1) == 0)
    def _():
        xn = _modulated(x_ref[...], g_ref[...], sh_ref[0], sc_ref[0]).astype(bf16)
        xn_ref[...] = xn
        og_ref[...] = _dot(xn, wg_ref[...])

    o_ref[...] = _dot(xn_ref[...], w_ref[...])


def _mod_specs(mod, m, tm, k, rows_per_batch):
    if rows_per_batch >= tm:
        assert rows_per_batch % tm == 0
        per = rows_per_batch // tm
        return mod.reshape(-1, 1, k), pl.BlockSpec((1, 1, k), lambda i, j: (i // per, 0, 0))
    assert rows_per_batch == 1
    rows = mod
    if rows.shape[0] < m:
        rows = jnp.pad(rows, ((0, m - rows.shape[0]), (0, 0)))
    return rows.reshape(1, m, k), pl.BlockSpec((1, tm, k), lambda i, j: (0, i, 0))


def mod_matmul(x, g, shift, scale, w, n_out, rows_per_batch, tm, tn, w_gate=None, name="modmm"):
    m, k = x.shape
    sh, sh_spec = _mod_specs(shift, m, tm, k, rows_per_batch)
    sc, sc_spec = _mod_specs(scale, m, tm, k, rows_per_batch)
    in_specs = [
        pl.BlockSpec((tm, k), lambda i, j: (i, 0)),
        pl.BlockSpec((1, k), lambda i, j: (0, 0)),
        sh_spec,
        sc_spec,
        pl.BlockSpec((k, tn), lambda i, j: (0, j)),
    ]
    args = [x, g.reshape(1, k), sh, sc, w]
    out_specs = pl.BlockSpec((tm, tn), lambda i, j: (i, j))
    out_shape = jax.ShapeDtypeStruct((m, n_out), f32)
    kern = _modmm_kernel
    if w_gate is not None:
        in_specs.append(pl.BlockSpec((k, LANES), lambda i, j: (0, 0)))
        args.append(w_gate)
        out_specs = [out_specs, pl.BlockSpec((tm, LANES), lambda i, j: (i, 0))]
        out_shape = [out_shape, jax.ShapeDtypeStruct((m, LANES), f32)]
        kern = _modmm_gate_kernel
    return pl.pallas_call(
        kern,
        grid=(m // tm, n_out // tn),
        in_specs=in_specs,
        out_specs=out_specs,
        out_shape=out_shape,
        scratch_shapes=[pltpu.VMEM((tm, k), bf16)],
        compiler_params=_cparams("arbitrary", "arbitrary"),
        name=name,
    )(*args)


def _outproj_kernel(a1_ref, a2_ref, w1_ref, w2_ref, x_ref, gt_ref, o_ref):
    y = _dot(a1_ref[...].astype(bf16), w1_ref[...]) + _dot(a2_ref[...].astype(bf16), w2_ref[...])
    o_ref[...] = x_ref[...] + gt_ref[0] * y


def out_proj(a1, a2, w, x, gate, rows_per_batch, tm, tn, name="outproj"):
    m, k1 = a1.shape
    k2 = a2.shape[1]
    n = w.shape[1]
    gt, gt_spec = _mod_specs(gate, m, tm, n, rows_per_batch)
    if gt.shape[1] == 1:
        per = rows_per_batch // tm
        gt_spec = pl.BlockSpec((1, 1, tn), lambda i, j: (i // per, 0, j))
    else:
        gt_spec = pl.BlockSpec((1, tm, tn), lambda i, j: (0, i, j))
    return pl.pallas_call(
        _outproj_kernel,
        grid=(m // tm, n // tn),
        in_specs=[
            pl.BlockSpec((tm, k1), lambda i, j: (i, 0)),
            pl.BlockSpec((tm, k2), lambda i, j: (i, 0)),
            pl.BlockSpec((k1, tn), lambda i, j: (0, j)),
            pl.BlockSpec((k2, tn), lambda i, j: (k1 // k2, j)),
            pl.BlockSpec((tm, tn), lambda i, j: (i, j)),
            gt_spec,
        ],
        out_specs=pl.BlockSpec((tm, tn), lambda i, j: (i, j)),
        out_shape=jax.ShapeDtypeStruct((m, n), f32),
        compiler_params=_cparams("arbitrary", "arbitrary"),
        name=name,
    )(a1, a2, w, w, x, gt)


def _rglru_gates(xc, wr_ref, br, wi_ref, bi, lam):
    xb = xc.astype(bf16)
    rs, gs = [], []
    for hb in range(RG_BLOCKS):
        sl = slice(hb * LANES, (hb + 1) * LANES)
        rs.append(_dot(xb[:, sl], wr_ref[hb]))
        gs.append(_dot(xb[:, sl], wi_ref[hb]))
    r = _sigmoid(jnp.concatenate(rs, axis=1) + br)
    ig = _sigmoid(jnp.concatenate(gs, axis=1) + bi)
    softplus_neg_lam = jnp.maximum(-lam, 0.0) + jnp.log1p(jnp.exp(-jnp.abs(lam)))
    log_a = -RG_C * r * softplus_neg_lam
    a = jnp.exp(log_a)
    u = jnp.sqrt(-jnp.tanh(log_a) * (a * a + 1.0)) * ig * xc
    return a, u


def _rglru_kernel(xa_ref, ga_ref, cw_ref, cb_ref, wr_ref, br_ref, wi_ref, bi_ref, lam_ref,
                  ya_ref, hl_ref, xprev_ref, hc_ref):
    t_idx = pl.program_id(1)
    tt = xa_ref.shape[0]

    @pl.when(t_idx == 0)
    def _():
        xprev_ref[...] = jnp.zeros_like(xprev_ref)
        hc_ref[...] = jnp.zeros_like(hc_ref)

    xa = xa_ref[...]
    xprev = xprev_ref[...]
    row8 = _iota((8, RG_WIDTH), 0)
    xc = cb_ref[...] + cw_ref[CONV_W - 1:CONV_W, :] * xa
    for k in range(1, CONV_W):
        rolled = pltpu.roll(xa, k, 0)
        head = jnp.where(row8 < k, pltpu.roll(xprev, k, 0), rolled[0:8])
        shifted = jnp.concatenate([head, rolled[8:]], axis=0)
        xc = xc + cw_ref[CONV_W - 1 - k:CONV_W - k, :] * shifted
    xprev_ref[...] = xa[tt - 8:tt]

    a, u = _rglru_gates(xc, wr_ref, br_ref[...], wi_ref, bi_ref[...], lam_ref[...])
    row = _iota((tt, RG_WIDTH), 0)
    s = 1
    while s < tt:
        a_sh = pltpu.roll(a, s, 0)
        u_sh = pltpu.roll(u, s, 0)
        ok = row >= s
        u = jnp.where(ok, a * u_sh + u, u)
        a = jnp.where(ok, a * a_sh, a)
        s *= 2
    h = a * hc_ref[...] + u
    hc_ref[...] = h[tt - 1:tt]
    hl_ref[0] = h[tt - 1:tt]
    ya_ref[...] = h * _gelu(ga_ref[...])


def rglru_prompt(proj, bsz, seq, cw, cb, wr, br, wi, bi, lam, tt=256):
    nt = seq // tt
    vec = lambda: pl.BlockSpec((1, RG_WIDTH), lambda b, t: (0, 0))
    return pl.pallas_call(
        _rglru_kernel,
        grid=(bsz, nt),
        in_specs=[
            pl.BlockSpec((tt, RG_WIDTH), lambda b, t: (b * nt + t, 0)),
            pl.BlockSpec((tt, RG_WIDTH), lambda b, t: (b * nt + t, 1)),
            pl.BlockSpec((CONV_W, RG_WIDTH), lambda b, t: (0, 0)),
            vec(),
            pl.BlockSpec((RG_BLOCKS, LANES, LANES), lambda b, t: (0, 0, 0)),
            vec(),
            pl.BlockSpec((RG_BLOCKS, LANES, LANES), lambda b, t: (0, 0, 0)),
            vec(),
            vec(),
        ],
        out_specs=[
            pl.BlockSpec((tt, RG_WIDTH), lambda b, t: (b * nt + t, 0)),
            pl.BlockSpec((1, 1, RG_WIDTH), lambda b, t: (b, 0, 0)),
        ],
        out_shape=[
            jax.ShapeDtypeStruct((bsz * seq, RG_WIDTH), f32),
            jax.ShapeDtypeStruct((bsz, 1, RG_WIDTH), f32),
        ],
        scratch_shapes=[pltpu.VMEM((8, RG_WIDTH), f32), pltpu.VMEM((1, RG_WIDTH), f32)],
        compiler_params=_cparams("arbitrary", "arbitrary"),
        name="rglru_prompt",
    )(proj, proj, cw, cb, wr, br, wi, bi, lam)


def _rglru_step_kernel(xa_ref, ga_ref, b0_ref, b1_ref, b2_ref, h0_ref, cw_ref, cb_ref,
                       wr_ref, br_ref, wi_ref, bi_ref, lam_ref, ya_ref, h_ref):
    xa = xa_ref[...]
    xc = (cb_ref[...] + cw_ref[0:1, :] * b0_ref[...] + cw_ref[1:2, :] * b1_ref[...]
          + cw_ref[2:3, :] * b2_ref[...] + cw_ref[3:4, :] * xa)
    a, u = _rglru_gates(xc, wr_ref, br_ref[...], wi_ref, bi_ref[...], lam_ref[...])
    h = a * h0_ref[...] + u
    h_ref[...] = h
    ya_ref[...] = h * _gelu(ga_ref[...])


def rglru_step(proj_s, conv_state, h0, cw, cb, wr, br, wi, bi, lam):
    n = proj_s.shape[0]
    full = lambda shape: pl.BlockSpec(shape, lambda i: tuple(0 for _ in shape))
    return pl.pallas_call(
        _rglru_step_kernel,
        grid=(1,),
        in_specs=[
            pl.BlockSpec((n, RG_WIDTH), lambda i: (0, 0)),
            pl.BlockSpec((n, RG_WIDTH), lambda i: (0, 1)),
            full((n, RG_WIDTH)), full((n, RG_WIDTH)), full((n, RG_WIDTH)), full((n, RG_WIDTH)),
            full((CONV_W, RG_WIDTH)), full((1, RG_WIDTH)),
            full((RG_BLOCKS, LANES, LANES)), full((1, RG_WIDTH)),
            full((RG_BLOCKS, LANES, LANES)), full((1, RG_WIDTH)), full((1, RG_WIDTH)),
        ],
        out_specs=[full((n, RG_WIDTH)), full((n, RG_WIDTH))],
        out_shape=[jax.ShapeDtypeStruct((n, RG_WIDTH), f32)] * 2,
        compiler_params=_cparams("arbitrary"),
        name="rglru_step",
    )(proj_s, proj_s, conv_state[:, 0], conv_state[:, 1], conv_state[:, 2], h0,
      cw, cb, wr, br, wi, bi, lam)


def _rope_tables(pos):
    half = ROT_DIMS // 2
    inv = ROPE_THETA ** (-jnp.arange(half, dtype=f32) / half)
    ang = pos.astype(f32)[:, None] * inv[None, :]
    cos, sin = jnp.cos(ang), jnp.sin(ang)
    n = pos.shape[0]
    ctab = jnp.concatenate([cos, cos, jnp.ones((n, DH - ROT_DIMS), f32)], axis=1)
    stab = jnp.concatenate([-sin, sin, jnp.zeros((n, DH - ROT_DIMS), f32)], axis=1)
    return ctab, stab


def _rope_head(xh, ctab, stab, lane):
    half = ROT_DIMS // 2
    partner = jnp.where(lane < half, pltpu.roll(xh, DH - half, 1), pltpu.roll(xh, half, 1))
    return xh * ctab + partner * stab


def _rope_kernel(q_ref, k_ref, v_ref, c_ref, s_ref, qo_ref, ko_ref, vo_ref):
    ctab, stab = c_ref[...], s_ref[...]
    lane = _iota((q_ref.shape[0], DH), 1)
    for hb in range(HEADS):
        sl = slice(hb * DH, (hb + 1) * DH)
        qo_ref[:, hb, :] = _rope_head(q_ref[:, sl], ctab, stab, lane) * (DH ** -0.5)
        ko_ref[:, hb, :] = _rope_head(k_ref[:, sl], ctab, stab, lane)
        vo_ref[:, hb, :] = v_ref[:, sl]


def rope_qkv(proj, ctab, stab, rows_per_seq, tt):
    m = proj.shape[0]
    nt = max(rows_per_seq // tt, 1)
    if ctab.shape[0] == 1:
        tab_spec = pl.BlockSpec((1, DH), lambda i: (0, 0))
    else:
        tab_spec = pl.BlockSpec((tt, DH), lambda i: (i % nt, 0))
    return pl.pallas_call(
        _rope_kernel,
        grid=(m // tt,),
        in_specs=[
            pl.BlockSpec((tt, ATT_W), lambda i: (i, 2)),
            pl.BlockSpec((tt, ATT_W), lambda i: (i, 3)),
            pl.BlockSpec((tt, ATT_W), lambda i: (i, 4)),
            tab_spec, tab_spec,
        ],
        out_specs=[pl.BlockSpec((tt, HEADS, DH), lambda i: (i, 0, 0))] * 3,
        out_shape=[jax.ShapeDtypeStruct((m, HEADS, DH), f32)] * 3,
        compiler_params=_cparams("arbitrary"),
        name="rope",
    )(proj, proj, proj, ctab, stab)


def _rope_split_kernel(q_ref, k_ref, v_ref, c_ref, s_ref, *refs):
    n_pat = len(DIL_PATTERNS)
    outs, (kf_ref, qs_ref, ks_ref) = refs[:3 * n_pat], refs[3 * n_pat:]
    tt = q_ref.shape[0]
    ctab, stab = c_ref[...], s_ref[...]
    lane = _iota((tt, DH), 1)
    qs_ref[...] = _rope_head(q_ref[...], ctab, stab, lane) * (DH ** -0.5)
    k_rot = _rope_head(k_ref[...], ctab, stab, lane)
    ks_ref[...] = k_rot
    kf_ref[...] = k_rot
    for gi, (_, d) in enumerate(DIL_PATTERNS):
        rows = tt // d
        for src, dst in ((qs_ref, outs[3 * gi]), (ks_ref, outs[3 * gi + 1]), (v_ref, outs[3 * gi + 2])):
            for r in range(d):
                dst[0, 0, r] = src[pl.ds(r, rows, stride=d), :].astype(bf16)


def rope_split(proj, ctab, stab, bsz, seq, tt=1024):
    m = proj.shape[0]
    nt = seq // tt
    col = lambda c: pl.BlockSpec((tt, DH), lambda i, h, c=c: (i, c * HEADS + h))
    tab = pl.BlockSpec((tt, DH), lambda i, h: (i % nt, 0))
    out_specs, out_shape = [], []
    for (_, d) in DIL_PATTERNS:
        for _ in range(3):
            out_specs.append(pl.BlockSpec((1, 1, d, tt // d, DH), lambda i, h: (i // nt, h, 0, i % nt, 0)))
            out_shape.append(jax.ShapeDtypeStruct((bsz, HEADS, d, seq // d, DH), bf16))
    out_specs.append(pl.BlockSpec((tt, DH), lambda i, h: (i, h)))
    out_shape.append(jax.ShapeDtypeStruct((m, ATT_W), f32))
    res = pl.pallas_call(
        _rope_split_kernel,
        grid=(m // tt, HEADS),
        in_specs=[col(2), col(3), col(4), tab, tab],
        out_specs=out_specs,
        out_shape=out_shape,
        scratch_shapes=[pltpu.VMEM((tt, DH), f32), pltpu.VMEM((tt, DH), f32)],
        compiler_params=_cparams("arbitrary", "arbitrary"),
        name="rope_split",
    )(proj, proj, proj, ctab, stab)
    return [res[3 * gi:3 * gi + 3] for gi in range(len(DIL_PATTERNS))], res[-1]


def _dil_kernel(q_ref, kc_ref, kp_ref, vc_ref, vp_ref, o_ref, l_ref, s_ref, p_ref, *, span):
    tq = q_ref.shape[3]
    blk = pl.program_id(2)
    qi = _iota((tq, 2 * tq), 0)
    col = _iota((tq, 2 * tq), 1)
    rel = jnp.where(col < tq, qi - col, qi - col + 2 * tq)
    ok = (rel >= 0) & (rel <= span) & ((col < tq) | (blk > 0))
    for hb in range(HEADS):
        qh = q_ref[0, hb, 0]
        s_ref[hb, :, :tq] = _dot_nt(qh, kc_ref[0, hb, 0])
        s_ref[hb, :, tq:] = _dot_nt(qh, kp_ref[0, hb, 0])
    for hb in range(HEADS):
        s = jnp.where(ok, s_ref[hb], NEG_INF)
        mx = jnp.max(s, axis=1, keepdims=True)
        p = jnp.exp(s - mx)
        den = jnp.sum(p, axis=1, keepdims=True)
        p_ref[hb] = (p * (1.0 / den)).astype(bf16)
        l_ref[0, hb, 0] = jnp.broadcast_to(mx + jnp.log(den), (tq, DH))
    for hb in range(HEADS):
        o_ref[0, hb, 0] = (_dot(p_ref[hb, :, :tq], vc_ref[0, hb, 0])
                           + _dot(p_ref[hb, :, tq:], vp_ref[0, hb, 0]))


def _dil_merge_kernel(*refs):
    n_pat = len(DIL_PATTERNS)
    ins, o_ref, scr = refs[:2 * n_pat], refs[2 * n_pat], refs[2 * n_pat + 1:]
    tt = o_ref.shape[0]
    for gi, (_, d) in enumerate(DIL_PATTERNS):
        rows = tt // d
        for src, dst in ((ins[2 * gi], scr[2 * gi]), (ins[2 * gi + 1], scr[2 * gi + 1])):
            for r in range(d):
                dst[pl.ds(r, rows, stride=d), :] = src[0, 0, r]
    lses = [scr[2 * gi + 1][...] for gi in range(n_pat)]
    top = functools.reduce(jnp.maximum, lses)
    es = [jnp.exp(l - top) for l in lses]
    num = sum(scr[2 * gi][...] * es[gi] for gi in range(n_pat))
    o_ref[...] = num / sum(es)


def dilated_prompt(qkv_by_pattern, bsz, seq, tq=256, tt=1024):
    partial = []
    for (w, d), (qd, kd, vd) in zip(DIL_PATTERNS, qkv_by_pattern):
        sd = seq // d
        blk = (1, HEADS, 1, tq, DH)
        cur = pl.BlockSpec(blk, lambda b, r, i: (b, 0, r, i, 0))
        prev = pl.BlockSpec(blk, lambda b, r, i: (b, 0, r, jnp.maximum(i - 1, 0), 0))
        partial += pl.pallas_call(
            functools.partial(_dil_kernel, span=w // d),
            grid=(bsz, d, sd // tq),
            in_specs=[cur, cur, prev, cur, prev],
            out_specs=[cur, cur],
            out_shape=[jax.ShapeDtypeStruct((bsz, HEADS, d, sd, DH), f32)] * 2,
            scratch_shapes=[pltpu.VMEM((HEADS, tq, 2 * tq), f32), pltpu.VMEM((HEADS, tq, 2 * tq), bf16)],
            compiler_params=_cparams("arbitrary", "arbitrary", "arbitrary"),
            name=f"dilattn_d{d}",
        )(qd, kd, kd, vd, vd)
    nt = seq // tt
    in_specs = []
    for (_, d) in DIL_PATTERNS:
        in_specs += [pl.BlockSpec((1, 1, d, tt // d, DH), lambda i, h: (i // nt, h, 0, i % nt, 0))] * 2
    return pl.pallas_call(
        _dil_merge_kernel,
        grid=(bsz * nt, HEADS),
        in_specs=in_specs,
        out_specs=pl.BlockSpec((tt, DH), lambda i, h: (i, h)),
        out_shape=jax.ShapeDtypeStruct((bsz * seq, ATT_W), f32),
        scratch_shapes=[pltpu.VMEM((tt, DH), f32)] * (2 * len(DIL_PATTERNS)),
        compiler_params=_cparams("arbitrary", "arbitrary"),
        name="dilattn_merge",
    )(*partial)


def _dil_step_kernel(q_ref, kn_ref, vn_ref, k1_ref, k4_ref, k16_ref, v1_ref, v4_ref, v16_ref, o_ref):
    q = q_ref[0]
    kn, vn = kn_ref[0], vn_ref[0]
    s_self = _rowsum_bcast(q * kn)
    o_gs, lse_gs = [], []
    for k_ref, v_ref in ((k1_ref, v1_ref), (k4_ref, v4_ref), (k16_ref, v16_ref)):
        k3 = k_ref[0, :, 0]
        nk = k3.shape[0]
        s = _rowsum_bcast((k3 * q[None]).reshape(nk * HEADS, DH)).reshape(nk, HEADS, DH)
        mx = jnp.maximum(jnp.max(s, axis=0), s_self)
        p = jnp.exp(s - mx[None])
        p_self = jnp.exp(s_self - mx)
        den = jnp.sum(p, axis=0) + p_self
        num = jnp.sum(p * v_ref[0, :, 0], axis=0) + p_self * vn
        o_gs.append(num / den)
        lse_gs.append(mx + jnp.log(den))
    top = jnp.maximum(jnp.maximum(lse_gs[0], lse_gs[1]), lse_gs[2])
    es = [jnp.exp(l - top) for l in lse_gs]
    tot = es[0] + es[1] + es[2]
    o_ref[0] = (o_gs[0] * es[0] + o_gs[1] * es[1] + o_gs[2] * es[2]) / tot


def dilated_step(q4, k4, v4, cache_k, cache_v):
    n, win = cache_k.shape[0], cache_k.shape[1]
    one = pl.BlockSpec((1, HEADS, DH), lambda b: (b, 0, 0))
    args = [q4, k4, v4]
    in_specs = [one, one, one]
    for cache in (cache_k, cache_v):
        for (w, d) in DIL_PATTERNS:
            nkeys = w // d
            assert win % d == 0 and (win // d) % nkeys == 0 and (win - w) % (d * nkeys) == 0
            args.append(cache.reshape(n, win // d, d, HEADS, DH))
            in_specs.append(pl.BlockSpec((1, nkeys, 1, HEADS, DH),
                                         lambda b, blk=(win - w) // d // nkeys: (b, blk, 0, 0, 0)))
    return pl.pallas_call(
        _dil_step_kernel,
        grid=(n,),
        in_specs=in_specs,
        out_specs=one,
        out_shape=jax.ShapeDtypeStruct((n, HEADS, DH), f32),
        compiler_params=_cparams("arbitrary"),
        name="dilattn_step",
    )(*args).reshape(n, ATT_W)


def _sb_kernel(bias_ref, q_ref, k_ref, v_ref, o_ref, kb_ref, vb_ref, qs_ref, t_ref, spb_ref, wb_ref,
               acc_ref, run_ref):
    tq = q_ref.shape[1]
    nh = q_ref.shape[2] // DH
    kt_w = t_ref.shape[2]
    sub = PAGE
    hg = pl.program_id(1)
    qb = pl.program_id(2)

    @pl.when(qb == 0)
    def _():
        kb_ref[...] = k_ref[0].astype(bf16)
        vb_ref[...] = v_ref[0].astype(bf16)

    rr = _iota((sub, 2 * sub), 0)
    cc = _iota((sub, 2 * sub), 1)
    tri = jnp.where((rr > cc) | (cc >= sub), 1.0, 0.0).astype(bf16)
    for h in range(nh):
        qs_ref[h] = (q_ref[0, :, h * DH:(h + 1) * DH] * (DH ** -0.5)).astype(bf16)
    acc_ref[...] = jnp.zeros_like(acc_ref)
    run_ref[...] = jnp.zeros_like(run_ref)

    def macro(start, width, masked):
        start = pl.multiple_of(start, width)
        for h in range(nh):
            hs = slice(h * DH, (h + 1) * DH)
            z = _dot_nt(qs_ref[h], kb_ref[pl.ds(start, width), hs]) + bias_ref[hg * nh + h]
            sp = _softplus(z)
            t = z - sp
            if masked:
                ok = start + _iota((tq, width), 1) < qb * tq + _iota((tq, width), 0)
                sp = jnp.where(ok, sp, 0.0)
                t = jnp.where(ok, t, NEG_INF)
            t_ref[h, :, :width] = t
            spb_ref[h, :, :width] = sp.astype(bf16)
        for h in range(nh):
            run = run_ref[h]
            for kt in range(width // sub - 1, -1, -1):
                ks = slice(kt * sub, (kt + 1) * sub)
                cs = _dot(spb_ref[h, :, ks], tri)
                wb_ref[h, :, ks] = jnp.exp(t_ref[h, :, ks] - cs[:, :sub] - run).astype(bf16)
                run = run + cs[:, sub:]
            run_ref[h] = run
        for h in range(nh):
            hs = slice(h * DH, (h + 1) * DH)
            acc_ref[h] += _dot(wb_ref[h, :, :width], vb_ref[pl.ds(start, width), hs])

    macro(qb * tq, tq, True)
    n_part = (qb * tq % kt_w) // tq
    for j in range(1, kt_w // tq):
        @pl.when(n_part >= j)
        def _():
            macro((qb - j) * tq, tq, False)

    top = (qb * tq) // kt_w

    def body(it, _):
        macro((top - 1 - it) * kt_w, kt_w, False)
        return 0

    lax.fori_loop(0, top, body, 0)
    for h in range(nh):
        o_ref[0, :, h * DH:(h + 1) * DH] = acc_ref[h]


def sb_prompt(proj2, bias, bsz, seq, tq=256, nh=4, kt_w=512):
    assert kt_w % tq == 0 and seq % kt_w == 0
    p3 = proj2.reshape(bsz, seq, proj2.shape[1])
    nq = seq // tq
    ng = HEADS // nh
    wd = nh * DH
    return pl.pallas_call(
        _sb_kernel,
        grid=(bsz, ng, nq),
        in_specs=[
            pl.BlockSpec(memory_space=pltpu.SMEM),
            pl.BlockSpec((1, tq, wd), lambda b, g, i: (b, i, g)),
            pl.BlockSpec((1, seq, wd), lambda b, g, i: (b, 0, ng + g)),
            pl.BlockSpec((1, seq, wd), lambda b, g, i: (b, 0, 2 * ng + g)),
        ],
        out_specs=pl.BlockSpec((1, tq, wd), lambda b, g, i: (b, i, g)),
        out_shape=jax.ShapeDtypeStruct((bsz, seq, ATT_W), f32),
        scratch_shapes=[
            pltpu.VMEM((seq, wd), bf16), pltpu.VMEM((seq, wd), bf16),
            pltpu.VMEM((nh, tq, DH), bf16),
            pltpu.VMEM((nh, tq, kt_w), f32), pltpu.VMEM((nh, tq, kt_w), bf16),
            pltpu.VMEM((nh, tq, kt_w), bf16),
            pltpu.VMEM((nh, tq, DH), f32), pltpu.VMEM((nh, tq, PAGE), f32),
        ],
        compiler_params=_cparams("arbitrary", "arbitrary", "arbitrary"),
        name="sb_prompt",
    )(bias, p3, p3, p3).reshape(bsz * seq, ATT_W)


def _sb_step_kernel(pt_ref, q_ref, bias_ref, eye_ref, *refs, npg):
    k_refs, v_refs = refs[:npg], refs[npg:2 * npg]
    o_ref, acc_ref, carry_ref = refs[2 * npg:]
    j = pl.program_id(1)

    @pl.when(j == 0)
    def _():
        acc_ref[...] = jnp.zeros_like(acc_ref)
        carry_ref[...] = jnp.zeros_like(carry_ref)

    q = q_ref[0]
    bias = bias_ref[...]
    ones = jnp.ones((DH, DH), bf16)
    rr = _iota((PAGE, 2 * PAGE), 0)
    cc = _iota((PAGE, 2 * PAGE), 1)
    tri = jnp.where((rr > cc) | (cc >= PAGE), 1.0, 0.0).astype(bf16)
    acc = acc_ref[...]
    run = carry_ref[...]
    pages = range(npg)
    zbs = [_dot((k_refs[p][0] * q[None]).reshape(PAGE * HEADS, DH).astype(bf16), ones)
           .reshape(PAGE, HEADS, DH) for p in pages]
    zs = [jnp.sum(zb * eye_ref[...], axis=0) + bias for zb in zbs]
    sps = [_softplus(z) for z in zs]
    css = []
    for sp in sps:
        hi, mid, lo = _split3(sp)
        css.append(_dot(hi, tri) + _dot(mid, tri) + _dot(lo, tri))
    ws = []
    for z, sp, cs in zip(zs, sps, css):
        ws.append(jnp.exp(z - sp - cs[:, :PAGE] - run))
        run = run + cs[:, PAGE:]
    wbs = [_dot((eye_ref[...] * w[None]).reshape(PAGE * HEADS, DH).astype(bf16), ones)
           .reshape(PAGE, HEADS, DH) for w in ws]
    for p, wb in zip(pages, wbs):
        acc = acc + jnp.sum(wb * v_refs[p][0], axis=0)
    acc_ref[...] = acc
    carry_ref[...] = run

    @pl.when(j == pl.num_programs(1) - 1)
    def _():
        o_ref[0] = acc


def sb_step(q_s, bias, cache_k, cache_v, page_table, npg=8):
    n, n_pages = page_table.shape
    q4 = (q_s * (DH ** -0.5)).reshape(n, HEADS, DH)
    bias4 = jnp.broadcast_to(bias[:, None], (HEADS, DH))
    eye3 = jnp.broadcast_to(jnp.eye(PAGE, DH, dtype=f32)[:, None, :], (PAGE, HEADS, DH))

    def page(p):
        return lambda b, j, pt: (pt[b, n_pages - 1 - (j * npg + p)], 0, 0, 0)

    kv_specs = [pl.BlockSpec((1, PAGE, HEADS, DH), page(p)) for p in range(npg)]
    grid_spec = pltpu.PrefetchScalarGridSpec(
        num_scalar_prefetch=1,
        grid=(n, n_pages // npg),
        in_specs=[
            pl.BlockSpec((1, HEADS, DH), lambda b, j, pt: (b, 0, 0)),
            pl.BlockSpec((HEADS, DH), lambda b, j, pt: (0, 0)),
            pl.BlockSpec((PAGE, HEADS, DH), lambda b, j, pt: (0, 0, 0)),
        ] + kv_specs + kv_specs,
        out_specs=pl.BlockSpec((1, HEADS, DH), lambda b, j, pt: (b, 0, 0)),
        scratch_shapes=[pltpu.VMEM((HEADS, DH), f32), pltpu.VMEM((HEADS, DH), f32)],
    )
    return pl.pallas_call(
        functools.partial(_sb_step_kernel, npg=npg),
        grid_spec=grid_spec,
        out_shape=jax.ShapeDtypeStruct((n, HEADS, DH), f32),
        compiler_params=_cparams("arbitrary", "arbitrary"),
        name="sb_step",
    )(page_table, q4, bias4, eye3, *([cache_k] * npg), *([cache_v] * npg)).reshape(n, ATT_W)


def _mlstm_kernel(q_ref, k_ref, v_ref, og_ref, gate_ref, gb_ref, gn_ref,
                  h_ref, c_out, n_out, m_out, c_s, n_s, m_s):
    ci = pl.program_id(1)
    L = CHUNK

    @pl.when(ci == 0)
    def _():
        c_s[...] = jnp.zeros_like(c_s)
        n_s[...] = jnp.zeros_like(n_s)
        m_s[...] = jnp.zeros_like(m_s)

    gt = gate_ref[...] + gb_ref[...]
    gt_t = gt.T
    ri = _iota((L, L), 0)
    li = _iota((L, L), 1)
    causal = li <= ri
    tri_incl = jnp.where(causal, 1.0, 0.0).astype(bf16)
    tri_incl_t = jnp.where(ri <= li, 1.0, 0.0).astype(bf16)
    heads = range(M_HEADS)
    sls = [slice(h * M_DH, (h + 1) * M_DH) for h in heads]
    ig_col = [gt[:, h:h + 1] for h in heads]
    ig_row = [gt_t[h:h + 1, :] for h in heads]
    bcum_col, bcum_row = [], []
    for h in heads:
        lf_col = _log_sigmoid_pair(gt[:, M_HEADS + h:M_HEADS + h + 1])[0]
        lf_row = _log_sigmoid_pair(gt_t[M_HEADS + h:M_HEADS + h + 1, :])[0]
        c_hi, c_mid, c_lo = _split3(jnp.broadcast_to(lf_col, (L, L)))
        bcum_col.append(_dot(tri_incl, c_hi) + _dot(tri_incl, c_mid) + _dot(tri_incl, c_lo))
        r_hi, r_mid, r_lo = _split3(jnp.broadcast_to(lf_row, (L, L)))
        bcum_row.append(_dot(r_hi, tri_incl_t) + _dot(r_mid, tri_incl_t) + _dot(r_lo, tri_incl_t))
    qb = [q_ref[:, sl].astype(bf16) for sl in sls]
    kb = [(k_ref[:, sl] * (M_DH ** -0.5)).astype(bf16) for sl in sls]
    vb = [v_ref[:, sl].astype(bf16) for sl in sls]
    qk = [_dot_nt(qb[h], kb[h]) for h in heads]
    qc = [_dot_nt(qb[h], c_s[h].astype(bf16)) for h in heads]
    m_prev = [m_s[h:h + 1, :] for h in heads]
    m_t, sw, iw_col = [], [], []
    for h in heads:
        dlog = jnp.where(causal, bcum_col[h] - bcum_row[h] + ig_row[h], NEG_INF)
        inter = bcum_col[h] + m_prev[h]
        mt = jnp.maximum(inter, jnp.max(dlog, axis=1, keepdims=True))
        m_t.append(mt)
        sw.append(jnp.exp(dlog - mt) * qk[h])
        iw_col.append(jnp.exp(inter - mt)[:, 0:1])
    sv = [_dot(sw[h].astype(bf16), vb[h]) for h in heads]
    for h in heads:
        n_prev = n_s[h:h + 1, :]
        num = sv[h] + iw_col[h] * qc[h]
        qn = jnp.sum(qb[h].astype(f32) * n_prev.astype(bf16).astype(f32), axis=1, keepdims=True)
        den = jnp.sum(sw[h], axis=1, keepdims=True) + iw_col[h] * qn
        hout = num / jnp.maximum(jnp.abs(den), jnp.exp(-m_t[h][:, 0:1]))
        hn = hout * lax.rsqrt(jnp.mean(hout * hout, axis=1, keepdims=True) + EPS)
        h_ref[:, sls[h]] = hn * gn_ref[:, sls[h]] * _sigmoid(og_ref[:, sls[h]])
    for h in heads:
        m_last = m_t[h][L - 1:L, :]
        b_last = bcum_col[h][L - 1:L, :]
        wl_col = jnp.exp(b_last[:, 0:1] - bcum_col[h][:, 0:1] + ig_col[h] - m_last[:, 0:1])
        wl_row = jnp.exp(b_last - bcum_row[h][0:1, :] + ig_row[h] - m_last)
        dsc = jnp.exp(b_last + m_prev[h] - m_last)[:, 0:1]
        c_s[h] = dsc * c_s[h] + _dot((v_ref[:, sls[h]] * wl_col).T.astype(bf16), kb[h])
        wl8 = jnp.broadcast_to(wl_row, (8, L)).astype(bf16)
        n_s[h:h + 1, :] = dsc * n_s[h:h + 1, :] + _dot(wl8, kb[h])[0:1, :]
        m_s[h:h + 1, :] = m_last

    @pl.when(ci == pl.num_programs(1) - 1)
    def _():
        c_out[0] = c_s[...]
        n_out[0] = n_s[...]
        m_out[0] = m_s[...]


def mlstm_prompt(proj2, gates, gate_bias, g_mnorm, bsz, seq):
    nc = seq // CHUNK
    col = lambda c: pl.BlockSpec((CHUNK, M_WIDTH), lambda b, i, c=c: (b * nc + i, c))
    hm, c1, n1, m1 = pl.pallas_call(
        _mlstm_kernel,
        grid=(bsz, nc),
        in_specs=[
            col(3), col(4), col(5), col(6),
            pl.BlockSpec((CHUNK, LANES), lambda b, i: (b * nc + i, 0)),
            pl.BlockSpec((1, LANES), lambda b, i: (0, 0)),
            pl.BlockSpec((1, M_WIDTH), lambda b, i: (0, 0)),
        ],
        out_specs=[
            pl.BlockSpec((CHUNK, M_WIDTH), lambda b, i: (b * nc + i, 0)),
            pl.BlockSpec((1, M_HEADS, M_DH, M_DH), lambda b, i: (b, 0, 0, 0)),
            pl.BlockSpec((1, M_HEADS, M_DH), lambda b, i: (b, 0, 0)),
            pl.BlockSpec((1, M_HEADS, LANES), lambda b, i: (b, 0, 0)),
        ],
        out_shape=[
            jax.ShapeDtypeStruct((bsz * seq, M_WIDTH), f32),
            jax.ShapeDtypeStruct((bsz, M_HEADS, M_DH, M_DH), f32),
            jax.ShapeDtypeStruct((bsz, M_HEADS, M_DH), f32),
            jax.ShapeDtypeStruct((bsz, M_HEADS, LANES), f32),
        ],
        scratch_shapes=[
            pltpu.VMEM((M_HEADS, M_DH, M_DH), f32),
            pltpu.VMEM((M_HEADS, M_DH), f32),
            pltpu.VMEM((M_HEADS, LANES), f32),
        ],
        compiler_params=_cparams("arbitrary", "arbitrary"),
        name="mlstm_prompt",
    )(proj2, proj2, proj2, proj2, gates, gate_bias, g_mnorm)
    return hm, c1, n1, m1[:, :, 0]


def _mlstm_step_kernel(q_ref, k_ref, v_ref, og_ref, gn_ref, ig_ref, fg_ref, c_ref, n_ref, m_ref,
                       h_ref, c_out, n_out, m_out):
    q = q_ref[0, 0]
    k = k_ref[0, 0] * (M_DH ** -0.5)
    v = v_ref[0, 0]
    ig = ig_ref[0, 0]
    lf = _log_sigmoid_pair(fg_ref[0, 0])[0]
    m0 = m_ref[0, 0]
    c0 = c_ref[0, 0]
    n0 = n_ref[0, 0]
    inter = lf + m0
    m_t = jnp.maximum(inter, ig)
    dw = jnp.exp(ig - m_t)
    iw = jnp.exp(inter - m_t)
    rnd = lambda a: a.astype(bf16).astype(f32)
    qr, kr, vr = rnd(q), rnd(k), rnd(v)
    qk = jnp.sum(qr * kr, axis=1, keepdims=True)
    sw = dw * qk
    cq = jnp.sum(rnd(c0) * qr, axis=1, keepdims=True)
    num = rnd(sw) * vr + iw * cq
    den = sw + iw * jnp.sum(rnd(n0) * qr, axis=1, keepdims=True)
    hout = num / jnp.maximum(jnp.abs(den), jnp.exp(-m_t))
    c_out[0, 0] = iw * c0 + rnd(dw * v) * kr
    n_out[0, 0] = iw * n0 + rnd(dw) * kr
    m_out[0, 0] = m_t
    hn = hout * lax.rsqrt(jnp.mean(hout * hout, axis=0, keepdims=True) + EPS)
    h_ref[0, 0] = hn * gn_ref[0] * _sigmoid(og_ref[0, 0])


def mlstm_step(proj2_s, gates_s, gate_bias, g_mnorm, c0, n0, m0):
    n = proj2_s.shape[0]
    base = 3 * ATT_W
    seg = lambda i: proj2_s[:, base + i * M_WIDTH: base + (i + 1) * M_WIDTH].reshape(n, M_HEADS, M_DH)
    g = gates_s + gate_bias
    rowb = pl.BlockSpec((1, 1, 1, M_DH), lambda b, h: (b, h, 0, 0))
    colb = pl.BlockSpec((1, 1, M_DH, 1), lambda b, h: (b, h, 0, 0))
    scal = pl.BlockSpec((1, 1, 1, 1), lambda b, h: (b, h, 0, 0))
    hcol, c1, n1, m1 = pl.pallas_call(
        _mlstm_step_kernel,
        grid=(n, M_HEADS),
        in_specs=[
            rowb, rowb, colb, colb,
            pl.BlockSpec((1, M_DH, 1), lambda b, h: (h, 0, 0)),
            scal, scal,
            pl.BlockSpec((1, 1, M_DH, M_DH), lambda b, h: (b, h, 0, 0)),
            rowb, scal,
        ],
        out_specs=[colb, pl.BlockSpec((1, 1, M_DH, M_DH), lambda b, h: (b, h, 0, 0)), rowb, scal],
        out_shape=[
            jax.ShapeDtypeStruct((n, M_HEADS, M_DH, 1), f32),
            jax.ShapeDtypeStruct((n, M_HEADS, M_DH, M_DH), f32),
            jax.ShapeDtypeStruct((n, M_HEADS, 1, M_DH), f32),
            jax.ShapeDtypeStruct((n, M_HEADS, 1, 1), f32),
        ],
        compiler_params=_cparams("arbitrary", "arbitrary"),
        name="mlstm_step",
    )(seg(0)[:, :, None, :], seg(1)[:, :, None, :], seg(2)[..., None], seg(3)[..., None],
      g_mnorm.reshape(M_HEADS, M_DH, 1),
      g[:, 0:M_HEADS].reshape(n, M_HEADS, 1, 1), g[:, M_HEADS:2 * M_HEADS].reshape(n, M_HEADS, 1, 1),
      c0, n0[:, :, None, :], m0.reshape(n, M_HEADS, 1, 1))
    return hcol.reshape(n, M_WIDTH), c1, n1.reshape(n, M_HEADS, M_DH), m1.reshape(n, M_HEADS)


def _peerq_kernel(x_ref, g_ref, sh_ref, sc_ref, wt_ref, qt_ref, xmt_ref, xs_ref):
    @pl.when(pl.program_id(1) == 0)
    def _():
        xm = _modulated(x_ref[...], g_ref[...], sh_ref[0], sc_ref[0])
        xt = xm.T.astype(bf16)
        xs_ref[...] = xt
        xmt_ref[...] = xt

    qt_ref[...] = _dot(wt_ref[...], xs_ref[...])


def peer_query(x, g, shift, scale, wq_t, rows_per_batch, tm, tn=512):
    m, k = x.shape
    n = wq_t.shape[0]
    sh, sh_spec = _mod_specs(shift, m, tm, k, rows_per_batch)
    sc, sc_spec = _mod_specs(scale, m, tm, k, rows_per_batch)
    return pl.pallas_call(
        _peerq_kernel,
        grid=(m // tm, n // tn),
        in_specs=[
            pl.BlockSpec((tm, k), lambda i, j: (i, 0)),
            pl.BlockSpec((1, k), lambda i, j: (0, 0)),
            sh_spec, sc_spec,
            pl.BlockSpec((tn, k), lambda i, j: (j, 0)),
        ],
        out_specs=[pl.BlockSpec((tn, tm), lambda i, j: (j, i)),
                   pl.BlockSpec((k, tm), lambda i, j: (0, i))],
        out_shape=[jax.ShapeDtypeStruct((n, m), f32), jax.ShapeDtypeStruct((k, m), bf16)],
        scratch_shapes=[pltpu.VMEM((k, tm), bf16)],
        compiler_params=_cparams("arbitrary", "arbitrary"),
        name="peer_query",
    )(x, g.reshape(1, k), sh, sc, wq_t)


def _cast_t_kernel(w_ref, o_ref):
    o_ref[...] = w_ref[0].T.astype(bf16)


def cast_layer_bf16_t(w, layer, tr=512):
    _, rows, cols = w.shape
    return pl.pallas_call(
        _cast_t_kernel,
        grid=(rows // tr,),
        in_specs=[pl.BlockSpec((1, tr, cols), lambda i: (layer, i, 0))],
        out_specs=pl.BlockSpec((cols, tr), lambda i: (0, i)),
        out_shape=jax.ShapeDtypeStruct((cols, rows), bf16),
        compiler_params=_cparams("arbitrary"),
        name="cast_bf16_t",
    )(w)


def _heads_layout_kernel(*refs):
    n = len(refs) // 2
    for x_ref, o_ref in zip(refs[:n], refs[n:]):
        for hb in range(HEADS):
            o_ref[:, hb, :] = x_ref[:, hb * DH:(hb + 1) * DH]


def heads_layout(srcs, bsz, seq, tail, tt=512):
    nt, first = tail // tt, (seq - tail) // tt
    rows = lambda b, i: b * (seq // tt) + first + i
    return pl.pallas_call(
        _heads_layout_kernel,
        grid=(bsz, nt),
        in_specs=[pl.BlockSpec((tt, ATT_W), lambda b, i, c=c: (rows(b, i), c)) for _, c in srcs],
        out_specs=[pl.BlockSpec((tt, HEADS, DH), lambda b, i: (b * nt + i, 0, 0))] * len(srcs),
        out_shape=[jax.ShapeDtypeStruct((bsz * tail, HEADS, DH), f32)] * len(srcs),
        compiler_params=_cparams("arbitrary", "arbitrary"),
        name="heads_layout",
    )(*[x for x, _ in srcs])


def _cast_kernel(w_ref, o_ref):
    o_ref[...] = w_ref[0].astype(bf16)


def cast_layer_bf16(w, layer, tr=1024):
    _, rows, cols = w.shape
    return pl.pallas_call(
        _cast_kernel,
        grid=(rows // tr,),
        in_specs=[pl.BlockSpec((1, tr, cols), lambda i: (layer, i, 0))],
        out_specs=pl.BlockSpec((tr, cols), lambda i: (i, 0)),
        out_shape=jax.ShapeDtypeStruct((rows, cols), bf16),
        compiler_params=_cparams("arbitrary"),
        name="cast_bf16",
    )(w)


_N_TOP = PEER_TOPK + 1
_CAND_PAIRS = [(a, b) for a in range(_N_TOP) for b in range(_N_TOP) if (a + 1) * (b + 1) <= _N_TOP]
_N_CAND = -(-len(_CAND_PAIRS) // 8) * 8


def _extract_top(cur, ridx, n):
    vals = []
    big = float(cur.shape[0])
    for _ in range(n):
        mx = jnp.max(cur, axis=0, keepdims=True)
        first = jnp.min(jnp.where(cur == mx, ridx, big), axis=0, keepdims=True)
        cur = jnp.where(ridx == first, NEG_INF, cur)
        vals.append(mx)
    return vals


def _oddeven_mergesort_pairs(n):
    pairs = []

    def merge(lo, cnt, r):
        step = 2 * r
        if step < cnt:
            merge(lo, cnt, step)
            merge(lo + r, cnt, step)
            pairs.extend((i, i + r) for i in range(lo + r, lo + cnt - r, step))
        else:
            pairs.append((lo, lo + r))

    def sort(lo, cnt):
        if cnt > 1:
            sort(lo, cnt // 2)
            sort(lo + cnt // 2, cnt // 2)
            merge(lo, cnt, 1)

    sort(0, n)
    return pairs


_SORT16 = _oddeven_mergesort_pairs(PEER_TOPK)


def _exchange(y, i, j):
    y[i], y[j] = jnp.maximum(y[i], y[j]), jnp.minimum(y[i], y[j])


def _top17_network(s):
    n = PEER_TOPK
    y = [s[v * 8:(v + 1) * 8, :] for v in range(n)]
    for i, j in _SORT16:
        _exchange(y, i, j)
    dropped = None
    for shift in (4, 2, 1):
        other = [pltpu.roll(y[n - 1 - i], shift, 0) for i in range(n)]
        low = functools.reduce(jnp.maximum, [jnp.minimum(y[i], other[i]) for i in range(n)])
        dropped = low if dropped is None else jnp.maximum(dropped, low)
        y = [jnp.maximum(y[i], other[i]) for i in range(n)]
        dist = n // 2
        while dist:
            for i in range(n):
                if not i & dist:
                    _exchange(y, i, i + dist)
            dist //= 2
    return [v[0:1, :] for v in y] + [jnp.max(dropped, axis=0, keepdims=True)]


_ROUTER_UNROLL = 8


def _router_kernel(qt_ref, keys_ref, e1_ref, e2_ref, th_ref, cand_ref):
    cidx = _iota((_N_CAND, LANES), 0).astype(f32)
    k0 = keys_ref[0].astype(bf16)
    k1 = keys_ref[1].astype(bf16)
    cand_ref[...] = jnp.full(cand_ref.shape, NEG_INF, f32)

    def one_head(h, slot):
        r0 = pl.multiple_of(h * 2 * N_KEYS, 2 * N_KEYS)
        s1 = _dot(k0, qt_ref[pl.ds(r0, N_KEYS), :].astype(bf16))
        s2 = _dot(k1, qt_ref[pl.ds(r0 + N_KEYS, N_KEYS), :].astype(bf16))
        top_a = _top17_network(s1)
        top_b = _top17_network(s2)
        for ci, (a, b) in enumerate(_CAND_PAIRS):
            cand_ref[slot, ci:ci + 1, :] = top_a[a] + top_b[b]
        cs = _extract_top(cand_ref[slot], cidx, _N_TOP)
        z = jnp.ones_like(cs[0])
        for r in range(1, PEER_TOPK):
            z = z + jnp.exp(cs[r] - cs[0])
        inv_z = 1.0 / z
        mid = 0.5 * (cs[PEER_TOPK - 1] + cs[PEER_TOPK])
        o0 = pl.multiple_of(h * N_KEYS, N_KEYS)
        e1 = jnp.exp(s1 - top_a[0]) * inv_z
        e1_ref[:, h] = e1.reshape(N_KEYS // SUBLANES, SUBLANES, LANES)
        e2_ref[pl.ds(o0, N_KEYS), :] = jnp.exp(s2 - top_b[0])
        th_ref[pl.ds(h, 1), :] = jnp.exp(mid - cs[0]) * inv_z

    def body(hp, _):
        for slot in range(_ROUTER_UNROLL):
            one_head(hp * _ROUTER_UNROLL + slot, slot)
        return 0

    lax.fori_loop(0, PEER_HEADS // _ROUTER_UNROLL, body, 0)


def peer_router(qt, keys):
    n, m = qt.shape
    half = PEER_HEADS * N_KEYS
    return pl.pallas_call(
        _router_kernel,
        grid=(m // LANES,),
        in_specs=[pl.BlockSpec((n, LANES), lambda i: (0, i)),
                  pl.BlockSpec((2, N_KEYS, N_KEYS), lambda i: (0, 0, 0))],
        out_specs=[pl.BlockSpec((N_KEYS // SUBLANES, PEER_HEADS, SUBLANES, LANES), lambda i: (0, 0, 0, i)),
                   pl.BlockSpec((half, LANES), lambda i: (0, i)),
                   pl.BlockSpec((PEER_HEADS, LANES), lambda i: (0, i))],
        out_shape=[jax.ShapeDtypeStruct((N_KEYS // SUBLANES, PEER_HEADS, SUBLANES, m), f32),
                   jax.ShapeDtypeStruct((half, m), f32),
                   jax.ShapeDtypeStruct((PEER_HEADS, m), f32)],
        scratch_shapes=[pltpu.VMEM((_ROUTER_UNROLL, _N_CAND, LANES), f32)],
        compiler_params=_cparams("arbitrary"),
        name="peer_router",
    )(qt, keys)


_G_CHAINS = 3
_G_ROWS = SUBLANES


class _Chains:
    def __init__(self, n, zero_ref):
        self.deps, self.cnt, self.zero_ref = [None] * n, 0, zero_ref

    def take(self):
        return self.deps[self.cnt % len(self.deps)]

    def put(self, g):
        bits = lax.bitcast_convert_type(g[0:1, :], jnp.int32) & self.zero_ref[0:1, :]
        self.deps[self.cnt % len(self.deps)] = lax.bitcast_convert_type(bits, f32)
        self.cnt += 1


def _routing_block(e1_ref, e2_ref, th_ref, ii, ls, chains):
    sub = _G_ROWS
    blk, r = divmod(ii, SUBLANES)
    rows = [e1_ref[blk, h, r:r + 1, ls] for h in range(PEER_HEADS)]
    ths = [th_ref[h:h + 1, ls] for h in range(PEER_HEADS)]
    parts = []
    for gi in range(N_KEYS // sub):
        dep = chains.take()
        g = jnp.zeros((sub, LANES), f32)
        for h in range(PEER_HEADS):
            row = rows[h] if (dep is None or h) else rows[h] + dep
            pr = e2_ref[h * N_KEYS + gi * sub:h * N_KEYS + (gi + 1) * sub, ls] * row
            g = g + jnp.where(pr >= ths[h], pr, 0.0)
        chains.put(g)
        parts.append(g)
    return jnp.concatenate(parts, axis=0)


def _experts_kernel(xmt_ref, e1_ref, e2_ref, th_ref, u_ref, v_ref, x_ref, gt_ref, gf_ref, zero_ref,
                    o_ref, w_ref, *, final_norm):
    c = pl.program_id(1)
    tm, te = w_ref.shape

    @pl.when(c == 0)
    def _():
        o_ref[...] = jnp.zeros_like(o_ref)

    chains = _Chains(_G_CHAINS, zero_ref)
    act = _gelu(_dot(u_ref[...], xmt_ref[...]))
    for ii in range(te // N_KEYS):
        es = slice(ii * N_KEYS, (ii + 1) * N_KEYS)
        for lc in range(tm // LANES):
            ls = slice(lc * LANES, (lc + 1) * LANES)
            g_blk = _routing_block(e1_ref, e2_ref, th_ref, ii, ls, chains)
            w_ref[ls, es] = (g_blk * act[es, ls]).T.astype(bf16)
    o_ref[...] += _dot(w_ref[...], v_ref[...])

    @pl.when(c == pl.num_programs(1) - 1)
    def _():
        y = x_ref[...] + gt_ref[0] * o_ref[...]
        if final_norm:
            ms = jnp.mean(y * y, axis=-1, keepdims=True)
            y = y * lax.rsqrt(ms + EPS) * gf_ref[...]
        o_ref[...] = y


def peer_experts(xmt, e1t, e2t, th, u, v, x, gate, g_final, rows_per_batch, tm, te, final_norm):
    m, d = x.shape
    gt, gt_spec = _mod_specs(gate, m, tm, d, rows_per_batch)
    half = PEER_HEADS * N_KEYS
    return pl.pallas_call(
        functools.partial(_experts_kernel, final_norm=final_norm),
        grid=(m // tm, N_EXPERTS // te),
        in_specs=[
            pl.BlockSpec((d, tm), lambda i, c: (0, i)),
            pl.BlockSpec((te // N_KEYS // SUBLANES, PEER_HEADS, SUBLANES, tm), lambda i, c: (c, 0, 0, i)),
            pl.BlockSpec((half, tm), lambda i, c: (0, i)),
            pl.BlockSpec((PEER_HEADS, tm), lambda i, c: (0, i)),
            pl.BlockSpec((te, d), lambda i, c: (c, 0)),
            pl.BlockSpec((te, d), lambda i, c: (c, 0)),
            pl.BlockSpec((tm, d), lambda i, c: (i, 0)),
            gt_spec,
            pl.BlockSpec((1, d), lambda i, c: (0, 0)),
            pl.BlockSpec((SUBLANES, LANES), lambda i, c: (0, 0)),
        ],
        out_specs=pl.BlockSpec((tm, d), lambda i, c: (i, 0)),
        out_shape=jax.ShapeDtypeStruct((m, d), f32),
        scratch_shapes=[pltpu.VMEM((tm, te), bf16)],
        compiler_params=_cparams("arbitrary", "arbitrary"),
        name="peer_experts",
    )(xmt, e1t, e2t, th, u, v, x, gt, g_final.reshape(1, d), jnp.zeros((SUBLANES, LANES), jnp.int32))


def peer_block(x, g, shift, scale, gate, wq_t, keys, u, v, g_final, rows_per_batch, tm, te, final_norm):
    tm_q = 1024 if x.shape[0] % 1024 == 0 and rows_per_batch % 1024 == 0 else tm
    qt, xmt = peer_query(x, g, shift, scale, wq_t, rows_per_batch, tm_q)
    e1t, e2t, th = peer_router(qt, keys)
    return peer_experts(xmt, e1t, e2t, th, u, v, x, gate, g_final, rows_per_batch, tm, te, final_norm)


def kernel(x_prompt, x_sample, c_prompt, c_sample, state_rglru_conv, state_rglru_h, cache_swa_k, cache_swa_v, cache_sb_k, cache_sb_v, state_mlstm_C, state_mlstm_n, state_mlstm_m, page_table, w_ada, b_ada, g_norm_mix, g_norm_ffn, e_w_in, e_conv_w, e_conv_b, e_w_r, e_b_r, e_w_i, e_b_i, e_lambda, e_w_out, o_w_in, o_b_if, o_sb_bias, o_g_mnorm, o_w_out, peer_w_q, peer_keys, peer_u, peer_v, g_final):
    bp, seq, d = x_prompt.shape
    bs = x_sample.shape[0]
    mp = bp * seq
    pad_s = LANES
    xp = x_prompt.reshape(mp, d)
    xs = x_sample.reshape(bs, d)

    c_rows = 16
    c_all = jnp.concatenate([c_prompt, c_sample, jnp.zeros((c_rows - bp - bs, d), f32)], axis=0)
    mod = adaln_all(c_all, w_ada, b_ada)

    def mods(layer):
        parts = [mod[layer, :, i * d:(i + 1) * d] for i in range(6)]
        return [p[:bp] for p in parts], [p[bp:bp + bs] for p in parts]

    ctab_p, stab_p = _rope_tables(jnp.arange(seq, dtype=jnp.int32))
    ctab_s, stab_s = _rope_tables(jnp.full((1,), PAST_LEN, jnp.int32))

    TM = 512
    m_p, m_s = mods(0)
    w_in = e_w_in[0].astype(bf16)
    w_out = e_w_out[0].astype(bf16)
    cw, cb = e_conv_w[0], e_conv_b[0].reshape(1, -1)
    wr, wi = e_w_r[0].astype(bf16), e_w_i[0].astype(bf16)
    br, bi, lam = e_b_r[0].reshape(1, -1), e_b_i[0].reshape(1, -1), e_lambda[0].reshape(1, -1)

    proj_p = mod_matmul(xp, g_norm_mix[0], m_p[0], m_p[1], w_in, E_IN, seq, 1024, 1024, name="e_in_p")
    proj_s = mod_matmul(xs, g_norm_mix[0], m_s[0], m_s[1], w_in, E_IN, 1, bs, 512, name="e_in_s")

    ya_p, h_p = rglru_prompt(proj_p, bp, seq, cw, cb, wr, br, wi, bi, lam)
    ya_s, h_s = rglru_step(proj_s, state_rglru_conv[0], state_rglru_h[0], cw, cb, wr, br, wi, bi, lam)
    conv_p = proj_p.reshape(bp, seq, E_IN)[:, seq - (CONV_W - 1):, :RG_WIDTH]
    conv_s = jnp.concatenate([state_rglru_conv[0][:, 1:], proj_s[:, None, :RG_WIDTH]], axis=1)

    qkv_p, k_p = rope_split(proj_p, ctab_p, stab_p, bp, seq)
    q_s, k_s, v_s = rope_qkv(proj_s, ctab_s, stab_s, 1, bs)
    o_p = dilated_prompt(qkv_p, bp, seq)
    win = cache_swa_k.shape[2]
    o_s = dilated_step(q_s, k_s, v_s, cache_swa_k.reshape(-1, win, HEADS, DH)[:bs],
                       cache_swa_v.reshape(-1, win, HEADS, DH)[:bs])
    wl = min(2048, seq)
    swa_k_p, swa_v_p = [a.reshape(bp, wl, HEADS, DH)
                        for a in heads_layout([(k_p, 0), (proj_p, E_IN // ATT_W - 1)], bp, seq, wl)]
    swa_k_s = k_s.reshape(bs, 1, HEADS, DH)
    swa_v_s = v_s.reshape(bs, 1, HEADS, DH)

    xp = out_proj(ya_p, o_p, w_out, xp, m_p[2], seq, 1024, 1024, name="e_out_p")
    xs = out_proj(ya_s, o_s, w_out, xs, m_s[2], 1, bs, 512, name="e_out_s")

    def peer_layer(layer, xp, xs, m_p, m_s, final_norm):
        wq_t = cast_layer_bf16_t(peer_w_q, layer)
        u = cast_layer_bf16(peer_u, layer)
        v = cast_layer_bf16(peer_v, layer)
        xp = peer_block(xp, g_norm_ffn[layer], m_p[3], m_p[4], m_p[5], wq_t, peer_keys[layer], u, v,
                        g_final, seq, TM, 1024, final_norm)
        xs_pad = jnp.pad(xs, ((0, pad_s - bs), (0, 0)))
        xs_new = peer_block(xs_pad, g_norm_ffn[layer], m_s[3], m_s[4], m_s[5], wq_t, peer_keys[layer],
                            u, v, g_final, 1, pad_s, 1024, final_norm)
        return xp, xs_new[:bs]

    xp, xs = peer_layer(0, xp, xs, m_p, m_s, False)

    m_p, m_s = mods(1)
    w_in2 = o_w_in[0].astype(bf16)
    w_gate = jnp.pad(w_in2[:, O_MAIN:], ((0, 0), (0, LANES - 2 * M_HEADS)))
    w_out2 = o_w_out[0].astype(bf16)
    gate_bias = jnp.pad(o_b_if[0].reshape(1, 2 * M_HEADS), ((0, 0), (0, LANES - 2 * M_HEADS)))
    gmn = o_g_mnorm[0].reshape(1, M_WIDTH)

    proj2_p, gates_p = mod_matmul(xp, g_norm_mix[1], m_p[0], m_p[1], w_in2, O_MAIN, seq, 1024, 1024,
                                  w_gate=w_gate, name="o_in_p")
    proj2_s, gates_s = mod_matmul(xs, g_norm_mix[1], m_s[0], m_s[1], w_in2, O_MAIN, 1, bs, 512,
                                  w_gate=w_gate, name="o_in_s")

    oc_p = sb_prompt(proj2_p, o_sb_bias[0], bp, seq)
    oc_s = sb_step(proj2_s[:, :ATT_W], o_sb_bias[0], cache_sb_k.reshape(-1, PAGE, HEADS, DH),
                   cache_sb_v.reshape(-1, PAGE, HEADS, DH), page_table)
    hm_p, mC_p, mn_p, mm_p = mlstm_prompt(proj2_p, gates_p, gate_bias, gmn, bp, seq)
    hm_s, mC_s, mn_s, mm_s = mlstm_step(proj2_s, gates_s, gate_bias, gmn,
                                        state_mlstm_C.reshape(-1, M_HEADS, M_DH, M_DH)[:bs],
                                        state_mlstm_n[0], state_mlstm_m[0])

    n_pg = seq // PAGE
    sb_k_p, sb_v_p = [a.reshape(bp, n_pg, PAGE, HEADS, DH)
                      for a in heads_layout([(proj2_p, 1), (proj2_p, 2)], bp, seq, seq)]
    sb_k_s = proj2_s[:, ATT_W:2 * ATT_W].reshape(bs, 1, HEADS, DH)
    sb_v_s = proj2_s[:, 2 * ATT_W:3 * ATT_W].reshape(bs, 1, HEADS, DH)

    xp = out_proj(oc_p, hm_p, w_out2, xp, m_p[2], seq, 1024, 1024, name="o_out_p")
    xs = out_proj(oc_s, hm_s, w_out2, xs, m_s[2], 1, bs, 512, name="o_out_s")
    xp, xs = peer_layer(1, xp, xs, m_p, m_s, True)

    y_prompt = xp.reshape(bp, seq, d)
    y_sample = xs.reshape(bs, 1, d)
    st = lambda a: a[None]
    return (y_prompt, y_sample, st(conv_p), st(conv_s), st(h_p.reshape(bp, RG_WIDTH)), st(h_s),
            st(swa_k_p), st(swa_k_s), st(swa_v_p), st(swa_v_s),
            st(sb_k_p), st(sb_k_s), st(sb_v_p), st(sb_v_s),
            st(mC_p), st(mC_s), st(mn_p), st(mn_s), st(mm_p), st(mm_s))
```

```python
import functools
import math

import jax
import jax.numpy as jnp
from jax import lax
from jax.experimental import pallas as pl
from jax.experimental.pallas import tpu as pltpu

f32 = jnp.float32
bf16 = jnp.bfloat16

D_MODEL = 2048
PAST_LEN = 16384
PAGE = 128
RG_WIDTH = 1024
RG_BLOCKS = 8
CONV_W = 4
RG_C = 8.0
HEADS = 8
DH = 128
ATT_W = HEADS * DH
DIL_PATTERNS = ((128, 1), (512, 4), (2048, 16))
ROT_DIMS = 32
ROPE_THETA = 500000.0
M_HEADS = 4
M_DH = 256
M_WIDTH = M_HEADS * M_DH
CHUNK = 128
E_IN = 2 * RG_WIDTH + 3 * ATT_W
O_MAIN = 3 * ATT_W + 4 * M_WIDTH
PEER_HEADS = 8
N_KEYS = 128
N_EXPERTS = N_KEYS * N_KEYS
PEER_TOPK = 16
EPS = 1e-6
LANES = 128
SUBLANES = 8
VMEM_LIMIT = 56 * 1024 * 1024
NEG_INF = float("-inf")


def _cparams(*sem):
    return pltpu.CompilerParams(dimension_semantics=sem, vmem_limit_bytes=VMEM_LIMIT)


def _dot(a, b):
    return jnp.dot(a, b, preferred_element_type=f32)


def _dot_nt(a, b):
    return lax.dot_general(a, b, (((1,), (1,)), ((), ())), preferred_element_type=f32)


def _split3(x):
    hi = x.astype(bf16)
    r = x - hi.astype(f32)
    mid = r.astype(bf16)
    lo = (r - mid.astype(f32)).astype(bf16)
    return hi, mid, lo


def _sigmoid(x):
    return 1.0 / (1.0 + jnp.exp(-x))


def _log_sigmoid_pair(z):
    l1p = jnp.log1p(jnp.exp(-jnp.abs(z)))
    return jnp.minimum(z, 0.0) - l1p, -jnp.maximum(z, 0.0) - l1p


def _softplus(z):
    return jnp.maximum(z, 0.0) + jnp.log(1.0 + jnp.exp(-jnp.abs(z)))


def _gelu(x):
    c = math.sqrt(2.0 / math.pi)
    h = 0.5 * x
    return h + h * jnp.tanh(x * (c + (c * 0.044715) * (x * x)))


def _iota(shape, dim):
    return lax.broadcasted_iota(jnp.int32, shape, dim)


def _rowsum_bcast(x):
    ones = jnp.ones((LANES, LANES), bf16)
    hi = x.astype(bf16)
    lo = (x - hi.astype(f32)).astype(bf16)
    return _dot(hi, ones) + _dot(lo, ones)


def _adaln_kernel(c_ref, w_ref, b_ref, o_ref):
    c = c_ref[...]
    s = c * _sigmoid(c)
    w = w_ref[0]
    s_hi = s.astype(bf16)
    s_lo = (s - s_hi.astype(f32)).astype(bf16)
    w_hi = w.astype(bf16)
    w_lo = (w - w_hi.astype(f32)).astype(bf16)
    o_ref[0] = _dot(s_hi, w_hi) + _dot(s_hi, w_lo) + _dot(s_lo, w_hi) + b_ref[0]


def adaln_all(c_all, w_ada, b_ada):
    depth, d, n = w_ada.shape
    rows = c_all.shape[0]
    tn = 1024
    return pl.pallas_call(
        _adaln_kernel,
        grid=(depth, n // tn),
        in_specs=[
            pl.BlockSpec((rows, d), lambda l, j: (0, 0)),
            pl.BlockSpec((1, d, tn), lambda l, j: (l, 0, j)),
            pl.BlockSpec((1, 1, tn), lambda l, j: (l, 0, j)),
        ],
        out_specs=pl.BlockSpec((1, rows, tn), lambda l, j: (l, 0, j)),
        out_shape=jax.ShapeDtypeStruct((depth, rows, n), f32),
        compiler_params=_cparams("arbitrary", "arbitrary"),
        name="adaln",
    )(c_all, w_ada, b_ada.reshape(depth, 1, n))


def _modulated(x, g, sh, sc):
    ms = jnp.mean(x * x, axis=-1, keepdims=True)
    y = x * lax.rsqrt(ms + EPS) * g
    return y * (1.0 + sc) + sh


def _modmm_kernel(x_ref, g_ref, sh_ref, sc_ref, w_ref, o_ref, xn_ref):
    @pl.when(pl.program_id(1) == 0)
    def _():
        xn_ref[...] = _modulated(x_ref[...], g_ref[...], sh_ref[0], sc_ref[0]).astype(bf16)

    o_ref[...] = _dot(xn_ref[...], w_ref[...])


def _modmm_gate_kernel(x_ref, g_ref, sh_ref, sc_ref, w_ref, wg_ref, o_ref, og_ref, xn_ref):
    @pl.when(pl.program_id(1) == 0)
    def _():
        xn = _modulated(x_ref[...], g_ref[...], sh_ref[0], sc_ref[0]).astype(bf16)
        xn_ref[...] = xn
        og_ref[...] = _dot(xn, wg_ref[...])

    o_ref[...] = _dot(xn_ref[...], w_ref[...])


def _mod_specs(mod, m, tm, k, rows_per_batch):
    if rows_per_batch >= tm:
        assert rows_per_batch % tm == 0
        per = rows_per_batch // tm
        return mod.reshape(-1, 1, k), pl.BlockSpec((1, 1, k), lambda i, j: (i // per, 0, 0))
    assert rows_per_batch == 1
    rows = mod
    if rows.shape[0] < m:
        rows = jnp.pad(rows, ((0, m - rows.shape[0]), (0, 0)))
    return rows.reshape(1, m, k), pl.BlockSpec((1, tm, k), lambda i, j: (0, i, 0))


def mod_matmul(x, g, shift, scale, w, n_out, rows_per_batch, tm, tn, w_gate=None, name="modmm"):
    m, k = x.shape
    sh, sh_spec = _mod_specs(shift, m, tm, k, rows_per_batch)
    sc, sc_spec = _mod_specs(scale, m, tm, k, rows_per_batch)
    in_specs = [
        pl.BlockSpec((tm, k), lambda i, j: (i, 0)),
        pl.BlockSpec((1, k), lambda i, j: (0, 0)),
        sh_spec,
        sc_spec,
        pl.BlockSpec((k, tn), lambda i, j: (0, j)),
    ]
    args = [x, g.reshape(1, k), sh, sc, w]
    out_specs = pl.BlockSpec((tm, tn), lambda i, j: (i, j))
    out_shape = jax.ShapeDtypeStruct((m, n_out), f32)
    kern = _modmm_kernel
    if w_gate is not None:
        in_specs.append(pl.BlockSpec((k, LANES), lambda i, j: (0, 0)))
        args.append(w_gate)
        out_specs = [out_specs, pl.BlockSpec((tm, LANES), lambda i, j: (i, 0))]
        out_shape = [out_shape, jax.ShapeDtypeStruct((m, LANES), f32)]
        kern = _modmm_gate_kernel
    return pl.pallas_call(
        kern,
        grid=(m // tm, n_out // tn),
        in_specs=in_specs,
        out_specs=out_specs,
        out_shape=out_shape,
        scratch_shapes=[pltpu.VMEM((tm, k), bf16)],
        compiler_params=_cparams("arbitrary", "arbitrary"),
        name=name,
    )(*args)


def _outproj_kernel(a1_ref, a2_ref, w1_ref, w2_ref, x_ref, gt_ref, o_ref):
    y = _dot(a1_ref[...].astype(bf16), w1_ref[...]) + _dot(a2_ref[...].astype(bf16), w2_ref[...])
    o_ref[...] = x_ref[...] + gt_ref[0] * y


def out_proj(a1, a2, w, x, gate, rows_per_batch, tm, tn, name="outproj"):
    m, k1 = a1.shape
    k2 = a2.shape[1]
    n = w.shape[1]
    gt, gt_spec = _mod_specs(gate, m, tm, n, rows_per_batch)
    if gt.shape[1] == 1:
        per = rows_per_batch // tm
        gt_spec = pl.BlockSpec((1, 1, tn), lambda i, j: (i // per, 0, j))
    else:
        gt_spec = pl.BlockSpec((1, tm, tn), lambda i, j: (0, i, j))
    return pl.pallas_call(
        _outproj_kernel,
        grid=(m // tm, n // tn),
        in_specs=[
            pl.BlockSpec((tm, k1), lambda i, j: (i, 0)),
            pl.BlockSpec((tm, k2), lambda i, j: (i, 0)),
            pl.BlockSpec((k1, tn), lambda i, j: (0, j)),
            pl.BlockSpec((k2, tn), lambda i, j: (k1 // k2, j)),
            pl.BlockSpec((tm, tn), lambda i, j: (i, j)),
            gt_spec,
        ],
        out_specs=pl.BlockSpec((tm, tn), lambda i, j: (i, j)),
        out_shape=jax.ShapeDtypeStruct((m, n), f32),
        compiler_params=_cparams("arbitrary", "arbitrary"),
        name=name,
    )(a1, a2, w, w, x, gt)


def _rglru_gates(xc, wr_ref, br, wi_ref, bi, lam):
    xb = xc.astype(bf16)
    rs, gs = [], []
    for hb in range(RG_BLOCKS):
        sl = slice(hb * LANES, (hb + 1) * LANES)
        rs.append(_dot(xb[:, sl], wr_ref[hb]))
        gs.append(_dot(xb[:, sl], wi_ref[hb]))
    r = _sigmoid(jnp.concatenate(rs, axis=1) + br)
    ig = _sigmoid(jnp.concatenate(gs, axis=1) + bi)
    softplus_neg_lam = jnp.maximum(-lam, 0.0) + jnp.log1p(jnp.exp(-jnp.abs(lam)))
    log_a = -RG_C * r * softplus_neg_lam
    a = jnp.exp(log_a)
    u = jnp.sqrt(-jnp.tanh(log_a) * (a * a + 1.0)) * ig * xc
    return a, u


def _rglru_kernel(xa_ref, ga_ref, cw_ref, cb_ref, wr_ref, br_ref, wi_ref, bi_ref, lam_ref,
                  ya_ref, hl_ref, xprev_ref, hc_ref):
    t_idx = pl.program_id(1)
    tt = xa_ref.shape[0]

    @pl.when(t_idx == 0)
    def _():
        xprev_ref[...] = jnp.zeros_like(xprev_ref)
        hc_ref[...] = jnp.zeros_like(hc_ref)

    xa = xa_ref[...]
    xprev = xprev_ref[...]
    row8 = _iota((8, RG_WIDTH), 0)
    xc = cb_ref[...] + cw_ref[CONV_W - 1:CONV_W, :] * xa
    for k in range(1, CONV_W):
        rolled = pltpu.roll(xa, k, 0)
        head = jnp.where(row8 < k, pltpu.roll(xprev, k, 0), rolled[0:8])
        shifted = jnp.concatenate([head, rolled[8:]], axis=0)
        xc = xc + cw_ref[CONV_W - 1 - k:CONV_W - k, :] * shifted
    xprev_ref[...] = xa[tt - 8:tt]

    a, u = _rglru_gates(xc, wr_ref, br_ref[...], wi_ref, bi_ref[...], lam_ref[...])
    row = _iota((tt, RG_WIDTH), 0)
    s = 1
    while s < tt:
        a_sh = pltpu.roll(a, s, 0)
        u_sh = pltpu.roll(u, s, 0)
        ok = row >= s
        u = jnp.where(ok, a * u_sh + u, u)
        a = jnp.where(ok, a * a_sh, a)
        s *= 2
    h = a * hc_ref[...] + u
    hc_ref[...] = h[tt - 1:tt]
    hl_ref[0] = h[tt - 1:tt]
    ya_ref[...] = h * _gelu(ga_ref[...])


def rglru_prompt(proj, bsz, seq, cw, cb, wr, br, wi, bi, lam, tt=256):
    nt = seq // tt
    vec = lambda: pl.BlockSpec((1, RG_WIDTH), lambda b, t: (0, 0))
    return pl.pallas_call(
        _rglru_kernel,
        grid=(bsz, nt),
        in_specs=[
            pl.BlockSpec((tt, RG_WIDTH), lambda b, t: (b * nt + t, 0)),
            pl.BlockSpec((tt, RG_WIDTH), lambda b, t: (b * nt + t, 1)),
            pl.BlockSpec((CONV_W, RG_WIDTH), lambda b, t: (0, 0)),
            vec(),
            pl.BlockSpec((RG_BLOCKS, LANES, LANES), lambda b, t: (0, 0, 0)),
            vec(),
            pl.BlockSpec((RG_BLOCKS, LANES, LANES), lambda b, t: (0, 0, 0)),
            vec(),
            vec(),
        ],
        out_specs=[
            pl.BlockSpec((tt, RG_WIDTH), lambda b, t: (b * nt + t, 0)),
            pl.BlockSpec((1, 1, RG_WIDTH), lambda b, t: (b, 0, 0)),
        ],
        out_shape=[
            jax.ShapeDtypeStruct((bsz * seq, RG_WIDTH), f32),
            jax.ShapeDtypeStruct((bsz, 1, RG_WIDTH), f32),
        ],
        scratch_shapes=[pltpu.VMEM((8, RG_WIDTH), f32), pltpu.VMEM((1, RG_WIDTH), f32)],
        compiler_params=_cparams("arbitrary", "arbitrary"),
        name="rglru_prompt",
    )(proj, proj, cw, cb, wr, br, wi, bi, lam)


def _rglru_step_kernel(xa_ref, ga_ref, b0_ref, b1_ref, b2_ref, h0_ref, cw_ref, cb_ref,
                       wr_ref, br_ref, wi_ref, bi_ref, lam_ref, ya_ref, h_ref):
    xa = xa_ref[...]
    xc = (cb_ref[...] + cw_ref[0:1, :] * b0_ref[...] + cw_ref[1:2, :] * b1_ref[...]
          + cw_ref[2:3, :] * b2_ref[...] + cw_ref[3:4, :] * xa)
    a, u = _rglru_gates(xc, wr_ref, br_ref[...], wi_ref, bi_ref[...], lam_ref[...])
    h = a * h0_ref[...] + u
    h_ref[...] = h
    ya_ref[...] = h * _gelu(ga_ref[...])


def rglru_step(proj_s, conv_state, h0, cw, cb, wr, br, wi, bi, lam):
    n = proj_s.shape[0]
    full = lambda shape: pl.BlockSpec(shape, lambda i: tuple(0 for _ in shape))
    return pl.pallas_call(
        _rglru_step_kernel,
        grid=(1,),
        in_specs=[
            pl.BlockSpec((n, RG_WIDTH), lambda i: (0, 0)),
            pl.BlockSpec((n, RG_WIDTH), lambda i: (0, 1)),
            full((n, RG_WIDTH)), full((n, RG_WIDTH)), full((n, RG_WIDTH)), full((n, RG_WIDTH)),
            full((CONV_W, RG_WIDTH)), full((1, RG_WIDTH)),
            full((RG_BLOCKS, LANES, LANES)), full((1, RG_WIDTH)),
            full((RG_BLOCKS, LANES, LANES)), full((1, RG_WIDTH)), full((1, RG_WIDTH)),
        ],
        out_specs=[full((n, RG_WIDTH)), full((n, RG_WIDTH))],
        out_shape=[jax.ShapeDtypeStruct((n, RG_WIDTH), f32)] * 2,
        compiler_params=_cparams("arbitrary"),
        name="rglru_step",
    )(proj_s, proj_s, conv_state[:, 0], conv_state[:, 1], conv_state[:, 2], h0,
      cw, cb, wr, br, wi, bi, lam)


def _rope_tables(pos):
    half = ROT_DIMS // 2
    inv = ROPE_THETA ** (-jnp.arange(half, dtype=f32) / half)
    ang = pos.astype(f32)[:, None] * inv[None, :]
    cos, sin = jnp.cos(ang), jnp.sin(ang)
    n = pos.shape[0]
    ctab = jnp.concatenate([cos, cos, jnp.ones((n, DH - ROT_DIMS), f32)], axis=1)
    stab = jnp.concatenate([-sin, sin, jnp.zeros((n, DH - ROT_DIMS), f32)], axis=1)
    return ctab, stab


def _rope_head(xh, ctab, stab, lane):
    half = ROT_DIMS // 2
    partner = jnp.where(lane < half, pltpu.roll(xh, DH - half, 1), pltpu.roll(xh, half, 1))
    return xh * ctab + partner * stab


def _rope_kernel(q_ref, k_ref, v_ref, c_ref, s_ref, qo_ref, ko_ref, vo_ref):
    ctab, stab = c_ref[...], s_ref[...]
    lane = _iota((q_ref.shape[0], DH), 1)
    for hb in range(HEADS):
        sl = slice(hb * DH, (hb + 1) * DH)
        qo_ref[:, hb, :] = _rope_head(q_ref[:, sl], ctab, stab, lane) * (DH ** -0.5)
        ko_ref[:, hb, :] = _rope_head(k_ref[:, sl], ctab, stab, lane)
        vo_ref[:, hb, :] = v_ref[:, sl]


def rope_qkv(proj, ctab, stab, rows_per_seq, tt):
    m = proj.shape[0]
    nt = max(rows_per_seq // tt, 1)
    if ctab.shape[0] == 1:
        tab_spec = pl.BlockSpec((1, DH), lambda i: (0, 0))
    else:
        tab_spec = pl.BlockSpec((tt, DH), lambda i: (i % nt, 0))
    return pl.pallas_call(
        _rope_kernel,
        grid=(m // tt,),
        in_specs=[
            pl.BlockSpec((tt, ATT_W), lambda i: (i, 2)),
            pl.BlockSpec((tt, ATT_W), lambda i: (i, 3)),
            pl.BlockSpec((tt, ATT_W), lambda i: (i, 4)),
            tab_spec, tab_spec,
        ],
        out_specs=[pl.BlockSpec((tt, HEADS, DH), lambda i: (i, 0, 0))] * 3,
        out_shape=[jax.ShapeDtypeStruct((m, HEADS, DH), f32)] * 3,
        compiler_params=_cparams("arbitrary"),
        name="rope",
    )(proj, proj, proj, ctab, stab)


def _rope_split_kernel(q_ref, k_ref, v_ref, c_ref, s_ref, *refs):
    n_pat = len(DIL_PATTERNS)
    outs, (kf_ref, qs_ref, ks_ref) = refs[:3 * n_pat], refs[3 * n_pat:]
    tt = q_ref.shape[0]
    ctab, stab = c_ref[...], s_ref[...]
    lane = _iota((tt, DH), 1)
    qs_ref[...] = _rope_head(q_ref[...], ctab, stab, lane) * (DH ** -0.5)
    k_rot = _rope_head(k_ref[...], ctab, stab, lane)
    ks_ref[...] = k_rot
    kf_ref[...] = k_rot
    for gi, (_, d) in enumerate(DIL_PATTERNS):
        rows = tt // d
        for src, dst in ((qs_ref, outs[3 * gi]), (ks_ref, outs[3 * gi + 1]), (v_ref, outs[3 * gi + 2])):
            for r in range(d):
                dst[0, 0, r] = src[pl.ds(r, rows, stride=d), :].astype(bf16)


def rope_split(proj, ctab, stab, bsz, seq, tt=1024):
    m = proj.shape[0]
    nt = seq // tt
    col = lambda c: pl.BlockSpec((tt, DH), lambda i, h, c=c: (i, c * HEADS + h))
    tab = pl.BlockSpec((tt, DH), lambda i, h: (i % nt, 0))
    out_specs, out_shape = [], []
    for (_, d) in DIL_PATTERNS:
        for _ in range(3):
            out_specs.append(pl.BlockSpec((1, 1, d, tt // d, DH), lambda i, h: (i // nt, h, 0, i % nt, 0)))
            out_shape.append(jax.ShapeDtypeStruct((bsz, HEADS, d, seq // d, DH), bf16))
    out_specs.append(pl.BlockSpec((tt, DH), lambda i, h: (i, h)))
    out_shape.append(jax.ShapeDtypeStruct((m, ATT_W), f32))
    res = pl.pallas_call(
        _rope_split_kernel,
        grid=(m // tt, HEADS),
        in_specs=[col(2), col(3), col(4), tab, tab],
        out_specs=out_specs,
        out_shape=out_shape,
        scratch_shapes=[pltpu.VMEM((tt, DH), f32), pltpu.VMEM((tt, DH), f32)],
        compiler_params=_cparams("arbitrary", "arbitrary"),
        name="rope_split",
    )(proj, proj, proj, ctab, stab)
    return [res[3 * gi:3 * gi + 3] for gi in range(len(DIL_PATTERNS))], res[-1]


def _dil_kernel(q_ref, kc_ref, kp_ref, vc_ref, vp_ref, o_ref, l_ref, s_ref, p_ref, *, span):
    tq = q_ref.shape[3]
    blk = pl.program_id(2)
    qi = _iota((tq, 2 * tq), 0)
    col = _iota((tq, 2 * tq), 1)
    rel = jnp.where(col < tq, qi - col, qi - col + 2 * tq)
    ok = (rel >= 0) & (rel <= span) & ((col < tq) | (blk > 0))
    for hb in range(HEADS):
        qh = q_ref[0, hb, 0]
        s_ref[hb, :, :tq] = _dot_nt(qh, kc_ref[0, hb, 0])
        s_ref[hb, :, tq:] = _dot_nt(qh, kp_ref[0, hb, 0])
    for hb in range(HEADS):
        s = jnp.where(ok, s_ref[hb], NEG_INF)
        mx = jnp.max(s, axis=1, keepdims=True)
        p = jnp.exp(s - mx)
        den = jnp.sum(p, axis=1, keepdims=True)
        p_ref[hb] = (p * (1.0 / den)).astype(bf16)
        l_ref[0, hb, 0] = jnp.broadcast_to(mx + jnp.log(den), (tq, DH))
    for hb in range(HEADS):
        o_ref[0, hb, 0] = (_dot(p_ref[hb, :, :tq], vc_ref[0, hb, 0])
                           + _dot(p_ref[hb, :, tq:], vp_ref[0, hb, 0]))


def _dil_merge_kernel(*refs):
    n_pat = len(DIL_PATTERNS)
    ins, o_ref, scr = refs[:2 * n_pat], refs[2 * n_pat], refs[2 * n_pat + 1:]
    tt = o_ref.shape[0]
    for gi, (_, d) in enumerate(DIL_PATTERNS):
        rows = tt // d
        for src, dst in ((ins[2 * gi], scr[2 * gi]), (ins[2 * gi + 1], scr[2 * gi + 1])):
            for r in range(d):
                dst[pl.ds(r, rows, stride=d), :] = src[0, 0, r]
    lses = [scr[2 * gi + 1][...] for gi in range(n_pat)]
    top = functools.reduce(jnp.maximum, lses)
    es = [jnp.exp(l - top) for l in lses]
    num = sum(scr[2 * gi][...] * es[gi] for gi in range(n_pat))
    o_ref[...] = num / sum(es)


def dilated_prompt(qkv_by_pattern, bsz, seq, tq=256, tt=1024):
    partial = []
    for (w, d), (qd, kd, vd) in zip(DIL_PATTERNS, qkv_by_pattern):
        sd = seq // d
        blk = (1, HEADS, 1, tq, DH)
        cur = pl.BlockSpec(blk, lambda b, r, i: (b, 0, r, i, 0))
        prev = pl.BlockSpec(blk, lambda b, r, i: (b, 0, r, jnp.maximum(i - 1, 0), 0))
        partial += pl.pallas_call(
            functools.partial(_dil_kernel, span=w // d),
            grid=(bsz, d, sd // tq),
            in_specs=[cur, cur, prev, cur, prev],
            out_specs=[cur, cur],
            out_shape=[jax.ShapeDtypeStruct((bsz, HEADS, d, sd, DH), f32)] * 2,
            scratch_shapes=[pltpu.VMEM((HEADS, tq, 2 * tq), f32), pltpu.VMEM((HEADS, tq, 2 * tq), bf16)],
            compiler_params=_cparams("arbitrary", "arbitrary", "arbitrary"),
            name=f"dilattn_d{d}",
        )(qd, kd, kd, vd, vd)
    nt = seq // tt
    in_specs = []
    for (_, d) in DIL_PATTERNS:
        in_specs += [pl.BlockSpec((1, 1, d, tt // d, DH), lambda i, h: (i // nt, h, 0, i % nt, 0))] * 2
    return pl.pallas_call(
        _dil_merge_kernel,
        grid=(bsz * nt, HEADS),
        in_specs=in_specs,
        out_specs=pl.BlockSpec((tt, DH), lambda i, h: (i, h)),
        out_shape=jax.ShapeDtypeStruct((bsz * seq, ATT_W), f32),
        scratch_shapes=[pltpu.VMEM((tt, DH), f32)] * (2 * len(DIL_PATTERNS)),
        compiler_params=_cparams("arbitrary", "arbitrary"),
        name="dilattn_merge",
    )(*partial)


def _dil_step_kernel(q_ref, kn_ref, vn_ref, k1_ref, k4_ref, k16_ref, v1_ref, v4_ref, v16_ref, o_ref):
    q = q_ref[0]
    kn, vn = kn_ref[0], vn_ref[0]
    s_self = _rowsum_bcast(q * kn)
    o_gs, lse_gs = [], []
    for k_ref, v_ref in ((k1_ref, v1_ref), (k4_ref, v4_ref), (k16_ref, v16_ref)):
        k3 = k_ref[0, :, 0]
        nk = k3.shape[0]
        s = _rowsum_bcast((k3 * q[None]).reshape(nk * HEADS, DH)).reshape(nk, HEADS, DH)
        mx = jnp.maximum(jnp.max(s, axis=0), s_self)
        p = jnp.exp(s - mx[None])
        p_self = jnp.exp(s_self - mx)
        den = jnp.sum(p, axis=0) + p_self
        num = jnp.sum(p * v_ref[0, :, 0], axis=0) + p_self * vn
        o_gs.append(num / den)
        lse_gs.append(mx + jnp.log(den))
    top = jnp.maximum(jnp.maximum(lse_gs[0], lse_gs[1]), lse_gs[2])
    es = [jnp.exp(l - top) for l in lse_gs]
    tot = es[0] + es[1] + es[2]
    o_ref[0] = (o_gs[0] * es[0] + o_gs[1] * es[1] + o_gs[2] * es[2]) / tot


def dilated_step(q4, k4, v4, cache_k, cache_v):
    n, win = cache_k.shape[0], cache_k.shape[1]
    one = pl.BlockSpec((1, HEADS, DH), lambda b: (b, 0, 0))
    args = [q4, k4, v4]
    in_specs = [one, one, one]
    for cache in (cache_k, cache_v):
        for (w, d) in DIL_PATTERNS:
            nkeys = w // d
            assert win % d == 0 and (win // d) % nkeys == 0 and (win - w) % (d * nkeys) == 0
            args.append(cache.reshape(n, win // d, d, HEADS, DH))
            in_specs.append(pl.BlockSpec((1, nkeys, 1, HEADS, DH),
                                         lambda b, blk=(win - w) // d // nkeys: (b, blk, 0, 0, 0)))
    return pl.pallas_call(
        _dil_step_kernel,
        grid=(n,),
        in_specs=in_specs,
        out_specs=one,
        out_shape=jax.ShapeDtypeStruct((n, HEADS, DH), f32),
        compiler_params=_cparams("arbitrary"),
        name="dilattn_step",
    )(*args).reshape(n, ATT_W)


def _sb_kernel(bias_ref, q_ref, k_ref, v_ref, o_ref, kb_ref, vb_ref, qs_ref, t_ref, spb_ref, wb_ref,
               acc_ref, run_ref):
    tq = q_ref.shape[1]
    nh = q_ref.shape[2] // DH
    kt_w = t_ref.shape[2]
    sub = PAGE
    hg = pl.program_id(1)
    qb = pl.program_id(2)

    @pl.when(qb == 0)
    def _():
        kb_ref[...] = k_ref[0].astype(bf16)
        vb_ref[...] = v_ref[0].astype(bf16)

    rr = _iota((sub, 2 * sub), 0)
    cc = _iota((sub, 2 * sub), 1)
    tri = jnp.where((rr > cc) | (cc >= sub), 1.0, 0.0).astype(bf16)
    for h in range(nh):
        qs_ref[h] = (q_ref[0, :, h * DH:(h + 1) * DH] * (DH ** -0.5)).astype(bf16)
    acc_ref[...] = jnp.zeros_like(acc_ref)
    run_ref[...] = jnp.zeros_like(run_ref)

    def macro(start, width, masked):
        start = pl.multiple_of(start, width)
        for h in range(nh):
            hs = slice(h * DH, (h + 1) * DH)
            z = _dot_nt(qs_ref[h], kb_ref[pl.ds(start, width), hs]) + bias_ref[hg * nh + h]
            sp = _softplus(z)
            t = z - sp
            if masked:
                ok = start + _iota((tq, width), 1) < qb * tq + _iota((tq, width), 0)
                sp = jnp.where(ok, sp, 0.0)
                t = jnp.where(ok, t, NEG_INF)
            t_ref[h, :, :width] = t
            spb_ref[h, :, :width] = sp.astype(bf16)
        for h in range(nh):
            run = run_ref[h]
            for kt in range(width // sub - 1, -1, -1):
                ks = slice(kt * sub, (kt + 1) * sub)
                cs = _dot(spb_ref[h, :, ks], tri)
                wb_ref[h, :, ks] = jnp.exp(t_ref[h, :, ks] - cs[:, :sub] - run).astype(bf16)
                run = run + cs[:, sub:]
            run_ref[h] = run
        for h in range(nh):
            hs = slice(h * DH, (h + 1) * DH)
            acc_ref[h] += _dot(wb_ref[h, :, :width], vb_ref[pl.ds(start, width), hs])

    macro(qb * tq, tq, True)
    n_part = (qb * tq % kt_w) // tq
    for j in range(1, kt_w // tq):
        @pl.when(n_part >= j)
        def _():
            macro((qb - j) * tq, tq, False)

    top = (qb * tq) // kt_w

    def body(it, _):
        macro((top - 1 - it) * kt_w, kt_w, False)
        return 0

    lax.fori_loop(0, top, body, 0)
    for h in range(nh):
        o_ref[0, :, h * DH:(h + 1) * DH] = acc_ref[h]


def sb_prompt(proj2, bias, bsz, seq, tq=256, nh=4, kt_w=512):
    assert kt_w % tq == 0 and seq % kt_w == 0
    p3 = proj2.reshape(bsz, seq, proj2.shape[1])
    nq = seq // tq
    ng = HEADS // nh
    wd = nh * DH
    return pl.pallas_call(
        _sb_kernel,
        grid=(bsz, ng, nq),
        in_specs=[
            pl.BlockSpec(memory_space=pltpu.SMEM),
            pl.BlockSpec((1, tq, wd), lambda b, g, i: (b, i, g)),
            pl.BlockSpec((1, seq, wd), lambda b, g, i: (b, 0, ng + g)),
            pl.BlockSpec((1, seq, wd), lambda b, g, i: (b, 0, 2 * ng + g)),
        ],
        out_specs=pl.BlockSpec((1, tq, wd), lambda b, g, i: (b, i, g)),
        out_shape=jax.ShapeDtypeStruct((bsz, seq, ATT_W), f32),
        scratch_shapes=[
            pltpu.VMEM((seq, wd), bf16), pltpu.VMEM((seq, wd), bf16),
            pltpu.VMEM((nh, tq, DH), bf16),
            pltpu.VMEM((nh, tq, kt_w), f32), pltpu.VMEM((nh, tq, kt_w), bf16),
            pltpu.VMEM((nh, tq, kt_w), bf16),
            pltpu.VMEM((nh, tq, DH), f32), pltpu.VMEM((nh, tq, PAGE), f32),
        ],
        compiler_params=_cparams("arbitrary", "arbitrary", "arbitrary"),
        name="sb_prompt",
    )(bias, p3, p3, p3).reshape(bsz * seq, ATT_W)


def _sb_step_kernel(pt_ref, q_ref, bias_ref, eye_ref, *refs, npg):
    k_refs, v_refs = refs[:npg], refs[npg:2 * npg]
    o_ref, acc_ref, carry_ref = refs[2 * npg:]
    j = pl.program_id(1)

    @pl.when(j == 0)
    def _():
        acc_ref[...] = jnp.zeros_like(acc_ref)
        carry_ref[...] = jnp.zeros_like(carry_ref)

    q = q_ref[0]
    bias = bias_ref[...]
    ones = jnp.ones((DH, DH), bf16)
    rr = _iota((PAGE, 2 * PAGE), 0)
    cc = _iota((PAGE, 2 * PAGE), 1)
    tri = jnp.where((rr > cc) | (cc >= PAGE), 1.0, 0.0).astype(bf16)
    acc = acc_ref[...]
    run = carry_ref[...]
    pages = range(npg)
    zbs = [_dot((k_refs[p][0] * q[None]).reshape(PAGE * HEADS, DH).astype(bf16), ones)
           .reshape(PAGE, HEADS, DH) for p in pages]
    zs = [jnp.sum(zb * eye_ref[...], axis=0) + bias for zb in zbs]
    sps = [_softplus(z) for z in zs]
    css = []
    for sp in sps:
        hi, mid, lo = _split3(sp)
        css.append(_dot(hi, tri) + _dot(mid, tri) + _dot(lo, tri))
    ws = []
    for z, sp, cs in zip(zs, sps, css):
        ws.append(jnp.exp(z - sp - cs[:, :PAGE] - run))
        run = run + cs[:, PAGE:]
    wbs = [_dot((eye_ref[...] * w[None]).reshape(PAGE * HEADS, DH).astype(bf16), ones)
           .reshape(PAGE, HEADS, DH) for w in ws]
    for p, wb in zip(pages, wbs):
        acc = acc + jnp.sum(wb * v_refs[p][0], axis=0)
    acc_ref[...] = acc
    carry_ref[...] = run

    @pl.when(j == pl.num_programs(1) - 1)
    def _():
        o_ref[0] = acc


def sb_step(q_s, bias, cache_k, cache_v, page_table, npg=8):
    n, n_pages = page_table.shape
    q4 = (q_s * (DH ** -0.5)).reshape(n, HEADS, DH)
    bias4 = jnp.broadcast_to(bias[:, None], (HEADS, DH))
    eye3 = jnp.broadcast_to(jnp.eye(PAGE, DH, dtype=f32)[:, None, :], (PAGE, HEADS, DH))

    def page(p):
        return lambda b, j, pt: (pt[b, n_pages - 1 - (j * npg + p)], 0, 0, 0)

    kv_specs = [pl.BlockSpec((1, PAGE, HEADS, DH), page(p)) for p in range(npg)]
    grid_spec = pltpu.PrefetchScalarGridSpec(
        num_scalar_prefetch=1,
        grid=(n, n_pages // npg),
        in_specs=[
            pl.BlockSpec((1, HEADS, DH), lambda b, j, pt: (b, 0, 0)),
            pl.BlockSpec((HEADS, DH), lambda b, j, pt: (0, 0)),
            pl.BlockSpec((PAGE, HEADS, DH), lambda b, j, pt: (0, 0, 0)),
        ] + kv_specs + kv_specs,
        out_specs=pl.BlockSpec((1, HEADS, DH), lambda b, j, pt: (b, 0, 0)),
        scratch_shapes=[pltpu.VMEM((HEADS, DH), f32), pltpu.VMEM((HEADS, DH), f32)],
    )
    return pl.pallas_call(
        functools.partial(_sb_step_kernel, npg=npg),
        grid_spec=grid_spec,
        out_shape=jax.ShapeDtypeStruct((n, HEADS, DH), f32),
        compiler_params=_cparams("arbitrary", "arbitrary"),
        name="sb_step",
    )(page_table, q4, bias4, eye3, *([cache_k] * npg), *([cache_v] * npg)).reshape(n, ATT_W)


def _mlstm_kernel(q_ref, k_ref, v_ref, og_ref, gate_ref, gb_ref, gn_ref,
                  h_ref, c_out, n_out, m_out, c_s, n_s, m_s):
    ci = pl.program_id(1)
    L = CHUNK

    @pl.when(ci == 0)
    def _():
        c_s[...] = jnp.zeros_like(c_s)
        n_s[...] = jnp.zeros_like(n_s)
        m_s[...] = jnp.zeros_like(m_s)

    gt = gate_ref[...] + gb_ref[...]
    gt_t = gt.T
    ri = _iota((L, L), 0)
    li = _iota((L, L), 1)
    causal = li <= ri
    tri_incl = jnp.where(causal, 1.0, 0.0).astype(bf16)
    tri_incl_t = jnp.where(ri <= li, 1.0, 0.0).astype(bf16)
    heads = range(M_HEADS)
    sls = [slice(h * M_DH, (h + 1) * M_DH) for h in heads]
    ig_col = [gt[:, h:h + 1] for h in heads]
    ig_row = [gt_t[h:h + 1, :] for h in heads]
    bcum_col, bcum_row = [], []
    for h in heads:
        lf_col = _log_sigmoid_pair(gt[:, M_HEADS + h:M_HEADS + h + 1])[0]
        lf_row = _log_sigmoid_pair(gt_t[M_HEADS + h:M_HEADS + h + 1, :])[0]
        c_hi, c_mid, c_lo = _split3(jnp.broadcast_to(lf_col, (L, L)))
        bcum_col.append(_dot(tri_incl, c_hi) + _dot(tri_incl, c_mid) + _dot(tri_incl, c_lo))
        r_hi, r_mid, r_lo = _split3(jnp.broadcast_to(lf_row, (L, L)))
        bcum_row.append(_dot(r_hi, tri_incl_t) + _dot(r_mid, tri_incl_t) + _dot(r_lo, tri_incl_t))
    qb = [q_ref[:, sl].astype(bf16) for sl in sls]
    kb = [(k_ref[:, sl] * (M_DH ** -0.5)).astype(bf16) for sl in sls]
    vb = [v_ref[:, sl].astype(bf16) for sl in sls]
    qk = [_dot_nt(qb[h], kb[h]) for h in heads]
    qc = [_dot_nt(qb[h], c_s[h].astype(bf16)) for h in heads]
    m_prev = [m_s[h:h + 1, :] for h in heads]
    m_t, sw, iw_col = [], [], []
    for h in heads:
        dlog = jnp.where(causal, bcum_col[h] - bcum_row[h] + ig_row[h], NEG_INF)
        inter = bcum_col[h] + m_prev[h]
        mt = jnp.maximum(inter, jnp.max(dlog, axis=1, keepdims=True))
        m_t.append(mt)
        sw.append(jnp.exp(dlog - mt) * qk[h])
        iw_col.append(jnp.exp(inter - mt)[:, 0:1])
    sv = [_dot(sw[h].astype(bf16), vb[h]) for h in heads]
    for h in heads:
        n_prev = n_s[h:h + 1, :]
        num = sv[h] + iw_col[h] * qc[h]
        qn = jnp.sum(qb[h].astype(f32) * n_prev.astype(bf16).astype(f32), axis=1, keepdims=True)
        den = jnp.sum(sw[h], axis=1, keepdims=True) + iw_col[h] * qn
        hout = num / jnp.maximum(jnp.abs(den), jnp.exp(-m_t[h][:, 0:1]))
        hn = hout * lax.rsqrt(jnp.mean(hout * hout, axis=1, keepdims=True) + EPS)
        h_ref[:, sls[h]] = hn * gn_ref[:, sls[h]] * _sigmoid(og_ref[:, sls[h]])
    for h in heads:
        m_last = m_t[h][L - 1:L, :]
        b_last = bcum_col[h][L - 1:L, :]
        wl_col = jnp.exp(b_last[:, 0:1] - bcum_col[h][:, 0:1] + ig_col[h] - m_last[:, 0:1])
        wl_row = jnp.exp(b_last - bcum_row[h][0:1, :] + ig_row[h] - m_last)
        dsc = jnp.exp(b_last + m_prev[h] - m_last)[:, 0:1]
        c_s[h] = dsc * c_s[h] + _dot((v_ref[:, sls[h]] * wl_col).T.astype(bf16), kb[h])
        wl8 = jnp.broadcast_to(wl_row, (8, L)).astype(bf16)
        n_s[h:h + 1, :] = dsc * n_s[h:h + 1, :] + _dot(wl8, kb[h])[0:1, :]
        m_s[h:h + 1, :] = m_last

    @pl.when(ci == pl.num_programs(1) - 1)
    def _():
        c_out[0] = c_s[...]
        n_out[0] = n_s[...]
        m_out[0] = m_s[...]


def mlstm_prompt(proj2, gates, gate_bias, g_mnorm, bsz, seq):
    nc = seq // CHUNK
    col = lambda c: pl.BlockSpec((CHUNK, M_WIDTH), lambda b, i, c=c: (b * nc + i, c))
    hm, c1, n1, m1 = pl.pallas_call(
        _mlstm_kernel,
        grid=(bsz, nc),
        in_specs=[
            col(3), col(4), col(5), col(6),
            pl.BlockSpec((CHUNK, LANES), lambda b, i: (b * nc + i, 0)),
            pl.BlockSpec((1, LANES), lambda b, i: (0, 0)),
            pl.BlockSpec((1, M_WIDTH), lambda b, i: (0, 0)),
        ],
        out_specs=[
            pl.BlockSpec((CHUNK, M_WIDTH), lambda b, i: (b * nc + i, 0)),
            pl.BlockSpec((1, M_HEADS, M_DH, M_DH), lambda b, i: (b, 0, 0, 0)),
            pl.BlockSpec((1, M_HEADS, M_DH), lambda b, i: (b, 0, 0)),
            pl.BlockSpec((1, M_HEADS, LANES), lambda b, i: (b, 0, 0)),
        ],
        out_shape=[
            jax.ShapeDtypeStruct((bsz * seq, M_WIDTH), f32),
            jax.ShapeDtypeStruct((bsz, M_HEADS, M_DH, M_DH), f32),
            jax.ShapeDtypeStruct((bsz, M_HEADS, M_DH), f32),
            jax.ShapeDtypeStruct((bsz, M_HEADS, LANES), f32),
        ],
        scratch_shapes=[
            pltpu.VMEM((M_HEADS, M_DH, M_DH), f32),
            pltpu.VMEM((M_HEADS, M_DH), f32),
            pltpu.VMEM((M_HEADS, LANES), f32),
        ],
        compiler_params=_cparams("arbitrary", "arbitrary"),
        name="mlstm_prompt",
    )(proj2, proj2, proj2, proj2, gates, gate_bias, g_mnorm)
    return hm, c1, n1, m1[:, :, 0]


def _mlstm_step_kernel(q_ref, k_ref, v_ref, og_ref, gn_ref, ig_ref, fg_ref, c_ref, n_ref, m_ref,
                       h_ref, c_out, n_out, m_out):
    q = q_ref[0, 0]
    k = k_ref[0, 0] * (M_DH ** -0.5)
    v = v_ref[0, 0]
    ig = ig_ref[0, 0]
    lf = _log_sigmoid_pair(fg_ref[0, 0])[0]
    m0 = m_ref[0, 0]
    c0 = c_ref[0, 0]
    n0 = n_ref[0, 0]
    inter = lf + m0
    m_t = jnp.maximum(inter, ig)
    dw = jnp.exp(ig - m_t)
    iw = jnp.exp(inter - m_t)
    rnd = lambda a: a.astype(bf16).astype(f32)
    qr, kr, vr = rnd(q), rnd(k), rnd(v)
    qk = jnp.sum(qr * kr, axis=1, keepdims=True)
    sw = dw * qk
    cq = jnp.sum(rnd(c0) * qr, axis=1, keepdims=True)
    num = rnd(sw) * vr + iw * cq
    den = sw + iw * jnp.sum(rnd(n0) * qr, axis=1, keepdims=True)
    hout = num / jnp.maximum(jnp.abs(den), jnp.exp(-m_t))
    c_out[0, 0] = iw * c0 + rnd(dw * v) * kr
    n_out[0, 0] = iw * n0 + rnd(dw) * kr
    m_out[0, 0] = m_t
    hn = hout * lax.rsqrt(jnp.mean(hout * hout, axis=0, keepdims=True) + EPS)
    h_ref[0, 0] = hn * gn_ref[0] * _sigmoid(og_ref[0, 0])


def mlstm_step(proj2_s, gates_s, gate_bias, g_mnorm, c0, n0, m0):
    n = proj2_s.shape[0]
    base = 3 * ATT_W
    seg = lambda i: proj2_s[:, base + i * M_WIDTH: base + (i + 1) * M_WIDTH].reshape(n, M_HEADS, M_DH)
    g = gates_s + gate_bias
    rowb = pl.BlockSpec((1, 1, 1, M_DH), lambda b, h: (b, h, 0, 0))
    colb = pl.BlockSpec((1, 1, M_DH, 1), lambda b, h: (b, h, 0, 0))
    scal = pl.BlockSpec((1, 1, 1, 1), lambda b, h: (b, h, 0, 0))
    hcol, c1, n1, m1 = pl.pallas_call(
        _mlstm_step_kernel,
        grid=(n, M_HEADS),
        in_specs=[
            rowb, rowb, colb, colb,
            pl.BlockSpec((1, M_DH, 1), lambda b, h: (h, 0, 0)),
            scal, scal,
            pl.BlockSpec((1, 1, M_DH, M_DH), lambda b, h: (b, h, 0, 0)),
            rowb, scal,
        ],
        out_specs=[colb, pl.BlockSpec((1, 1, M_DH, M_DH), lambda b, h: (b, h, 0, 0)), rowb, scal],
        out_shape=[
            jax.ShapeDtypeStruct((n, M_HEADS, M_DH, 1), f32),
            jax.ShapeDtypeStruct((n, M_HEADS, M_DH, M_DH), f32),
            jax.ShapeDtypeStruct((n, M_HEADS, 1, M_DH), f32),
            jax.ShapeDtypeStruct((n, M_HEADS, 1, 1), f32),
        ],
        compiler_params=_cparams("arbitrary", "arbitrary"),
        name="mlstm_step",
    )(seg(0)[:, :, None, :], seg(1)[:, :, None, :], seg(2)[..., None], seg(3)[..., None],
      g_mnorm.reshape(M_HEADS, M_DH, 1),
      g[:, 0:M_HEADS].reshape(n, M_HEADS, 1, 1), g[:, M_HEADS:2 * M_HEADS].reshape(n, M_HEADS, 1, 1),
      c0, n0[:, :, None, :], m0.reshape(n, M_HEADS, 1, 1))
    return hcol.reshape(n, M_WIDTH), c1, n1.reshape(n, M_HEADS, M_DH), m1.reshape(n, M_HEADS)


def _peerq_kernel(x_ref, g_ref, sh_ref, sc_ref, wt_ref, qt_ref, xmt_ref, xs_ref):
    @pl.when(pl.program_id(1) == 0)
    def _():
        xm = _modulated(x_ref[...], g_ref[...], sh_ref[0], sc_ref[0])
        xt = xm.T.astype(bf16)
        xs_ref[...] = xt
        xmt_ref[...] = xt

    qt_ref[...] = _dot(wt_ref[...], xs_ref[...])


def peer_query(x, g, shift, scale, wq_t, rows_per_batch, tm, tn=512):
    m, k = x.shape
    n = wq_t.shape[0]
    sh, sh_spec = _mod_specs(shift, m, tm, k, rows_per_batch)
    sc, sc_spec = _mod_specs(scale, m, tm, k, rows_per_batch)
    return pl.pallas_call(
        _peerq_kernel,
        grid=(m // tm, n // tn),
        in_specs=[
            pl.BlockSpec((tm, k), lambda i, j: (i, 0)),
            pl.BlockSpec((1, k), lambda i, j: (0, 0)),
            sh_spec, sc_spec,
            pl.BlockSpec((tn, k), lambda i, j: (j, 0)),
        ],
        out_specs=[pl.BlockSpec((tn, tm), lambda i, j: (j, i)),
                   pl.BlockSpec((k, tm), lambda i, j: (0, i))],
        out_shape=[jax.ShapeDtypeStruct((n, m), f32), jax.ShapeDtypeStruct((k, m), bf16)],
        scratch_shapes=[pltpu.VMEM((k, tm), bf16)],
        compiler_params=_cparams("arbitrary", "arbitrary"),
        name="peer_query",
    )(x, g.reshape(1, k), sh, sc, wq_t)


def _cast_t_kernel(w_ref, o_ref):
    o_ref[...] = w_ref[0].T.astype(bf16)


def cast_layer_bf16_t(w, layer, tr=512):
    _, rows, cols = w.shape
    return pl.pallas_call(
        _cast_t_kernel,
        grid=(rows // tr,),
        in_specs=[pl.BlockSpec((1, tr, cols), lambda i: (layer, i, 0))],
        out_specs=pl.BlockSpec((cols, tr), lambda i: (0, i)),
        out_shape=jax.ShapeDtypeStruct((cols, rows), bf16),
        compiler_params=_cparams("arbitrary"),
        name="cast_bf16_t",
    )(w)


def _heads_layout_kernel(*refs):
    n = len(refs) // 2
    for x_ref, o_ref in zip(refs[:n], refs[n:]):
        for hb in range(HEADS):
            o_ref[:, hb, :] = x_ref[:, hb * DH:(hb + 1) * DH]


def heads_layout(srcs, bsz, seq, tail, tt=512):
    nt, first = tail // tt, (seq - tail) // tt
    rows = lambda b, i: b * (seq // tt) + first + i
    return pl.pallas_call(
        _heads_layout_kernel,
        grid=(bsz, nt),
        in_specs=[pl.BlockSpec((tt, ATT_W), lambda b, i, c=c: (rows(b, i), c)) for _, c in srcs],
        out_specs=[pl.BlockSpec((tt, HEADS, DH), lambda b, i: (b * nt + i, 0, 0))] * len(srcs),
        out_shape=[jax.ShapeDtypeStruct((bsz * tail, HEADS, DH), f32)] * len(srcs),
        compiler_params=_cparams("arbitrary", "arbitrary"),
        name="heads_layout",
    )(*[x for x, _ in srcs])


def _cast_kernel(w_ref, o_ref):
    o_ref[...] = w_ref[0].astype(bf16)


def cast_layer_bf16(w, layer, tr=1024):
    _, rows, cols = w.shape
    return pl.pallas_call(
        _cast_kernel,
        grid=(rows // tr,),
        in_specs=[pl.BlockSpec((1, tr, cols), lambda i: (layer, i, 0))],
        out_specs=pl.BlockSpec((tr, cols), lambda i: (i, 0)),
        out_shape=jax.ShapeDtypeStruct((rows, cols), bf16),
        compiler_params=_cparams("arbitrary"),
        name="cast_bf16",
    )(w)


_N_TOP = PEER_TOPK + 1
_CAND_PAIRS = [(a, b) for a in range(_N_TOP) for b in range(_N_TOP) if (a + 1) * (b + 1) <= _N_TOP]
_N_CAND = -(-len(_CAND_PAIRS) // 8) * 8


def _extract_top(cur, ridx, n):
    vals = []
    big = float(cur.shape[0])
    for _ in range(n):
        mx = jnp.max(cur, axis=0, keepdims=True)
        first = jnp.min(jnp.where(cur == mx, ridx, big), axis=0, keepdims=True)
        cur = jnp.where(ridx == first, NEG_INF, cur)
        vals.append(mx)
    return vals


def _oddeven_mergesort_pairs(n):
    pairs = []

    def merge(lo, cnt, r):
        step = 2 * r
        if step < cnt:
            merge(lo, cnt, step)
            merge(lo + r, cnt, step)
            pairs.extend((i, i + r) for i in range(lo + r, lo + cnt - r, step))
        else:
            pairs.append((lo, lo + r))

    def sort(lo, cnt):
        if cnt > 1:
            sort(lo, cnt // 2)
            sort(lo + cnt // 2, cnt // 2)
            merge(lo, cnt, 1)

    sort(0, n)
    return pairs


_SORT16 = _oddeven_mergesort_pairs(PEER_TOPK)


def _exchange(y, i, j):
    y[i], y[j] = jnp.maximum(y[i], y[j]), jnp.minimum(y[i], y[j])


def _top17_network(s):
    n = PEER_TOPK
    y = [s[v * 8:(v + 1) * 8, :] for v in range(n)]
    for i, j in _SORT16:
        _exchange(y, i, j)
    dropped = None
    for shift in (4, 2, 1):
        other = [pltpu.roll(y[n - 1 - i], shift, 0) for i in range(n)]
        low = functools.reduce(jnp.maximum, [jnp.minimum(y[i], other[i]) for i in range(n)])
        dropped = low if dropped is None else jnp.maximum(dropped, low)
        y = [jnp.maximum(y[i], other[i]) for i in range(n)]
        dist = n // 2
        while dist:
            for i in range(n):
                if not i & dist:
                    _exchange(y, i, i + dist)
            dist //= 2
    return [v[0:1, :] for v in y] + [jnp.max(dropped, axis=0, keepdims=True)]


_ROUTER_UNROLL = 8


def _router_kernel(qt_ref, keys_ref, e1_ref, e2_ref, th_ref, cand_ref):
    cidx = _iota((_N_CAND, LANES), 0).astype(f32)
    k0 = keys_ref[0].astype(bf16)
    k1 = keys_ref[1].astype(bf16)
    cand_ref[...] = jnp.full(cand_ref.shape, NEG_INF, f32)

    def one_head(h, slot):
        r0 = pl.multiple_of(h * 2 * N_KEYS, 2 * N_KEYS)
        s1 = _dot(k0, qt_ref[pl.ds(r0, N_KEYS), :].astype(bf16))
        s2 = _dot(k1, qt_ref[pl.ds(r0 + N_KEYS, N_KEYS), :].astype(bf16))
        top_a = _top17_network(s1)
        top_b = _top17_network(s2)
        for ci, (a, b) in enumerate(_CAND_PAIRS):
            cand_ref[slot, ci:ci + 1, :] = top_a[a] + top_b[b]
        cs = _extract_top(cand_ref[slot], cidx, _N_TOP)
        z = jnp.ones_like(cs[0])
        for r in range(1, PEER_TOPK):
            z = z + jnp.exp(cs[r] - cs[0])
        inv_z = 1.0 / z
        mid = 0.5 * (cs[PEER_TOPK - 1] + cs[PEER_TOPK])
        o0 = pl.multiple_of(h * N_KEYS, N_KEYS)
        e1 = jnp.exp(s1 - top_a[0]) * inv_z
        e1_ref[:, h] = e1.reshape(N_KEYS // SUBLANES, SUBLANES, LANES)
        e2_ref[pl.ds(o0, N_KEYS), :] = jnp.exp(s2 - top_b[0])
        th_ref[pl.ds(h, 1), :] = jnp.exp(mid - cs[0]) * inv_z

    def body(hp, _):
        for slot in range(_ROUTER_UNROLL):
            one_head(hp * _ROUTER_UNROLL + slot, slot)
        return 0

    lax.fori_loop(0, PEER_HEADS // _ROUTER_UNROLL, body, 0)


def peer_router(qt, keys):
    n, m = qt.shape
    half = PEER_HEADS * N_KEYS
    return pl.pallas_call(
        _router_kernel,
        grid=(m // LANES,),
        in_specs=[pl.BlockSpec((n, LANES), lambda i: (0, i)),
                  pl.BlockSpec((2, N_KEYS, N_KEYS), lambda i: (0, 0, 0))],
        out_specs=[pl.BlockSpec((N_KEYS // SUBLANES, PEER_HEADS, SUBLANES, LANES), lambda i: (0, 0, 0, i)),
                   pl.BlockSpec((half, LANES), lambda i: (0, i)),
                   pl.BlockSpec((PEER_HEADS, LANES), lambda i: (0, i))],
        out_shape=[jax.ShapeDtypeStruct((N_KEYS // SUBLANES, PEER_HEADS, SUBLANES, m), f32),
                   jax.ShapeDtypeStruct((half, m), f32),
                   jax.ShapeDtypeStruct((PEER_HEADS, m), f32)],
        scratch_shapes=[pltpu.VMEM((_ROUTER_UNROLL, _N_CAND, LANES), f32)],
        compiler_params=_cparams("arbitrary"),
        name="peer_router",
    )(qt, keys)


_G_CHAINS = 2
_G_ROWS = SUBLANES
_G_EARLY = 4


class _Chains:
    def __init__(self, n, zero_ref):
        self.deps, self.cnt, self.zero_ref = [None] * n, 0, zero_ref

    def take(self):
        return self.deps[self.cnt % len(self.deps)]

    def put(self, g):
        bits = lax.bitcast_convert_type(g[0:1, :], jnp.int32) & self.zero_ref[0:1, :]
        self.deps[self.cnt % len(self.deps)] = lax.bitcast_convert_type(bits, f32)
        self.cnt += 1


def _routing_block(e1_ref, e2_ref, th_ref, ii, ls, chains):
    sub = _G_ROWS
    blk, r = divmod(ii, SUBLANES)
    rows = [e1_ref[blk, h, r:r + 1, ls] for h in range(PEER_HEADS)]
    ths = [th_ref[h:h + 1, ls] for h in range(PEER_HEADS)]
    parts = []
    for gi in range(N_KEYS // sub):
        dep = chains.take()
        g = jnp.zeros((sub, LANES), f32)
        for h in range(PEER_HEADS):
            row = rows[h] if (dep is None or h) else rows[h] + dep
            pr = e2_ref[h * N_KEYS + gi * sub:h * N_KEYS + (gi + 1) * sub, ls] * row
            g = g + jnp.where(pr >= ths[h], pr, 0.0)
        chains.put(g)
        parts.append(g)
    return jnp.concatenate(parts, axis=0)


def _experts_kernel(xmt_ref, e1_ref, e1n_ref, e2_ref, th_ref, u_ref, v_ref, x_ref, gt_ref, gf_ref, zero_ref,
                    o_ref, w_ref, g_ref, *, final_norm):
    c = pl.program_id(1)
    tm, te = w_ref.shape

    ni = te // N_KEYS
    lane_tiles = [slice(lc * LANES, (lc + 1) * LANES) for lc in range(tm // LANES)]
    early = range(ni - _G_EARLY, ni)

    def fill_early(e1, chains):
        for ii in early:
            for ls in lane_tiles:
                g_ref[ii - early[0], :, ls] = _routing_block(e1, e2_ref, th_ref, ii, ls, chains)

    @pl.when(c == 0)
    def _():
        o_ref[...] = jnp.zeros_like(o_ref)
        fill_early(e1_ref, _Chains(_G_CHAINS, zero_ref))

    chains = _Chains(_G_CHAINS, zero_ref)
    act = _gelu(_dot(u_ref[...], xmt_ref[...]))
    for ii in range(ni):
        es = slice(ii * N_KEYS, (ii + 1) * N_KEYS)
        for ls in lane_tiles:
            if ii in early:
                g_blk = g_ref[ii - early[0], :, ls]
            else:
                g_blk = _routing_block(e1_ref, e2_ref, th_ref, ii, ls, chains)
            wgt = g_blk * act[es, ls]
            w_ref[ls, es] = wgt.T.astype(bf16)
    o_ref[...] += _dot(w_ref[...], v_ref[...])
    chains = _Chains(_G_CHAINS, zero_ref)
    for _ in range(_G_CHAINS):
        chains.put(wgt)
    fill_early(e1n_ref, chains)

    @pl.when(c == pl.num_programs(1) - 1)
    def _():
        y = x_ref[...] + gt_ref[0] * o_ref[...]
        if final_norm:
            ms = jnp.mean(y * y, axis=-1, keepdims=True)
            y = y * lax.rsqrt(ms + EPS) * gf_ref[...]
        o_ref[...] = y


def peer_experts(xmt, e1t, e2t, th, u, v, x, gate, g_final, rows_per_batch, tm, te, final_norm):
    m, d = x.shape
    gt, gt_spec = _mod_specs(gate, m, tm, d, rows_per_batch)
    half = PEER_HEADS * N_KEYS
    nc = N_EXPERTS // te
    e1_blk = (te // N_KEYS // SUBLANES, PEER_HEADS, SUBLANES, tm)
    return pl.pallas_call(
        functools.partial(_experts_kernel, final_norm=final_norm),
        grid=(m // tm, nc),
        in_specs=[
            pl.BlockSpec((d, tm), lambda i, c: (0, i)),
            pl.BlockSpec(e1_blk, lambda i, c: (c, 0, 0, i)),
            pl.BlockSpec(e1_blk, lambda i, c: (jnp.minimum(c + 1, nc - 1), 0, 0, i)),
            pl.BlockSpec((half, tm), lambda i, c: (0, i)),
            pl.BlockSpec((PEER_HEADS, tm), lambda i, c: (0, i)),
            pl.BlockSpec((te, d), lambda i, c: (c, 0)),
            pl.BlockSpec((te, d), lambda i, c: (c, 0)),
            pl.BlockSpec((tm, d), lambda i, c: (i, 0)),
            gt_spec,
            pl.BlockSpec((1, d), lambda i, c: (0, 0)),
            pl.BlockSpec((SUBLANES, LANES), lambda i, c: (0, 0)),
        ],
        out_specs=pl.BlockSpec((tm, d), lambda i, c: (i, 0)),
        out_shape=jax.ShapeDtypeStruct((m, d), f32),
        scratch_shapes=[pltpu.VMEM((tm, te), bf16), pltpu.VMEM((_G_EARLY, N_KEYS, tm), f32)],
        compiler_params=_cparams("arbitrary", "arbitrary"),
        name="peer_experts",
    )(xmt, e1t, e1t, e2t, th, u, v, x, gt, g_final.reshape(1, d), jnp.zeros((SUBLANES, LANES), jnp.int32))


def peer_block(x, g, shift, scale, gate, wq_t, keys, u, v, g_final, rows_per_batch, tm, te, final_norm):
    tm_q = 1024 if x.shape[0] % 1024 == 0 and rows_per_batch % 1024 == 0 else tm
    qt, xmt = peer_query(x, g, shift, scale, wq_t, rows_per_batch, tm_q)
    e1t, e2t, th = peer_router(qt, keys)
    return peer_experts(xmt, e1t, e2t, th, u, v, x, gate, g_final, rows_per_batch, tm, te, final_norm)


def kernel(x_prompt, x_sample, c_prompt, c_sample, state_rglru_conv, state_rglru_h, cache_swa_k, cache_swa_v, cache_sb_k, cache_sb_v, state_mlstm_C, state_mlstm_n, state_mlstm_m, page_table, w_ada, b_ada, g_norm_mix, g_norm_ffn, e_w_in, e_conv_w, e_conv_b, e_w_r, e_b_r, e_w_i, e_b_i, e_lambda, e_w_out, o_w_in, o_b_if, o_sb_bias, o_g_mnorm, o_w_out, peer_w_q, peer_keys, peer_u, peer_v, g_final):
    bp, seq, d = x_prompt.shape
    bs = x_sample.shape[0]
    mp = bp * seq
    pad_s = LANES
    xp = x_prompt.reshape(mp, d)
    xs = x_sample.reshape(bs, d)

    c_rows = 16
    c_all = jnp.concatenate([c_prompt, c_sample, jnp.zeros((c_rows - bp - bs, d), f32)], axis=0)
    mod = adaln_all(c_all, w_ada, b_ada)

    def mods(layer):
        parts = [mod[layer, :, i * d:(i + 1) * d] for i in range(6)]
        return [p[:bp] for p in parts], [p[bp:bp + bs] for p in parts]

    ctab_p, stab_p = _rope_tables(jnp.arange(seq, dtype=jnp.int32))
    ctab_s, stab_s = _rope_tables(jnp.full((1,), PAST_LEN, jnp.int32))

    TM = 512
    m_p, m_s = mods(0)
    w_in = e_w_in[0].astype(bf16)
    w_out = e_w_out[0].astype(bf16)
    cw, cb = e_conv_w[0], e_conv_b[0].reshape(1, -1)
    wr, wi = e_w_r[0].astype(bf16), e_w_i[0].astype(bf16)
    br, bi, lam = e_b_r[0].reshape(1, -1), e_b_i[0].reshape(1, -1), e_lambda[0].reshape(1, -1)

    proj_p = mod_matmul(xp, g_norm_mix[0], m_p[0], m_p[1], w_in, E_IN, seq, 1024, 1024, name="e_in_p")
    proj_s = mod_matmul(xs, g_norm_mix[0], m_s[0], m_s[1], w_in, E_IN, 1, bs, 512, name="e_in_s")

    ya_p, h_p = rglru_prompt(proj_p, bp, seq, cw, cb, wr, br, wi, bi, lam)
    ya_s, h_s = rglru_step(proj_s, state_rglru_conv[0], state_rglru_h[0], cw, cb, wr, br, wi, bi, lam)
    conv_p = proj_p.reshape(bp, seq, E_IN)[:, seq - (CONV_W - 1):, :RG_WIDTH]
    conv_s = jnp.concatenate([state_rglru_conv[0][:, 1:], proj_s[:, None, :RG_WIDTH]], axis=1)

    qkv_p, k_p = rope_split(proj_p, ctab_p, stab_p, bp, seq)
    q_s, k_s, v_s = rope_qkv(proj_s, ctab_s, stab_s, 1, bs)
    o_p = dilated_prompt(qkv_p, bp, seq)
    win = cache_swa_k.shape[2]
    o_s = dilated_step(q_s, k_s, v_s, cache_swa_k.reshape(-1, win, HEADS, DH)[:bs],
                       cache_swa_v.reshape(-1, win, HEADS, DH)[:bs])
    wl = min(2048, seq)
    swa_k_p, swa_v_p = [a.reshape(bp, wl, HEADS, DH)
                        for a in heads_layout([(k_p, 0), (proj_p, E_IN // ATT_W - 1)], bp, seq, wl)]
    swa_k_s = k_s.reshape(bs, 1, HEADS, DH)
    swa_v_s = v_s.reshape(bs, 1, HEADS, DH)

    xp = out_proj(ya_p, o_p, w_out, xp, m_p[2], seq, 1024, 1024, name="e_out_p")
    xs = out_proj(ya_s, o_s, w_out, xs, m_s[2], 1, bs, 512, name="e_out_s")

    def peer_layer(layer, xp, xs, m_p, m_s, final_norm):
        wq_t = cast_layer_bf16_t(peer_w_q, layer)
        u = cast_layer_bf16(peer_u, layer)
        v = cast_layer_bf16(peer_v, layer)
        xp = peer_block(xp, g_norm_ffn[layer], m_p[3], m_p[4], m_p[5], wq_t, peer_keys[layer], u, v,
                        g_final, seq, TM, 1024, final_norm)
        xs_pad = jnp.pad(xs, ((0, pad_s - bs), (0, 0)))
        xs_new = peer_block(xs_pad, g_norm_ffn[layer], m_s[3], m_s[4], m_s[5], wq_t, peer_keys[layer],
                            u, v, g_final, 1, pad_s, 1024, final_norm)
        return xp, xs_new[:bs]

    xp, xs = peer_layer(0, xp, xs, m_p, m_s, False)

    m_p, m_s = mods(1)
    w_in2 = o_w_in[0].astype(bf16)
    w_gate = jnp.pad(w_in2[:, O_MAIN:], ((0, 0), (0, LANES - 2 * M_HEADS)))
    w_out2 = o_w_out[0].astype(bf16)
    gate_bias = jnp.pad(o_b_if[0].reshape(1, 2 * M_HEADS), ((0, 0), (0, LANES - 2 * M_HEADS)))
    gmn = o_g_mnorm[0].reshape(1, M_WIDTH)

    proj2_p, gates_p = mod_matmul(xp, g_norm_mix[1], m_p[0], m_p[1], w_in2, O_MAIN, seq, 1024, 1024,
                                  w_gate=w_gate, name="o_in_p")
    proj2_s, gates_s = mod_matmul(xs, g_norm_mix[1], m_s[0], m_s[1], w_in2, O_MAIN, 1, bs, 512,
                                  w_gate=w_gate, name="o_in_s")

    oc_p = sb_prompt(proj2_p, o_sb_bias[0], bp, seq)
    oc_s = sb_step(proj2_s[:, :ATT_W], o_sb_bias[0], cache_sb_k.reshape(-1, PAGE, HEADS, DH),
                   cache_sb_v.reshape(-1, PAGE, HEADS, DH), page_table)
    hm_p, mC_p, mn_p, mm_p = mlstm_prompt(proj2_p, gates_p, gate_bias, gmn, bp, seq)
    hm_s, mC_s, mn_s, mm_s = mlstm_step(proj2_s, gates_s, gate_bias, gmn,
                                        state_mlstm_C.reshape(-1, M_HEADS, M_DH, M_DH)[:bs],
                                        state_mlstm_n[0], state_mlstm_m[0])

    n_pg = seq // PAGE
    sb_k_p, sb_v_p = [a.reshape(bp, n_pg, PAGE, HEADS, DH)
                      for a in heads_layout([(proj2_p, 1), (proj2_p, 2)], bp, seq, seq)]
    sb_k_s = proj2_s[:, ATT_W:2 * ATT_W].reshape(bs, 1, HEADS, DH)
    sb_v_s = proj2_s[:, 2 * ATT_W:3 * ATT_W].reshape(bs, 1, HEADS, DH)

    xp = out_proj(oc_p, hm_p, w_out2, xp, m_p[2], seq, 1024, 1024, name="o_out_p")
    xs = out_proj(oc_s, hm_s, w_out2, xs, m_s[2], 1, bs, 512, name="o_out_s")
    xp, xs = peer_layer(1, xp, xs, m_p, m_s, True)

    y_prompt = xp.reshape(bp, seq, d)
    y_sample = xs.reshape(bs, 1, d)
    st = lambda a: a[None]
    return (y_prompt, y_sample, st(conv_p), st(conv_s), st(h_p.reshape(bp, RG_WIDTH)), st(h_s),
            st(swa_k_p), st(swa_k_s), st(swa_v_p), st(swa_v_s),
            st(sb_k_p), st(sb_k_s), st(sb_v_p), st(sb_v_s),
            st(mC_p), st(mC_s), st(mn_p), st(mn_s), st(mm_p), st(mm_s))
```

```python
import functools
import math

import jax
import jax.numpy as jnp
from jax import lax
from jax.experimental import pallas as pl
from jax.experimental.pallas import tpu as pltpu

f32 = jnp.float32
bf16 = jnp.bfloat16

D_MODEL = 2048
PAST_LEN = 16384
PAGE = 128
RG_WIDTH = 1024
RG_BLOCKS = 8
CONV_W = 4
RG_C = 8.0
HEADS = 8
DH = 128
ATT_W = HEADS * DH
DIL_PATTERNS = ((128, 1), (512, 4), (2048, 16))
ROT_DIMS = 32
ROPE_THETA = 500000.0
M_HEADS = 4
M_DH = 256
M_WIDTH = M_HEADS * M_DH
CHUNK = 128
E_IN = 2 * RG_WIDTH + 3 * ATT_W
O_MAIN = 3 * ATT_W + 4 * M_WIDTH
PEER_HEADS = 8
N_KEYS = 128
N_EXPERTS = N_KEYS * N_KEYS
PEER_TOPK = 16
EPS = 1e-6
LANES = 128
SUBLANES = 8
VMEM_LIMIT = 56 * 1024 * 1024
NEG_INF = float("-inf")


def _cparams(*sem):
    return pltpu.CompilerParams(dimension_semantics=sem, vmem_limit_bytes=VMEM_LIMIT)


def _dot(a, b):
    return jnp.dot(a, b, preferred_element_type=f32)


def _dot_nt(a, b):
    return lax.dot_general(a, b, (((1,), (1,)), ((), ())), preferred_element_type=f32)


def _split3(x):
    hi = x.astype(bf16)
    r = x - hi.astype(f32)
    mid = r.astype(bf16)
    lo = (r - mid.astype(f32)).astype(bf16)
    return hi, mid, lo


def _sigmoid(x):
    return 1.0 / (1.0 + jnp.exp(-x))


def _log_sigmoid_pair(z):
    l1p = jnp.log1p(jnp.exp(-jnp.abs(z)))
    return jnp.minimum(z, 0.0) - l1p, -jnp.maximum(z, 0.0) - l1p


def _softplus(z):
    return jnp.maximum(z, 0.0) + jnp.log(1.0 + jnp.exp(-jnp.abs(z)))


def _gelu(x):
    c = math.sqrt(2.0 / math.pi)
    h = 0.5 * x
    return h + h * jnp.tanh(x * (c + (c * 0.044715) * (x * x)))


def _iota(shape, dim):
    return lax.broadcasted_iota(jnp.int32, shape, dim)


def _rowsum_bcast(x):
    ones = jnp.ones((LANES, LANES), bf16)
    hi = x.astype(bf16)
    lo = (x - hi.astype(f32)).astype(bf16)
    return _dot(hi, ones) + _dot(lo, ones)


def _adaln_kernel(c_ref, w_ref, b_ref, o_ref):
    c = c_ref[...]
    s = c * _sigmoid(c)
    w = w_ref[0]
    s_hi = s.astype(bf16)
    s_lo = (s - s_hi.astype(f32)).astype(bf16)
    w_hi = w.astype(bf16)
    w_lo = (w - w_hi.astype(f32)).astype(bf16)
    o_ref[0] = _dot(s_hi, w_hi) + _dot(s_hi, w_lo) + _dot(s_lo, w_hi) + b_ref[0]


def adaln_all(c_all, w_ada, b_ada):
    depth, d, n = w_ada.shape
    rows = c_all.shape[0]
    tn = 1024
    return pl.pallas_call(
        _adaln_kernel,
        grid=(depth, n // tn),
        in_specs=[
            pl.BlockSpec((rows, d), lambda l, j: (0, 0)),
            pl.BlockSpec((1, d, tn), lambda l, j: (l, 0, j)),
            pl.BlockSpec((1, 1, tn), lambda l, j: (l, 0, j)),
        ],
        out_specs=pl.BlockSpec((1, rows, tn), lambda l, j: (l, 0, j)),
        out_shape=jax.ShapeDtypeStruct((depth, rows, n), f32),
        compiler_params=_cparams("arbitrary", "arbitrary"),
        name="adaln",
    )(c_all, w_ada, b_ada.reshape(depth, 1, n))


def _modulated(x, g, sh, sc):
    ms = jnp.mean(x * x, axis=-1, keepdims=True)
    y = x * lax.rsqrt(ms + EPS) * g
    return y * (1.0 + sc) + sh


def _modmm_kernel(x_ref, g_ref, sh_ref, sc_ref, w_ref, o_ref, xn_ref):
    @pl.when(pl.program_id(1) == 0)
    def _():
        xn_ref[...] = _modulated(x_ref[...], g_ref[...], sh_ref[0], sc_ref[0]).astype(bf16)

    o_ref[...] = _dot(xn_ref[...], w_ref[...])


def _modmm_gate_kernel(x_ref, g_ref, sh_ref, sc_ref, w_ref, wg_ref, o_ref, og_ref, xn_ref):
    @pl.when(pl.program_id(1) == 0)
    def _():
        xn = _modulated(x_ref[...], g_ref[...], sh_ref[0], sc_ref[0]).astype(bf16)
        xn_ref[...] = xn
        og_ref[...] = _dot(xn, wg_ref[...])

    o_ref[...] = _dot(xn_ref[...], w_ref[...])


def _mod_specs(mod, m, tm, k, rows_per_batch):
    if rows_per_batch >= tm:
        assert rows_per_batch % tm == 0
        per = rows_per_batch // tm
        return mod.reshape(-1, 1, k), pl.BlockSpec((1, 1, k), lambda i, j: (i // per, 0, 0))
    assert rows_per_batch == 1
    rows = mod
    if rows.shape[0] < m:
        rows = jnp.pad(rows, ((0, m - rows.shape[0]), (0, 0)))
    return rows.reshape(1, m, k), pl.BlockSpec((1, tm, k), lambda i, j: (0, i, 0))


def mod_matmul(x, g, shift, scale, w, n_out, rows_per_batch, tm, tn, w_gate=None, name="modmm"):
    m, k = x.shape
    sh, sh_spec = _mod_specs(shift, m, tm, k, rows_per_batch)
    sc, sc_spec = _mod_specs(scale, m, tm, k, rows_per_batch)
    in_specs = [
        pl.BlockSpec((tm, k), lambda i, j: (i, 0)),
        pl.BlockSpec((1, k), lambda i, j: (0, 0)),
        sh_spec,
        sc_spec,
        pl.BlockSpec((k, tn), lambda i, j: (0, j)),
    ]
    args = [x, g.reshape(1, k), sh, sc, w]
    out_specs = pl.BlockSpec((tm, tn), lambda i, j: (i, j))
    out_shape = jax.ShapeDtypeStruct((m, n_out), f32)
    kern = _modmm_kernel
    if w_gate is not None:
        in_specs.append(pl.BlockSpec((k, LANES), lambda i, j: (0, 0)))
        args.append(w_gate)
        out_specs = [out_specs, pl.BlockSpec((tm, LANES), lambda i, j: (i, 0))]
        out_shape = [out_shape, jax.ShapeDtypeStruct((m, LANES), f32)]
        kern = _modmm_gate_kernel
    return pl.pallas_call(
        kern,
        grid=(m // tm, n_out // tn),
        in_specs=in_specs,
        out_specs=out_specs,
        out_shape=out_shape,
        scratch_shapes=[pltpu.VMEM((tm, k), bf16)],
        compiler_params=_cparams("arbitrary", "arbitrary"),
        name=name,
    )(*args)


def _outproj_kernel(a1_ref, a2_ref, w1_ref, w2_ref, x_ref, gt_ref, o_ref):
    y = _dot(a1_ref[...].astype(bf16), w1_ref[...]) + _dot(a2_ref[...].astype(bf16), w2_ref[...])
    o_ref[...] = x_ref[...] + gt_ref[0] * y


def out_proj(a1, a2, w, x, gate, rows_per_batch, tm, tn, name="outproj"):
    m, k1 = a1.shape
    k2 = a2.shape[1]
    n = w.shape[1]
    gt, gt_spec = _mod_specs(gate, m, tm, n, rows_per_batch)
    if gt.shape[1] == 1:
        per = rows_per_batch // tm
        gt_spec = pl.BlockSpec((1, 1, tn), lambda i, j: (i // per, 0, j))
    else:
        gt_spec = pl.BlockSpec((1, tm, tn), lambda i, j: (0, i, j))
    return pl.pallas_call(
        _outproj_kernel,
        grid=(m // tm, n // tn),
        in_specs=[
            pl.BlockSpec((tm, k1), lambda i, j: (i, 0)),
            pl.BlockSpec((tm, k2), lambda i, j: (i, 0)),
            pl.BlockSpec((k1, tn), lambda i, j: (0, j)),
            pl.BlockSpec((k2, tn), lambda i, j: (k1 // k2, j)),
            pl.BlockSpec((tm, tn), lambda i, j: (i, j)),
            gt_spec,
        ],
        out_specs=pl.BlockSpec((tm, tn), lambda i, j: (i, j)),
        out_shape=jax.ShapeDtypeStruct((m, n), f32),
        compiler_params=_cparams("arbitrary", "arbitrary"),
        name=name,
    )(a1, a2, w, w, x, gt)


def _rglru_gates(xc, wr_ref, br, wi_ref, bi, lam):
    xb = xc.astype(bf16)
    rs, gs = [], []
    for hb in range(RG_BLOCKS):
        sl = slice(hb * LANES, (hb + 1) * LANES)
        rs.append(_dot(xb[:, sl], wr_ref[hb]))
        gs.append(_dot(xb[:, sl], wi_ref[hb]))
    r = _sigmoid(jnp.concatenate(rs, axis=1) + br)
    ig = _sigmoid(jnp.concatenate(gs, axis=1) + bi)
    softplus_neg_lam = jnp.maximum(-lam, 0.0) + jnp.log1p(jnp.exp(-jnp.abs(lam)))
    log_a = -RG_C * r * softplus_neg_lam
    a = jnp.exp(log_a)
    u = jnp.sqrt(-jnp.tanh(log_a) * (a * a + 1.0)) * ig * xc
    return a, u


def _rglru_kernel(xa_ref, ga_ref, cw_ref, cb_ref, wr_ref, br_ref, wi_ref, bi_ref, lam_ref,
                  ya_ref, hl_ref, xprev_ref, hc_ref):
    t_idx = pl.program_id(1)
    tt = xa_ref.shape[0]

    @pl.when(t_idx == 0)
    def _():
        xprev_ref[...] = jnp.zeros_like(xprev_ref)
        hc_ref[...] = jnp.zeros_like(hc_ref)

    xa = xa_ref[...]
    xprev = xprev_ref[...]
    row8 = _iota((8, RG_WIDTH), 0)
    xc = cb_ref[...] + cw_ref[CONV_W - 1:CONV_W, :] * xa
    for k in range(1, CONV_W):
        rolled = pltpu.roll(xa, k, 0)
        head = jnp.where(row8 < k, pltpu.roll(xprev, k, 0), rolled[0:8])
        shifted = jnp.concatenate([head, rolled[8:]], axis=0)
        xc = xc + cw_ref[CONV_W - 1 - k:CONV_W - k, :] * shifted
    xprev_ref[...] = xa[tt - 8:tt]

    a, u = _rglru_gates(xc, wr_ref, br_ref[...], wi_ref, bi_ref[...], lam_ref[...])
    row = _iota((tt, RG_WIDTH), 0)
    s = 1
    while s < tt:
        a_sh = pltpu.roll(a, s, 0)
        u_sh = pltpu.roll(u, s, 0)
        ok = row >= s
        u = jnp.where(ok, a * u_sh + u, u)
        a = jnp.where(ok, a * a_sh, a)
        s *= 2
    h = a * hc_ref[...] + u
    hc_ref[...] = h[tt - 1:tt]
    hl_ref[0] = h[tt - 1:tt]
    ya_ref[...] = h * _gelu(ga_ref[...])


def rglru_prompt(proj, bsz, seq, cw, cb, wr, br, wi, bi, lam, tt=256):
    nt = seq // tt
    vec = lambda: pl.BlockSpec((1, RG_WIDTH), lambda b, t: (0, 0))
    return pl.pallas_call(
        _rglru_kernel,
        grid=(bsz, nt),
        in_specs=[
            pl.BlockSpec((tt, RG_WIDTH), lambda b, t: (b * nt + t, 0)),
            pl.BlockSpec((tt, RG_WIDTH), lambda b, t: (b * nt + t, 1)),
            pl.BlockSpec((CONV_W, RG_WIDTH), lambda b, t: (0, 0)),
            vec(),
            pl.BlockSpec((RG_BLOCKS, LANES, LANES), lambda b, t: (0, 0, 0)),
            vec(),
            pl.BlockSpec((RG_BLOCKS, LANES, LANES), lambda b, t: (0, 0, 0)),
            vec(),
            vec(),
        ],
        out_specs=[
            pl.BlockSpec((tt, RG_WIDTH), lambda b, t: (b * nt + t, 0)),
            pl.BlockSpec((1, 1, RG_WIDTH), lambda b, t: (b, 0, 0)),
        ],
        out_shape=[
            jax.ShapeDtypeStruct((bsz * seq, RG_WIDTH), f32),
            jax.ShapeDtypeStruct((bsz, 1, RG_WIDTH), f32),
        ],
        scratch_shapes=[pltpu.VMEM((8, RG_WIDTH), f32), pltpu.VMEM((1, RG_WIDTH), f32)],
        compiler_params=_cparams("arbitrary", "arbitrary"),
        name="rglru_prompt",
    )(proj, proj, cw, cb, wr, br, wi, bi, lam)


def _rglru_step_kernel(xa_ref, ga_ref, b0_ref, b1_ref, b2_ref, h0_ref, cw_ref, cb_ref,
                       wr_ref, br_ref, wi_ref, bi_ref, lam_ref, ya_ref, h_ref):
    xa = xa_ref[...]
    xc = (cb_ref[...] + cw_ref[0:1, :] * b0_ref[...] + cw_ref[1:2, :] * b1_ref[...]
          + cw_ref[2:3, :] * b2_ref[...] + cw_ref[3:4, :] * xa)
    a, u = _rglru_gates(xc, wr_ref, br_ref[...], wi_ref, bi_ref[...], lam_ref[...])
    h = a * h0_ref[...] + u
    h_ref[...] = h
    ya_ref[...] = h * _gelu(ga_ref[...])


def rglru_step(proj_s, conv_state, h0, cw, cb, wr, br, wi, bi, lam):
    n = proj_s.shape[0]
    full = lambda shape: pl.BlockSpec(shape, lambda i: tuple(0 for _ in shape))
    return pl.pallas_call(
        _rglru_step_kernel,
        grid=(1,),
        in_specs=[
            pl.BlockSpec((n, RG_WIDTH), lambda i: (0, 0)),
            pl.BlockSpec((n, RG_WIDTH), lambda i: (0, 1)),
            full((n, RG_WIDTH)), full((n, RG_WIDTH)), full((n, RG_WIDTH)), full((n, RG_WIDTH)),
            full((CONV_W, RG_WIDTH)), full((1, RG_WIDTH)),
            full((RG_BLOCKS, LANES, LANES)), full((1, RG_WIDTH)),
            full((RG_BLOCKS, LANES, LANES)), full((1, RG_WIDTH)), full((1, RG_WIDTH)),
        ],
        out_specs=[full((n, RG_WIDTH)), full((n, RG_WIDTH))],
        out_shape=[jax.ShapeDtypeStruct((n, RG_WIDTH), f32)] * 2,
        compiler_params=_cparams("arbitrary"),
        name="rglru_step",
    )(proj_s, proj_s, conv_state[:, 0], conv_state[:, 1], conv_state[:, 2], h0,
      cw, cb, wr, br, wi, bi, lam)


def _rope_tables(pos):
    half = ROT_DIMS // 2
    inv = ROPE_THETA ** (-jnp.arange(half, dtype=f32) / half)
    ang = pos.astype(f32)[:, None] * inv[None, :]
    cos, sin = jnp.cos(ang), jnp.sin(ang)
    n = pos.shape[0]
    ctab = jnp.concatenate([cos, cos, jnp.ones((n, DH - ROT_DIMS), f32)], axis=1)
    stab = jnp.concatenate([-sin, sin, jnp.zeros((n, DH - ROT_DIMS), f32)], axis=1)
    return ctab, stab


def _rope_head(xh, ctab, stab, lane):
    half = ROT_DIMS // 2
    partner = jnp.where(lane < half, pltpu.roll(xh, DH - half, 1), pltpu.roll(xh, half, 1))
    return xh * ctab + partner * stab


def _rope_kernel(q_ref, k_ref, v_ref, c_ref, s_ref, qo_ref, ko_ref, vo_ref):
    ctab, stab = c_ref[...], s_ref[...]
    lane = _iota((q_ref.shape[0], DH), 1)
    for hb in range(HEADS):
        sl = slice(hb * DH, (hb + 1) * DH)
        qo_ref[:, hb, :] = _rope_head(q_ref[:, sl], ctab, stab, lane) * (DH ** -0.5)
        ko_ref[:, hb, :] = _rope_head(k_ref[:, sl], ctab, stab, lane)
        vo_ref[:, hb, :] = v_ref[:, sl]


def rope_qkv(proj, ctab, stab, rows_per_seq, tt):
    m = proj.shape[0]
    nt = max(rows_per_seq // tt, 1)
    if ctab.shape[0] == 1:
        tab_spec = pl.BlockSpec((1, DH), lambda i: (0, 0))
    else:
        tab_spec = pl.BlockSpec((tt, DH), lambda i: (i % nt, 0))
    return pl.pallas_call(
        _rope_kernel,
        grid=(m // tt,),
        in_specs=[
            pl.BlockSpec((tt, ATT_W), lambda i: (i, 2)),
            pl.BlockSpec((tt, ATT_W), lambda i: (i, 3)),
            pl.BlockSpec((tt, ATT_W), lambda i: (i, 4)),
            tab_spec, tab_spec,
        ],
        out_specs=[pl.BlockSpec((tt, HEADS, DH), lambda i: (i, 0, 0))] * 3,
        out_shape=[jax.ShapeDtypeStruct((m, HEADS, DH), f32)] * 3,
        compiler_params=_cparams("arbitrary"),
        name="rope",
    )(proj, proj, proj, ctab, stab)


def _rope_split_kernel(q_ref, k_ref, v_ref, c_ref, s_ref, *refs):
    n_pat = len(DIL_PATTERNS)
    outs, (kf_ref, qs_ref, ks_ref) = refs[:3 * n_pat], refs[3 * n_pat:]
    tt = q_ref.shape[0]
    ctab, stab = c_ref[...], s_ref[...]
    lane = _iota((tt, DH), 1)
    qs_ref[...] = _rope_head(q_ref[...], ctab, stab, lane) * (DH ** -0.5)
    k_rot = _rope_head(k_ref[...], ctab, stab, lane)
    ks_ref[...] = k_rot
    kf_ref[...] = k_rot
    for gi, (_, d) in enumerate(DIL_PATTERNS):
        rows = tt // d
        for src, dst in ((qs_ref, outs[3 * gi]), (ks_ref, outs[3 * gi + 1]), (v_ref, outs[3 * gi + 2])):
            for r in range(d):
                dst[0, 0, r] = src[pl.ds(r, rows, stride=d), :].astype(bf16)


def rope_split(proj, ctab, stab, bsz, seq, tt=1024):
    m = proj.shape[0]
    nt = seq // tt
    col = lambda c: pl.BlockSpec((tt, DH), lambda i, h, c=c: (i, c * HEADS + h))
    tab = pl.BlockSpec((tt, DH), lambda i, h: (i % nt, 0))
    out_specs, out_shape = [], []
    for (_, d) in DIL_PATTERNS:
        for _ in range(3):
            out_specs.append(pl.BlockSpec((1, 1, d, tt // d, DH), lambda i, h: (i // nt, h, 0, i % nt, 0)))
            out_shape.append(jax.ShapeDtypeStruct((bsz, HEADS, d, seq // d, DH), bf16))
    out_specs.append(pl.BlockSpec((tt, DH), lambda i, h: (i, h)))
    out_shape.append(jax.ShapeDtypeStruct((m, ATT_W), f32))
    res = pl.pallas_call(
        _rope_split_kernel,
        grid=(m // tt, HEADS),
        in_specs=[col(2), col(3), col(4), tab, tab],
        out_specs=out_specs,
        out_shape=out_shape,
        scratch_shapes=[pltpu.VMEM((tt, DH), f32), pltpu.VMEM((tt, DH), f32)],
        compiler_params=_cparams("arbitrary", "arbitrary"),
        name="rope_split",
    )(proj, proj, proj, ctab, stab)
    return [res[3 * gi:3 * gi + 3] for gi in range(len(DIL_PATTERNS))], res[-1]


def _dil_kernel(q_ref, kc_ref, kp_ref, vc_ref, vp_ref, o_ref, l_ref, s_ref, p_ref, *, span):
    tq = q_ref.shape[3]
    blk = pl.program_id(2)
    qi = _iota((tq, 2 * tq), 0)
    col = _iota((tq, 2 * tq), 1)
    rel = jnp.where(col < tq, qi - col, qi - col + 2 * tq)
    ok = (rel >= 0) & (rel <= span) & ((col < tq) | (blk > 0))
    for hb in range(HEADS):
        qh = q_ref[0, hb, 0]
        s_ref[hb, :, :tq] = _dot_nt(qh, kc_ref[0, hb, 0])
        s_ref[hb, :, tq:] = _dot_nt(qh, kp_ref[0, hb, 0])
    for hb in range(HEADS):
        s = jnp.where(ok, s_ref[hb], NEG_INF)
        mx = jnp.max(s, axis=1, keepdims=True)
        p = jnp.exp(s - mx)
        den = jnp.sum(p, axis=1, keepdims=True)
        p_ref[hb] = (p * (1.0 / den)).astype(bf16)
        l_ref[0, hb, 0] = jnp.broadcast_to(mx + jnp.log(den), (tq, DH))
    for hb in range(HEADS):
        o_ref[0, hb, 0] = (_dot(p_ref[hb, :, :tq], vc_ref[0, hb, 0])
                           + _dot(p_ref[hb, :, tq:], vp_ref[0, hb, 0]))


def _dil_merge_kernel(*refs):
    n_pat = len(DIL_PATTERNS)
    ins, o_ref, scr = refs[:2 * n_pat], refs[2 * n_pat], refs[2 * n_pat + 1:]
    tt = o_ref.shape[0]
    for gi, (_, d) in enumerate(DIL_PATTERNS):
        rows = tt // d
        for src, dst in ((ins[2 * gi], scr[2 * gi]), (ins[2 * gi + 1], scr[2 * gi + 1])):
            for r in range(d):
                dst[pl.ds(r, rows, stride=d), :] = src[0, 0, r]
    lses = [scr[2 * gi + 1][...] for gi in range(n_pat)]
    top = functools.reduce(jnp.maximum, lses)
    es = [jnp.exp(l - top) for l in lses]
    num = sum(scr[2 * gi][...] * es[gi] for gi in range(n_pat))
    o_ref[...] = num / sum(es)


def dilated_prompt(qkv_by_pattern, bsz, seq, tq=256, tt=1024):
    partial = []
    for (w, d), (qd, kd, vd) in zip(DIL_PATTERNS, qkv_by_pattern):
        sd = seq // d
        blk = (1, HEADS, 1, tq, DH)
        cur = pl.BlockSpec(blk, lambda b, r, i: (b, 0, r, i, 0))
        prev = pl.BlockSpec(blk, lambda b, r, i: (b, 0, r, jnp.maximum(i - 1, 0), 0))
        partial += pl.pallas_call(
            functools.partial(_dil_kernel, span=w // d),
            grid=(bsz, d, sd // tq),
            in_specs=[cur, cur, prev, cur, prev],
            out_specs=[cur, cur],
            out_shape=[jax.ShapeDtypeStruct((bsz, HEADS, d, sd, DH), f32)] * 2,
            scratch_shapes=[pltpu.VMEM((HEADS, tq, 2 * tq), f32), pltpu.VMEM((HEADS, tq, 2 * tq), bf16)],
            compiler_params=_cparams("arbitrary", "arbitrary", "arbitrary"),
            name=f"dilattn_d{d}",
        )(qd, kd, kd, vd, vd)
    nt = seq // tt
    in_specs = []
    for (_, d) in DIL_PATTERNS:
        in_specs += [pl.BlockSpec((1, 1, d, tt // d, DH), lambda i, h: (i // nt, h, 0, i % nt, 0))] * 2
    return pl.pallas_call(
        _dil_merge_kernel,
        grid=(bsz * nt, HEADS),
        in_specs=in_specs,
        out_specs=pl.BlockSpec((tt, DH), lambda i, h: (i, h)),
        out_shape=jax.ShapeDtypeStruct((bsz * seq, ATT_W), f32),
        scratch_shapes=[pltpu.VMEM((tt, DH), f32)] * (2 * len(DIL_PATTERNS)),
        compiler_params=_cparams("arbitrary", "arbitrary"),
        name="dilattn_merge",
    )(*partial)


def _dil_step_kernel(q_ref, kn_ref, vn_ref, k1_ref, k4_ref, k16_ref, v1_ref, v4_ref, v16_ref, o_ref):
    q = q_ref[0]
    kn, vn = kn_ref[0], vn_ref[0]
    s_self = _rowsum_bcast(q * kn)
    o_gs, lse_gs = [], []
    for k_ref, v_ref in ((k1_ref, v1_ref), (k4_ref, v4_ref), (k16_ref, v16_ref)):
        k3 = k_ref[0, :, 0]
        nk = k3.shape[0]
        s = _rowsum_bcast((k3 * q[None]).reshape(nk * HEADS, DH)).reshape(nk, HEADS, DH)
        mx = jnp.maximum(jnp.max(s, axis=0), s_self)
        p = jnp.exp(s - mx[None])
        p_self = jnp.exp(s_self - mx)
        den = jnp.sum(p, axis=0) + p_self
        num = jnp.sum(p * v_ref[0, :, 0], axis=0) + p_self * vn
        o_gs.append(num / den)
        lse_gs.append(mx + jnp.log(den))
    top = jnp.maximum(jnp.maximum(lse_gs[0], lse_gs[1]), lse_gs[2])
    es = [jnp.exp(l - top) for l in lse_gs]
    tot = es[0] + es[1] + es[2]
    o_ref[0] = (o_gs[0] * es[0] + o_gs[1] * es[1] + o_gs[2] * es[2]) / tot


def dilated_step(q4, k4, v4, cache_k, cache_v):
    n, win = cache_k.shape[0], cache_k.shape[1]
    one = pl.BlockSpec((1, HEADS, DH), lambda b: (b, 0, 0))
    args = [q4, k4, v4]
    in_specs = [one, one, one]
    for cache in (cache_k, cache_v):
        for (w, d) in DIL_PATTERNS:
            nkeys = w // d
            assert win % d == 0 and (win // d) % nkeys == 0 and (win - w) % (d * nkeys) == 0
            args.append(cache.reshape(n, win // d, d, HEADS, DH))
            in_specs.append(pl.BlockSpec((1, nkeys, 1, HEADS, DH),
                                         lambda b, blk=(win - w) // d // nkeys: (b, blk, 0, 0, 0)))
    return pl.pallas_call(
        _dil_step_kernel,
        grid=(n,),
        in_specs=in_specs,
        out_specs=one,
        out_shape=jax.ShapeDtypeStruct((n, HEADS, DH), f32),
        compiler_params=_cparams("arbitrary"),
        name="dilattn_step",
    )(*args).reshape(n, ATT_W)


def _sb_kernel(bias_ref, q_ref, k_ref, v_ref, o_ref, kb_ref, vb_ref, qs_ref, t_ref, spb_ref, wb_ref,
               acc_ref, run_ref):
    tq = q_ref.shape[1]
    nh = q_ref.shape[2] // DH
    kt_w = t_ref.shape[2]
    sub = PAGE
    hg = pl.program_id(1)
    qb = pl.program_id(2)

    @pl.when(qb == 0)
    def _():
        kb_ref[...] = k_ref[0].astype(bf16)
        vb_ref[...] = v_ref[0].astype(bf16)

    rr = _iota((sub, 2 * sub), 0)
    cc = _iota((sub, 2 * sub), 1)
    tri = jnp.where((rr > cc) | (cc >= sub), 1.0, 0.0).astype(bf16)
    for h in range(nh):
        qs_ref[h] = (q_ref[0, :, h * DH:(h + 1) * DH] * (DH ** -0.5)).astype(bf16)
    acc_ref[...] = jnp.zeros_like(acc_ref)
    run_ref[...] = jnp.zeros_like(run_ref)

    def macro(start, width, masked):
        start = pl.multiple_of(start, width)
        for h in range(nh):
            hs = slice(h * DH, (h + 1) * DH)
            z = _dot_nt(qs_ref[h], kb_ref[pl.ds(start, width), hs]) + bias_ref[hg * nh + h]
            sp = _softplus(z)
            t = z - sp
            if masked:
                ok = start + _iota((tq, width), 1) < qb * tq + _iota((tq, width), 0)
                sp = jnp.where(ok, sp, 0.0)
                t = jnp.where(ok, t, NEG_INF)
            t_ref[h, :, :width] = t
            spb_ref[h, :, :width] = sp.astype(bf16)
        for h in range(nh):
            run = run_ref[h]
            for kt in range(width // sub - 1, -1, -1):
                ks = slice(kt * sub, (kt + 1) * sub)
                cs = _dot(spb_ref[h, :, ks], tri)
                wb_ref[h, :, ks] = jnp.exp(t_ref[h, :, ks] - cs[:, :sub] - run).astype(bf16)
                run = run + cs[:, sub:]
            run_ref[h] = run
        for h in range(nh):
            hs = slice(h * DH, (h + 1) * DH)
            acc_ref[h] += _dot(wb_ref[h, :, :width], vb_ref[pl.ds(start, width), hs])

    macro(qb * tq, tq, True)
    n_part = (qb * tq % kt_w) // tq
    for j in range(1, kt_w // tq):
        @pl.when(n_part >= j)
        def _():
            macro((qb - j) * tq, tq, False)

    top = (qb * tq) // kt_w

    def body(it, _):
        macro((top - 1 - it) * kt_w, kt_w, False)
        return 0

    lax.fori_loop(0, top, body, 0)
    for h in range(nh):
        o_ref[0, :, h * DH:(h + 1) * DH] = acc_ref[h]


def sb_prompt(proj2, bias, bsz, seq, tq=256, nh=4, kt_w=512):
    assert kt_w % tq == 0 and seq % kt_w == 0
    p3 = proj2.reshape(bsz, seq, proj2.shape[1])
    nq = seq // tq
    ng = HEADS // nh
    wd = nh * DH
    return pl.pallas_call(
        _sb_kernel,
        grid=(bsz, ng, nq),
        in_specs=[
            pl.BlockSpec(memory_space=pltpu.SMEM),
            pl.BlockSpec((1, tq, wd), lambda b, g, i: (b, i, g)),
            pl.BlockSpec((1, seq, wd), lambda b, g, i: (b, 0, ng + g)),
            pl.BlockSpec((1, seq, wd), lambda b, g, i: (b, 0, 2 * ng + g)),
        ],
        out_specs=pl.BlockSpec((1, tq, wd), lambda b, g, i: (b, i, g)),
        out_shape=jax.ShapeDtypeStruct((bsz, seq, ATT_W), f32),
        scratch_shapes=[
            pltpu.VMEM((seq, wd), bf16), pltpu.VMEM((seq, wd), bf16),
            pltpu.VMEM((nh, tq, DH), bf16),
            pltpu.VMEM((nh, tq, kt_w), f32), pltpu.VMEM((nh, tq, kt_w), bf16),
            pltpu.VMEM((nh, tq, kt_w), bf16),
            pltpu.VMEM((nh, tq, DH), f32), pltpu.VMEM((nh, tq, PAGE), f32),
        ],
        compiler_params=_cparams("arbitrary", "arbitrary", "arbitrary"),
        name="sb_prompt",
    )(bias, p3, p3, p3).reshape(bsz * seq, ATT_W)


def _sb_step_kernel(pt_ref, q_ref, bias_ref, eye_ref, *refs, npg):
    k_refs, v_refs = refs[:npg], refs[npg:2 * npg]
    o_ref, acc_ref, carry_ref = refs[2 * npg:]
    j = pl.program_id(1)

    @pl.when(j == 0)
    def _():
        acc_ref[...] = jnp.zeros_like(acc_ref)
        carry_ref[...] = jnp.zeros_like(carry_ref)

    q = q_ref[0]
    bias = bias_ref[...]
    ones = jnp.ones((DH, DH), bf16)
    rr = _iota((PAGE, 2 * PAGE), 0)
    cc = _iota((PAGE, 2 * PAGE), 1)
    tri = jnp.where((rr > cc) | (cc >= PAGE), 1.0, 0.0).astype(bf16)
    acc = acc_ref[...]
    run = carry_ref[...]
    pages = range(npg)
    zbs = [_dot((k_refs[p][0] * q[None]).reshape(PAGE * HEADS, DH).astype(bf16), ones)
           .reshape(PAGE, HEADS, DH) for p in pages]
    zs = [jnp.sum(zb * eye_ref[...], axis=0) + bias for zb in zbs]
    sps = [_softplus(z) for z in zs]
    css = []
    for sp in sps:
        hi, mid, lo = _split3(sp)
        css.append(_dot(hi, tri) + _dot(mid, tri) + _dot(lo, tri))
    ws = []
    for z, sp, cs in zip(zs, sps, css):
        ws.append(jnp.exp(z - sp - cs[:, :PAGE] - run))
        run = run + cs[:, PAGE:]
    wbs = [_dot((eye_ref[...] * w[None]).reshape(PAGE * HEADS, DH).astype(bf16), ones)
           .reshape(PAGE, HEADS, DH) for w in ws]
    for p, wb in zip(pages, wbs):
        acc = acc + jnp.sum(wb * v_refs[p][0], axis=0)
    acc_ref[...] = acc
    carry_ref[...] = run

    @pl.when(j == pl.num_programs(1) - 1)
    def _():
        o_ref[0] = acc


def sb_step(q_s, bias, cache_k, cache_v, page_table, npg=8):
    n, n_pages = page_table.shape
    q4 = (q_s * (DH ** -0.5)).reshape(n, HEADS, DH)
    bias4 = jnp.broadcast_to(bias[:, None], (HEADS, DH))
    eye3 = jnp.broadcast_to(jnp.eye(PAGE, DH, dtype=f32)[:, None, :], (PAGE, HEADS, DH))

    def page(p):
        return lambda b, j, pt: (pt[b, n_pages - 1 - (j * npg + p)], 0, 0, 0)

    kv_specs = [pl.BlockSpec((1, PAGE, HEADS, DH), page(p)) for p in range(npg)]
    grid_spec = pltpu.PrefetchScalarGridSpec(
        num_scalar_prefetch=1,
        grid=(n, n_pages // npg),
        in_specs=[
            pl.BlockSpec((1, HEADS, DH), lambda b, j, pt: (b, 0, 0)),
            pl.BlockSpec((HEADS, DH), lambda b, j, pt: (0, 0)),
            pl.BlockSpec((PAGE, HEADS, DH), lambda b, j, pt: (0, 0, 0)),
        ] + kv_specs + kv_specs,
        out_specs=pl.BlockSpec((1, HEADS, DH), lambda b, j, pt: (b, 0, 0)),
        scratch_shapes=[pltpu.VMEM((HEADS, DH), f32), pltpu.VMEM((HEADS, DH), f32)],
    )
    return pl.pallas_call(
        functools.partial(_sb_step_kernel, npg=npg),
        grid_spec=grid_spec,
        out_shape=jax.ShapeDtypeStruct((n, HEADS, DH), f32),
        compiler_params=_cparams("arbitrary", "arbitrary"),
        name="sb_step",
    )(page_table, q4, bias4, eye3, *([cache_k] * npg), *([cache_v] * npg)).reshape(n, ATT_W)


def _mlstm_kernel(q_ref, k_ref, v_ref, og_ref, gate_ref, gb_ref, gn_ref,
                  h_ref, c_out, n_out, m_out, c_s, n_s, m_s):
    ci = pl.program_id(1)
    L = CHUNK

    @pl.when(ci == 0)
    def _():
        c_s[...] = jnp.zeros_like(c_s)
        n_s[...] = jnp.zeros_like(n_s)
        m_s[...] = jnp.zeros_like(m_s)

    gt = gate_ref[...] + gb_ref[...]
    gt_t = gt.T
    ri = _iota((L, L), 0)
    li = _iota((L, L), 1)
    causal = li <= ri
    tri_incl = jnp.where(causal, 1.0, 0.0).astype(bf16)
    tri_incl_t = jnp.where(ri <= li, 1.0, 0.0).astype(bf16)
    heads = range(M_HEADS)
    sls = [slice(h * M_DH, (h + 1) * M_DH) for h in heads]
    ig_col = [gt[:, h:h + 1] for h in heads]
    ig_row = [gt_t[h:h + 1, :] for h in heads]
    bcum_col, bcum_row = [], []
    for h in heads:
        lf_col = _log_sigmoid_pair(gt[:, M_HEADS + h:M_HEADS + h + 1])[0]
        lf_row = _log_sigmoid_pair(gt_t[M_HEADS + h:M_HEADS + h + 1, :])[0]
        c_hi, c_mid, c_lo = _split3(jnp.broadcast_to(lf_col, (L, L)))
        bcum_col.append(_dot(tri_incl, c_hi) + _dot(tri_incl, c_mid) + _dot(tri_incl, c_lo))
        r_hi, r_mid, r_lo = _split3(jnp.broadcast_to(lf_row, (L, L)))
        bcum_row.append(_dot(r_hi, tri_incl_t) + _dot(r_mid, tri_incl_t) + _dot(r_lo, tri_incl_t))
    qb = [q_ref[:, sl].astype(bf16) for sl in sls]
    kb = [(k_ref[:, sl] * (M_DH ** -0.5)).astype(bf16) for sl in sls]
    vb = [v_ref[:, sl].astype(bf16) for sl in sls]
    qk = [_dot_nt(qb[h], kb[h]) for h in heads]
    qc = [_dot_nt(qb[h], c_s[h].astype(bf16)) for h in heads]
    m_prev = [m_s[h:h + 1, :] for h in heads]
    m_t, sw, iw_col = [], [], []
    for h in heads:
        dlog = jnp.where(causal, bcum_col[h] - bcum_row[h] + ig_row[h], NEG_INF)
        inter = bcum_col[h] + m_prev[h]
        mt = jnp.maximum(inter, jnp.max(dlog, axis=1, keepdims=True))
        m_t.append(mt)
        sw.append(jnp.exp(dlog - mt) * qk[h])
        iw_col.append(jnp.exp(inter - mt)[:, 0:1])
    sv = [_dot(sw[h].astype(bf16), vb[h]) for h in heads]
    for h in heads:
        n_prev = n_s[h:h + 1, :]
        num = sv[h] + iw_col[h] * qc[h]
        qn = jnp.sum(qb[h].astype(f32) * n_prev.astype(bf16).astype(f32), axis=1, keepdims=True)
        den = jnp.sum(sw[h], axis=1, keepdims=True) + iw_col[h] * qn
        hout = num / jnp.maximum(jnp.abs(den), jnp.exp(-m_t[h][:, 0:1]))
        hn = hout * lax.rsqrt(jnp.mean(hout * hout, axis=1, keepdims=True) + EPS)
        h_ref[:, sls[h]] = hn * gn_ref[:, sls[h]] * _sigmoid(og_ref[:, sls[h]])
    for h in heads:
        m_last = m_t[h][L - 1:L, :]
        b_last = bcum_col[h][L - 1:L, :]
        wl_col = jnp.exp(b_last[:, 0:1] - bcum_col[h][:, 0:1] + ig_col[h] - m_last[:, 0:1])
        wl_row = jnp.exp(b_last - bcum_row[h][0:1, :] + ig_row[h] - m_last)
        dsc = jnp.exp(b_last + m_prev[h] - m_last)[:, 0:1]
        c_s[h] = dsc * c_s[h] + _dot((v_ref[:, sls[h]] * wl_col).T.astype(bf16), kb[h])
        wl8 = jnp.broadcast_to(wl_row, (8, L)).astype(bf16)
        n_s[h:h + 1, :] = dsc * n_s[h:h + 1, :] + _dot(wl8, kb[h])[0:1, :]
        m_s[h:h + 1, :] = m_last

    @pl.when(ci == pl.num_programs(1) - 1)
    def _():
        c_out[0] = c_s[...]
        n_out[0] = n_s[...]
        m_out[0] = m_s[...]


def mlstm_prompt(proj2, gates, gate_bias, g_mnorm, bsz, seq):
    nc = seq // CHUNK
    col = lambda c: pl.BlockSpec((CHUNK, M_WIDTH), lambda b, i, c=c: (b * nc + i, c))
    hm, c1, n1, m1 = pl.pallas_call(
        _mlstm_kernel,
        grid=(bsz, nc),
        in_specs=[
            col(3), col(4), col(5), col(6),
            pl.BlockSpec((CHUNK, LANES), lambda b, i: (b * nc + i, 0)),
            pl.BlockSpec((1, LANES), lambda b, i: (0, 0)),
            pl.BlockSpec((1, M_WIDTH), lambda b, i: (0, 0)),
        ],
        out_specs=[
            pl.BlockSpec((CHUNK, M_WIDTH), lambda b, i: (b * nc + i, 0)),
            pl.BlockSpec((1, M_HEADS, M_DH, M_DH), lambda b, i: (b, 0, 0, 0)),
            pl.BlockSpec((1, M_HEADS, M_DH), lambda b, i: (b, 0, 0)),
            pl.BlockSpec((1, M_HEADS, LANES), lambda b, i: (b, 0, 0)),
        ],
        out_shape=[
            jax.ShapeDtypeStruct((bsz * seq, M_WIDTH), f32),
            jax.ShapeDtypeStruct((bsz, M_HEADS, M_DH, M_DH), f32),
            jax.ShapeDtypeStruct((bsz, M_HEADS, M_DH), f32),
            jax.ShapeDtypeStruct((bsz, M_HEADS, LANES), f32),
        ],
        scratch_shapes=[
            pltpu.VMEM((M_HEADS, M_DH, M_DH), f32),
            pltpu.VMEM((M_HEADS, M_DH), f32),
            pltpu.VMEM((M_HEADS, LANES), f32),
        ],
        compiler_params=_cparams("arbitrary", "arbitrary"),
        name="mlstm_prompt",
    )(proj2, proj2, proj2, proj2, gates, gate_bias, g_mnorm)
    return hm, c1, n1, m1[:, :, 0]


def _mlstm_step_kernel(q_ref, k_ref, v_ref, og_ref, gn_ref, ig_ref, fg_ref, c_ref, n_ref, m_ref,
                       h_ref, c_out, n_out, m_out):
    q = q_ref[0, 0]
    k = k_ref[0, 0] * (M_DH ** -0.5)
    v = v_ref[0, 0]
    ig = ig_ref[0, 0]
    lf = _log_sigmoid_pair(fg_ref[0, 0])[0]
    m0 = m_ref[0, 0]
    c0 = c_ref[0, 0]
    n0 = n_ref[0, 0]
    inter = lf + m0
    m_t = jnp.maximum(inter, ig)
    dw = jnp.exp(ig - m_t)
    iw = jnp.exp(inter - m_t)
    rnd = lambda a: a.astype(bf16).astype(f32)
    qr, kr, vr = rnd(q), rnd(k), rnd(v)
    qk = jnp.sum(qr * kr, axis=1, keepdims=True)
    sw = dw * qk
    cq = jnp.sum(rnd(c0) * qr, axis=1, keepdims=True)
    num = rnd(sw) * vr + iw * cq
    den = sw + iw * jnp.sum(rnd(n0) * qr, axis=1, keepdims=True)
    hout = num / jnp.maximum(jnp.abs(den), jnp.exp(-m_t))
    c_out[0, 0] = iw * c0 + rnd(dw * v) * kr
    n_out[0, 0] = iw * n0 + rnd(dw) * kr
    m_out[0, 0] = m_t
    hn = hout * lax.rsqrt(jnp.mean(hout * hout, axis=0, keepdims=True) + EPS)
    h_ref[0, 0] = hn * gn_ref[0] * _sigmoid(og_ref[0, 0])


def mlstm_step(proj2_s, gates_s, gate_bias, g_mnorm, c0, n0, m0):
    n = proj2_s.shape[0]
    base = 3 * ATT_W
    seg = lambda i: proj2_s[:, base + i * M_WIDTH: base + (i + 1) * M_WIDTH].reshape(n, M_HEADS, M_DH)
    g = gates_s + gate_bias
    rowb = pl.BlockSpec((1, 1, 1, M_DH), lambda b, h: (b, h, 0, 0))
    colb = pl.BlockSpec((1, 1, M_DH, 1), lambda b, h: (b, h, 0, 0))
    scal = pl.BlockSpec((1, 1, 1, 1), lambda b, h: (b, h, 0, 0))
    hcol, c1, n1, m1 = pl.pallas_call(
        _mlstm_step_kernel,
        grid=(n, M_HEADS),
        in_specs=[
            rowb, rowb, colb, colb,
            pl.BlockSpec((1, M_DH, 1), lambda b, h: (h, 0, 0)),
            scal, scal,
            pl.BlockSpec((1, 1, M_DH, M_DH), lambda b, h: (b, h, 0, 0)),
            rowb, scal,
        ],
        out_specs=[colb, pl.BlockSpec((1, 1, M_DH, M_DH), lambda b, h: (b, h, 0, 0)), rowb, scal],
        out_shape=[
            jax.ShapeDtypeStruct((n, M_HEADS, M_DH, 1), f32),
            jax.ShapeDtypeStruct((n, M_HEADS, M_DH, M_DH), f32),
            jax.ShapeDtypeStruct((n, M_HEADS, 1, M_DH), f32),
            jax.ShapeDtypeStruct((n, M_HEADS, 1, 1), f32),
        ],
        compiler_params=_cparams("arbitrary", "arbitrary"),
        name="mlstm_step",
    )(seg(0)[:, :, None, :], seg(1)[:, :, None, :], seg(2)[..., None], seg(3)[..., None],
      g_mnorm.reshape(M_HEADS, M_DH, 1),
      g[:, 0:M_HEADS].reshape(n, M_HEADS, 1, 1), g[:, M_HEADS:2 * M_HEADS].reshape(n, M_HEADS, 1, 1),
      c0, n0[:, :, None, :], m0.reshape(n, M_HEADS, 1, 1))
    return hcol.reshape(n, M_WIDTH), c1, n1.reshape(n, M_HEADS, M_DH), m1.reshape(n, M_HEADS)


def _peerq_kernel(x_ref, g_ref, sh_ref, sc_ref, wt_ref, qt_ref, xmt_ref, xs_ref):
    @pl.when(pl.program_id(1) == 0)
    def _():
        xm = _modulated(x_ref[...], g_ref[...], sh_ref[0], sc_ref[0])
        xt = xm.T.astype(bf16)
        xs_ref[...] = xt
        xmt_ref[...] = xt

    qt_ref[...] = _dot(wt_ref[...], xs_ref[...])


def peer_query(x, g, shift, scale, wq_t, rows_per_batch, tm, tn=512):
    m, k = x.shape
    n = wq_t.shape[0]
    sh, sh_spec = _mod_specs(shift, m, tm, k, rows_per_batch)
    sc, sc_spec = _mod_specs(scale, m, tm, k, rows_per_batch)
    return pl.pallas_call(
        _peerq_kernel,
        grid=(m // tm, n // tn),
        in_specs=[
            pl.BlockSpec((tm, k), lambda i, j: (i, 0)),
            pl.BlockSpec((1, k), lambda i, j: (0, 0)),
            sh_spec, sc_spec,
            pl.BlockSpec((tn, k), lambda i, j: (j, 0)),
        ],
        out_specs=[pl.BlockSpec((tn, tm), lambda i, j: (j, i)),
                   pl.BlockSpec((k, tm), lambda i, j: (0, i))],
        out_shape=[jax.ShapeDtypeStruct((n, m), f32), jax.ShapeDtypeStruct((k, m), bf16)],
        scratch_shapes=[pltpu.VMEM((k, tm), bf16)],
        compiler_params=_cparams("arbitrary", "arbitrary"),
        name="peer_query",
    )(x, g.reshape(1, k), sh, sc, wq_t)


def _cast_t_kernel(w_ref, o_ref):
    o_ref[...] = w_ref[0].T.astype(bf16)


def cast_layer_bf16_t(w, layer, tr=512):
    _, rows, cols = w.shape
    return pl.pallas_call(
        _cast_t_kernel,
        grid=(rows // tr,),
        in_specs=[pl.BlockSpec((1, tr, cols), lambda i: (layer, i, 0))],
        out_specs=pl.BlockSpec((cols, tr), lambda i: (0, i)),
        out_shape=jax.ShapeDtypeStruct((cols, rows), bf16),
        compiler_params=_cparams("arbitrary"),
        name="cast_bf16_t",
    )(w)


def _heads_layout_kernel(*refs):
    n = len(refs) // 2
    for x_ref, o_ref in zip(refs[:n], refs[n:]):
        for hb in range(HEADS):
            o_ref[:, hb, :] = x_ref[:, hb * DH:(hb + 1) * DH]


def heads_layout(srcs, bsz, seq, tail, tt=512):
    nt, first = tail // tt, (seq - tail) // tt
    rows = lambda b, i: b * (seq // tt) + first + i
    return pl.pallas_call(
        _heads_layout_kernel,
        grid=(bsz, nt),
        in_specs=[pl.BlockSpec((tt, ATT_W), lambda b, i, c=c: (rows(b, i), c)) for _, c in srcs],
        out_specs=[pl.BlockSpec((tt, HEADS, DH), lambda b, i: (b * nt + i, 0, 0))] * len(srcs),
        out_shape=[jax.ShapeDtypeStruct((bsz * tail, HEADS, DH), f32)] * len(srcs),
        compiler_params=_cparams("arbitrary", "arbitrary"),
        name="heads_layout",
    )(*[x for x, _ in srcs])


def _cast_kernel(w_ref, o_ref):
    o_ref[...] = w_ref[0].astype(bf16)


def cast_layer_bf16(w, layer, tr=1024):
    _, rows, cols = w.shape
    return pl.pallas_call(
        _cast_kernel,
        grid=(rows // tr,),
        in_specs=[pl.BlockSpec((1, tr, cols), lambda i: (layer, i, 0))],
        out_specs=pl.BlockSpec((tr, cols), lambda i: (i, 0)),
        out_shape=jax.ShapeDtypeStruct((rows, cols), bf16),
        compiler_params=_cparams("arbitrary"),
        name="cast_bf16",
    )(w)


_N_TOP = PEER_TOPK + 1
_CAND_PAIRS = [(a, b) for a in range(_N_TOP) for b in range(_N_TOP) if (a + 1) * (b + 1) <= _N_TOP]
_N_CAND = -(-len(_CAND_PAIRS) // 8) * 8


def _oddeven_mergesort_pairs(n):
    pairs = []

    def merge(lo, cnt, r):
        step = 2 * r
        if step < cnt:
            merge(lo, cnt, step)
            merge(lo + r, cnt, step)
            pairs.extend((i, i + r) for i in range(lo + r, lo + cnt - r, step))
        else:
            pairs.append((lo, lo + r))

    def sort(lo, cnt):
        if cnt > 1:
            sort(lo, cnt // 2)
            sort(lo + cnt // 2, cnt // 2)
            merge(lo, cnt, 1)

    sort(0, n)
    return pairs


_SORT16 = _oddeven_mergesort_pairs(PEER_TOPK)


def _exchange(y, i, j):
    y[i], y[j] = jnp.maximum(y[i], y[j]), jnp.minimum(y[i], y[j])


def _top17_network(s):
    y = [s[v * SUBLANES:(v + 1) * SUBLANES, :] for v in range(PEER_TOPK)]
    for i, j in _SORT16:
        _exchange(y, i, j)
    return _merge_top17(y, (4, 2, 1))


def _bitonic_sort(y):
    dist = len(y) // 2
    while dist:
        for i in range(len(y)):
            if not i & dist:
                _exchange(y, i, i + dist)
        dist //= 2


def _merge_top17(y, shifts):
    n = len(y)
    dropped = None
    for shift in shifts:
        other = [pltpu.roll(y[n - 1 - i], shift, 0) for i in range(n)]
        low = functools.reduce(jnp.maximum, [jnp.minimum(y[i], other[i]) for i in range(n)])
        dropped = low if dropped is None else jnp.maximum(dropped, low)
        y = [jnp.maximum(y[i], other[i]) for i in range(n)]
        _bitonic_sort(y)
    return [v[0:1, :] for v in y] + [jnp.max(dropped, axis=0, keepdims=True)]


_SORT8 = _oddeven_mergesort_pairs(SUBLANES)


def _cand_top17(cand):
    n = SUBLANES
    y = [cand[v * SUBLANES:(v + 1) * SUBLANES, :] for v in range(cand.shape[0] // SUBLANES)]
    y += [jnp.full((SUBLANES, LANES), NEG_INF, f32)] * (n - len(y))
    for i, j in _SORT8:
        _exchange(y, i, j)
    other = [pltpu.roll(y[n - 1 - i], 4, 0) for i in range(n)]
    high = [jnp.maximum(y[i], other[i]) for i in range(n)]
    low = [jnp.minimum(y[i], other[i]) for i in range(n)]
    _bitonic_sort(high)
    _bitonic_sort(low)
    return _merge_top17(high + low, (2, 1))


_ROUTER_UNROLL = 8


def _router_kernel(qt_ref, keys_ref, e1_ref, e2_ref, th_ref, cand_ref):
    k0 = keys_ref[0].astype(bf16)
    k1 = keys_ref[1].astype(bf16)
    cand_ref[...] = jnp.full(cand_ref.shape, NEG_INF, f32)

    def one_head(h, slot):
        r0 = pl.multiple_of(h * 2 * N_KEYS, 2 * N_KEYS)
        s1 = _dot(k0, qt_ref[pl.ds(r0, N_KEYS), :].astype(bf16))
        s2 = _dot(k1, qt_ref[pl.ds(r0 + N_KEYS, N_KEYS), :].astype(bf16))
        top_a = _top17_network(s1)
        top_b = _top17_network(s2)
        for ci, (a, b) in enumerate(_CAND_PAIRS):
            cand_ref[slot, ci:ci + 1, :] = top_a[a] + top_b[b]
        cs = _cand_top17(cand_ref[slot])
        z = jnp.ones_like(cs[0])
        for r in range(1, PEER_TOPK):
            z = z + jnp.exp(cs[r] - cs[0])
        inv_z = 1.0 / z
        mid = 0.5 * (cs[PEER_TOPK - 1] + cs[PEER_TOPK])
        o0 = pl.multiple_of(h * N_KEYS, N_KEYS)
        e1 = jnp.exp(s1 - top_a[0]) * inv_z
        e1_ref[:, h] = e1.reshape(N_KEYS // SUBLANES, SUBLANES, LANES)
        e2_ref[pl.ds(o0, N_KEYS), :] = jnp.exp(s2 - top_b[0])
        th_ref[pl.ds(h, 1), :] = jnp.exp(mid - cs[0]) * inv_z

    def body(hp, _):
        for slot in range(_ROUTER_UNROLL):
            one_head(hp * _ROUTER_UNROLL + slot, slot)
        return 0

    lax.fori_loop(0, PEER_HEADS // _ROUTER_UNROLL, body, 0)


def peer_router(qt, keys):
    n, m = qt.shape
    half = PEER_HEADS * N_KEYS
    return pl.pallas_call(
        _router_kernel,
        grid=(m // LANES,),
        in_specs=[pl.BlockSpec((n, LANES), lambda i: (0, i)),
                  pl.BlockSpec((2, N_KEYS, N_KEYS), lambda i: (0, 0, 0))],
        out_specs=[pl.BlockSpec((N_KEYS // SUBLANES, PEER_HEADS, SUBLANES, LANES), lambda i: (0, 0, 0, i)),
                   pl.BlockSpec((half, LANES), lambda i: (0, i)),
                   pl.BlockSpec((PEER_HEADS, LANES), lambda i: (0, i))],
        out_shape=[jax.ShapeDtypeStruct((N_KEYS // SUBLANES, PEER_HEADS, SUBLANES, m), f32),
                   jax.ShapeDtypeStruct((half, m), f32),
                   jax.ShapeDtypeStruct((PEER_HEADS, m), f32)],
        scratch_shapes=[pltpu.VMEM((_ROUTER_UNROLL, _N_CAND, LANES), f32)],
        compiler_params=_cparams("arbitrary"),
        name="peer_router",
    )(qt, keys)


_G_CHAINS = 3
_G_ROWS = SUBLANES


class _Chains:
    def __init__(self, n, zero_ref):
        self.deps, self.cnt, self.zero_ref = [None] * n, 0, zero_ref

    def take(self):
        return self.deps[self.cnt % len(self.deps)]

    def put(self, g):
        bits = lax.bitcast_convert_type(g[0:1, :], jnp.int32) & self.zero_ref[0:1, :]
        self.deps[self.cnt % len(self.deps)] = lax.bitcast_convert_type(bits, f32)
        self.cnt += 1


def _routing_block(e1_ref, e2_ref, th_ref, ii, ls, chains):
    sub = _G_ROWS
    blk, r = divmod(ii, SUBLANES)
    rows = [e1_ref[blk, h, r:r + 1, ls] for h in range(PEER_HEADS)]
    ths = [th_ref[h:h + 1, ls] for h in range(PEER_HEADS)]
    parts = []
    for gi in range(N_KEYS // sub):
        dep = chains.take()
        g = jnp.zeros((sub, LANES), f32)
        for h in range(PEER_HEADS):
            row = rows[h] if (dep is None or h) else rows[h] + dep
            pr = e2_ref[h * N_KEYS + gi * sub:h * N_KEYS + (gi + 1) * sub, ls] * row
            g = g + jnp.where(pr >= ths[h], pr, 0.0)
        chains.put(g)
        parts.append(g)
    return jnp.concatenate(parts, axis=0)


def _experts_kernel(xmt_ref, e1_ref, e2_ref, th_ref, u_ref, v_ref, x_ref, gt_ref, gf_ref, zero_ref,
                    o_ref, w_ref, *, final_norm):
    c = pl.program_id(1)
    tm, te = w_ref.shape

    @pl.when(c == 0)
    def _():
        o_ref[...] = jnp.zeros_like(o_ref)

    chains = _Chains(_G_CHAINS, zero_ref)
    act = _gelu(_dot(u_ref[...], xmt_ref[...]))
    for ii in range(te // N_KEYS):
        es = slice(ii * N_KEYS, (ii + 1) * N_KEYS)
        for lc in range(tm // LANES):
            ls = slice(lc * LANES, (lc + 1) * LANES)
            g_blk = _routing_block(e1_ref, e2_ref, th_ref, ii, ls, chains)
            w_ref[ls, es] = (g_blk * act[es, ls]).T.astype(bf16)
    o_ref[...] += _dot(w_ref[...], v_ref[...])

    @pl.when(c == pl.num_programs(1) - 1)
    def _():
        y = x_ref[...] + gt_ref[0] * o_ref[...]
        if final_norm:
            ms = jnp.mean(y * y, axis=-1, keepdims=True)
            y = y * lax.rsqrt(ms + EPS) * gf_ref[...]
        o_ref[...] = y


def peer_experts(xmt, e1t, e2t, th, u, v, x, gate, g_final, rows_per_batch, tm, te, final_norm):
    m, d = x.shape
    gt, gt_spec = _mod_specs(gate, m, tm, d, rows_per_batch)
    half = PEER_HEADS * N_KEYS
    return pl.pallas_call(
        functools.partial(_experts_kernel, final_norm=final_norm),
        grid=(m // tm, N_EXPERTS // te),
        in_specs=[
            pl.BlockSpec((d, tm), lambda i, c: (0, i)),
            pl.BlockSpec((te // N_KEYS // SUBLANES, PEER_HEADS, SUBLANES, tm), lambda i, c: (c, 0, 0, i)),
            pl.BlockSpec((half, tm), lambda i, c: (0, i)),
            pl.BlockSpec((PEER_HEADS, tm), lambda i, c: (0, i)),
            pl.BlockSpec((te, d), lambda i, c: (c, 0)),
            pl.BlockSpec((te, d), lambda i, c: (c, 0)),
            pl.BlockSpec((tm, d), lambda i, c: (i, 0)),
            gt_spec,
            pl.BlockSpec((1, d), lambda i, c: (0, 0)),
            pl.BlockSpec((SUBLANES, LANES), lambda i, c: (0, 0)),
        ],
        out_specs=pl.BlockSpec((tm, d), lambda i, c: (i, 0)),
        out_shape=jax.ShapeDtypeStruct((m, d), f32),
        scratch_shapes=[pltpu.VMEM((tm, te), bf16)],
        compiler_params=_cparams("arbitrary", "arbitrary"),
        name="peer_experts",
    )(xmt, e1t, e2t, th, u, v, x, gt, g_final.reshape(1, d), jnp.zeros((SUBLANES, LANES), jnp.int32))


def peer_block(x, g, shift, scale, gate, wq_t, keys, u, v, g_final, rows_per_batch, tm, te, final_norm):
    tm_q = 1024 if x.shape[0] % 1024 == 0 and rows_per_batch % 1024 == 0 else tm
    qt, xmt = peer_query(x, g, shift, scale, wq_t, rows_per_batch, tm_q)
    e1t, e2t, th = peer_router(qt, keys)
    return peer_experts(xmt, e1t, e2t, th, u, v, x, gate, g_final, rows_per_batch, tm, te, final_norm)


def kernel(x_prompt, x_sample, c_prompt, c_sample, state_rglru_conv, state_rglru_h, cache_swa_k, cache_swa_v, cache_sb_k, cache_sb_v, state_mlstm_C, state_mlstm_n, state_mlstm_m, page_table, w_ada, b_ada, g_norm_mix, g_norm_ffn, e_w_in, e_conv_w, e_conv_b, e_w_r, e_b_r, e_w_i, e_b_i, e_lambda, e_w_out, o_w_in, o_b_if, o_sb_bias, o_g_mnorm, o_w_out, peer_w_q, peer_keys, peer_u, peer_v, g_final):
    bp, seq, d = x_prompt.shape
    bs = x_sample.shape[0]
    mp = bp * seq
    pad_s = LANES
    xp = x_prompt.reshape(mp, d)
    xs = x_sample.reshape(bs, d)

    c_rows = 16
    c_all = jnp.concatenate([c_prompt, c_sample, jnp.zeros((c_rows - bp - bs, d), f32)], axis=0)
    mod = adaln_all(c_all, w_ada, b_ada)

    def mods(layer):
        parts = [mod[layer, :, i * d:(i + 1) * d] for i in range(6)]
        return [p[:bp] for p in parts], [p[bp:bp + bs] for p in parts]

    ctab_p, stab_p = _rope_tables(jnp.arange(seq, dtype=jnp.int32))
    ctab_s, stab_s = _rope_tables(jnp.full((1,), PAST_LEN, jnp.int32))

    TM = 512
    m_p, m_s = mods(0)
    w_in = e_w_in[0].astype(bf16)
    w_out = e_w_out[0].astype(bf16)
    cw, cb = e_conv_w[0], e_conv_b[0].reshape(1, -1)
    wr, wi = e_w_r[0].astype(bf16), e_w_i[0].astype(bf16)
    br, bi, lam = e_b_r[0].reshape(1, -1), e_b_i[0].reshape(1, -1), e_lambda[0].reshape(1, -1)

    proj_p = mod_matmul(xp, g_norm_mix[0], m_p[0], m_p[1], w_in, E_IN, seq, 1024, 1024, name="e_in_p")
    proj_s = mod_matmul(xs, g_norm_mix[0], m_s[0], m_s[1], w_in, E_IN, 1, bs, 512, name="e_in_s")

    ya_p, h_p = rglru_prompt(proj_p, bp, seq, cw, cb, wr, br, wi, bi, lam)
    ya_s, h_s = rglru_step(proj_s, state_rglru_conv[0], state_rglru_h[0], cw, cb, wr, br, wi, bi, lam)
    conv_p = proj_p.reshape(bp, seq, E_IN)[:, seq - (CONV_W - 1):, :RG_WIDTH]
    conv_s = jnp.concatenate([state_rglru_conv[0][:, 1:], proj_s[:, None, :RG_WIDTH]], axis=1)

    qkv_p, k_p = rope_split(proj_p, ctab_p, stab_p, bp, seq)
    q_s, k_s, v_s = rope_qkv(proj_s, ctab_s, stab_s, 1, bs)
    o_p = dilated_prompt(qkv_p, bp, seq)
    win = cache_swa_k.shape[2]
    o_s = dilated_step(q_s, k_s, v_s, cache_swa_k.reshape(-1, win, HEADS, DH)[:bs],
                       cache_swa_v.reshape(-1, win, HEADS, DH)[:bs])
    wl = min(2048, seq)
    swa_k_p, swa_v_p = [a.reshape(bp, wl, HEADS, DH)
                        for a in heads_layout([(k_p, 0), (proj_p, E_IN // ATT_W - 1)], bp, seq, wl)]
    swa_k_s = k_s.reshape(bs, 1, HEADS, DH)
    swa_v_s = v_s.reshape(bs, 1, HEADS, DH)

    xp = out_proj(ya_p, o_p, w_out, xp, m_p[2], seq, 1024, 1024, name="e_out_p")
    xs = out_proj(ya_s, o_s, w_out, xs, m_s[2], 1, bs, 512, name="e_out_s")

    def peer_layer(layer, xp, xs, m_p, m_s, final_norm):
        wq_t = cast_layer_bf16_t(peer_w_q, layer)
        u = cast_layer_bf16(peer_u, layer)
        v = cast_layer_bf16(peer_v, layer)
        xp = peer_block(xp, g_norm_ffn[layer], m_p[3], m_p[4], m_p[5], wq_t, peer_keys[layer], u, v,
                        g_final, seq, TM, 1024, final_norm)
        xs_pad = jnp.pad(xs, ((0, pad_s - bs), (0, 0)))
        xs_new = peer_block(xs_pad, g_norm_ffn[layer], m_s[3], m_s[4], m_s[5], wq_t, peer_keys[layer],
                            u, v, g_final, 1, pad_s, 1024, final_norm)
        return xp, xs_new[:bs]

    xp, xs = peer_layer(0, xp, xs, m_p, m_s, False)

    m_p, m_s = mods(1)
    w_in2 = o_w_in[0].astype(bf16)
    w_gate = jnp.pad(w_in2[:, O_MAIN:], ((0, 0), (0, LANES - 2 * M_HEADS)))
    w_out2 = o_w_out[0].astype(bf16)
    gate_bias = jnp.pad(o_b_if[0].reshape(1, 2 * M_HEADS), ((0, 0), (0, LANES - 2 * M_HEADS)))
    gmn = o_g_mnorm[0].reshape(1, M_WIDTH)

    proj2_p, gates_p = mod_matmul(xp, g_norm_mix[1], m_p[0], m_p[1], w_in2, O_MAIN, seq, 1024, 1024,
                                  w_gate=w_gate, name="o_in_p")
    proj2_s, gates_s = mod_matmul(xs, g_norm_mix[1], m_s[0], m_s[1], w_in2, O_MAIN, 1, bs, 512,
                                  w_gate=w_gate, name="o_in_s")

    oc_p = sb_prompt(proj2_p, o_sb_bias[0], bp, seq)
    oc_s = sb_step(proj2_s[:, :ATT_W], o_sb_bias[0], cache_sb_k.reshape(-1, PAGE, HEADS, DH),
                   cache_sb_v.reshape(-1, PAGE, HEADS, DH), page_table)
    hm_p, mC_p, mn_p, mm_p = mlstm_prompt(proj2_p, gates_p, gate_bias, gmn, bp, seq)
    hm_s, mC_s, mn_s, mm_s = mlstm_step(proj2_s, gates_s, gate_bias, gmn,
                                        state_mlstm_C.reshape(-1, M_HEADS, M_DH, M_DH)[:bs],
                                        state_mlstm_n[0], state_mlstm_m[0])

    n_pg = seq // PAGE
    sb_k_p, sb_v_p = [a.reshape(bp, n_pg, PAGE, HEADS, DH)
                      for a in heads_layout([(proj2_p, 1), (proj2_p, 2)], bp, seq, seq)]
    sb_k_s = proj2_s[:, ATT_W:2 * ATT_W].reshape(bs, 1, HEADS, DH)
    sb_v_s = proj2_s[:, 2 * ATT_W:3 * ATT_W].reshape(bs, 1, HEADS, DH)

    xp = out_proj(oc_p, hm_p, w_out2, xp, m_p[2], seq, 1024, 1024, name="o_out_p")
    xs = out_proj(oc_s, hm_s, w_out2, xs, m_s[2], 1, bs, 512, name="o_out_s")
    xp, xs = peer_layer(1, xp, xs, m_p, m_s, True)

    y_prompt = xp.reshape(bp, seq, d)
    y_sample = xs.reshape(bs, 1, d)
    st = lambda a: a[None]
    return (y_prompt, y_sample, st(conv_p), st(conv_s), st(h_p.reshape(bp, RG_WIDTH)), st(h_s),
            st(swa_k_p), st(swa_k_s), st(swa_v_p), st(swa_v_s),
            st(sb_k_p), st(sb_k_s), st(sb_v_p), st(sb_v_s),
            st(mC_p), st(mC_s), st(mn_p), st(mn_s), st(mm_p), st(mm_s))
```

```python
import functools
import math

import jax
import jax.numpy as jnp
from jax import lax
from jax.experimental import pallas as pl
from jax.experimental.pallas import tpu as pltpu

f32 = jnp.float32
bf16 = jnp.bfloat16

D_MODEL = 2048
PAST_LEN = 16384
PAGE = 128
RG_WIDTH = 1024
RG_BLOCKS = 8
CONV_W = 4
RG_C = 8.0
HEADS = 8
DH = 128
ATT_W = HEADS * DH
DIL_PATTERNS = ((128, 1), (512, 4), (2048, 16))
ROT_DIMS = 32
ROPE_THETA = 500000.0
M_HEADS = 4
M_DH = 256
M_WIDTH = M_HEADS * M_DH
CHUNK = 128
E_IN = 2 * RG_WIDTH + 3 * ATT_W
O_MAIN = 3 * ATT_W + 4 * M_WIDTH
PEER_HEADS = 8
N_KEYS = 128
N_EXPERTS = N_KEYS * N_KEYS
PEER_TOPK = 16
EPS = 1e-6
LANES = 128
SUBLANES = 8
VMEM_LIMIT = 56 * 1024 * 1024
NEG_INF = float("-inf")


def _cparams(*sem):
    return pltpu.CompilerParams(dimension_semantics=sem, vmem_limit_bytes=VMEM_LIMIT)


def _dot(a, b):
    return jnp.dot(a, b, preferred_element_type=f32)


def _dot_nt(a, b):
    return lax.dot_general(a, b, (((1,), (1,)), ((), ())), preferred_element_type=f32)


def _split3(x):
    hi = x.astype(bf16)
    r = x - hi.astype(f32)
    mid = r.astype(bf16)
    lo = (r - mid.astype(f32)).astype(bf16)
    return hi, mid, lo


def _sigmoid(x):
    return 1.0 / (1.0 + jnp.exp(-x))


def _log_sigmoid_pair(z):
    l1p = jnp.log1p(jnp.exp(-jnp.abs(z)))
    return jnp.minimum(z, 0.0) - l1p, -jnp.maximum(z, 0.0) - l1p


def _softplus(z):
    return jnp.maximum(z, 0.0) + jnp.log(1.0 + jnp.exp(-jnp.abs(z)))


def _gelu(x):
    c = math.sqrt(2.0 / math.pi)
    h = 0.5 * x
    return h + h * jnp.tanh(x * (c + (c * 0.044715) * (x * x)))


def _iota(shape, dim):
    return lax.broadcasted_iota(jnp.int32, shape, dim)


def _rowsum_bcast(x):
    ones = jnp.ones((LANES, LANES), bf16)
    hi = x.astype(bf16)
    lo = (x - hi.astype(f32)).astype(bf16)
    return _dot(hi, ones) + _dot(lo, ones)


def _adaln_kernel(c_ref, w_ref, b_ref, o_ref):
    c = c_ref[...]
    s = c * _sigmoid(c)
    w = w_ref[0]
    s_hi = s.astype(bf16)
    s_lo = (s - s_hi.astype(f32)).astype(bf16)
    w_hi = w.astype(bf16)
    w_lo = (w - w_hi.astype(f32)).astype(bf16)
    o_ref[0] = _dot(s_hi, w_hi) + _dot(s_hi, w_lo) + _dot(s_lo, w_hi) + b_ref[0]


def adaln_all(c_all, w_ada, b_ada):
    depth, d, n = w_ada.shape
    rows = c_all.shape[0]
    tn = 1024
    return pl.pallas_call(
        _adaln_kernel,
        grid=(depth, n // tn),
        in_specs=[
            pl.BlockSpec((rows, d), lambda l, j: (0, 0)),
            pl.BlockSpec((1, d, tn), lambda l, j: (l, 0, j)),
            pl.BlockSpec((1, 1, tn), lambda l, j: (l, 0, j)),
        ],
        out_specs=pl.BlockSpec((1, rows, tn), lambda l, j: (l, 0, j)),
        out_shape=jax.ShapeDtypeStruct((depth, rows, n), f32),
        compiler_params=_cparams("arbitrary", "arbitrary"),
        name="adaln",
    )(c_all, w_ada, b_ada.reshape(depth, 1, n))


def _modulated(x, g, sh, sc):
    ms = jnp.mean(x * x, axis=-1, keepdims=True)
    y = x * lax.rsqrt(ms + EPS) * g
    return y * (1.0 + sc) + sh


def _modmm_kernel(x_ref, g_ref, sh_ref, sc_ref, w_ref, o_ref, xn_ref):
    @pl.when(pl.program_id(1) == 0)
    def _():
        xn_ref[...] = _modulated(x_ref[...], g_ref[...], sh_ref[0], sc_ref[0]).astype(bf16)

    o_ref[...] = _dot(xn_ref[...], w_ref[...])


def _modmm_gate_kernel(x_ref, g_ref, sh_ref, sc_ref, w_ref, wg_ref, o_ref, og_ref, xn_ref):
    @pl.when(pl.program_id(1) == 0)
    def _():
        xn = _modulated(x_ref[...], g_ref[...], sh_ref[0], sc_ref[0]).astype(bf16)
        xn_ref[...] = xn
        og_ref[...] = _dot(xn, wg_ref[...])

    o_ref[...] = _dot(xn_ref[...], w_ref[...])


def _mod_specs(mod, m, tm, k, rows_per_batch):
    if rows_per_batch >= tm:
        assert rows_per_batch % tm == 0
        per = rows_per_batch // tm
        return mod.reshape(-1, 1, k), pl.BlockSpec((1, 1, k), lambda i, j: (i // per, 0, 0))
    assert rows_per_batch == 1
    rows = mod
    if rows.shape[0] < m:
        rows = jnp.pad(rows, ((0, m - rows.shape[0]), (0, 0)))
    return rows.reshape(1, m, k), pl.BlockSpec((1, tm, k), lambda i, j: (0, i, 0))


def mod_matmul(x, g, shift, scale, w, n_out, rows_per_batch, tm, tn, w_gate=None, name="modmm"):
    m, k = x.shape
    sh, sh_spec = _mod_specs(shift, m, tm, k, rows_per_batch)
    sc, sc_spec = _mod_specs(scale, m, tm, k, rows_per_batch)
    in_specs = [
        pl.BlockSpec((tm, k), lambda i, j: (i, 0)),
        pl.BlockSpec((1, k), lambda i, j: (0, 0)),
        sh_spec,
        sc_spec,
        pl.BlockSpec((k, tn), lambda i, j: (0, j)),
    ]
    args = [x, g.reshape(1, k), sh, sc, w]
    out_specs = pl.BlockSpec((tm, tn), lambda i, j: (i, j))
    out_shape = jax.ShapeDtypeStruct((m, n_out), f32)
    kern = _modmm_kernel
    if w_gate is not None:
        in_specs.append(pl.BlockSpec((k, LANES), lambda i, j: (0, 0)))
        args.append(w_gate)
        out_specs = [out_specs, pl.BlockSpec((tm, LANES), lambda i, j: (i, 0))]
        out_shape = [out_shape, jax.ShapeDtypeStruct((m, LANES), f32)]
        kern = _modmm_gate_kernel
    return pl.pallas_call(
        kern,
        grid=(m // tm, n_out // tn),
        in_specs=in_specs,
        out_specs=out_specs,
        out_shape=out_shape,
        scratch_shapes=[pltpu.VMEM((tm, k), bf16)],
        compiler_params=_cparams("arbitrary", "arbitrary"),
        name=name,
    )(*args)


def _outproj_kernel(a1_ref, a2_ref, w1_ref, w2_ref, x_ref, gt_ref, o_ref):
    y = _dot(a1_ref[...].astype(bf16), w1_ref[...]) + _dot(a2_ref[...].astype(bf16), w2_ref[...])
    o_ref[...] = x_ref[...] + gt_ref[0] * y


def out_proj(a1, a2, w, x, gate, rows_per_batch, tm, tn, name="outproj"):
    m, k1 = a1.shape
    k2 = a2.shape[1]
    n = w.shape[1]
    gt, gt_spec = _mod_specs(gate, m, tm, n, rows_per_batch)
    if gt.shape[1] == 1:
        per = rows_per_batch // tm
        gt_spec = pl.BlockSpec((1, 1, tn), lambda i, j: (i // per, 0, j))
    else:
        gt_spec = pl.BlockSpec((1, tm, tn), lambda i, j: (0, i, j))
    return pl.pallas_call(
        _outproj_kernel,
        grid=(m // tm, n // tn),
        in_specs=[
            pl.BlockSpec((tm, k1), lambda i, j: (i, 0)),
            pl.BlockSpec((tm, k2), lambda i, j: (i, 0)),
            pl.BlockSpec((k1, tn), lambda i, j: (0, j)),
            pl.BlockSpec((k2, tn), lambda i, j: (k1 // k2, j)),
            pl.BlockSpec((tm, tn), lambda i, j: (i, j)),
            gt_spec,
        ],
        out_specs=pl.BlockSpec((tm, tn), lambda i, j: (i, j)),
        out_shape=jax.ShapeDtypeStruct((m, n), f32),
        compiler_params=_cparams("arbitrary", "arbitrary"),
        name=name,
    )(a1, a2, w, w, x, gt)


def _rglru_gates(xc, wr_ref, br, wi_ref, bi, lam):
    xb = xc.astype(bf16)
    rs, gs = [], []
    for hb in range(RG_BLOCKS):
        sl = slice(hb * LANES, (hb + 1) * LANES)
        rs.append(_dot(xb[:, sl], wr_ref[hb]))
        gs.append(_dot(xb[:, sl], wi_ref[hb]))
    r = _sigmoid(jnp.concatenate(rs, axis=1) + br)
    ig = _sigmoid(jnp.concatenate(gs, axis=1) + bi)
    softplus_neg_lam = jnp.maximum(-lam, 0.0) + jnp.log1p(jnp.exp(-jnp.abs(lam)))
    log_a = -RG_C * r * softplus_neg_lam
    a = jnp.exp(log_a)
    u = jnp.sqrt(-jnp.tanh(log_a) * (a * a + 1.0)) * ig * xc
    return a, u


def _rglru_kernel(xa_ref, ga_ref, cw_ref, cb_ref, wr_ref, br_ref, wi_ref, bi_ref, lam_ref,
                  ya_ref, hl_ref, xprev_ref, hc_ref):
    t_idx = pl.program_id(1)
    tt = xa_ref.shape[0]

    @pl.when(t_idx == 0)
    def _():
        xprev_ref[...] = jnp.zeros_like(xprev_ref)
        hc_ref[...] = jnp.zeros_like(hc_ref)

    xa = xa_ref[...]
    xprev = xprev_ref[...]
    row8 = _iota((8, RG_WIDTH), 0)
    xc = cb_ref[...] + cw_ref[CONV_W - 1:CONV_W, :] * xa
    for k in range(1, CONV_W):
        rolled = pltpu.roll(xa, k, 0)
        head = jnp.where(row8 < k, pltpu.roll(xprev, k, 0), rolled[0:8])
        shifted = jnp.concatenate([head, rolled[8:]], axis=0)
        xc = xc + cw_ref[CONV_W - 1 - k:CONV_W - k, :] * shifted
    xprev_ref[...] = xa[tt - 8:tt]

    a, u = _rglru_gates(xc, wr_ref, br_ref[...], wi_ref, bi_ref[...], lam_ref[...])
    row = _iota((tt, RG_WIDTH), 0)
    s = 1
    while s < tt:
        a_sh = pltpu.roll(a, s, 0)
        u_sh = pltpu.roll(u, s, 0)
        ok = row >= s
        u = jnp.where(ok, a * u_sh + u, u)
        a = jnp.where(ok, a * a_sh, a)
        s *= 2
    h = a * hc_ref[...] + u
    hc_ref[...] = h[tt - 1:tt]
    hl_ref[0] = h[tt - 1:tt]
    ya_ref[...] = h * _gelu(ga_ref[...])


def rglru_prompt(proj, bsz, seq, cw, cb, wr, br, wi, bi, lam, tt=256):
    nt = seq // tt
    vec = lambda: pl.BlockSpec((1, RG_WIDTH), lambda b, t: (0, 0))
    return pl.pallas_call(
        _rglru_kernel,
        grid=(bsz, nt),
        in_specs=[
            pl.BlockSpec((tt, RG_WIDTH), lambda b, t: (b * nt + t, 0)),
            pl.BlockSpec((tt, RG_WIDTH), lambda b, t: (b * nt + t, 1)),
            pl.BlockSpec((CONV_W, RG_WIDTH), lambda b, t: (0, 0)),
            vec(),
            pl.BlockSpec((RG_BLOCKS, LANES, LANES), lambda b, t: (0, 0, 0)),
            vec(),
            pl.BlockSpec((RG_BLOCKS, LANES, LANES), lambda b, t: (0, 0, 0)),
            vec(),
            vec(),
        ],
        out_specs=[
            pl.BlockSpec((tt, RG_WIDTH), lambda b, t: (b * nt + t, 0)),
            pl.BlockSpec((1, 1, RG_WIDTH), lambda b, t: (b, 0, 0)),
        ],
        out_shape=[
            jax.ShapeDtypeStruct((bsz * seq, RG_WIDTH), f32),
            jax.ShapeDtypeStruct((bsz, 1, RG_WIDTH), f32),
        ],
        scratch_shapes=[pltpu.VMEM((8, RG_WIDTH), f32), pltpu.VMEM((1, RG_WIDTH), f32)],
        compiler_params=_cparams("arbitrary", "arbitrary"),
        name="rglru_prompt",
    )(proj, proj, cw, cb, wr, br, wi, bi, lam)


def _rglru_step_kernel(xa_ref, ga_ref, b0_ref, b1_ref, b2_ref, h0_ref, cw_ref, cb_ref,
                       wr_ref, br_ref, wi_ref, bi_ref, lam_ref, ya_ref, h_ref):
    xa = xa_ref[...]
    xc = (cb_ref[...] + cw_ref[0:1, :] * b0_ref[...] + cw_ref[1:2, :] * b1_ref[...]
          + cw_ref[2:3, :] * b2_ref[...] + cw_ref[3:4, :] * xa)
    a, u = _rglru_gates(xc, wr_ref, br_ref[...], wi_ref, bi_ref[...], lam_ref[...])
    h = a * h0_ref[...] + u
    h_ref[...] = h
    ya_ref[...] = h * _gelu(ga_ref[...])


def rglru_step(proj_s, conv_state, h0, cw, cb, wr, br, wi, bi, lam):
    n = proj_s.shape[0]
    full = lambda shape: pl.BlockSpec(shape, lambda i: tuple(0 for _ in shape))
    return pl.pallas_call(
        _rglru_step_kernel,
        grid=(1,),
        in_specs=[
            pl.BlockSpec((n, RG_WIDTH), lambda i: (0, 0)),
            pl.BlockSpec((n, RG_WIDTH), lambda i: (0, 1)),
            full((n, RG_WIDTH)), full((n, RG_WIDTH)), full((n, RG_WIDTH)), full((n, RG_WIDTH)),
            full((CONV_W, RG_WIDTH)), full((1, RG_WIDTH)),
            full((RG_BLOCKS, LANES, LANES)), full((1, RG_WIDTH)),
            full((RG_BLOCKS, LANES, LANES)), full((1, RG_WIDTH)), full((1, RG_WIDTH)),
        ],
        out_specs=[full((n, RG_WIDTH)), full((n, RG_WIDTH))],
        out_shape=[jax.ShapeDtypeStruct((n, RG_WIDTH), f32)] * 2,
        compiler_params=_cparams("arbitrary"),
        name="rglru_step",
    )(proj_s, proj_s, conv_state[:, 0], conv_state[:, 1], conv_state[:, 2], h0,
      cw, cb, wr, br, wi, bi, lam)


def _rope_tables(pos):
    half = ROT_DIMS // 2
    inv = ROPE_THETA ** (-jnp.arange(half, dtype=f32) / half)
    ang = pos.astype(f32)[:, None] * inv[None, :]
    cos, sin = jnp.cos(ang), jnp.sin(ang)
    n = pos.shape[0]
    ctab = jnp.concatenate([cos, cos, jnp.ones((n, DH - ROT_DIMS), f32)], axis=1)
    stab = jnp.concatenate([-sin, sin, jnp.zeros((n, DH - ROT_DIMS), f32)], axis=1)
    return ctab, stab


def _rope_head(xh, ctab, stab, lane):
    half = ROT_DIMS // 2
    partner = jnp.where(lane < half, pltpu.roll(xh, DH - half, 1), pltpu.roll(xh, half, 1))
    return xh * ctab + partner * stab


def _rope_kernel(q_ref, k_ref, v_ref, c_ref, s_ref, qo_ref, ko_ref, vo_ref):
    ctab, stab = c_ref[...], s_ref[...]
    lane = _iota((q_ref.shape[0], DH), 1)
    for hb in range(HEADS):
        sl = slice(hb * DH, (hb + 1) * DH)
        qo_ref[:, hb, :] = _rope_head(q_ref[:, sl], ctab, stab, lane) * (DH ** -0.5)
        ko_ref[:, hb, :] = _rope_head(k_ref[:, sl], ctab, stab, lane)
        vo_ref[:, hb, :] = v_ref[:, sl]


def rope_qkv(proj, ctab, stab, rows_per_seq, tt):
    m = proj.shape[0]
    nt = max(rows_per_seq // tt, 1)
    if ctab.shape[0] == 1:
        tab_spec = pl.BlockSpec((1, DH), lambda i: (0, 0))
    else:
        tab_spec = pl.BlockSpec((tt, DH), lambda i: (i % nt, 0))
    return pl.pallas_call(
        _rope_kernel,
        grid=(m // tt,),
        in_specs=[
            pl.BlockSpec((tt, ATT_W), lambda i: (i, 2)),
            pl.BlockSpec((tt, ATT_W), lambda i: (i, 3)),
            pl.BlockSpec((tt, ATT_W), lambda i: (i, 4)),
            tab_spec, tab_spec,
        ],
        out_specs=[pl.BlockSpec((tt, HEADS, DH), lambda i: (i, 0, 0))] * 3,
        out_shape=[jax.ShapeDtypeStruct((m, HEADS, DH), f32)] * 3,
        compiler_params=_cparams("arbitrary"),
        name="rope",
    )(proj, proj, proj, ctab, stab)


def _rope_split_kernel(q_ref, k_ref, v_ref, c_ref, s_ref, *refs):
    n_pat = len(DIL_PATTERNS)
    outs, (kf_ref, qs_ref, ks_ref) = refs[:3 * n_pat], refs[3 * n_pat:]
    tt = q_ref.shape[0]
    ctab, stab = c_ref[...], s_ref[...]
    lane = _iota((tt, DH), 1)
    qs_ref[...] = _rope_head(q_ref[...], ctab, stab, lane) * (DH ** -0.5)
    k_rot = _rope_head(k_ref[...], ctab, stab, lane)
    ks_ref[...] = k_rot
    kf_ref[...] = k_rot
    for gi, (_, d) in enumerate(DIL_PATTERNS):
        rows = tt // d
        for src, dst in ((qs_ref, outs[3 * gi]), (ks_ref, outs[3 * gi + 1]), (v_ref, outs[3 * gi + 2])):
            for r in range(d):
                dst[0, 0, r] = src[pl.ds(r, rows, stride=d), :].astype(bf16)


def rope_split(proj, ctab, stab, bsz, seq, tt=1024):
    m = proj.shape[0]
    nt = seq // tt
    col = lambda c: pl.BlockSpec((tt, DH), lambda i, h, c=c: (i, c * HEADS + h))
    tab = pl.BlockSpec((tt, DH), lambda i, h: (i % nt, 0))
    out_specs, out_shape = [], []
    for (_, d) in DIL_PATTERNS:
        for _ in range(3):
            out_specs.append(pl.BlockSpec((1, 1, d, tt // d, DH), lambda i, h: (i // nt, h, 0, i % nt, 0)))
            out_shape.append(jax.ShapeDtypeStruct((bsz, HEADS, d, seq // d, DH), bf16))
    out_specs.append(pl.BlockSpec((tt, DH), lambda i, h: (i, h)))
    out_shape.append(jax.ShapeDtypeStruct((m, ATT_W), f32))
    res = pl.pallas_call(
        _rope_split_kernel,
        grid=(m // tt, HEADS),
        in_specs=[col(2), col(3), col(4), tab, tab],
        out_specs=out_specs,
        out_shape=out_shape,
        scratch_shapes=[pltpu.VMEM((tt, DH), f32), pltpu.VMEM((tt, DH), f32)],
        compiler_params=_cparams("arbitrary", "arbitrary"),
        name="rope_split",
    )(proj, proj, proj, ctab, stab)
    return [res[3 * gi:3 * gi + 3] for gi in range(len(DIL_PATTERNS))], res[-1]


def _dil_kernel(q_ref, kc_ref, kp_ref, vc_ref, vp_ref, o_ref, l_ref, s_ref, p_ref, *, span):
    tq = q_ref.shape[3]
    blk = pl.program_id(2)
    qi = _iota((tq, 2 * tq), 0)
    col = _iota((tq, 2 * tq), 1)
    rel = jnp.where(col < tq, qi - col, qi - col + 2 * tq)
    ok = (rel >= 0) & (rel <= span) & ((col < tq) | (blk > 0))
    for hb in range(HEADS):
        qh = q_ref[0, hb, 0]
        s_ref[hb, :, :tq] = _dot_nt(qh, kc_ref[0, hb, 0])
        s_ref[hb, :, tq:] = _dot_nt(qh, kp_ref[0, hb, 0])
    for hb in range(HEADS):
        s = jnp.where(ok, s_ref[hb], NEG_INF)
        mx = jnp.max(s, axis=1, keepdims=True)
        p = jnp.exp(s - mx)
        den = jnp.sum(p, axis=1, keepdims=True)
        p_ref[hb] = (p * (1.0 / den)).astype(bf16)
        l_ref[0, hb, 0] = jnp.broadcast_to(mx + jnp.log(den), (tq, DH))
    for hb in range(HEADS):
        o_ref[0, hb, 0] = (_dot(p_ref[hb, :, :tq], vc_ref[0, hb, 0])
                           + _dot(p_ref[hb, :, tq:], vp_ref[0, hb, 0]))


def _dil_merge_kernel(*refs):
    n_pat = len(DIL_PATTERNS)
    ins, o_ref, scr = refs[:2 * n_pat], refs[2 * n_pat], refs[2 * n_pat + 1:]
    tt = o_ref.shape[0]
    for gi, (_, d) in enumerate(DIL_PATTERNS):
        rows = tt // d
        for src, dst in ((ins[2 * gi], scr[2 * gi]), (ins[2 * gi + 1], scr[2 * gi + 1])):
            for r in range(d):
                dst[pl.ds(r, rows, stride=d), :] = src[0, 0, r]
    lses = [scr[2 * gi + 1][...] for gi in range(n_pat)]
    top = functools.reduce(jnp.maximum, lses)
    es = [jnp.exp(l - top) for l in lses]
    num = sum(scr[2 * gi][...] * es[gi] for gi in range(n_pat))
    o_ref[...] = num / sum(es)


def dilated_prompt(qkv_by_pattern, bsz, seq, tq=256, tt=1024):
    partial = []
    for (w, d), (qd, kd, vd) in zip(DIL_PATTERNS, qkv_by_pattern):
        sd = seq // d
        blk = (1, HEADS, 1, tq, DH)
        cur = pl.BlockSpec(blk, lambda b, r, i: (b, 0, r, i, 0))
        prev = pl.BlockSpec(blk, lambda b, r, i: (b, 0, r, jnp.maximum(i - 1, 0), 0))
        partial += pl.pallas_call(
            functools.partial(_dil_kernel, span=w // d),
            grid=(bsz, d, sd // tq),
            in_specs=[cur, cur, prev, cur, prev],
            out_specs=[cur, cur],
            out_shape=[jax.ShapeDtypeStruct((bsz, HEADS, d, sd, DH), f32)] * 2,
            scratch_shapes=[pltpu.VMEM((HEADS, tq, 2 * tq), f32), pltpu.VMEM((HEADS, tq, 2 * tq), bf16)],
            compiler_params=_cparams("arbitrary", "arbitrary", "arbitrary"),
            name=f"dilattn_d{d}",
        )(qd, kd, kd, vd, vd)
    nt = seq // tt
    in_specs = []
    for (_, d) in DIL_PATTERNS:
        in_specs += [pl.BlockSpec((1, 1, d, tt // d, DH), lambda i, h: (i // nt, h, 0, i % nt, 0))] * 2
    return pl.pallas_call(
        _dil_merge_kernel,
        grid=(bsz * nt, HEADS),
        in_specs=in_specs,
        out_specs=pl.BlockSpec((tt, DH), lambda i, h: (i, h)),
        out_shape=jax.ShapeDtypeStruct((bsz * seq, ATT_W), f32),
        scratch_shapes=[pltpu.VMEM((tt, DH), f32)] * (2 * len(DIL_PATTERNS)),
        compiler_params=_cparams("arbitrary", "arbitrary"),
        name="dilattn_merge",
    )(*partial)


def _dil_step_kernel(q_ref, kn_ref, vn_ref, k1_ref, k4_ref, k16_ref, v1_ref, v4_ref, v16_ref, o_ref):
    q = q_ref[0]
    kn, vn = kn_ref[0], vn_ref[0]
    s_self = _rowsum_bcast(q * kn)
    o_gs, lse_gs = [], []
    for k_ref, v_ref in ((k1_ref, v1_ref), (k4_ref, v4_ref), (k16_ref, v16_ref)):
        k3 = k_ref[0, :, 0]
        nk = k3.shape[0]
        s = _rowsum_bcast((k3 * q[None]).reshape(nk * HEADS, DH)).reshape(nk, HEADS, DH)
        mx = jnp.maximum(jnp.max(s, axis=0), s_self)
        p = jnp.exp(s - mx[None])
        p_self = jnp.exp(s_self - mx)
        den = jnp.sum(p, axis=0) + p_self
        num = jnp.sum(p * v_ref[0, :, 0], axis=0) + p_self * vn
        o_gs.append(num / den)
        lse_gs.append(mx + jnp.log(den))
    top = jnp.maximum(jnp.maximum(lse_gs[0], lse_gs[1]), lse_gs[2])
    es = [jnp.exp(l - top) for l in lse_gs]
    tot = es[0] + es[1] + es[2]
    o_ref[0] = (o_gs[0] * es[0] + o_gs[1] * es[1] + o_gs[2] * es[2]) / tot


def dilated_step(q4, k4, v4, cache_k, cache_v):
    n, win = cache_k.shape[0], cache_k.shape[1]
    one = pl.BlockSpec((1, HEADS, DH), lambda b: (b, 0, 0))
    args = [q4, k4, v4]
    in_specs = [one, one, one]
    for cache in (cache_k, cache_v):
        for (w, d) in DIL_PATTERNS:
            nkeys = w // d
            assert win % d == 0 and (win // d) % nkeys == 0 and (win - w) % (d * nkeys) == 0
            args.append(cache.reshape(n, win // d, d, HEADS, DH))
            in_specs.append(pl.BlockSpec((1, nkeys, 1, HEADS, DH),
                                         lambda b, blk=(win - w) // d // nkeys: (b, blk, 0, 0, 0)))
    return pl.pallas_call(
        _dil_step_kernel,
        grid=(n,),
        in_specs=in_specs,
        out_specs=one,
        out_shape=jax.ShapeDtypeStruct((n, HEADS, DH), f32),
        compiler_params=_cparams("arbitrary"),
        name="dilattn_step",
    )(*args).reshape(n, ATT_W)


def _sb_kernel(bias_ref, q_ref, k_ref, v_ref, o_ref, kb_ref, vb_ref, qs_ref, t_ref, spb_ref, wb_ref,
               acc_ref, run_ref):
    tq = q_ref.shape[1]
    nh = q_ref.shape[2] // DH
    kt_w = t_ref.shape[2]
    sub = PAGE
    hg = pl.program_id(1)
    qb = pl.program_id(2)

    @pl.when(qb == 0)
    def _():
        kb_ref[...] = k_ref[0].astype(bf16)
        vb_ref[...] = v_ref[0].astype(bf16)

    rr = _iota((sub, 2 * sub), 0)
    cc = _iota((sub, 2 * sub), 1)
    tri = jnp.where((rr > cc) | (cc >= sub), 1.0, 0.0).astype(bf16)
    for h in range(nh):
        qs_ref[h] = (q_ref[0, :, h * DH:(h + 1) * DH] * (DH ** -0.5)).astype(bf16)
    acc_ref[...] = jnp.zeros_like(acc_ref)
    run_ref[...] = jnp.zeros_like(run_ref)

    def macro(start, width, masked):
        start = pl.multiple_of(start, width)
        for h in range(nh):
            hs = slice(h * DH, (h + 1) * DH)
            z = _dot_nt(qs_ref[h], kb_ref[pl.ds(start, width), hs]) + bias_ref[hg * nh + h]
            sp = _softplus(z)
            t = z - sp
            if masked:
                ok = start + _iota((tq, width), 1) < qb * tq + _iota((tq, width), 0)
                sp = jnp.where(ok, sp, 0.0)
                t = jnp.where(ok, t, NEG_INF)
            t_ref[h, :, :width] = t
            spb_ref[h, :, :width] = sp.astype(bf16)
        for h in range(nh):
            run = run_ref[h]
            for kt in range(width // sub - 1, -1, -1):
                ks = slice(kt * sub, (kt + 1) * sub)
                cs = _dot(spb_ref[h, :, ks], tri)
                wb_ref[h, :, ks] = jnp.exp(t_ref[h, :, ks] - cs[:, :sub] - run).astype(bf16)
                run = run + cs[:, sub:]
            run_ref[h] = run
        for h in range(nh):
            hs = slice(h * DH, (h + 1) * DH)
            acc_ref[h] += _dot(wb_ref[h, :, :width], vb_ref[pl.ds(start, width), hs])

    macro(qb * tq, tq, True)
    n_part = (qb * tq % kt_w) // tq
    for j in range(1, kt_w // tq):
        @pl.when(n_part >= j)
        def _():
            macro((qb - j) * tq, tq, False)

    top = (qb * tq) // kt_w

    def body(it, _):
        macro((top - 1 - it) * kt_w, kt_w, False)
        return 0

    lax.fori_loop(0, top, body, 0)
    for h in range(nh):
        o_ref[0, :, h * DH:(h + 1) * DH] = acc_ref[h]


def sb_prompt(proj2, bias, bsz, seq, tq=256, nh=4, kt_w=512):
    assert kt_w % tq == 0 and seq % kt_w == 0
    p3 = proj2.reshape(bsz, seq, proj2.shape[1])
    nq = seq // tq
    ng = HEADS // nh
    wd = nh * DH
    return pl.pallas_call(
        _sb_kernel,
        grid=(bsz, ng, nq),
        in_specs=[
            pl.BlockSpec(memory_space=pltpu.SMEM),
            pl.BlockSpec((1, tq, wd), lambda b, g, i: (b, i, g)),
            pl.BlockSpec((1, seq, wd), lambda b, g, i: (b, 0, ng + g)),
            pl.BlockSpec((1, seq, wd), lambda b, g, i: (b, 0, 2 * ng + g)),
        ],
        out_specs=pl.BlockSpec((1, tq, wd), lambda b, g, i: (b, i, g)),
        out_shape=jax.ShapeDtypeStruct((bsz, seq, ATT_W), f32),
        scratch_shapes=[
            pltpu.VMEM((seq, wd), bf16), pltpu.VMEM((seq, wd), bf16),
            pltpu.VMEM((nh, tq, DH), bf16),
            pltpu.VMEM((nh, tq, kt_w), f32), pltpu.VMEM((nh, tq, kt_w), bf16),
            pltpu.VMEM((nh, tq, kt_w), bf16),
            pltpu.VMEM((nh, tq, DH), f32), pltpu.VMEM((nh, tq, PAGE), f32),
        ],
        compiler_params=_cparams("arbitrary", "arbitrary", "arbitrary"),
        name="sb_prompt",
    )(bias, p3, p3, p3).reshape(bsz * seq, ATT_W)


def _sb_step_kernel(pt_ref, q_ref, bias_ref, eye_ref, *refs, npg):
    k_refs, v_refs = refs[:npg], refs[npg:2 * npg]
    o_ref, acc_ref, carry_ref = refs[2 * npg:]
    j = pl.program_id(1)

    @pl.when(j == 0)
    def _():
        acc_ref[...] = jnp.zeros_like(acc_ref)
        carry_ref[...] = jnp.zeros_like(carry_ref)

    q = q_ref[0]
    bias = bias_ref[...]
    ones = jnp.ones((DH, DH), bf16)
    rr = _iota((PAGE, 2 * PAGE), 0)
    cc = _iota((PAGE, 2 * PAGE), 1)
    tri = jnp.where((rr > cc) | (cc >= PAGE), 1.0, 0.0).astype(bf16)
    acc = acc_ref[...]
    run = carry_ref[...]
    pages = range(npg)
    zbs = [_dot((k_refs[p][0] * q[None]).reshape(PAGE * HEADS, DH).astype(bf16), ones)
           .reshape(PAGE, HEADS, DH) for p in pages]
    zs = [jnp.sum(zb * eye_ref[...], axis=0) + bias for zb in zbs]
    sps = [_softplus(z) for z in zs]
    css = []
    for sp in sps:
        hi, mid, lo = _split3(sp)
        css.append(_dot(hi, tri) + _dot(mid, tri) + _dot(lo, tri))
    ws = []
    for z, sp, cs in zip(zs, sps, css):
        ws.append(jnp.exp(z - sp - cs[:, :PAGE] - run))
        run = run + cs[:, PAGE:]
    wbs = [_dot((eye_ref[...] * w[None]).reshape(PAGE * HEADS, DH).astype(bf16), ones)
           .reshape(PAGE, HEADS, DH) for w in ws]
    for p, wb in zip(pages, wbs):
        acc = acc + jnp.sum(wb * v_refs[p][0], axis=0)
    acc_ref[...] = acc
    carry_ref[...] = run

    @pl.when(j == pl.num_programs(1) - 1)
    def _():
        o_ref[0] = acc


def sb_step(q_s, bias, cache_k, cache_v, page_table, npg=16):
    n, n_pages = page_table.shape
    q4 = (q_s * (DH ** -0.5)).reshape(n, HEADS, DH)
    bias4 = jnp.broadcast_to(bias[:, None], (HEADS, DH))
    eye3 = jnp.broadcast_to(jnp.eye(PAGE, DH, dtype=f32)[:, None, :], (PAGE, HEADS, DH))

    def page(p):
        return lambda b, j, pt: (pt[b, n_pages - 1 - (j * npg + p)], 0, 0, 0)

    kv_specs = [pl.BlockSpec((1, PAGE, HEADS, DH), page(p)) for p in range(npg)]
    grid_spec = pltpu.PrefetchScalarGridSpec(
        num_scalar_prefetch=1,
        grid=(n, n_pages // npg),
        in_specs=[
            pl.BlockSpec((1, HEADS, DH), lambda b, j, pt: (b, 0, 0)),
            pl.BlockSpec((HEADS, DH), lambda b, j, pt: (0, 0)),
            pl.BlockSpec((PAGE, HEADS, DH), lambda b, j, pt: (0, 0, 0)),
        ] + kv_specs + kv_specs,
        out_specs=pl.BlockSpec((1, HEADS, DH), lambda b, j, pt: (b, 0, 0)),
        scratch_shapes=[pltpu.VMEM((HEADS, DH), f32), pltpu.VMEM((HEADS, DH), f32)],
    )
    return pl.pallas_call(
        functools.partial(_sb_step_kernel, npg=npg),
        grid_spec=grid_spec,
        out_shape=jax.ShapeDtypeStruct((n, HEADS, DH), f32),
        compiler_params=_cparams("arbitrary", "arbitrary"),
        name="sb_step",
    )(page_table, q4, bias4, eye3, *([cache_k] * npg), *([cache_v] * npg)).reshape(n, ATT_W)


def _mlstm_kernel(q_ref, k_ref, v_ref, og_ref, gate_ref, gb_ref, gn_ref,
                  h_ref, c_out, n_out, m_out, c_s, n_s, m_s):
    ci = pl.program_id(1)
    L = CHUNK

    @pl.when(ci == 0)
    def _():
        c_s[...] = jnp.zeros_like(c_s)
        n_s[...] = jnp.zeros_like(n_s)
        m_s[...] = jnp.zeros_like(m_s)

    gt = gate_ref[...] + gb_ref[...]
    gt_t = gt.T
    ri = _iota((L, L), 0)
    li = _iota((L, L), 1)
    causal = li <= ri
    tri_incl = jnp.where(causal, 1.0, 0.0).astype(bf16)
    tri_incl_t = jnp.where(ri <= li, 1.0, 0.0).astype(bf16)
    heads = range(M_HEADS)
    sls = [slice(h * M_DH, (h + 1) * M_DH) for h in heads]
    ig_col = [gt[:, h:h + 1] for h in heads]
    ig_row = [gt_t[h:h + 1, :] for h in heads]
    bcum_col, bcum_row = [], []
    for h in heads:
        lf_col = _log_sigmoid_pair(gt[:, M_HEADS + h:M_HEADS + h + 1])[0]
        lf_row = _log_sigmoid_pair(gt_t[M_HEADS + h:M_HEADS + h + 1, :])[0]
        c_hi, c_mid, c_lo = _split3(jnp.broadcast_to(lf_col, (L, L)))
        bcum_col.append(_dot(tri_incl, c_hi) + _dot(tri_incl, c_mid) + _dot(tri_incl, c_lo))
        r_hi, r_mid, r_lo = _split3(jnp.broadcast_to(lf_row, (L, L)))
        bcum_row.append(_dot(r_hi, tri_incl_t) + _dot(r_mid, tri_incl_t) + _dot(r_lo, tri_incl_t))
    qb = [q_ref[:, sl].astype(bf16) for sl in sls]
    kb = [(k_ref[:, sl] * (M_DH ** -0.5)).astype(bf16) for sl in sls]
    vb = [v_ref[:, sl].astype(bf16) for sl in sls]
    qk = [_dot_nt(qb[h], kb[h]) for h in heads]
    qc = [_dot_nt(qb[h], c_s[h].astype(bf16)) for h in heads]
    m_prev = [m_s[h:h + 1, :] for h in heads]
    m_t, sw, iw_col = [], [], []
    for h in heads:
        dlog = jnp.where(causal, bcum_col[h] - bcum_row[h] + ig_row[h], NEG_INF)
        inter = bcum_col[h] + m_prev[h]
        mt = jnp.maximum(inter, jnp.max(dlog, axis=1, keepdims=True))
        m_t.append(mt)
        sw.append(jnp.exp(dlog - mt) * qk[h])
        iw_col.append(jnp.exp(inter - mt)[:, 0:1])
    sv = [_dot(sw[h].astype(bf16), vb[h]) for h in heads]
    for h in heads:
        n_prev = n_s[h:h + 1, :]
        num = sv[h] + iw_col[h] * qc[h]
        qn = jnp.sum(qb[h].astype(f32) * n_prev.astype(bf16).astype(f32), axis=1, keepdims=True)
        den = jnp.sum(sw[h], axis=1, keepdims=True) + iw_col[h] * qn
        hout = num / jnp.maximum(jnp.abs(den), jnp.exp(-m_t[h][:, 0:1]))
        hn = hout * lax.rsqrt(jnp.mean(hout * hout, axis=1, keepdims=True) + EPS)
        h_ref[:, sls[h]] = hn * gn_ref[:, sls[h]] * _sigmoid(og_ref[:, sls[h]])
    for h in heads:
        m_last = m_t[h][L - 1:L, :]
        b_last = bcum_col[h][L - 1:L, :]
        wl_col = jnp.exp(b_last[:, 0:1] - bcum_col[h][:, 0:1] + ig_col[h] - m_last[:, 0:1])
        wl_row = jnp.exp(b_last - bcum_row[h][0:1, :] + ig_row[h] - m_last)
        dsc = jnp.exp(b_last + m_prev[h] - m_last)[:, 0:1]
        c_s[h] = dsc * c_s[h] + _dot((v_ref[:, sls[h]] * wl_col).T.astype(bf16), kb[h])
        wl8 = jnp.broadcast_to(wl_row, (8, L)).astype(bf16)
        n_s[h:h + 1, :] = dsc * n_s[h:h + 1, :] + _dot(wl8, kb[h])[0:1, :]
        m_s[h:h + 1, :] = m_last

    @pl.when(ci == pl.num_programs(1) - 1)
    def _():
        c_out[0] = c_s[...]
        n_out[0] = n_s[...]
        m_out[0] = m_s[...]


def mlstm_prompt(proj2, gates, gate_bias, g_mnorm, bsz, seq):
    nc = seq // CHUNK
    col = lambda c: pl.BlockSpec((CHUNK, M_WIDTH), lambda b, i, c=c: (b * nc + i, c))
    hm, c1, n1, m1 = pl.pallas_call(
        _mlstm_kernel,
        grid=(bsz, nc),
        in_specs=[
            col(3), col(4), col(5), col(6),
            pl.BlockSpec((CHUNK, LANES), lambda b, i: (b * nc + i, 0)),
            pl.BlockSpec((1, LANES), lambda b, i: (0, 0)),
            pl.BlockSpec((1, M_WIDTH), lambda b, i: (0, 0)),
        ],
        out_specs=[
            pl.BlockSpec((CHUNK, M_WIDTH), lambda b, i: (b * nc + i, 0)),
            pl.BlockSpec((1, M_HEADS, M_DH, M_DH), lambda b, i: (b, 0, 0, 0)),
            pl.BlockSpec((1, M_HEADS, M_DH), lambda b, i: (b, 0, 0)),
            pl.BlockSpec((1, M_HEADS, LANES), lambda b, i: (b, 0, 0)),
        ],
        out_shape=[
            jax.ShapeDtypeStruct((bsz * seq, M_WIDTH), f32),
            jax.ShapeDtypeStruct((bsz, M_HEADS, M_DH, M_DH), f32),
            jax.ShapeDtypeStruct((bsz, M_HEADS, M_DH), f32),
            jax.ShapeDtypeStruct((bsz, M_HEADS, LANES), f32),
        ],
        scratch_shapes=[
            pltpu.VMEM((M_HEADS, M_DH, M_DH), f32),
            pltpu.VMEM((M_HEADS, M_DH), f32),
            pltpu.VMEM((M_HEADS, LANES), f32),
        ],
        compiler_params=_cparams("arbitrary", "arbitrary"),
        name="mlstm_prompt",
    )(proj2, proj2, proj2, proj2, gates, gate_bias, g_mnorm)
    return hm, c1, n1, m1[:, :, 0]


def _mlstm_step_kernel(q_ref, k_ref, v_ref, og_ref, gn_ref, ig_ref, fg_ref, c_ref, n_ref, m_ref,
                       h_ref, c_out, n_out, m_out):
    q = q_ref[0, 0]
    k = k_ref[0, 0] * (M_DH ** -0.5)
    v = v_ref[0, 0]
    ig = ig_ref[0, 0]
    lf = _log_sigmoid_pair(fg_ref[0, 0])[0]
    m0 = m_ref[0, 0]
    c0 = c_ref[0, 0]
    n0 = n_ref[0, 0]
    inter = lf + m0
    m_t = jnp.maximum(inter, ig)
    dw = jnp.exp(ig - m_t)
    iw = jnp.exp(inter - m_t)
    rnd = lambda a: a.astype(bf16).astype(f32)
    qr, kr, vr = rnd(q), rnd(k), rnd(v)
    qk = jnp.sum(qr * kr, axis=1, keepdims=True)
    sw = dw * qk
    cq = jnp.sum(rnd(c0) * qr, axis=1, keepdims=True)
    num = rnd(sw) * vr + iw * cq
    den = sw + iw * jnp.sum(rnd(n0) * qr, axis=1, keepdims=True)
    hout = num / jnp.maximum(jnp.abs(den), jnp.exp(-m_t))
    c_out[0, 0] = iw * c0 + rnd(dw * v) * kr
    n_out[0, 0] = iw * n0 + rnd(dw) * kr
    m_out[0, 0] = m_t
    hn = hout * lax.rsqrt(jnp.mean(hout * hout, axis=0, keepdims=True) + EPS)
    h_ref[0, 0] = hn * gn_ref[0] * _sigmoid(og_ref[0, 0])


def mlstm_step(proj2_s, gates_s, gate_bias, g_mnorm, c0, n0, m0):
    n = proj2_s.shape[0]
    base = 3 * ATT_W
    seg = lambda i: proj2_s[:, base + i * M_WIDTH: base + (i + 1) * M_WIDTH].reshape(n, M_HEADS, M_DH)
    g = gates_s + gate_bias
    rowb = pl.BlockSpec((1, 1, 1, M_DH), lambda b, h: (b, h, 0, 0))
    colb = pl.BlockSpec((1, 1, M_DH, 1), lambda b, h: (b, h, 0, 0))
    scal = pl.BlockSpec((1, 1, 1, 1), lambda b, h: (b, h, 0, 0))
    hcol, c1, n1, m1 = pl.pallas_call(
        _mlstm_step_kernel,
        grid=(n, M_HEADS),
        in_specs=[
            rowb, rowb, colb, colb,
            pl.BlockSpec((1, M_DH, 1), lambda b, h: (h, 0, 0)),
            scal, scal,
            pl.BlockSpec((1, 1, M_DH, M_DH), lambda b, h: (b, h, 0, 0)),
            rowb, scal,
        ],
        out_specs=[colb, pl.BlockSpec((1, 1, M_DH, M_DH), lambda b, h: (b, h, 0, 0)), rowb, scal],
        out_shape=[
            jax.ShapeDtypeStruct((n, M_HEADS, M_DH, 1), f32),
            jax.ShapeDtypeStruct((n, M_HEADS, M_DH, M_DH), f32),
            jax.ShapeDtypeStruct((n, M_HEADS, 1, M_DH), f32),
            jax.ShapeDtypeStruct((n, M_HEADS, 1, 1), f32),
        ],
        compiler_params=_cparams("arbitrary", "arbitrary"),
        name="mlstm_step",
    )(seg(0)[:, :, None, :], seg(1)[:, :, None, :], seg(2)[..., None], seg(3)[..., None],
      g_mnorm.reshape(M_HEADS, M_DH, 1),
      g[:, 0:M_HEADS].reshape(n, M_HEADS, 1, 1), g[:, M_HEADS:2 * M_HEADS].reshape(n, M_HEADS, 1, 1),
      c0, n0[:, :, None, :], m0.reshape(n, M_HEADS, 1, 1))
    return hcol.reshape(n, M_WIDTH), c1, n1.reshape(n, M_HEADS, M_DH), m1.reshape(n, M_HEADS)


def _peerq_kernel(x_ref, g_ref, sh_ref, sc_ref, wt_ref, qt_ref, xmt_ref, xs_ref):
    @pl.when(pl.program_id(1) == 0)
    def _():
        xm = _modulated(x_ref[...], g_ref[...], sh_ref[0], sc_ref[0])
        xt = xm.T.astype(bf16)
        xs_ref[...] = xt
        xmt_ref[...] = xt

    qt_ref[...] = _dot(wt_ref[...], xs_ref[...])


def peer_query(x, g, shift, scale, wq_t, rows_per_batch, tm, tn=512):
    m, k = x.shape
    n = wq_t.shape[0]
    sh, sh_spec = _mod_specs(shift, m, tm, k, rows_per_batch)
    sc, sc_spec = _mod_specs(scale, m, tm, k, rows_per_batch)
    return pl.pallas_call(
        _peerq_kernel,
        grid=(m // tm, n // tn),
        in_specs=[
            pl.BlockSpec((tm, k), lambda i, j: (i, 0)),
            pl.BlockSpec((1, k), lambda i, j: (0, 0)),
            sh_spec, sc_spec,
            pl.BlockSpec((tn, k), lambda i, j: (j, 0)),
        ],
        out_specs=[pl.BlockSpec((tn, tm), lambda i, j: (j, i)),
                   pl.BlockSpec((k, tm), lambda i, j: (0, i))],
        out_shape=[jax.ShapeDtypeStruct((n, m), f32), jax.ShapeDtypeStruct((k, m), bf16)],
        scratch_shapes=[pltpu.VMEM((k, tm), bf16)],
        compiler_params=_cparams("arbitrary", "arbitrary"),
        name="peer_query",
    )(x, g.reshape(1, k), sh, sc, wq_t)


def _cast_t_kernel(w_ref, o_ref):
    o_ref[...] = w_ref[0].T.astype(bf16)


def cast_layer_bf16_t(w, layer, tr=512):
    _, rows, cols = w.shape
    return pl.pallas_call(
        _cast_t_kernel,
        grid=(rows // tr,),
        in_specs=[pl.BlockSpec((1, tr, cols), lambda i: (layer, i, 0))],
        out_specs=pl.BlockSpec((cols, tr), lambda i: (0, i)),
        out_shape=jax.ShapeDtypeStruct((cols, rows), bf16),
        compiler_params=_cparams("arbitrary"),
        name="cast_bf16_t",
    )(w)


def _heads_layout_kernel(*refs):
    n = len(refs) // 2
    for x_ref, o_ref in zip(refs[:n], refs[n:]):
        for hb in range(HEADS):
            o_ref[:, hb, :] = x_ref[:, hb * DH:(hb + 1) * DH]


def heads_layout(srcs, bsz, seq, tail, tt=512):
    nt, first = tail // tt, (seq - tail) // tt
    rows = lambda b, i: b * (seq // tt) + first + i
    return pl.pallas_call(
        _heads_layout_kernel,
        grid=(bsz, nt),
        in_specs=[pl.BlockSpec((tt, ATT_W), lambda b, i, c=c: (rows(b, i), c)) for _, c in srcs],
        out_specs=[pl.BlockSpec((tt, HEADS, DH), lambda b, i: (b * nt + i, 0, 0))] * len(srcs),
        out_shape=[jax.ShapeDtypeStruct((bsz * tail, HEADS, DH), f32)] * len(srcs),
        compiler_params=_cparams("arbitrary", "arbitrary"),
        name="heads_layout",
    )(*[x for x, _ in srcs])


def _cast_kernel(w_ref, o_ref):
    o_ref[...] = w_ref[0].astype(bf16)


def cast_layer_bf16(w, layer, tr=1024):
    _, rows, cols = w.shape
    return pl.pallas_call(
        _cast_kernel,
        grid=(rows // tr,),
        in_specs=[pl.BlockSpec((1, tr, cols), lambda i: (layer, i, 0))],
        out_specs=pl.BlockSpec((tr, cols), lambda i: (i, 0)),
        out_shape=jax.ShapeDtypeStruct((rows, cols), bf16),
        compiler_params=_cparams("arbitrary"),
        name="cast_bf16",
    )(w)


_N_TOP = PEER_TOPK + 1
_CAND_PAIRS = [(a, b) for a in range(_N_TOP) for b in range(_N_TOP) if (a + 1) * (b + 1) <= _N_TOP]
_N_CAND = -(-len(_CAND_PAIRS) // 8) * 8


def _oddeven_mergesort_pairs(n):
    pairs = []

    def merge(lo, cnt, r):
        step = 2 * r
        if step < cnt:
            merge(lo, cnt, step)
            merge(lo + r, cnt, step)
            pairs.extend((i, i + r) for i in range(lo + r, lo + cnt - r, step))
        else:
            pairs.append((lo, lo + r))

    def sort(lo, cnt):
        if cnt > 1:
            sort(lo, cnt // 2)
            sort(lo + cnt // 2, cnt // 2)
            merge(lo, cnt, 1)

    sort(0, n)
    return pairs


_SORT16 = _oddeven_mergesort_pairs(PEER_TOPK)


def _exchange(y, i, j):
    y[i], y[j] = jnp.maximum(y[i], y[j]), jnp.minimum(y[i], y[j])


def _top17_network(s):
    y = [s[v * SUBLANES:(v + 1) * SUBLANES, :] for v in range(PEER_TOPK)]
    for i, j in _SORT16:
        _exchange(y, i, j)
    return _merge_top17(y, (4, 2, 1))


def _bitonic_sort(y):
    dist = len(y) // 2
    while dist:
        for i in range(len(y)):
            if not i & dist:
                _exchange(y, i, i + dist)
        dist //= 2


def _merge_top17(y, shifts):
    n = len(y)
    dropped = None
    for shift in shifts:
        other = [pltpu.roll(y[n - 1 - i], shift, 0) for i in range(n)]
        low = functools.reduce(jnp.maximum, [jnp.minimum(y[i], other[i]) for i in range(n)])
        dropped = low if dropped is None else jnp.maximum(dropped, low)
        y = [jnp.maximum(y[i], other[i]) for i in range(n)]
        _bitonic_sort(y)
    return [v[0:1, :] for v in y] + [jnp.max(dropped, axis=0, keepdims=True)]


_SORT8 = _oddeven_mergesort_pairs(SUBLANES)


def _cand_top17(cand):
    n = SUBLANES
    y = [cand[v * SUBLANES:(v + 1) * SUBLANES, :] for v in range(cand.shape[0] // SUBLANES)]
    y += [jnp.full((SUBLANES, LANES), NEG_INF, f32)] * (n - len(y))
    for i, j in _SORT8:
        _exchange(y, i, j)
    other = [pltpu.roll(y[n - 1 - i], 4, 0) for i in range(n)]
    high = [jnp.maximum(y[i], other[i]) for i in range(n)]
    low = [jnp.minimum(y[i], other[i]) for i in range(n)]
    _bitonic_sort(high)
    _bitonic_sort(low)
    return _merge_top17(high + low, (2, 1))


_ROUTER_UNROLL = 8


def _router_kernel(qt_ref, keys_ref, e1_ref, e2_ref, th_ref, cand_ref):
    k0 = keys_ref[0].astype(bf16)
    k1 = keys_ref[1].astype(bf16)
    cand_ref[...] = jnp.full(cand_ref.shape, NEG_INF, f32)

    def one_head(h, slot):
        r0 = pl.multiple_of(h * 2 * N_KEYS, 2 * N_KEYS)
        s1 = _dot(k0, qt_ref[pl.ds(r0, N_KEYS), :].astype(bf16))
        s2 = _dot(k1, qt_ref[pl.ds(r0 + N_KEYS, N_KEYS), :].astype(bf16))
        top_a = _top17_network(s1)
        top_b = _top17_network(s2)
        for ci, (a, b) in enumerate(_CAND_PAIRS):
            cand_ref[slot, ci:ci + 1, :] = top_a[a] + top_b[b]
        cs = _cand_top17(cand_ref[slot])
        z = jnp.ones_like(cs[0])
        for r in range(1, PEER_TOPK):
            z = z + jnp.exp(cs[r] - cs[0])
        inv_z = 1.0 / z
        mid = 0.5 * (cs[PEER_TOPK - 1] + cs[PEER_TOPK])
        o0 = pl.multiple_of(h * N_KEYS, N_KEYS)
        e1 = jnp.exp(s1 - top_a[0]) * inv_z
        e1_ref[:, h] = e1.reshape(N_KEYS // SUBLANES, SUBLANES, LANES)
        e2_ref[pl.ds(o0, N_KEYS), :] = jnp.exp(s2 - top_b[0])
        th_ref[pl.ds(h, 1), :] = jnp.exp(mid - cs[0]) * inv_z

    def body(hp, _):
        for slot in range(_ROUTER_UNROLL):
            one_head(hp * _ROUTER_UNROLL + slot, slot)
        return 0

    lax.fori_loop(0, PEER_HEADS // _ROUTER_UNROLL, body, 0)


def peer_router(qt, keys):
    n, m = qt.shape
    half = PEER_HEADS * N_KEYS
    return pl.pallas_call(
        _router_kernel,
        grid=(m // LANES,),
        in_specs=[pl.BlockSpec((n, LANES), lambda i: (0, i)),
                  pl.BlockSpec((2, N_KEYS, N_KEYS), lambda i: (0, 0, 0))],
        out_specs=[pl.BlockSpec((N_KEYS // SUBLANES, PEER_HEADS, SUBLANES, LANES), lambda i: (0, 0, 0, i)),
                   pl.BlockSpec((half, LANES), lambda i: (0, i)),
                   pl.BlockSpec((PEER_HEADS, LANES), lambda i: (0, i))],
        out_shape=[jax.ShapeDtypeStruct((N_KEYS // SUBLANES, PEER_HEADS, SUBLANES, m), f32),
                   jax.ShapeDtypeStruct((half, m), f32),
                   jax.ShapeDtypeStruct((PEER_HEADS, m), f32)],
        scratch_shapes=[pltpu.VMEM((_ROUTER_UNROLL, _N_CAND, LANES), f32)],
        compiler_params=_cparams("arbitrary"),
        name="peer_router",
    )(qt, keys)


_G_CHAINS = 3
_G_ROWS = SUBLANES


class _Chains:
    def __init__(self, n, zero_ref):
        self.deps, self.cnt, self.zero_ref = [None] * n, 0, zero_ref

    def take(self):
        return self.deps[self.cnt % len(self.deps)]

    def put(self, g):
        bits = lax.bitcast_convert_type(g[0:1, :], jnp.int32) & self.zero_ref[0:1, :]
        self.deps[self.cnt % len(self.deps)] = lax.bitcast_convert_type(bits, f32)
        self.cnt += 1


def _routing_block(e1_ref, e2_ref, th_ref, ii, ls, chains):
    sub = _G_ROWS
    blk, r = divmod(ii, SUBLANES)
    rows = [e1_ref[blk, h, r:r + 1, ls] for h in range(PEER_HEADS)]
    ths = [th_ref[h:h + 1, ls] for h in range(PEER_HEADS)]
    parts = []
    for gi in range(N_KEYS // sub):
        dep = chains.take()
        g = jnp.zeros((sub, LANES), f32)
        for h in range(PEER_HEADS):
            row = rows[h] if (dep is None or h) else rows[h] + dep
            pr = e2_ref[h * N_KEYS + gi * sub:h * N_KEYS + (gi + 1) * sub, ls] * row
            g = g + jnp.where(pr >= ths[h], pr, 0.0)
        chains.put(g)
        parts.append(g)
    return jnp.concatenate(parts, axis=0)


def _experts_kernel(xmt_ref, e1_ref, e2_ref, th_ref, u_ref, v_ref, x_ref, gt_ref, gf_ref, zero_ref,
                    o_ref, w_ref, *, final_norm):
    c = pl.program_id(1)
    tm, te = w_ref.shape

    @pl.when(c == 0)
    def _():
        o_ref[...] = jnp.zeros_like(o_ref)

    chains = _Chains(_G_CHAINS, zero_ref)
    act = _gelu(_dot(u_ref[...], xmt_ref[...]))
    for ii in range(te // N_KEYS):
        es = slice(ii * N_KEYS, (ii + 1) * N_KEYS)
        for lc in range(tm // LANES):
            ls = slice(lc * LANES, (lc + 1) * LANES)
            g_blk = _routing_block(e1_ref, e2_ref, th_ref, ii, ls, chains)
            w_ref[ls, es] = (g_blk * act[es, ls]).T.astype(bf16)
    o_ref[...] += _dot(w_ref[...], v_ref[...])

    @pl.when(c == pl.num_programs(1) - 1)
    def _():
        y = x_ref[...] + gt_ref[0] * o_ref[...]
        if final_norm:
            ms = jnp.mean(y * y, axis=-1, keepdims=True)
            y = y * lax.rsqrt(ms + EPS) * gf_ref[...]
        o_ref[...] = y


def peer_experts(xmt, e1t, e2t, th, u, v, x, gate, g_final, rows_per_batch, tm, te, final_norm):
    m, d = x.shape
    gt, gt_spec = _mod_specs(gate, m, tm, d, rows_per_batch)
    half = PEER_HEADS * N_KEYS
    return pl.pallas_call(
        functools.partial(_experts_kernel, final_norm=final_norm),
        grid=(m // tm, N_EXPERTS // te),
        in_specs=[
            pl.BlockSpec((d, tm), lambda i, c: (0, i)),
            pl.BlockSpec((te // N_KEYS // SUBLANES, PEER_HEADS, SUBLANES, tm), lambda i, c: (c, 0, 0, i)),
            pl.BlockSpec((half, tm), lambda i, c: (0, i)),
            pl.BlockSpec((PEER_HEADS, tm), lambda i, c: (0, i)),
            pl.BlockSpec((te, d), lambda i, c: (c, 0)),
            pl.BlockSpec((te, d), lambda i, c: (c, 0)),
            pl.BlockSpec((tm, d), lambda i, c: (i, 0)),
            gt_spec,
            pl.BlockSpec((1, d), lambda i, c: (0, 0)),
            pl.BlockSpec((SUBLANES, LANES), lambda i, c: (0, 0)),
        ],
        out_specs=pl.BlockSpec((tm, d), lambda i, c: (i, 0)),
        out_shape=jax.ShapeDtypeStruct((m, d), f32),
        scratch_shapes=[pltpu.VMEM((tm, te), bf16)],
        compiler_params=_cparams("arbitrary", "arbitrary"),
        name="peer_experts",
    )(xmt, e1t, e2t, th, u, v, x, gt, g_final.reshape(1, d), jnp.zeros((SUBLANES, LANES), jnp.int32))


def peer_block(x, g, shift, scale, gate, wq_t, keys, u, v, g_final, rows_per_batch, tm, te, final_norm):
    tm_q = 1024 if x.shape[0] % 1024 == 0 and rows_per_batch % 1024 == 0 else tm
    qt, xmt = peer_query(x, g, shift, scale, wq_t, rows_per_batch, tm_q)
    e1t, e2t, th = peer_router(qt, keys)
    return peer_experts(xmt, e1t, e2t, th, u, v, x, gate, g_final, rows_per_batch, tm, te, final_norm)


def kernel(x_prompt, x_sample, c_prompt, c_sample, state_rglru_conv, state_rglru_h, cache_swa_k, cache_swa_v, cache_sb_k, cache_sb_v, state_mlstm_C, state_mlstm_n, state_mlstm_m, page_table, w_ada, b_ada, g_norm_mix, g_norm_ffn, e_w_in, e_conv_w, e_conv_b, e_w_r, e_b_r, e_w_i, e_b_i, e_lambda, e_w_out, o_w_in, o_b_if, o_sb_bias, o_g_mnorm, o_w_out, peer_w_q, peer_keys, peer_u, peer_v, g_final):
    bp, seq, d = x_prompt.shape
    bs = x_sample.shape[0]
    mp = bp * seq
    pad_s = LANES
    xp = x_prompt.reshape(mp, d)
    xs = x_sample.reshape(bs, d)

    c_rows = 16
    c_all = jnp.concatenate([c_prompt, c_sample, jnp.zeros((c_rows - bp - bs, d), f32)], axis=0)
    mod = adaln_all(c_all, w_ada, b_ada)

    def mods(layer):
        parts = [mod[layer, :, i * d:(i + 1) * d] for i in range(6)]
        return [p[:bp] for p in parts], [p[bp:bp + bs] for p in parts]

    ctab_p, stab_p = _rope_tables(jnp.arange(seq, dtype=jnp.int32))
    ctab_s, stab_s = _rope_tables(jnp.full((1,), PAST_LEN, jnp.int32))

    TM = 512
    m_p, m_s = mods(0)
    w_in = e_w_in[0].astype(bf16)
    w_out = e_w_out[0].astype(bf16)
    cw, cb = e_conv_w[0], e_conv_b[0].reshape(1, -1)
    wr, wi = e_w_r[0].astype(bf16), e_w_i[0].astype(bf16)
    br, bi, lam = e_b_r[0].reshape(1, -1), e_b_i[0].reshape(1, -1), e_lambda[0].reshape(1, -1)

    proj_p = mod_matmul(xp, g_norm_mix[0], m_p[0], m_p[1], w_in, E_IN, seq, 1024, E_IN // 4, name="e_in_p")
    proj_s = mod_matmul(xs, g_norm_mix[0], m_s[0], m_s[1], w_in, E_IN, 1, bs, 512, name="e_in_s")

    ya_p, h_p = rglru_prompt(proj_p, bp, seq, cw, cb, wr, br, wi, bi, lam)
    ya_s, h_s = rglru_step(proj_s, state_rglru_conv[0], state_rglru_h[0], cw, cb, wr, br, wi, bi, lam)
    conv_p = proj_p.reshape(bp, seq, E_IN)[:, seq - (CONV_W - 1):, :RG_WIDTH]
    conv_s = jnp.concatenate([state_rglru_conv[0][:, 1:], proj_s[:, None, :RG_WIDTH]], axis=1)

    qkv_p, k_p = rope_split(proj_p, ctab_p, stab_p, bp, seq)
    q_s, k_s, v_s = rope_qkv(proj_s, ctab_s, stab_s, 1, bs)
    o_p = dilated_prompt(qkv_p, bp, seq)
    win = cache_swa_k.shape[2]
    o_s = dilated_step(q_s, k_s, v_s, cache_swa_k.reshape(-1, win, HEADS, DH)[:bs],
                       cache_swa_v.reshape(-1, win, HEADS, DH)[:bs])
    wl = min(2048, seq)
    swa_k_p, swa_v_p = [a.reshape(bp, wl, HEADS, DH)
                        for a in heads_layout([(k_p, 0), (proj_p, E_IN // ATT_W - 1)], bp, seq, wl)]
    swa_k_s = k_s.reshape(bs, 1, HEADS, DH)
    swa_v_s = v_s.reshape(bs, 1, HEADS, DH)

    xp = out_proj(ya_p, o_p, w_out, xp, m_p[2], seq, 1024, 1024, name="e_out_p")
    xs = out_proj(ya_s, o_s, w_out, xs, m_s[2], 1, bs, 512, name="e_out_s")

    def peer_layer(layer, xp, xs, m_p, m_s, final_norm):
        wq_t = cast_layer_bf16_t(peer_w_q, layer)
        u = cast_layer_bf16(peer_u, layer)
        v = cast_layer_bf16(peer_v, layer)
        xp = peer_block(xp, g_norm_ffn[layer], m_p[3], m_p[4], m_p[5], wq_t, peer_keys[layer], u, v,
                        g_final, seq, TM, 1024, final_norm)
        xs_pad = jnp.pad(xs, ((0, pad_s - bs), (0, 0)))
        xs_new = peer_block(xs_pad, g_norm_ffn[layer], m_s[3], m_s[4], m_s[5], wq_t, peer_keys[layer],
                            u, v, g_final, 1, pad_s, 1024, final_norm)
        return xp, xs_new[:bs]

    xp, xs = peer_layer(0, xp, xs, m_p, m_s, False)

    m_p, m_s = mods(1)
    w_in2 = o_w_in[0].astype(bf16)
    w_gate = jnp.pad(w_in2[:, O_MAIN:], ((0, 0), (0, LANES - 2 * M_HEADS)))
    w_out2 = o_w_out[0].astype(bf16)
    gate_bias = jnp.pad(o_b_if[0].reshape(1, 2 * M_HEADS), ((0, 0), (0, LANES - 2 * M_HEADS)))
    gmn = o_g_mnorm[0].reshape(1, M_WIDTH)

    proj2_p, gates_p = mod_matmul(xp, g_norm_mix[1], m_p[0], m_p[1], w_in2, O_MAIN, seq, 1024, 1024,
                                  w_gate=w_gate, name="o_in_p")
    proj2_s, gates_s = mod_matmul(xs, g_norm_mix[1], m_s[0], m_s[1], w_in2, O_MAIN, 1, bs, 512,
                                  w_gate=w_gate, name="o_in_s")

    oc_p = sb_prompt(proj2_p, o_sb_bias[0], bp, seq)
    oc_s = sb_step(proj2_s[:, :ATT_W], o_sb_bias[0], cache_sb_k.reshape(-1, PAGE, HEADS, DH),
                   cache_sb_v.reshape(-1, PAGE, HEADS, DH), page_table)
    hm_p, mC_p, mn_p, mm_p = mlstm_prompt(proj2_p, gates_p, gate_bias, gmn, bp, seq)
    hm_s, mC_s, mn_s, mm_s = mlstm_step(proj2_s, gates_s, gate_bias, gmn,
                                        state_mlstm_C.reshape(-1, M_HEADS, M_DH, M_DH)[:bs],
                                        state_mlstm_n[0], state_mlstm_m[0])

    n_pg = seq // PAGE
    sb_k_p, sb_v_p = [a.reshape(bp, n_pg, PAGE, HEADS, DH)
                      for a in heads_layout([(proj2_p, 1), (proj2_p, 2)], bp, seq, seq)]
    sb_k_s = proj2_s[:, ATT_W:2 * ATT_W].reshape(bs, 1, HEADS, DH)
    sb_v_s = proj2_s[:, 2 * ATT_W:3 * ATT_W].reshape(bs, 1, HEADS, DH)

    xp = out_proj(oc_p, hm_p, w_out2, xp, m_p[2], seq, 1024, 1024, name="o_out_p")
    xs = out_proj(oc_s, hm_s, w_out2, xs, m_s[2], 1, bs, 512, name="o_out_s")
    xp, xs = peer_layer(1, xp, xs, m_p, m_s, True)

    y_prompt = xp.reshape(bp, seq, d)
    y_sample = xs.reshape(bs, 1, d)
    st = lambda a: a[None]
    return (y_prompt, y_sample, st(conv_p), st(conv_s), st(h_p.reshape(bp, RG_WIDTH)), st(h_s),
            st(swa_k_p), st(swa_k_s), st(swa_v_p), st(swa_v_s),
            st(sb_k_p), st(sb_k_s), st(sb_v_p), st(sb_v_s),
            st(mC_p), st(mC_s), st(mn_p), st(mn_s), st(mm_p), st(mm_s))
```

```python
import functools
import math

import jax
import jax.numpy as jnp
from jax import lax
from jax.experimental import pallas as pl
from jax.experimental.pallas import tpu as pltpu

f32 = jnp.float32
bf16 = jnp.bfloat16

D_MODEL = 2048
PAST_LEN = 16384
PAGE = 128
RG_WIDTH = 1024
RG_BLOCKS = 8
CONV_W = 4
RG_C = 8.0
HEADS = 8
DH = 128
ATT_W = HEADS * DH
DIL_PATTERNS = ((128, 1), (512, 4), (2048, 16))
ROT_DIMS = 32
ROPE_THETA = 500000.0
M_HEADS = 4
M_DH = 256
M_WIDTH = M_HEADS * M_DH
CHUNK = 128
E_IN = 2 * RG_WIDTH + 3 * ATT_W
O_MAIN = 3 * ATT_W + 4 * M_WIDTH
PEER_HEADS = 8
N_KEYS = 128
N_EXPERTS = N_KEYS * N_KEYS
PEER_TOPK = 16
EPS = 1e-6
LANES = 128
SUBLANES = 8
VMEM_LIMIT = 56 * 1024 * 1024
NEG_INF = float("-inf")


def _cparams(*sem):
    return pltpu.CompilerParams(dimension_semantics=sem, vmem_limit_bytes=VMEM_LIMIT)


def _dot(a, b):
    return jnp.dot(a, b, preferred_element_type=f32)


def _dot_nt(a, b):
    return lax.dot_general(a, b, (((1,), (1,)), ((), ())), preferred_element_type=f32)


def _split3(x):
    hi = x.astype(bf16)
    r = x - hi.astype(f32)
    mid = r.astype(bf16)
    lo = (r - mid.astype(f32)).astype(bf16)
    return hi, mid, lo


def _sigmoid(x):
    return 1.0 / (1.0 + jnp.exp(-x))


def _log_sigmoid_pair(z):
    l1p = jnp.log1p(jnp.exp(-jnp.abs(z)))
    return jnp.minimum(z, 0.0) - l1p, -jnp.maximum(z, 0.0) - l1p


def _softplus(z):
    return jnp.maximum(z, 0.0) + jnp.log(1.0 + jnp.exp(-jnp.abs(z)))


def _gelu(x):
    c = math.sqrt(2.0 / math.pi)
    h = 0.5 * x
    return h + h * jnp.tanh(x * (c + (c * 0.044715) * (x * x)))


def _iota(shape, dim):
    return lax.broadcasted_iota(jnp.int32, shape, dim)


def _rowsum_bcast(x):
    ones = jnp.ones((LANES, LANES), bf16)
    hi = x.astype(bf16)
    lo = (x - hi.astype(f32)).astype(bf16)
    return _dot(hi, ones) + _dot(lo, ones)


def _adaln_kernel(c_ref, w_ref, b_ref, o_ref):
    c = c_ref[...]
    s = c * _sigmoid(c)
    w = w_ref[0]
    s_hi = s.astype(bf16)
    s_lo = (s - s_hi.astype(f32)).astype(bf16)
    w_hi = w.astype(bf16)
    w_lo = (w - w_hi.astype(f32)).astype(bf16)
    o_ref[0] = _dot(s_hi, w_hi) + _dot(s_hi, w_lo) + _dot(s_lo, w_hi) + b_ref[0]


def adaln_all(c_all, w_ada, b_ada):
    depth, d, n = w_ada.shape
    rows = c_all.shape[0]
    tn = 1024
    return pl.pallas_call(
        _adaln_kernel,
        grid=(depth, n // tn),
        in_specs=[
            pl.BlockSpec((rows, d), lambda l, j: (0, 0)),
            pl.BlockSpec((1, d, tn), lambda l, j: (l, 0, j)),
            pl.BlockSpec((1, 1, tn), lambda l, j: (l, 0, j)),
        ],
        out_specs=pl.BlockSpec((1, rows, tn), lambda l, j: (l, 0, j)),
        out_shape=jax.ShapeDtypeStruct((depth, rows, n), f32),
        compiler_params=_cparams("arbitrary", "arbitrary"),
        name="adaln",
    )(c_all, w_ada, b_ada.reshape(depth, 1, n))


def _modulated(x, g, sh, sc):
    ms = jnp.mean(x * x, axis=-1, keepdims=True)
    y = x * lax.rsqrt(ms + EPS) * g
    return y * (1.0 + sc) + sh


def _modmm_kernel(x_ref, g_ref, sh_ref, sc_ref, w_ref, o_ref, xn_ref):
    @pl.when(pl.program_id(1) == 0)
    def _():
        xn_ref[...] = _modulated(x_ref[...], g_ref[...], sh_ref[0], sc_ref[0]).astype(bf16)

    o_ref[...] = _dot(xn_ref[...], w_ref[...])


def _modmm_gate_kernel(x_ref, g_ref, sh_ref, sc_ref, w_ref, wg_ref, o_ref, og_ref, xn_ref):
    @pl.when(pl.program_id(1) == 0)
    def _():
        xn = _modulated(x_ref[...], g_ref[...], sh_ref[0], sc_ref[0]).astype(bf16)
        xn_ref[...] = xn
        og_ref[...] = _dot(xn, wg_ref[...])

    o_ref[...] = _dot(xn_ref[...], w_ref[...])


def _mod_specs(mod, m, tm, k, rows_per_batch):
    if rows_per_batch >= tm:
        assert rows_per_batch % tm == 0
        per = rows_per_batch // tm
        return mod.reshape(-1, 1, k), pl.BlockSpec((1, 1, k), lambda i, j: (i // per, 0, 0))
    assert rows_per_batch == 1
    rows = mod
    if rows.shape[0] < m:
        rows = jnp.pad(rows, ((0, m - rows.shape[0]), (0, 0)))
    return rows.reshape(1, m, k), pl.BlockSpec((1, tm, k), lambda i, j: (0, i, 0))


def mod_matmul(x, g, shift, scale, w, n_out, rows_per_batch, tm, tn, w_gate=None, name="modmm"):
    m, k = x.shape
    sh, sh_spec = _mod_specs(shift, m, tm, k, rows_per_batch)
    sc, sc_spec = _mod_specs(scale, m, tm, k, rows_per_batch)
    in_specs = [
        pl.BlockSpec((tm, k), lambda i, j: (i, 0)),
        pl.BlockSpec((1, k), lambda i, j: (0, 0)),
        sh_spec,
        sc_spec,
        pl.BlockSpec((k, tn), lambda i, j: (0, j)),
    ]
    args = [x, g.reshape(1, k), sh, sc, w]
    out_specs = pl.BlockSpec((tm, tn), lambda i, j: (i, j))
    out_shape = jax.ShapeDtypeStruct((m, n_out), f32)
    kern = _modmm_kernel
    if w_gate is not None:
        in_specs.append(pl.BlockSpec((k, LANES), lambda i, j: (0, 0)))
        args.append(w_gate)
        out_specs = [out_specs, pl.BlockSpec((tm, LANES), lambda i, j: (i, 0))]
        out_shape = [out_shape, jax.ShapeDtypeStruct((m, LANES), f32)]
        kern = _modmm_gate_kernel
    return pl.pallas_call(
        kern,
        grid=(m // tm, n_out // tn),
        in_specs=in_specs,
        out_specs=out_specs,
        out_shape=out_shape,
        scratch_shapes=[pltpu.VMEM((tm, k), bf16)],
        compiler_params=_cparams("arbitrary", "arbitrary"),
        name=name,
    )(*args)


def _outproj_kernel(a1_ref, a2_ref, w1_ref, w2_ref, x_ref, gt_ref, o_ref):
    y = _dot(a1_ref[...].astype(bf16), w1_ref[...]) + _dot(a2_ref[...].astype(bf16), w2_ref[...])
    o_ref[...] = x_ref[...] + gt_ref[0] * y


def out_proj(a1, a2, w, x, gate, rows_per_batch, tm, tn, name="outproj"):
    m, k1 = a1.shape
    k2 = a2.shape[1]
    n = w.shape[1]
    gt, gt_spec = _mod_specs(gate, m, tm, n, rows_per_batch)
    if gt.shape[1] == 1:
        per = rows_per_batch // tm
        gt_spec = pl.BlockSpec((1, 1, tn), lambda i, j: (i // per, 0, j))
    else:
        gt_spec = pl.BlockSpec((1, tm, tn), lambda i, j: (0, i, j))
    return pl.pallas_call(
        _outproj_kernel,
        grid=(m // tm, n // tn),
        in_specs=[
            pl.BlockSpec((tm, k1), lambda i, j: (i, 0)),
            pl.BlockSpec((tm, k2), lambda i, j: (i, 0)),
            pl.BlockSpec((k1, tn), lambda i, j: (0, j)),
            pl.BlockSpec((k2, tn), lambda i, j: (k1 // k2, j)),
            pl.BlockSpec((tm, tn), lambda i, j: (i, j)),
            gt_spec,
        ],
        out_specs=pl.BlockSpec((tm, tn), lambda i, j: (i, j)),
        out_shape=jax.ShapeDtypeStruct((m, n), f32),
        compiler_params=_cparams("arbitrary", "arbitrary"),
        name=name,
    )(a1, a2, w, w, x, gt)


def _rglru_gates(xc, wr_ref, br, wi_ref, bi, lam):
    xb = xc.astype(bf16)
    rs, gs = [], []
    for hb in range(RG_BLOCKS):
        sl = slice(hb * LANES, (hb + 1) * LANES)
        rs.append(_dot(xb[:, sl], wr_ref[hb]))
        gs.append(_dot(xb[:, sl], wi_ref[hb]))
    r = _sigmoid(jnp.concatenate(rs, axis=1) + br)
    ig = _sigmoid(jnp.concatenate(gs, axis=1) + bi)
    softplus_neg_lam = jnp.maximum(-lam, 0.0) + jnp.log1p(jnp.exp(-jnp.abs(lam)))
    log_a = -RG_C * r * softplus_neg_lam
    a = jnp.exp(log_a)
    u = jnp.sqrt(-jnp.tanh(log_a) * (a * a + 1.0)) * ig * xc
    return a, u


def _rglru_kernel(xa_ref, ga_ref, cw_ref, cb_ref, wr_ref, br_ref, wi_ref, bi_ref, lam_ref,
                  ya_ref, hl_ref, xprev_ref, hc_ref):
    t_idx = pl.program_id(1)
    tt = xa_ref.shape[0]

    @pl.when(t_idx == 0)
    def _():
        xprev_ref[...] = jnp.zeros_like(xprev_ref)
        hc_ref[...] = jnp.zeros_like(hc_ref)

    xa = xa_ref[...]
    xprev = xprev_ref[...]
    row8 = _iota((8, RG_WIDTH), 0)
    xc = cb_ref[...] + cw_ref[CONV_W - 1:CONV_W, :] * xa
    for k in range(1, CONV_W):
        rolled = pltpu.roll(xa, k, 0)
        head = jnp.where(row8 < k, pltpu.roll(xprev, k, 0), rolled[0:8])
        shifted = jnp.concatenate([head, rolled[8:]], axis=0)
        xc = xc + cw_ref[CONV_W - 1 - k:CONV_W - k, :] * shifted
    xprev_ref[...] = xa[tt - 8:tt]

    a, u = _rglru_gates(xc, wr_ref, br_ref[...], wi_ref, bi_ref[...], lam_ref[...])
    row = _iota((tt, RG_WIDTH), 0)
    s = 1
    while s < tt:
        a_sh = pltpu.roll(a, s, 0)
        u_sh = pltpu.roll(u, s, 0)
        ok = row >= s
        u = jnp.where(ok, a * u_sh + u, u)
        a = jnp.where(ok, a * a_sh, a)
        s *= 2
    h = a * hc_ref[...] + u
    hc_ref[...] = h[tt - 1:tt]
    hl_ref[0] = h[tt - 1:tt]
    ya_ref[...] = (h * _gelu(ga_ref[...])).astype(ya_ref.dtype)


def rglru_prompt(proj, bsz, seq, cw, cb, wr, br, wi, bi, lam, tt=256):
    nt = seq // tt
    vec = lambda: pl.BlockSpec((1, RG_WIDTH), lambda b, t: (0, 0))
    return pl.pallas_call(
        _rglru_kernel,
        grid=(bsz, nt),
        in_specs=[
            pl.BlockSpec((tt, RG_WIDTH), lambda b, t: (b * nt + t, 0)),
            pl.BlockSpec((tt, RG_WIDTH), lambda b, t: (b * nt + t, 1)),
            pl.BlockSpec((CONV_W, RG_WIDTH), lambda b, t: (0, 0)),
            vec(),
            pl.BlockSpec((RG_BLOCKS, LANES, LANES), lambda b, t: (0, 0, 0)),
            vec(),
            pl.BlockSpec((RG_BLOCKS, LANES, LANES), lambda b, t: (0, 0, 0)),
            vec(),
            vec(),
        ],
        out_specs=[
            pl.BlockSpec((tt, RG_WIDTH), lambda b, t: (b * nt + t, 0)),
            pl.BlockSpec((1, 1, RG_WIDTH), lambda b, t: (b, 0, 0)),
        ],
        out_shape=[
            jax.ShapeDtypeStruct((bsz * seq, RG_WIDTH), bf16),
            jax.ShapeDtypeStruct((bsz, 1, RG_WIDTH), f32),
        ],
        scratch_shapes=[pltpu.VMEM((8, RG_WIDTH), f32), pltpu.VMEM((1, RG_WIDTH), f32)],
        compiler_params=_cparams("arbitrary", "arbitrary"),
        name="rglru_prompt",
    )(proj, proj, cw, cb, wr, br, wi, bi, lam)


def _rglru_step_kernel(xa_ref, ga_ref, b0_ref, b1_ref, b2_ref, h0_ref, cw_ref, cb_ref,
                       wr_ref, br_ref, wi_ref, bi_ref, lam_ref, ya_ref, h_ref):
    xa = xa_ref[...]
    xc = (cb_ref[...] + cw_ref[0:1, :] * b0_ref[...] + cw_ref[1:2, :] * b1_ref[...]
          + cw_ref[2:3, :] * b2_ref[...] + cw_ref[3:4, :] * xa)
    a, u = _rglru_gates(xc, wr_ref, br_ref[...], wi_ref, bi_ref[...], lam_ref[...])
    h = a * h0_ref[...] + u
    h_ref[...] = h
    ya_ref[...] = (h * _gelu(ga_ref[...])).astype(ya_ref.dtype)


def rglru_step(proj_s, conv_state, h0, cw, cb, wr, br, wi, bi, lam):
    n = proj_s.shape[0]
    full = lambda shape: pl.BlockSpec(shape, lambda i: tuple(0 for _ in shape))
    return pl.pallas_call(
        _rglru_step_kernel,
        grid=(1,),
        in_specs=[
            pl.BlockSpec((n, RG_WIDTH), lambda i: (0, 0)),
            pl.BlockSpec((n, RG_WIDTH), lambda i: (0, 1)),
            full((n, RG_WIDTH)), full((n, RG_WIDTH)), full((n, RG_WIDTH)), full((n, RG_WIDTH)),
            full((CONV_W, RG_WIDTH)), full((1, RG_WIDTH)),
            full((RG_BLOCKS, LANES, LANES)), full((1, RG_WIDTH)),
            full((RG_BLOCKS, LANES, LANES)), full((1, RG_WIDTH)), full((1, RG_WIDTH)),
        ],
        out_specs=[full((n, RG_WIDTH)), full((n, RG_WIDTH))],
        out_shape=[jax.ShapeDtypeStruct((n, RG_WIDTH), f32)] * 2,
        compiler_params=_cparams("arbitrary"),
        name="rglru_step",
    )(proj_s, proj_s, conv_state[:, 0], conv_state[:, 1], conv_state[:, 2], h0,
      cw, cb, wr, br, wi, bi, lam)


def _rope_tables(pos):
    half = ROT_DIMS // 2
    inv = ROPE_THETA ** (-jnp.arange(half, dtype=f32) / half)
    ang = pos.astype(f32)[:, None] * inv[None, :]
    cos, sin = jnp.cos(ang), jnp.sin(ang)
    n = pos.shape[0]
    ctab = jnp.concatenate([cos, cos, jnp.ones((n, DH - ROT_DIMS), f32)], axis=1)
    stab = jnp.concatenate([-sin, sin, jnp.zeros((n, DH - ROT_DIMS), f32)], axis=1)
    return ctab, stab


def _rope_head(xh, ctab, stab, lane):
    half = ROT_DIMS // 2
    partner = jnp.where(lane < half, pltpu.roll(xh, DH - half, 1), pltpu.roll(xh, half, 1))
    return xh * ctab + partner * stab


def _rope_kernel(q_ref, k_ref, v_ref, c_ref, s_ref, qo_ref, ko_ref, vo_ref):
    ctab, stab = c_ref[...], s_ref[...]
    lane = _iota((q_ref.shape[0], DH), 1)
    for hb in range(HEADS):
        sl = slice(hb * DH, (hb + 1) * DH)
        qo_ref[:, hb, :] = _rope_head(q_ref[:, sl], ctab, stab, lane) * (DH ** -0.5)
        ko_ref[:, hb, :] = _rope_head(k_ref[:, sl], ctab, stab, lane)
        vo_ref[:, hb, :] = v_ref[:, sl]


def rope_qkv(proj, ctab, stab, rows_per_seq, tt):
    m = proj.shape[0]
    nt = max(rows_per_seq // tt, 1)
    if ctab.shape[0] == 1:
        tab_spec = pl.BlockSpec((1, DH), lambda i: (0, 0))
    else:
        tab_spec = pl.BlockSpec((tt, DH), lambda i: (i % nt, 0))
    return pl.pallas_call(
        _rope_kernel,
        grid=(m // tt,),
        in_specs=[
            pl.BlockSpec((tt, ATT_W), lambda i: (i, 2)),
            pl.BlockSpec((tt, ATT_W), lambda i: (i, 3)),
            pl.BlockSpec((tt, ATT_W), lambda i: (i, 4)),
            tab_spec, tab_spec,
        ],
        out_specs=[pl.BlockSpec((tt, HEADS, DH), lambda i: (i, 0, 0))] * 3,
        out_shape=[jax.ShapeDtypeStruct((m, HEADS, DH), f32)] * 3,
        compiler_params=_cparams("arbitrary"),
        name="rope",
    )(proj, proj, proj, ctab, stab)


def _rope_split_kernel(q_ref, k_ref, v_ref, c_ref, s_ref, *refs):
    n_pat = len(DIL_PATTERNS)
    outs, (kf_ref, qs_ref, ks_ref) = refs[:3 * n_pat], refs[3 * n_pat:]
    tt = q_ref.shape[0]
    ctab, stab = c_ref[...], s_ref[...]
    lane = _iota((tt, DH), 1)
    qs_ref[...] = _rope_head(q_ref[...], ctab, stab, lane) * (DH ** -0.5)
    k_rot = _rope_head(k_ref[...], ctab, stab, lane)
    ks_ref[...] = k_rot
    kf_ref[...] = k_rot
    for gi, (_, d) in enumerate(DIL_PATTERNS):
        rows = tt // d
        for src, dst in ((qs_ref, outs[3 * gi]), (ks_ref, outs[3 * gi + 1]), (v_ref, outs[3 * gi + 2])):
            for r in range(d):
                dst[0, 0, r] = src[pl.ds(r, rows, stride=d), :].astype(bf16)


def rope_split(proj, ctab, stab, bsz, seq, tt=1024):
    m = proj.shape[0]
    nt = seq // tt
    col = lambda c: pl.BlockSpec((tt, DH), lambda i, h, c=c: (i, c * HEADS + h))
    tab = pl.BlockSpec((tt, DH), lambda i, h: (i % nt, 0))
    out_specs, out_shape = [], []
    for (_, d) in DIL_PATTERNS:
        for _ in range(3):
            out_specs.append(pl.BlockSpec((1, 1, d, tt // d, DH), lambda i, h: (i // nt, h, 0, i % nt, 0)))
            out_shape.append(jax.ShapeDtypeStruct((bsz, HEADS, d, seq // d, DH), bf16))
    out_specs.append(pl.BlockSpec((tt, DH), lambda i, h: (i, h)))
    out_shape.append(jax.ShapeDtypeStruct((m, ATT_W), f32))
    res = pl.pallas_call(
        _rope_split_kernel,
        grid=(m // tt, HEADS),
        in_specs=[col(2), col(3), col(4), tab, tab],
        out_specs=out_specs,
        out_shape=out_shape,
        scratch_shapes=[pltpu.VMEM((tt, DH), f32), pltpu.VMEM((tt, DH), f32)],
        compiler_params=_cparams("arbitrary", "arbitrary"),
        name="rope_split",
    )(proj, proj, proj, ctab, stab)
    return [res[3 * gi:3 * gi + 3] for gi in range(len(DIL_PATTERNS))], res[-1]


def _dil_kernel(q_ref, kc_ref, kp_ref, vc_ref, vp_ref, o_ref, l_ref, s_ref, p_ref, *, span):
    tq = q_ref.shape[3]
    blk = pl.program_id(2)
    qi = _iota((tq, 2 * tq), 0)
    col = _iota((tq, 2 * tq), 1)
    rel = jnp.where(col < tq, qi - col, qi - col + 2 * tq)
    ok = (rel >= 0) & (rel <= span) & ((col < tq) | (blk > 0))
    for hb in range(HEADS):
        qh = q_ref[0, hb, 0]
        s_ref[hb, :, :tq] = _dot_nt(qh, kc_ref[0, hb, 0])
        s_ref[hb, :, tq:] = _dot_nt(qh, kp_ref[0, hb, 0])
    for hb in range(HEADS):
        s = jnp.where(ok, s_ref[hb], NEG_INF)
        mx = jnp.max(s, axis=1, keepdims=True)
        p = jnp.exp(s - mx)
        den = jnp.sum(p, axis=1, keepdims=True)
        p_ref[hb] = (p * (1.0 / den)).astype(bf16)
        l_ref[0, hb, 0] = jnp.broadcast_to(mx + jnp.log(den), (tq, DH))
    for hb in range(HEADS):
        o_ref[0, hb, 0] = (_dot(p_ref[hb, :, :tq], vc_ref[0, hb, 0])
                           + _dot(p_ref[hb, :, tq:], vp_ref[0, hb, 0]))


def _dil_merge_kernel(*refs):
    n_pat = len(DIL_PATTERNS)
    ins, o_ref, scr = refs[:2 * n_pat], refs[2 * n_pat], refs[2 * n_pat + 1:]
    tt = o_ref.shape[0]
    for gi, (_, d) in enumerate(DIL_PATTERNS):
        rows = tt // d
        for src, dst in ((ins[2 * gi], scr[2 * gi]), (ins[2 * gi + 1], scr[2 * gi + 1])):
            for r in range(d):
                dst[pl.ds(r, rows, stride=d), :] = src[0, 0, r]
    lses = [scr[2 * gi + 1][...] for gi in range(n_pat)]
    top = functools.reduce(jnp.maximum, lses)
    es = [jnp.exp(l - top) for l in lses]
    num = sum(scr[2 * gi][...] * es[gi] for gi in range(n_pat))
    o_ref[...] = (num / sum(es)).astype(o_ref.dtype)


def dilated_prompt(qkv_by_pattern, bsz, seq, tq=256, tt=1024):
    partial = []
    for (w, d), (qd, kd, vd) in zip(DIL_PATTERNS, qkv_by_pattern):
        sd = seq // d
        blk = (1, HEADS, 1, tq, DH)
        cur = pl.BlockSpec(blk, lambda b, r, i: (b, 0, r, i, 0))
        prev = pl.BlockSpec(blk, lambda b, r, i: (b, 0, r, jnp.maximum(i - 1, 0), 0))
        partial += pl.pallas_call(
            functools.partial(_dil_kernel, span=w // d),
            grid=(bsz, d, sd // tq),
            in_specs=[cur, cur, prev, cur, prev],
            out_specs=[cur, cur],
            out_shape=[jax.ShapeDtypeStruct((bsz, HEADS, d, sd, DH), f32)] * 2,
            scratch_shapes=[pltpu.VMEM((HEADS, tq, 2 * tq), f32), pltpu.VMEM((HEADS, tq, 2 * tq), bf16)],
            compiler_params=_cparams("arbitrary", "arbitrary", "arbitrary"),
            name=f"dilattn_d{d}",
        )(qd, kd, kd, vd, vd)
    nt = seq // tt
    in_specs = []
    for (_, d) in DIL_PATTERNS:
        in_specs += [pl.BlockSpec((1, 1, d, tt // d, DH), lambda i, h: (i // nt, h, 0, i % nt, 0))] * 2
    return pl.pallas_call(
        _dil_merge_kernel,
        grid=(bsz * nt, HEADS),
        in_specs=in_specs,
        out_specs=pl.BlockSpec((tt, DH), lambda i, h: (i, h)),
        out_shape=jax.ShapeDtypeStruct((bsz * seq, ATT_W), bf16),
        scratch_shapes=[pltpu.VMEM((tt, DH), f32)] * (2 * len(DIL_PATTERNS)),
        compiler_params=_cparams("arbitrary", "arbitrary"),
        name="dilattn_merge",
    )(*partial)


def _dil_step_kernel(q_ref, kn_ref, vn_ref, k1_ref, k4_ref, k16_ref, v1_ref, v4_ref, v16_ref, o_ref):
    q = q_ref[0]
    kn, vn = kn_ref[0], vn_ref[0]
    s_self = _rowsum_bcast(q * kn)
    o_gs, lse_gs = [], []
    for k_ref, v_ref in ((k1_ref, v1_ref), (k4_ref, v4_ref), (k16_ref, v16_ref)):
        k3 = k_ref[0, :, 0]
        nk = k3.shape[0]
        s = _rowsum_bcast((k3 * q[None]).reshape(nk * HEADS, DH)).reshape(nk, HEADS, DH)
        mx = jnp.maximum(jnp.max(s, axis=0), s_self)
        p = jnp.exp(s - mx[None])
        p_self = jnp.exp(s_self - mx)
        den = jnp.sum(p, axis=0) + p_self
        num = jnp.sum(p * v_ref[0, :, 0], axis=0) + p_self * vn
        o_gs.append(num / den)
        lse_gs.append(mx + jnp.log(den))
    top = jnp.maximum(jnp.maximum(lse_gs[0], lse_gs[1]), lse_gs[2])
    es = [jnp.exp(l - top) for l in lse_gs]
    tot = es[0] + es[1] + es[2]
    o_ref[0] = (o_gs[0] * es[0] + o_gs[1] * es[1] + o_gs[2] * es[2]) / tot


def dilated_step(q4, k4, v4, cache_k, cache_v):
    n, win = cache_k.shape[0], cache_k.shape[1]
    one = pl.BlockSpec((1, HEADS, DH), lambda b: (b, 0, 0))
    args = [q4, k4, v4]
    in_specs = [one, one, one]
    for cache in (cache_k, cache_v):
        for (w, d) in DIL_PATTERNS:
            nkeys = w // d
            assert win % d == 0 and (win // d) % nkeys == 0 and (win - w) % (d * nkeys) == 0
            args.append(cache.reshape(n, win // d, d, HEADS, DH))
            in_specs.append(pl.BlockSpec((1, nkeys, 1, HEADS, DH),
                                         lambda b, blk=(win - w) // d // nkeys: (b, blk, 0, 0, 0)))
    return pl.pallas_call(
        _dil_step_kernel,
        grid=(n,),
        in_specs=in_specs,
        out_specs=one,
        out_shape=jax.ShapeDtypeStruct((n, HEADS, DH), f32),
        compiler_params=_cparams("arbitrary"),
        name="dilattn_step",
    )(*args).reshape(n, ATT_W)


def _sb_kernel(bias_ref, q_ref, k_ref, v_ref, o_ref, kb_ref, vb_ref, qs_ref, t_ref, spb_ref, wb_ref,
               acc_ref, run_ref):
    tq = q_ref.shape[1]
    nh = q_ref.shape[2] // DH
    kt_w = t_ref.shape[2]
    sub = PAGE
    hg = pl.program_id(1)
    qb = pl.program_id(2)

    @pl.when(qb == 0)
    def _():
        kb_ref[...] = k_ref[0].astype(bf16)
        vb_ref[...] = v_ref[0].astype(bf16)

    rr = _iota((sub, 2 * sub), 0)
    cc = _iota((sub, 2 * sub), 1)
    tri = jnp.where((rr > cc) | (cc >= sub), 1.0, 0.0).astype(bf16)
    for h in range(nh):
        qs_ref[h] = (q_ref[0, :, h * DH:(h + 1) * DH] * (DH ** -0.5)).astype(bf16)
    acc_ref[...] = jnp.zeros_like(acc_ref)
    run_ref[...] = jnp.zeros_like(run_ref)

    def macro(start, width, masked):
        start = pl.multiple_of(start, width)
        for h in range(nh):
            hs = slice(h * DH, (h + 1) * DH)
            z = _dot_nt(qs_ref[h], kb_ref[pl.ds(start, width), hs]) + bias_ref[hg * nh + h]
            sp = _softplus(z)
            t = z - sp
            if masked:
                ok = start + _iota((tq, width), 1) < qb * tq + _iota((tq, width), 0)
                sp = jnp.where(ok, sp, 0.0)
                t = jnp.where(ok, t, NEG_INF)
            t_ref[h, :, :width] = t
            spb_ref[h, :, :width] = sp.astype(bf16)
        for h in range(nh):
            run = run_ref[h]
            for kt in range(width // sub - 1, -1, -1):
                ks = slice(kt * sub, (kt + 1) * sub)
                cs = _dot(spb_ref[h, :, ks], tri)
                wb_ref[h, :, ks] = jnp.exp(t_ref[h, :, ks] - cs[:, :sub] - run).astype(bf16)
                run = run + cs[:, sub:]
            run_ref[h] = run
        for h in range(nh):
            hs = slice(h * DH, (h + 1) * DH)
            acc_ref[h] += _dot(wb_ref[h, :, :width], vb_ref[pl.ds(start, width), hs])

    macro(qb * tq, tq, True)
    n_part = (qb * tq % kt_w) // tq
    for j in range(1, kt_w // tq):
        @pl.when(n_part >= j)
        def _():
            macro((qb - j) * tq, tq, False)

    top = (qb * tq) // kt_w

    def body(it, _):
        macro((top - 1 - it) * kt_w, kt_w, False)
        return 0

    lax.fori_loop(0, top, body, 0)
    for h in range(nh):
        o_ref[0, :, h * DH:(h + 1) * DH] = acc_ref[h].astype(o_ref.dtype)


def sb_prompt(proj2, bias, bsz, seq, tq=256, nh=4, kt_w=512):
    assert kt_w % tq == 0 and seq % kt_w == 0
    p3 = proj2.reshape(bsz, seq, proj2.shape[1])
    nq = seq // tq
    ng = HEADS // nh
    wd = nh * DH
    return pl.pallas_call(
        _sb_kernel,
        grid=(bsz, ng, nq),
        in_specs=[
            pl.BlockSpec(memory_space=pltpu.SMEM),
            pl.BlockSpec((1, tq, wd), lambda b, g, i: (b, i, g)),
            pl.BlockSpec((1, seq, wd), lambda b, g, i: (b, 0, ng + g)),
            pl.BlockSpec((1, seq, wd), lambda b, g, i: (b, 0, 2 * ng + g)),
        ],
        out_specs=pl.BlockSpec((1, tq, wd), lambda b, g, i: (b, i, g)),
        out_shape=jax.ShapeDtypeStruct((bsz, seq, ATT_W), bf16),
        scratch_shapes=[
            pltpu.VMEM((seq, wd), bf16), pltpu.VMEM((seq, wd), bf16),
            pltpu.VMEM((nh, tq, DH), bf16),
            pltpu.VMEM((nh, tq, kt_w), f32), pltpu.VMEM((nh, tq, kt_w), bf16),
            pltpu.VMEM((nh, tq, kt_w), bf16),
            pltpu.VMEM((nh, tq, DH), f32), pltpu.VMEM((nh, tq, PAGE), f32),
        ],
        compiler_params=_cparams("arbitrary", "arbitrary", "arbitrary"),
        name="sb_prompt",
    )(bias, p3, p3, p3).reshape(bsz * seq, ATT_W)


def _sb_step_kernel(pt_ref, q_ref, bias_ref, eye_ref, *refs, npg):
    k_refs, v_refs = refs[:npg], refs[npg:2 * npg]
    o_ref, acc_ref, carry_ref = refs[2 * npg:]
    j = pl.program_id(1)

    @pl.when(j == 0)
    def _():
        acc_ref[...] = jnp.zeros_like(acc_ref)
        carry_ref[...] = jnp.zeros_like(carry_ref)

    q = q_ref[0]
    bias = bias_ref[...]
    ones = jnp.ones((DH, DH), bf16)
    rr = _iota((PAGE, 2 * PAGE), 0)
    cc = _iota((PAGE, 2 * PAGE), 1)
    tri = jnp.where((rr > cc) | (cc >= PAGE), 1.0, 0.0).astype(bf16)
    acc = acc_ref[...]
    run = carry_ref[...]
    pages = range(npg)
    zbs = [_dot((k_refs[p][0] * q[None]).reshape(PAGE * HEADS, DH).astype(bf16), ones)
           .reshape(PAGE, HEADS, DH) for p in pages]
    zs = [jnp.sum(zb * eye_ref[...], axis=0) + bias for zb in zbs]
    sps = [_softplus(z) for z in zs]
    css = []
    for sp in sps:
        hi, mid, lo = _split3(sp)
        css.append(_dot(hi, tri) + _dot(mid, tri) + _dot(lo, tri))
    ws = []
    for z, sp, cs in zip(zs, sps, css):
        ws.append(jnp.exp(z - sp - cs[:, :PAGE] - run))
        run = run + cs[:, PAGE:]
    wbs = [_dot((eye_ref[...] * w[None]).reshape(PAGE * HEADS, DH).astype(bf16), ones)
           .reshape(PAGE, HEADS, DH) for w in ws]
    for p, wb in zip(pages, wbs):
        acc = acc + jnp.sum(wb * v_refs[p][0], axis=0)
    acc_ref[...] = acc
    carry_ref[...] = run

    @pl.when(j == pl.num_programs(1) - 1)
    def _():
        o_ref[0] = acc


def sb_step(q_s, bias, cache_k, cache_v, page_table, npg=16):
    n, n_pages = page_table.shape
    q4 = (q_s * (DH ** -0.5)).reshape(n, HEADS, DH)
    bias4 = jnp.broadcast_to(bias[:, None], (HEADS, DH))
    eye3 = jnp.broadcast_to(jnp.eye(PAGE, DH, dtype=f32)[:, None, :], (PAGE, HEADS, DH))

    def page(p):
        return lambda b, j, pt: (pt[b, n_pages - 1 - (j * npg + p)], 0, 0, 0)

    kv_specs = [pl.BlockSpec((1, PAGE, HEADS, DH), page(p)) for p in range(npg)]
    grid_spec = pltpu.PrefetchScalarGridSpec(
        num_scalar_prefetch=1,
        grid=(n, n_pages // npg),
        in_specs=[
            pl.BlockSpec((1, HEADS, DH), lambda b, j, pt: (b, 0, 0)),
            pl.BlockSpec((HEADS, DH), lambda b, j, pt: (0, 0)),
            pl.BlockSpec((PAGE, HEADS, DH), lambda b, j, pt: (0, 0, 0)),
        ] + kv_specs + kv_specs,
        out_specs=pl.BlockSpec((1, HEADS, DH), lambda b, j, pt: (b, 0, 0)),
        scratch_shapes=[pltpu.VMEM((HEADS, DH), f32), pltpu.VMEM((HEADS, DH), f32)],
    )
    return pl.pallas_call(
        functools.partial(_sb_step_kernel, npg=npg),
        grid_spec=grid_spec,
        out_shape=jax.ShapeDtypeStruct((n, HEADS, DH), f32),
        compiler_params=_cparams("arbitrary", "arbitrary"),
        name="sb_step",
    )(page_table, q4, bias4, eye3, *([cache_k] * npg), *([cache_v] * npg)).reshape(n, ATT_W)


def _mlstm_kernel(q_ref, k_ref, v_ref, og_ref, gate_ref, gb_ref, gn_ref,
                  h_ref, c_out, n_out, m_out, c_s, n_s, m_s):
    ci = pl.program_id(1)
    L = CHUNK

    @pl.when(ci == 0)
    def _():
        c_s[...] = jnp.zeros_like(c_s)
        n_s[...] = jnp.zeros_like(n_s)
        m_s[...] = jnp.zeros_like(m_s)

    gt = gate_ref[...] + gb_ref[...]
    gt_t = gt.T
    ri = _iota((L, L), 0)
    li = _iota((L, L), 1)
    causal = li <= ri
    tri_incl = jnp.where(causal, 1.0, 0.0).astype(bf16)
    tri_incl_t = jnp.where(ri <= li, 1.0, 0.0).astype(bf16)
    heads = range(M_HEADS)
    sls = [slice(h * M_DH, (h + 1) * M_DH) for h in heads]
    ig_col = [gt[:, h:h + 1] for h in heads]
    ig_row = [gt_t[h:h + 1, :] for h in heads]
    bcum_col, bcum_row = [], []
    for h in heads:
        lf_col = _log_sigmoid_pair(gt[:, M_HEADS + h:M_HEADS + h + 1])[0]
        lf_row = _log_sigmoid_pair(gt_t[M_HEADS + h:M_HEADS + h + 1, :])[0]
        c_hi, c_mid, c_lo = _split3(jnp.broadcast_to(lf_col, (L, L)))
        bcum_col.append(_dot(tri_incl, c_hi) + _dot(tri_incl, c_mid) + _dot(tri_incl, c_lo))
        r_hi, r_mid, r_lo = _split3(jnp.broadcast_to(lf_row, (L, L)))
        bcum_row.append(_dot(r_hi, tri_incl_t) + _dot(r_mid, tri_incl_t) + _dot(r_lo, tri_incl_t))
    qb = [q_ref[:, sl].astype(bf16) for sl in sls]
    kb = [(k_ref[:, sl] * (M_DH ** -0.5)).astype(bf16) for sl in sls]
    vb = [v_ref[:, sl].astype(bf16) for sl in sls]
    qk = [_dot_nt(qb[h], kb[h]) for h in heads]
    qc = [_dot_nt(qb[h], c_s[h].astype(bf16)) for h in heads]
    m_prev = [m_s[h:h + 1, :] for h in heads]
    m_t, sw, iw_col = [], [], []
    for h in heads:
        dlog = jnp.where(causal, bcum_col[h] - bcum_row[h] + ig_row[h], NEG_INF)
        inter = bcum_col[h] + m_prev[h]
        mt = jnp.maximum(inter, jnp.max(dlog, axis=1, keepdims=True))
        m_t.append(mt)
        sw.append(jnp.exp(dlog - mt) * qk[h])
        iw_col.append(jnp.exp(inter - mt)[:, 0:1])
    sv = [_dot(sw[h].astype(bf16), vb[h]) for h in heads]
    for h in heads:
        n_prev = n_s[h:h + 1, :]
        num = sv[h] + iw_col[h] * qc[h]
        qn = jnp.sum(qb[h].astype(f32) * n_prev.astype(bf16).astype(f32), axis=1, keepdims=True)
        den = jnp.sum(sw[h], axis=1, keepdims=True) + iw_col[h] * qn
        hout = num / jnp.maximum(jnp.abs(den), jnp.exp(-m_t[h][:, 0:1]))
        hn = hout * lax.rsqrt(jnp.mean(hout * hout, axis=1, keepdims=True) + EPS)
        h_ref[:, sls[h]] = (hn * gn_ref[:, sls[h]] * _sigmoid(og_ref[:, sls[h]])).astype(h_ref.dtype)
    for h in heads:
        m_last = m_t[h][L - 1:L, :]
        b_last = bcum_col[h][L - 1:L, :]
        wl_col = jnp.exp(b_last[:, 0:1] - bcum_col[h][:, 0:1] + ig_col[h] - m_last[:, 0:1])
        wl_row = jnp.exp(b_last - bcum_row[h][0:1, :] + ig_row[h] - m_last)
        dsc = jnp.exp(b_last + m_prev[h] - m_last)[:, 0:1]
        c_s[h] = dsc * c_s[h] + _dot((v_ref[:, sls[h]] * wl_col).T.astype(bf16), kb[h])
        wl8 = jnp.broadcast_to(wl_row, (8, L)).astype(bf16)
        n_s[h:h + 1, :] = dsc * n_s[h:h + 1, :] + _dot(wl8, kb[h])[0:1, :]
        m_s[h:h + 1, :] = m_last

    @pl.when(ci == pl.num_programs(1) - 1)
    def _():
        c_out[0] = c_s[...]
        n_out[0] = n_s[...]
        m_out[0] = m_s[...]


def mlstm_prompt(proj2, gates, gate_bias, g_mnorm, bsz, seq):
    nc = seq // CHUNK
    col = lambda c: pl.BlockSpec((CHUNK, M_WIDTH), lambda b, i, c=c: (b * nc + i, c))
    hm, c1, n1, m1 = pl.pallas_call(
        _mlstm_kernel,
        grid=(bsz, nc),
        in_specs=[
            col(3), col(4), col(5), col(6),
            pl.BlockSpec((CHUNK, LANES), lambda b, i: (b * nc + i, 0)),
            pl.BlockSpec((1, LANES), lambda b, i: (0, 0)),
            pl.BlockSpec((1, M_WIDTH), lambda b, i: (0, 0)),
        ],
        out_specs=[
            pl.BlockSpec((CHUNK, M_WIDTH), lambda b, i: (b * nc + i, 0)),
            pl.BlockSpec((1, M_HEADS, M_DH, M_DH), lambda b, i: (b, 0, 0, 0)),
            pl.BlockSpec((1, M_HEADS, M_DH), lambda b, i: (b, 0, 0)),
            pl.BlockSpec((1, M_HEADS, LANES), lambda b, i: (b, 0, 0)),
        ],
        out_shape=[
            jax.ShapeDtypeStruct((bsz * seq, M_WIDTH), bf16),
            jax.ShapeDtypeStruct((bsz, M_HEADS, M_DH, M_DH), f32),
            jax.ShapeDtypeStruct((bsz, M_HEADS, M_DH), f32),
            jax.ShapeDtypeStruct((bsz, M_HEADS, LANES), f32),
        ],
        scratch_shapes=[
            pltpu.VMEM((M_HEADS, M_DH, M_DH), f32),
            pltpu.VMEM((M_HEADS, M_DH), f32),
            pltpu.VMEM((M_HEADS, LANES), f32),
        ],
        compiler_params=_cparams("arbitrary", "arbitrary"),
        name="mlstm_prompt",
    )(proj2, proj2, proj2, proj2, gates, gate_bias, g_mnorm)
    return hm, c1, n1, m1[:, :, 0]


def _mlstm_step_kernel(q_ref, k_ref, v_ref, og_ref, gn_ref, ig_ref, fg_ref, c_ref, n_ref, m_ref,
                       h_ref, c_out, n_out, m_out):
    q = q_ref[0, 0]
    k = k_ref[0, 0] * (M_DH ** -0.5)
    v = v_ref[0, 0]
    ig = ig_ref[0, 0]
    lf = _log_sigmoid_pair(fg_ref[0, 0])[0]
    m0 = m_ref[0, 0]
    c0 = c_ref[0, 0]
    n0 = n_ref[0, 0]
    inter = lf + m0
    m_t = jnp.maximum(inter, ig)
    dw = jnp.exp(ig - m_t)
    iw = jnp.exp(inter - m_t)
    rnd = lambda a: a.astype(bf16).astype(f32)
    qr, kr, vr = rnd(q), rnd(k), rnd(v)
    qk = jnp.sum(qr * kr, axis=1, keepdims=True)
    sw = dw * qk
    cq = jnp.sum(rnd(c0) * qr, axis=1, keepdims=True)
    num = rnd(sw) * vr + iw * cq
    den = sw + iw * jnp.sum(rnd(n0) * qr, axis=1, keepdims=True)
    hout = num / jnp.maximum(jnp.abs(den), jnp.exp(-m_t))
    c_out[0, 0] = iw * c0 + rnd(dw * v) * kr
    n_out[0, 0] = iw * n0 + rnd(dw) * kr
    m_out[0, 0] = m_t
    hn = hout * lax.rsqrt(jnp.mean(hout * hout, axis=0, keepdims=True) + EPS)
    h_ref[0, 0] = hn * gn_ref[0] * _sigmoid(og_ref[0, 0])


def mlstm_step(proj2_s, gates_s, gate_bias, g_mnorm, c0, n0, m0):
    n = proj2_s.shape[0]
    base = 3 * ATT_W
    seg = lambda i: proj2_s[:, base + i * M_WIDTH: base + (i + 1) * M_WIDTH].reshape(n, M_HEADS, M_DH)
    g = gates_s + gate_bias
    rowb = pl.BlockSpec((1, 1, 1, M_DH), lambda b, h: (b, h, 0, 0))
    colb = pl.BlockSpec((1, 1, M_DH, 1), lambda b, h: (b, h, 0, 0))
    scal = pl.BlockSpec((1, 1, 1, 1), lambda b, h: (b, h, 0, 0))
    hcol, c1, n1, m1 = pl.pallas_call(
        _mlstm_step_kernel,
        grid=(n, M_HEADS),
        in_specs=[
            rowb, rowb, colb, colb,
            pl.BlockSpec((1, M_DH, 1), lambda b, h: (h, 0, 0)),
            scal, scal,
            pl.BlockSpec((1, 1, M_DH, M_DH), lambda b, h: (b, h, 0, 0)),
            rowb, scal,
        ],
        out_specs=[colb, pl.BlockSpec((1, 1, M_DH, M_DH), lambda b, h: (b, h, 0, 0)), rowb, scal],
        out_shape=[
            jax.ShapeDtypeStruct((n, M_HEADS, M_DH, 1), f32),
            jax.ShapeDtypeStruct((n, M_HEADS, M_DH, M_DH), f32),
            jax.ShapeDtypeStruct((n, M_HEADS, 1, M_DH), f32),
            jax.ShapeDtypeStruct((n, M_HEADS, 1, 1), f32),
        ],
        compiler_params=_cparams("arbitrary", "arbitrary"),
        name="mlstm_step",
    )(seg(0)[:, :, None, :], seg(1)[:, :, None, :], seg(2)[..., None], seg(3)[..., None],
      g_mnorm.reshape(M_HEADS, M_DH, 1),
      g[:, 0:M_HEADS].reshape(n, M_HEADS, 1, 1), g[:, M_HEADS:2 * M_HEADS].reshape(n, M_HEADS, 1, 1),
      c0, n0[:, :, None, :], m0.reshape(n, M_HEADS, 1, 1))
    return hcol.reshape(n, M_WIDTH), c1, n1.reshape(n, M_HEADS, M_DH), m1.reshape(n, M_HEADS)


def _peerq_kernel(x_ref, g_ref, sh_ref, sc_ref, wt_ref, qt_ref, xmt_ref, xs_ref):
    @pl.when(pl.program_id(1) == 0)
    def _():
        xm = _modulated(x_ref[...], g_ref[...], sh_ref[0], sc_ref[0])
        xt = xm.T.astype(bf16)
        xs_ref[...] = xt
        xmt_ref[...] = xt

    qt_ref[...] = _dot(wt_ref[...], xs_ref[...])


def peer_query(x, g, shift, scale, wq_t, rows_per_batch, tm, tn=512):
    m, k = x.shape
    n = wq_t.shape[0]
    sh, sh_spec = _mod_specs(shift, m, tm, k, rows_per_batch)
    sc, sc_spec = _mod_specs(scale, m, tm, k, rows_per_batch)
    return pl.pallas_call(
        _peerq_kernel,
        grid=(m // tm, n // tn),
        in_specs=[
            pl.BlockSpec((tm, k), lambda i, j: (i, 0)),
            pl.BlockSpec((1, k), lambda i, j: (0, 0)),
            sh_spec, sc_spec,
            pl.BlockSpec((tn, k), lambda i, j: (j, 0)),
        ],
        out_specs=[pl.BlockSpec((tn, tm), lambda i, j: (j, i)),
                   pl.BlockSpec((k, tm), lambda i, j: (0, i))],
        out_shape=[jax.ShapeDtypeStruct((n, m), f32), jax.ShapeDtypeStruct((k, m), bf16)],
        scratch_shapes=[pltpu.VMEM((k, tm), bf16)],
        compiler_params=_cparams("arbitrary", "arbitrary"),
        name="peer_query",
    )(x, g.reshape(1, k), sh, sc, wq_t)


def _cast_t_kernel(w_ref, o_ref):
    o_ref[...] = w_ref[0].T.astype(bf16)


def cast_layer_bf16_t(w, layer, tr=512):
    _, rows, cols = w.shape
    return pl.pallas_call(
        _cast_t_kernel,
        grid=(rows // tr,),
        in_specs=[pl.BlockSpec((1, tr, cols), lambda i: (layer, i, 0))],
        out_specs=pl.BlockSpec((cols, tr), lambda i: (0, i)),
        out_shape=jax.ShapeDtypeStruct((cols, rows), bf16),
        compiler_params=_cparams("arbitrary"),
        name="cast_bf16_t",
    )(w)


def _heads_layout_kernel(*refs):
    n = len(refs) // 2
    for x_ref, o_ref in zip(refs[:n], refs[n:]):
        for hb in range(HEADS):
            o_ref[:, hb, :] = x_ref[:, hb * DH:(hb + 1) * DH]


def heads_layout(srcs, bsz, seq, tail, tt=512):
    nt, first = tail // tt, (seq - tail) // tt
    rows = lambda b, i: b * (seq // tt) + first + i
    return pl.pallas_call(
        _heads_layout_kernel,
        grid=(bsz, nt),
        in_specs=[pl.BlockSpec((tt, ATT_W), lambda b, i, c=c: (rows(b, i), c)) for _, c in srcs],
        out_specs=[pl.BlockSpec((tt, HEADS, DH), lambda b, i: (b * nt + i, 0, 0))] * len(srcs),
        out_shape=[jax.ShapeDtypeStruct((bsz * tail, HEADS, DH), f32)] * len(srcs),
        compiler_params=_cparams("arbitrary", "arbitrary"),
        name="heads_layout",
    )(*[x for x, _ in srcs])


def _cast_kernel(w_ref, o_ref):
    o_ref[...] = w_ref[0].astype(bf16)


def cast_layer_bf16(w, layer, tr=1024):
    _, rows, cols = w.shape
    return pl.pallas_call(
        _cast_kernel,
        grid=(rows // tr,),
        in_specs=[pl.BlockSpec((1, tr, cols), lambda i: (layer, i, 0))],
        out_specs=pl.BlockSpec((tr, cols), lambda i: (i, 0)),
        out_shape=jax.ShapeDtypeStruct((rows, cols), bf16),
        compiler_params=_cparams("arbitrary"),
        name="cast_bf16",
    )(w)


_N_TOP = PEER_TOPK + 1
_CAND_PAIRS = [(a, b) for a in range(_N_TOP) for b in range(_N_TOP) if (a + 1) * (b + 1) <= _N_TOP]
_N_CAND = -(-len(_CAND_PAIRS) // 8) * 8


def _oddeven_mergesort_pairs(n):
    pairs = []

    def merge(lo, cnt, r):
        step = 2 * r
        if step < cnt:
            merge(lo, cnt, step)
            merge(lo + r, cnt, step)
            pairs.extend((i, i + r) for i in range(lo + r, lo + cnt - r, step))
        else:
            pairs.append((lo, lo + r))

    def sort(lo, cnt):
        if cnt > 1:
            sort(lo, cnt // 2)
            sort(lo + cnt // 2, cnt // 2)
            merge(lo, cnt, 1)

    sort(0, n)
    return pairs


_SORT16 = _oddeven_mergesort_pairs(PEER_TOPK)


def _exchange(y, i, j):
    y[i], y[j] = jnp.maximum(y[i], y[j]), jnp.minimum(y[i], y[j])


def _top17_network(s):
    y = [s[v * SUBLANES:(v + 1) * SUBLANES, :] for v in range(PEER_TOPK)]
    for i, j in _SORT16:
        _exchange(y, i, j)
    return _merge_top17(y, (4, 2, 1))


def _bitonic_sort(y):
    dist = len(y) // 2
    while dist:
        for i in range(len(y)):
            if not i & dist:
                _exchange(y, i, i + dist)
        dist //= 2


def _merge_top17(y, shifts):
    n = len(y)
    dropped = None
    for shift in shifts:
        other = [pltpu.roll(y[n - 1 - i], shift, 0) for i in range(n)]
        low = functools.reduce(jnp.maximum, [jnp.minimum(y[i], other[i]) for i in range(n)])
        dropped = low if dropped is None else jnp.maximum(dropped, low)
        y = [jnp.maximum(y[i], other[i]) for i in range(n)]
        _bitonic_sort(y)
    return [v[0:1, :] for v in y] + [jnp.max(dropped, axis=0, keepdims=True)]


_SORT8 = _oddeven_mergesort_pairs(SUBLANES)


def _cand_top17(cand):
    n = SUBLANES
    y = [cand[v * SUBLANES:(v + 1) * SUBLANES, :] for v in range(cand.shape[0] // SUBLANES)]
    y += [jnp.full((SUBLANES, LANES), NEG_INF, f32)] * (n - len(y))
    for i, j in _SORT8:
        _exchange(y, i, j)
    other = [pltpu.roll(y[n - 1 - i], 4, 0) for i in range(n)]
    high = [jnp.maximum(y[i], other[i]) for i in range(n)]
    low = [jnp.minimum(y[i], other[i]) for i in range(n)]
    _bitonic_sort(high)
    _bitonic_sort(low)
    return _merge_top17(high + low, (2, 1))


_ROUTER_UNROLL = 8


def _router_kernel(qt_ref, keys_ref, e1_ref, e2_ref, th_ref, cand_ref):
    k0 = keys_ref[0].astype(bf16)
    k1 = keys_ref[1].astype(bf16)
    cand_ref[...] = jnp.full(cand_ref.shape, NEG_INF, f32)

    def one_head(h, slot):
        r0 = pl.multiple_of(h * 2 * N_KEYS, 2 * N_KEYS)
        s1 = _dot(k0, qt_ref[pl.ds(r0, N_KEYS), :].astype(bf16))
        s2 = _dot(k1, qt_ref[pl.ds(r0 + N_KEYS, N_KEYS), :].astype(bf16))
        top_a = _top17_network(s1)
        top_b = _top17_network(s2)
        for ci, (a, b) in enumerate(_CAND_PAIRS):
            cand_ref[slot, ci:ci + 1, :] = top_a[a] + top_b[b]
        cs = _cand_top17(cand_ref[slot])
        z = jnp.ones_like(cs[0])
        for r in range(1, PEER_TOPK):
            z = z + jnp.exp(cs[r] - cs[0])
        inv_z = 1.0 / z
        mid = 0.5 * (cs[PEER_TOPK - 1] + cs[PEER_TOPK])
        o0 = pl.multiple_of(h * N_KEYS, N_KEYS)
        e1 = jnp.exp(s1 - top_a[0]) * inv_z
        e1_ref[:, h] = e1.reshape(N_KEYS // SUBLANES, SUBLANES, LANES)
        e2_ref[pl.ds(o0, N_KEYS), :] = jnp.exp(s2 - top_b[0])
        th_ref[pl.ds(h, 1), :] = jnp.exp(mid - cs[0]) * inv_z

    def body(hp, _):
        for slot in range(_ROUTER_UNROLL):
            one_head(hp * _ROUTER_UNROLL + slot, slot)
        return 0

    lax.fori_loop(0, PEER_HEADS // _ROUTER_UNROLL, body, 0)


def peer_router(qt, keys):
    n, m = qt.shape
    half = PEER_HEADS * N_KEYS
    return pl.pallas_call(
        _router_kernel,
        grid=(m // LANES,),
        in_specs=[pl.BlockSpec((n, LANES), lambda i: (0, i)),
                  pl.BlockSpec((2, N_KEYS, N_KEYS), lambda i: (0, 0, 0))],
        out_specs=[pl.BlockSpec((N_KEYS // SUBLANES, PEER_HEADS, SUBLANES, LANES), lambda i: (0, 0, 0, i)),
                   pl.BlockSpec((half, LANES), lambda i: (0, i)),
                   pl.BlockSpec((PEER_HEADS, LANES), lambda i: (0, i))],
        out_shape=[jax.ShapeDtypeStruct((N_KEYS // SUBLANES, PEER_HEADS, SUBLANES, m), f32),
                   jax.ShapeDtypeStruct((half, m), f32),
                   jax.ShapeDtypeStruct((PEER_HEADS, m), f32)],
        scratch_shapes=[pltpu.VMEM((_ROUTER_UNROLL, _N_CAND, LANES), f32)],
        compiler_params=_cparams("arbitrary"),
        name="peer_router",
    )(qt, keys)


_G_CHAINS = 3
_G_ROWS = SUBLANES


class _Chains:
    def __init__(self, n, zero_ref):
        self.deps, self.cnt, self.zero_ref = [None] * n, 0, zero_ref

    def take(self):
        return self.deps[self.cnt % len(self.deps)]

    def put(self, g):
        bits = lax.bitcast_convert_type(g[0:1, :], jnp.int32) & self.zero_ref[0:1, :]
        self.deps[self.cnt % len(self.deps)] = lax.bitcast_convert_type(bits, f32)
        self.cnt += 1


def _routing_block(e1_ref, e2_ref, th_ref, ii, ls, chains):
    sub = _G_ROWS
    blk, r = divmod(ii, SUBLANES)
    rows = [e1_ref[blk, h, r:r + 1, ls] for h in range(PEER_HEADS)]
    ths = [th_ref[h:h + 1, ls] for h in range(PEER_HEADS)]
    parts = []
    for gi in range(N_KEYS // sub):
        dep = chains.take()
        g = jnp.zeros((sub, LANES), f32)
        for h in range(PEER_HEADS):
            row = rows[h] if (dep is None or h) else rows[h] + dep
            pr = e2_ref[h * N_KEYS + gi * sub:h * N_KEYS + (gi + 1) * sub, ls] * row
            g = g + jnp.where(pr >= ths[h], pr, 0.0)
        chains.put(g)
        parts.append(g)
    return jnp.concatenate(parts, axis=0)


def _experts_kernel(xmt_ref, e1_ref, e2_ref, th_ref, u_ref, v_ref, x_ref, gt_ref, gf_ref, zero_ref,
                    o_ref, w_ref, *, final_norm):
    c = pl.program_id(1)
    tm, te = w_ref.shape

    @pl.when(c == 0)
    def _():
        o_ref[...] = jnp.zeros_like(o_ref)

    chains = _Chains(_G_CHAINS, zero_ref)
    act = _gelu(_dot(u_ref[...], xmt_ref[...]))
    for ii in range(te // N_KEYS):
        es = slice(ii * N_KEYS, (ii + 1) * N_KEYS)
        for lc in range(tm // LANES):
            ls = slice(lc * LANES, (lc + 1) * LANES)
            g_blk = _routing_block(e1_ref, e2_ref, th_ref, ii, ls, chains)
            w_ref[ls, es] = (g_blk * act[es, ls]).T.astype(bf16)
    o_ref[...] += _dot(w_ref[...], v_ref[...])

    @pl.when(c == pl.num_programs(1) - 1)
    def _():
        y = x_ref[...] + gt_ref[0] * o_ref[...]
        if final_norm:
            ms = jnp.mean(y * y, axis=-1, keepdims=True)
            y = y * lax.rsqrt(ms + EPS) * gf_ref[...]
        o_ref[...] = y


def peer_experts(xmt, e1t, e2t, th, u, v, x, gate, g_final, rows_per_batch, tm, te, final_norm):
    m, d = x.shape
    gt, gt_spec = _mod_specs(gate, m, tm, d, rows_per_batch)
    half = PEER_HEADS * N_KEYS
    return pl.pallas_call(
        functools.partial(_experts_kernel, final_norm=final_norm),
        grid=(m // tm, N_EXPERTS // te),
        in_specs=[
            pl.BlockSpec((d, tm), lambda i, c: (0, i)),
            pl.BlockSpec((te // N_KEYS // SUBLANES, PEER_HEADS, SUBLANES, tm), lambda i, c: (c, 0, 0, i)),
            pl.BlockSpec((half, tm), lambda i, c: (0, i)),
            pl.BlockSpec((PEER_HEADS, tm), lambda i, c: (0, i)),
            pl.BlockSpec((te, d), lambda i, c: (c, 0)),
            pl.BlockSpec((te, d), lambda i, c: (c, 0)),
            pl.BlockSpec((tm, d), lambda i, c: (i, 0)),
            gt_spec,
            pl.BlockSpec((1, d), lambda i, c: (0, 0)),
            pl.BlockSpec((SUBLANES, LANES), lambda i, c: (0, 0)),
        ],
        out_specs=pl.BlockSpec((tm, d), lambda i, c: (i, 0)),
        out_shape=jax.ShapeDtypeStruct((m, d), f32),
        scratch_shapes=[pltpu.VMEM((tm, te), bf16)],
        compiler_params=_cparams("arbitrary", "arbitrary"),
        name="peer_experts",
    )(xmt, e1t, e2t, th, u, v, x, gt, g_final.reshape(1, d), jnp.zeros((SUBLANES, LANES), jnp.int32))


def peer_block(x, g, shift, scale, gate, wq_t, keys, u, v, g_final, rows_per_batch, tm, te, final_norm):
    tm_q = 1024 if x.shape[0] % 1024 == 0 and rows_per_batch % 1024 == 0 else tm
    qt, xmt = peer_query(x, g, shift, scale, wq_t, rows_per_batch, tm_q)
    e1t, e2t, th = peer_router(qt, keys)
    return peer_experts(xmt, e1t, e2t, th, u, v, x, gate, g_final, rows_per_batch, tm, te, final_norm)


def kernel(x_prompt, x_sample, c_prompt, c_sample, state_rglru_conv, state_rglru_h, cache_swa_k, cache_swa_v, cache_sb_k, cache_sb_v, state_mlstm_C, state_mlstm_n, state_mlstm_m, page_table, w_ada, b_ada, g_norm_mix, g_norm_ffn, e_w_in, e_conv_w, e_conv_b, e_w_r, e_b_r, e_w_i, e_b_i, e_lambda, e_w_out, o_w_in, o_b_if, o_sb_bias, o_g_mnorm, o_w_out, peer_w_q, peer_keys, peer_u, peer_v, g_final):
    bp, seq, d = x_prompt.shape
    bs = x_sample.shape[0]
    mp = bp * seq
    pad_s = LANES
    xp = x_prompt.reshape(mp, d)
    xs = x_sample.reshape(bs, d)

    c_rows = 16
    c_all = jnp.concatenate([c_prompt, c_sample, jnp.zeros((c_rows - bp - bs, d), f32)], axis=0)
    mod = adaln_all(c_all, w_ada, b_ada)

    def mods(layer):
        parts = [mod[layer, :, i * d:(i + 1) * d] for i in range(6)]
        return [p[:bp] for p in parts], [p[bp:bp + bs] for p in parts]

    ctab_p, stab_p = _rope_tables(jnp.arange(seq, dtype=jnp.int32))
    ctab_s, stab_s = _rope_tables(jnp.full((1,), PAST_LEN, jnp.int32))

    TM = 512
    m_p, m_s = mods(0)
    w_in = e_w_in[0].astype(bf16)
    w_out = e_w_out[0].astype(bf16)
    cw, cb = e_conv_w[0], e_conv_b[0].reshape(1, -1)
    wr, wi = e_w_r[0].astype(bf16), e_w_i[0].astype(bf16)
    br, bi, lam = e_b_r[0].reshape(1, -1), e_b_i[0].reshape(1, -1), e_lambda[0].reshape(1, -1)

    proj_p = mod_matmul(xp, g_norm_mix[0], m_p[0], m_p[1], w_in, E_IN, seq, 1024, E_IN // 4, name="e_in_p")
    proj_s = mod_matmul(xs, g_norm_mix[0], m_s[0], m_s[1], w_in, E_IN, 1, bs, 512, name="e_in_s")

    ya_p, h_p = rglru_prompt(proj_p, bp, seq, cw, cb, wr, br, wi, bi, lam)
    ya_s, h_s = rglru_step(proj_s, state_rglru_conv[0], state_rglru_h[0], cw, cb, wr, br, wi, bi, lam)
    conv_p = proj_p.reshape(bp, seq, E_IN)[:, seq - (CONV_W - 1):, :RG_WIDTH]
    conv_s = jnp.concatenate([state_rglru_conv[0][:, 1:], proj_s[:, None, :RG_WIDTH]], axis=1)

    qkv_p, k_p = rope_split(proj_p, ctab_p, stab_p, bp, seq)
    q_s, k_s, v_s = rope_qkv(proj_s, ctab_s, stab_s, 1, bs)
    o_p = dilated_prompt(qkv_p, bp, seq)
    win = cache_swa_k.shape[2]
    o_s = dilated_step(q_s, k_s, v_s, cache_swa_k.reshape(-1, win, HEADS, DH)[:bs],
                       cache_swa_v.reshape(-1, win, HEADS, DH)[:bs])
    wl = min(2048, seq)
    swa_k_p, swa_v_p = [a.reshape(bp, wl, HEADS, DH)
                        for a in heads_layout([(k_p, 0), (proj_p, E_IN // ATT_W - 1)], bp, seq, wl)]
    swa_k_s = k_s.reshape(bs, 1, HEADS, DH)
    swa_v_s = v_s.reshape(bs, 1, HEADS, DH)

    xp = out_proj(ya_p, o_p, w_out, xp, m_p[2], seq, 1024, 1024, name="e_out_p")
    xs = out_proj(ya_s, o_s, w_out, xs, m_s[2], 1, bs, 512, name="e_out_s")

    def peer_layer(layer, xp, xs, m_p, m_s, final_norm):
        wq_t = cast_layer_bf16_t(peer_w_q, layer)
        u = cast_layer_bf16(peer_u, layer)
        v = cast_layer_bf16(peer_v, layer)
        xp = peer_block(xp, g_norm_ffn[layer], m_p[3], m_p[4], m_p[5], wq_t, peer_keys[layer], u, v,
                        g_final, seq, TM, 1024, final_norm)
        xs_pad = jnp.pad(xs, ((0, pad_s - bs), (0, 0)))
        xs_new = peer_block(xs_pad, g_norm_ffn[layer], m_s[3], m_s[4], m_s[5], wq_t, peer_keys[layer],
                            u, v, g_final, 1, pad_s, 1024, final_norm)
        return xp, xs_new[:bs]

    xp, xs = peer_layer(0, xp, xs, m_p, m_s, False)

    m_p, m_s = mods(1)
    w_in2 = o_w_in[0].astype(bf16)
    w_gate = jnp.pad(w_in2[:, O_MAIN:], ((0, 0), (0, LANES - 2 * M_HEADS)))
    w_out2 = o_w_out[0].astype(bf16)
    gate_bias = jnp.pad(o_b_if[0].reshape(1, 2 * M_HEADS), ((0, 0), (0, LANES - 2 * M_HEADS)))
    gmn = o_g_mnorm[0].reshape(1, M_WIDTH)

    proj2_p, gates_p = mod_matmul(xp, g_norm_mix[1], m_p[0], m_p[1], w_in2, O_MAIN, seq, 1024, 1024,
                                  w_gate=w_gate, name="o_in_p")
    proj2_s, gates_s = mod_matmul(xs, g_norm_mix[1], m_s[0], m_s[1], w_in2, O_MAIN, 1, bs, 512,
                                  w_gate=w_gate, name="o_in_s")

    oc_p = sb_prompt(proj2_p, o_sb_bias[0], bp, seq)
    oc_s = sb_step(proj2_s[:, :ATT_W], o_sb_bias[0], cache_sb_k.reshape(-1, PAGE, HEADS, DH),
                   cache_sb_v.reshape(-1, PAGE, HEADS, DH), page_table)
    hm_p, mC_p, mn_p, mm_p = mlstm_prompt(proj2_p, gates_p, gate_bias, gmn, bp, seq)
    hm_s, mC_s, mn_s, mm_s = mlstm_step(proj2_s, gates_s, gate_bias, gmn,
                                        state_mlstm_C.reshape(-1, M_HEADS, M_DH, M_DH)[:bs],
                                        state_mlstm_n[0], state_mlstm_m[0])

    n_pg = seq // PAGE
    sb_k_p, sb_v_p = [a.reshape(bp, n_pg, PAGE, HEADS, DH)
                      for a in heads_layout([(proj2_p, 1), (proj2_p, 2)], bp, seq, seq)]
    sb_k_s = proj2_s[:, ATT_W:2 * ATT_W].reshape(bs, 1, HEADS, DH)
    sb_v_s = proj2_s[:, 2 * ATT_W:3 * ATT_W].reshape(bs, 1, HEADS, DH)

    xp = out_proj(oc_p, hm_p, w_out2, xp, m_p[2], seq, 1024, 1024, name="o_out_p")
    xs = out_proj(oc_s, hm_s, w_out2, xs, m_s[2], 1, bs, 512, name="o_out_s")
    xp, xs = peer_layer(1, xp, xs, m_p, m_s, True)

    y_prompt = xp.reshape(bp, seq, d)
    y_sample = xs.reshape(bs, 1, d)
    st = lambda a: a[None]
    return (y_prompt, y_sample, st(conv_p), st(conv_s), st(h_p.reshape(bp, RG_WIDTH)), st(h_s),
            st(swa_k_p), st(swa_k_s), st(swa_v_p), st(swa_v_s),
            st(sb_k_p), st(sb_k_s), st(sb_v_p), st(sb_v_s),
            st(mC_p), st(mC_s), st(mn_p), st(mn_s), st(mm_p), st(mm_s))
```
